```python
import math
import jax, jax.numpy as jnp
from jax import lax
import numpy as np

D_MODEL = 1024
BATCH = 8
SEQ = 4096
DEPTH = 2

CHUNK = 64
EPS = 1e-6
SB_HEADS = 8
SB_HEAD_DIM = 64
SB_WIDTH = SB_HEADS * SB_HEAD_DIM
SB_BLOCK = 128
DN_HEADS = 4
DN_HEAD_DIM = 128
DN_WIDTH = DN_HEADS * DN_HEAD_DIM
CONV_WIDTH = 4
DN_CHUNK = CHUNK
ADA_SCALE = 0.3
IN_SIZES = [SB_WIDTH] * 4 + [DN_WIDTH] * 4 + [DN_HEADS] * 2 + [D_MODEL] * 2
IN_COLS = sum(IN_SIZES)
IN_SPLITS = [int(s) for s in np.cumsum(IN_SIZES)[:-1]]

kernel_name = "hybrid_stickbreak_gated_deltanet_block"


def rms_norm(x, g):
    xf = x.astype(jnp.float32)
    y = xf * lax.rsqrt(jnp.mean(xf * xf, axis=-1, keepdims=True) + EPS)
    return (y * g.astype(jnp.float32)).astype(x.dtype)


def l2_norm(x):
    xf = x.astype(jnp.float32)
    return xf * lax.rsqrt(jnp.sum(xf * xf, axis=-1, keepdims=True) + EPS)


def to_heads(t, n, d):
    b, s, _ = t.shape
    return t.reshape(b, s, n, d).transpose(0, 2, 1, 3)


def from_heads(t):
    b, h, s, d = t.shape
    return t.transpose(0, 2, 1, 3).reshape(b, s, h * d)


def stick_breaking_attention(q, k, v):
    S = q.shape[2]
    scale = 1.0 / math.sqrt(SB_HEAD_DIM)
    outs = []
    for start in range(0, S, SB_BLOCK):
        end = start + SB_BLOCK
        qb = q[:, :, start:end].astype(jnp.float32)
        kb = k[:, :, :end].astype(jnp.float32)
        vb = v[:, :, :end].astype(jnp.float32)
        z = jnp.einsum('bhqd,bhkd->bhqk', qb, kb) * scale
        q_pos = start + jnp.arange(SB_BLOCK)
        k_pos = jnp.arange(end)
        valid = k_pos[None, :] < q_pos[:, None]
        log_keep = jnp.where(valid, jax.nn.log_sigmoid(-z), 0.0)
        later = lax.cumsum(log_keep, axis=3, reverse=True) - log_keep
        w = jnp.where(valid, jnp.exp(jax.nn.log_sigmoid(z) + later), 0.0)
        outs.append(jnp.einsum('bhqk,bhkd->bhqd', w, vb))
    return jnp.concatenate(outs, axis=2).astype(v.dtype)


def causal_short_conv(x, w):
    K, C = w.shape
    out = lax.conv_general_dilated(x, w[:, None, :], window_strides=(1,), padding=[(K - 1, 0)],
                                   dimension_numbers=('NWC', 'WIO', 'NWC'), feature_group_count=C)
    return jax.nn.silu(out)


def gated_delta_rule(q, k, v, beta, g):
    B, H, S, Dk = q.shape
    Dv = v.shape[-1]
    C = DN_CHUNK
    N = S // C
    f32 = jnp.float32
    q = q.astype(f32) * (Dk ** -0.5)
    k, v, beta, g = k.astype(f32), v.astype(f32), beta.astype(f32), g.astype(f32)
    q, k, v = (t.reshape(B, H, N, C, t.shape[-1]) for t in (q, k, v))
    beta = beta.reshape(B, H, N, C)
    g = jnp.cumsum(g.reshape(B, H, N, C), axis=-1)
    tril = jnp.tril(jnp.ones((C, C), dtype=bool))
    strict = jnp.tril(jnp.ones((C, C), dtype=bool), -1)
    decay = jnp.exp(jnp.where(tril, g[..., :, None] - g[..., None, :], -jnp.inf))
    k_beta = k * beta[..., None]
    v_beta = v * beta[..., None]
    m = jnp.where(strict, jnp.einsum('bhnid,bhnjd->bhnij', k_beta, k) * decay, 0.0)
    u = lax.linalg.triangular_solve(m, v_beta, left_side=True, lower=True, unit_diagonal=True)
    w = lax.linalg.triangular_solve(m, k_beta * jnp.exp(g)[..., None], left_side=True,
                                    lower=True, unit_diagonal=True)
    intra = jnp.where(tril, jnp.einsum('bhnid,bhnjd->bhnij', q, k) * decay, 0.0)
    xs = tuple(jnp.moveaxis(t, 2, 0) for t in (q, k, u, w, g, intra))

    def step(state, inp):
        qc, kc, uc, wc, gc, ac = inp
        v_new = uc - jnp.einsum('bhcd,bhde->bhce', wc, state)
        o = (jnp.einsum('bhcd,bhde->bhce', qc * jnp.exp(gc)[..., None], state)
             + jnp.einsum('bhij,bhje->bhie', ac, v_new))
        g_last = gc[..., -1]
        state = (state * jnp.exp(g_last)[..., None, None]
                 + jnp.einsum('bhcd,bhce->bhde', kc * jnp.exp(g_last[..., None] - gc)[..., None], v_new))
        return state, o

    _, o = lax.scan(step, jnp.zeros((B, H, Dk, Dv), f32), xs)
    return jnp.moveaxis(o, 0, 2).reshape(B, H, S, Dv)


def hybrid_layer(x, c, ada_w, ada_b, norm_g, w_in, sb_q_g, sb_k_g, conv_w, dn_a_log,
                 dn_dt_bias, dn_norm_g, w_branch_sb, w_branch_dn, w_out):
    B, S, _ = x.shape
    mod = jax.nn.silu(c) @ ada_w + ada_b
    shift, scale, gate = jnp.split(mod, 3, axis=-1)
    h = rms_norm(x, norm_g) * (1.0 + scale[:, None]) + shift[:, None]

    p = h @ w_in
    (sb_q, sb_k, sb_v, sb_z, dn_q, dn_k, dn_v, dn_z, dn_b, dn_a,
     merge_sb, merge_dn) = jnp.split(p, IN_SPLITS, axis=-1)

    qa = rms_norm(to_heads(sb_q, SB_HEADS, SB_HEAD_DIM), sb_q_g)
    ka = rms_norm(to_heads(sb_k, SB_HEADS, SB_HEAD_DIM), sb_k_g)
    va = to_heads(sb_v, SB_HEADS, SB_HEAD_DIM)
    o_sb = from_heads(stick_breaking_attention(qa, ka, va)) * jax.nn.silu(sb_z)

    qkv = causal_short_conv(jnp.concatenate([dn_q, dn_k, dn_v], axis=-1), conv_w)
    qb, kb, vb = jnp.split(qkv, 3, axis=-1)
    qb = l2_norm(to_heads(qb, DN_HEADS, DN_HEAD_DIM))
    kb = l2_norm(to_heads(kb, DN_HEADS, DN_HEAD_DIM))
    vb = to_heads(vb, DN_HEADS, DN_HEAD_DIM)
    beta = jax.nn.sigmoid(dn_b.astype(jnp.float32)).transpose(0, 2, 1)
    log_decay = (-jnp.exp(dn_a_log.astype(jnp.float32))
                 * jax.nn.softplus(dn_a.astype(jnp.float32) + dn_dt_bias.astype(jnp.float32)))
    log_decay = log_decay.transpose(0, 2, 1)
    o_dn = gated_delta_rule(qb, kb, vb, beta, log_decay)
    o_dn = o_dn.transpose(0, 2, 1, 3)
    z_dn = dn_z.reshape(B, S, DN_HEADS, DN_HEAD_DIM).astype(jnp.float32)
    o_dn = (rms_norm(o_dn, dn_norm_g) * jax.nn.silu(z_dn)).reshape(B, S, DN_WIDTH).astype(x.dtype)

    y = (jax.nn.sigmoid(merge_sb) * (o_sb @ w_branch_sb)
         + jax.nn.sigmoid(merge_dn) * (o_dn @ w_branch_dn))
    out = y @ w_out
    return x + gate[:, None] * out


def _fwd_setup_inputs(seed: int = 0) -> dict:
    key = jax.random.key(seed)
    ks = jax.random.split(key, 16)
    f32 = jnp.float32
    D = D_MODEL
    nrm = lambda k, shape, s: jax.random.normal(k, shape, f32) * s
    dt = jnp.exp(jax.random.uniform(ks[10], (DEPTH, DN_HEADS), f32, math.log(1e-3), math.log(1e-1)))
    return {
        "x": nrm(ks[0], (BATCH, SEQ, D), 1.0),
        "c": nrm(ks[1], (BATCH, D), 1.0),
        "ada_w": nrm(ks[2], (DEPTH, D, 3 * D), ADA_SCALE * D ** -0.5),
        "ada_b": nrm(ks[3], (DEPTH, 3 * D), 0.02),
        "norm_g": 1.0 + nrm(ks[4], (DEPTH, D), 0.02),
        "w_in": nrm(ks[5], (DEPTH, D, IN_COLS), D ** -0.5),
        "sb_q_g": 1.0 + nrm(ks[6], (DEPTH, SB_HEAD_DIM), 0.02),
        "sb_k_g": 1.0 + nrm(ks[7], (DEPTH, SB_HEAD_DIM), 0.02),
        "conv_w": nrm(ks[8], (DEPTH, CONV_WIDTH, 3 * DN_WIDTH), CONV_WIDTH ** -0.5),
        "dn_a_log": jnp.log(jax.random.uniform(ks[9], (DEPTH, DN_HEADS), f32, 1.0, 16.0)),
        "dn_dt_bias": dt + jnp.log(-jnp.expm1(-dt)),
        "dn_norm_g": 1.0 + nrm(ks[11], (DEPTH, DN_HEAD_DIM), 0.02),
        "w_branch_sb": nrm(ks[12], (DEPTH, SB_WIDTH, D), SB_WIDTH ** -0.5),
        "w_branch_dn": nrm(ks[13], (DEPTH, DN_WIDTH, D), DN_WIDTH ** -0.5),
        "w_out": nrm(ks[14], (DEPTH, D, D), D ** -0.5),
    }


def _fwd_reference(x, c, ada_w, ada_b, norm_g, w_in, sb_q_g, sb_k_g, conv_w, dn_a_log,
              dn_dt_bias, dn_norm_g, w_branch_sb, w_branch_dn, w_out):
    for l in range(DEPTH):
        x = hybrid_layer(x, c, ada_w[l], ada_b[l], norm_g[l], w_in[l], sb_q_g[l], sb_k_g[l],
                         conv_w[l], dn_a_log[l], dn_dt_bias[l], dn_norm_g[l],
                         w_branch_sb[l], w_branch_dn[l], w_out[l])
    return x


import jax as _jax
import jax.numpy as _jnp

TWIN_FORMAT = 'train_step'
FWD_PARAMS = ['x', 'c', 'ada_w', 'ada_b', 'norm_g', 'w_in', 'sb_q_g', 'sb_k_g', 'conv_w', 'dn_a_log', 'dn_dt_bias', 'dn_norm_g', 'w_branch_sb', 'w_branch_dn', 'w_out']
TWIN_WEIGHTS = ['ada_w', 'ada_b', 'norm_g', 'w_in', 'sb_q_g', 'sb_k_g', 'conv_w', 'dn_a_log', 'dn_dt_bias', 'dn_norm_g', 'w_branch_sb', 'w_branch_dn', 'w_out']
TWIN_DIFF_INPUT = 'x'
TWIN_INPUTS = ['x', 'c', 'ada_w', 'ada_b', 'norm_g', 'w_in', 'sb_q_g', 'sb_k_g', 'conv_w', 'dn_a_log', 'dn_dt_bias', 'dn_norm_g', 'w_branch_sb', 'w_branch_dn', 'w_out', 'loss_target', 'm_ada_w', 'm_ada_b', 'm_norm_g', 'm_w_in', 'm_sb_q_g', 'm_sb_k_g', 'm_conv_w', 'm_dn_a_log', 'm_dn_dt_bias', 'm_dn_norm_g', 'm_w_branch_sb', 'm_w_branch_dn', 'm_w_out', 'v_ada_w', 'v_ada_b', 'v_norm_g', 'v_w_in', 'v_sb_q_g', 'v_sb_k_g', 'v_conv_w', 'v_dn_a_log', 'v_dn_dt_bias', 'v_dn_norm_g', 'v_w_branch_sb', 'v_w_branch_dn', 'v_w_out']
TWIN_OUTPUTS = ['loss', 'grad_x', 'grad_ada_w', 'grad_ada_b', 'grad_norm_g', 'grad_w_in', 'grad_sb_q_g', 'grad_sb_k_g', 'grad_conv_w', 'grad_dn_a_log', 'grad_dn_dt_bias', 'grad_dn_norm_g', 'grad_w_branch_sb', 'grad_w_branch_dn', 'grad_w_out', 'delta_ada_w', 'delta_ada_b', 'delta_norm_g', 'delta_w_in', 'delta_sb_q_g', 'delta_sb_k_g', 'delta_conv_w', 'delta_dn_a_log', 'delta_dn_dt_bias', 'delta_dn_norm_g', 'delta_w_branch_sb', 'delta_w_branch_dn', 'delta_w_out', 'new_m_ada_w', 'new_m_ada_b', 'new_m_norm_g', 'new_m_w_in', 'new_m_sb_q_g', 'new_m_sb_k_g', 'new_m_conv_w', 'new_m_dn_a_log', 'new_m_dn_dt_bias', 'new_m_dn_norm_g', 'new_m_w_branch_sb', 'new_m_w_branch_dn', 'new_m_w_out', 'new_v_ada_w', 'new_v_ada_b', 'new_v_norm_g', 'new_v_w_in', 'new_v_sb_q_g', 'new_v_sb_k_g', 'new_v_conv_w', 'new_v_dn_a_log', 'new_v_dn_dt_bias', 'new_v_dn_norm_g', 'new_v_w_branch_sb', 'new_v_w_branch_dn', 'new_v_w_out']
TWIN_LEAF_KINDS = {'loss': 'loss', 'grad_x': 'grad_x', 'grad_ada_w': 'grad_w', 'grad_ada_b': 'grad_w', 'grad_norm_g': 'grad_w', 'grad_w_in': 'grad_w', 'grad_sb_q_g': 'grad_w', 'grad_sb_k_g': 'grad_w', 'grad_conv_w': 'grad_w', 'grad_dn_a_log': 'grad_w', 'grad_dn_dt_bias': 'grad_w', 'grad_dn_norm_g': 'grad_w', 'grad_w_branch_sb': 'grad_w', 'grad_w_branch_dn': 'grad_w', 'grad_w_out': 'grad_w', 'delta_ada_w': 'delta_w', 'delta_ada_b': 'delta_w', 'delta_norm_g': 'delta_w', 'delta_w_in': 'delta_w', 'delta_sb_q_g': 'delta_w', 'delta_sb_k_g': 'delta_w', 'delta_conv_w': 'delta_w', 'delta_dn_a_log': 'delta_w', 'delta_dn_dt_bias': 'delta_w', 'delta_dn_norm_g': 'delta_w', 'delta_w_branch_sb': 'delta_w', 'delta_w_branch_dn': 'delta_w', 'delta_w_out': 'delta_w', 'new_m_ada_w': 'new_m', 'new_m_ada_b': 'new_m', 'new_m_norm_g': 'new_m', 'new_m_w_in': 'new_m', 'new_m_sb_q_g': 'new_m', 'new_m_sb_k_g': 'new_m', 'new_m_conv_w': 'new_m', 'new_m_dn_a_log': 'new_m', 'new_m_dn_dt_bias': 'new_m', 'new_m_dn_norm_g': 'new_m', 'new_m_w_branch_sb': 'new_m', 'new_m_w_branch_dn': 'new_m', 'new_m_w_out': 'new_m', 'new_v_ada_w': 'new_v', 'new_v_ada_b': 'new_v', 'new_v_norm_g': 'new_v', 'new_v_w_in': 'new_v', 'new_v_sb_q_g': 'new_v', 'new_v_sb_k_g': 'new_v', 'new_v_conv_w': 'new_v', 'new_v_dn_a_log': 'new_v', 'new_v_dn_dt_bias': 'new_v', 'new_v_dn_norm_g': 'new_v', 'new_v_w_branch_sb': 'new_v', 'new_v_w_branch_dn': 'new_v', 'new_v_w_out': 'new_v'}


def _forward(args):
    return _fwd_reference(*[args[k] for k in FWD_PARAMS])


def _output_shape():
    out = _jax.eval_shape(lambda: _forward(_fwd_setup_inputs(0)))
    return out.shape, out.dtype

N_MICROBATCH = 1
ADAM_LR = 0.001
ADAM_B1 = 0.9
ADAM_B2 = 0.999
ADAM_EPS = 1e-08
ADAM_WD = 0.01
ADAM_STEP = 10
PER_EXAMPLE_BATCH_AXIS = {'x': 0, 'c': 0, 'loss_target': 0}
SHARED_INPUTS = []
_WEIGHT_DTYPES = {'ada_w': _jnp.float32, 'ada_b': _jnp.float32, 'norm_g': _jnp.float32, 'w_in': _jnp.float32, 'sb_q_g': _jnp.float32, 'sb_k_g': _jnp.float32, 'conv_w': _jnp.float32, 'dn_a_log': _jnp.float32, 'dn_dt_bias': _jnp.float32, 'dn_norm_g': _jnp.float32, 'w_branch_sb': _jnp.float32, 'w_branch_dn': _jnp.float32, 'w_out': _jnp.float32}
MOMENT_SCALE = {'ada_w': 1.469269e-01, 'ada_b': 3.109094e-01, 'norm_g': 2.477217e-01, 'w_in': 1.543870e-02, 'sb_q_g': 1.110944e-01, 'sb_k_g': 1.102657e-01, 'conv_w': 2.122578e-02, 'dn_a_log': 4.640583e-01, 'dn_dt_bias': 4.520346e-01, 'dn_norm_g': 9.980343e-01, 'w_branch_sb': 1.130085e-02, 'w_branch_dn': 2.003932e-02, 'w_out': 2.240510e-02}


def _to_microbatches(a, axis):
    t = _jnp.moveaxis(a, axis, 0)
    t = t.reshape((N_MICROBATCH, t.shape[0] // N_MICROBATCH) + t.shape[1:])
    return _jnp.moveaxis(t, 1, axis + 1)


def setup_inputs(seed: int = 0) -> dict:
    inp = _fwd_setup_inputs(seed)
    key = _jax.random.fold_in(_jax.random.key(seed), 7919)
    shape, _ = _output_shape()
    out = dict(inp)
    out["loss_target"] = _jax.random.normal(_jax.random.fold_in(key, 0), shape, _jnp.float32)
    for i, name in enumerate(TWIN_WEIGHTS):
        w = inp[name].astype(_jnp.float32)
        if MOMENT_SCALE is None:
            s = _jnp.sqrt(_jnp.mean(_jnp.square(w)) + 1e-30)
        else:
            s = MOMENT_SCALE[name]
        km, kv = _jax.random.split(_jax.random.fold_in(key, i + 1))
        out[name] = w
        out["m_" + name] = s * _jax.random.normal(km, w.shape, _jnp.float32)
        out["v_" + name] = (s * s) * _jax.random.uniform(kv, w.shape, _jnp.float32, 0.5, 1.5)
    if N_MICROBATCH > 1:
        for name, axis in PER_EXAMPLE_BATCH_AXIS.items():
            out[name] = _to_microbatches(out[name], axis)
    return {'x': out['x'], 'c': out['c'], 'ada_w': out['ada_w'], 'ada_b': out['ada_b'], 'norm_g': out['norm_g'], 'w_in': out['w_in'], 'sb_q_g': out['sb_q_g'], 'sb_k_g': out['sb_k_g'], 'conv_w': out['conv_w'], 'dn_a_log': out['dn_a_log'], 'dn_dt_bias': out['dn_dt_bias'], 'dn_norm_g': out['dn_norm_g'], 'w_branch_sb': out['w_branch_sb'], 'w_branch_dn': out['w_branch_dn'], 'w_out': out['w_out'], 'loss_target': out['loss_target'], 'm_ada_w': out['m_ada_w'], 'm_ada_b': out['m_ada_b'], 'm_norm_g': out['m_norm_g'], 'm_w_in': out['m_w_in'], 'm_sb_q_g': out['m_sb_q_g'], 'm_sb_k_g': out['m_sb_k_g'], 'm_conv_w': out['m_conv_w'], 'm_dn_a_log': out['m_dn_a_log'], 'm_dn_dt_bias': out['m_dn_dt_bias'], 'm_dn_norm_g': out['m_dn_norm_g'], 'm_w_branch_sb': out['m_w_branch_sb'], 'm_w_branch_dn': out['m_w_branch_dn'], 'm_w_out': out['m_w_out'], 'v_ada_w': out['v_ada_w'], 'v_ada_b': out['v_ada_b'], 'v_norm_g': out['v_norm_g'], 'v_w_in': out['v_w_in'], 'v_sb_q_g': out['v_sb_q_g'], 'v_sb_k_g': out['v_sb_k_g'], 'v_conv_w': out['v_conv_w'], 'v_dn_a_log': out['v_dn_a_log'], 'v_dn_dt_bias': out['v_dn_dt_bias'], 'v_dn_norm_g': out['v_dn_norm_g'], 'v_w_branch_sb': out['v_w_branch_sb'], 'v_w_branch_dn': out['v_w_branch_dn'], 'v_w_out': out['v_w_out']}


def _loss(weights, diff, rest, loss_target):
    with _jax.named_scope("forward"):
        args = {**rest, TWIN_DIFF_INPUT: diff, **{k: w.astype(_WEIGHT_DTYPES[k]) for k, w in weights.items()}}
        y = _forward(args)
    with _jax.named_scope("loss_head"):
        err = _jnp.square(y.astype(_jnp.float32) - loss_target)
        return 0.5 * _jnp.sum(_jnp.mean(err, axis=-1)) if err.ndim else 0.5 * err


def _adamw(w, g, m, v):
    m = ADAM_B1 * m + (1.0 - ADAM_B1) * g
    v = ADAM_B2 * v + (1.0 - ADAM_B2) * _jnp.square(g)
    m_hat = m / (1.0 - ADAM_B1 ** ADAM_STEP)
    v_hat = v / (1.0 - ADAM_B2 ** ADAM_STEP)
    delta = -ADAM_LR * (m_hat / (_jnp.sqrt(v_hat) + ADAM_EPS) + ADAM_WD * w)
    return delta, m, v


def reference(x, c, ada_w, ada_b, norm_g, w_in, sb_q_g, sb_k_g, conv_w, dn_a_log, dn_dt_bias, dn_norm_g, w_branch_sb, w_branch_dn, w_out, loss_target, m_ada_w, m_ada_b, m_norm_g, m_w_in, m_sb_q_g, m_sb_k_g, m_conv_w, m_dn_a_log, m_dn_dt_bias, m_dn_norm_g, m_w_branch_sb, m_w_branch_dn, m_w_out, v_ada_w, v_ada_b, v_norm_g, v_w_in, v_sb_q_g, v_sb_k_g, v_conv_w, v_dn_a_log, v_dn_dt_bias, v_dn_norm_g, v_w_branch_sb, v_w_branch_dn, v_w_out):
    given = dict(x=x, c=c, ada_w=ada_w, ada_b=ada_b, norm_g=norm_g, w_in=w_in, sb_q_g=sb_q_g, sb_k_g=sb_k_g, conv_w=conv_w, dn_a_log=dn_a_log, dn_dt_bias=dn_dt_bias, dn_norm_g=dn_norm_g, w_branch_sb=w_branch_sb, w_branch_dn=w_branch_dn, w_out=w_out, loss_target=loss_target, m_ada_w=m_ada_w, m_ada_b=m_ada_b, m_norm_g=m_norm_g, m_w_in=m_w_in, m_sb_q_g=m_sb_q_g, m_sb_k_g=m_sb_k_g, m_conv_w=m_conv_w, m_dn_a_log=m_dn_a_log, m_dn_dt_bias=m_dn_dt_bias, m_dn_norm_g=m_dn_norm_g, m_w_branch_sb=m_w_branch_sb, m_w_branch_dn=m_w_branch_dn, m_w_out=m_w_out, v_ada_w=v_ada_w, v_ada_b=v_ada_b, v_norm_g=v_norm_g, v_w_in=v_w_in, v_sb_q_g=v_sb_q_g, v_sb_k_g=v_sb_k_g, v_conv_w=v_conv_w, v_dn_a_log=v_dn_a_log, v_dn_dt_bias=v_dn_dt_bias, v_dn_norm_g=v_dn_norm_g, v_w_branch_sb=v_w_branch_sb, v_w_branch_dn=v_w_branch_dn, v_w_out=v_w_out)
    weights = {n: given[n] for n in TWIN_WEIGHTS}
    shared = {n: given[n] for n in SHARED_INPUTS}
    per_example = {n: given[n] for n in ['x', 'c']}
    grad_fn = _jax.value_and_grad(_loss, argnums=(0, 1))

    def one_microbatch(ex, loss_target):
        ex = dict(ex)
        diff = ex.pop(TWIN_DIFF_INPUT)
        return grad_fn(weights, diff, {**shared, **ex}, loss_target)

    if N_MICROBATCH == 1:
        loss, (grad_w, grad_x) = one_microbatch(per_example, given["loss_target"])
    else:
        def body(carry, xs):
            loss_sum, grad_sum = carry
            l_k, (gw_k, gx_k) = one_microbatch(xs[0], xs[1])
            with _jax.named_scope("update"):
                return (loss_sum + l_k, _jax.tree.map(_jnp.add, grad_sum, gw_k)), gx_k

        init = (_jnp.zeros((), _jnp.float32), _jax.tree.map(_jnp.zeros_like, weights))
        (loss, grad_w), grad_x = _jax.lax.scan(body, init, (per_example, given["loss_target"]))
    with _jax.named_scope("update"):
        delta_w, new_m, new_v = {}, {}, {}
        for n in TWIN_WEIGHTS:
            delta_w[n], new_m[n], new_v[n] = _adamw(weights[n], grad_w[n], given["m_" + n], given["v_" + n])
    return (loss, grad_x, *[grad_w[n] for n in TWIN_WEIGHTS], *[delta_w[n] for n in TWIN_WEIGHTS],
            *[new_m[n] for n in TWIN_WEIGHTS], *[new_v[n] for n in TWIN_WEIGHTS])
```

```python
import math

import jax
import jax.numpy as jnp
from jax import lax
from jax.experimental import pallas as pl
from jax.experimental.pallas import tpu as pltpu

F32 = jnp.float32
BF16 = jnp.bfloat16
_MXU_DTYPE = BF16
_VMEM_LIMIT = 48 * 1024 * 1024
LANES = 128

EPS = 1e-6
SB_HEADS, SB_HD, SB_W = 8, 64, 512
DN_HEADS, DN_HD, DN_W = 4, 128, 512
CONV_K = 4
CHUNK = 64
QB = 128
ADAM_LR, ADAM_B1, ADAM_B2, ADAM_EPS, ADAM_WD, ADAM_STEP = 0.001, 0.9, 0.999, 1e-08, 0.01, 10

C_DN_QKV, C_DN_Z, C_SB_Q, C_SB_K, C_SB_V, C_SB_Z, C_MG = 0, 1536, 2048, 2560, 3072, 3584, 4096

_NN = (((1,), (0,)), ((), ()))
_NT = (((1,), (1,)), ((), ()))
_TN = (((0,), (0,)), ((), ()))
MESH = pl.DeviceIdType.MESH


def _sds(shape, dtype):
    return jax.ShapeDtypeStruct(shape, dtype)


def _cp(n):
    return pltpu.CompilerParams(dimension_semantics=("arbitrary",) * n, vmem_limit_bytes=_VMEM_LIMIT)


def _rb(tm, w, cb=0):
    return pl.BlockSpec((tm, w), lambda i: (i, cb))


def _fs(shape):
    nd = len(shape)
    return pl.BlockSpec(shape, lambda i: (0,) * nd)


def _dg(a, b, dims):
    return lax.dot_general(a, b, dims, preferred_element_type=F32)


def _mm(a, b, dims=_NN):
    return _dg(a.astype(_MXU_DTYPE), b.astype(_MXU_DTYPE), dims)


def _split3(x):
    hi = x.astype(BF16)
    r = x - hi.astype(F32)
    mid = r.astype(BF16)
    lo = (r - mid.astype(F32)).astype(BF16)
    return hi, mid, lo


def _mm_xl(x, const, dims=_NN):
    cb = const.astype(BF16)
    hi, mid, lo = _split3(x)
    return _dg(hi, cb, dims) + _dg(mid, cb, dims) + _dg(lo, cb, dims)


def _mm_xr(const, x, dims=_NN):
    cb = const.astype(BF16)
    hi, mid, lo = _split3(x)
    return _dg(cb, hi, dims) + _dg(cb, mid, dims) + _dg(cb, lo, dims)


def _mm3(a, b, dims=_NN):
    ah, am, _ = _split3(a)
    bh, bm, _ = _split3(b)
    return _dg(ah, bh, dims) + (_dg(ah, bm, dims) + _dg(am, bh, dims))


def _sigmoid(z):
    return 1.0 / (1.0 + jnp.exp(-z))


def _silu(z):
    return z * _sigmoid(z)


def _dsilu(z):
    s = _sigmoid(z)
    return s * (1.0 + z * (1.0 - s))


def _softplus(z):
    return jnp.maximum(z, 0.0) + jnp.log(1.0 + jnp.exp(-jnp.abs(z)))


def _iota2(shape, dim):
    return lax.broadcasted_iota(jnp.int32, shape, dim)


def _pick(n, cap, mult):
    best = None
    for t in range(mult, min(n, cap) + 1, mult):
        if n % t == 0:
            best = t
    assert best is not None, (n, cap, mult)
    return best


def _matmul(name, a, b, form, out_dtype, tm_cap=512, tn_cap=1024, tk_cap=1024):
    if form == "nn":
        (M, K), (_, N) = a.shape, b.shape
    elif form == "nt":
        (M, K), (N, _) = a.shape, b.shape
    else:
        (K, M), (_, N) = a.shape, b.shape
    tm = _pick(M, tm_cap, 128 if form == "tn" else 8)
    tn = _pick(N, tn_cap, 128)
    tk = _pick(K, tk_cap, 128)
    nk = K // tk
    dims = {"nn": _NN, "nt": _NT, "tn": _TN}[form]
    if form == "nn":
        a_spec = pl.BlockSpec((tm, tk), lambda i, j, k: (i, k))
        b_spec = pl.BlockSpec((tk, tn), lambda i, j, k: (k, j))
    elif form == "nt":
        a_spec = pl.BlockSpec((tm, tk), lambda i, j, k: (i, k))
        b_spec = pl.BlockSpec((tn, tk), lambda i, j, k: (j, k))
    else:
        a_spec = pl.BlockSpec((tk, tm), lambda i, j, k: (k, i))
        b_spec = pl.BlockSpec((tk, tn), lambda i, j, k: (k, j))

    def body(a_ref, b_ref, o_ref, acc_ref):
        k = pl.program_id(2)

        @pl.when(k == 0)
        def _():
            acc_ref[...] = jnp.zeros_like(acc_ref)

        acc_ref[...] += _mm(a_ref[...], b_ref[...], dims)

        @pl.when(k == nk - 1)
        def _():
            o_ref[...] = acc_ref[...].astype(o_ref.dtype)

    return pl.pallas_call(
        body, name=name, grid=(M // tm, N // tn, nk),
        in_specs=[a_spec, b_spec],
        out_specs=pl.BlockSpec((tm, tn), lambda i, j, k: (i, j)),
        out_shape=_sds((M, N), out_dtype),
        scratch_shapes=[pltpu.VMEM((tm, tn), F32)],
        compiler_params=_cp(3),
    )(a, b)


def _norm_mod(x, g, scale, shift, tm=256):
    S, D = x.shape

    def body(x_ref, g_ref, sc_ref, sh_ref, h_ref):
        xv = x_ref[...]
        r = lax.rsqrt(jnp.mean(xv * xv, axis=1, keepdims=True) + EPS)
        h_ref[...] = ((xv * r * g_ref[...]) * (1.0 + sc_ref[...]) + sh_ref[...]).astype(h_ref.dtype)

    return pl.pallas_call(
        body, name="norm_mod", grid=(S // tm,),
        in_specs=[_rb(tm, D), _fs((1, D)), _fs((1, D)), _fs((1, D))],
        out_specs=_rb(tm, D), out_shape=_sds((S, D), _MXU_DTYPE), compiler_params=_cp(1),
    )(x, g, scale, shift)


def _norm_mod_bwd(x, dh, dxn, g, scale, tm=256):
    S, D = x.shape

    def body(x_ref, dh_ref, dxn_ref, g_ref, sc_ref, dx_ref, dsh_ref, dsc_ref, dg_ref):
        @pl.when(pl.program_id(0) == 0)
        def _():
            dsh_ref[...] = jnp.zeros_like(dsh_ref)
            dsc_ref[...] = jnp.zeros_like(dsc_ref)
            dg_ref[...] = jnp.zeros_like(dg_ref)

        xv, dhv, gv = x_ref[...], dh_ref[...], g_ref[...]
        r = lax.rsqrt(jnp.mean(xv * xv, axis=1, keepdims=True) + EPS)
        xh = xv * r
        one_sc = 1.0 + sc_ref[...]
        dsh_ref[...] += jnp.sum(dhv, axis=0, keepdims=True)
        dsc_ref[...] += jnp.sum(dhv * xh * gv, axis=0, keepdims=True)
        dg_ref[...] += jnp.sum(dhv * one_sc * xh, axis=0, keepdims=True)
        dxh = dhv * (gv * one_sc)
        dx_ref[...] = r * (dxh - xh * jnp.mean(dxh * xh, axis=1, keepdims=True)) + dxn_ref[...]

    return pl.pallas_call(
        body, name="norm_mod_bwd", grid=(S // tm,),
        in_specs=[_rb(tm, D), _rb(tm, D), _rb(tm, D), _fs((1, D)), _fs((1, D))],
        out_specs=[_rb(tm, D), _fs((1, D)), _fs((1, D)), _fs((1, D))],
        out_shape=[_sds((S, D), F32)] + [_sds((1, D), F32)] * 3, compiler_params=_cp(1),
    )(x, dh, dxn, g, scale)


def _head_sum_matrix():
    r = jnp.arange(SB_W)
    return (r[:, None] // SB_HD == r[None, :] // SB_HD).astype(BF16)


def _sb_prep(p, gq_t, gk_t, tm=256):
    S = p.shape[0]
    bd = _head_sum_matrix()

    def body(q_ref, k_ref, gq_ref, gk_ref, bd_ref, qn_ref, kn_ref):
        for src, g_ref, dst in ((q_ref, gq_ref, qn_ref), (k_ref, gk_ref, kn_ref)):
            v = src[...]
            ms = _mm_xl(v * v, bd_ref[...]) * (1.0 / SB_HD)
            dst[...] = (v * lax.rsqrt(ms + EPS) * g_ref[...]).astype(dst.dtype)

    return pl.pallas_call(
        body, name="sb_prep", grid=(S // tm,),
        in_specs=[_rb(tm, SB_W, C_SB_Q // SB_W), _rb(tm, SB_W, C_SB_K // SB_W),
                  _fs((1, SB_W)), _fs((1, SB_W)), _fs((SB_W, SB_W))],
        out_specs=[_rb(tm, SB_W), _rb(tm, SB_W)],
        out_shape=[_sds((S, SB_W), _MXU_DTYPE)] * 2, compiler_params=_cp(1),
    )(p, p, gq_t, gk_t, bd)


def _sb_prep_bwd(p, dqn, dkn, gq_t, gk_t, tm=256):
    S = p.shape[0]
    bd = _head_sum_matrix()

    def body(q_ref, k_ref, dqn_ref, dkn_ref, gq_ref, gk_ref, bd_ref, dq_ref, dk_ref, dgq_ref, dgk_ref):
        @pl.when(pl.program_id(0) == 0)
        def _():
            dgq_ref[...] = jnp.zeros_like(dgq_ref)
            dgk_ref[...] = jnp.zeros_like(dgk_ref)

        for src, dn_ref, g_ref, dst, dg_ref in ((q_ref, dqn_ref, gq_ref, dq_ref, dgq_ref),
                                                (k_ref, dkn_ref, gk_ref, dk_ref, dgk_ref)):
            v, dn = src[...], dn_ref[...]
            r = lax.rsqrt(_mm_xl(v * v, bd_ref[...]) * (1.0 / SB_HD) + EPS)
            vh = v * r
            dg_ref[...] += jnp.sum(dn * vh, axis=0, keepdims=True)
            dvh = dn * g_ref[...]
            m = _mm_xl(dvh * vh, bd_ref[...]) * (1.0 / SB_HD)
            dst[...] = (r * (dvh - vh * m)).astype(dst.dtype)

    return pl.pallas_call(
        body, name="sb_prep_bwd", grid=(S // tm,),
        in_specs=[_rb(tm, SB_W, C_SB_Q // SB_W), _rb(tm, SB_W, C_SB_K // SB_W), _rb(tm, SB_W), _rb(tm, SB_W),
                  _fs((1, SB_W)), _fs((1, SB_W)), _fs((SB_W, SB_W))],
        out_specs=[_rb(tm, SB_W), _rb(tm, SB_W), _fs((1, SB_W)), _fs((1, SB_W))],
        out_shape=[_sds((S, SB_W), _MXU_DTYPE)] * 2 + [_sds((1, SB_W), F32)] * 2, compiler_params=_cp(1),
    )(p, p, dqn, dkn, gq_t, gk_t, bd)


def _sb_consts():
    r, c = _iota2((QB, QB), 0), _iota2((QB, QB), 1)
    lane = _iota2((1, LANES), 1)
    return r, c, lane


def _sb_fwd(qn, kn, p):
    S = qn.shape[0]
    scale = 1.0 / math.sqrt(SB_HD)

    def body(q_ref, k_ref, v_ref, o_ref, tot_ref):
        i = pl.program_id(1)
        r, c, lane = _sb_consts()
        u_gt = (r > c).astype(BF16)
        strict = c < r
        q = q_ref[...]
        o_acc = jnp.zeros((QB, LANES), F32)
        tot = jnp.zeros((QB, LANES), F32)
        for h in range(2):
            mask = (lane // SB_HD) == h
            qh = jnp.where(mask, q, jnp.zeros_like(q))

            def step(jj, carry, qh=qh):
                o_h, run = carry
                off = pl.multiple_of((i - jj) * QB, QB)
                kj = k_ref[pl.ds(off, QB), :]
                vj = v_ref[pl.ds(off, QB), :]
                z = _mm(qh, kj, _NT) * scale
                sp = _softplus(z)
                valid = jnp.logical_or(jj > 0, strict)
                sp_m = jnp.where(valid, sp, 0.0)
                later = _mm_xl(sp_m, u_gt)
                w = jnp.where(valid, jnp.exp((z - sp) - later - run), 0.0)
                o_h = o_h + _mm(w, vj)
                run = run + jnp.sum(sp_m, axis=1, keepdims=True)
                return o_h, run

            o_h, run = lax.fori_loop(0, i + 1, step, (jnp.zeros((QB, LANES), F32), jnp.zeros((QB, 1), F32)))
            o_acc = jnp.where(mask, o_h, o_acc)
            tot = jnp.where(mask, run, tot)
        o_ref[...] = o_acc
        tot_ref[...] = tot

    return pl.pallas_call(
        body, name="sb_fwd", grid=(SB_W // LANES, S // QB),
        in_specs=[pl.BlockSpec((QB, LANES), lambda hp, i: (i, hp)),
                  pl.BlockSpec((S, LANES), lambda hp, i: (0, hp)),
                  pl.BlockSpec((S, LANES), lambda hp, i: (0, C_SB_V // LANES + hp))],
        out_specs=[pl.BlockSpec((QB, LANES), lambda hp, i: (i, hp))] * 2,
        out_shape=[_sds((S, SB_W), F32)] * 2, compiler_params=_cp(2),
    )(qn, kn, p)


def _sb_bwd(qn, kn, p, do, tot):
    S = qn.shape[0]
    scale = 1.0 / math.sqrt(SB_HD)

    def body(q_ref, k_ref, v_ref, do_ref, tot_ref, dq_ref, dk_ref, dv_ref):
        i = pl.program_id(1)

        @pl.when(i == 0)
        def _():
            dk_ref[...] = jnp.zeros_like(dk_ref)
            dv_ref[...] = jnp.zeros_like(dv_ref)

        r, c, lane = _sb_consts()
        u_le = (r <= c).astype(BF16)
        u_lt = (r < c).astype(BF16)
        strict = c < r
        q = q_ref[...]
        do = do_ref[...]
        tot_pair = tot_ref[...]
        dq_acc = jnp.zeros((QB, LANES), F32)
        for h in range(2):
            mask = (lane // SB_HD) == h
            qh = jnp.where(mask, q, jnp.zeros_like(q))
            doh = jnp.where(mask, do, 0.0)
            tot_h = jnp.max(jnp.where(mask, tot_pair, 0.0), axis=1, keepdims=True)

            def step(j, carry, mask=mask, qh=qh, doh=doh, tot_h=tot_h):
                dq_h, pre_sp, pre_e = carry
                off = pl.multiple_of(j * QB, QB)
                kj = k_ref[pl.ds(off, QB), :]
                vj = v_ref[pl.ds(off, QB), :]
                z = _mm(qh, kj, _NT) * scale
                sp = _softplus(z)
                valid = jnp.logical_or(j < i, strict)
                sp_m = jnp.where(valid, sp, 0.0)
                incl = _mm_xl(sp_m, u_le)
                w = jnp.where(valid, jnp.exp((z - sp) - (tot_h - pre_sp - incl)), 0.0)
                dw = _mm(doh, vj, _NT)
                e = w * dw
                db = pre_e + _mm_xl(e, u_lt)
                sig = _sigmoid(z)
                dz = jnp.where(valid, e * (1.0 - sig) - db * sig, 0.0) * scale
                dq_h = dq_h + _mm(dz, kj)
                dk_ref[pl.ds(off, QB), :] += jnp.where(mask, _mm(dz, q, _TN), 0.0)
                dv_ref[pl.ds(off, QB), :] += jnp.where(mask, _mm(w, do, _TN), 0.0)
                pre_sp = pre_sp + jnp.sum(sp_m, axis=1, keepdims=True)
                pre_e = pre_e + jnp.sum(e, axis=1, keepdims=True)
                return dq_h, pre_sp, pre_e

            zero_col = jnp.zeros((QB, 1), F32)
            dq_h, _, _ = lax.fori_loop(0, i + 1, step, (jnp.zeros((QB, LANES), F32), zero_col, zero_col))
            dq_acc = jnp.where(mask, dq_h, dq_acc)
        dq_ref[...] = dq_acc

    blk = pl.BlockSpec((QB, LANES), lambda hp, i: (i, hp))
    full = pl.BlockSpec((S, LANES), lambda hp, i: (0, hp))
    return pl.pallas_call(
        body, name="sb_bwd", grid=(SB_W // LANES, S // QB),
        in_specs=[blk, full, pl.BlockSpec((S, LANES), lambda hp, i: (0, C_SB_V // LANES + hp)), blk, blk],
        out_specs=[blk, full, full],
        out_shape=[_sds((S, SB_W), F32)] * 3, compiler_params=_cp(2),
    )(qn, kn, p, do, tot)


def _dn_prep(p, conv_w, a_row, dtb_row, tm=256):
    S = p.shape[0]
    W3 = 3 * DN_W
    nhalo = tm // 8

    def body(x_ref, halo_ref, w_ref, ba_ref, a_ref, dtb_ref, qkv_ref, bb_ref, gc_ref, gl_ref):
        i = pl.program_id(0)
        halo = jnp.where(i > 0, halo_ref[...], 0.0)
        xf = jnp.concatenate([halo, x_ref[...]], axis=0)
        acc = jnp.zeros((tm, W3), F32)
        for k in range(CONV_K):
            sh = CONV_K - 1 - k
            xs = xf if sh == 0 else pltpu.roll(xf, sh, 0)
            acc = acc + xs[8:, :] * w_ref[k:k + 1, :]
        s = _silu(acc)
        for gi in range(2 * DN_HEADS):
            sl = slice(gi * LANES, (gi + 1) * LANES)
            sg = s[:, sl]
            rinv = lax.rsqrt(jnp.sum(sg * sg, axis=1, keepdims=True) + EPS)
            qkv_ref[:, sl] = sg * rinv * (DN_HD ** -0.5 if gi < DN_HEADS else 1.0)
        qkv_ref[:, 2 * DN_W:] = s[:, 2 * DN_W:]

        ba = ba_ref[...]
        beta = _sigmoid(ba)
        g = -jnp.exp(a_ref[...]) * _softplus(ba + dtb_ref[...])
        lr, lc = _iota2((LANES, DN_W), 0), _iota2((LANES, DN_W), 1)
        sel_b = (lr == lc // LANES).astype(BF16)
        sel_g = (lr == lc // LANES + DN_HEADS).astype(BF16)
        bb_ref[...] = _mm_xl(beta, sel_b)
        graw = _mm_xl(g, sel_g)
        rr, cc = _iota2((tm, tm), 0), _iota2((tm, tm), 1)
        tri = jnp.logical_and(rr >= cc, rr // CHUNK == cc // CHUNK).astype(BF16)
        gc = _mm_xr(tri, graw)
        last = (cc == (rr // CHUNK) * CHUNK + (CHUNK - 1)).astype(BF16)
        gc_ref[...] = gc
        gl_ref[...] = _mm_xr(last, gc)

    return pl.pallas_call(
        body, name="dn_prep", grid=(S // tm,),
        in_specs=[_rb(tm, W3, 0), pl.BlockSpec((8, W3), lambda i: (jnp.maximum(i * nhalo - 1, 0), 0)),
                  _fs((CONV_K, W3)), _rb(tm, LANES, (p.shape[1] - LANES) // LANES),
                  _fs((1, LANES)), _fs((1, LANES))],
        out_specs=[_rb(tm, W3), _rb(tm, DN_W), _rb(tm, DN_W), _rb(tm, DN_W)],
        out_shape=[_sds((S, W3), F32)] + [_sds((S, DN_W), F32)] * 3, compiler_params=_cp(1),
    )(p, p, conv_w, p, a_row, dtb_row)


def _dn_chunk_terms(q, k, v, beta, gc, gl):
    r, c = _iota2((CHUNK, CHUNK), 0), _iota2((CHUNK, CHUNK), 1)
    tril, strict = r >= c, r > c
    gcol = _mm_xl(gc, jnp.full((LANES, CHUNK), 1.0 / LANES, F32))
    grow = _mm_xr(jnp.full((CHUNK, LANES), 1.0 / LANES, F32), gc, _NT)
    dec = jnp.where(tril, jnp.exp(jnp.where(tril, gcol - grow, 0.0)), 0.0)
    gam = jnp.exp(gc)
    dlt = jnp.exp(gl - gc)
    kb, vb = k * beta, v * beta
    pm = _mm(kb, k, _NT)
    qk = _mm(q, k, _NT)
    m = jnp.where(strict, pm * dec, 0.0)
    a = jnp.where(tril, qk * dec, 0.0)
    return dict(tril=tril, strict=strict, dec=dec, gam=gam, dlt=dlt, kb=kb, vb=vb, m=m, a=a)


def _dn_fwd(qkv, bb, gcb, glb):
    S = qkv.shape[0]
    N = S // CHUNK

    def body(qkv_ref, bb_ref, gc_ref, gl_ref, o_ref, t_ref, sall_ref, s_scr):
        @pl.when(pl.program_id(0) == 0)
        def _():
            s_scr[...] = jnp.zeros_like(s_scr)

        r, c = _iota2((CHUNK, CHUNK), 0), _iota2((CHUNK, CHUNK), 1)
        eye = (r == c).astype(F32)
        for h in range(DN_HEADS):
            sl = slice(h * LANES, (h + 1) * LANES)
            q, k = qkv_ref[:, sl], qkv_ref[:, DN_W + h * LANES:DN_W + (h + 1) * LANES]
            v = qkv_ref[:, 2 * DN_W + h * LANES:2 * DN_W + (h + 1) * LANES]
            beta, gc, gl = bb_ref[:, sl], gc_ref[:, sl], gl_ref[:, sl]
            s0 = s_scr[h]
            sall_ref[0, h] = s0.astype(sall_ref.dtype)
            s0 = s0.astype(sall_ref.dtype).astype(F32)
            t = _dn_chunk_terms(q, k, v, beta, gc, gl)
            pw = -t["m"]
            tinv = eye + pw
            for _ in range(5):
                pw = _mm3(pw, pw)
                tinv = tinv + _mm3(tinv, pw)
            t_ref[h] = tinv
            u = _mm3(tinv, t["vb"])
            w = _mm3(tinv, t["kb"] * t["gam"])
            vn = u - _mm(w, s0)
            o_ref[:, sl] = _mm(q * t["gam"], s0) + _mm(t["a"], vn)
            egl = jnp.exp(jnp.concatenate([gl, gl], axis=0))
            s_scr[h] = s_scr[h] * egl + _mm(k * t["dlt"], vn, _TN)

    return pl.pallas_call(
        body, name="dn_fwd", grid=(N,),
        in_specs=[_rb(CHUNK, 3 * DN_W), _rb(CHUNK, DN_W), _rb(CHUNK, DN_W), _rb(CHUNK, DN_W)],
        out_specs=[_rb(CHUNK, DN_W), pl.BlockSpec((DN_HEADS, CHUNK, CHUNK), lambda n: (0, n, 0)),
                   pl.BlockSpec((1, DN_HEADS, DN_HD, DN_HD), lambda n: (n, 0, 0, 0))],
        out_shape=[_sds((S, DN_W), F32), _sds((DN_HEADS, S, CHUNK), F32),
                   _sds((N, DN_HEADS, DN_HD, DN_HD), _MXU_DTYPE)],
        scratch_shapes=[pltpu.VMEM((DN_HEADS, DN_HD, DN_HD), F32)],
        compiler_params=_cp(1),
    )(qkv, bb, gcb, glb)


def _dn_bwd(qkv, bb, gcb, glb, tinv_all, sall, do):
    S = qkv.shape[0]
    N = S // CHUNK

    def body(qkv_ref, bb_ref, gc_ref, gl_ref, t_ref, sall_ref, do_ref, dqkv_ref, dbb_ref, dg_ref, ds_scr):
        @pl.when(pl.program_id(0) == 0)
        def _():
            ds_scr[...] = jnp.zeros_like(ds_scr)

        r, c = _iota2((CHUNK, CHUNK), 0), _iota2((CHUNK, CHUNK), 1)
        eye = (r == c).astype(F32)
        u_ge = (c >= r).astype(F32)
        last_row = _iota2((CHUNK, LANES), 0) == CHUNK - 1
        for h in range(DN_HEADS):
            sl = slice(h * LANES, (h + 1) * LANES)
            slk = slice(DN_W + h * LANES, DN_W + (h + 1) * LANES)
            slv = slice(2 * DN_W + h * LANES, 2 * DN_W + (h + 1) * LANES)
            q, k, v = qkv_ref[:, sl], qkv_ref[:, slk], qkv_ref[:, slv]
            beta, gc, gl = bb_ref[:, sl], gc_ref[:, sl], gl_ref[:, sl]
            tinv = t_ref[h]
            s0 = sall_ref[0, h].astype(F32)
            do = do_ref[:, sl]
            ds1 = ds_scr[h]
            t = _dn_chunk_terms(q, k, v, beta, gc, gl)
            gam, dlt, kb, vb, dec = t["gam"], t["dlt"], t["kb"], t["vb"], t["dec"]
            kbg = kb * gam
            u = _mm3(tinv, vb)
            w = _mm3(tinv, kbg)
            vn = u - _mm(w, s0)
            qg, kd = q * gam, k * dlt
            egl = jnp.exp(gl)
            egl2 = jnp.concatenate([egl, egl], axis=0)

            dvn = _mm(t["a"], do, _TN) + _mm(kd, ds1)
            da = jnp.where(t["tril"], _mm(do, vn, _NT), 0.0)
            dqg = _mm(do, s0, _NT)
            dkd = _mm(vn, ds1, _NT)
            dw = -_mm(dvn, s0, _NT)
            ds_scr[h] = _mm(qg, do, _TN) + egl2 * ds1 - _mm(w, dvn, _TN)
            tt = _mm_xr(eye, tinv, _NT)
            dvb = _mm3(tt, dvn)
            dkbg = _mm3(tt, dw)
            dm = -jnp.where(t["strict"], _mm(dvb, u, _NT) + _mm(dkbg, w, _NT), 0.0)
            dpm = dm * dec
            dqk = da * dec
            dkb = dkbg * gam + _mm(dpm, k)
            dk = dkd * dlt + _mm(dpm, kb, _TN) + _mm(dqk, q, _TN) + dkb * beta
            dq = dqg * gam + _mm(dqk, k)
            dqkv_ref[:, sl] = dq
            dqkv_ref[:, slk] = dk
            dqkv_ref[:, slv] = dvb * beta
            dbb_ref[:, sl] = jnp.broadcast_to(
                jnp.sum(dkb * k, axis=1, keepdims=True) + jnp.sum(dvb * v, axis=1, keepdims=True), (CHUNK, LANES))
            dgam = jnp.sum(dqg * q, axis=1, keepdims=True) + jnp.sum(dkbg * kb, axis=1, keepdims=True)
            ddlt = jnp.sum(dkd * k, axis=1, keepdims=True)
            xm = dm * t["m"] + da * t["a"]
            xt = _mm_xr(eye, xm, _NT)
            dgc = (dgam * gam - ddlt * dlt + jnp.sum(xm, axis=1, keepdims=True)
                   - jnp.sum(xt, axis=1, keepdims=True))
            dgl = jnp.sum(ddlt * dlt, axis=0, keepdims=True) + jnp.sum(
                jnp.sum(ds1 * s0, axis=1, keepdims=True), axis=0, keepdims=True) * jnp.max(egl, axis=0, keepdims=True)
            dgc = dgc + jnp.where(last_row, dgl, 0.0)
            dg_ref[:, sl] = _mm_xr(u_ge, dgc)

    rev = lambda w: pl.BlockSpec((CHUNK, w), lambda n: (N - 1 - n, 0))
    return pl.pallas_call(
        body, name="dn_bwd", grid=(N,),
        in_specs=[rev(3 * DN_W), rev(DN_W), rev(DN_W), rev(DN_W),
                  pl.BlockSpec((DN_HEADS, CHUNK, CHUNK), lambda n: (0, N - 1 - n, 0)),
                  pl.BlockSpec((1, DN_HEADS, DN_HD, DN_HD), lambda n: (N - 1 - n, 0, 0, 0)), rev(DN_W)],
        out_specs=[rev(3 * DN_W), rev(DN_W), rev(DN_W)],
        out_shape=[_sds((S, 3 * DN_W), F32), _sds((S, DN_W), F32), _sds((S, DN_W), F32)],
        scratch_shapes=[pltpu.VMEM((DN_HEADS, DN_HD, DN_HD), F32)],
        compiler_params=_cp(1),
    )(qkv, bb, gcb, glb, tinv_all, sall, do)


def _dn_prep_bwd_a(p, dqkv, dbb, dgb, conv_w, a_row, dtb_row, tm=256):
    S, PC = p.shape
    W3 = 3 * DN_W
    nhalo = tm // 8

    def body(x_ref, halo_ref, w_ref, ba_ref, a_ref, dtb_ref, dqkv_ref, dbb_ref, dgb_ref,
             dc_ref, dba_ref, dal_ref, ddt_ref):
        i = pl.program_id(0)

        @pl.when(i == 0)
        def _():
            dal_ref[...] = jnp.zeros_like(dal_ref)
            ddt_ref[...] = jnp.zeros_like(ddt_ref)

        halo = jnp.where(i > 0, halo_ref[...], 0.0)
        xf = jnp.concatenate([halo, x_ref[...]], axis=0)
        acc = jnp.zeros((tm, W3), F32)
        for k in range(CONV_K):
            sh = CONV_K - 1 - k
            xs = xf if sh == 0 else pltpu.roll(xf, sh, 0)
            acc = acc + xs[8:, :] * w_ref[k:k + 1, :]
        s = _silu(acc)
        ds_act = _dsilu(acc)
        for gi in range(2 * DN_HEADS):
            sl = slice(gi * LANES, (gi + 1) * LANES)
            sg = s[:, sl]
            rinv = lax.rsqrt(jnp.sum(sg * sg, axis=1, keepdims=True) + EPS)
            nh = sg * rinv
            dn = dqkv_ref[:, sl] * (DN_HD ** -0.5 if gi < DN_HEADS else 1.0)
            dsg = rinv * (dn - nh * jnp.sum(dn * nh, axis=1, keepdims=True))
            dc_ref[:, sl] = dsg * ds_act[:, sl]
        dc_ref[:, 2 * DN_W:] = dqkv_ref[:, 2 * DN_W:] * ds_act[:, 2 * DN_W:]

        ba = ba_ref[...]
        beta = _sigmoid(ba)
        ea = jnp.exp(a_ref[...])
        pre = ba + dtb_ref[...]
        g = -ea * _softplus(pre)
        lr, lc = _iota2((DN_W, LANES), 0), _iota2((DN_W, LANES), 1)
        pick_b = jnp.where(lc == lr // LANES, 1.0 / LANES, 0.0)
        pick_g = jnp.where(lc == lr // LANES + DN_HEADS, 1.0 / LANES, 0.0)
        dbeta = _mm_xl(dbb_ref[...], pick_b)
        dg = _mm_xl(dgb_ref[...], pick_g)
        lane = _iota2((1, LANES), 1)
        da = dg * (-ea) * _sigmoid(pre)
        dba_ref[...] = jnp.where(lane < DN_HEADS, dbeta * beta * (1.0 - beta),
                                 jnp.where(lane < 2 * DN_HEADS, da, 0.0)).astype(dba_ref.dtype)
        dal_ref[...] += jnp.sum(dg * g, axis=0, keepdims=True)
        ddt_ref[...] += jnp.sum(da, axis=0, keepdims=True)

    return pl.pallas_call(
        body, name="dn_prep_bwd_a", grid=(S // tm,),
        in_specs=[_rb(tm, W3, 0), pl.BlockSpec((8, W3), lambda i: (jnp.maximum(i * nhalo - 1, 0), 0)),
                  _fs((CONV_K, W3)), _rb(tm, LANES, (PC - LANES) // LANES), _fs((1, LANES)), _fs((1, LANES)),
                  _rb(tm, W3), _rb(tm, DN_W), _rb(tm, DN_W)],
        out_specs=[_rb(tm, W3), _rb(tm, LANES), _fs((1, LANES)), _fs((1, LANES))],
        out_shape=[_sds((S, W3), F32), _sds((S, LANES), _MXU_DTYPE), _sds((1, LANES), F32), _sds((1, LANES), F32)],
        compiler_params=_cp(1),
    )(p, p, conv_w, p, a_row, dtb_row, dqkv, dbb, dgb)


def _dn_prep_bwd_b(p, dc, conv_w, tm=256):
    S = p.shape[0]
    W3 = 3 * DN_W
    nhalo = tm // 8
    nblk = S // tm

    def body(x_ref, xh_ref, dc_ref, dch_ref, w_ref, dx_ref, dw_ref):
        i = pl.program_id(0)

        @pl.when(i == 0)
        def _():
            dw_ref[...] = jnp.zeros_like(dw_ref)

        dcv = dc_ref[...]
        xf = jnp.concatenate([jnp.where(i > 0, xh_ref[...], 0.0), x_ref[...]], axis=0)
        df = jnp.concatenate([dcv, jnp.where(i < nblk - 1, dch_ref[...], 0.0)], axis=0)
        acc = jnp.zeros((tm, W3), F32)
        for k in range(CONV_K):
            sh = CONV_K - 1 - k
            xs = xf if sh == 0 else pltpu.roll(xf, sh, 0)
            dw_ref[k:k + 1, :] += jnp.sum(dcv * xs[8:, :], axis=0, keepdims=True)
            ds = df if sh == 0 else pltpu.roll(df, tm + 8 - sh, 0)
            acc = acc + ds[:tm, :] * w_ref[k:k + 1, :]
        dx_ref[...] = acc.astype(dx_ref.dtype)

    return pl.pallas_call(
        body, name="dn_prep_bwd_b", grid=(nblk,),
        in_specs=[_rb(tm, W3, 0), pl.BlockSpec((8, W3), lambda i: (jnp.maximum(i * nhalo - 1, 0), 0)),
                  _rb(tm, W3), pl.BlockSpec((8, W3), lambda i: (jnp.minimum((i + 1) * nhalo, S // 8 - 1), 0)),
                  _fs((CONV_K, W3))],
        out_specs=[_rb(tm, W3), _fs((CONV_K, W3))],
        out_shape=[_sds((S, W3), _MXU_DTYPE), _sds((CONV_K, W3), F32)], compiler_params=_cp(1),
    )(p, p, dc, dc, conv_w)


def _gate(o_att, o_dn, p, gn, tm=256):
    S = p.shape[0]

    def body(oa_ref, zs_ref, od_ref, zd_ref, gn_ref, osb_ref, odn_ref):
        osb_ref[...] = (oa_ref[...] * _silu(zs_ref[...])).astype(osb_ref.dtype)
        for h in range(DN_HEADS):
            sl = slice(h * LANES, (h + 1) * LANES)
            o = od_ref[:, sl]
            r = lax.rsqrt(jnp.mean(o * o, axis=1, keepdims=True) + EPS)
            odn_ref[:, sl] = (o * r * gn_ref[...] * _silu(zd_ref[:, sl])).astype(odn_ref.dtype)

    return pl.pallas_call(
        body, name="gate", grid=(S // tm,),
        in_specs=[_rb(tm, SB_W), _rb(tm, SB_W, C_SB_Z // SB_W), _rb(tm, DN_W), _rb(tm, DN_W, C_DN_Z // DN_W),
                  _fs((1, LANES))],
        out_specs=[_rb(tm, SB_W), _rb(tm, DN_W)],
        out_shape=[_sds((S, SB_W), _MXU_DTYPE), _sds((S, DN_W), _MXU_DTYPE)], compiler_params=_cp(1),
    )(o_att, p, o_dn, p, gn)


def _gate_bwd(db_sb, db_dn, wb_sb, wb_dn, o_att, o_dn, p, gn, tm=256):
    S = p.shape[0]
    D = db_sb.shape[1]

    def body(dbs_ref, dbd_ref, ws_ref, wd_ref, oa_ref, zs_ref, od_ref, zd_ref, gn_ref,
             doa_ref, dzs_ref, dod_ref, dzd_ref, dgn_ref):
        @pl.when(pl.program_id(0) == 0)
        def _():
            dgn_ref[...] = jnp.zeros_like(dgn_ref)

        do_sb = _mm(dbs_ref[...], ws_ref[...], _NT)
        zs = zs_ref[...]
        doa_ref[...] = do_sb * _silu(zs)
        dzs_ref[...] = (do_sb * oa_ref[...] * _dsilu(zs)).astype(dzs_ref.dtype)
        do_dnn = _mm(dbd_ref[...], wd_ref[...], _NT)
        gnv = gn_ref[...]
        for h in range(DN_HEADS):
            sl = slice(h * LANES, (h + 1) * LANES)
            o, z, dout = od_ref[:, sl], zd_ref[:, sl], do_dnn[:, sl]
            r = lax.rsqrt(jnp.mean(o * o, axis=1, keepdims=True) + EPS)
            oh = o * r
            sz = _silu(z)
            dzd_ref[:, sl] = (dout * oh * gnv * _dsilu(z)).astype(dzd_ref.dtype)
            dgn_ref[...] += jnp.sum(dout * sz * oh, axis=0, keepdims=True)
            doh = dout * gnv * sz
            dod_ref[:, sl] = r * (doh - oh * jnp.mean(doh * oh, axis=1, keepdims=True))

    return pl.pallas_call(
        body, name="gate_bwd", grid=(S // tm,),
        in_specs=[_rb(tm, D), _rb(tm, D), _fs((SB_W, D)), _fs((DN_W, D)), _rb(tm, SB_W),
                  _rb(tm, SB_W, C_SB_Z // SB_W), _rb(tm, DN_W), _rb(tm, DN_W, C_DN_Z // DN_W), _fs((1, LANES))],
        out_specs=[_rb(tm, SB_W), _rb(tm, SB_W), _rb(tm, DN_W), _rb(tm, DN_W), _fs((1, LANES))],
        out_shape=[_sds((S, SB_W), F32), _sds((S, SB_W), _MXU_DTYPE), _sds((S, DN_W), F32),
                   _sds((S, DN_W), _MXU_DTYPE), _sds((1, LANES), F32)],
        compiler_params=_cp(1),
    )(db_sb, db_dn, wb_sb, wb_dn, o_att, p, o_dn, p, gn)


def _branch(o_sb, o_dnn, wb_sb, wb_dn, p, D, tm=256):
    S = p.shape[0]

    def body(os_ref, od_ref, ws_ref, wd_ref, ms_ref, md_ref, y_ref, bs_ref, bd_ref):
        bs = _mm(os_ref[...], ws_ref[...])
        bdn = _mm(od_ref[...], wd_ref[...])
        bs_ref[...] = bs
        bd_ref[...] = bdn
        y_ref[...] = (_sigmoid(ms_ref[...]) * bs + _sigmoid(md_ref[...]) * bdn).astype(y_ref.dtype)

    return pl.pallas_call(
        body, name="branch", grid=(S // tm,),
        in_specs=[_rb(tm, SB_W), _rb(tm, DN_W), _fs((SB_W, D)), _fs((DN_W, D)),
                  _rb(tm, D, C_MG // D), _rb(tm, D, C_MG // D + 1)],
        out_specs=[_rb(tm, D), _rb(tm, D), _rb(tm, D)],
        out_shape=[_sds((S, D), _MXU_DTYPE), _sds((S, D), F32), _sds((S, D), F32)], compiler_params=_cp(1),
    )(o_sb, o_dnn, wb_sb, wb_dn, p, p)


def _out_proj(x, y, w_out, gate, tm=256):
    S, D = x.shape

    def body(x_ref, y_ref, w_ref, g_ref, xn_ref, out_ref):
        out = _mm(y_ref[...], w_ref[...])
        out_ref[...] = out
        xn_ref[...] = x_ref[...] + g_ref[...] * out

    return pl.pallas_call(
        body, name="out_proj", grid=(S // tm,),
        in_specs=[_rb(tm, D), _rb(tm, D), _fs((D, D)), _fs((1, D))],
        out_specs=[_rb(tm, D), _rb(tm, D)],
        out_shape=[_sds((S, D), F32), _sds((S, D), F32)], compiler_params=_cp(1),
    )(x, y, w_out, gate)


def _out_bwd(dxn, out, gate, w_out, p, b_sb, b_dn, tm=256):
    S, D = dxn.shape

    def body(dxn_ref, out_ref, g_ref, w_ref, ms_ref, md_ref, bs_ref, bd_ref,
             dout_ref, dbs_ref, dbd_ref, dm_ref, dgate_ref):
        @pl.when(pl.program_id(0) == 0)
        def _():
            dgate_ref[...] = jnp.zeros_like(dgate_ref)

        dxv = dxn_ref[...]
        dgate_ref[...] += jnp.sum(dxv * out_ref[...], axis=0, keepdims=True)
        dout = (g_ref[...] * dxv).astype(dout_ref.dtype)
        dout_ref[...] = dout
        dy = _mm(dout, w_ref[...], _NT)
        s1, s2 = _sigmoid(ms_ref[...]), _sigmoid(md_ref[...])
        dbs_ref[...] = (dy * s1).astype(dbs_ref.dtype)
        dbd_ref[...] = (dy * s2).astype(dbd_ref.dtype)
        dm_ref[:, :D] = (dy * bs_ref[...] * s1 * (1.0 - s1)).astype(dm_ref.dtype)
        dm_ref[:, D:] = (dy * bd_ref[...] * s2 * (1.0 - s2)).astype(dm_ref.dtype)

    return pl.pallas_call(
        body, name="out_bwd", grid=(S // tm,),
        in_specs=[_rb(tm, D), _rb(tm, D), _fs((1, D)), _fs((D, D)), _rb(tm, D, C_MG // D),
                  _rb(tm, D, C_MG // D + 1), _rb(tm, D), _rb(tm, D)],
        out_specs=[_rb(tm, D), _rb(tm, D), _rb(tm, D), _rb(tm, 2 * D), _fs((1, D))],
        out_shape=[_sds((S, D), _MXU_DTYPE)] * 3 + [_sds((S, 2 * D), _MXU_DTYPE), _sds((1, D), F32)],
        compiler_params=_cp(1),
    )(dxn, out, gate, w_out, p, p, b_sb, b_dn)


def _loss_head(xf, target, tm=256):
    S, D = xf.shape

    def body(x_ref, t_ref, dy_ref, loss_ref):
        @pl.when(pl.program_id(0) == 0)
        def _():
            loss_ref[...] = jnp.zeros_like(loss_ref)

        e = x_ref[...] - t_ref[...]
        dy_ref[...] = e * (1.0 / D)
        row = jnp.sum(e * e, axis=1, keepdims=True) * (1.0 / D)
        loss_ref[...] += 0.5 * jnp.sum(row, axis=0, keepdims=True)

    return pl.pallas_call(
        body, name="loss_head", grid=(S // tm,),
        in_specs=[_rb(tm, D), _rb(tm, D)], out_specs=[_rb(tm, D), _fs((1, LANES))],
        out_shape=[_sds((S, D), F32), _sds((1, LANES), F32)], compiler_params=_cp(1),
    )(xf, target)


def _ada_fwd(c_all, ada_w, ada_b_sh):
    L, D, n = ada_w.shape
    B = c_all.shape[0]

    def body(c_ref, w_ref, b_ref, o_ref):
        sc = _silu(c_ref[...])
        o_ref[0] = _mm(sc, w_ref[0]) + b_ref[0]

    return pl.pallas_call(
        body, name="ada_fwd", grid=(L,),
        in_specs=[_fs((B, D)), pl.BlockSpec((1, D, n), lambda l: (l, 0, 0)), pl.BlockSpec((1, 1, n), lambda l: (l, 0, 0))],
        out_specs=pl.BlockSpec((1, B, n), lambda l: (l, 0, 0)),
        out_shape=_sds((L, B, n), F32), compiler_params=_cp(1),
    )(c_all, ada_w, ada_b_sh)


def _ada_bwd(c_all_t, dmod_sh):
    D, B = c_all_t.shape
    L, _, n = dmod_sh.shape

    def body(c_ref, d_ref, o_ref):
        acc = jnp.zeros((D, n), F32)
        for b in range(B):
            acc = acc + _silu(c_ref[:, b:b + 1]) * d_ref[0, b:b + 1, :]
        o_ref[0] = acc

    return pl.pallas_call(
        body, name="ada_bwd", grid=(L,),
        in_specs=[_fs((D, B)), pl.BlockSpec((1, B, n), lambda l: (l, 0, 0))],
        out_specs=pl.BlockSpec((1, D, n), lambda l: (l, 0, 0)),
        out_shape=_sds((L, D, n), F32), compiler_params=_cp(1),
    )(c_all_t, dmod_sh)


def _sum_parts(name, parts):
    P, R, C = parts.shape
    tr = _pick(R, 512, 8)

    def body(p_ref, o_ref):
        acc = p_ref[0]
        for k in range(1, P):
            acc = acc + p_ref[k]
        o_ref[...] = acc

    return pl.pallas_call(
        body, name=name, grid=(R // tr,),
        in_specs=[pl.BlockSpec((P, tr, C), lambda i: (0, i, 0))], out_specs=_rb(tr, C),
        out_shape=_sds((R, C), F32), compiler_params=_cp(1),
    )(parts)


def _adamw(name, w, g, m, v):
    R, C = w.shape
    tr = _pick(R, 256, 8) if R % 8 == 0 else R
    c1 = 1.0 - ADAM_B1 ** ADAM_STEP
    c2 = 1.0 - ADAM_B2 ** ADAM_STEP

    def body(w_ref, g_ref, m_ref, v_ref, d_ref, mo_ref, vo_ref):
        gv = g_ref[...]
        mn = ADAM_B1 * m_ref[...] + (1.0 - ADAM_B1) * gv
        vn = ADAM_B2 * v_ref[...] + (1.0 - ADAM_B2) * (gv * gv)
        mo_ref[...] = mn
        vo_ref[...] = vn
        d_ref[...] = -ADAM_LR * ((mn / c1) / (jnp.sqrt(vn / c2) + ADAM_EPS) + ADAM_WD * w_ref[...])

    spec = _rb(tr, C)
    return pl.pallas_call(
        body, name=name, grid=(R // tr,),
        in_specs=[spec] * 4, out_specs=[spec] * 3, out_shape=[_sds((R, C), F32)] * 3, compiler_params=_cp(1),
    )(w, g, m, v)


def _ag_small(name, blk):
    R, C = blk.shape

    def body(x_ref, out_ref, send_sems, recv_sems, local_sem):
        x, y, c = lax.axis_index("x"), lax.axis_index("y"), lax.axis_index("c")
        me, sibling = (x, y, c), (x, y, 1 - c)
        chips = [(1 - x, y), (x, 1 - y), (1 - x, 1 - y)]

        def rows(px, py, pc):
            return out_ref.at[pl.ds((4 * px + 2 * py + pc) * R, R), :]

        def copy(k, block, to, src=None):
            return pltpu.make_async_remote_copy(
                src_ref=rows(*block) if src is None else src, dst_ref=rows(*block),
                send_sem=send_sems.at[k], recv_sem=recv_sems.at[k], device_id=to, device_id_type=MESH)

        mine = pltpu.make_async_copy(x_ref, rows(*me), local_sem)
        mine.start()
        first = [copy(0, me, sibling, src=x_ref)]
        first += [copy(1 + j, me, (*chip, c), src=x_ref) for j, chip in enumerate(chips)]
        for cp in first:
            cp.start()
        passed = [copy(4 + j, (*chip, c), sibling) for j, chip in enumerate(chips)]
        for j, chip in enumerate(chips):
            copy(1 + j, (*chip, c), me).wait_recv()
            passed[j].start()
        copy(0, sibling, me).wait_recv()
        for j, chip in enumerate(chips):
            copy(4 + j, (*chip, 1 - c), me).wait_recv()
        for cp in first + passed:
            cp.wait_send()
        mine.wait()

    return pl.pallas_call(
        body, name=name, out_shape=_sds((8 * R, C), blk.dtype),
        in_specs=[pl.BlockSpec(memory_space=pltpu.VMEM)], out_specs=pl.BlockSpec(memory_space=pltpu.VMEM),
        scratch_shapes=[pltpu.SemaphoreType.DMA((7,)), pltpu.SemaphoreType.DMA((7,)), pltpu.SemaphoreType.DMA],
    )(blk)


def _ag_weights(wflat):
    R, C = wflat.shape
    half = R // 2

    def body(w_ref, out_ref, send_sems, recv_sems, local_sem):
        x, y, c = lax.axis_index("x"), lax.axis_index("y"), lax.axis_index("c")
        sibling = (x, y, 1 - c)
        chips = [(1 - x, y), (x, 1 - y), (1 - x, 1 - y)]

        def blk(px, py, pc):
            return out_ref.at[2 * px + py, pl.ds(pc * half, half), :]

        def copy(k, block, to, src=None):
            return pltpu.make_async_remote_copy(
                src_ref=blk(*block) if src is None else src, dst_ref=blk(*block),
                send_sem=send_sems.at[k], recv_sem=recv_sems.at[k], device_id=to, device_id_type=MESH)

        mine = pltpu.make_async_copy(w_ref, out_ref.at[2 * x + y], local_sem)
        mine.start()
        first = [copy(j, (x, y, c), (*chip, c), src=w_ref.at[pl.ds(c * half, half), :]) for j, chip in enumerate(chips)]
        for cp in first:
            cp.start()
        passed = [copy(3 + j, (*chip, c), sibling) for j, chip in enumerate(chips)]
        for j, chip in enumerate(chips):
            copy(j, (*chip, c), (x, y, c)).wait_recv()
            passed[j].start()
        for j, chip in enumerate(chips):
            copy(3 + j, (*chip, 1 - c), (x, y, c)).wait_recv()
        for cp in first + passed:
            cp.wait_send()
        mine.wait()

    return pl.pallas_call(
        body, name="ag_weights", out_shape=_sds((4, R, C), wflat.dtype),
        in_specs=[pl.BlockSpec(memory_space=pl.ANY)], out_specs=pl.BlockSpec(memory_space=pl.ANY),
        scratch_shapes=[pltpu.SemaphoreType.DMA((6,)), pltpu.SemaphoreType.DMA((6,)), pltpu.SemaphoreType.DMA],
    )(wflat)


def _all_to_all(src):
    _, R, C = src.shape

    def body(s_ref, out_ref, send_sems, recv_sems, local_sem):
        x, y, c = lax.axis_index("x"), lax.axis_index("y"), lax.axis_index("c")
        me = 4 * x + 2 * y + c
        mine = pltpu.make_async_copy(s_ref.at[me], out_ref.at[me], local_sem)
        mine.start()
        copies = []
        for k in range(1, 8):
            px = 1 - x if k & 4 else x
            py = 1 - y if k & 2 else y
            pc = 1 - c if k & 1 else c
            peer = 4 * px + 2 * py + pc
            copies.append((peer, pltpu.make_async_remote_copy(
                src_ref=s_ref.at[peer], dst_ref=out_ref.at[me], send_sem=send_sems.at[k - 1],
                recv_sem=recv_sems.at[k - 1], device_id=(px, py, pc), device_id_type=MESH)))
        for _, cp in copies:
            cp.start()
        for k, (peer, _) in enumerate(copies):
            pltpu.make_async_remote_copy(
                src_ref=s_ref.at[peer], dst_ref=out_ref.at[peer], send_sem=send_sems.at[k],
                recv_sem=recv_sems.at[k], device_id=(x, y, c), device_id_type=MESH).wait_recv()
        for _, cp in copies:
            cp.wait_send()
        mine.wait()

    return pl.pallas_call(
        body, name="grad_all_to_all", out_shape=_sds(src.shape, src.dtype),
        in_specs=[pl.BlockSpec(memory_space=pl.ANY)], out_specs=pl.BlockSpec(memory_space=pl.ANY),
        scratch_shapes=[pltpu.SemaphoreType.DMA((7,)), pltpu.SemaphoreType.DMA((7,)), pltpu.SemaphoreType.DMA],
    )(src)


def _sibling_join(half_blk):
    R, C = half_blk.shape

    def body(h_ref, out_ref, send_sem, recv_sem, local_sem):
        x, y, c = lax.axis_index("x"), lax.axis_index("y"), lax.axis_index("c")
        mine = pltpu.make_async_copy(h_ref, out_ref.at[c], local_sem)
        mine.start()
        cp = pltpu.make_async_remote_copy(src_ref=h_ref, dst_ref=out_ref.at[c], send_sem=send_sem, recv_sem=recv_sem,
                                          device_id=(x, y, 1 - c), device_id_type=MESH)
        cp.start()
        pltpu.make_async_remote_copy(src_ref=h_ref, dst_ref=out_ref.at[1 - c], send_sem=send_sem, recv_sem=recv_sem,
                                     device_id=(x, y, c), device_id_type=MESH).wait_recv()
        cp.wait_send()
        mine.wait()

    return pl.pallas_call(
        body, name="sibling_join", out_shape=_sds((2, R, C), half_blk.dtype),
        in_specs=[pl.BlockSpec(memory_space=pl.ANY)], out_specs=pl.BlockSpec(memory_space=pl.ANY),
        scratch_shapes=[pltpu.SemaphoreType.DMA, pltpu.SemaphoreType.DMA, pltpu.SemaphoreType.DMA],
    )(half_blk)


def _layer_fwd(x, shift, scale, gate, lw):
    D = x.shape[1]
    h = _norm_mod(x, lw["norm_g"], scale, shift)
    p = _matmul("in_proj", h, lw["w_cat"], "nn", F32, tn_cap=896)
    qn, kn = _sb_prep(p, lw["gq_t"], lw["gk_t"])
    o_att, tot = _sb_fwd(qn, kn, p)
    qkv, bb, gcb, glb = _dn_prep(p, lw["conv_w"], lw["a_row"], lw["dtb_row"])
    o_dn, tinv, sall = _dn_fwd(qkv, bb, gcb, glb)
    o_sb, o_dnn = _gate(o_att, o_dn, p, lw["gn"])
    y, b_sb, b_dn = _branch(o_sb, o_dnn, lw["wb_sb"], lw["wb_dn"], p, D)
    x_next, out = _out_proj(x, y, lw["w_out"], gate)
    res = dict(x=x, h=h, p=p, qn=qn, kn=kn, o_att=o_att, tot=tot, qkv=qkv, bb=bb, gcb=gcb, glb=glb, o_dn=o_dn,
               tinv=tinv, sall=sall, o_sb=o_sb, o_dnn=o_dnn, y=y, b_sb=b_sb, b_dn=b_dn, out=out,
               shift=shift, scale=scale, gate=gate)
    return x_next, res


def _layer_bwd(dxn, res, lw):
    p = res["p"]
    dout, db_sb, db_dn, dm, dgate = _out_bwd(dxn, res["out"], res["gate"], lw["w_out"], p, res["b_sb"], res["b_dn"])
    dw_out = _matmul("dw_out", res["y"], dout, "tn", F32)
    dwb_sb = _matmul("dwb_sb", res["o_sb"], db_sb, "tn", F32)
    dwb_dn = _matmul("dwb_dn", res["o_dnn"], db_dn, "tn", F32)
    do_att, dz_sb, do_dn, dz_dn, dgn = _gate_bwd(db_sb, db_dn, lw["wb_sb"], lw["wb_dn"], res["o_att"], res["o_dn"],
                                                  p, lw["gn"])
    dqn, dkn, dv = _sb_bwd(res["qn"], res["kn"], p, do_att, res["tot"])
    dq_sb, dk_sb, dgq, dgk = _sb_prep_bwd(p, dqn, dkn, lw["gq_t"], lw["gk_t"])
    dqkv, dbb, dgb = _dn_bwd(res["qkv"], res["bb"], res["gcb"], res["glb"], res["tinv"], res["sall"], do_dn)
    dc, dp_ba, dal, ddt = _dn_prep_bwd_a(p, dqkv, dbb, dgb, lw["conv_w"], lw["a_row"], lw["dtb_row"])
    dp_dn, dconv = _dn_prep_bwd_b(p, dc, lw["conv_w"])
    dp = jnp.concatenate([dp_dn, dz_dn, dq_sb, dk_sb, dv.astype(_MXU_DTYPE), dz_sb, dm, dp_ba], axis=1)
    dh = _matmul("dh", dp, lw["w_cat"], "nt", F32, tk_cap=896)
    dw_cat = _matmul("dw_cat", res["h"], dp, "tn", F32, tm_cap=1024, tn_cap=896, tk_cap=512)
    dx, dshift, dscale, dnorm_g = _norm_mod_bwd(res["x"], dh, dxn, lw["norm_g"], res["scale"])
    small = dict(dmod=jnp.concatenate([dshift, dscale, dgate], axis=1)[0], norm_g=dnorm_g[0],
                 sb_q_g=dgq.reshape(SB_HEADS, SB_HD).sum(0), sb_k_g=dgk.reshape(SB_HEADS, SB_HD).sum(0),
                 conv_w=dconv, dn_a_log=dal[0, DN_HEADS:2 * DN_HEADS], dn_dt_bias=ddt[0, DN_HEADS:2 * DN_HEADS],
                 dn_norm_g=dgn[0])
    big = dict(w_cat=dw_cat, w_branch_sb=dwb_sb, w_branch_dn=dwb_dn, w_out=dw_out)
    return dx, small, big


def _cat_cols(w, D):
    return jnp.concatenate([w[:, 2048:4096], w[:, 0:2048], w[:, 4104:4104 + 2 * D], w[:, 4096:4104],
                            jnp.zeros((w.shape[0], LANES - 8), w.dtype)], axis=1)


def _uncat_cols(g, D):
    return jnp.concatenate([g[:, 2048:4096], g[:, 0:2048], g[:, 4096 + 2 * D:4096 + 2 * D + 8],
                            g[:, 4096:4096 + 2 * D]], axis=1)


def _flat_pack(arrs, mult):
    flat = jnp.concatenate([a.reshape(-1) for a in arrs])
    n = flat.shape[0]
    pad = (-n) % mult
    if pad:
        flat = jnp.concatenate([flat, jnp.zeros((pad,), flat.dtype)])
    return flat.reshape(-1, LANES)


def _flat_unpack(flat, shapes):
    flat = flat.reshape(-1)
    out, off = [], 0
    for s in shapes:
        n = math.prod(s)
        out.append(flat[off:off + n].reshape(s))
        off += n
    return out


BIG = ("w_in", "w_branch_sb", "w_branch_dn", "w_out")
SMALL = ("ada_b", "norm_g", "sb_q_g", "sb_k_g", "conv_w", "dn_a_log", "dn_dt_bias", "dn_norm_g")
_BIG_MULT = LANES * 32


def kernel(x, c, ada_w, ada_b, norm_g, w_in, sb_q_g, sb_k_g, conv_w, dn_a_log, dn_dt_bias, dn_norm_g, w_branch_sb, w_branch_dn, w_out, loss_target, m_ada_w, m_ada_b, m_norm_g, m_w_in, m_sb_q_g, m_sb_k_g, m_conv_w, m_dn_a_log, m_dn_dt_bias, m_dn_norm_g, m_w_branch_sb, m_w_branch_dn, m_w_out, v_ada_w, v_ada_b, v_norm_g, v_w_in, v_sb_q_g, v_sb_k_g, v_conv_w, v_dn_a_log, v_dn_dt_bias, v_dn_norm_g, v_w_branch_sb, v_w_branch_dn, v_w_out):
    W = dict(ada_w=ada_w, ada_b=ada_b, norm_g=norm_g, w_in=w_in, sb_q_g=sb_q_g, sb_k_g=sb_k_g, conv_w=conv_w,
             dn_a_log=dn_a_log, dn_dt_bias=dn_dt_bias, dn_norm_g=dn_norm_g, w_branch_sb=w_branch_sb,
             w_branch_dn=w_branch_dn, w_out=w_out)
    M = dict(ada_w=m_ada_w, ada_b=m_ada_b, norm_g=m_norm_g, w_in=m_w_in, sb_q_g=m_sb_q_g, sb_k_g=m_sb_k_g,
             conv_w=m_conv_w, dn_a_log=m_dn_a_log, dn_dt_bias=m_dn_dt_bias, dn_norm_g=m_dn_norm_g,
             w_branch_sb=m_w_branch_sb, w_branch_dn=m_w_branch_dn, w_out=m_w_out)
    V = dict(ada_w=v_ada_w, ada_b=v_ada_b, norm_g=v_norm_g, w_in=v_w_in, sb_q_g=v_sb_q_g, sb_k_g=v_sb_k_g,
             conv_w=v_conv_w, dn_a_log=v_dn_a_log, dn_dt_bias=v_dn_dt_bias, dn_norm_g=v_dn_norm_g,
             w_branch_sb=v_w_branch_sb, w_branch_dn=v_w_branch_dn, w_out=v_w_out)
    L = ada_w.shape[0]
    S, D = x.shape[1], x.shape[2]
    ix, iy, ic = lax.axis_index("x"), lax.axis_index("y"), lax.axis_index("c")
    shard = 2 * ix + iy
    me = 2 * shard + ic
    n_ada = ada_w.shape[2]
    n_in = w_in.shape[2]
    n_conv = conv_w.shape[2]
    n_br = w_branch_sb.shape[2]
    n_out = w_out.shape[1]

    big_shapes = [W[n].shape for n in BIG]
    wflat = _flat_pack([W[n].astype(_MXU_DTYPE) for n in BIG], _BIG_MULT)
    wall = _ag_weights(wflat)
    per_shard = [_flat_unpack(wall[s], big_shapes) for s in range(4)]
    w_in_f = jnp.concatenate([per_shard[s][0] for s in range(4)], axis=2)
    wb_sb_f = jnp.concatenate([per_shard[s][1] for s in range(4)], axis=2)
    wb_dn_f = jnp.concatenate([per_shard[s][2] for s in range(4)], axis=2)
    w_out_f = jnp.concatenate([per_shard[s][3] for s in range(4)], axis=1)

    g1 = _ag_small("ag_c_conv", _flat_pack([c, conv_w], LANES * 8))
    g1 = g1.reshape(8, -1)
    c_all = g1[:, :D]
    conv_parts = g1[:, D:D + L * CONV_K * n_conv].reshape(4, 2, L, CONV_K, n_conv)[:, 0]
    conv_full = jnp.concatenate([conv_parts[s] for s in range(4)], axis=2)
    ada_b_sh = lax.dynamic_slice_in_dim(ada_b, shard * n_ada, n_ada, axis=1)[:, None, :]
    mod_sh = _ada_fwd(c_all, ada_w, ada_b_sh)
    g2 = _ag_small("ag_mod", _flat_pack([mod_sh], LANES * 8)).reshape(8, -1)
    mod_parts = g2[:, :L * 8 * n_ada].reshape(4, 2, L, 8, n_ada)[:, 0]
    mod_all = jnp.concatenate([mod_parts[s] for s in range(4)], axis=2)
    mod = lax.dynamic_index_in_dim(mod_all, me, axis=1, keepdims=False)

    def layer_weights(l):
        pad_lo = jnp.zeros((DN_HEADS,), F32)
        pad_hi = jnp.zeros((LANES - 2 * DN_HEADS,), F32)
        return dict(
            norm_g=norm_g[l][None, :], w_cat=_cat_cols(w_in_f[l], D),
            gq_t=jnp.tile(sb_q_g[l], SB_HEADS)[None, :], gk_t=jnp.tile(sb_k_g[l], SB_HEADS)[None, :],
            conv_w=conv_full[l],
            a_row=jnp.concatenate([pad_lo, dn_a_log[l], pad_hi])[None, :],
            dtb_row=jnp.concatenate([pad_lo, dn_dt_bias[l], pad_hi])[None, :],
            gn=dn_norm_g[l][None, :], wb_sb=wb_sb_f[l], wb_dn=wb_dn_f[l], w_out=w_out_f[l])

    xs = x[0]
    lws, ress = [], []
    for l in range(L):
        lw = layer_weights(l)
        xs, res = _layer_fwd(xs, mod[l, None, 0:D], mod[l, None, D:2 * D], mod[l, None, 2 * D:3 * D], lw)
        lws.append(lw)
        ress.append(res)
    dxs, loss_row = _loss_head(xs, loss_target[0])
    loss = lax.psum(loss_row[0, 0], ("x", "y", "c"))
    smalls, bigs = [None] * L, [None] * L
    for l in reversed(range(L)):
        dxs, smalls[l], bigs[l] = _layer_bwd(dxs, ress[l], lws[l])
    grad_x = dxs[None]

    small_names = ("dmod",) + SMALL[1:]
    small_pack = _flat_pack([jnp.stack([smalls[l][n] for l in range(L)]) for n in small_names], LANES * 8)
    g3 = _ag_small("ag_small_grads", small_pack)
    R3 = small_pack.shape[0]
    g3 = g3.reshape(8, R3, LANES)
    small_sum = _sum_parts("sum_small", g3)
    small_shapes = [(L, 3 * D), (L, D), (L, SB_HD), (L, SB_HD), (L, CONV_K, 3 * DN_W), (L, DN_HEADS), (L, DN_HEADS),
                    (L, DN_HD)]
    sg = dict(zip(small_names, _flat_unpack(small_sum, small_shapes)))
    G = dict(ada_b=sg["dmod"], norm_g=sg["norm_g"], sb_q_g=sg["sb_q_g"], sb_k_g=sg["sb_k_g"],
             conv_w=lax.dynamic_slice_in_dim(sg["conv_w"], shard * n_conv, n_conv, axis=2),
             dn_a_log=sg["dn_a_log"], dn_dt_bias=sg["dn_dt_bias"], dn_norm_g=sg["dn_norm_g"])
    dmod_all = g3.reshape(8, -1)[:, :L * 3 * D].reshape(8, L, 3 * D)
    dmod_sh = lax.dynamic_slice_in_dim(dmod_all, shard * n_ada, n_ada, axis=2).transpose(1, 0, 2)
    G["ada_w"] = _ada_bwd(c_all.T, dmod_sh)

    gfull = dict(w_in=jnp.stack([_uncat_cols(bigs[l]["w_cat"], D) for l in range(L)]),
                 w_branch_sb=jnp.stack([bigs[l]["w_branch_sb"] for l in range(L)]),
                 w_branch_dn=jnp.stack([bigs[l]["w_branch_dn"] for l in range(L)]),
                 w_out=jnp.stack([bigs[l]["w_out"] for l in range(L)]))
    pieces = []
    for s in range(4):
        pieces.append(_flat_pack([gfull["w_in"][:, :, s * n_in:(s + 1) * n_in],
                                  gfull["w_branch_sb"][:, :, s * n_br:(s + 1) * n_br],
                                  gfull["w_branch_dn"][:, :, s * n_br:(s + 1) * n_br],
                                  gfull["w_out"][:, s * n_out:(s + 1) * n_out, :]], _BIG_MULT))
    Rw = pieces[0].shape[0]
    send = jnp.stack(pieces).reshape(8, Rw // 2, LANES)
    got = _all_to_all(send)
    ghalf = _sum_parts("sum_big", got)
    gshard = _sibling_join(ghalf).reshape(Rw, LANES)
    for n, g in zip(BIG, _flat_unpack(gshard, big_shapes)):
        G[n] = g

    delta, new_m, new_v = {}, {}, {}
    for n in ("ada_w",) + BIG:
        sh = W[n].shape
        two = (sh[0] * sh[1], sh[2])
        d, mo, vo = _adamw("adamw_" + n, W[n].reshape(two), G[n].reshape(two), M[n].reshape(two), V[n].reshape(two))
        delta[n], new_m[n], new_v[n] = d.reshape(sh), mo.reshape(sh), vo.reshape(sh)
    sm_shapes = [W[n].shape for n in SMALL]
    d, mo, vo = _adamw("adamw_small", _flat_pack([W[n] for n in SMALL], LANES * 8),
                       _flat_pack([G[n] for n in SMALL], LANES * 8), _flat_pack([M[n] for n in SMALL], LANES * 8),
                       _flat_pack([V[n] for n in SMALL], LANES * 8))
    for n, dd, mm, vv in zip(SMALL, _flat_unpack(d, sm_shapes), _flat_unpack(mo, sm_shapes),
                             _flat_unpack(vo, sm_shapes)):
        delta[n], new_m[n], new_v[n] = dd, mm, vv

    order = ("ada_w", "ada_b", "norm_g", "w_in", "sb_q_g", "sb_k_g", "conv_w", "dn_a_log", "dn_dt_bias", "dn_norm_g",
             "w_branch_sb", "w_branch_dn", "w_out")
    return (loss, grad_x, *[G[n] for n in order], *[delta[n] for n in order], *[new_m[n] for n in order],
            *[new_v[n] for n in order])
```

```python
import math

import jax
import jax.numpy as jnp
from jax import lax
from jax.experimental import pallas as pl
from jax.experimental.pallas import tpu as pltpu

F32 = jnp.float32
BF16 = jnp.bfloat16
_MXU_DTYPE = BF16
_VMEM_LIMIT = 48 * 1024 * 1024
LANES = 128

EPS = 1e-6
SB_HEADS, SB_HD, SB_W = 8, 64, 512
DN_HEADS, DN_HD, DN_W = 4, 128, 512
CONV_K = 4
CHUNK = 64
QB = 256
ADAM_LR, ADAM_B1, ADAM_B2, ADAM_EPS, ADAM_WD, ADAM_STEP = 0.001, 0.9, 0.999, 1e-08, 0.01, 10

C_DN_QKV, C_DN_Z, C_SB_Q, C_SB_K, C_SB_V, C_SB_Z, C_MG = 0, 1536, 2048, 2560, 3072, 3584, 4096

_NN = (((1,), (0,)), ((), ()))
_NT = (((1,), (1,)), ((), ()))
_TN = (((0,), (0,)), ((), ()))
MESH = pl.DeviceIdType.MESH


def _sds(shape, dtype):
    return jax.ShapeDtypeStruct(shape, dtype)


def _cp(n):
    return pltpu.CompilerParams(dimension_semantics=("arbitrary",) * n, vmem_limit_bytes=_VMEM_LIMIT)


def _rb(tm, w, cb=0):
    return pl.BlockSpec((tm, w), lambda i: (i, cb))


def _fs(shape):
    nd = len(shape)
    return pl.BlockSpec(shape, lambda i: (0,) * nd)


def _dg(a, b, dims):
    return lax.dot_general(a, b, dims, preferred_element_type=F32)


def _mm(a, b, dims=_NN):
    return _dg(a.astype(_MXU_DTYPE), b.astype(_MXU_DTYPE), dims)


def _split3(x):
    hi = x.astype(BF16)
    r = x - hi.astype(F32)
    mid = r.astype(BF16)
    lo = (r - mid.astype(F32)).astype(BF16)
    return hi, mid, lo


def _mm_xl(x, const, dims=_NN):
    cb = const.astype(BF16)
    hi, mid, lo = _split3(x)
    return _dg(hi, cb, dims) + _dg(mid, cb, dims) + _dg(lo, cb, dims)


def _mm_xl2(x, const, dims=_NN):
    cb = const.astype(BF16)
    hi = x.astype(BF16)
    lo = (x - hi.astype(F32)).astype(BF16)
    return _dg(hi, cb, dims) + _dg(lo, cb, dims)


def _mm_xr(const, x, dims=_NN):
    cb = const.astype(BF16)
    hi, mid, lo = _split3(x)
    return _dg(cb, hi, dims) + _dg(cb, mid, dims) + _dg(cb, lo, dims)


def _mm3(a, b, dims=_NN):
    ah, am, _ = _split3(a)
    bh, bm, _ = _split3(b)
    return _dg(ah, bh, dims) + (_dg(ah, bm, dims) + _dg(am, bh, dims))


def _sigmoid(z):
    return 1.0 / (1.0 + jnp.exp(-z))


def _silu(z):
    return z * _sigmoid(z)


def _dsilu(z):
    s = _sigmoid(z)
    return s * (1.0 + z * (1.0 - s))


def _softplus(z):
    return jnp.maximum(z, 0.0) + jnp.log(1.0 + jnp.exp(-jnp.abs(z)))


def _iota2(shape, dim):
    return lax.broadcasted_iota(jnp.int32, shape, dim)


def _pick(n, cap, mult):
    best = None
    for t in range(mult, min(n, cap) + 1, mult):
        if n % t == 0:
            best = t
    assert best is not None, (n, cap, mult)
    return best


def _matmul(name, a, b, form, out_dtype, tm_cap=512, tn_cap=1024, tk_cap=1024):
    if form == "nn":
        (M, K), (_, N) = a.shape, b.shape
    elif form == "nt":
        (M, K), (N, _) = a.shape, b.shape
    else:
        (K, M), (_, N) = a.shape, b.shape
    tm = _pick(M, tm_cap, 128 if form == "tn" else 8)
    tn = _pick(N, tn_cap, 128)
    tk = _pick(K, tk_cap, 128)
    nk = K // tk
    dims = {"nn": _NN, "nt": _NT, "tn": _TN}[form]
    if form == "nn":
        a_spec = pl.BlockSpec((tm, tk), lambda i, j, k: (i, k))
        b_spec = pl.BlockSpec((tk, tn), lambda i, j, k: (k, j))
    elif form == "nt":
        a_spec = pl.BlockSpec((tm, tk), lambda i, j, k: (i, k))
        b_spec = pl.BlockSpec((tn, tk), lambda i, j, k: (j, k))
    else:
        a_spec = pl.BlockSpec((tk, tm), lambda i, j, k: (k, i))
        b_spec = pl.BlockSpec((tk, tn), lambda i, j, k: (k, j))

    def body(a_ref, b_ref, o_ref, acc_ref):
        k = pl.program_id(2)

        @pl.when(k == 0)
        def _():
            acc_ref[...] = jnp.zeros_like(acc_ref)

        acc_ref[...] += _mm(a_ref[...], b_ref[...], dims)

        @pl.when(k == nk - 1)
        def _():
            o_ref[...] = acc_ref[...].astype(o_ref.dtype)

    return pl.pallas_call(
        body, name=name, grid=(M // tm, N // tn, nk),
        in_specs=[a_spec, b_spec],
        out_specs=pl.BlockSpec((tm, tn), lambda i, j, k: (i, j)),
        out_shape=_sds((M, N), out_dtype),
        scratch_shapes=[pltpu.VMEM((tm, tn), F32)],
        compiler_params=_cp(3),
    )(a, b)


def _norm_mod(x, g, scale, shift, tm=256):
    S, D = x.shape

    def body(x_ref, g_ref, sc_ref, sh_ref, h_ref):
        xv = x_ref[...]
        r = lax.rsqrt(jnp.mean(xv * xv, axis=1, keepdims=True) + EPS)
        h_ref[...] = ((xv * r * g_ref[...]) * (1.0 + sc_ref[...]) + sh_ref[...]).astype(h_ref.dtype)

    return pl.pallas_call(
        body, name="norm_mod", grid=(S // tm,),
        in_specs=[_rb(tm, D), _fs((1, D)), _fs((1, D)), _fs((1, D))],
        out_specs=_rb(tm, D), out_shape=_sds((S, D), _MXU_DTYPE), compiler_params=_cp(1),
    )(x, g, scale, shift)


def _norm_mod_bwd(x, dh, dxn, g, scale, tm=256):
    S, D = x.shape

    def body(x_ref, dh_ref, dxn_ref, g_ref, sc_ref, dx_ref, dsh_ref, dsc_ref, dg_ref):
        @pl.when(pl.program_id(0) == 0)
        def _():
            dsh_ref[...] = jnp.zeros_like(dsh_ref)
            dsc_ref[...] = jnp.zeros_like(dsc_ref)
            dg_ref[...] = jnp.zeros_like(dg_ref)

        xv, dhv, gv = x_ref[...], dh_ref[...], g_ref[...]
        r = lax.rsqrt(jnp.mean(xv * xv, axis=1, keepdims=True) + EPS)
        xh = xv * r
        one_sc = 1.0 + sc_ref[...]
        dsh_ref[...] += jnp.sum(dhv, axis=0, keepdims=True)
        dsc_ref[...] += jnp.sum(dhv * xh * gv, axis=0, keepdims=True)
        dg_ref[...] += jnp.sum(dhv * one_sc * xh, axis=0, keepdims=True)
        dxh = dhv * (gv * one_sc)
        dx_ref[...] = r * (dxh - xh * jnp.mean(dxh * xh, axis=1, keepdims=True)) + dxn_ref[...]

    return pl.pallas_call(
        body, name="norm_mod_bwd", grid=(S // tm,),
        in_specs=[_rb(tm, D), _rb(tm, D), _rb(tm, D), _fs((1, D)), _fs((1, D))],
        out_specs=[_rb(tm, D), _fs((1, D)), _fs((1, D)), _fs((1, D))],
        out_shape=[_sds((S, D), F32)] + [_sds((1, D), F32)] * 3, compiler_params=_cp(1),
    )(x, dh, dxn, g, scale)


def _head_sum_matrix():
    r = jnp.arange(SB_W)
    return (r[:, None] // SB_HD == r[None, :] // SB_HD).astype(BF16)


def _sb_prep(p, gq_t, gk_t, tm=256):
    S = p.shape[0]
    bd = _head_sum_matrix()

    def body(q_ref, k_ref, gq_ref, gk_ref, bd_ref, qn_ref, kn_ref):
        for src, g_ref, dst in ((q_ref, gq_ref, qn_ref), (k_ref, gk_ref, kn_ref)):
            v = src[...]
            ms = _mm_xl(v * v, bd_ref[...]) * (1.0 / SB_HD)
            dst[...] = (v * lax.rsqrt(ms + EPS) * g_ref[...]).astype(dst.dtype)

    return pl.pallas_call(
        body, name="sb_prep", grid=(S // tm,),
        in_specs=[_rb(tm, SB_W, C_SB_Q // SB_W), _rb(tm, SB_W, C_SB_K // SB_W),
                  _fs((1, SB_W)), _fs((1, SB_W)), _fs((SB_W, SB_W))],
        out_specs=[_rb(tm, SB_W), _rb(tm, SB_W)],
        out_shape=[_sds((S, SB_W), _MXU_DTYPE)] * 2, compiler_params=_cp(1),
    )(p, p, gq_t, gk_t, bd)


def _sb_prep_bwd(p, dqn, dkn, gq_t, gk_t, tm=256):
    S = p.shape[0]
    bd = _head_sum_matrix()

    def body(q_ref, k_ref, dqn_ref, dkn_ref, gq_ref, gk_ref, bd_ref, dq_ref, dk_ref, dgq_ref, dgk_ref):
        @pl.when(pl.program_id(0) == 0)
        def _():
            dgq_ref[...] = jnp.zeros_like(dgq_ref)
            dgk_ref[...] = jnp.zeros_like(dgk_ref)

        for src, dn_ref, g_ref, dst, dg_ref in ((q_ref, dqn_ref, gq_ref, dq_ref, dgq_ref),
                                                (k_ref, dkn_ref, gk_ref, dk_ref, dgk_ref)):
            v, dn = src[...], dn_ref[...]
            r = lax.rsqrt(_mm_xl(v * v, bd_ref[...]) * (1.0 / SB_HD) + EPS)
            vh = v * r
            dg_ref[...] += jnp.sum(dn * vh, axis=0, keepdims=True)
            dvh = dn * g_ref[...]
            m = _mm_xl(dvh * vh, bd_ref[...]) * (1.0 / SB_HD)
            dst[...] = (r * (dvh - vh * m)).astype(dst.dtype)

    return pl.pallas_call(
        body, name="sb_prep_bwd", grid=(S // tm,),
        in_specs=[_rb(tm, SB_W, C_SB_Q // SB_W), _rb(tm, SB_W, C_SB_K // SB_W), _rb(tm, SB_W), _rb(tm, SB_W),
                  _fs((1, SB_W)), _fs((1, SB_W)), _fs((SB_W, SB_W))],
        out_specs=[_rb(tm, SB_W), _rb(tm, SB_W), _fs((1, SB_W)), _fs((1, SB_W))],
        out_shape=[_sds((S, SB_W), _MXU_DTYPE)] * 2 + [_sds((1, SB_W), F32)] * 2, compiler_params=_cp(1),
    )(p, p, dqn, dkn, gq_t, gk_t, bd)


def _sb_consts():
    r, c = _iota2((QB, QB), 0), _iota2((QB, QB), 1)
    lane = _iota2((1, LANES), 1)
    return r, c, lane


def _sb_fwd(qn, kn, p):
    S = qn.shape[0]
    scale = 1.0 / math.sqrt(SB_HD)

    def body(q_ref, k_ref, v_ref, o_ref, tot_ref):
        i = pl.program_id(1)
        r, c, lane = _sb_consts()
        u_gt = (r > c).astype(BF16)
        strict = c < r
        q = q_ref[...]
        masks = [(lane // SB_HD) == h for h in range(2)]
        qhs = [jnp.where(m, q, jnp.zeros_like(q)) for m in masks]

        def block(off, carry, diagonal):
            kj = k_ref[pl.ds(off, QB), :]
            vj = v_ref[pl.ds(off, QB), :].astype(_MXU_DTYPE)
            out = []
            for h in range(2):
                o_h, run = carry[2 * h], carry[2 * h + 1]
                z = _mm(qhs[h], kj, _NT) * scale
                sp = _softplus(z)
                sp_m = jnp.where(strict, sp, 0.0) if diagonal else sp
                later = _mm_xl2(sp_m, u_gt)
                w = jnp.exp((z - sp) - later - run)
                if diagonal:
                    w = jnp.where(strict, w, 0.0)
                out += [o_h + _mm(w, vj), run + jnp.sum(sp_m, axis=1, keepdims=True)]
            return tuple(out)

        init = (jnp.zeros((QB, LANES), F32), jnp.zeros((QB, 1), F32)) * 2
        carry = block(pl.multiple_of(i * QB, QB), init, True)
        o0, run0, o1, run1 = lax.fori_loop(
            1, i + 1, lambda jj, cr: block(pl.multiple_of((i - jj) * QB, QB), cr, False), carry)
        o_ref[...] = jnp.where(masks[0], o0, o1)
        tot_ref[...] = jnp.where(masks[0], run0, run1)

    return pl.pallas_call(
        body, name="sb_fwd", grid=(SB_W // LANES, S // QB),
        in_specs=[pl.BlockSpec((QB, LANES), lambda hp, i: (i, hp)),
                  pl.BlockSpec((S, LANES), lambda hp, i: (0, hp)),
                  pl.BlockSpec((S, LANES), lambda hp, i: (0, C_SB_V // LANES + hp))],
        out_specs=[pl.BlockSpec((QB, LANES), lambda hp, i: (i, hp))] * 2,
        out_shape=[_sds((S, SB_W), F32)] * 2, compiler_params=_cp(2),
    )(qn, kn, p)


def _sb_bwd(qn, kn, p, do, tot):
    S = qn.shape[0]
    scale = 1.0 / math.sqrt(SB_HD)

    def body(q_ref, k_ref, v_ref, do_ref, tot_ref, dq_ref, dk_ref, dv_ref):
        i = pl.program_id(1)

        @pl.when(i == 0)
        def _():
            dk_ref[...] = jnp.zeros_like(dk_ref)
            dv_ref[...] = jnp.zeros_like(dv_ref)

        r, c, lane = _sb_consts()
        u_le = (r <= c).astype(BF16)
        u_lt = (r < c).astype(BF16)
        strict = c < r
        q = q_ref[...]
        do = do_ref[...].astype(_MXU_DTYPE)
        tot_pair = tot_ref[...]
        masks = [(lane // SB_HD) == h for h in range(2)]
        qhs = [jnp.where(m, q, jnp.zeros_like(q)) for m in masks]
        dohs = [jnp.where(m, do, jnp.zeros_like(do)) for m in masks]
        tots = [jnp.max(jnp.where(m, tot_pair, 0.0), axis=1, keepdims=True) for m in masks]

        def block(off, carry, diagonal):
            kj = k_ref[pl.ds(off, QB), :]
            vj = v_ref[pl.ds(off, QB), :].astype(_MXU_DTYPE)
            out, dk_blk, dv_blk = [], None, None
            for h in range(2):
                dq_h, pre_sp, pre_e = carry[3 * h:3 * h + 3]
                z = _mm(qhs[h], kj, _NT) * scale
                sp = _softplus(z)
                a = z - sp
                sp_m = jnp.where(strict, sp, 0.0) if diagonal else sp
                incl = _mm_xl2(sp_m, u_le)
                w = jnp.exp(a - ((tots[h] - pre_sp) - incl))
                if diagonal:
                    w = jnp.where(strict, w, 0.0)
                e = w * _mm(dohs[h], vj, _NT)
                db = pre_e + _mm_xl2(e, u_lt)
                dz = (e - jnp.exp(a) * (e + db)) * scale
                if diagonal:
                    dz = jnp.where(strict, dz, 0.0)
                dkh = jnp.where(masks[h], _mm(dz, q, _TN), 0.0)
                dvh = jnp.where(masks[h], _mm(w, do, _TN), 0.0)
                dk_blk = dkh if dk_blk is None else dk_blk + dkh
                dv_blk = dvh if dv_blk is None else dv_blk + dvh
                out += [dq_h + _mm(dz, kj), pre_sp + jnp.sum(sp_m, axis=1, keepdims=True),
                        pre_e + jnp.sum(e, axis=1, keepdims=True)]
            dk_ref[pl.ds(off, QB), :] += dk_blk
            dv_ref[pl.ds(off, QB), :] += dv_blk
            return tuple(out)

        zero_col = jnp.zeros((QB, 1), F32)
        init = (jnp.zeros((QB, LANES), F32), zero_col, zero_col) * 2
        carry = lax.fori_loop(0, i, lambda j, cr: block(pl.multiple_of(j * QB, QB), cr, False), init)
        carry = block(pl.multiple_of(i * QB, QB), carry, True)
        dq_ref[...] = jnp.where(masks[0], carry[0], carry[3])

    blk = pl.BlockSpec((QB, LANES), lambda hp, i: (i, hp))
    full = pl.BlockSpec((S, LANES), lambda hp, i: (0, hp))
    return pl.pallas_call(
        body, name="sb_bwd", grid=(SB_W // LANES, S // QB),
        in_specs=[blk, full, pl.BlockSpec((S, LANES), lambda hp, i: (0, C_SB_V // LANES + hp)), blk, blk],
        out_specs=[blk, full, full],
        out_shape=[_sds((S, SB_W), F32)] * 3, compiler_params=_cp(2),
    )(qn, kn, p, do, tot)


def _dn_prep(p, conv_w, a_row, dtb_row, tm=256):
    S = p.shape[0]
    W3 = 3 * DN_W
    nhalo = tm // 8

    def body(x_ref, halo_ref, w_ref, ba_ref, a_ref, dtb_ref, qkv_ref, bb_ref, gc_ref, gl_ref):
        i = pl.program_id(0)
        halo = jnp.where(i > 0, halo_ref[...], 0.0)
        xf = jnp.concatenate([halo, x_ref[...]], axis=0)
        acc = jnp.zeros((tm, W3), F32)
        for k in range(CONV_K):
            sh = CONV_K - 1 - k
            xs = xf if sh == 0 else pltpu.roll(xf, sh, 0)
            acc = acc + xs[8:, :] * w_ref[k:k + 1, :]
        s = _silu(acc)
        for gi in range(2 * DN_HEADS):
            sl = slice(gi * LANES, (gi + 1) * LANES)
            sg = s[:, sl]
            rinv = lax.rsqrt(jnp.sum(sg * sg, axis=1, keepdims=True) + EPS)
            qkv_ref[:, sl] = sg * rinv * (DN_HD ** -0.5 if gi < DN_HEADS else 1.0)
        qkv_ref[:, 2 * DN_W:] = s[:, 2 * DN_W:]

        ba = ba_ref[...]
        beta = _sigmoid(ba)
        g = -jnp.exp(a_ref[...]) * _softplus(ba + dtb_ref[...])
        lr, lc = _iota2((LANES, DN_W), 0), _iota2((LANES, DN_W), 1)
        sel_b = (lr == lc // LANES).astype(BF16)
        sel_g = (lr == lc // LANES + DN_HEADS).astype(BF16)
        bb_ref[...] = _mm_xl(beta, sel_b)
        graw = _mm_xl(g, sel_g)
        rr, cc = _iota2((tm, tm), 0), _iota2((tm, tm), 1)
        tri = jnp.logical_and(rr >= cc, rr // CHUNK == cc // CHUNK).astype(BF16)
        gc = _mm_xr(tri, graw)
        last = (cc == (rr // CHUNK) * CHUNK + (CHUNK - 1)).astype(BF16)
        gc_ref[...] = gc
        gl_ref[...] = _mm_xr(last, gc)

    return pl.pallas_call(
        body, name="dn_prep", grid=(S // tm,),
        in_specs=[_rb(tm, W3, 0), pl.BlockSpec((8, W3), lambda i: (jnp.maximum(i * nhalo - 1, 0), 0)),
                  _fs((CONV_K, W3)), _rb(tm, LANES, (p.shape[1] - LANES) // LANES),
                  _fs((1, LANES)), _fs((1, LANES))],
        out_specs=[_rb(tm, W3), _rb(tm, DN_W), _rb(tm, DN_W), _rb(tm, DN_W)],
        out_shape=[_sds((S, W3), F32)] + [_sds((S, DN_W), F32)] * 3, compiler_params=_cp(1),
    )(p, p, conv_w, p, a_row, dtb_row)


def _dn_chunk_terms(q, k, v, beta, gc, gl):
    r, c = _iota2((CHUNK, CHUNK), 0), _iota2((CHUNK, CHUNK), 1)
    tril, strict = r >= c, r > c
    gcol = _mm_xl(gc, jnp.full((LANES, CHUNK), 1.0 / LANES, F32))
    grow = _mm_xr(jnp.full((CHUNK, LANES), 1.0 / LANES, F32), gc, _NT)
    dec = jnp.where(tril, jnp.exp(jnp.where(tril, gcol - grow, 0.0)), 0.0)
    gam = jnp.exp(gc)
    dlt = jnp.exp(gl - gc)
    kb, vb = k * beta, v * beta
    pm = _mm(kb, k, _NT)
    qk = _mm(q, k, _NT)
    m = jnp.where(strict, pm * dec, 0.0)
    a = jnp.where(tril, qk * dec, 0.0)
    return dict(tril=tril, strict=strict, dec=dec, gam=gam, dlt=dlt, kb=kb, vb=vb, m=m, a=a)


def _dn_fwd(qkv, bb, gcb, glb):
    S = qkv.shape[0]
    N = S // CHUNK

    def body(qkv_ref, bb_ref, gc_ref, gl_ref, o_ref, t_ref, sall_ref, s_scr):
        @pl.when(pl.program_id(0) == 0)
        def _():
            s_scr[...] = jnp.zeros_like(s_scr)

        r, c = _iota2((CHUNK, CHUNK), 0), _iota2((CHUNK, CHUNK), 1)
        eye = (r == c).astype(F32)
        for h in range(DN_HEADS):
            sl = slice(h * LANES, (h + 1) * LANES)
            q, k = qkv_ref[:, sl], qkv_ref[:, DN_W + h * LANES:DN_W + (h + 1) * LANES]
            v = qkv_ref[:, 2 * DN_W + h * LANES:2 * DN_W + (h + 1) * LANES]
            beta, gc, gl = bb_ref[:, sl], gc_ref[:, sl], gl_ref[:, sl]
            s0 = s_scr[h]
            sall_ref[0, h] = s0.astype(sall_ref.dtype)
            s0 = s0.astype(sall_ref.dtype).astype(F32)
            t = _dn_chunk_terms(q, k, v, beta, gc, gl)
            pw = -t["m"]
            tinv = eye + pw
            for _ in range(5):
                pw = _mm3(pw, pw)
                tinv = tinv + _mm3(tinv, pw)
            t_ref[h] = tinv
            u = _mm3(tinv, t["vb"])
            w = _mm3(tinv, t["kb"] * t["gam"])
            vn = u - _mm(w, s0)
            o_ref[:, sl] = _mm(q * t["gam"], s0) + _mm(t["a"], vn)
            egl = jnp.exp(jnp.concatenate([gl, gl], axis=0))
            s_scr[h] = s_scr[h] * egl + _mm(k * t["dlt"], vn, _TN)

    return pl.pallas_call(
        body, name="dn_fwd", grid=(N,),
        in_specs=[_rb(CHUNK, 3 * DN_W), _rb(CHUNK, DN_W), _rb(CHUNK, DN_W), _rb(CHUNK, DN_W)],
        out_specs=[_rb(CHUNK, DN_W), pl.BlockSpec((DN_HEADS, CHUNK, CHUNK), lambda n: (0, n, 0)),
                   pl.BlockSpec((1, DN_HEADS, DN_HD, DN_HD), lambda n: (n, 0, 0, 0))],
        out_shape=[_sds((S, DN_W), F32), _sds((DN_HEADS, S, CHUNK), F32),
                   _sds((N, DN_HEADS, DN_HD, DN_HD), _MXU_DTYPE)],
        scratch_shapes=[pltpu.VMEM((DN_HEADS, DN_HD, DN_HD), F32)],
        compiler_params=_cp(1),
    )(qkv, bb, gcb, glb)


def _dn_bwd(qkv, bb, gcb, glb, tinv_all, sall, do):
    S = qkv.shape[0]
    N = S // CHUNK

    def body(qkv_ref, bb_ref, gc_ref, gl_ref, t_ref, sall_ref, do_ref, dqkv_ref, dbb_ref, dg_ref, ds_scr):
        @pl.when(pl.program_id(0) == 0)
        def _():
            ds_scr[...] = jnp.zeros_like(ds_scr)

        r, c = _iota2((CHUNK, CHUNK), 0), _iota2((CHUNK, CHUNK), 1)
        eye = (r == c).astype(F32)
        u_ge = (c >= r).astype(F32)
        last_row = _iota2((CHUNK, LANES), 0) == CHUNK - 1
        for h in range(DN_HEADS):
            sl = slice(h * LANES, (h + 1) * LANES)
            slk = slice(DN_W + h * LANES, DN_W + (h + 1) * LANES)
            slv = slice(2 * DN_W + h * LANES, 2 * DN_W + (h + 1) * LANES)
            q, k, v = qkv_ref[:, sl], qkv_ref[:, slk], qkv_ref[:, slv]
            beta, gc, gl = bb_ref[:, sl], gc_ref[:, sl], gl_ref[:, sl]
            tinv = t_ref[h]
            s0 = sall_ref[0, h].astype(F32)
            do = do_ref[:, sl]
            ds1 = ds_scr[h]
            t = _dn_chunk_terms(q, k, v, beta, gc, gl)
            gam, dlt, kb, vb, dec = t["gam"], t["dlt"], t["kb"], t["vb"], t["dec"]
            kbg = kb * gam
            u = _mm3(tinv, vb)
            w = _mm3(tinv, kbg)
            vn = u - _mm(w, s0)
            qg, kd = q * gam, k * dlt
            egl = jnp.exp(gl)
            egl2 = jnp.concatenate([egl, egl], axis=0)

            dvn = _mm(t["a"], do, _TN) + _mm(kd, ds1)
            da = jnp.where(t["tril"], _mm(do, vn, _NT), 0.0)
            dqg = _mm(do, s0, _NT)
            dkd = _mm(vn, ds1, _NT)
            dw = -_mm(dvn, s0, _NT)
            ds_scr[h] = _mm(qg, do, _TN) + egl2 * ds1 - _mm(w, dvn, _TN)
            tt = _mm_xr(eye, tinv, _NT)
            dvb = _mm3(tt, dvn)
            dkbg = _mm3(tt, dw)
            dm = -jnp.where(t["strict"], _mm(dvb, u, _NT) + _mm(dkbg, w, _NT), 0.0)
            dpm = dm * dec
            dqk = da * dec
            dkb = dkbg * gam + _mm(dpm, k)
            dk = dkd * dlt + _mm(dpm, kb, _TN) + _mm(dqk, q, _TN) + dkb * beta
            dq = dqg * gam + _mm(dqk, k)
            dqkv_ref[:, sl] = dq
            dqkv_ref[:, slk] = dk
            dqkv_ref[:, slv] = dvb * beta
            dbb_ref[:, sl] = jnp.broadcast_to(
                jnp.sum(dkb * k, axis=1, keepdims=True) + jnp.sum(dvb * v, axis=1, keepdims=True), (CHUNK, LANES))
            dgam = jnp.sum(dqg * q, axis=1, keepdims=True) + jnp.sum(dkbg * kb, axis=1, keepdims=True)
            ddlt = jnp.sum(dkd * k, axis=1, keepdims=True)
            xm = dm * t["m"] + da * t["a"]
            xt = _mm_xr(eye, xm, _NT)
            dgc = (dgam * gam - ddlt * dlt + jnp.sum(xm, axis=1, keepdims=True)
                   - jnp.sum(xt, axis=1, keepdims=True))
            dgl = jnp.sum(ddlt * dlt, axis=0, keepdims=True) + jnp.sum(
                jnp.sum(ds1 * s0, axis=1, keepdims=True), axis=0, keepdims=True) * jnp.max(egl, axis=0, keepdims=True)
            dgc = dgc + jnp.where(last_row, dgl, 0.0)
            dg_ref[:, sl] = _mm_xr(u_ge, dgc)

    rev = lambda w: pl.BlockSpec((CHUNK, w), lambda n: (N - 1 - n, 0))
    return pl.pallas_call(
        body, name="dn_bwd", grid=(N,),
        in_specs=[rev(3 * DN_W), rev(DN_W), rev(DN_W), rev(DN_W),
                  pl.BlockSpec((DN_HEADS, CHUNK, CHUNK), lambda n: (0, N - 1 - n, 0)),
                  pl.BlockSpec((1, DN_HEADS, DN_HD, DN_HD), lambda n: (N - 1 - n, 0, 0, 0)), rev(DN_W)],
        out_specs=[rev(3 * DN_W), rev(DN_W), rev(DN_W)],
        out_shape=[_sds((S, 3 * DN_W), F32), _sds((S, DN_W), F32), _sds((S, DN_W), F32)],
        scratch_shapes=[pltpu.VMEM((DN_HEADS, DN_HD, DN_HD), F32)],
        compiler_params=_cp(1),
    )(qkv, bb, gcb, glb, tinv_all, sall, do)


def _dn_prep_bwd_a(p, dqkv, dbb, dgb, conv_w, a_row, dtb_row, tm=256):
    S, PC = p.shape
    W3 = 3 * DN_W
    nhalo = tm // 8

    def body(x_ref, halo_ref, w_ref, ba_ref, a_ref, dtb_ref, dqkv_ref, dbb_ref, dgb_ref,
             dc_ref, dba_ref, dal_ref, ddt_ref):
        i = pl.program_id(0)

        @pl.when(i == 0)
        def _():
            dal_ref[...] = jnp.zeros_like(dal_ref)
            ddt_ref[...] = jnp.zeros_like(ddt_ref)

        halo = jnp.where(i > 0, halo_ref[...], 0.0)
        xf = jnp.concatenate([halo, x_ref[...]], axis=0)
        acc = jnp.zeros((tm, W3), F32)
        for k in range(CONV_K):
            sh = CONV_K - 1 - k
            xs = xf if sh == 0 else pltpu.roll(xf, sh, 0)
            acc = acc + xs[8:, :] * w_ref[k:k + 1, :]
        s = _silu(acc)
        ds_act = _dsilu(acc)
        for gi in range(2 * DN_HEADS):
            sl = slice(gi * LANES, (gi + 1) * LANES)
            sg = s[:, sl]
            rinv = lax.rsqrt(jnp.sum(sg * sg, axis=1, keepdims=True) + EPS)
            nh = sg * rinv
            dn = dqkv_ref[:, sl] * (DN_HD ** -0.5 if gi < DN_HEADS else 1.0)
            dsg = rinv * (dn - nh * jnp.sum(dn * nh, axis=1, keepdims=True))
            dc_ref[:, sl] = dsg * ds_act[:, sl]
        dc_ref[:, 2 * DN_W:] = dqkv_ref[:, 2 * DN_W:] * ds_act[:, 2 * DN_W:]

        ba = ba_ref[...]
        beta = _sigmoid(ba)
        ea = jnp.exp(a_ref[...])
        pre = ba + dtb_ref[...]
        g = -ea * _softplus(pre)
        lr, lc = _iota2((DN_W, LANES), 0), _iota2((DN_W, LANES), 1)
        pick_b = jnp.where(lc == lr // LANES, 1.0 / LANES, 0.0)
        pick_g = jnp.where(lc == lr // LANES + DN_HEADS, 1.0 / LANES, 0.0)
        dbeta = _mm_xl(dbb_ref[...], pick_b)
        dg = _mm_xl(dgb_ref[...], pick_g)
        lane = _iota2((1, LANES), 1)
        da = dg * (-ea) * _sigmoid(pre)
        dba_ref[...] = jnp.where(lane < DN_HEADS, dbeta * beta * (1.0 - beta),
                                 jnp.where(lane < 2 * DN_HEADS, da, 0.0)).astype(dba_ref.dtype)
        dal_ref[...] += jnp.sum(dg * g, axis=0, keepdims=True)
        ddt_ref[...] += jnp.sum(da, axis=0, keepdims=True)

    return pl.pallas_call(
        body, name="dn_prep_bwd_a", grid=(S // tm,),
        in_specs=[_rb(tm, W3, 0), pl.BlockSpec((8, W3), lambda i: (jnp.maximum(i * nhalo - 1, 0), 0)),
                  _fs((CONV_K, W3)), _rb(tm, LANES, (PC - LANES) // LANES), _fs((1, LANES)), _fs((1, LANES)),
                  _rb(tm, W3), _rb(tm, DN_W), _rb(tm, DN_W)],
        out_specs=[_rb(tm, W3), _rb(tm, LANES), _fs((1, LANES)), _fs((1, LANES))],
        out_shape=[_sds((S, W3), F32), _sds((S, LANES), _MXU_DTYPE), _sds((1, LANES), F32), _sds((1, LANES), F32)],
        compiler_params=_cp(1),
    )(p, p, conv_w, p, a_row, dtb_row, dqkv, dbb, dgb)


def _dn_prep_bwd_b(p, dc, conv_w, tm=256):
    S = p.shape[0]
    W3 = 3 * DN_W
    nhalo = tm // 8
    nblk = S // tm

    def body(x_ref, xh_ref, dc_ref, dch_ref, w_ref, dx_ref, dw_ref):
        i = pl.program_id(0)

        @pl.when(i == 0)
        def _():
            dw_ref[...] = jnp.zeros_like(dw_ref)

        dcv = dc_ref[...]
        xf = jnp.concatenate([jnp.where(i > 0, xh_ref[...], 0.0), x_ref[...]], axis=0)
        df = jnp.concatenate([dcv, jnp.where(i < nblk - 1, dch_ref[...], 0.0)], axis=0)
        acc = jnp.zeros((tm, W3), F32)
        for k in range(CONV_K):
            sh = CONV_K - 1 - k
            xs = xf if sh == 0 else pltpu.roll(xf, sh, 0)
            dw_ref[k:k + 1, :] += jnp.sum(dcv * xs[8:, :], axis=0, keepdims=True)
            ds = df if sh == 0 else pltpu.roll(df, tm + 8 - sh, 0)
            acc = acc + ds[:tm, :] * w_ref[k:k + 1, :]
        dx_ref[...] = acc.astype(dx_ref.dtype)

    return pl.pallas_call(
        body, name="dn_prep_bwd_b", grid=(nblk,),
        in_specs=[_rb(tm, W3, 0), pl.BlockSpec((8, W3), lambda i: (jnp.maximum(i * nhalo - 1, 0), 0)),
                  _rb(tm, W3), pl.BlockSpec((8, W3), lambda i: (jnp.minimum((i + 1) * nhalo, S // 8 - 1), 0)),
                  _fs((CONV_K, W3))],
        out_specs=[_rb(tm, W3), _fs((CONV_K, W3))],
        out_shape=[_sds((S, W3), _MXU_DTYPE), _sds((CONV_K, W3), F32)], compiler_params=_cp(1),
    )(p, p, dc, dc, conv_w)


def _gate(o_att, o_dn, p, gn, tm=256):
    S = p.shape[0]

    def body(oa_ref, zs_ref, od_ref, zd_ref, gn_ref, osb_ref, odn_ref):
        osb_ref[...] = (oa_ref[...] * _silu(zs_ref[...])).astype(osb_ref.dtype)
        for h in range(DN_HEADS):
            sl = slice(h * LANES, (h + 1) * LANES)
            o = od_ref[:, sl]
            r = lax.rsqrt(jnp.mean(o * o, axis=1, keepdims=True) + EPS)
            odn_ref[:, sl] = (o * r * gn_ref[...] * _silu(zd_ref[:, sl])).astype(odn_ref.dtype)

    return pl.pallas_call(
        body, name="gate", grid=(S // tm,),
        in_specs=[_rb(tm, SB_W), _rb(tm, SB_W, C_SB_Z // SB_W), _rb(tm, DN_W), _rb(tm, DN_W, C_DN_Z // DN_W),
                  _fs((1, LANES))],
        out_specs=[_rb(tm, SB_W), _rb(tm, DN_W)],
        out_shape=[_sds((S, SB_W), _MXU_DTYPE), _sds((S, DN_W), _MXU_DTYPE)], compiler_params=_cp(1),
    )(o_att, p, o_dn, p, gn)


def _gate_bwd(db_sb, db_dn, wb_sb, wb_dn, o_att, o_dn, p, gn, tm=256):
    S = p.shape[0]
    D = db_sb.shape[1]

    def body(dbs_ref, dbd_ref, ws_ref, wd_ref, oa_ref, zs_ref, od_ref, zd_ref, gn_ref,
             doa_ref, dzs_ref, dod_ref, dzd_ref, dgn_ref):
        @pl.when(pl.program_id(0) == 0)
        def _():
            dgn_ref[...] = jnp.zeros_like(dgn_ref)

        do_sb = _mm(dbs_ref[...], ws_ref[...], _NT)
        zs = zs_ref[...]
        doa_ref[...] = do_sb * _silu(zs)
        dzs_ref[...] = (do_sb * oa_ref[...] * _dsilu(zs)).astype(dzs_ref.dtype)
        do_dnn = _mm(dbd_ref[...], wd_ref[...], _NT)
        gnv = gn_ref[...]
        for h in range(DN_HEADS):
            sl = slice(h * LANES, (h + 1) * LANES)
            o, z, dout = od_ref[:, sl], zd_ref[:, sl], do_dnn[:, sl]
            r = lax.rsqrt(jnp.mean(o * o, axis=1, keepdims=True) + EPS)
            oh = o * r
            sz = _silu(z)
            dzd_ref[:, sl] = (dout * oh * gnv * _dsilu(z)).astype(dzd_ref.dtype)
            dgn_ref[...] += jnp.sum(dout * sz * oh, axis=0, keepdims=True)
            doh = dout * gnv * sz
            dod_ref[:, sl] = r * (doh - oh * jnp.mean(doh * oh, axis=1, keepdims=True))

    return pl.pallas_call(
        body, name="gate_bwd", grid=(S // tm,),
        in_specs=[_rb(tm, D), _rb(tm, D), _fs((SB_W, D)), _fs((DN_W, D)), _rb(tm, SB_W),
                  _rb(tm, SB_W, C_SB_Z // SB_W), _rb(tm, DN_W), _rb(tm, DN_W, C_DN_Z // DN_W), _fs((1, LANES))],
        out_specs=[_rb(tm, SB_W), _rb(tm, SB_W), _rb(tm, DN_W), _rb(tm, DN_W), _fs((1, LANES))],
        out_shape=[_sds((S, SB_W), F32), _sds((S, SB_W), _MXU_DTYPE), _sds((S, DN_W), F32),
                   _sds((S, DN_W), _MXU_DTYPE), _sds((1, LANES), F32)],
        compiler_params=_cp(1),
    )(db_sb, db_dn, wb_sb, wb_dn, o_att, p, o_dn, p, gn)


def _branch(o_sb, o_dnn, wb_sb, wb_dn, p, D, tm=256):
    S = p.shape[0]

    def body(os_ref, od_ref, ws_ref, wd_ref, ms_ref, md_ref, y_ref, bs_ref, bd_ref):
        bs = _mm(os_ref[...], ws_ref[...])
        bdn = _mm(od_ref[...], wd_ref[...])
        bs_ref[...] = bs
        bd_ref[...] = bdn
        y_ref[...] = (_sigmoid(ms_ref[...]) * bs + _sigmoid(md_ref[...]) * bdn).astype(y_ref.dtype)

    return pl.pallas_call(
        body, name="branch", grid=(S // tm,),
        in_specs=[_rb(tm, SB_W), _rb(tm, DN_W), _fs((SB_W, D)), _fs((DN_W, D)),
                  _rb(tm, D, C_MG // D), _rb(tm, D, C_MG // D + 1)],
        out_specs=[_rb(tm, D), _rb(tm, D), _rb(tm, D)],
        out_shape=[_sds((S, D), _MXU_DTYPE), _sds((S, D), F32), _sds((S, D), F32)], compiler_params=_cp(1),
    )(o_sb, o_dnn, wb_sb, wb_dn, p, p)


def _out_proj(x, y, w_out, gate, tm=256):
    S, D = x.shape

    def body(x_ref, y_ref, w_ref, g_ref, xn_ref, out_ref):
        out = _mm(y_ref[...], w_ref[...])
        out_ref[...] = out
        xn_ref[...] = x_ref[...] + g_ref[...] * out

    return pl.pallas_call(
        body, name="out_proj", grid=(S // tm,),
        in_specs=[_rb(tm, D), _rb(tm, D), _fs((D, D)), _fs((1, D))],
        out_specs=[_rb(tm, D), _rb(tm, D)],
        out_shape=[_sds((S, D), F32), _sds((S, D), F32)], compiler_params=_cp(1),
    )(x, y, w_out, gate)


def _out_bwd(dxn, out, gate, w_out, p, b_sb, b_dn, tm=256):
    S, D = dxn.shape

    def body(dxn_ref, out_ref, g_ref, w_ref, ms_ref, md_ref, bs_ref, bd_ref,
             dout_ref, dbs_ref, dbd_ref, dm_ref, dgate_ref):
        @pl.when(pl.program_id(0) == 0)
        def _():
            dgate_ref[...] = jnp.zeros_like(dgate_ref)

        dxv = dxn_ref[...]
        dgate_ref[...] += jnp.sum(dxv * out_ref[...], axis=0, keepdims=True)
        dout = (g_ref[...] * dxv).astype(dout_ref.dtype)
        dout_ref[...] = dout
        dy = _mm(dout, w_ref[...], _NT)
        s1, s2 = _sigmoid(ms_ref[...]), _sigmoid(md_ref[...])
        dbs_ref[...] = (dy * s1).astype(dbs_ref.dtype)
        dbd_ref[...] = (dy * s2).astype(dbd_ref.dtype)
        dm_ref[:, :D] = (dy * bs_ref[...] * s1 * (1.0 - s1)).astype(dm_ref.dtype)
        dm_ref[:, D:] = (dy * bd_ref[...] * s2 * (1.0 - s2)).astype(dm_ref.dtype)

    return pl.pallas_call(
        body, name="out_bwd", grid=(S // tm,),
        in_specs=[_rb(tm, D), _rb(tm, D), _fs((1, D)), _fs((D, D)), _rb(tm, D, C_MG // D),
                  _rb(tm, D, C_MG // D + 1), _rb(tm, D), _rb(tm, D)],
        out_specs=[_rb(tm, D), _rb(tm, D), _rb(tm, D), _rb(tm, 2 * D), _fs((1, D))],
        out_shape=[_sds((S, D), _MXU_DTYPE)] * 3 + [_sds((S, 2 * D), _MXU_DTYPE), _sds((1, D), F32)],
        compiler_params=_cp(1),
    )(dxn, out, gate, w_out, p, p, b_sb, b_dn)


def _loss_head(xf, target, tm=256):
    S, D = xf.shape

    def body(x_ref, t_ref, dy_ref, loss_ref):
        @pl.when(pl.program_id(0) == 0)
        def _():
            loss_ref[...] = jnp.zeros_like(loss_ref)

        e = x_ref[...] - t_ref[...]
        dy_ref[...] = e * (1.0 / D)
        row = jnp.sum(e * e, axis=1, keepdims=True) * (1.0 / D)
        loss_ref[...] += 0.5 * jnp.sum(row, axis=0, keepdims=True)

    return pl.pallas_call(
        body, name="loss_head", grid=(S // tm,),
        in_specs=[_rb(tm, D), _rb(tm, D)], out_specs=[_rb(tm, D), _fs((1, LANES))],
        out_shape=[_sds((S, D), F32), _sds((1, LANES), F32)], compiler_params=_cp(1),
    )(xf, target)


def _ada_fwd(c_all, ada_w, ada_b_sh):
    L, D, n = ada_w.shape
    B = c_all.shape[0]

    def body(c_ref, w_ref, b_ref, o_ref):
        sc = _silu(c_ref[...])
        o_ref[0] = _mm(sc, w_ref[0]) + b_ref[0]

    return pl.pallas_call(
        body, name="ada_fwd", grid=(L,),
        in_specs=[_fs((B, D)), pl.BlockSpec((1, D, n), lambda l: (l, 0, 0)), pl.BlockSpec((1, 1, n), lambda l: (l, 0, 0))],
        out_specs=pl.BlockSpec((1, B, n), lambda l: (l, 0, 0)),
        out_shape=_sds((L, B, n), F32), compiler_params=_cp(1),
    )(c_all, ada_w, ada_b_sh)


def _ada_bwd(c_all_t, dmod_sh):
    D, B = c_all_t.shape
    L, _, n = dmod_sh.shape

    def body(c_ref, d_ref, o_ref):
        acc = jnp.zeros((D, n), F32)
        for b in range(B):
            acc = acc + _silu(c_ref[:, b:b + 1]) * d_ref[0, b:b + 1, :]
        o_ref[0] = acc

    return pl.pallas_call(
        body, name="ada_bwd", grid=(L,),
        in_specs=[_fs((D, B)), pl.BlockSpec((1, B, n), lambda l: (l, 0, 0))],
        out_specs=pl.BlockSpec((1, D, n), lambda l: (l, 0, 0)),
        out_shape=_sds((L, D, n), F32), compiler_params=_cp(1),
    )(c_all_t, dmod_sh)


def _sum_parts(name, parts):
    P, R, C = parts.shape
    tr = _pick(R, 512, 8)

    def body(p_ref, o_ref):
        acc = p_ref[0]
        for k in range(1, P):
            acc = acc + p_ref[k]
        o_ref[...] = acc

    return pl.pallas_call(
        body, name=name, grid=(R // tr,),
        in_specs=[pl.BlockSpec((P, tr, C), lambda i: (0, i, 0))], out_specs=_rb(tr, C),
        out_shape=_sds((R, C), F32), compiler_params=_cp(1),
    )(parts)


def _adamw(name, w, g, m, v):
    R, C = w.shape
    tr = _pick(R, 256, 8) if R % 8 == 0 else R
    c1 = 1.0 - ADAM_B1 ** ADAM_STEP
    c2 = 1.0 - ADAM_B2 ** ADAM_STEP

    def body(w_ref, g_ref, m_ref, v_ref, d_ref, mo_ref, vo_ref):
        gv = g_ref[...]
        mn = ADAM_B1 * m_ref[...] + (1.0 - ADAM_B1) * gv
        vn = ADAM_B2 * v_ref[...] + (1.0 - ADAM_B2) * (gv * gv)
        mo_ref[...] = mn
        vo_ref[...] = vn
        d_ref[...] = -ADAM_LR * ((mn / c1) / (jnp.sqrt(vn / c2) + ADAM_EPS) + ADAM_WD * w_ref[...])

    spec = _rb(tr, C)
    return pl.pallas_call(
        body, name=name, grid=(R // tr,),
        in_specs=[spec] * 4, out_specs=[spec] * 3, out_shape=[_sds((R, C), F32)] * 3, compiler_params=_cp(1),
    )(w, g, m, v)


def _ag_small(name, blk):
    R, C = blk.shape

    def body(x_ref, out_ref, send_sems, recv_sems, local_sem):
        x, y, c = lax.axis_index("x"), lax.axis_index("y"), lax.axis_index("c")
        me, sibling = (x, y, c), (x, y, 1 - c)
        chips = [(1 - x, y), (x, 1 - y), (1 - x, 1 - y)]

        def rows(px, py, pc):
            return out_ref.at[pl.ds((4 * px + 2 * py + pc) * R, R), :]

        def copy(k, block, to, src=None):
            return pltpu.make_async_remote_copy(
                src_ref=rows(*block) if src is None else src, dst_ref=rows(*block),
                send_sem=send_sems.at[k], recv_sem=recv_sems.at[k], device_id=to, device_id_type=MESH)

        mine = pltpu.make_async_copy(x_ref, rows(*me), local_sem)
        mine.start()
        first = [copy(0, me, sibling, src=x_ref)]
        first += [copy(1 + j, me, (*chip, c), src=x_ref) for j, chip in enumerate(chips)]
        for cp in first:
            cp.start()
        passed = [copy(4 + j, (*chip, c), sibling) for j, chip in enumerate(chips)]
        for j, chip in enumerate(chips):
            copy(1 + j, (*chip, c), me).wait_recv()
            passed[j].start()
        copy(0, sibling, me).wait_recv()
        for j, chip in enumerate(chips):
            copy(4 + j, (*chip, 1 - c), me).wait_recv()
        for cp in first + passed:
            cp.wait_send()
        mine.wait()

    return pl.pallas_call(
        body, name=name, out_shape=_sds((8 * R, C), blk.dtype),
        in_specs=[pl.BlockSpec(memory_space=pltpu.VMEM)], out_specs=pl.BlockSpec(memory_space=pltpu.VMEM),
        scratch_shapes=[pltpu.SemaphoreType.DMA((7,)), pltpu.SemaphoreType.DMA((7,)), pltpu.SemaphoreType.DMA],
    )(blk)


def _ag_weights(wflat):
    R, C = wflat.shape
    half = R // 2

    def body(w_ref, out_ref, send_sems, recv_sems, local_sem):
        x, y, c = lax.axis_index("x"), lax.axis_index("y"), lax.axis_index("c")
        sibling = (x, y, 1 - c)
        chips = [(1 - x, y), (x, 1 - y), (1 - x, 1 - y)]

        def blk(px, py, pc):
            return out_ref.at[2 * px + py, pl.ds(pc * half, half), :]

        def copy(k, block, to, src=None):
            return pltpu.make_async_remote_copy(
                src_ref=blk(*block) if src is None else src, dst_ref=blk(*block),
                send_sem=send_sems.at[k], recv_sem=recv_sems.at[k], device_id=to, device_id_type=MESH)

        mine = pltpu.make_async_copy(w_ref, out_ref.at[2 * x + y], local_sem)
        mine.start()
        first = [copy(j, (x, y, c), (*chip, c), src=w_ref.at[pl.ds(c * half, half), :]) for j, chip in enumerate(chips)]
        for cp in first:
            cp.start()
        passed = [copy(3 + j, (*chip, c), sibling) for j, chip in enumerate(chips)]
        for j, chip in enumerate(chips):
            copy(j, (*chip, c), (x, y, c)).wait_recv()
            passed[j].start()
        for j, chip in enumerate(chips):
            copy(3 + j, (*chip, 1 - c), (x, y, c)).wait_recv()
        for cp in first + passed:
            cp.wait_send()
        mine.wait()

    return pl.pallas_call(
        body, name="ag_weights", out_shape=_sds((4, R, C), wflat.dtype),
        in_specs=[pl.BlockSpec(memory_space=pl.ANY)], out_specs=pl.BlockSpec(memory_space=pl.ANY),
        scratch_shapes=[pltpu.SemaphoreType.DMA((6,)), pltpu.SemaphoreType.DMA((6,)), pltpu.SemaphoreType.DMA],
    )(wflat)


def _all_to_all(src):
    _, R, C = src.shape

    def body(s_ref, out_ref, send_sems, recv_sems, local_sem):
        x, y, c = lax.axis_index("x"), lax.axis_index("y"), lax.axis_index("c")
        me = 4 * x + 2 * y + c
        mine = pltpu.make_async_copy(s_ref.at[me], out_ref.at[me], local_sem)
        mine.start()
        copies = []
        for k in range(1, 8):
            px = 1 - x if k & 4 else x
            py = 1 - y if k & 2 else y
            pc = 1 - c if k & 1 else c
            peer = 4 * px + 2 * py + pc
            copies.append((peer, pltpu.make_async_remote_copy(
                src_ref=s_ref.at[peer], dst_ref=out_ref.at[me], send_sem=send_sems.at[k - 1],
                recv_sem=recv_sems.at[k - 1], device_id=(px, py, pc), device_id_type=MESH)))
        for _, cp in copies:
            cp.start()
        for k, (peer, _) in enumerate(copies):
            pltpu.make_async_remote_copy(
                src_ref=s_ref.at[peer], dst_ref=out_ref.at[peer], send_sem=send_sems.at[k],
                recv_sem=recv_sems.at[k], device_id=(x, y, c), device_id_type=MESH).wait_recv()
        for _, cp in copies:
            cp.wait_send()
        mine.wait()

    return pl.pallas_call(
        body, name="grad_all_to_all", out_shape=_sds(src.shape, src.dtype),
        in_specs=[pl.BlockSpec(memory_space=pl.ANY)], out_specs=pl.BlockSpec(memory_space=pl.ANY),
        scratch_shapes=[pltpu.SemaphoreType.DMA((7,)), pltpu.SemaphoreType.DMA((7,)), pltpu.SemaphoreType.DMA],
    )(src)


def _sibling_join(half_blk):
    R, C = half_blk.shape

    def body(h_ref, out_ref, send_sem, recv_sem, local_sem):
        x, y, c = lax.axis_index("x"), lax.axis_index("y"), lax.axis_index("c")
        mine = pltpu.make_async_copy(h_ref, out_ref.at[c], local_sem)
        mine.start()
        cp = pltpu.make_async_remote_copy(src_ref=h_ref, dst_ref=out_ref.at[c], send_sem=send_sem, recv_sem=recv_sem,
                                          device_id=(x, y, 1 - c), device_id_type=MESH)
        cp.start()
        pltpu.make_async_remote_copy(src_ref=h_ref, dst_ref=out_ref.at[1 - c], send_sem=send_sem, recv_sem=recv_sem,
                                     device_id=(x, y, c), device_id_type=MESH).wait_recv()
        cp.wait_send()
        mine.wait()

    return pl.pallas_call(
        body, name="sibling_join", out_shape=_sds((2, R, C), half_blk.dtype),
        in_specs=[pl.BlockSpec(memory_space=pl.ANY)], out_specs=pl.BlockSpec(memory_space=pl.ANY),
        scratch_shapes=[pltpu.SemaphoreType.DMA, pltpu.SemaphoreType.DMA, pltpu.SemaphoreType.DMA],
    )(half_blk)


def _layer_fwd(x, shift, scale, gate, lw):
    D = x.shape[1]
    h = _norm_mod(x, lw["norm_g"], scale, shift)
    p = _matmul("in_proj", h, lw["w_cat"], "nn", F32, tn_cap=896)
    qn, kn = _sb_prep(p, lw["gq_t"], lw["gk_t"])
    o_att, tot = _sb_fwd(qn, kn, p)
    qkv, bb, gcb, glb = _dn_prep(p, lw["conv_w"], lw["a_row"], lw["dtb_row"])
    o_dn, tinv, sall = _dn_fwd(qkv, bb, gcb, glb)
    o_sb, o_dnn = _gate(o_att, o_dn, p, lw["gn"])
    y, b_sb, b_dn = _branch(o_sb, o_dnn, lw["wb_sb"], lw["wb_dn"], p, D)
    x_next, out = _out_proj(x, y, lw["w_out"], gate)
    res = dict(x=x, h=h, p=p, qn=qn, kn=kn, o_att=o_att, tot=tot, qkv=qkv, bb=bb, gcb=gcb, glb=glb, o_dn=o_dn,
               tinv=tinv, sall=sall, o_sb=o_sb, o_dnn=o_dnn, y=y, b_sb=b_sb, b_dn=b_dn, out=out,
               shift=shift, scale=scale, gate=gate)
    return x_next, res


def _layer_bwd(dxn, res, lw):
    p = res["p"]
    dout, db_sb, db_dn, dm, dgate = _out_bwd(dxn, res["out"], res["gate"], lw["w_out"], p, res["b_sb"], res["b_dn"])
    dw_out = _matmul("dw_out", res["y"], dout, "tn", F32)
    dwb_sb = _matmul("dwb_sb", res["o_sb"], db_sb, "tn", F32)
    dwb_dn = _matmul("dwb_dn", res["o_dnn"], db_dn, "tn", F32)
    do_att, dz_sb, do_dn, dz_dn, dgn = _gate_bwd(db_sb, db_dn, lw["wb_sb"], lw["wb_dn"], res["o_att"], res["o_dn"],
                                                  p, lw["gn"])
    dqn, dkn, dv = _sb_bwd(res["qn"], res["kn"], p, do_att, res["tot"])
    dq_sb, dk_sb, dgq, dgk = _sb_prep_bwd(p, dqn, dkn, lw["gq_t"], lw["gk_t"])
    dqkv, dbb, dgb = _dn_bwd(res["qkv"], res["bb"], res["gcb"], res["glb"], res["tinv"], res["sall"], do_dn)
    dc, dp_ba, dal, ddt = _dn_prep_bwd_a(p, dqkv, dbb, dgb, lw["conv_w"], lw["a_row"], lw["dtb_row"])
    dp_dn, dconv = _dn_prep_bwd_b(p, dc, lw["conv_w"])
    dp = jnp.concatenate([dp_dn, dz_dn, dq_sb, dk_sb, dv.astype(_MXU_DTYPE), dz_sb, dm, dp_ba], axis=1)
    dh = _matmul("dh", dp, lw["w_cat"], "nt", F32, tk_cap=896)
    dw_cat = _matmul("dw_cat", res["h"], dp, "tn", F32, tm_cap=1024, tn_cap=896, tk_cap=512)
    dx, dshift, dscale, dnorm_g = _norm_mod_bwd(res["x"], dh, dxn, lw["norm_g"], res["scale"])
    small = dict(dmod=jnp.concatenate([dshift, dscale, dgate], axis=1)[0], norm_g=dnorm_g[0],
                 sb_q_g=dgq.reshape(SB_HEADS, SB_HD).sum(0), sb_k_g=dgk.reshape(SB_HEADS, SB_HD).sum(0),
                 conv_w=dconv, dn_a_log=dal[0, DN_HEADS:2 * DN_HEADS], dn_dt_bias=ddt[0, DN_HEADS:2 * DN_HEADS],
                 dn_norm_g=dgn[0])
    big = dict(w_cat=dw_cat, w_branch_sb=dwb_sb, w_branch_dn=dwb_dn, w_out=dw_out)
    return dx, small, big


def _cat_cols(w, D):
    return jnp.concatenate([w[:, 2048:4096], w[:, 0:2048], w[:, 4104:4104 + 2 * D], w[:, 4096:4104],
                            jnp.zeros((w.shape[0], LANES - 8), w.dtype)], axis=1)


def _uncat_cols(g, D):
    return jnp.concatenate([g[:, 2048:4096], g[:, 0:2048], g[:, 4096 + 2 * D:4096 + 2 * D + 8],
                            g[:, 4096:4096 + 2 * D]], axis=1)


def _flat_pack(arrs, mult):
    flat = jnp.concatenate([a.reshape(-1) for a in arrs])
    n = flat.shape[0]
    pad = (-n) % mult
    if pad:
        flat = jnp.concatenate([flat, jnp.zeros((pad,), flat.dtype)])
    return flat.reshape(-1, LANES)


def _flat_unpack(flat, shapes):
    flat = flat.reshape(-1)
    out, off = [], 0
    for s in shapes:
        n = math.prod(s)
        out.append(flat[off:off + n].reshape(s))
        off += n
    return out


BIG = ("w_in", "w_branch_sb", "w_branch_dn", "w_out")
SMALL = ("ada_b", "norm_g", "sb_q_g", "sb_k_g", "conv_w", "dn_a_log", "dn_dt_bias", "dn_norm_g")
_BIG_MULT = LANES * 32


def kernel(x, c, ada_w, ada_b, norm_g, w_in, sb_q_g, sb_k_g, conv_w, dn_a_log, dn_dt_bias, dn_norm_g, w_branch_sb, w_branch_dn, w_out, loss_target, m_ada_w, m_ada_b, m_norm_g, m_w_in, m_sb_q_g, m_sb_k_g, m_conv_w, m_dn_a_log, m_dn_dt_bias, m_dn_norm_g, m_w_branch_sb, m_w_branch_dn, m_w_out, v_ada_w, v_ada_b, v_norm_g, v_w_in, v_sb_q_g, v_sb_k_g, v_conv_w, v_dn_a_log, v_dn_dt_bias, v_dn_norm_g, v_w_branch_sb, v_w_branch_dn, v_w_out):
    W = dict(ada_w=ada_w, ada_b=ada_b, norm_g=norm_g, w_in=w_in, sb_q_g=sb_q_g, sb_k_g=sb_k_g, conv_w=conv_w,
             dn_a_log=dn_a_log, dn_dt_bias=dn_dt_bias, dn_norm_g=dn_norm_g, w_branch_sb=w_branch_sb,
             w_branch_dn=w_branch_dn, w_out=w_out)
    M = dict(ada_w=m_ada_w, ada_b=m_ada_b, norm_g=m_norm_g, w_in=m_w_in, sb_q_g=m_sb_q_g, sb_k_g=m_sb_k_g,
             conv_w=m_conv_w, dn_a_log=m_dn_a_log, dn_dt_bias=m_dn_dt_bias, dn_norm_g=m_dn_norm_g,
             w_branch_sb=m_w_branch_sb, w_branch_dn=m_w_branch_dn, w_out=m_w_out)
    V = dict(ada_w=v_ada_w, ada_b=v_ada_b, norm_g=v_norm_g, w_in=v_w_in, sb_q_g=v_sb_q_g, sb_k_g=v_sb_k_g,
             conv_w=v_conv_w, dn_a_log=v_dn_a_log, dn_dt_bias=v_dn_dt_bias, dn_norm_g=v_dn_norm_g,
             w_branch_sb=v_w_branch_sb, w_branch_dn=v_w_branch_dn, w_out=v_w_out)
    L = ada_w.shape[0]
    S, D = x.shape[1], x.shape[2]
    ix, iy, ic = lax.axis_index("x"), lax.axis_index("y"), lax.axis_index("c")
    shard = 2 * ix + iy
    me = 2 * shard + ic
    n_ada = ada_w.shape[2]
    n_in = w_in.shape[2]
    n_conv = conv_w.shape[2]
    n_br = w_branch_sb.shape[2]
    n_out = w_out.shape[1]

    big_shapes = [W[n].shape for n in BIG]
    wflat = _flat_pack([W[n].astype(_MXU_DTYPE) for n in BIG], _BIG_MULT)
    wall = _ag_weights(wflat)
    per_shard = [_flat_unpack(wall[s], big_shapes) for s in range(4)]
    w_in_f = jnp.concatenate([per_shard[s][0] for s in range(4)], axis=2)
    wb_sb_f = jnp.concatenate([per_shard[s][1] for s in range(4)], axis=2)
    wb_dn_f = jnp.concatenate([per_shard[s][2] for s in range(4)], axis=2)
    w_out_f = jnp.concatenate([per_shard[s][3] for s in range(4)], axis=1)

    g1 = _ag_small("ag_c_conv", _flat_pack([c, conv_w], LANES * 8))
    g1 = g1.reshape(8, -1)
    c_all = g1[:, :D]
    conv_parts = g1[:, D:D + L * CONV_K * n_conv].reshape(4, 2, L, CONV_K, n_conv)[:, 0]
    conv_full = jnp.concatenate([conv_parts[s] for s in range(4)], axis=2)
    ada_b_sh = lax.dynamic_slice_in_dim(ada_b, shard * n_ada, n_ada, axis=1)[:, None, :]
    mod_sh = _ada_fwd(c_all, ada_w, ada_b_sh)
    g2 = _ag_small("ag_mod", _flat_pack([mod_sh], LANES * 8)).reshape(8, -1)
    mod_parts = g2[:, :L * 8 * n_ada].reshape(4, 2, L, 8, n_ada)[:, 0]
    mod_all = jnp.concatenate([mod_parts[s] for s in range(4)], axis=2)
    mod = lax.dynamic_index_in_dim(mod_all, me, axis=1, keepdims=False)

    def layer_weights(l):
        pad_lo = jnp.zeros((DN_HEADS,), F32)
        pad_hi = jnp.zeros((LANES - 2 * DN_HEADS,), F32)
        return dict(
            norm_g=norm_g[l][None, :], w_cat=_cat_cols(w_in_f[l], D),
            gq_t=jnp.tile(sb_q_g[l], SB_HEADS)[None, :], gk_t=jnp.tile(sb_k_g[l], SB_HEADS)[None, :],
            conv_w=conv_full[l],
            a_row=jnp.concatenate([pad_lo, dn_a_log[l], pad_hi])[None, :],
            dtb_row=jnp.concatenate([pad_lo, dn_dt_bias[l], pad_hi])[None, :],
            gn=dn_norm_g[l][None, :], wb_sb=wb_sb_f[l], wb_dn=wb_dn_f[l], w_out=w_out_f[l])

    xs = x[0]
    lws, ress = [], []
    for l in range(L):
        lw = layer_weights(l)
        xs, res = _layer_fwd(xs, mod[l, None, 0:D], mod[l, None, D:2 * D], mod[l, None, 2 * D:3 * D], lw)
        lws.append(lw)
        ress.append(res)
    dxs, loss_row = _loss_head(xs, loss_target[0])
    loss = lax.psum(loss_row[0, 0], ("x", "y", "c"))
    smalls, bigs = [None] * L, [None] * L
    for l in reversed(range(L)):
        dxs, smalls[l], bigs[l] = _layer_bwd(dxs, ress[l], lws[l])
    grad_x = dxs[None]

    small_names = ("dmod",) + SMALL[1:]
    small_pack = _flat_pack([jnp.stack([smalls[l][n] for l in range(L)]) for n in small_names], LANES * 8)
    g3 = _ag_small("ag_small_grads", small_pack)
    R3 = small_pack.shape[0]
    g3 = g3.reshape(8, R3, LANES)
    small_sum = _sum_parts("sum_small", g3)
    small_shapes = [(L, 3 * D), (L, D), (L, SB_HD), (L, SB_HD), (L, CONV_K, 3 * DN_W), (L, DN_HEADS), (L, DN_HEADS),
                    (L, DN_HD)]
    sg = dict(zip(small_names, _flat_unpack(small_sum, small_shapes)))
    G = dict(ada_b=sg["dmod"], norm_g=sg["norm_g"], sb_q_g=sg["sb_q_g"], sb_k_g=sg["sb_k_g"],
             conv_w=lax.dynamic_slice_in_dim(sg["conv_w"], shard * n_conv, n_conv, axis=2),
             dn_a_log=sg["dn_a_log"], dn_dt_bias=sg["dn_dt_bias"], dn_norm_g=sg["dn_norm_g"])
    dmod_all = g3.reshape(8, -1)[:, :L * 3 * D].reshape(8, L, 3 * D)
    dmod_sh = lax.dynamic_slice_in_dim(dmod_all, shard * n_ada, n_ada, axis=2).transpose(1, 0, 2)
    G["ada_w"] = _ada_bwd(c_all.T, dmod_sh)

    gfull = dict(w_in=jnp.stack([_uncat_cols(bigs[l]["w_cat"], D) for l in range(L)]),
                 w_branch_sb=jnp.stack([bigs[l]["w_branch_sb"] for l in range(L)]),
                 w_branch_dn=jnp.stack([bigs[l]["w_branch_dn"] for l in range(L)]),
                 w_out=jnp.stack([bigs[l]["w_out"] for l in range(L)]))
    pieces = []
    for s in range(4):
        pieces.append(_flat_pack([gfull["w_in"][:, :, s * n_in:(s + 1) * n_in],
                                  gfull["w_branch_sb"][:, :, s * n_br:(s + 1) * n_br],
                                  gfull["w_branch_dn"][:, :, s * n_br:(s + 1) * n_br],
                                  gfull["w_out"][:, s * n_out:(s + 1) * n_out, :]], _BIG_MULT))
    Rw = pieces[0].shape[0]
    send = jnp.stack(pieces).reshape(8, Rw // 2, LANES)
    got = _all_to_all(send)
    ghalf = _sum_parts("sum_big", got)
    gshard = _sibling_join(ghalf).reshape(Rw, LANES)
    for n, g in zip(BIG, _flat_unpack(gshard, big_shapes)):
        G[n] = g

    delta, new_m, new_v = {}, {}, {}
    for n in ("ada_w",) + BIG:
        sh = W[n].shape
        two = (sh[0] * sh[1], sh[2])
        d, mo, vo = _adamw("adamw_" + n, W[n].reshape(two), G[n].reshape(two), M[n].reshape(two), V[n].reshape(two))
        delta[n], new_m[n], new_v[n] = d.reshape(sh), mo.reshape(sh), vo.reshape(sh)
    sm_shapes = [W[n].shape for n in SMALL]
    d, mo, vo = _adamw("adamw_small", _flat_pack([W[n] for n in SMALL], LANES * 8),
                       _flat_pack([G[n] for n in SMALL], LANES * 8), _flat_pack([M[n] for n in SMALL], LANES * 8),
                       _flat_pack([V[n] for n in SMALL], LANES * 8))
    for n, dd, mm, vv in zip(SMALL, _flat_unpack(d, sm_shapes), _flat_unpack(mo, sm_shapes),
                             _flat_unpack(vo, sm_shapes)):
        delta[n], new_m[n], new_v[n] = dd, mm, vv

    order = ("ada_w", "ada_b", "norm_g", "w_in", "sb_q_g", "sb_k_g", "conv_w", "dn_a_log", "dn_dt_bias", "dn_norm_g",
             "w_branch_sb", "w_branch_dn", "w_out")
    return (loss, grad_x, *[G[n] for n in order], *[delta[n] for n in order], *[new_m[n] for n in order],
            *[new_v[n] for n in order])
```

```python
import math

import jax
import jax.numpy as jnp
from jax import lax
from jax.experimental import pallas as pl
from jax.experimental.pallas import tpu as pltpu

F32 = jnp.float32
BF16 = jnp.bfloat16
_MXU_DTYPE = BF16
_VMEM_LIMIT = 48 * 1024 * 1024
LANES = 128

EPS = 1e-6
SB_HEADS, SB_HD, SB_W = 8, 64, 512
DN_HEADS, DN_HD, DN_W = 4, 128, 512
CONV_K = 4
CHUNK = 64
QB = 256
ADAM_LR, ADAM_B1, ADAM_B2, ADAM_EPS, ADAM_WD, ADAM_STEP = 0.001, 0.9, 0.999, 1e-08, 0.01, 10

C_DN_QKV, C_DN_Z, C_SB_Q, C_SB_K, C_SB_V, C_SB_Z, C_MG = 0, 1536, 2048, 2560, 3072, 3584, 4096

_NN = (((1,), (0,)), ((), ()))
_NT = (((1,), (1,)), ((), ()))
_TN = (((0,), (0,)), ((), ()))
MESH = pl.DeviceIdType.MESH


def _sds(shape, dtype):
    return jax.ShapeDtypeStruct(shape, dtype)


def _cp(n):
    return pltpu.CompilerParams(dimension_semantics=("arbitrary",) * n, vmem_limit_bytes=_VMEM_LIMIT)


def _rb(tm, w, cb=0):
    return pl.BlockSpec((tm, w), lambda i: (i, cb))


def _fs(shape):
    nd = len(shape)
    return pl.BlockSpec(shape, lambda i: (0,) * nd)


def _dg(a, b, dims):
    return lax.dot_general(a, b, dims, preferred_element_type=F32)


def _mm(a, b, dims=_NN):
    return _dg(a.astype(_MXU_DTYPE), b.astype(_MXU_DTYPE), dims)


def _split3(x):
    hi = x.astype(BF16)
    r = x - hi.astype(F32)
    mid = r.astype(BF16)
    lo = (r - mid.astype(F32)).astype(BF16)
    return hi, mid, lo


def _mm_xl(x, const, dims=_NN):
    cb = const.astype(BF16)
    hi, mid, lo = _split3(x)
    return _dg(hi, cb, dims) + _dg(mid, cb, dims) + _dg(lo, cb, dims)


def _mm_xl2(x, const, dims=_NN):
    cb = const.astype(BF16)
    hi = x.astype(BF16)
    lo = (x - hi.astype(F32)).astype(BF16)
    return _dg(hi, cb, dims) + _dg(lo, cb, dims)


def _mm_xr(const, x, dims=_NN):
    cb = const.astype(BF16)
    hi, mid, lo = _split3(x)
    return _dg(cb, hi, dims) + _dg(cb, mid, dims) + _dg(cb, lo, dims)


def _mm3(a, b, dims=_NN):
    ah, am, _ = _split3(a)
    bh, bm, _ = _split3(b)
    return _dg(ah, bh, dims) + (_dg(ah, bm, dims) + _dg(am, bh, dims))


def _sigmoid(z):
    return 1.0 / (1.0 + jnp.exp(-z))


def _silu(z):
    return z * _sigmoid(z)


def _dsilu(z):
    s = _sigmoid(z)
    return s * (1.0 + z * (1.0 - s))


def _softplus(z):
    return jnp.maximum(z, 0.0) + jnp.log(1.0 + jnp.exp(-jnp.abs(z)))


def _iota2(shape, dim):
    return lax.broadcasted_iota(jnp.int32, shape, dim)


def _pick(n, cap, mult):
    best = None
    for t in range(mult, min(n, cap) + 1, mult):
        if n % t == 0:
            best = t
    assert best is not None, (n, cap, mult)
    return best


def _matmul(name, a, b, form, out_dtype, tm_cap=512, tn_cap=1024, tk_cap=1024):
    if form == "nn":
        (M, K), (_, N) = a.shape, b.shape
    elif form == "nt":
        (M, K), (N, _) = a.shape, b.shape
    else:
        (K, M), (_, N) = a.shape, b.shape
    tm = _pick(M, tm_cap, 128 if form == "tn" else 8)
    tn = _pick(N, tn_cap, 128)
    tk = _pick(K, tk_cap, 128)
    nk = K // tk
    dims = {"nn": _NN, "nt": _NT, "tn": _TN}[form]
    if form == "nn":
        a_spec = pl.BlockSpec((tm, tk), lambda i, j, k: (i, k))
        b_spec = pl.BlockSpec((tk, tn), lambda i, j, k: (k, j))
    elif form == "nt":
        a_spec = pl.BlockSpec((tm, tk), lambda i, j, k: (i, k))
        b_spec = pl.BlockSpec((tn, tk), lambda i, j, k: (j, k))
    else:
        a_spec = pl.BlockSpec((tk, tm), lambda i, j, k: (k, i))
        b_spec = pl.BlockSpec((tk, tn), lambda i, j, k: (k, j))

    def body(a_ref, b_ref, o_ref, acc_ref):
        k = pl.program_id(2)

        @pl.when(k == 0)
        def _():
            acc_ref[...] = jnp.zeros_like(acc_ref)

        acc_ref[...] += _mm(a_ref[...], b_ref[...], dims)

        @pl.when(k == nk - 1)
        def _():
            o_ref[...] = acc_ref[...].astype(o_ref.dtype)

    return pl.pallas_call(
        body, name=name, grid=(M // tm, N // tn, nk),
        in_specs=[a_spec, b_spec],
        out_specs=pl.BlockSpec((tm, tn), lambda i, j, k: (i, j)),
        out_shape=_sds((M, N), out_dtype),
        scratch_shapes=[pltpu.VMEM((tm, tn), F32)],
        compiler_params=_cp(3),
    )(a, b)


def _norm_mod(x, g, scale, shift, tm=256):
    S, D = x.shape

    def body(x_ref, g_ref, sc_ref, sh_ref, h_ref):
        xv = x_ref[...]
        r = lax.rsqrt(jnp.mean(xv * xv, axis=1, keepdims=True) + EPS)
        h_ref[...] = ((xv * r * g_ref[...]) * (1.0 + sc_ref[...]) + sh_ref[...]).astype(h_ref.dtype)

    return pl.pallas_call(
        body, name="norm_mod", grid=(S // tm,),
        in_specs=[_rb(tm, D), _fs((1, D)), _fs((1, D)), _fs((1, D))],
        out_specs=_rb(tm, D), out_shape=_sds((S, D), _MXU_DTYPE), compiler_params=_cp(1),
    )(x, g, scale, shift)


def _norm_mod_bwd(x, dh, dxn, g, scale, tm=256):
    S, D = x.shape

    def body(x_ref, dh_ref, dxn_ref, g_ref, sc_ref, dx_ref, dsh_ref, dsc_ref, dg_ref):
        @pl.when(pl.program_id(0) == 0)
        def _():
            dsh_ref[...] = jnp.zeros_like(dsh_ref)
            dsc_ref[...] = jnp.zeros_like(dsc_ref)
            dg_ref[...] = jnp.zeros_like(dg_ref)

        xv, dhv, gv = x_ref[...], dh_ref[...], g_ref[...]
        r = lax.rsqrt(jnp.mean(xv * xv, axis=1, keepdims=True) + EPS)
        xh = xv * r
        one_sc = 1.0 + sc_ref[...]
        dsh_ref[...] += jnp.sum(dhv, axis=0, keepdims=True)
        dsc_ref[...] += jnp.sum(dhv * xh * gv, axis=0, keepdims=True)
        dg_ref[...] += jnp.sum(dhv * one_sc * xh, axis=0, keepdims=True)
        dxh = dhv * (gv * one_sc)
        dx_ref[...] = r * (dxh - xh * jnp.mean(dxh * xh, axis=1, keepdims=True)) + dxn_ref[...]

    return pl.pallas_call(
        body, name="norm_mod_bwd", grid=(S // tm,),
        in_specs=[_rb(tm, D), _rb(tm, D), _rb(tm, D), _fs((1, D)), _fs((1, D))],
        out_specs=[_rb(tm, D), _fs((1, D)), _fs((1, D)), _fs((1, D))],
        out_shape=[_sds((S, D), F32)] + [_sds((1, D), F32)] * 3, compiler_params=_cp(1),
    )(x, dh, dxn, g, scale)


def _head_sum_matrix():
    r = jnp.arange(SB_W)
    return (r[:, None] // SB_HD == r[None, :] // SB_HD).astype(BF16)


def _sb_prep(p, gq_t, gk_t, tm=256):
    S = p.shape[0]
    bd = _head_sum_matrix()

    def body(q_ref, k_ref, gq_ref, gk_ref, bd_ref, qn_ref, kn_ref):
        for src, g_ref, dst in ((q_ref, gq_ref, qn_ref), (k_ref, gk_ref, kn_ref)):
            v = src[...]
            ms = _mm_xl(v * v, bd_ref[...]) * (1.0 / SB_HD)
            dst[...] = (v * lax.rsqrt(ms + EPS) * g_ref[...]).astype(dst.dtype)

    return pl.pallas_call(
        body, name="sb_prep", grid=(S // tm,),
        in_specs=[_rb(tm, SB_W, C_SB_Q // SB_W), _rb(tm, SB_W, C_SB_K // SB_W),
                  _fs((1, SB_W)), _fs((1, SB_W)), _fs((SB_W, SB_W))],
        out_specs=[_rb(tm, SB_W), _rb(tm, SB_W)],
        out_shape=[_sds((S, SB_W), _MXU_DTYPE)] * 2, compiler_params=_cp(1),
    )(p, p, gq_t, gk_t, bd)


def _sb_prep_bwd(p, dqn, dkn, gq_t, gk_t, tm=256):
    S = p.shape[0]
    bd = _head_sum_matrix()

    def body(q_ref, k_ref, dqn_ref, dkn_ref, gq_ref, gk_ref, bd_ref, dq_ref, dk_ref, dgq_ref, dgk_ref):
        @pl.when(pl.program_id(0) == 0)
        def _():
            dgq_ref[...] = jnp.zeros_like(dgq_ref)
            dgk_ref[...] = jnp.zeros_like(dgk_ref)

        for src, dn_ref, g_ref, dst, dg_ref in ((q_ref, dqn_ref, gq_ref, dq_ref, dgq_ref),
                                                (k_ref, dkn_ref, gk_ref, dk_ref, dgk_ref)):
            v, dn = src[...], dn_ref[...]
            r = lax.rsqrt(_mm_xl(v * v, bd_ref[...]) * (1.0 / SB_HD) + EPS)
            vh = v * r
            dg_ref[...] += jnp.sum(dn * vh, axis=0, keepdims=True)
            dvh = dn * g_ref[...]
            m = _mm_xl(dvh * vh, bd_ref[...]) * (1.0 / SB_HD)
            dst[...] = (r * (dvh - vh * m)).astype(dst.dtype)

    return pl.pallas_call(
        body, name="sb_prep_bwd", grid=(S // tm,),
        in_specs=[_rb(tm, SB_W, C_SB_Q // SB_W), _rb(tm, SB_W, C_SB_K // SB_W), _rb(tm, SB_W), _rb(tm, SB_W),
                  _fs((1, SB_W)), _fs((1, SB_W)), _fs((SB_W, SB_W))],
        out_specs=[_rb(tm, SB_W), _rb(tm, SB_W), _fs((1, SB_W)), _fs((1, SB_W))],
        out_shape=[_sds((S, SB_W), _MXU_DTYPE)] * 2 + [_sds((1, SB_W), F32)] * 2, compiler_params=_cp(1),
    )(p, p, dqn, dkn, gq_t, gk_t, bd)


def _sb_consts():
    r, c = _iota2((QB, QB), 0), _iota2((QB, QB), 1)
    lane = _iota2((1, LANES), 1)
    return r, c, lane


def _sb_fwd(qn, kn, p):
    S = qn.shape[0]
    scale = 1.0 / math.sqrt(SB_HD)

    def body(q_ref, k_ref, v_ref, o_ref, tot_ref):
        i = pl.program_id(1)
        r, c, lane = _sb_consts()
        u_gt = (r > c).astype(BF16)
        strict = c < r
        q = q_ref[...]
        masks = [(lane // SB_HD) == h for h in range(2)]
        qhs = [jnp.where(m, q, jnp.zeros_like(q)) for m in masks]

        def block(off, carry, diagonal):
            kj = k_ref[pl.ds(off, QB), :]
            vj = v_ref[pl.ds(off, QB), :].astype(_MXU_DTYPE)
            out = []
            for h in range(2):
                o_h, run = carry[2 * h], carry[2 * h + 1]
                z = _mm(qhs[h], kj, _NT) * scale
                sp = _softplus(z)
                sp_m = jnp.where(strict, sp, 0.0) if diagonal else sp
                later = _mm_xl2(sp_m, u_gt)
                w = jnp.exp((z - sp) - later - run)
                if diagonal:
                    w = jnp.where(strict, w, 0.0)
                out += [o_h + _mm(w, vj), run + jnp.sum(sp_m, axis=1, keepdims=True)]
            return tuple(out)

        init = (jnp.zeros((QB, LANES), F32), jnp.zeros((QB, 1), F32)) * 2
        carry = block(pl.multiple_of(i * QB, QB), init, True)
        o0, run0, o1, run1 = lax.fori_loop(
            1, i + 1, lambda jj, cr: block(pl.multiple_of((i - jj) * QB, QB), cr, False), carry)
        o_ref[...] = jnp.where(masks[0], o0, o1)
        tot_ref[...] = jnp.where(masks[0], run0, run1)

    return pl.pallas_call(
        body, name="sb_fwd", grid=(SB_W // LANES, S // QB),
        in_specs=[pl.BlockSpec((QB, LANES), lambda hp, i: (i, hp)),
                  pl.BlockSpec((S, LANES), lambda hp, i: (0, hp)),
                  pl.BlockSpec((S, LANES), lambda hp, i: (0, C_SB_V // LANES + hp))],
        out_specs=[pl.BlockSpec((QB, LANES), lambda hp, i: (i, hp))] * 2,
        out_shape=[_sds((S, SB_W), F32)] * 2, compiler_params=_cp(2),
    )(qn, kn, p)


def _sb_bwd(qn, kn, p, do, tot):
    S = qn.shape[0]
    scale = 1.0 / math.sqrt(SB_HD)

    def body(q_ref, k_ref, v_ref, do_ref, tot_ref, dq_ref, dk_ref, dv_ref):
        i = pl.program_id(1)

        @pl.when(i == 0)
        def _():
            dk_ref[...] = jnp.zeros_like(dk_ref)
            dv_ref[...] = jnp.zeros_like(dv_ref)

        r, c, lane = _sb_consts()
        u_le = (r <= c).astype(BF16)
        u_lt = (r < c).astype(BF16)
        strict = c < r
        q = q_ref[...]
        do = do_ref[...].astype(_MXU_DTYPE)
        tot_pair = tot_ref[...]
        masks = [(lane // SB_HD) == h for h in range(2)]
        qhs = [jnp.where(m, q, jnp.zeros_like(q)) for m in masks]
        dohs = [jnp.where(m, do, jnp.zeros_like(do)) for m in masks]
        tots = [jnp.max(jnp.where(m, tot_pair, 0.0), axis=1, keepdims=True) for m in masks]

        def block(off, carry, diagonal):
            kj = k_ref[pl.ds(off, QB), :]
            vj = v_ref[pl.ds(off, QB), :].astype(_MXU_DTYPE)
            out, dk_blk, dv_blk = [], None, None
            for h in range(2):
                dq_h, pre_sp, pre_e = carry[3 * h:3 * h + 3]
                z = _mm(qhs[h], kj, _NT) * scale
                sp = _softplus(z)
                a = z - sp
                sp_m = jnp.where(strict, sp, 0.0) if diagonal else sp
                incl = _mm_xl2(sp_m, u_le)
                w = jnp.exp(a - ((tots[h] - pre_sp) - incl))
                if diagonal:
                    w = jnp.where(strict, w, 0.0)
                e = w * _mm(dohs[h], vj, _NT)
                db = pre_e + _mm_xl2(e, u_lt)
                dz = (e - jnp.exp(a) * (e + db)) * scale
                if diagonal:
                    dz = jnp.where(strict, dz, 0.0)
                dkh = jnp.where(masks[h], _mm(dz, q, _TN), 0.0)
                dvh = jnp.where(masks[h], _mm(w, do, _TN), 0.0)
                dk_blk = dkh if dk_blk is None else dk_blk + dkh
                dv_blk = dvh if dv_blk is None else dv_blk + dvh
                out += [dq_h + _mm(dz, kj), pre_sp + jnp.sum(sp_m, axis=1, keepdims=True),
                        pre_e + jnp.sum(e, axis=1, keepdims=True)]
            dk_ref[pl.ds(off, QB), :] += dk_blk
            dv_ref[pl.ds(off, QB), :] += dv_blk
            return tuple(out)

        zero_col = jnp.zeros((QB, 1), F32)
        init = (jnp.zeros((QB, LANES), F32), zero_col, zero_col) * 2
        carry = lax.fori_loop(0, i, lambda j, cr: block(pl.multiple_of(j * QB, QB), cr, False), init)
        carry = block(pl.multiple_of(i * QB, QB), carry, True)
        dq_ref[...] = jnp.where(masks[0], carry[0], carry[3])

    blk = pl.BlockSpec((QB, LANES), lambda hp, i: (i, hp))
    full = pl.BlockSpec((S, LANES), lambda hp, i: (0, hp))
    return pl.pallas_call(
        body, name="sb_bwd", grid=(SB_W // LANES, S // QB),
        in_specs=[blk, full, pl.BlockSpec((S, LANES), lambda hp, i: (0, C_SB_V // LANES + hp)), blk, blk],
        out_specs=[blk, full, full],
        out_shape=[_sds((S, SB_W), F32)] * 3, compiler_params=_cp(2),
    )(qn, kn, p, do, tot)


def _dn_prep(p, conv_w, a_row, dtb_row, tm=256):
    S = p.shape[0]
    W3 = 3 * DN_W
    nhalo = tm // 8

    def body(x_ref, halo_ref, w_ref, ba_ref, a_ref, dtb_ref, qkv_ref, bb_ref, gc_ref, gl_ref):
        i = pl.program_id(0)
        halo = jnp.where(i > 0, halo_ref[...], 0.0)
        xf = jnp.concatenate([halo, x_ref[...]], axis=0)
        acc = jnp.zeros((tm, W3), F32)
        for k in range(CONV_K):
            sh = CONV_K - 1 - k
            xs = xf if sh == 0 else pltpu.roll(xf, sh, 0)
            acc = acc + xs[8:, :] * w_ref[k:k + 1, :]
        s = _silu(acc)
        for gi in range(2 * DN_HEADS):
            sl = slice(gi * LANES, (gi + 1) * LANES)
            sg = s[:, sl]
            rinv = lax.rsqrt(jnp.sum(sg * sg, axis=1, keepdims=True) + EPS)
            qkv_ref[:, sl] = sg * rinv * (DN_HD ** -0.5 if gi < DN_HEADS else 1.0)
        qkv_ref[:, 2 * DN_W:] = s[:, 2 * DN_W:]

        ba = ba_ref[...]
        beta = _sigmoid(ba)
        g = -jnp.exp(a_ref[...]) * _softplus(ba + dtb_ref[...])
        lr, lc = _iota2((LANES, DN_W), 0), _iota2((LANES, DN_W), 1)
        sel_b = (lr == lc // LANES).astype(BF16)
        sel_g = (lr == lc // LANES + DN_HEADS).astype(BF16)
        bb_ref[...] = _mm_xl(beta, sel_b)
        graw = _mm_xl(g, sel_g)
        rr, cc = _iota2((tm, tm), 0), _iota2((tm, tm), 1)
        tri = jnp.logical_and(rr >= cc, rr // CHUNK == cc // CHUNK).astype(BF16)
        gc = _mm_xr(tri, graw)
        last = (cc == (rr // CHUNK) * CHUNK + (CHUNK - 1)).astype(BF16)
        gc_ref[...] = gc
        gl_ref[...] = _mm_xr(last, gc)

    return pl.pallas_call(
        body, name="dn_prep", grid=(S // tm,),
        in_specs=[_rb(tm, W3, 0), pl.BlockSpec((8, W3), lambda i: (jnp.maximum(i * nhalo - 1, 0), 0)),
                  _fs((CONV_K, W3)), _rb(tm, LANES, (p.shape[1] - LANES) // LANES),
                  _fs((1, LANES)), _fs((1, LANES))],
        out_specs=[_rb(tm, W3), _rb(tm, DN_W), _rb(tm, DN_W), _rb(tm, DN_W)],
        out_shape=[_sds((S, W3), F32)] + [_sds((S, DN_W), F32)] * 3, compiler_params=_cp(1),
    )(p, p, conv_w, p, a_row, dtb_row)


def _dn_chunk_terms(q, k, v, beta, gc, gl):
    r, c = _iota2((CHUNK, CHUNK), 0), _iota2((CHUNK, CHUNK), 1)
    tril, strict = r >= c, r > c
    gcol = _mm_xl(gc, jnp.full((LANES, CHUNK), 1.0 / LANES, F32))
    grow = _mm_xr(jnp.full((CHUNK, LANES), 1.0 / LANES, F32), gc, _NT)
    dec = jnp.where(tril, jnp.exp(jnp.where(tril, gcol - grow, 0.0)), 0.0)
    gam = jnp.exp(gc)
    dlt = jnp.exp(gl - gc)
    kb, vb = k * beta, v * beta
    pm = _mm(kb, k, _NT)
    qk = _mm(q, k, _NT)
    m = jnp.where(strict, pm * dec, 0.0)
    a = jnp.where(tril, qk * dec, 0.0)
    return dict(tril=tril, strict=strict, dec=dec, gam=gam, dlt=dlt, kb=kb, vb=vb, m=m, a=a)


def _dn_fwd(qkv, bb, gcb, glb):
    S = qkv.shape[0]
    N = S // CHUNK

    def body(qkv_ref, bb_ref, gc_ref, gl_ref, o_ref, t_ref, sall_ref, s_scr):
        @pl.when(pl.program_id(0) == 0)
        def _():
            s_scr[...] = jnp.zeros_like(s_scr)

        r, c = _iota2((CHUNK, CHUNK), 0), _iota2((CHUNK, CHUNK), 1)
        eye = (r == c).astype(F32)
        for h in range(DN_HEADS):
            sl = slice(h * LANES, (h + 1) * LANES)
            q, k = qkv_ref[:, sl], qkv_ref[:, DN_W + h * LANES:DN_W + (h + 1) * LANES]
            v = qkv_ref[:, 2 * DN_W + h * LANES:2 * DN_W + (h + 1) * LANES]
            beta, gc, gl = bb_ref[:, sl], gc_ref[:, sl], gl_ref[:, sl]
            s0 = s_scr[h]
            sall_ref[0, h] = s0.astype(sall_ref.dtype)
            s0 = s0.astype(sall_ref.dtype).astype(F32)
            t = _dn_chunk_terms(q, k, v, beta, gc, gl)
            pw = -t["m"]
            tinv = eye + pw
            for _ in range(5):
                pw = _mm3(pw, pw)
                tinv = tinv + _mm3(tinv, pw)
            t_ref[h] = tinv
            u = _mm3(tinv, t["vb"])
            w = _mm3(tinv, t["kb"] * t["gam"])
            vn = u - _mm(w, s0)
            o_ref[:, sl] = _mm(q * t["gam"], s0) + _mm(t["a"], vn)
            egl = jnp.exp(jnp.concatenate([gl, gl], axis=0))
            s_scr[h] = s_scr[h] * egl + _mm(k * t["dlt"], vn, _TN)

    return pl.pallas_call(
        body, name="dn_fwd", grid=(N,),
        in_specs=[_rb(CHUNK, 3 * DN_W), _rb(CHUNK, DN_W), _rb(CHUNK, DN_W), _rb(CHUNK, DN_W)],
        out_specs=[_rb(CHUNK, DN_W), pl.BlockSpec((DN_HEADS, CHUNK, CHUNK), lambda n: (0, n, 0)),
                   pl.BlockSpec((1, DN_HEADS, DN_HD, DN_HD), lambda n: (n, 0, 0, 0))],
        out_shape=[_sds((S, DN_W), F32), _sds((DN_HEADS, S, CHUNK), F32),
                   _sds((N, DN_HEADS, DN_HD, DN_HD), _MXU_DTYPE)],
        scratch_shapes=[pltpu.VMEM((DN_HEADS, DN_HD, DN_HD), F32)],
        compiler_params=_cp(1),
    )(qkv, bb, gcb, glb)


def _dn_bwd(qkv, bb, gcb, glb, tinv_all, sall, do):
    S = qkv.shape[0]
    N = S // CHUNK

    def body(qkv_ref, bb_ref, gc_ref, gl_ref, t_ref, sall_ref, do_ref, dqkv_ref, dbb_ref, dg_ref, ds_scr):
        @pl.when(pl.program_id(0) == 0)
        def _():
            ds_scr[...] = jnp.zeros_like(ds_scr)

        r, c = _iota2((CHUNK, CHUNK), 0), _iota2((CHUNK, CHUNK), 1)
        eye = (r == c).astype(F32)
        u_ge = (c >= r).astype(F32)
        last_row = _iota2((CHUNK, LANES), 0) == CHUNK - 1
        for h in range(DN_HEADS):
            sl = slice(h * LANES, (h + 1) * LANES)
            slk = slice(DN_W + h * LANES, DN_W + (h + 1) * LANES)
            slv = slice(2 * DN_W + h * LANES, 2 * DN_W + (h + 1) * LANES)
            q, k, v = qkv_ref[:, sl], qkv_ref[:, slk], qkv_ref[:, slv]
            beta, gc, gl = bb_ref[:, sl], gc_ref[:, sl], gl_ref[:, sl]
            tinv = t_ref[h]
            s0 = sall_ref[0, h].astype(F32)
            do = do_ref[:, sl]
            ds1 = ds_scr[h]
            t = _dn_chunk_terms(q, k, v, beta, gc, gl)
            gam, dlt, kb, vb, dec = t["gam"], t["dlt"], t["kb"], t["vb"], t["dec"]
            kbg = kb * gam
            u = _mm3(tinv, vb)
            w = _mm3(tinv, kbg)
            vn = u - _mm(w, s0)
            qg, kd = q * gam, k * dlt
            egl = jnp.exp(gl)
            egl2 = jnp.concatenate([egl, egl], axis=0)

            dvn = _mm(t["a"], do, _TN) + _mm(kd, ds1)
            da = jnp.where(t["tril"], _mm(do, vn, _NT), 0.0)
            dqg = _mm(do, s0, _NT)
            dkd = _mm(vn, ds1, _NT)
            dw = -_mm(dvn, s0, _NT)
            ds_scr[h] = _mm(qg, do, _TN) + egl2 * ds1 - _mm(w, dvn, _TN)
            tt = _mm_xr(eye, tinv, _NT)
            dvb = _mm3(tt, dvn)
            dkbg = _mm3(tt, dw)
            dm = -jnp.where(t["strict"], _mm(dvb, u, _NT) + _mm(dkbg, w, _NT), 0.0)
            dpm = dm * dec
            dqk = da * dec
            dkb = dkbg * gam + _mm(dpm, k)
            dk = dkd * dlt + _mm(dpm, kb, _TN) + _mm(dqk, q, _TN) + dkb * beta
            dq = dqg * gam + _mm(dqk, k)
            dqkv_ref[:, sl] = dq
            dqkv_ref[:, slk] = dk
            dqkv_ref[:, slv] = dvb * beta
            dbb_ref[:, sl] = jnp.broadcast_to(
                jnp.sum(dkb * k, axis=1, keepdims=True) + jnp.sum(dvb * v, axis=1, keepdims=True), (CHUNK, LANES))
            dgam = jnp.sum(dqg * q, axis=1, keepdims=True) + jnp.sum(dkbg * kb, axis=1, keepdims=True)
            ddlt = jnp.sum(dkd * k, axis=1, keepdims=True)
            xm = dm * t["m"] + da * t["a"]
            xt = _mm_xr(eye, xm, _NT)
            dgc = (dgam * gam - ddlt * dlt + jnp.sum(xm, axis=1, keepdims=True)
                   - jnp.sum(xt, axis=1, keepdims=True))
            dgl = jnp.sum(ddlt * dlt, axis=0, keepdims=True) + jnp.sum(
                jnp.sum(ds1 * s0, axis=1, keepdims=True), axis=0, keepdims=True) * jnp.max(egl, axis=0, keepdims=True)
            dgc = dgc + jnp.where(last_row, dgl, 0.0)
            dg_ref[:, sl] = _mm_xr(u_ge, dgc)

    rev = lambda w: pl.BlockSpec((CHUNK, w), lambda n: (N - 1 - n, 0))
    return pl.pallas_call(
        body, name="dn_bwd", grid=(N,),
        in_specs=[rev(3 * DN_W), rev(DN_W), rev(DN_W), rev(DN_W),
                  pl.BlockSpec((DN_HEADS, CHUNK, CHUNK), lambda n: (0, N - 1 - n, 0)),
                  pl.BlockSpec((1, DN_HEADS, DN_HD, DN_HD), lambda n: (N - 1 - n, 0, 0, 0)), rev(DN_W)],
        out_specs=[rev(3 * DN_W), rev(DN_W), rev(DN_W)],
        out_shape=[_sds((S, 3 * DN_W), F32), _sds((S, DN_W), F32), _sds((S, DN_W), F32)],
        scratch_shapes=[pltpu.VMEM((DN_HEADS, DN_HD, DN_HD), F32)],
        compiler_params=_cp(1),
    )(qkv, bb, gcb, glb, tinv_all, sall, do)


def _dn_prep_bwd_a(p, dqkv, dbb, dgb, conv_w, a_row, dtb_row, tm=256):
    S, PC = p.shape
    W3 = 3 * DN_W
    nhalo = tm // 8

    def body(x_ref, halo_ref, w_ref, ba_ref, a_ref, dtb_ref, dqkv_ref, dbb_ref, dgb_ref,
             dc_ref, dba_ref, dal_ref, ddt_ref):
        i = pl.program_id(0)

        @pl.when(i == 0)
        def _():
            dal_ref[...] = jnp.zeros_like(dal_ref)
            ddt_ref[...] = jnp.zeros_like(ddt_ref)

        halo = jnp.where(i > 0, halo_ref[...], 0.0)
        xf = jnp.concatenate([halo, x_ref[...]], axis=0)
        acc = jnp.zeros((tm, W3), F32)
        for k in range(CONV_K):
            sh = CONV_K - 1 - k
            xs = xf if sh == 0 else pltpu.roll(xf, sh, 0)
            acc = acc + xs[8:, :] * w_ref[k:k + 1, :]
        s = _silu(acc)
        ds_act = _dsilu(acc)
        for gi in range(2 * DN_HEADS):
            sl = slice(gi * LANES, (gi + 1) * LANES)
            sg = s[:, sl]
            rinv = lax.rsqrt(jnp.sum(sg * sg, axis=1, keepdims=True) + EPS)
            nh = sg * rinv
            dn = dqkv_ref[:, sl] * (DN_HD ** -0.5 if gi < DN_HEADS else 1.0)
            dsg = rinv * (dn - nh * jnp.sum(dn * nh, axis=1, keepdims=True))
            dc_ref[:, sl] = dsg * ds_act[:, sl]
        dc_ref[:, 2 * DN_W:] = dqkv_ref[:, 2 * DN_W:] * ds_act[:, 2 * DN_W:]

        ba = ba_ref[...]
        beta = _sigmoid(ba)
        ea = jnp.exp(a_ref[...])
        pre = ba + dtb_ref[...]
        g = -ea * _softplus(pre)
        lr, lc = _iota2((DN_W, LANES), 0), _iota2((DN_W, LANES), 1)
        pick_b = jnp.where(lc == lr // LANES, 1.0 / LANES, 0.0)
        pick_g = jnp.where(lc == lr // LANES + DN_HEADS, 1.0 / LANES, 0.0)
        dbeta = _mm_xl(dbb_ref[...], pick_b)
        dg = _mm_xl(dgb_ref[...], pick_g)
        lane = _iota2((1, LANES), 1)
        da = dg * (-ea) * _sigmoid(pre)
        dba_ref[...] = jnp.where(lane < DN_HEADS, dbeta * beta * (1.0 - beta),
                                 jnp.where(lane < 2 * DN_HEADS, da, 0.0)).astype(dba_ref.dtype)
        dal_ref[...] += jnp.sum(dg * g, axis=0, keepdims=True)
        ddt_ref[...] += jnp.sum(da, axis=0, keepdims=True)

    return pl.pallas_call(
        body, name="dn_prep_bwd_a", grid=(S // tm,),
        in_specs=[_rb(tm, W3, 0), pl.BlockSpec((8, W3), lambda i: (jnp.maximum(i * nhalo - 1, 0), 0)),
                  _fs((CONV_K, W3)), _rb(tm, LANES, (PC - LANES) // LANES), _fs((1, LANES)), _fs((1, LANES)),
                  _rb(tm, W3), _rb(tm, DN_W), _rb(tm, DN_W)],
        out_specs=[_rb(tm, W3), _rb(tm, LANES), _fs((1, LANES)), _fs((1, LANES))],
        out_shape=[_sds((S, W3), F32), _sds((S, LANES), _MXU_DTYPE), _sds((1, LANES), F32), _sds((1, LANES), F32)],
        compiler_params=_cp(1),
    )(p, p, conv_w, p, a_row, dtb_row, dqkv, dbb, dgb)


def _dn_prep_bwd_b(p, dc, conv_w, tm=256):
    S = p.shape[0]
    W3 = 3 * DN_W
    nhalo = tm // 8
    nblk = S // tm

    def body(x_ref, xh_ref, dc_ref, dch_ref, w_ref, dx_ref, dw_ref):
        i = pl.program_id(0)

        @pl.when(i == 0)
        def _():
            dw_ref[...] = jnp.zeros_like(dw_ref)

        dcv = dc_ref[...]
        xf = jnp.concatenate([jnp.where(i > 0, xh_ref[...], 0.0), x_ref[...]], axis=0)
        df = jnp.concatenate([dcv, jnp.where(i < nblk - 1, dch_ref[...], 0.0)], axis=0)
        acc = jnp.zeros((tm, W3), F32)
        for k in range(CONV_K):
            sh = CONV_K - 1 - k
            xs = xf if sh == 0 else pltpu.roll(xf, sh, 0)
            dw_ref[k:k + 1, :] += jnp.sum(dcv * xs[8:, :], axis=0, keepdims=True)
            ds = df if sh == 0 else pltpu.roll(df, tm + 8 - sh, 0)
            acc = acc + ds[:tm, :] * w_ref[k:k + 1, :]
        dx_ref[...] = acc.astype(dx_ref.dtype)

    return pl.pallas_call(
        body, name="dn_prep_bwd_b", grid=(nblk,),
        in_specs=[_rb(tm, W3, 0), pl.BlockSpec((8, W3), lambda i: (jnp.maximum(i * nhalo - 1, 0), 0)),
                  _rb(tm, W3), pl.BlockSpec((8, W3), lambda i: (jnp.minimum((i + 1) * nhalo, S // 8 - 1), 0)),
                  _fs((CONV_K, W3))],
        out_specs=[_rb(tm, W3), _fs((CONV_K, W3))],
        out_shape=[_sds((S, W3), _MXU_DTYPE), _sds((CONV_K, W3), F32)], compiler_params=_cp(1),
    )(p, p, dc, dc, conv_w)


def _gate(o_att, o_dn, p, gn, tm=256):
    S = p.shape[0]

    def body(oa_ref, zs_ref, od_ref, zd_ref, gn_ref, osb_ref, odn_ref):
        osb_ref[...] = (oa_ref[...] * _silu(zs_ref[...])).astype(osb_ref.dtype)
        for h in range(DN_HEADS):
            sl = slice(h * LANES, (h + 1) * LANES)
            o = od_ref[:, sl]
            r = lax.rsqrt(jnp.mean(o * o, axis=1, keepdims=True) + EPS)
            odn_ref[:, sl] = (o * r * gn_ref[...] * _silu(zd_ref[:, sl])).astype(odn_ref.dtype)

    return pl.pallas_call(
        body, name="gate", grid=(S // tm,),
        in_specs=[_rb(tm, SB_W), _rb(tm, SB_W, C_SB_Z // SB_W), _rb(tm, DN_W), _rb(tm, DN_W, C_DN_Z // DN_W),
                  _fs((1, LANES))],
        out_specs=[_rb(tm, SB_W), _rb(tm, DN_W)],
        out_shape=[_sds((S, SB_W), _MXU_DTYPE), _sds((S, DN_W), _MXU_DTYPE)], compiler_params=_cp(1),
    )(o_att, p, o_dn, p, gn)


def _gate_bwd(db_sb, db_dn, wb_sb, wb_dn, o_att, o_dn, p, gn, tm=256):
    S = p.shape[0]
    D = db_sb.shape[1]

    def body(dbs_ref, dbd_ref, ws_ref, wd_ref, oa_ref, zs_ref, od_ref, zd_ref, gn_ref,
             doa_ref, dzs_ref, dod_ref, dzd_ref, dgn_ref):
        @pl.when(pl.program_id(0) == 0)
        def _():
            dgn_ref[...] = jnp.zeros_like(dgn_ref)

        do_sb = _mm(dbs_ref[...], ws_ref[...], _NT)
        zs = zs_ref[...]
        doa_ref[...] = do_sb * _silu(zs)
        dzs_ref[...] = (do_sb * oa_ref[...] * _dsilu(zs)).astype(dzs_ref.dtype)
        do_dnn = _mm(dbd_ref[...], wd_ref[...], _NT)
        gnv = gn_ref[...]
        for h in range(DN_HEADS):
            sl = slice(h * LANES, (h + 1) * LANES)
            o, z, dout = od_ref[:, sl], zd_ref[:, sl], do_dnn[:, sl]
            r = lax.rsqrt(jnp.mean(o * o, axis=1, keepdims=True) + EPS)
            oh = o * r
            sz = _silu(z)
            dzd_ref[:, sl] = (dout * oh * gnv * _dsilu(z)).astype(dzd_ref.dtype)
            dgn_ref[...] += jnp.sum(dout * sz * oh, axis=0, keepdims=True)
            doh = dout * gnv * sz
            dod_ref[:, sl] = r * (doh - oh * jnp.mean(doh * oh, axis=1, keepdims=True))

    return pl.pallas_call(
        body, name="gate_bwd", grid=(S // tm,),
        in_specs=[_rb(tm, D), _rb(tm, D), _fs((SB_W, D)), _fs((DN_W, D)), _rb(tm, SB_W),
                  _rb(tm, SB_W, C_SB_Z // SB_W), _rb(tm, DN_W), _rb(tm, DN_W, C_DN_Z // DN_W), _fs((1, LANES))],
        out_specs=[_rb(tm, SB_W), _rb(tm, SB_W), _rb(tm, DN_W), _rb(tm, DN_W), _fs((1, LANES))],
        out_shape=[_sds((S, SB_W), F32), _sds((S, SB_W), _MXU_DTYPE), _sds((S, DN_W), F32),
                   _sds((S, DN_W), _MXU_DTYPE), _sds((1, LANES), F32)],
        compiler_params=_cp(1),
    )(db_sb, db_dn, wb_sb, wb_dn, o_att, p, o_dn, p, gn)


def _branch(o_sb, o_dnn, wb_sb, wb_dn, p, D, tm=256):
    S = p.shape[0]

    def body(os_ref, od_ref, ws_ref, wd_ref, ms_ref, md_ref, y_ref, bs_ref, bd_ref):
        bs = _mm(os_ref[...], ws_ref[...])
        bdn = _mm(od_ref[...], wd_ref[...])
        bs_ref[...] = bs
        bd_ref[...] = bdn
        y_ref[...] = (_sigmoid(ms_ref[...]) * bs + _sigmoid(md_ref[...]) * bdn).astype(y_ref.dtype)

    return pl.pallas_call(
        body, name="branch", grid=(S // tm,),
        in_specs=[_rb(tm, SB_W), _rb(tm, DN_W), _fs((SB_W, D)), _fs((DN_W, D)),
                  _rb(tm, D, C_MG // D), _rb(tm, D, C_MG // D + 1)],
        out_specs=[_rb(tm, D), _rb(tm, D), _rb(tm, D)],
        out_shape=[_sds((S, D), _MXU_DTYPE), _sds((S, D), F32), _sds((S, D), F32)], compiler_params=_cp(1),
    )(o_sb, o_dnn, wb_sb, wb_dn, p, p)


def _out_proj(x, y, w_out, gate, tm=256):
    S, D = x.shape

    def body(x_ref, y_ref, w_ref, g_ref, xn_ref, out_ref):
        out = _mm(y_ref[...], w_ref[...])
        out_ref[...] = out
        xn_ref[...] = x_ref[...] + g_ref[...] * out

    return pl.pallas_call(
        body, name="out_proj", grid=(S // tm,),
        in_specs=[_rb(tm, D), _rb(tm, D), _fs((D, D)), _fs((1, D))],
        out_specs=[_rb(tm, D), _rb(tm, D)],
        out_shape=[_sds((S, D), F32), _sds((S, D), F32)], compiler_params=_cp(1),
    )(x, y, w_out, gate)


def _out_bwd(dxn, out, gate, w_out, p, b_sb, b_dn, tm=256):
    S, D = dxn.shape

    def body(dxn_ref, out_ref, g_ref, w_ref, ms_ref, md_ref, bs_ref, bd_ref,
             dout_ref, dbs_ref, dbd_ref, dm_ref, dgate_ref):
        @pl.when(pl.program_id(0) == 0)
        def _():
            dgate_ref[...] = jnp.zeros_like(dgate_ref)

        dxv = dxn_ref[...]
        dgate_ref[...] += jnp.sum(dxv * out_ref[...], axis=0, keepdims=True)
        dout = (g_ref[...] * dxv).astype(dout_ref.dtype)
        dout_ref[...] = dout
        dy = _mm(dout, w_ref[...], _NT)
        s1, s2 = _sigmoid(ms_ref[...]), _sigmoid(md_ref[...])
        dbs_ref[...] = (dy * s1).astype(dbs_ref.dtype)
        dbd_ref[...] = (dy * s2).astype(dbd_ref.dtype)
        dm_ref[:, :D] = (dy * bs_ref[...] * s1 * (1.0 - s1)).astype(dm_ref.dtype)
        dm_ref[:, D:] = (dy * bd_ref[...] * s2 * (1.0 - s2)).astype(dm_ref.dtype)

    return pl.pallas_call(
        body, name="out_bwd", grid=(S // tm,),
        in_specs=[_rb(tm, D), _rb(tm, D), _fs((1, D)), _fs((D, D)), _rb(tm, D, C_MG // D),
                  _rb(tm, D, C_MG // D + 1), _rb(tm, D), _rb(tm, D)],
        out_specs=[_rb(tm, D), _rb(tm, D), _rb(tm, D), _rb(tm, 2 * D), _fs((1, D))],
        out_shape=[_sds((S, D), _MXU_DTYPE)] * 3 + [_sds((S, 2 * D), _MXU_DTYPE), _sds((1, D), F32)],
        compiler_params=_cp(1),
    )(dxn, out, gate, w_out, p, p, b_sb, b_dn)


def _loss_head(xf, target, tm=256):
    S, D = xf.shape

    def body(x_ref, t_ref, dy_ref, loss_ref):
        @pl.when(pl.program_id(0) == 0)
        def _():
            loss_ref[...] = jnp.zeros_like(loss_ref)

        e = x_ref[...] - t_ref[...]
        dy_ref[...] = e * (1.0 / D)
        row = jnp.sum(e * e, axis=1, keepdims=True) * (1.0 / D)
        loss_ref[...] += 0.5 * jnp.sum(row, axis=0, keepdims=True)

    return pl.pallas_call(
        body, name="loss_head", grid=(S // tm,),
        in_specs=[_rb(tm, D), _rb(tm, D)], out_specs=[_rb(tm, D), _fs((1, LANES))],
        out_shape=[_sds((S, D), F32), _sds((1, LANES), F32)], compiler_params=_cp(1),
    )(xf, target)


def _ada_fwd(c_all, ada_w, ada_b_sh):
    L, D, n = ada_w.shape
    B = c_all.shape[0]

    def body(c_ref, w_ref, b_ref, o_ref):
        sc = _silu(c_ref[...])
        o_ref[0] = _mm(sc, w_ref[0]) + b_ref[0]

    return pl.pallas_call(
        body, name="ada_fwd", grid=(L,),
        in_specs=[_fs((B, D)), pl.BlockSpec((1, D, n), lambda l: (l, 0, 0)), pl.BlockSpec((1, 1, n), lambda l: (l, 0, 0))],
        out_specs=pl.BlockSpec((1, B, n), lambda l: (l, 0, 0)),
        out_shape=_sds((L, B, n), F32), compiler_params=_cp(1),
    )(c_all, ada_w, ada_b_sh)


def _ada_bwd(c_all_t, dmod_sh):
    D, B = c_all_t.shape
    L, _, n = dmod_sh.shape

    def body(c_ref, d_ref, o_ref):
        acc = jnp.zeros((D, n), F32)
        for b in range(B):
            acc = acc + _silu(c_ref[:, b:b + 1]) * d_ref[0, b:b + 1, :]
        o_ref[0] = acc

    return pl.pallas_call(
        body, name="ada_bwd", grid=(L,),
        in_specs=[_fs((D, B)), pl.BlockSpec((1, B, n), lambda l: (l, 0, 0))],
        out_specs=pl.BlockSpec((1, D, n), lambda l: (l, 0, 0)),
        out_shape=_sds((L, D, n), F32), compiler_params=_cp(1),
    )(c_all_t, dmod_sh)


def _sum_parts(name, parts):
    P, R, C = parts.shape
    tr = _pick(R, max(16, min(512, (1 << 19) // (P * C))), 16) if R % 16 == 0 else R

    def body(p_ref, o_ref):
        acc = p_ref[0].astype(F32)
        for k in range(1, P):
            acc = acc + p_ref[k].astype(F32)
        o_ref[...] = acc

    return pl.pallas_call(
        body, name=name, grid=(R // tr,),
        in_specs=[pl.BlockSpec((P, tr, C), lambda i: (0, i, 0))], out_specs=_rb(tr, C),
        out_shape=_sds((R, C), F32), compiler_params=_cp(1),
    )(parts)


def _adamw(name, w, g, m, v):
    R, C = w.shape
    tr = _pick(R, 256, 8) if R % 8 == 0 else R
    c1 = 1.0 - ADAM_B1 ** ADAM_STEP
    c2 = 1.0 - ADAM_B2 ** ADAM_STEP

    def body(w_ref, g_ref, m_ref, v_ref, d_ref, mo_ref, vo_ref):
        gv = g_ref[...]
        mn = ADAM_B1 * m_ref[...] + (1.0 - ADAM_B1) * gv
        vn = ADAM_B2 * v_ref[...] + (1.0 - ADAM_B2) * (gv * gv)
        mo_ref[...] = mn
        vo_ref[...] = vn
        d_ref[...] = -ADAM_LR * ((mn / c1) / (jnp.sqrt(vn / c2) + ADAM_EPS) + ADAM_WD * w_ref[...])

    spec = _rb(tr, C)
    return pl.pallas_call(
        body, name=name, grid=(R // tr,),
        in_specs=[spec] * 4, out_specs=[spec] * 3, out_shape=[_sds((R, C), F32)] * 3, compiler_params=_cp(1),
    )(w, g, m, v)


def _ag_small(name, blk):
    R, C = blk.shape

    def body(x_ref, out_ref, send_sems, recv_sems, local_sem):
        x, y, c = lax.axis_index("x"), lax.axis_index("y"), lax.axis_index("c")
        me, sibling = (x, y, c), (x, y, 1 - c)
        chips = [(1 - x, y), (x, 1 - y), (1 - x, 1 - y)]

        def rows(px, py, pc):
            return out_ref.at[pl.ds((4 * px + 2 * py + pc) * R, R), :]

        def copy(k, block, to, src=None):
            return pltpu.make_async_remote_copy(
                src_ref=rows(*block) if src is None else src, dst_ref=rows(*block),
                send_sem=send_sems.at[k], recv_sem=recv_sems.at[k], device_id=to, device_id_type=MESH)

        mine = pltpu.make_async_copy(x_ref, rows(*me), local_sem)
        mine.start()
        first = [copy(0, me, sibling, src=x_ref)]
        first += [copy(1 + j, me, (*chip, c), src=x_ref) for j, chip in enumerate(chips)]
        for cp in first:
            cp.start()
        passed = [copy(4 + j, (*chip, c), sibling) for j, chip in enumerate(chips)]
        for j, chip in enumerate(chips):
            copy(1 + j, (*chip, c), me).wait_recv()
            passed[j].start()
        copy(0, sibling, me).wait_recv()
        for j, chip in enumerate(chips):
            copy(4 + j, (*chip, 1 - c), me).wait_recv()
        for cp in first + passed:
            cp.wait_send()
        mine.wait()

    return pl.pallas_call(
        body, name=name, out_shape=_sds((8 * R, C), blk.dtype),
        in_specs=[pl.BlockSpec(memory_space=pltpu.VMEM)], out_specs=pl.BlockSpec(memory_space=pltpu.VMEM),
        scratch_shapes=[pltpu.SemaphoreType.DMA((7,)), pltpu.SemaphoreType.DMA((7,)), pltpu.SemaphoreType.DMA],
    )(blk)


def _row_chunks(ts, row_axis):
    pieces = []
    for t, a in enumerate(ts):
        rows = a.shape[row_axis]
        n = 4 if rows >= 1024 else 1
        pieces += [(t, i * (rows // n), rows // n) for i in range(n)]
    return pieces


def _ag_weights(ts):
    nt = len(ts)
    pieces = _row_chunks(ts, 1)
    NP = len(pieces)

    def body(*refs):
        w, out = refs[:nt], refs[nt:2 * nt]
        send_sems, recv_sems, local_sems = refs[2 * nt:]
        x, y, c = lax.axis_index("x"), lax.axis_index("y"), lax.axis_index("c")
        me, sibling = (x, y, c), (x, y, 1 - c)
        mine = 2 * x + y
        chips = [(1 - x, y), (x, 1 - y), (1 - x, 1 - y)]

        def blk(t, shard, layer, r0, nr):
            return out[t].at[shard, layer, r0:r0 + nr, :]

        def copy(k, dst, to, src=None):
            return pltpu.make_async_remote_copy(
                src_ref=dst if src is None else src, dst_ref=dst, send_sem=send_sems.at[k], recv_sem=recv_sems.at[k],
                device_id=to, device_id_type=MESH)

        own = [pltpu.make_async_copy(w[t], out[t].at[mine], local_sems.at[t]) for t in range(nt)]
        for cp in own:
            cp.start()
        sent = []
        for j, chip in enumerate(chips):
            for pi, (t, r0, nr) in enumerate(pieces):
                sent.append(copy(j * NP + pi, blk(t, mine, c, r0, nr), (*chip, c), src=w[t].at[c, r0:r0 + nr, :]))
                sent[-1].start()
        for j, chip in enumerate(chips):
            theirs = 2 * chip[0] + chip[1]
            for pi, (t, r0, nr) in enumerate(pieces):
                copy(j * NP + pi, blk(t, theirs, c, r0, nr), me).wait_recv()
                sent.append(copy((3 + j) * NP + pi, blk(t, theirs, c, r0, nr), sibling))
                sent[-1].start()
        for j, chip in enumerate(chips):
            theirs = 2 * chip[0] + chip[1]
            for pi, (t, r0, nr) in enumerate(pieces):
                copy((3 + j) * NP + pi, blk(t, theirs, 1 - c, r0, nr), me).wait_recv()
        for cp in sent:
            cp.wait_send()
        for cp in own:
            cp.wait()

    hbm = pl.BlockSpec(memory_space=pl.ANY)
    return pl.pallas_call(
        body, name="ag_weights", out_shape=[_sds((4,) + a.shape, a.dtype) for a in ts],
        in_specs=[hbm] * nt, out_specs=[hbm] * nt,
        scratch_shapes=[pltpu.SemaphoreType.DMA((6 * NP,)), pltpu.SemaphoreType.DMA((6 * NP,)),
                        pltpu.SemaphoreType.DMA((nt,))],
    )(*ts)


def _grad_exchange(ts):
    nt = len(ts)
    pieces = _row_chunks(ts, 2)
    NP = len(pieces)

    def body(*refs):
        src, out = refs[:nt], refs[nt:2 * nt]
        send_sems, recv_sems, local_sems = refs[2 * nt:]
        x, y, c = lax.axis_index("x"), lax.axis_index("y"), lax.axis_index("c")
        me = 4 * x + 2 * y + c
        own = [pltpu.make_async_copy(src[t].at[c, 2 * x + y], out[t].at[me], local_sems.at[t]) for t in range(nt)]
        for cp in own:
            cp.start()
        sent, peers = [], []
        for k in range(1, 8):
            px = 1 - x if k & 4 else x
            py = 1 - y if k & 2 else y
            pc = 1 - c if k & 1 else c
            peers.append(4 * px + 2 * py + pc)
            for pi, (t, r0, nr) in enumerate(pieces):
                idx = (k - 1) * NP + pi
                sent.append(pltpu.make_async_remote_copy(
                    src_ref=src[t].at[pc, 2 * px + py, r0:r0 + nr, :], dst_ref=out[t].at[me, r0:r0 + nr, :],
                    send_sem=send_sems.at[idx], recv_sem=recv_sems.at[idx], device_id=(px, py, pc),
                    device_id_type=MESH))
                sent[-1].start()
        for k, peer in enumerate(peers):
            for pi, (t, r0, nr) in enumerate(pieces):
                idx = k * NP + pi
                pltpu.make_async_remote_copy(
                    src_ref=out[t].at[peer, r0:r0 + nr, :], dst_ref=out[t].at[peer, r0:r0 + nr, :],
                    send_sem=send_sems.at[idx], recv_sem=recv_sems.at[idx], device_id=(x, y, c),
                    device_id_type=MESH).wait_recv()
        for cp in sent:
            cp.wait_send()
        for cp in own:
            cp.wait()

    hbm = pl.BlockSpec(memory_space=pl.ANY)
    return pl.pallas_call(
        body, name="grad_exchange", out_shape=[_sds((8,) + a.shape[2:], a.dtype) for a in ts],
        in_specs=[hbm] * nt, out_specs=[hbm] * nt,
        scratch_shapes=[pltpu.SemaphoreType.DMA((7 * NP,)), pltpu.SemaphoreType.DMA((7 * NP,)),
                        pltpu.SemaphoreType.DMA((nt,))],
    )(*ts)


def _sibling_join(ts):
    nt = len(ts)
    pieces = _row_chunks(ts, 0)
    NP = len(pieces)

    def body(*refs):
        src, out = refs[:nt], refs[nt:2 * nt]
        send_sems, recv_sems, local_sems = refs[2 * nt:]
        x, y, c = lax.axis_index("x"), lax.axis_index("y"), lax.axis_index("c")
        own = [pltpu.make_async_copy(src[t], out[t].at[c], local_sems.at[t]) for t in range(nt)]
        for cp in own:
            cp.start()
        sent = []
        for pi, (t, r0, nr) in enumerate(pieces):
            sent.append(pltpu.make_async_remote_copy(
                src_ref=src[t].at[r0:r0 + nr, :], dst_ref=out[t].at[c, r0:r0 + nr, :], send_sem=send_sems.at[pi],
                recv_sem=recv_sems.at[pi], device_id=(x, y, 1 - c), device_id_type=MESH))
            sent[-1].start()
        for pi, (t, r0, nr) in enumerate(pieces):
            pltpu.make_async_remote_copy(
                src_ref=src[t].at[r0:r0 + nr, :], dst_ref=out[t].at[1 - c, r0:r0 + nr, :], send_sem=send_sems.at[pi],
                recv_sem=recv_sems.at[pi], device_id=(x, y, c), device_id_type=MESH).wait_recv()
        for cp in sent:
            cp.wait_send()
        for cp in own:
            cp.wait()

    hbm = pl.BlockSpec(memory_space=pl.ANY)
    return pl.pallas_call(
        body, name="sibling_join", out_shape=[_sds((2,) + a.shape, a.dtype) for a in ts],
        in_specs=[hbm] * nt, out_specs=[hbm] * nt,
        scratch_shapes=[pltpu.SemaphoreType.DMA((NP,)), pltpu.SemaphoreType.DMA((NP,)),
                        pltpu.SemaphoreType.DMA((nt,))],
    )(*ts)


def _layer_fwd(x, shift, scale, gate, lw):
    D = x.shape[1]
    h = _norm_mod(x, lw["norm_g"], scale, shift)
    p = _matmul("in_proj", h, lw["w_cat"], "nn", F32, tn_cap=896)
    qn, kn = _sb_prep(p, lw["gq_t"], lw["gk_t"])
    o_att, tot = _sb_fwd(qn, kn, p)
    qkv, bb, gcb, glb = _dn_prep(p, lw["conv_w"], lw["a_row"], lw["dtb_row"])
    o_dn, tinv, sall = _dn_fwd(qkv, bb, gcb, glb)
    o_sb, o_dnn = _gate(o_att, o_dn, p, lw["gn"])
    y, b_sb, b_dn = _branch(o_sb, o_dnn, lw["wb_sb"], lw["wb_dn"], p, D)
    x_next, out = _out_proj(x, y, lw["w_out"], gate)
    res = dict(x=x, h=h, p=p, qn=qn, kn=kn, o_att=o_att, tot=tot, qkv=qkv, bb=bb, gcb=gcb, glb=glb, o_dn=o_dn,
               tinv=tinv, sall=sall, o_sb=o_sb, o_dnn=o_dnn, y=y, b_sb=b_sb, b_dn=b_dn, out=out,
               shift=shift, scale=scale, gate=gate)
    return x_next, res


def _layer_bwd(dxn, res, lw):
    p = res["p"]
    dout, db_sb, db_dn, dm, dgate = _out_bwd(dxn, res["out"], res["gate"], lw["w_out"], p, res["b_sb"], res["b_dn"])
    dw_out = _matmul("dw_out", res["y"], dout, "tn", _MXU_DTYPE)
    dwb_sb = _matmul("dwb_sb", res["o_sb"], db_sb, "tn", _MXU_DTYPE)
    dwb_dn = _matmul("dwb_dn", res["o_dnn"], db_dn, "tn", _MXU_DTYPE)
    do_att, dz_sb, do_dn, dz_dn, dgn = _gate_bwd(db_sb, db_dn, lw["wb_sb"], lw["wb_dn"], res["o_att"], res["o_dn"],
                                                  p, lw["gn"])
    dqn, dkn, dv = _sb_bwd(res["qn"], res["kn"], p, do_att, res["tot"])
    dq_sb, dk_sb, dgq, dgk = _sb_prep_bwd(p, dqn, dkn, lw["gq_t"], lw["gk_t"])
    dqkv, dbb, dgb = _dn_bwd(res["qkv"], res["bb"], res["gcb"], res["glb"], res["tinv"], res["sall"], do_dn)
    dc, dp_ba, dal, ddt = _dn_prep_bwd_a(p, dqkv, dbb, dgb, lw["conv_w"], lw["a_row"], lw["dtb_row"])
    dp_dn, dconv = _dn_prep_bwd_b(p, dc, lw["conv_w"])
    dp = jnp.concatenate([dp_dn, dz_dn, dq_sb, dk_sb, dv.astype(_MXU_DTYPE), dz_sb, dm, dp_ba], axis=1)
    dh = _matmul("dh", dp, lw["w_cat"], "nt", F32, tk_cap=896)
    dw_cat = _matmul("dw_cat", res["h"], dp, "tn", _MXU_DTYPE, tm_cap=1024, tn_cap=896, tk_cap=512)
    dx, dshift, dscale, dnorm_g = _norm_mod_bwd(res["x"], dh, dxn, lw["norm_g"], res["scale"])
    small = dict(dmod=jnp.concatenate([dshift, dscale, dgate], axis=1)[0], norm_g=dnorm_g[0],
                 sb_q_g=dgq.reshape(SB_HEADS, SB_HD).sum(0), sb_k_g=dgk.reshape(SB_HEADS, SB_HD).sum(0),
                 conv_w=dconv, dn_a_log=dal[0, DN_HEADS:2 * DN_HEADS], dn_dt_bias=ddt[0, DN_HEADS:2 * DN_HEADS],
                 dn_norm_g=dgn[0])
    big = dict(w_cat=dw_cat, w_branch_sb=dwb_sb, w_branch_dn=dwb_dn, w_out=dw_out)
    return dx, small, big


def _cat_cols(w, D):
    return jnp.concatenate([w[:, 2048:4096], w[:, 0:2048], w[:, 4104:4104 + 2 * D], w[:, 4096:4104],
                            jnp.zeros((w.shape[0], LANES - 8), w.dtype)], axis=1)


def _uncat_cols(g, D):
    return jnp.concatenate([g[:, 2048:4096], g[:, 0:2048], g[:, 4096 + 2 * D:4096 + 2 * D + 8],
                            g[:, 4096:4096 + 2 * D]], axis=1)


def _flat_pack(arrs, mult):
    flat = jnp.concatenate([a.reshape(-1) for a in arrs])
    n = flat.shape[0]
    pad = (-n) % mult
    if pad:
        flat = jnp.concatenate([flat, jnp.zeros((pad,), flat.dtype)])
    return flat.reshape(-1, LANES)


def _flat_unpack(flat, shapes):
    flat = flat.reshape(-1)
    out, off = [], 0
    for s in shapes:
        n = math.prod(s)
        out.append(flat[off:off + n].reshape(s))
        off += n
    return out


BIG = ("w_in", "w_branch_sb", "w_branch_dn", "w_out")
SMALL = ("ada_b", "norm_g", "sb_q_g", "sb_k_g", "conv_w", "dn_a_log", "dn_dt_bias", "dn_norm_g")


def kernel(x, c, ada_w, ada_b, norm_g, w_in, sb_q_g, sb_k_g, conv_w, dn_a_log, dn_dt_bias, dn_norm_g, w_branch_sb, w_branch_dn, w_out, loss_target, m_ada_w, m_ada_b, m_norm_g, m_w_in, m_sb_q_g, m_sb_k_g, m_conv_w, m_dn_a_log, m_dn_dt_bias, m_dn_norm_g, m_w_branch_sb, m_w_branch_dn, m_w_out, v_ada_w, v_ada_b, v_norm_g, v_w_in, v_sb_q_g, v_sb_k_g, v_conv_w, v_dn_a_log, v_dn_dt_bias, v_dn_norm_g, v_w_branch_sb, v_w_branch_dn, v_w_out):
    W = dict(ada_w=ada_w, ada_b=ada_b, norm_g=norm_g, w_in=w_in, sb_q_g=sb_q_g, sb_k_g=sb_k_g, conv_w=conv_w,
             dn_a_log=dn_a_log, dn_dt_bias=dn_dt_bias, dn_norm_g=dn_norm_g, w_branch_sb=w_branch_sb,
             w_branch_dn=w_branch_dn, w_out=w_out)
    M = dict(ada_w=m_ada_w, ada_b=m_ada_b, norm_g=m_norm_g, w_in=m_w_in, sb_q_g=m_sb_q_g, sb_k_g=m_sb_k_g,
             conv_w=m_conv_w, dn_a_log=m_dn_a_log, dn_dt_bias=m_dn_dt_bias, dn_norm_g=m_dn_norm_g,
             w_branch_sb=m_w_branch_sb, w_branch_dn=m_w_branch_dn, w_out=m_w_out)
    V = dict(ada_w=v_ada_w, ada_b=v_ada_b, norm_g=v_norm_g, w_in=v_w_in, sb_q_g=v_sb_q_g, sb_k_g=v_sb_k_g,
             conv_w=v_conv_w, dn_a_log=v_dn_a_log, dn_dt_bias=v_dn_dt_bias, dn_norm_g=v_dn_norm_g,
             w_branch_sb=v_w_branch_sb, w_branch_dn=v_w_branch_dn, w_out=v_w_out)
    L = ada_w.shape[0]
    S, D = x.shape[1], x.shape[2]
    ix, iy, ic = lax.axis_index("x"), lax.axis_index("y"), lax.axis_index("c")
    shard = 2 * ix + iy
    me = 2 * shard + ic
    n_ada = ada_w.shape[2]
    n_in = w_in.shape[2]
    n_conv = conv_w.shape[2]
    n_br = w_branch_sb.shape[2]
    n_out = w_out.shape[1]

    g_in, g_bs, g_bd, g_out = _ag_weights([W[n].astype(_MXU_DTYPE) for n in BIG])
    w_in_f = jnp.concatenate([g_in[s] for s in range(4)], axis=2)
    wb_sb_f = jnp.concatenate([g_bs[s] for s in range(4)], axis=2)
    wb_dn_f = jnp.concatenate([g_bd[s] for s in range(4)], axis=2)
    w_out_f = jnp.concatenate([g_out[s] for s in range(4)], axis=1)

    g1 = _ag_small("ag_c_conv", _flat_pack([c, conv_w], LANES * 8))
    g1 = g1.reshape(8, -1)
    c_all = g1[:, :D]
    conv_parts = g1[:, D:D + L * CONV_K * n_conv].reshape(4, 2, L, CONV_K, n_conv)[:, 0]
    conv_full = jnp.concatenate([conv_parts[s] for s in range(4)], axis=2)
    ada_b_sh = lax.dynamic_slice_in_dim(ada_b, shard * n_ada, n_ada, axis=1)[:, None, :]
    mod_sh = _ada_fwd(c_all, ada_w, ada_b_sh)
    g2 = _ag_small("ag_mod", _flat_pack([mod_sh], LANES * 8)).reshape(8, -1)
    mod_parts = g2[:, :L * 8 * n_ada].reshape(4, 2, L, 8, n_ada)[:, 0]
    mod_all = jnp.concatenate([mod_parts[s] for s in range(4)], axis=2)
    mod = lax.dynamic_index_in_dim(mod_all, me, axis=1, keepdims=False)

    def layer_weights(l):
        pad_lo = jnp.zeros((DN_HEADS,), F32)
        pad_hi = jnp.zeros((LANES - 2 * DN_HEADS,), F32)
        return dict(
            norm_g=norm_g[l][None, :], w_cat=_cat_cols(w_in_f[l], D),
            gq_t=jnp.tile(sb_q_g[l], SB_HEADS)[None, :], gk_t=jnp.tile(sb_k_g[l], SB_HEADS)[None, :],
            conv_w=conv_full[l],
            a_row=jnp.concatenate([pad_lo, dn_a_log[l], pad_hi])[None, :],
            dtb_row=jnp.concatenate([pad_lo, dn_dt_bias[l], pad_hi])[None, :],
            gn=dn_norm_g[l][None, :], wb_sb=wb_sb_f[l], wb_dn=wb_dn_f[l], w_out=w_out_f[l])

    xs = x[0]
    lws, ress = [], []
    for l in range(L):
        lw = layer_weights(l)
        xs, res = _layer_fwd(xs, mod[l, None, 0:D], mod[l, None, D:2 * D], mod[l, None, 2 * D:3 * D], lw)
        lws.append(lw)
        ress.append(res)
    dxs, loss_row = _loss_head(xs, loss_target[0])
    loss = lax.psum(loss_row[0, 0], ("x", "y", "c"))
    smalls, bigs = [None] * L, [None] * L
    for l in reversed(range(L)):
        dxs, smalls[l], bigs[l] = _layer_bwd(dxs, ress[l], lws[l])
    grad_x = dxs[None]

    small_names = ("dmod",) + SMALL[1:]
    small_pack = _flat_pack([jnp.stack([smalls[l][n] for l in range(L)]) for n in small_names], LANES * 8)
    g3 = _ag_small("ag_small_grads", small_pack)
    R3 = small_pack.shape[0]
    g3 = g3.reshape(8, R3, LANES)
    small_sum = _sum_parts("sum_small", g3)
    small_shapes = [(L, 3 * D), (L, D), (L, SB_HD), (L, SB_HD), (L, CONV_K, 3 * DN_W), (L, DN_HEADS), (L, DN_HEADS),
                    (L, DN_HD)]
    sg = dict(zip(small_names, _flat_unpack(small_sum, small_shapes)))
    G = dict(ada_b=sg["dmod"], norm_g=sg["norm_g"], sb_q_g=sg["sb_q_g"], sb_k_g=sg["sb_k_g"],
             conv_w=lax.dynamic_slice_in_dim(sg["conv_w"], shard * n_conv, n_conv, axis=2),
             dn_a_log=sg["dn_a_log"], dn_dt_bias=sg["dn_dt_bias"], dn_norm_g=sg["dn_norm_g"])
    dmod_all = g3.reshape(8, -1)[:, :L * 3 * D].reshape(8, L, 3 * D)
    dmod_sh = lax.dynamic_slice_in_dim(dmod_all, shard * n_ada, n_ada, axis=2).transpose(1, 0, 2)
    G["ada_w"] = _ada_bwd(c_all.T, dmod_sh)

    def by_col_shard(name, n_sh):
        g = jnp.stack([bigs[l][name] for l in range(L)])
        return g.reshape(L, g.shape[1], 4, n_sh).transpose(0, 2, 1, 3)

    send = [jnp.stack([_uncat_cols(bigs[l]["w_cat"], D) for l in range(L)]).reshape(L, D, 4, n_in).transpose(0, 2, 1, 3),
            by_col_shard("w_branch_sb", n_br), by_col_shard("w_branch_dn", n_br),
            jnp.stack([bigs[l]["w_out"] for l in range(L)]).reshape(L, 4, n_out, D)]
    got = _grad_exchange(send)
    mine = [_sum_parts("sum_" + n, g) for n, g in zip(BIG, got)]
    for n, g in zip(BIG, _sibling_join(mine)):
        G[n] = g

    delta, new_m, new_v = {}, {}, {}
    for n in ("ada_w",) + BIG:
        sh = W[n].shape
        two = (sh[0] * sh[1], sh[2])
        d, mo, vo = _adamw("adamw_" + n, W[n].reshape(two), G[n].reshape(two), M[n].reshape(two), V[n].reshape(two))
        delta[n], new_m[n], new_v[n] = d.reshape(sh), mo.reshape(sh), vo.reshape(sh)
    sm_shapes = [W[n].shape for n in SMALL]
    d, mo, vo = _adamw("adamw_small", _flat_pack([W[n] for n in SMALL], LANES * 8),
                       _flat_pack([G[n] for n in SMALL], LANES * 8), _flat_pack([M[n] for n in SMALL], LANES * 8),
                       _flat_pack([V[n] for n in SMALL], LANES * 8))
    for n, dd, mm, vv in zip(SMALL, _flat_unpack(d, sm_shapes), _flat_unpack(mo, sm_shapes),
                             _flat_unpack(vo, sm_shapes)):
        delta[n], new_m[n], new_v[n] = dd, mm, vv

    order = ("ada_w", "ada_b", "norm_g", "w_in", "sb_q_g", "sb_k_g", "conv_w", "dn_a_log", "dn_dt_bias", "dn_norm_g",
             "w_branch_sb", "w_branch_dn", "w_out")
    return (loss, grad_x, *[G[n] for n in order], *[delta[n] for n in order], *[new_m[n] for n in order],
            *[new_v[n] for n in order])
```

```python
import math

import jax
import jax.numpy as jnp
from jax import lax
from jax.experimental import pallas as pl
from jax.experimental.pallas import tpu as pltpu

F32 = jnp.float32
BF16 = jnp.bfloat16
_MXU_DTYPE = BF16
_VMEM_LIMIT = 48 * 1024 * 1024
LANES = 128

EPS = 1e-6
SB_HEADS, SB_HD, SB_W = 8, 64, 512
DN_HEADS, DN_HD, DN_W = 4, 128, 512
CONV_K = 4
CHUNK = 64
QB = 256
_SB_DEAD = 104.0
ADAM_LR, ADAM_B1, ADAM_B2, ADAM_EPS, ADAM_WD, ADAM_STEP = 0.001, 0.9, 0.999, 1e-08, 0.01, 10

C_DN_QKV, C_DN_Z, C_SB_Q, C_SB_K, C_SB_V, C_SB_Z, C_MG = 0, 1536, 2048, 2560, 3072, 3584, 4096

_NN = (((1,), (0,)), ((), ()))
_NT = (((1,), (1,)), ((), ()))
_TN = (((0,), (0,)), ((), ()))
MESH = pl.DeviceIdType.MESH


def _sds(shape, dtype):
    return jax.ShapeDtypeStruct(shape, dtype)


def _cp(n):
    return pltpu.CompilerParams(dimension_semantics=("arbitrary",) * n, vmem_limit_bytes=_VMEM_LIMIT)


def _rb(tm, w, cb=0):
    return pl.BlockSpec((tm, w), lambda i: (i, cb))


def _fs(shape):
    nd = len(shape)
    return pl.BlockSpec(shape, lambda i: (0,) * nd)


def _dg(a, b, dims):
    return lax.dot_general(a, b, dims, preferred_element_type=F32)


def _mm(a, b, dims=_NN):
    return _dg(a.astype(_MXU_DTYPE), b.astype(_MXU_DTYPE), dims)


def _split3(x):
    hi = x.astype(BF16)
    r = x - hi.astype(F32)
    mid = r.astype(BF16)
    lo = (r - mid.astype(F32)).astype(BF16)
    return hi, mid, lo


def _mm_xl(x, const, dims=_NN):
    cb = const.astype(BF16)
    hi, mid, lo = _split3(x)
    return _dg(hi, cb, dims) + _dg(mid, cb, dims) + _dg(lo, cb, dims)


def _mm_xl2(x, const, dims=_NN):
    cb = const.astype(BF16)
    hi = x.astype(BF16)
    lo = (x - hi.astype(F32)).astype(BF16)
    return _dg(hi, cb, dims) + _dg(lo, cb, dims)


def _mm_xr(const, x, dims=_NN):
    cb = const.astype(BF16)
    hi, mid, lo = _split3(x)
    return _dg(cb, hi, dims) + _dg(cb, mid, dims) + _dg(cb, lo, dims)


def _mm3(a, b, dims=_NN):
    ah, am, _ = _split3(a)
    bh, bm, _ = _split3(b)
    return _dg(ah, bh, dims) + (_dg(ah, bm, dims) + _dg(am, bh, dims))


def _sigmoid(z):
    return 1.0 / (1.0 + jnp.exp(-z))


def _silu(z):
    return z * _sigmoid(z)


def _dsilu(z):
    s = _sigmoid(z)
    return s * (1.0 + z * (1.0 - s))


def _softplus(z):
    return jnp.maximum(z, 0.0) + jnp.log(1.0 + jnp.exp(-jnp.abs(z)))


def _iota2(shape, dim):
    return lax.broadcasted_iota(jnp.int32, shape, dim)


def _pick(n, cap, mult):
    best = None
    for t in range(mult, min(n, cap) + 1, mult):
        if n % t == 0:
            best = t
    assert best is not None, (n, cap, mult)
    return best


def _matmul(name, a, b, form, out_dtype, tm_cap=512, tn_cap=1024, tk_cap=1024):
    if form == "nn":
        (M, K), (_, N) = a.shape, b.shape
    elif form == "nt":
        (M, K), (N, _) = a.shape, b.shape
    else:
        (K, M), (_, N) = a.shape, b.shape
    tm = _pick(M, tm_cap, 128 if form == "tn" else 8)
    tn = _pick(N, tn_cap, 128)
    tk = _pick(K, tk_cap, 128)
    nk = K // tk
    dims = {"nn": _NN, "nt": _NT, "tn": _TN}[form]
    if form == "nn":
        a_spec = pl.BlockSpec((tm, tk), lambda i, j, k: (i, k))
        b_spec = pl.BlockSpec((tk, tn), lambda i, j, k: (k, j))
    elif form == "nt":
        a_spec = pl.BlockSpec((tm, tk), lambda i, j, k: (i, k))
        b_spec = pl.BlockSpec((tn, tk), lambda i, j, k: (j, k))
    else:
        a_spec = pl.BlockSpec((tk, tm), lambda i, j, k: (k, i))
        b_spec = pl.BlockSpec((tk, tn), lambda i, j, k: (k, j))

    def body(a_ref, b_ref, o_ref, acc_ref):
        k = pl.program_id(2)

        @pl.when(k == 0)
        def _():
            acc_ref[...] = jnp.zeros_like(acc_ref)

        acc_ref[...] += _mm(a_ref[...], b_ref[...], dims)

        @pl.when(k == nk - 1)
        def _():
            o_ref[...] = acc_ref[...].astype(o_ref.dtype)

    return pl.pallas_call(
        body, name=name, grid=(M // tm, N // tn, nk),
        in_specs=[a_spec, b_spec],
        out_specs=pl.BlockSpec((tm, tn), lambda i, j, k: (i, j)),
        out_shape=_sds((M, N), out_dtype),
        scratch_shapes=[pltpu.VMEM((tm, tn), F32)],
        compiler_params=_cp(3),
    )(a, b)


def _norm_mod(x, g, scale, shift, tm=256):
    S, D = x.shape

    def body(x_ref, g_ref, sc_ref, sh_ref, h_ref):
        xv = x_ref[...]
        r = lax.rsqrt(jnp.mean(xv * xv, axis=1, keepdims=True) + EPS)
        h_ref[...] = ((xv * r * g_ref[...]) * (1.0 + sc_ref[...]) + sh_ref[...]).astype(h_ref.dtype)

    return pl.pallas_call(
        body, name="norm_mod", grid=(S // tm,),
        in_specs=[_rb(tm, D), _fs((1, D)), _fs((1, D)), _fs((1, D))],
        out_specs=_rb(tm, D), out_shape=_sds((S, D), _MXU_DTYPE), compiler_params=_cp(1),
    )(x, g, scale, shift)


def _norm_mod_bwd(x, dh, dxn, g, scale, tm=256):
    S, D = x.shape

    def body(x_ref, dh_ref, dxn_ref, g_ref, sc_ref, dx_ref, dsh_ref, dsc_ref, dg_ref):
        @pl.when(pl.program_id(0) == 0)
        def _():
            dsh_ref[...] = jnp.zeros_like(dsh_ref)
            dsc_ref[...] = jnp.zeros_like(dsc_ref)
            dg_ref[...] = jnp.zeros_like(dg_ref)

        xv, dhv, gv = x_ref[...], dh_ref[...], g_ref[...]
        r = lax.rsqrt(jnp.mean(xv * xv, axis=1, keepdims=True) + EPS)
        xh = xv * r
        one_sc = 1.0 + sc_ref[...]
        dsh_ref[...] += jnp.sum(dhv, axis=0, keepdims=True)
        dsc_ref[...] += jnp.sum(dhv * xh * gv, axis=0, keepdims=True)
        dg_ref[...] += jnp.sum(dhv * one_sc * xh, axis=0, keepdims=True)
        dxh = dhv * (gv * one_sc)
        dx_ref[...] = r * (dxh - xh * jnp.mean(dxh * xh, axis=1, keepdims=True)) + dxn_ref[...]

    return pl.pallas_call(
        body, name="norm_mod_bwd", grid=(S // tm,),
        in_specs=[_rb(tm, D), _rb(tm, D), _rb(tm, D), _fs((1, D)), _fs((1, D))],
        out_specs=[_rb(tm, D), _fs((1, D)), _fs((1, D)), _fs((1, D))],
        out_shape=[_sds((S, D), F32)] + [_sds((1, D), F32)] * 3, compiler_params=_cp(1),
    )(x, dh, dxn, g, scale)


def _head_sum_matrix():
    r = jnp.arange(SB_W)
    return (r[:, None] // SB_HD == r[None, :] // SB_HD).astype(BF16)


def _sb_prep(p, gq_t, gk_t, tm=256):
    S = p.shape[0]
    bd = _head_sum_matrix()

    def body(q_ref, k_ref, gq_ref, gk_ref, bd_ref, qn_ref, kn_ref):
        for src, g_ref, dst in ((q_ref, gq_ref, qn_ref), (k_ref, gk_ref, kn_ref)):
            v = src[...]
            ms = _mm_xl(v * v, bd_ref[...]) * (1.0 / SB_HD)
            dst[...] = (v * lax.rsqrt(ms + EPS) * g_ref[...]).astype(dst.dtype)

    return pl.pallas_call(
        body, name="sb_prep", grid=(S // tm,),
        in_specs=[_rb(tm, SB_W, C_SB_Q // SB_W), _rb(tm, SB_W, C_SB_K // SB_W),
                  _fs((1, SB_W)), _fs((1, SB_W)), _fs((SB_W, SB_W))],
        out_specs=[_rb(tm, SB_W), _rb(tm, SB_W)],
        out_shape=[_sds((S, SB_W), _MXU_DTYPE)] * 2, compiler_params=_cp(1),
    )(p, p, gq_t, gk_t, bd)


def _sb_prep_bwd(p, dqn, dkn, gq_t, gk_t, tm=256):
    S = p.shape[0]
    bd = _head_sum_matrix()

    def body(q_ref, k_ref, dqn_ref, dkn_ref, gq_ref, gk_ref, bd_ref, dq_ref, dk_ref, dgq_ref, dgk_ref):
        @pl.when(pl.program_id(0) == 0)
        def _():
            dgq_ref[...] = jnp.zeros_like(dgq_ref)
            dgk_ref[...] = jnp.zeros_like(dgk_ref)

        for src, dn_ref, g_ref, dst, dg_ref in ((q_ref, dqn_ref, gq_ref, dq_ref, dgq_ref),
                                                (k_ref, dkn_ref, gk_ref, dk_ref, dgk_ref)):
            v, dn = src[...], dn_ref[...]
            r = lax.rsqrt(_mm_xl(v * v, bd_ref[...]) * (1.0 / SB_HD) + EPS)
            vh = v * r
            dg_ref[...] += jnp.sum(dn * vh, axis=0, keepdims=True)
            dvh = dn * g_ref[...]
            m = _mm_xl(dvh * vh, bd_ref[...]) * (1.0 / SB_HD)
            dst[...] = (r * (dvh - vh * m)).astype(dst.dtype)

    return pl.pallas_call(
        body, name="sb_prep_bwd", grid=(S // tm,),
        in_specs=[_rb(tm, SB_W, C_SB_Q // SB_W), _rb(tm, SB_W, C_SB_K // SB_W), _rb(tm, SB_W), _rb(tm, SB_W),
                  _fs((1, SB_W)), _fs((1, SB_W)), _fs((SB_W, SB_W))],
        out_specs=[_rb(tm, SB_W), _rb(tm, SB_W), _fs((1, SB_W)), _fs((1, SB_W))],
        out_shape=[_sds((S, SB_W), _MXU_DTYPE)] * 2 + [_sds((1, SB_W), F32)] * 2, compiler_params=_cp(1),
    )(p, p, dqn, dkn, gq_t, gk_t, bd)


def _sb_consts():
    r, c = _iota2((QB, QB), 0), _iota2((QB, QB), 1)
    lane = _iota2((1, LANES), 1)
    return r, c, lane


def _sb_fwd(qn, kn, p):
    S = qn.shape[0]
    scale = 1.0 / math.sqrt(SB_HD)

    def body(q_ref, k_ref, v_ref, o_ref):
        i = pl.program_id(1)
        r, c, lane = _sb_consts()
        u_gt = (r > c).astype(BF16)
        strict = c < r
        q = q_ref[...]
        masks = [(lane // SB_HD) == h for h in range(2)]
        qhs = [jnp.where(m, q, jnp.zeros_like(q)) for m in masks]

        def block(off, carry, diagonal):
            kj = k_ref[pl.ds(off, QB), :]
            vj = v_ref[pl.ds(off, QB), :].astype(_MXU_DTYPE)
            out = []
            for h in range(2):
                o_h, run = carry[2 * h], carry[2 * h + 1]
                z = _mm(qhs[h], kj, _NT) * scale
                sp = _softplus(z)
                sp_m = jnp.where(strict, sp, 0.0) if diagonal else sp
                later = _mm_xl2(sp_m, u_gt)
                w = jnp.exp((z - sp) - later - run)
                if diagonal:
                    w = jnp.where(strict, w, 0.0)
                out += [o_h + _mm(w, vj), run + jnp.sum(sp_m, axis=1, keepdims=True)]
            return tuple(out)

        init = (jnp.zeros((QB, LANES), F32), jnp.zeros((QB, 1), F32)) * 2
        carry = block(pl.multiple_of(i * QB, QB), init, True)
        st = lax.while_loop(
            lambda st: jnp.logical_and(st[0] <= i, jnp.minimum(jnp.min(st[2]), jnp.min(st[4])) < _SB_DEAD),
            lambda st: (st[0] + 1,) + block(pl.multiple_of((i - st[0]) * QB, QB), st[1:], False),
            (jnp.int32(1),) + carry)
        o_ref[...] = jnp.where(masks[0], st[1], st[3])

    return pl.pallas_call(
        body, name="sb_fwd", grid=(SB_W // LANES, S // QB),
        in_specs=[pl.BlockSpec((QB, LANES), lambda hp, i: (i, hp)),
                  pl.BlockSpec((S, LANES), lambda hp, i: (0, hp)),
                  pl.BlockSpec((S, LANES), lambda hp, i: (0, C_SB_V // LANES + hp))],
        out_specs=pl.BlockSpec((QB, LANES), lambda hp, i: (i, hp)),
        out_shape=_sds((S, SB_W), F32), compiler_params=_cp(2),
    )(qn, kn, p)


def _sb_bwd(qn, kn, p, do):
    S = qn.shape[0]
    scale = 1.0 / math.sqrt(SB_HD)

    def body(q_ref, k_ref, v_ref, do_ref, dq_ref, dk_ref, dv_ref):
        i = pl.program_id(1)

        @pl.when(i == 0)
        def _():
            dk_ref[...] = jnp.zeros_like(dk_ref)
            dv_ref[...] = jnp.zeros_like(dv_ref)

        r, c, lane = _sb_consts()
        u_le = (r <= c).astype(BF16)
        u_lt = (r < c).astype(BF16)
        strict = c < r
        q = q_ref[...]
        do = do_ref[...].astype(_MXU_DTYPE)
        masks = [(lane // SB_HD) == h for h in range(2)]
        qhs = [jnp.where(m, q, jnp.zeros_like(q)) for m in masks]
        dohs = [jnp.where(m, do, jnp.zeros_like(do)) for m in masks]

        def sums(off, runs, diagonal):
            kj = k_ref[pl.ds(off, QB), :]
            out = []
            for h in range(2):
                sp = _softplus(_mm(qhs[h], kj, _NT) * scale)
                if diagonal:
                    sp = jnp.where(strict, sp, 0.0)
                out.append(runs[h] + jnp.sum(sp, axis=1, keepdims=True))
            return tuple(out)

        runs = sums(pl.multiple_of(i * QB, QB), (jnp.zeros((QB, 1), F32),) * 2, True)
        st = lax.while_loop(
            lambda st: jnp.logical_and(st[0] <= i, jnp.minimum(jnp.min(st[1]), jnp.min(st[2])) < _SB_DEAD),
            lambda st: (st[0] + 1,) + sums(pl.multiple_of((i - st[0]) * QB, QB), st[1:], False),
            (jnp.int32(1),) + runs)
        first = i + 1 - st[0]
        tots = st[1:]

        def block(off, carry, diagonal):
            kj = k_ref[pl.ds(off, QB), :]
            vj = v_ref[pl.ds(off, QB), :].astype(_MXU_DTYPE)
            out, dk_blk, dv_blk = [], None, None
            for h in range(2):
                dq_h, pre_sp, pre_e = carry[3 * h:3 * h + 3]
                z = _mm(qhs[h], kj, _NT) * scale
                sp = _softplus(z)
                a = z - sp
                sp_m = jnp.where(strict, sp, 0.0) if diagonal else sp
                incl = _mm_xl2(sp_m, u_le)
                w = jnp.exp(a - ((tots[h] - pre_sp) - incl))
                if diagonal:
                    w = jnp.where(strict, w, 0.0)
                e = w * _mm(dohs[h], vj, _NT)
                db = pre_e + _mm_xl2(e, u_lt)
                dz = (e - jnp.exp(a) * (e + db)) * scale
                if diagonal:
                    dz = jnp.where(strict, dz, 0.0)
                dkh = jnp.where(masks[h], _mm(dz, q, _TN), 0.0)
                dvh = jnp.where(masks[h], _mm(w, do, _TN), 0.0)
                dk_blk = dkh if dk_blk is None else dk_blk + dkh
                dv_blk = dvh if dv_blk is None else dv_blk + dvh
                out += [dq_h + _mm(dz, kj), pre_sp + jnp.sum(sp_m, axis=1, keepdims=True),
                        pre_e + jnp.sum(e, axis=1, keepdims=True)]
            dk_ref[pl.ds(off, QB), :] += dk_blk
            dv_ref[pl.ds(off, QB), :] += dv_blk
            return tuple(out)

        zero_col = jnp.zeros((QB, 1), F32)
        init = (jnp.zeros((QB, LANES), F32), zero_col, zero_col) * 2
        carry = lax.fori_loop(first, i, lambda j, cr: block(pl.multiple_of(j * QB, QB), cr, False), init)
        carry = block(pl.multiple_of(i * QB, QB), carry, True)
        dq_ref[...] = jnp.where(masks[0], carry[0], carry[3])

    blk = pl.BlockSpec((QB, LANES), lambda hp, i: (i, hp))
    full = pl.BlockSpec((S, LANES), lambda hp, i: (0, hp))
    return pl.pallas_call(
        body, name="sb_bwd", grid=(SB_W // LANES, S // QB),
        in_specs=[blk, full, pl.BlockSpec((S, LANES), lambda hp, i: (0, C_SB_V // LANES + hp)), blk],
        out_specs=[blk, full, full],
        out_shape=[_sds((S, SB_W), F32)] * 3, compiler_params=_cp(2),
    )(qn, kn, p, do)


def _dn_prep(p, conv_w, a_row, dtb_row, tm=256):
    S = p.shape[0]
    W3 = 3 * DN_W
    nhalo = tm // 8

    def body(x_ref, halo_ref, w_ref, ba_ref, a_ref, dtb_ref, qkv_ref, bb_ref, gc_ref, gl_ref):
        i = pl.program_id(0)
        halo = jnp.where(i > 0, halo_ref[...], 0.0)
        xf = jnp.concatenate([halo, x_ref[...]], axis=0)
        acc = jnp.zeros((tm, W3), F32)
        for k in range(CONV_K):
            sh = CONV_K - 1 - k
            xs = xf if sh == 0 else pltpu.roll(xf, sh, 0)
            acc = acc + xs[8:, :] * w_ref[k:k + 1, :]
        s = _silu(acc)
        for gi in range(2 * DN_HEADS):
            sl = slice(gi * LANES, (gi + 1) * LANES)
            sg = s[:, sl]
            rinv = lax.rsqrt(jnp.sum(sg * sg, axis=1, keepdims=True) + EPS)
            qkv_ref[:, sl] = sg * rinv * (DN_HD ** -0.5 if gi < DN_HEADS else 1.0)
        qkv_ref[:, 2 * DN_W:] = s[:, 2 * DN_W:]

        ba = ba_ref[...]
        beta = _sigmoid(ba)
        g = -jnp.exp(a_ref[...]) * _softplus(ba + dtb_ref[...])
        lr, lc = _iota2((LANES, DN_W), 0), _iota2((LANES, DN_W), 1)
        sel_b = (lr == lc // LANES).astype(BF16)
        sel_g = (lr == lc // LANES + DN_HEADS).astype(BF16)
        bb_ref[...] = _mm_xl(beta, sel_b)
        graw = _mm_xl(g, sel_g)
        rr, cc = _iota2((tm, tm), 0), _iota2((tm, tm), 1)
        tri = jnp.logical_and(rr >= cc, rr // CHUNK == cc // CHUNK).astype(BF16)
        gc = _mm_xr(tri, graw)
        last = (cc == (rr // CHUNK) * CHUNK + (CHUNK - 1)).astype(BF16)
        gc_ref[...] = gc
        gl_ref[...] = _mm_xr(last, gc)

    return pl.pallas_call(
        body, name="dn_prep", grid=(S // tm,),
        in_specs=[_rb(tm, W3, 0), pl.BlockSpec((8, W3), lambda i: (jnp.maximum(i * nhalo - 1, 0), 0)),
                  _fs((CONV_K, W3)), _rb(tm, LANES, (p.shape[1] - LANES) // LANES),
                  _fs((1, LANES)), _fs((1, LANES))],
        out_specs=[_rb(tm, W3), _rb(tm, DN_W), _rb(tm, DN_W), _rb(tm, DN_W)],
        out_shape=[_sds((S, W3), F32)] + [_sds((S, DN_W), F32)] * 3, compiler_params=_cp(1),
    )(p, p, conv_w, p, a_row, dtb_row)


def _dn_chunk_terms(q, k, v, beta, gc, gl):
    r, c = _iota2((CHUNK, CHUNK), 0), _iota2((CHUNK, CHUNK), 1)
    tril, strict = r >= c, r > c
    gcol = _mm_xl(gc, jnp.full((LANES, CHUNK), 1.0 / LANES, F32))
    grow = _mm_xr(jnp.full((CHUNK, LANES), 1.0 / LANES, F32), gc, _NT)
    dec = jnp.where(tril, jnp.exp(jnp.where(tril, gcol - grow, 0.0)), 0.0)
    gam = jnp.exp(gc)
    dlt = jnp.exp(gl - gc)
    kb, vb = k * beta, v * beta
    pm = _mm(kb, k, _NT)
    qk = _mm(q, k, _NT)
    m = jnp.where(strict, pm * dec, 0.0)
    a = jnp.where(tril, qk * dec, 0.0)
    return dict(tril=tril, strict=strict, dec=dec, gam=gam, dlt=dlt, kb=kb, vb=vb, m=m, a=a)


def _dn_fwd(qkv, bb, gcb, glb):
    S = qkv.shape[0]
    N = S // CHUNK

    def body(qkv_ref, bb_ref, gc_ref, gl_ref, o_ref, t_ref, sall_ref, s_scr):
        @pl.when(pl.program_id(0) == 0)
        def _():
            s_scr[...] = jnp.zeros_like(s_scr)

        r, c = _iota2((CHUNK, CHUNK), 0), _iota2((CHUNK, CHUNK), 1)
        eye = (r == c).astype(F32)
        for h in range(DN_HEADS):
            sl = slice(h * LANES, (h + 1) * LANES)
            q, k = qkv_ref[:, sl], qkv_ref[:, DN_W + h * LANES:DN_W + (h + 1) * LANES]
            v = qkv_ref[:, 2 * DN_W + h * LANES:2 * DN_W + (h + 1) * LANES]
            beta, gc, gl = bb_ref[:, sl], gc_ref[:, sl], gl_ref[:, sl]
            s0 = s_scr[h]
            sall_ref[0, h] = s0.astype(sall_ref.dtype)
            s0 = s0.astype(sall_ref.dtype).astype(F32)
            t = _dn_chunk_terms(q, k, v, beta, gc, gl)
            pw = -t["m"]
            tinv = eye + pw
            for _ in range(5):
                pw = _mm3(pw, pw)
                tinv = tinv + _mm3(tinv, pw)
            t_ref[h] = tinv
            u = _mm3(tinv, t["vb"])
            w = _mm3(tinv, t["kb"] * t["gam"])
            vn = u - _mm(w, s0)
            o_ref[:, sl] = _mm(q * t["gam"], s0) + _mm(t["a"], vn)
            egl = jnp.exp(jnp.concatenate([gl, gl], axis=0))
            s_scr[h] = s_scr[h] * egl + _mm(k * t["dlt"], vn, _TN)

    return pl.pallas_call(
        body, name="dn_fwd", grid=(N,),
        in_specs=[_rb(CHUNK, 3 * DN_W), _rb(CHUNK, DN_W), _rb(CHUNK, DN_W), _rb(CHUNK, DN_W)],
        out_specs=[_rb(CHUNK, DN_W), pl.BlockSpec((DN_HEADS, CHUNK, CHUNK), lambda n: (0, n, 0)),
                   pl.BlockSpec((1, DN_HEADS, DN_HD, DN_HD), lambda n: (n, 0, 0, 0))],
        out_shape=[_sds((S, DN_W), F32), _sds((DN_HEADS, S, CHUNK), F32),
                   _sds((N, DN_HEADS, DN_HD, DN_HD), _MXU_DTYPE)],
        scratch_shapes=[pltpu.VMEM((DN_HEADS, DN_HD, DN_HD), F32)],
        compiler_params=_cp(1),
    )(qkv, bb, gcb, glb)


def _dn_bwd(qkv, bb, gcb, glb, tinv_all, sall, do):
    S = qkv.shape[0]
    N = S // CHUNK

    def body(qkv_ref, bb_ref, gc_ref, gl_ref, t_ref, sall_ref, do_ref, dqkv_ref, dbb_ref, dg_ref, ds_scr):
        @pl.when(pl.program_id(0) == 0)
        def _():
            ds_scr[...] = jnp.zeros_like(ds_scr)

        r, c = _iota2((CHUNK, CHUNK), 0), _iota2((CHUNK, CHUNK), 1)
        eye = (r == c).astype(F32)
        u_ge = (c >= r).astype(F32)
        last_row = _iota2((CHUNK, LANES), 0) == CHUNK - 1
        for h in range(DN_HEADS):
            sl = slice(h * LANES, (h + 1) * LANES)
            slk = slice(DN_W + h * LANES, DN_W + (h + 1) * LANES)
            slv = slice(2 * DN_W + h * LANES, 2 * DN_W + (h + 1) * LANES)
            q, k, v = qkv_ref[:, sl], qkv_ref[:, slk], qkv_ref[:, slv]
            beta, gc, gl = bb_ref[:, sl], gc_ref[:, sl], gl_ref[:, sl]
            tinv = t_ref[h]
            s0 = sall_ref[0, h].astype(F32)
            do = do_ref[:, sl]
            ds1 = ds_scr[h]
            t = _dn_chunk_terms(q, k, v, beta, gc, gl)
            gam, dlt, kb, vb, dec = t["gam"], t["dlt"], t["kb"], t["vb"], t["dec"]
            kbg = kb * gam
            u = _mm3(tinv, vb)
            w = _mm3(tinv, kbg)
            vn = u - _mm(w, s0)
            qg, kd = q * gam, k * dlt
            egl = jnp.exp(gl)
            egl2 = jnp.concatenate([egl, egl], axis=0)

            dvn = _mm(t["a"], do, _TN) + _mm(kd, ds1)
            da = jnp.where(t["tril"], _mm(do, vn, _NT), 0.0)
            dqg = _mm(do, s0, _NT)
            dkd = _mm(vn, ds1, _NT)
            dw = -_mm(dvn, s0, _NT)
            ds_scr[h] = _mm(qg, do, _TN) + egl2 * ds1 - _mm(w, dvn, _TN)
            tt = _mm_xr(eye, tinv, _NT)
            dvb = _mm3(tt, dvn)
            dkbg = _mm3(tt, dw)
            dm = -jnp.where(t["strict"], _mm(dvb, u, _NT) + _mm(dkbg, w, _NT), 0.0)
            dpm = dm * dec
            dqk = da * dec
            dkb = dkbg * gam + _mm(dpm, k)
            dk = dkd * dlt + _mm(dpm, kb, _TN) + _mm(dqk, q, _TN) + dkb * beta
            dq = dqg * gam + _mm(dqk, k)
            dqkv_ref[:, sl] = dq
            dqkv_ref[:, slk] = dk
            dqkv_ref[:, slv] = dvb * beta
            dbb_ref[:, sl] = jnp.broadcast_to(
                jnp.sum(dkb * k, axis=1, keepdims=True) + jnp.sum(dvb * v, axis=1, keepdims=True), (CHUNK, LANES))
            dgam = jnp.sum(dqg * q, axis=1, keepdims=True) + jnp.sum(dkbg * kb, axis=1, keepdims=True)
            ddlt = jnp.sum(dkd * k, axis=1, keepdims=True)
            xm = dm * t["m"] + da * t["a"]
            xt = _mm_xr(eye, xm, _NT)
            dgc = (dgam * gam - ddlt * dlt + jnp.sum(xm, axis=1, keepdims=True)
                   - jnp.sum(xt, axis=1, keepdims=True))
            dgl = jnp.sum(ddlt * dlt, axis=0, keepdims=True) + jnp.sum(
                jnp.sum(ds1 * s0, axis=1, keepdims=True), axis=0, keepdims=True) * jnp.max(egl, axis=0, keepdims=True)
            dgc = dgc + jnp.where(last_row, dgl, 0.0)
            dg_ref[:, sl] = _mm_xr(u_ge, dgc)

    rev = lambda w: pl.BlockSpec((CHUNK, w), lambda n: (N - 1 - n, 0))
    return pl.pallas_call(
        body, name="dn_bwd", grid=(N,),
        in_specs=[rev(3 * DN_W), rev(DN_W), rev(DN_W), rev(DN_W),
                  pl.BlockSpec((DN_HEADS, CHUNK, CHUNK), lambda n: (0, N - 1 - n, 0)),
                  pl.BlockSpec((1, DN_HEADS, DN_HD, DN_HD), lambda n: (N - 1 - n, 0, 0, 0)), rev(DN_W)],
        out_specs=[rev(3 * DN_W), rev(DN_W), rev(DN_W)],
        out_shape=[_sds((S, 3 * DN_W), F32), _sds((S, DN_W), F32), _sds((S, DN_W), F32)],
        scratch_shapes=[pltpu.VMEM((DN_HEADS, DN_HD, DN_HD), F32)],
        compiler_params=_cp(1),
    )(qkv, bb, gcb, glb, tinv_all, sall, do)


def _dn_prep_bwd_a(p, dqkv, dbb, dgb, conv_w, a_row, dtb_row, tm=256):
    S, PC = p.shape
    W3 = 3 * DN_W
    nhalo = tm // 8

    def body(x_ref, halo_ref, w_ref, ba_ref, a_ref, dtb_ref, dqkv_ref, dbb_ref, dgb_ref,
             dc_ref, dba_ref, dal_ref, ddt_ref):
        i = pl.program_id(0)

        @pl.when(i == 0)
        def _():
            dal_ref[...] = jnp.zeros_like(dal_ref)
            ddt_ref[...] = jnp.zeros_like(ddt_ref)

        halo = jnp.where(i > 0, halo_ref[...], 0.0)
        xf = jnp.concatenate([halo, x_ref[...]], axis=0)
        acc = jnp.zeros((tm, W3), F32)
        for k in range(CONV_K):
            sh = CONV_K - 1 - k
            xs = xf if sh == 0 else pltpu.roll(xf, sh, 0)
            acc = acc + xs[8:, :] * w_ref[k:k + 1, :]
        s = _silu(acc)
        ds_act = _dsilu(acc)
        for gi in range(2 * DN_HEADS):
            sl = slice(gi * LANES, (gi + 1) * LANES)
            sg = s[:, sl]
            rinv = lax.rsqrt(jnp.sum(sg * sg, axis=1, keepdims=True) + EPS)
            nh = sg * rinv
            dn = dqkv_ref[:, sl] * (DN_HD ** -0.5 if gi < DN_HEADS else 1.0)
            dsg = rinv * (dn - nh * jnp.sum(dn * nh, axis=1, keepdims=True))
            dc_ref[:, sl] = dsg * ds_act[:, sl]
        dc_ref[:, 2 * DN_W:] = dqkv_ref[:, 2 * DN_W:] * ds_act[:, 2 * DN_W:]

        ba = ba_ref[...]
        beta = _sigmoid(ba)
        ea = jnp.exp(a_ref[...])
        pre = ba + dtb_ref[...]
        g = -ea * _softplus(pre)
        lr, lc = _iota2((DN_W, LANES), 0), _iota2((DN_W, LANES), 1)
        pick_b = jnp.where(lc == lr // LANES, 1.0 / LANES, 0.0)
        pick_g = jnp.where(lc == lr // LANES + DN_HEADS, 1.0 / LANES, 0.0)
        dbeta = _mm_xl(dbb_ref[...], pick_b)
        dg = _mm_xl(dgb_ref[...], pick_g)
        lane = _iota2((1, LANES), 1)
        da = dg * (-ea) * _sigmoid(pre)
        dba_ref[...] = jnp.where(lane < DN_HEADS, dbeta * beta * (1.0 - beta),
                                 jnp.where(lane < 2 * DN_HEADS, da, 0.0)).astype(dba_ref.dtype)
        dal_ref[...] += jnp.sum(dg * g, axis=0, keepdims=True)
        ddt_ref[...] += jnp.sum(da, axis=0, keepdims=True)

    return pl.pallas_call(
        body, name="dn_prep_bwd_a", grid=(S // tm,),
        in_specs=[_rb(tm, W3, 0), pl.BlockSpec((8, W3), lambda i: (jnp.maximum(i * nhalo - 1, 0), 0)),
                  _fs((CONV_K, W3)), _rb(tm, LANES, (PC - LANES) // LANES), _fs((1, LANES)), _fs((1, LANES)),
                  _rb(tm, W3), _rb(tm, DN_W), _rb(tm, DN_W)],
        out_specs=[_rb(tm, W3), _rb(tm, LANES), _fs((1, LANES)), _fs((1, LANES))],
        out_shape=[_sds((S, W3), F32), _sds((S, LANES), _MXU_DTYPE), _sds((1, LANES), F32), _sds((1, LANES), F32)],
        compiler_params=_cp(1),
    )(p, p, conv_w, p, a_row, dtb_row, dqkv, dbb, dgb)


def _dn_prep_bwd_b(p, dc, conv_w, tm=256):
    S = p.shape[0]
    W3 = 3 * DN_W
    nhalo = tm // 8
    nblk = S // tm

    def body(x_ref, xh_ref, dc_ref, dch_ref, w_ref, dx_ref, dw_ref):
        i = pl.program_id(0)

        @pl.when(i == 0)
        def _():
            dw_ref[...] = jnp.zeros_like(dw_ref)

        dcv = dc_ref[...]
        xf = jnp.concatenate([jnp.where(i > 0, xh_ref[...], 0.0), x_ref[...]], axis=0)
        df = jnp.concatenate([dcv, jnp.where(i < nblk - 1, dch_ref[...], 0.0)], axis=0)
        acc = jnp.zeros((tm, W3), F32)
        for k in range(CONV_K):
            sh = CONV_K - 1 - k
            xs = xf if sh == 0 else pltpu.roll(xf, sh, 0)
            dw_ref[k:k + 1, :] += jnp.sum(dcv * xs[8:, :], axis=0, keepdims=True)
            ds = df if sh == 0 else pltpu.roll(df, tm + 8 - sh, 0)
            acc = acc + ds[:tm, :] * w_ref[k:k + 1, :]
        dx_ref[...] = acc.astype(dx_ref.dtype)

    return pl.pallas_call(
        body, name="dn_prep_bwd_b", grid=(nblk,),
        in_specs=[_rb(tm, W3, 0), pl.BlockSpec((8, W3), lambda i: (jnp.maximum(i * nhalo - 1, 0), 0)),
                  _rb(tm, W3), pl.BlockSpec((8, W3), lambda i: (jnp.minimum((i + 1) * nhalo, S // 8 - 1), 0)),
                  _fs((CONV_K, W3))],
        out_specs=[_rb(tm, W3), _fs((CONV_K, W3))],
        out_shape=[_sds((S, W3), _MXU_DTYPE), _sds((CONV_K, W3), F32)], compiler_params=_cp(1),
    )(p, p, dc, dc, conv_w)


def _gate(o_att, o_dn, p, gn, tm=256):
    S = p.shape[0]

    def body(oa_ref, zs_ref, od_ref, zd_ref, gn_ref, osb_ref, odn_ref):
        osb_ref[...] = (oa_ref[...] * _silu(zs_ref[...])).astype(osb_ref.dtype)
        for h in range(DN_HEADS):
            sl = slice(h * LANES, (h + 1) * LANES)
            o = od_ref[:, sl]
            r = lax.rsqrt(jnp.mean(o * o, axis=1, keepdims=True) + EPS)
            odn_ref[:, sl] = (o * r * gn_ref[...] * _silu(zd_ref[:, sl])).astype(odn_ref.dtype)

    return pl.pallas_call(
        body, name="gate", grid=(S // tm,),
        in_specs=[_rb(tm, SB_W), _rb(tm, SB_W, C_SB_Z // SB_W), _rb(tm, DN_W), _rb(tm, DN_W, C_DN_Z // DN_W),
                  _fs((1, LANES))],
        out_specs=[_rb(tm, SB_W), _rb(tm, DN_W)],
        out_shape=[_sds((S, SB_W), _MXU_DTYPE), _sds((S, DN_W), _MXU_DTYPE)], compiler_params=_cp(1),
    )(o_att, p, o_dn, p, gn)


def _gate_bwd(db_sb, db_dn, wb_sb, wb_dn, o_att, o_dn, p, gn, tm=256):
    S = p.shape[0]
    D = db_sb.shape[1]

    def body(dbs_ref, dbd_ref, ws_ref, wd_ref, oa_ref, zs_ref, od_ref, zd_ref, gn_ref,
             doa_ref, dzs_ref, dod_ref, dzd_ref, dgn_ref):
        @pl.when(pl.program_id(0) == 0)
        def _():
            dgn_ref[...] = jnp.zeros_like(dgn_ref)

        do_sb = _mm(dbs_ref[...], ws_ref[...], _NT)
        zs = zs_ref[...]
        doa_ref[...] = do_sb * _silu(zs)
        dzs_ref[...] = (do_sb * oa_ref[...] * _dsilu(zs)).astype(dzs_ref.dtype)
        do_dnn = _mm(dbd_ref[...], wd_ref[...], _NT)
        gnv = gn_ref[...]
        for h in range(DN_HEADS):
            sl = slice(h * LANES, (h + 1) * LANES)
            o, z, dout = od_ref[:, sl], zd_ref[:, sl], do_dnn[:, sl]
            r = lax.rsqrt(jnp.mean(o * o, axis=1, keepdims=True) + EPS)
            oh = o * r
            sz = _silu(z)
            dzd_ref[:, sl] = (dout * oh * gnv * _dsilu(z)).astype(dzd_ref.dtype)
            dgn_ref[...] += jnp.sum(dout * sz * oh, axis=0, keepdims=True)
            doh = dout * gnv * sz
            dod_ref[:, sl] = r * (doh - oh * jnp.mean(doh * oh, axis=1, keepdims=True))

    return pl.pallas_call(
        body, name="gate_bwd", grid=(S // tm,),
        in_specs=[_rb(tm, D), _rb(tm, D), _fs((SB_W, D)), _fs((DN_W, D)), _rb(tm, SB_W),
                  _rb(tm, SB_W, C_SB_Z // SB_W), _rb(tm, DN_W), _rb(tm, DN_W, C_DN_Z // DN_W), _fs((1, LANES))],
        out_specs=[_rb(tm, SB_W), _rb(tm, SB_W), _rb(tm, DN_W), _rb(tm, DN_W), _fs((1, LANES))],
        out_shape=[_sds((S, SB_W), F32), _sds((S, SB_W), _MXU_DTYPE), _sds((S, DN_W), F32),
                   _sds((S, DN_W), _MXU_DTYPE), _sds((1, LANES), F32)],
        compiler_params=_cp(1),
    )(db_sb, db_dn, wb_sb, wb_dn, o_att, p, o_dn, p, gn)


def _branch(o_sb, o_dnn, wb_sb, wb_dn, p, D, tm=256):
    S = p.shape[0]

    def body(os_ref, od_ref, ws_ref, wd_ref, ms_ref, md_ref, y_ref, bs_ref, bd_ref):
        bs = _mm(os_ref[...], ws_ref[...])
        bdn = _mm(od_ref[...], wd_ref[...])
        bs_ref[...] = bs
        bd_ref[...] = bdn
        y_ref[...] = (_sigmoid(ms_ref[...]) * bs + _sigmoid(md_ref[...]) * bdn).astype(y_ref.dtype)

    return pl.pallas_call(
        body, name="branch", grid=(S // tm,),
        in_specs=[_rb(tm, SB_W), _rb(tm, DN_W), _fs((SB_W, D)), _fs((DN_W, D)),
                  _rb(tm, D, C_MG // D), _rb(tm, D, C_MG // D + 1)],
        out_specs=[_rb(tm, D), _rb(tm, D), _rb(tm, D)],
        out_shape=[_sds((S, D), _MXU_DTYPE), _sds((S, D), F32), _sds((S, D), F32)], compiler_params=_cp(1),
    )(o_sb, o_dnn, wb_sb, wb_dn, p, p)


def _out_proj(x, y, w_out, gate, tm=256):
    S, D = x.shape

    def body(x_ref, y_ref, w_ref, g_ref, xn_ref, out_ref):
        out = _mm(y_ref[...], w_ref[...])
        out_ref[...] = out
        xn_ref[...] = x_ref[...] + g_ref[...] * out

    return pl.pallas_call(
        body, name="out_proj", grid=(S // tm,),
        in_specs=[_rb(tm, D), _rb(tm, D), _fs((D, D)), _fs((1, D))],
        out_specs=[_rb(tm, D), _rb(tm, D)],
        out_shape=[_sds((S, D), F32), _sds((S, D), F32)], compiler_params=_cp(1),
    )(x, y, w_out, gate)


def _out_bwd(dxn, out, gate, w_out, p, b_sb, b_dn, tm=256):
    S, D = dxn.shape

    def body(dxn_ref, out_ref, g_ref, w_ref, ms_ref, md_ref, bs_ref, bd_ref,
             dout_ref, dbs_ref, dbd_ref, dm_ref, dgate_ref):
        @pl.when(pl.program_id(0) == 0)
        def _():
            dgate_ref[...] = jnp.zeros_like(dgate_ref)

        dxv = dxn_ref[...]
        dgate_ref[...] += jnp.sum(dxv * out_ref[...], axis=0, keepdims=True)
        dout = (g_ref[...] * dxv).astype(dout_ref.dtype)
        dout_ref[...] = dout
        dy = _mm(dout, w_ref[...], _NT)
        s1, s2 = _sigmoid(ms_ref[...]), _sigmoid(md_ref[...])
        dbs_ref[...] = (dy * s1).astype(dbs_ref.dtype)
        dbd_ref[...] = (dy * s2).astype(dbd_ref.dtype)
        dm_ref[:, :D] = (dy * bs_ref[...] * s1 * (1.0 - s1)).astype(dm_ref.dtype)
        dm_ref[:, D:] = (dy * bd_ref[...] * s2 * (1.0 - s2)).astype(dm_ref.dtype)

    return pl.pallas_call(
        body, name="out_bwd", grid=(S // tm,),
        in_specs=[_rb(tm, D), _rb(tm, D), _fs((1, D)), _fs((D, D)), _rb(tm, D, C_MG // D),
                  _rb(tm, D, C_MG // D + 1), _rb(tm, D), _rb(tm, D)],
        out_specs=[_rb(tm, D), _rb(tm, D), _rb(tm, D), _rb(tm, 2 * D), _fs((1, D))],
        out_shape=[_sds((S, D), _MXU_DTYPE)] * 3 + [_sds((S, 2 * D), _MXU_DTYPE), _sds((1, D), F32)],
        compiler_params=_cp(1),
    )(dxn, out, gate, w_out, p, p, b_sb, b_dn)


def _loss_head(xf, target, tm=256):
    S, D = xf.shape

    def body(x_ref, t_ref, dy_ref, loss_ref):
        @pl.when(pl.program_id(0) == 0)
        def _():
            loss_ref[...] = jnp.zeros_like(loss_ref)

        e = x_ref[...] - t_ref[...]
        dy_ref[...] = e * (1.0 / D)
        row = jnp.sum(e * e, axis=1, keepdims=True) * (1.0 / D)
        loss_ref[...] += 0.5 * jnp.sum(row, axis=0, keepdims=True)

    return pl.pallas_call(
        body, name="loss_head", grid=(S // tm,),
        in_specs=[_rb(tm, D), _rb(tm, D)], out_specs=[_rb(tm, D), _fs((1, LANES))],
        out_shape=[_sds((S, D), F32), _sds((1, LANES), F32)], compiler_params=_cp(1),
    )(xf, target)


def _ada_fwd(c_all, ada_w, ada_b_sh):
    L, D, n = ada_w.shape
    B = c_all.shape[0]

    def body(c_ref, w_ref, b_ref, o_ref):
        sc = _silu(c_ref[...])
        o_ref[0] = _mm(sc, w_ref[0]) + b_ref[0]

    return pl.pallas_call(
        body, name="ada_fwd", grid=(L,),
        in_specs=[_fs((B, D)), pl.BlockSpec((1, D, n), lambda l: (l, 0, 0)), pl.BlockSpec((1, 1, n), lambda l: (l, 0, 0))],
        out_specs=pl.BlockSpec((1, B, n), lambda l: (l, 0, 0)),
        out_shape=_sds((L, B, n), F32), compiler_params=_cp(1),
    )(c_all, ada_w, ada_b_sh)


def _ada_bwd(c_all_t, dmod_sh):
    D, B = c_all_t.shape
    L, _, n = dmod_sh.shape

    def body(c_ref, d_ref, o_ref):
        acc = jnp.zeros((D, n), F32)
        for b in range(B):
            acc = acc + _silu(c_ref[:, b:b + 1]) * d_ref[0, b:b + 1, :]
        o_ref[0] = acc

    return pl.pallas_call(
        body, name="ada_bwd", grid=(L,),
        in_specs=[_fs((D, B)), pl.BlockSpec((1, B, n), lambda l: (l, 0, 0))],
        out_specs=pl.BlockSpec((1, D, n), lambda l: (l, 0, 0)),
        out_shape=_sds((L, D, n), F32), compiler_params=_cp(1),
    )(c_all_t, dmod_sh)


def _sum_parts(name, parts):
    P, R, C = parts.shape
    tr = _pick(R, max(16, min(512, (1 << 19) // (P * C))), 16) if R % 16 == 0 else R

    def body(p_ref, o_ref):
        acc = p_ref[0].astype(F32)
        for k in range(1, P):
            acc = acc + p_ref[k].astype(F32)
        o_ref[...] = acc

    return pl.pallas_call(
        body, name=name, grid=(R // tr,),
        in_specs=[pl.BlockSpec((P, tr, C), lambda i: (0, i, 0))], out_specs=_rb(tr, C),
        out_shape=_sds((R, C), F32), compiler_params=_cp(1),
    )(parts)


def _adamw(name, w, g, m, v):
    R, C = w.shape
    tr = _pick(R, 256, 8) if R % 8 == 0 else R
    c1 = 1.0 - ADAM_B1 ** ADAM_STEP
    c2 = 1.0 - ADAM_B2 ** ADAM_STEP

    def body(w_ref, g_ref, m_ref, v_ref, d_ref, mo_ref, vo_ref):
        gv = g_ref[...]
        mn = ADAM_B1 * m_ref[...] + (1.0 - ADAM_B1) * gv
        vn = ADAM_B2 * v_ref[...] + (1.0 - ADAM_B2) * (gv * gv)
        mo_ref[...] = mn
        vo_ref[...] = vn
        d_ref[...] = -ADAM_LR * ((mn / c1) / (jnp.sqrt(vn / c2) + ADAM_EPS) + ADAM_WD * w_ref[...])

    spec = _rb(tr, C)
    return pl.pallas_call(
        body, name=name, grid=(R // tr,),
        in_specs=[spec] * 4, out_specs=[spec] * 3, out_shape=[_sds((R, C), F32)] * 3, compiler_params=_cp(1),
    )(w, g, m, v)


def _ag_small(name, blk):
    R, C = blk.shape

    def body(x_ref, out_ref, send_sems, recv_sems, local_sem):
        x, y, c = lax.axis_index("x"), lax.axis_index("y"), lax.axis_index("c")
        me, sibling = (x, y, c), (x, y, 1 - c)
        chips = [(1 - x, y), (x, 1 - y), (1 - x, 1 - y)]

        def rows(px, py, pc):
            return out_ref.at[pl.ds((4 * px + 2 * py + pc) * R, R), :]

        def copy(k, block, to, src=None):
            return pltpu.make_async_remote_copy(
                src_ref=rows(*block) if src is None else src, dst_ref=rows(*block),
                send_sem=send_sems.at[k], recv_sem=recv_sems.at[k], device_id=to, device_id_type=MESH)

        mine = pltpu.make_async_copy(x_ref, rows(*me), local_sem)
        mine.start()
        first = [copy(0, me, sibling, src=x_ref)]
        first += [copy(1 + j, me, (*chip, c), src=x_ref) for j, chip in enumerate(chips)]
        for cp in first:
            cp.start()
        passed = [copy(4 + j, (*chip, c), sibling) for j, chip in enumerate(chips)]
        for j, chip in enumerate(chips):
            copy(1 + j, (*chip, c), me).wait_recv()
            passed[j].start()
        copy(0, sibling, me).wait_recv()
        for j, chip in enumerate(chips):
            copy(4 + j, (*chip, 1 - c), me).wait_recv()
        for cp in first + passed:
            cp.wait_send()
        mine.wait()

    return pl.pallas_call(
        body, name=name, out_shape=_sds((8 * R, C), blk.dtype),
        in_specs=[pl.BlockSpec(memory_space=pltpu.VMEM)], out_specs=pl.BlockSpec(memory_space=pltpu.VMEM),
        scratch_shapes=[pltpu.SemaphoreType.DMA((7,)), pltpu.SemaphoreType.DMA((7,)), pltpu.SemaphoreType.DMA],
    )(blk)


def _row_chunks(ts, row_axis):
    pieces = []
    for t, a in enumerate(ts):
        rows = a.shape[row_axis]
        n = 4 if rows >= 1024 else 1
        pieces += [(t, i * (rows // n), rows // n) for i in range(n)]
    return pieces


def _ag_weights(ts):
    nt = len(ts)
    pieces = _row_chunks(ts, 1)
    NP = len(pieces)

    def body(*refs):
        w, out = refs[:nt], refs[nt:2 * nt]
        send_sems, recv_sems, local_sems = refs[2 * nt:]
        x, y, c = lax.axis_index("x"), lax.axis_index("y"), lax.axis_index("c")
        me, sibling = (x, y, c), (x, y, 1 - c)
        mine = 2 * x + y
        chips = [(1 - x, y), (x, 1 - y), (1 - x, 1 - y)]

        def blk(t, shard, layer, r0, nr):
            return out[t].at[shard, layer, r0:r0 + nr, :]

        def copy(k, dst, to, src=None):
            return pltpu.make_async_remote_copy(
                src_ref=dst if src is None else src, dst_ref=dst, send_sem=send_sems.at[k], recv_sem=recv_sems.at[k],
                device_id=to, device_id_type=MESH)

        own = [pltpu.make_async_copy(w[t], out[t].at[mine], local_sems.at[t]) for t in range(nt)]
        for cp in own:
            cp.start()
        sent = []
        for j, chip in enumerate(chips):
            for pi, (t, r0, nr) in enumerate(pieces):
                sent.append(copy(j * NP + pi, blk(t, mine, c, r0, nr), (*chip, c), src=w[t].at[c, r0:r0 + nr, :]))
                sent[-1].start()
        for j, chip in enumerate(chips):
            theirs = 2 * chip[0] + chip[1]
            for pi, (t, r0, nr) in enumerate(pieces):
                copy(j * NP + pi, blk(t, theirs, c, r0, nr), me).wait_recv()
                sent.append(copy((3 + j) * NP + pi, blk(t, theirs, c, r0, nr), sibling))
                sent[-1].start()
        for j, chip in enumerate(chips):
            theirs = 2 * chip[0] + chip[1]
            for pi, (t, r0, nr) in enumerate(pieces):
                copy((3 + j) * NP + pi, blk(t, theirs, 1 - c, r0, nr), me).wait_recv()
        for cp in sent:
            cp.wait_send()
        for cp in own:
            cp.wait()

    hbm = pl.BlockSpec(memory_space=pl.ANY)
    return pl.pallas_call(
        body, name="ag_weights", out_shape=[_sds((4,) + a.shape, a.dtype) for a in ts],
        in_specs=[hbm] * nt, out_specs=[hbm] * nt,
        scratch_shapes=[pltpu.SemaphoreType.DMA((6 * NP,)), pltpu.SemaphoreType.DMA((6 * NP,)),
                        pltpu.SemaphoreType.DMA((nt,))],
    )(*ts)


def _grad_exchange(ts):
    nt = len(ts)
    pieces = _row_chunks(ts, 2)
    NP = len(pieces)

    def body(*refs):
        src, out = refs[:nt], refs[nt:2 * nt]
        send_sems, recv_sems, local_sems = refs[2 * nt:]
        x, y, c = lax.axis_index("x"), lax.axis_index("y"), lax.axis_index("c")
        me = 4 * x + 2 * y + c
        own = [pltpu.make_async_copy(src[t].at[c, 2 * x + y], out[t].at[me], local_sems.at[t]) for t in range(nt)]
        for cp in own:
            cp.start()
        sent, peers = [], []
        for k in range(1, 8):
            px = 1 - x if k & 4 else x
            py = 1 - y if k & 2 else y
            pc = 1 - c if k & 1 else c
            peers.append(4 * px + 2 * py + pc)
            for pi, (t, r0, nr) in enumerate(pieces):
                idx = (k - 1) * NP + pi
                sent.append(pltpu.make_async_remote_copy(
                    src_ref=src[t].at[pc, 2 * px + py, r0:r0 + nr, :], dst_ref=out[t].at[me, r0:r0 + nr, :],
                    send_sem=send_sems.at[idx], recv_sem=recv_sems.at[idx], device_id=(px, py, pc),
                    device_id_type=MESH))
                sent[-1].start()
        for k, peer in enumerate(peers):
            for pi, (t, r0, nr) in enumerate(pieces):
                idx = k * NP + pi
                pltpu.make_async_remote_copy(
                    src_ref=out[t].at[peer, r0:r0 + nr, :], dst_ref=out[t].at[peer, r0:r0 + nr, :],
                    send_sem=send_sems.at[idx], recv_sem=recv_sems.at[idx], device_id=(x, y, c),
                    device_id_type=MESH).wait_recv()
        for cp in sent:
            cp.wait_send()
        for cp in own:
            cp.wait()

    hbm = pl.BlockSpec(memory_space=pl.ANY)
    return pl.pallas_call(
        body, name="grad_exchange", out_shape=[_sds((8,) + a.shape[2:], a.dtype) for a in ts],
        in_specs=[hbm] * nt, out_specs=[hbm] * nt,
        scratch_shapes=[pltpu.SemaphoreType.DMA((7 * NP,)), pltpu.SemaphoreType.DMA((7 * NP,)),
                        pltpu.SemaphoreType.DMA((nt,))],
    )(*ts)


def _sibling_join(ts):
    nt = len(ts)
    pieces = _row_chunks(ts, 0)
    NP = len(pieces)

    def body(*refs):
        src, out = refs[:nt], refs[nt:2 * nt]
        send_sems, recv_sems, local_sems = refs[2 * nt:]
        x, y, c = lax.axis_index("x"), lax.axis_index("y"), lax.axis_index("c")
        own = [pltpu.make_async_copy(src[t], out[t].at[c], local_sems.at[t]) for t in range(nt)]
        for cp in own:
            cp.start()
        sent = []
        for pi, (t, r0, nr) in enumerate(pieces):
            sent.append(pltpu.make_async_remote_copy(
                src_ref=src[t].at[r0:r0 + nr, :], dst_ref=out[t].at[c, r0:r0 + nr, :], send_sem=send_sems.at[pi],
                recv_sem=recv_sems.at[pi], device_id=(x, y, 1 - c), device_id_type=MESH))
            sent[-1].start()
        for pi, (t, r0, nr) in enumerate(pieces):
            pltpu.make_async_remote_copy(
                src_ref=src[t].at[r0:r0 + nr, :], dst_ref=out[t].at[1 - c, r0:r0 + nr, :], send_sem=send_sems.at[pi],
                recv_sem=recv_sems.at[pi], device_id=(x, y, c), device_id_type=MESH).wait_recv()
        for cp in sent:
            cp.wait_send()
        for cp in own:
            cp.wait()

    vmem = pl.BlockSpec(memory_space=pltpu.VMEM)
    return pl.pallas_call(
        body, name="sibling_join", out_shape=[_sds((2,) + a.shape, a.dtype) for a in ts],
        in_specs=[vmem] * nt, out_specs=[vmem] * nt,
        scratch_shapes=[pltpu.SemaphoreType.DMA((NP,)), pltpu.SemaphoreType.DMA((NP,)),
                        pltpu.SemaphoreType.DMA((nt,))],
        compiler_params=pltpu.CompilerParams(vmem_limit_bytes=_VMEM_LIMIT),
    )(*ts)


def _layer_fwd(x, shift, scale, gate, lw):
    D = x.shape[1]
    h = _norm_mod(x, lw["norm_g"], scale, shift)
    p = _matmul("in_proj", h, lw["w_cat"], "nn", F32, tn_cap=896)
    qn, kn = _sb_prep(p, lw["gq_t"], lw["gk_t"])
    o_att = _sb_fwd(qn, kn, p)
    qkv, bb, gcb, glb = _dn_prep(p, lw["conv_w"], lw["a_row"], lw["dtb_row"])
    o_dn, tinv, sall = _dn_fwd(qkv, bb, gcb, glb)
    o_sb, o_dnn = _gate(o_att, o_dn, p, lw["gn"])
    y, b_sb, b_dn = _branch(o_sb, o_dnn, lw["wb_sb"], lw["wb_dn"], p, D)
    x_next, out = _out_proj(x, y, lw["w_out"], gate)
    res = dict(x=x, h=h, p=p, qn=qn, kn=kn, o_att=o_att, qkv=qkv, bb=bb, gcb=gcb, glb=glb, o_dn=o_dn,
               tinv=tinv, sall=sall, o_sb=o_sb, o_dnn=o_dnn, y=y, b_sb=b_sb, b_dn=b_dn, out=out,
               shift=shift, scale=scale, gate=gate)
    return x_next, res


def _layer_bwd(dxn, res, lw):
    p = res["p"]
    dout, db_sb, db_dn, dm, dgate = _out_bwd(dxn, res["out"], res["gate"], lw["w_out"], p, res["b_sb"], res["b_dn"])
    dw_out = _matmul("dw_out", res["y"], dout, "tn", _MXU_DTYPE)
    dwb_sb = _matmul("dwb_sb", res["o_sb"], db_sb, "tn", _MXU_DTYPE)
    dwb_dn = _matmul("dwb_dn", res["o_dnn"], db_dn, "tn", _MXU_DTYPE)
    do_att, dz_sb, do_dn, dz_dn, dgn = _gate_bwd(db_sb, db_dn, lw["wb_sb"], lw["wb_dn"], res["o_att"], res["o_dn"],
                                                  p, lw["gn"])
    dqn, dkn, dv = _sb_bwd(res["qn"], res["kn"], p, do_att)
    dq_sb, dk_sb, dgq, dgk = _sb_prep_bwd(p, dqn, dkn, lw["gq_t"], lw["gk_t"])
    dqkv, dbb, dgb = _dn_bwd(res["qkv"], res["bb"], res["gcb"], res["glb"], res["tinv"], res["sall"], do_dn)
    dc, dp_ba, dal, ddt = _dn_prep_bwd_a(p, dqkv, dbb, dgb, lw["conv_w"], lw["a_row"], lw["dtb_row"])
    dp_dn, dconv = _dn_prep_bwd_b(p, dc, lw["conv_w"])
    dp = jnp.concatenate([dp_dn, dz_dn, dq_sb, dk_sb, dv.astype(_MXU_DTYPE), dz_sb, dm, dp_ba], axis=1)
    dh = _matmul("dh", dp, lw["w_cat"], "nt", F32, tk_cap=896)
    dw_cat = _matmul("dw_cat", res["h"], dp, "tn", _MXU_DTYPE, tm_cap=1024, tn_cap=896, tk_cap=512)
    dx, dshift, dscale, dnorm_g = _norm_mod_bwd(res["x"], dh, dxn, lw["norm_g"], res["scale"])
    small = dict(dmod=jnp.concatenate([dshift, dscale, dgate], axis=1)[0], norm_g=dnorm_g[0],
                 sb_q_g=dgq.reshape(SB_HEADS, SB_HD).sum(0), sb_k_g=dgk.reshape(SB_HEADS, SB_HD).sum(0),
                 conv_w=dconv, dn_a_log=dal[0, DN_HEADS:2 * DN_HEADS], dn_dt_bias=ddt[0, DN_HEADS:2 * DN_HEADS],
                 dn_norm_g=dgn[0])
    big = dict(w_cat=dw_cat, w_branch_sb=dwb_sb, w_branch_dn=dwb_dn, w_out=dw_out)
    return dx, small, big


def _cat_cols(w, D):
    return jnp.concatenate([w[:, 2048:4096], w[:, 0:2048], w[:, 4104:4104 + 2 * D], w[:, 4096:4104],
                            jnp.zeros((w.shape[0], LANES - 8), w.dtype)], axis=1)


def _uncat_cols(g, D):
    return jnp.concatenate([g[:, 2048:4096], g[:, 0:2048], g[:, 4096 + 2 * D:4096 + 2 * D + 8],
                            g[:, 4096:4096 + 2 * D]], axis=1)


def _flat_pack(arrs, mult):
    flat = jnp.concatenate([a.reshape(-1) for a in arrs])
    n = flat.shape[0]
    pad = (-n) % mult
    if pad:
        flat = jnp.concatenate([flat, jnp.zeros((pad,), flat.dtype)])
    return flat.reshape(-1, LANES)


def _flat_unpack(flat, shapes):
    flat = flat.reshape(-1)
    out, off = [], 0
    for s in shapes:
        n = math.prod(s)
        out.append(flat[off:off + n].reshape(s))
        off += n
    return out


BIG = ("w_in", "w_branch_sb", "w_branch_dn", "w_out")
SMALL = ("ada_b", "norm_g", "sb_q_g", "sb_k_g", "conv_w", "dn_a_log", "dn_dt_bias", "dn_norm_g")


def kernel(x, c, ada_w, ada_b, norm_g, w_in, sb_q_g, sb_k_g, conv_w, dn_a_log, dn_dt_bias, dn_norm_g, w_branch_sb, w_branch_dn, w_out, loss_target, m_ada_w, m_ada_b, m_norm_g, m_w_in, m_sb_q_g, m_sb_k_g, m_conv_w, m_dn_a_log, m_dn_dt_bias, m_dn_norm_g, m_w_branch_sb, m_w_branch_dn, m_w_out, v_ada_w, v_ada_b, v_norm_g, v_w_in, v_sb_q_g, v_sb_k_g, v_conv_w, v_dn_a_log, v_dn_dt_bias, v_dn_norm_g, v_w_branch_sb, v_w_branch_dn, v_w_out):
    W = dict(ada_w=ada_w, ada_b=ada_b, norm_g=norm_g, w_in=w_in, sb_q_g=sb_q_g, sb_k_g=sb_k_g, conv_w=conv_w,
             dn_a_log=dn_a_log, dn_dt_bias=dn_dt_bias, dn_norm_g=dn_norm_g, w_branch_sb=w_branch_sb,
             w_branch_dn=w_branch_dn, w_out=w_out)
    M = dict(ada_w=m_ada_w, ada_b=m_ada_b, norm_g=m_norm_g, w_in=m_w_in, sb_q_g=m_sb_q_g, sb_k_g=m_sb_k_g,
             conv_w=m_conv_w, dn_a_log=m_dn_a_log, dn_dt_bias=m_dn_dt_bias, dn_norm_g=m_dn_norm_g,
             w_branch_sb=m_w_branch_sb, w_branch_dn=m_w_branch_dn, w_out=m_w_out)
    V = dict(ada_w=v_ada_w, ada_b=v_ada_b, norm_g=v_norm_g, w_in=v_w_in, sb_q_g=v_sb_q_g, sb_k_g=v_sb_k_g,
             conv_w=v_conv_w, dn_a_log=v_dn_a_log, dn_dt_bias=v_dn_dt_bias, dn_norm_g=v_dn_norm_g,
             w_branch_sb=v_w_branch_sb, w_branch_dn=v_w_branch_dn, w_out=v_w_out)
    L = ada_w.shape[0]
    S, D = x.shape[1], x.shape[2]
    ix, iy, ic = lax.axis_index("x"), lax.axis_index("y"), lax.axis_index("c")
    shard = 2 * ix + iy
    me = 2 * shard + ic
    n_ada = ada_w.shape[2]
    n_in = w_in.shape[2]
    n_conv = conv_w.shape[2]
    n_br = w_branch_sb.shape[2]
    n_out = w_out.shape[1]

    g_in, g_bs, g_bd, g_out = _ag_weights([W[n].astype(_MXU_DTYPE) for n in BIG])
    w_in_f = jnp.concatenate([g_in[s] for s in range(4)], axis=2)
    wb_sb_f = jnp.concatenate([g_bs[s] for s in range(4)], axis=2)
    wb_dn_f = jnp.concatenate([g_bd[s] for s in range(4)], axis=2)
    w_out_f = jnp.concatenate([g_out[s] for s in range(4)], axis=1)

    g1 = _ag_small("ag_c_conv", _flat_pack([c, conv_w], LANES * 8))
    g1 = g1.reshape(8, -1)
    c_all = g1[:, :D]
    conv_parts = g1[:, D:D + L * CONV_K * n_conv].reshape(4, 2, L, CONV_K, n_conv)[:, 0]
    conv_full = jnp.concatenate([conv_parts[s] for s in range(4)], axis=2)
    ada_b_sh = lax.dynamic_slice_in_dim(ada_b, shard * n_ada, n_ada, axis=1)[:, None, :]
    mod_sh = _ada_fwd(c_all, ada_w, ada_b_sh)
    g2 = _ag_small("ag_mod", _flat_pack([mod_sh], LANES * 8)).reshape(8, -1)
    mod_parts = g2[:, :L * 8 * n_ada].reshape(4, 2, L, 8, n_ada)[:, 0]
    mod_all = jnp.concatenate([mod_parts[s] for s in range(4)], axis=2)
    mod = lax.dynamic_index_in_dim(mod_all, me, axis=1, keepdims=False)

    def layer_weights(l):
        pad_lo = jnp.zeros((DN_HEADS,), F32)
        pad_hi = jnp.zeros((LANES - 2 * DN_HEADS,), F32)
        return dict(
            norm_g=norm_g[l][None, :], w_cat=_cat_cols(w_in_f[l], D),
            gq_t=jnp.tile(sb_q_g[l], SB_HEADS)[None, :], gk_t=jnp.tile(sb_k_g[l], SB_HEADS)[None, :],
            conv_w=conv_full[l],
            a_row=jnp.concatenate([pad_lo, dn_a_log[l], pad_hi])[None, :],
            dtb_row=jnp.concatenate([pad_lo, dn_dt_bias[l], pad_hi])[None, :],
            gn=dn_norm_g[l][None, :], wb_sb=wb_sb_f[l], wb_dn=wb_dn_f[l], w_out=w_out_f[l])

    xs = x[0]
    lws, ress = [], []
    for l in range(L):
        lw = layer_weights(l)
        xs, res = _layer_fwd(xs, mod[l, None, 0:D], mod[l, None, D:2 * D], mod[l, None, 2 * D:3 * D], lw)
        lws.append(lw)
        ress.append(res)
    dxs, loss_row = _loss_head(xs, loss_target[0])
    loss = lax.psum(loss_row[0, 0], ("x", "y", "c"))
    smalls, bigs = [None] * L, [None] * L
    for l in reversed(range(L)):
        dxs, smalls[l], bigs[l] = _layer_bwd(dxs, ress[l], lws[l])
    grad_x = dxs[None]

    small_names = ("dmod",) + SMALL[1:]
    small_pack = _flat_pack([jnp.stack([smalls[l][n] for l in range(L)]) for n in small_names], LANES * 8)
    g3 = _ag_small("ag_small_grads", small_pack)
    R3 = small_pack.shape[0]
    g3 = g3.reshape(8, R3, LANES)
    small_sum = _sum_parts("sum_small", g3)
    small_shapes = [(L, 3 * D), (L, D), (L, SB_HD), (L, SB_HD), (L, CONV_K, 3 * DN_W), (L, DN_HEADS), (L, DN_HEADS),
                    (L, DN_HD)]
    sg = dict(zip(small_names, _flat_unpack(small_sum, small_shapes)))
    G = dict(ada_b=sg["dmod"], norm_g=sg["norm_g"], sb_q_g=sg["sb_q_g"], sb_k_g=sg["sb_k_g"],
             conv_w=lax.dynamic_slice_in_dim(sg["conv_w"], shard * n_conv, n_conv, axis=2),
             dn_a_log=sg["dn_a_log"], dn_dt_bias=sg["dn_dt_bias"], dn_norm_g=sg["dn_norm_g"])
    dmod_all = g3.reshape(8, -1)[:, :L * 3 * D].reshape(8, L, 3 * D)
    dmod_sh = lax.dynamic_slice_in_dim(dmod_all, shard * n_ada, n_ada, axis=2).transpose(1, 0, 2)
    G["ada_w"] = _ada_bwd(c_all.T, dmod_sh)

    def by_col_shard(name, n_sh):
        g = jnp.stack([bigs[l][name] for l in range(L)])
        return g.reshape(L, g.shape[1], 4, n_sh).transpose(0, 2, 1, 3)

    send = [jnp.stack([_uncat_cols(bigs[l]["w_cat"], D) for l in range(L)]).reshape(L, D, 4, n_in).transpose(0, 2, 1, 3),
            by_col_shard("w_branch_sb", n_br), by_col_shard("w_branch_dn", n_br),
            jnp.stack([bigs[l]["w_out"] for l in range(L)]).reshape(L, 4, n_out, D)]
    got = _grad_exchange(send)
    mine = [_sum_parts("sum_" + n, g) for n, g in zip(BIG, got)]
    for n, g in zip(BIG, _sibling_join(mine)):
        G[n] = g

    delta, new_m, new_v = {}, {}, {}
    for n in ("ada_w",) + BIG:
        sh = W[n].shape
        two = (sh[0] * sh[1], sh[2])
        d, mo, vo = _adamw("adamw_" + n, W[n].reshape(two), G[n].reshape(two), M[n].reshape(two), V[n].reshape(two))
        delta[n], new_m[n], new_v[n] = d.reshape(sh), mo.reshape(sh), vo.reshape(sh)
    sm_shapes = [W[n].shape for n in SMALL]
    d, mo, vo = _adamw("adamw_small", _flat_pack([W[n] for n in SMALL], LANES * 8),
                       _flat_pack([G[n] for n in SMALL], LANES * 8), _flat_pack([M[n] for n in SMALL], LANES * 8),
                       _flat_pack([V[n] for n in SMALL], LANES * 8))
    for n, dd, mm, vv in zip(SMALL, _flat_unpack(d, sm_shapes), _flat_unpack(mo, sm_shapes),
                             _flat_unpack(vo, sm_shapes)):
        delta[n], new_m[n], new_v[n] = dd, mm, vv

    order = ("ada_w", "ada_b", "norm_g", "w_in", "sb_q_g", "sb_k_g", "conv_w", "dn_a_log", "dn_dt_bias", "dn_norm_g",
             "w_branch_sb", "w_branch_dn", "w_out")
    return (loss, grad_x, *[G[n] for n in order], *[delta[n] for n in order], *[new_m[n] for n in order],
            *[new_v[n] for n in order])
```

```python
import math

import jax
import jax.numpy as jnp
from jax import lax
from jax.experimental import pallas as pl
from jax.experimental.pallas import tpu as pltpu

F32 = jnp.float32
BF16 = jnp.bfloat16
_MXU_DTYPE = BF16
_VMEM_LIMIT = 48 * 1024 * 1024
LANES = 128

EPS = 1e-6
SB_HEADS, SB_HD, SB_W = 8, 64, 512
DN_HEADS, DN_HD, DN_W = 4, 128, 512
CONV_K = 4
CHUNK = 64
QB = 256
_SB_DEAD = 104.0
ADAM_LR, ADAM_B1, ADAM_B2, ADAM_EPS, ADAM_WD, ADAM_STEP = 0.001, 0.9, 0.999, 1e-08, 0.01, 10

C_DN_QKV, C_DN_Z, C_SB_Q, C_SB_K, C_SB_V, C_SB_Z, C_MG = 0, 1536, 2048, 2560, 3072, 3584, 4096

_NN = (((1,), (0,)), ((), ()))
_NT = (((1,), (1,)), ((), ()))
_TN = (((0,), (0,)), ((), ()))
MESH = pl.DeviceIdType.MESH


def _sds(shape, dtype):
    return jax.ShapeDtypeStruct(shape, dtype)


def _cp(n):
    return pltpu.CompilerParams(dimension_semantics=("arbitrary",) * n, vmem_limit_bytes=_VMEM_LIMIT)


def _rb(tm, w, cb=0):
    return pl.BlockSpec((tm, w), lambda i: (i, cb))


def _fs(shape):
    nd = len(shape)
    return pl.BlockSpec(shape, lambda i: (0,) * nd)


def _dg(a, b, dims):
    return lax.dot_general(a, b, dims, preferred_element_type=F32)


def _mm(a, b, dims=_NN):
    return _dg(a.astype(_MXU_DTYPE), b.astype(_MXU_DTYPE), dims)


def _split3(x):
    hi = x.astype(BF16)
    r = x - hi.astype(F32)
    mid = r.astype(BF16)
    lo = (r - mid.astype(F32)).astype(BF16)
    return hi, mid, lo


def _mm_xl(x, const, dims=_NN):
    cb = const.astype(BF16)
    hi, mid, lo = _split3(x)
    return _dg(hi, cb, dims) + _dg(mid, cb, dims) + _dg(lo, cb, dims)


def _mm_xl2(x, const, dims=_NN):
    cb = const.astype(BF16)
    hi = x.astype(BF16)
    lo = (x - hi.astype(F32)).astype(BF16)
    return _dg(hi, cb, dims) + _dg(lo, cb, dims)


def _mm_xr(const, x, dims=_NN):
    cb = const.astype(BF16)
    hi, mid, lo = _split3(x)
    return _dg(cb, hi, dims) + _dg(cb, mid, dims) + _dg(cb, lo, dims)


def _mm3(a, b, dims=_NN):
    ah, am, _ = _split3(a)
    bh, bm, _ = _split3(b)
    return _dg(ah, bh, dims) + (_dg(ah, bm, dims) + _dg(am, bh, dims))


def _sigmoid(z):
    return 1.0 / (1.0 + jnp.exp(-z))


def _silu(z):
    return z * _sigmoid(z)


def _dsilu(z):
    s = _sigmoid(z)
    return s * (1.0 + z * (1.0 - s))


def _softplus(z):
    return jnp.maximum(z, 0.0) + jnp.log(1.0 + jnp.exp(-jnp.abs(z)))


def _iota2(shape, dim):
    return lax.broadcasted_iota(jnp.int32, shape, dim)


def _pick(n, cap, mult):
    best = None
    for t in range(mult, min(n, cap) + 1, mult):
        if n % t == 0:
            best = t
    assert best is not None, (n, cap, mult)
    return best


def _matmul(name, a, b, form, out_dtype, tm_cap=512, tn_cap=1024, tk_cap=1024):
    if form == "nn":
        (M, K), (_, N) = a.shape, b.shape
    elif form == "nt":
        (M, K), (N, _) = a.shape, b.shape
    else:
        (K, M), (_, N) = a.shape, b.shape
    tm = _pick(M, tm_cap, 128 if form == "tn" else 8)
    tn = _pick(N, tn_cap, 128)
    tk = _pick(K, tk_cap, 128)
    nk = K // tk
    dims = {"nn": _NN, "nt": _NT, "tn": _TN}[form]
    if form == "nn":
        a_spec = pl.BlockSpec((tm, tk), lambda i, j, k: (i, k))
        b_spec = pl.BlockSpec((tk, tn), lambda i, j, k: (k, j))
    elif form == "nt":
        a_spec = pl.BlockSpec((tm, tk), lambda i, j, k: (i, k))
        b_spec = pl.BlockSpec((tn, tk), lambda i, j, k: (j, k))
    else:
        a_spec = pl.BlockSpec((tk, tm), lambda i, j, k: (k, i))
        b_spec = pl.BlockSpec((tk, tn), lambda i, j, k: (k, j))

    def body(a_ref, b_ref, o_ref, acc_ref):
        k = pl.program_id(2)

        @pl.when(k == 0)
        def _():
            acc_ref[...] = jnp.zeros_like(acc_ref)

        acc_ref[...] += _mm(a_ref[...], b_ref[...], dims)

        @pl.when(k == nk - 1)
        def _():
            o_ref[...] = acc_ref[...].astype(o_ref.dtype)

    return pl.pallas_call(
        body, name=name, grid=(M // tm, N // tn, nk),
        in_specs=[a_spec, b_spec],
        out_specs=pl.BlockSpec((tm, tn), lambda i, j, k: (i, j)),
        out_shape=_sds((M, N), out_dtype),
        scratch_shapes=[pltpu.VMEM((tm, tn), F32)],
        compiler_params=_cp(3),
    )(a, b)


def _norm_mod(x, g, scale, shift, tm=256):
    S, D = x.shape

    def body(x_ref, g_ref, sc_ref, sh_ref, h_ref):
        xv = x_ref[...]
        r = lax.rsqrt(jnp.mean(xv * xv, axis=1, keepdims=True) + EPS)
        h_ref[...] = ((xv * r * g_ref[...]) * (1.0 + sc_ref[...]) + sh_ref[...]).astype(h_ref.dtype)

    return pl.pallas_call(
        body, name="norm_mod", grid=(S // tm,),
        in_specs=[_rb(tm, D), _fs((1, D)), _fs((1, D)), _fs((1, D))],
        out_specs=_rb(tm, D), out_shape=_sds((S, D), _MXU_DTYPE), compiler_params=_cp(1),
    )(x, g, scale, shift)


def _norm_mod_bwd(x, dh, dxn, g, scale, tm=256):
    S, D = x.shape

    def body(x_ref, dh_ref, dxn_ref, g_ref, sc_ref, dx_ref, dsh_ref, dsc_ref, dg_ref):
        @pl.when(pl.program_id(0) == 0)
        def _():
            dsh_ref[...] = jnp.zeros_like(dsh_ref)
            dsc_ref[...] = jnp.zeros_like(dsc_ref)
            dg_ref[...] = jnp.zeros_like(dg_ref)

        xv, dhv, gv = x_ref[...], dh_ref[...], g_ref[...]
        r = lax.rsqrt(jnp.mean(xv * xv, axis=1, keepdims=True) + EPS)
        xh = xv * r
        one_sc = 1.0 + sc_ref[...]
        dsh_ref[...] += jnp.sum(dhv, axis=0, keepdims=True)
        dsc_ref[...] += jnp.sum(dhv * xh * gv, axis=0, keepdims=True)
        dg_ref[...] += jnp.sum(dhv * one_sc * xh, axis=0, keepdims=True)
        dxh = dhv * (gv * one_sc)
        dx_ref[...] = r * (dxh - xh * jnp.mean(dxh * xh, axis=1, keepdims=True)) + dxn_ref[...]

    return pl.pallas_call(
        body, name="norm_mod_bwd", grid=(S // tm,),
        in_specs=[_rb(tm, D), _rb(tm, D), _rb(tm, D), _fs((1, D)), _fs((1, D))],
        out_specs=[_rb(tm, D), _fs((1, D)), _fs((1, D)), _fs((1, D))],
        out_shape=[_sds((S, D), F32)] + [_sds((1, D), F32)] * 3, compiler_params=_cp(1),
    )(x, dh, dxn, g, scale)


def _head_sum_matrix():
    r = jnp.arange(SB_W)
    return (r[:, None] // SB_HD == r[None, :] // SB_HD).astype(BF16)


def _sb_prep(p, gq_t, gk_t, tm=256):
    S = p.shape[0]
    bd = _head_sum_matrix()

    def body(q_ref, k_ref, gq_ref, gk_ref, bd_ref, qn_ref, kn_ref):
        for src, g_ref, dst in ((q_ref, gq_ref, qn_ref), (k_ref, gk_ref, kn_ref)):
            v = src[...]
            ms = _mm_xl(v * v, bd_ref[...]) * (1.0 / SB_HD)
            dst[...] = (v * lax.rsqrt(ms + EPS) * g_ref[...]).astype(dst.dtype)

    return pl.pallas_call(
        body, name="sb_prep", grid=(S // tm,),
        in_specs=[_rb(tm, SB_W, C_SB_Q // SB_W), _rb(tm, SB_W, C_SB_K // SB_W),
                  _fs((1, SB_W)), _fs((1, SB_W)), _fs((SB_W, SB_W))],
        out_specs=[_rb(tm, SB_W), _rb(tm, SB_W)],
        out_shape=[_sds((S, SB_W), _MXU_DTYPE)] * 2, compiler_params=_cp(1),
    )(p, p, gq_t, gk_t, bd)


def _sb_prep_bwd(p, dqn, dkn, gq_t, gk_t, tm=256):
    S = p.shape[0]
    bd = _head_sum_matrix()

    def body(q_ref, k_ref, dqn_ref, dkn_ref, gq_ref, gk_ref, bd_ref, dq_ref, dk_ref, dgq_ref, dgk_ref):
        @pl.when(pl.program_id(0) == 0)
        def _():
            dgq_ref[...] = jnp.zeros_like(dgq_ref)
            dgk_ref[...] = jnp.zeros_like(dgk_ref)

        for src, dn_ref, g_ref, dst, dg_ref in ((q_ref, dqn_ref, gq_ref, dq_ref, dgq_ref),
                                                (k_ref, dkn_ref, gk_ref, dk_ref, dgk_ref)):
            v, dn = src[...], dn_ref[...]
            r = lax.rsqrt(_mm_xl(v * v, bd_ref[...]) * (1.0 / SB_HD) + EPS)
            vh = v * r
            dg_ref[...] += jnp.sum(dn * vh, axis=0, keepdims=True)
            dvh = dn * g_ref[...]
            m = _mm_xl(dvh * vh, bd_ref[...]) * (1.0 / SB_HD)
            dst[...] = (r * (dvh - vh * m)).astype(dst.dtype)

    return pl.pallas_call(
        body, name="sb_prep_bwd", grid=(S // tm,),
        in_specs=[_rb(tm, SB_W, C_SB_Q // SB_W), _rb(tm, SB_W, C_SB_K // SB_W), _rb(tm, SB_W), _rb(tm, SB_W),
                  _fs((1, SB_W)), _fs((1, SB_W)), _fs((SB_W, SB_W))],
        out_specs=[_rb(tm, SB_W), _rb(tm, SB_W), _fs((1, SB_W)), _fs((1, SB_W))],
        out_shape=[_sds((S, SB_W), _MXU_DTYPE)] * 2 + [_sds((1, SB_W), F32)] * 2, compiler_params=_cp(1),
    )(p, p, dqn, dkn, gq_t, gk_t, bd)


def _sb_consts():
    r, c = _iota2((QB, QB), 0), _iota2((QB, QB), 1)
    lane = _iota2((1, LANES), 1)
    return r, c, lane


def _sb_fwd(qn, kn, p):
    S = qn.shape[0]
    scale = 1.0 / math.sqrt(SB_HD)

    def body(q_ref, k_ref, v_ref, o_ref):
        i = pl.program_id(1)
        r, c, lane = _sb_consts()
        u_gt = (r > c).astype(BF16)
        strict = c < r
        q = q_ref[...]
        masks = [(lane // SB_HD) == h for h in range(2)]
        qhs = [jnp.where(m, q, jnp.zeros_like(q)) for m in masks]

        def block(off, carry, diagonal):
            kj = k_ref[pl.ds(off, QB), :]
            vj = v_ref[pl.ds(off, QB), :].astype(_MXU_DTYPE)
            out = []
            for h in range(2):
                o_h, run = carry[2 * h], carry[2 * h + 1]
                z = _mm(qhs[h], kj, _NT) * scale
                sp = _softplus(z)
                sp_m = jnp.where(strict, sp, 0.0) if diagonal else sp
                later = _mm_xl2(sp_m, u_gt)
                w = jnp.exp((z - sp) - later - run)
                if diagonal:
                    w = jnp.where(strict, w, 0.0)
                out += [o_h + _mm(w, vj), run + jnp.sum(sp_m, axis=1, keepdims=True)]
            return tuple(out)

        init = (jnp.zeros((QB, LANES), F32), jnp.zeros((QB, 1), F32)) * 2
        carry = block(pl.multiple_of(i * QB, QB), init, True)
        st = lax.while_loop(
            lambda st: jnp.logical_and(st[0] <= i, jnp.minimum(jnp.min(st[2]), jnp.min(st[4])) < _SB_DEAD),
            lambda st: (st[0] + 1,) + block(pl.multiple_of((i - st[0]) * QB, QB), st[1:], False),
            (jnp.int32(1),) + carry)
        o_ref[...] = jnp.where(masks[0], st[1], st[3])

    return pl.pallas_call(
        body, name="sb_fwd", grid=(SB_W // LANES, S // QB),
        in_specs=[pl.BlockSpec((QB, LANES), lambda hp, i: (i, hp)),
                  pl.BlockSpec((S, LANES), lambda hp, i: (0, hp)),
                  pl.BlockSpec((S, LANES), lambda hp, i: (0, C_SB_V // LANES + hp))],
        out_specs=pl.BlockSpec((QB, LANES), lambda hp, i: (i, hp)),
        out_shape=_sds((S, SB_W), F32), compiler_params=_cp(2),
    )(qn, kn, p)


def _sb_bwd(qn, kn, p, do):
    S = qn.shape[0]
    scale = 1.0 / math.sqrt(SB_HD)

    def body(q_ref, k_ref, v_ref, do_ref, dq_ref, dk_ref, dv_ref):
        i = pl.program_id(1)

        @pl.when(i == 0)
        def _():
            dk_ref[...] = jnp.zeros_like(dk_ref)
            dv_ref[...] = jnp.zeros_like(dv_ref)

        r, c, lane = _sb_consts()
        u_le = (r <= c).astype(BF16)
        u_lt = (r < c).astype(BF16)
        strict = c < r
        q = q_ref[...]
        do = do_ref[...].astype(_MXU_DTYPE)
        masks = [(lane // SB_HD) == h for h in range(2)]
        qhs = [jnp.where(m, q, jnp.zeros_like(q)) for m in masks]
        dohs = [jnp.where(m, do, jnp.zeros_like(do)) for m in masks]

        def sums(off, runs, diagonal):
            kj = k_ref[pl.ds(off, QB), :]
            out = []
            for h in range(2):
                sp = _softplus(_mm(qhs[h], kj, _NT) * scale)
                if diagonal:
                    sp = jnp.where(strict, sp, 0.0)
                out.append(runs[h] + jnp.sum(sp, axis=1, keepdims=True))
            return tuple(out)

        runs = sums(pl.multiple_of(i * QB, QB), (jnp.zeros((QB, 1), F32),) * 2, True)
        st = lax.while_loop(
            lambda st: jnp.logical_and(st[0] <= i, jnp.minimum(jnp.min(st[1]), jnp.min(st[2])) < _SB_DEAD),
            lambda st: (st[0] + 1,) + sums(pl.multiple_of((i - st[0]) * QB, QB), st[1:], False),
            (jnp.int32(1),) + runs)
        first = i + 1 - st[0]
        tots = st[1:]

        def block(off, carry, diagonal):
            kj = k_ref[pl.ds(off, QB), :]
            vj = v_ref[pl.ds(off, QB), :].astype(_MXU_DTYPE)
            out, dk_blk, dv_blk = [], None, None
            for h in range(2):
                dq_h, pre_sp, pre_e = carry[3 * h:3 * h + 3]
                z = _mm(qhs[h], kj, _NT) * scale
                sp = _softplus(z)
                a = z - sp
                sp_m = jnp.where(strict, sp, 0.0) if diagonal else sp
                incl = _mm_xl2(sp_m, u_le)
                w = jnp.exp(a - ((tots[h] - pre_sp) - incl))
                if diagonal:
                    w = jnp.where(strict, w, 0.0)
                e = w * _mm(dohs[h], vj, _NT)
                db = pre_e + _mm_xl2(e, u_lt)
                dz = (e - jnp.exp(a) * (e + db)) * scale
                if diagonal:
                    dz = jnp.where(strict, dz, 0.0)
                dkh = jnp.where(masks[h], _mm(dz, q, _TN), 0.0)
                dvh = jnp.where(masks[h], _mm(w, do, _TN), 0.0)
                dk_blk = dkh if dk_blk is None else dk_blk + dkh
                dv_blk = dvh if dv_blk is None else dv_blk + dvh
                out += [dq_h + _mm(dz, kj), pre_sp + jnp.sum(sp_m, axis=1, keepdims=True),
                        pre_e + jnp.sum(e, axis=1, keepdims=True)]
            dk_ref[pl.ds(off, QB), :] += dk_blk
            dv_ref[pl.ds(off, QB), :] += dv_blk
            return tuple(out)

        zero_col = jnp.zeros((QB, 1), F32)
        init = (jnp.zeros((QB, LANES), F32), zero_col, zero_col) * 2
        carry = lax.fori_loop(first, i, lambda j, cr: block(pl.multiple_of(j * QB, QB), cr, False), init)
        carry = block(pl.multiple_of(i * QB, QB), carry, True)
        dq_ref[...] = jnp.where(masks[0], carry[0], carry[3])

    blk = pl.BlockSpec((QB, LANES), lambda hp, i: (i, hp))
    full = pl.BlockSpec((S, LANES), lambda hp, i: (0, hp))
    return pl.pallas_call(
        body, name="sb_bwd", grid=(SB_W // LANES, S // QB),
        in_specs=[blk, full, pl.BlockSpec((S, LANES), lambda hp, i: (0, C_SB_V // LANES + hp)), blk],
        out_specs=[blk, full, full],
        out_shape=[_sds((S, SB_W), F32)] * 3, compiler_params=_cp(2),
    )(qn, kn, p, do)


def _dn_prep(p, conv_w, a_row, dtb_row, tm=256):
    S = p.shape[0]
    W3 = 3 * DN_W
    nhalo = tm // 8

    def body(x_ref, halo_ref, w_ref, ba_ref, a_ref, dtb_ref, qkv_ref, bb_ref, gc_ref, gl_ref):
        i = pl.program_id(0)
        halo = jnp.where(i > 0, halo_ref[...], 0.0)
        xf = jnp.concatenate([halo, x_ref[...]], axis=0)
        acc = jnp.zeros((tm, W3), F32)
        for k in range(CONV_K):
            sh = CONV_K - 1 - k
            xs = xf if sh == 0 else pltpu.roll(xf, sh, 0)
            acc = acc + xs[8:, :] * w_ref[k:k + 1, :]
        s = _silu(acc)
        for gi in range(2 * DN_HEADS):
            sl = slice(gi * LANES, (gi + 1) * LANES)
            sg = s[:, sl]
            rinv = lax.rsqrt(jnp.sum(sg * sg, axis=1, keepdims=True) + EPS)
            qkv_ref[:, sl] = sg * rinv * (DN_HD ** -0.5 if gi < DN_HEADS else 1.0)
        qkv_ref[:, 2 * DN_W:] = s[:, 2 * DN_W:]

        ba = ba_ref[...]
        beta = _sigmoid(ba)
        g = -jnp.exp(a_ref[...]) * _softplus(ba + dtb_ref[...])
        lr, lc = _iota2((LANES, DN_W), 0), _iota2((LANES, DN_W), 1)
        sel_b = (lr == lc // LANES).astype(BF16)
        sel_g = (lr == lc // LANES + DN_HEADS).astype(BF16)
        bb_ref[...] = _mm_xl(beta, sel_b)
        graw = _mm_xl(g, sel_g)
        rr, cc = _iota2((tm, tm), 0), _iota2((tm, tm), 1)
        tri = jnp.logical_and(rr >= cc, rr // CHUNK == cc // CHUNK).astype(BF16)
        gc = _mm_xr(tri, graw)
        last = (cc == (rr // CHUNK) * CHUNK + (CHUNK - 1)).astype(BF16)
        gc_ref[...] = gc
        gl_ref[...] = _mm_xr(last, gc)

    return pl.pallas_call(
        body, name="dn_prep", grid=(S // tm,),
        in_specs=[_rb(tm, W3, 0), pl.BlockSpec((8, W3), lambda i: (jnp.maximum(i * nhalo - 1, 0), 0)),
                  _fs((CONV_K, W3)), _rb(tm, LANES, (p.shape[1] - LANES) // LANES),
                  _fs((1, LANES)), _fs((1, LANES))],
        out_specs=[_rb(tm, W3), _rb(tm, DN_W), _rb(tm, DN_W), _rb(tm, DN_W)],
        out_shape=[_sds((S, W3), F32)] + [_sds((S, DN_W), F32)] * 3, compiler_params=_cp(1),
    )(p, p, conv_w, p, a_row, dtb_row)


_BNN = (((2,), (1,)), ((0,), (0,)))
_BNT = (((2,), (2,)), ((0,), (0,)))
_BTN = (((1,), (1,)), ((0,), (0,)))


def _heads(ref, base=0):
    return jnp.stack([ref[:, base + h * LANES:base + (h + 1) * LANES] for h in range(DN_HEADS)])


def _per_head(const):
    return jnp.broadcast_to(const[None], (DN_HEADS,) + const.shape)


def _dn_chunk_terms(q, k, v, beta, gc, gl):
    r, c = _iota2((CHUNK, CHUNK), 0), _iota2((CHUNK, CHUNK), 1)
    tril, strict = r >= c, r > c
    gcol = _mm_xl(gc, _per_head(jnp.full((LANES, CHUNK), 1.0 / LANES, F32)), _BNN)
    grow = _mm_xr(_per_head(jnp.full((CHUNK, LANES), 1.0 / LANES, F32)), gc, _BNT)
    dec = jnp.where(tril, jnp.exp(jnp.where(tril, gcol - grow, 0.0)), 0.0)
    gam = jnp.exp(gc)
    dlt = jnp.exp(gl - gc)
    kb, vb = k * beta, v * beta
    pm = _mm(kb, k, _BNT)
    qk = _mm(q, k, _BNT)
    m = jnp.where(strict, pm * dec, 0.0)
    a = jnp.where(tril, qk * dec, 0.0)
    return dict(tril=tril, strict=strict, dec=dec, gam=gam, dlt=dlt, kb=kb, vb=vb, m=m, a=a)


def _dn_fwd(qkv, bb, gcb, glb):
    S = qkv.shape[0]
    N = S // CHUNK

    def body(qkv_ref, bb_ref, gc_ref, gl_ref, o_ref, t_ref, sall_ref, s_scr):
        @pl.when(pl.program_id(0) == 0)
        def _():
            s_scr[...] = jnp.zeros_like(s_scr)

        r, c = _iota2((CHUNK, CHUNK), 0), _iota2((CHUNK, CHUNK), 1)
        eye = (r == c).astype(F32)
        q, k, v = _heads(qkv_ref), _heads(qkv_ref, DN_W), _heads(qkv_ref, 2 * DN_W)
        beta, gc, gl = _heads(bb_ref), _heads(gc_ref), _heads(gl_ref)
        s_prev = s_scr[...]
        sall_ref[0] = s_prev.astype(sall_ref.dtype)
        s0 = s_prev.astype(sall_ref.dtype).astype(F32)
        t = _dn_chunk_terms(q, k, v, beta, gc, gl)
        pw = -t["m"]
        tinv = eye + pw
        for _ in range(5):
            pw = _mm3(pw, pw, _BNN)
            tinv = tinv + _mm3(tinv, pw, _BNN)
        t_ref[...] = tinv
        u = _mm3(tinv, t["vb"], _BNN)
        w = _mm3(tinv, t["kb"] * t["gam"], _BNN)
        vn = u - _mm(w, s0, _BNN)
        o = _mm(q * t["gam"], s0, _BNN) + _mm(t["a"], vn, _BNN)
        for h in range(DN_HEADS):
            o_ref[:, h * LANES:(h + 1) * LANES] = o[h]
        egl = jnp.exp(jnp.concatenate([gl, gl], axis=1))
        s_scr[...] = s_prev * egl + _mm(k * t["dlt"], vn, _BTN)

    return pl.pallas_call(
        body, name="dn_fwd", grid=(N,),
        in_specs=[_rb(CHUNK, 3 * DN_W), _rb(CHUNK, DN_W), _rb(CHUNK, DN_W), _rb(CHUNK, DN_W)],
        out_specs=[_rb(CHUNK, DN_W), pl.BlockSpec((DN_HEADS, CHUNK, CHUNK), lambda n: (0, n, 0)),
                   pl.BlockSpec((1, DN_HEADS, DN_HD, DN_HD), lambda n: (n, 0, 0, 0))],
        out_shape=[_sds((S, DN_W), F32), _sds((DN_HEADS, S, CHUNK), F32),
                   _sds((N, DN_HEADS, DN_HD, DN_HD), _MXU_DTYPE)],
        scratch_shapes=[pltpu.VMEM((DN_HEADS, DN_HD, DN_HD), F32)],
        compiler_params=_cp(1),
    )(qkv, bb, gcb, glb)


def _dn_bwd(qkv, bb, gcb, glb, tinv_all, sall, do):
    S = qkv.shape[0]
    N = S // CHUNK

    def body(qkv_ref, bb_ref, gc_ref, gl_ref, t_ref, sall_ref, do_ref, dqkv_ref, dbb_ref, dg_ref, ds_scr):
        @pl.when(pl.program_id(0) == 0)
        def _():
            ds_scr[...] = jnp.zeros_like(ds_scr)

        r, c = _iota2((CHUNK, CHUNK), 0), _iota2((CHUNK, CHUNK), 1)
        eye = (r == c).astype(F32)
        u_ge = (c >= r).astype(F32)
        last_row = _iota2((CHUNK, LANES), 0) == CHUNK - 1
        eye_h, u_ge_h = _per_head(eye), _per_head(u_ge)
        q, k, v = _heads(qkv_ref), _heads(qkv_ref, DN_W), _heads(qkv_ref, 2 * DN_W)
        beta, gc, gl = _heads(bb_ref), _heads(gc_ref), _heads(gl_ref)
        tinv = t_ref[...]
        s0 = sall_ref[0].astype(F32)
        do = _heads(do_ref)
        ds1 = ds_scr[...]
        t = _dn_chunk_terms(q, k, v, beta, gc, gl)
        gam, dlt, kb, vb, dec = t["gam"], t["dlt"], t["kb"], t["vb"], t["dec"]
        kbg = kb * gam
        u = _mm3(tinv, vb, _BNN)
        w = _mm3(tinv, kbg, _BNN)
        vn = u - _mm(w, s0, _BNN)
        qg, kd = q * gam, k * dlt
        egl = jnp.exp(gl)
        egl2 = jnp.concatenate([egl, egl], axis=1)

        dvn = _mm(t["a"], do, _BTN) + _mm(kd, ds1, _BNN)
        da = jnp.where(t["tril"], _mm(do, vn, _BNT), 0.0)
        dqg = _mm(do, s0, _BNT)
        dkd = _mm(vn, ds1, _BNT)
        dw = -_mm(dvn, s0, _BNT)
        ds_scr[...] = _mm(qg, do, _BTN) + egl2 * ds1 - _mm(w, dvn, _BTN)
        tt = _mm_xr(eye_h, tinv, _BNT)
        dvb = _mm3(tt, dvn, _BNN)
        dkbg = _mm3(tt, dw, _BNN)
        dm = -jnp.where(t["strict"], _mm(dvb, u, _BNT) + _mm(dkbg, w, _BNT), 0.0)
        dpm = dm * dec
        dqk = da * dec
        dkb = dkbg * gam + _mm(dpm, k, _BNN)
        dk = dkd * dlt + _mm(dpm, kb, _BTN) + _mm(dqk, q, _BTN) + dkb * beta
        dq = dqg * gam + _mm(dqk, k, _BNN)
        dv = dvb * beta
        dbeta = jnp.sum(dkb * k, axis=2, keepdims=True) + jnp.sum(dvb * v, axis=2, keepdims=True)
        dgam = jnp.sum(dqg * q, axis=2, keepdims=True) + jnp.sum(dkbg * kb, axis=2, keepdims=True)
        ddlt = jnp.sum(dkd * k, axis=2, keepdims=True)
        xm = dm * t["m"] + da * t["a"]
        xt = _mm_xr(eye_h, xm, _BNT)
        dgc = (dgam * gam - ddlt * dlt + jnp.sum(xm, axis=2, keepdims=True) - jnp.sum(xt, axis=2, keepdims=True))
        dgl = jnp.sum(ddlt * dlt, axis=1, keepdims=True) + jnp.sum(
            jnp.sum(ds1 * s0, axis=2, keepdims=True), axis=1, keepdims=True) * jnp.max(egl, axis=1, keepdims=True)
        dgc = dgc + jnp.where(last_row, dgl, 0.0)
        dg = _mm_xr(u_ge_h, dgc, _BNN)
        for h in range(DN_HEADS):
            sl = slice(h * LANES, (h + 1) * LANES)
            dqkv_ref[:, sl] = dq[h]
            dqkv_ref[:, DN_W + h * LANES:DN_W + (h + 1) * LANES] = dk[h]
            dqkv_ref[:, 2 * DN_W + h * LANES:2 * DN_W + (h + 1) * LANES] = dv[h]
            dbb_ref[:, sl] = jnp.broadcast_to(dbeta[h], (CHUNK, LANES))
            dg_ref[:, sl] = dg[h]

    rev = lambda w: pl.BlockSpec((CHUNK, w), lambda n: (N - 1 - n, 0))
    return pl.pallas_call(
        body, name="dn_bwd", grid=(N,),
        in_specs=[rev(3 * DN_W), rev(DN_W), rev(DN_W), rev(DN_W),
                  pl.BlockSpec((DN_HEADS, CHUNK, CHUNK), lambda n: (0, N - 1 - n, 0)),
                  pl.BlockSpec((1, DN_HEADS, DN_HD, DN_HD), lambda n: (N - 1 - n, 0, 0, 0)), rev(DN_W)],
        out_specs=[rev(3 * DN_W), rev(DN_W), rev(DN_W)],
        out_shape=[_sds((S, 3 * DN_W), F32), _sds((S, DN_W), F32), _sds((S, DN_W), F32)],
        scratch_shapes=[pltpu.VMEM((DN_HEADS, DN_HD, DN_HD), F32)],
        compiler_params=_cp(1),
    )(qkv, bb, gcb, glb, tinv_all, sall, do)


def _dn_prep_bwd_a(p, dqkv, dbb, dgb, conv_w, a_row, dtb_row, tm=256):
    S, PC = p.shape
    W3 = 3 * DN_W
    nhalo = tm // 8

    def body(x_ref, halo_ref, w_ref, ba_ref, a_ref, dtb_ref, dqkv_ref, dbb_ref, dgb_ref,
             dc_ref, dba_ref, dal_ref, ddt_ref):
        i = pl.program_id(0)

        @pl.when(i == 0)
        def _():
            dal_ref[...] = jnp.zeros_like(dal_ref)
            ddt_ref[...] = jnp.zeros_like(ddt_ref)

        halo = jnp.where(i > 0, halo_ref[...], 0.0)
        xf = jnp.concatenate([halo, x_ref[...]], axis=0)
        acc = jnp.zeros((tm, W3), F32)
        for k in range(CONV_K):
            sh = CONV_K - 1 - k
            xs = xf if sh == 0 else pltpu.roll(xf, sh, 0)
            acc = acc + xs[8:, :] * w_ref[k:k + 1, :]
        s = _silu(acc)
        ds_act = _dsilu(acc)
        for gi in range(2 * DN_HEADS):
            sl = slice(gi * LANES, (gi + 1) * LANES)
            sg = s[:, sl]
            rinv = lax.rsqrt(jnp.sum(sg * sg, axis=1, keepdims=True) + EPS)
            nh = sg * rinv
            dn = dqkv_ref[:, sl] * (DN_HD ** -0.5 if gi < DN_HEADS else 1.0)
            dsg = rinv * (dn - nh * jnp.sum(dn * nh, axis=1, keepdims=True))
            dc_ref[:, sl] = dsg * ds_act[:, sl]
        dc_ref[:, 2 * DN_W:] = dqkv_ref[:, 2 * DN_W:] * ds_act[:, 2 * DN_W:]

        ba = ba_ref[...]
        beta = _sigmoid(ba)
        ea = jnp.exp(a_ref[...])
        pre = ba + dtb_ref[...]
        g = -ea * _softplus(pre)
        lr, lc = _iota2((DN_W, LANES), 0), _iota2((DN_W, LANES), 1)
        pick_b = jnp.where(lc == lr // LANES, 1.0 / LANES, 0.0)
        pick_g = jnp.where(lc == lr // LANES + DN_HEADS, 1.0 / LANES, 0.0)
        dbeta = _mm_xl(dbb_ref[...], pick_b)
        dg = _mm_xl(dgb_ref[...], pick_g)
        lane = _iota2((1, LANES), 1)
        da = dg * (-ea) * _sigmoid(pre)
        dba_ref[...] = jnp.where(lane < DN_HEADS, dbeta * beta * (1.0 - beta),
                                 jnp.where(lane < 2 * DN_HEADS, da, 0.0)).astype(dba_ref.dtype)
        dal_ref[...] += jnp.sum(dg * g, axis=0, keepdims=True)
        ddt_ref[...] += jnp.sum(da, axis=0, keepdims=True)

    return pl.pallas_call(
        body, name="dn_prep_bwd_a", grid=(S // tm,),
        in_specs=[_rb(tm, W3, 0), pl.BlockSpec((8, W3), lambda i: (jnp.maximum(i * nhalo - 1, 0), 0)),
                  _fs((CONV_K, W3)), _rb(tm, LANES, (PC - LANES) // LANES), _fs((1, LANES)), _fs((1, LANES)),
                  _rb(tm, W3), _rb(tm, DN_W), _rb(tm, DN_W)],
        out_specs=[_rb(tm, W3), _rb(tm, LANES), _fs((1, LANES)), _fs((1, LANES))],
        out_shape=[_sds((S, W3), F32), _sds((S, LANES), _MXU_DTYPE), _sds((1, LANES), F32), _sds((1, LANES), F32)],
        compiler_params=_cp(1),
    )(p, p, conv_w, p, a_row, dtb_row, dqkv, dbb, dgb)


def _dn_prep_bwd_b(p, dc, conv_w, tm=256):
    S = p.shape[0]
    W3 = 3 * DN_W
    nhalo = tm // 8
    nblk = S // tm

    def body(x_ref, xh_ref, dc_ref, dch_ref, w_ref, dx_ref, dw_ref):
        i = pl.program_id(0)

        @pl.when(i == 0)
        def _():
            dw_ref[...] = jnp.zeros_like(dw_ref)

        dcv = dc_ref[...]
        xf = jnp.concatenate([jnp.where(i > 0, xh_ref[...], 0.0), x_ref[...]], axis=0)
        df = jnp.concatenate([dcv, jnp.where(i < nblk - 1, dch_ref[...], 0.0)], axis=0)
        acc = jnp.zeros((tm, W3), F32)
        for k in range(CONV_K):
            sh = CONV_K - 1 - k
            xs = xf if sh == 0 else pltpu.roll(xf, sh, 0)
            dw_ref[k:k + 1, :] += jnp.sum(dcv * xs[8:, :], axis=0, keepdims=True)
            ds = df if sh == 0 else pltpu.roll(df, tm + 8 - sh, 0)
            acc = acc + ds[:tm, :] * w_ref[k:k + 1, :]
        dx_ref[...] = acc.astype(dx_ref.dtype)

    return pl.pallas_call(
        body, name="dn_prep_bwd_b", grid=(nblk,),
        in_specs=[_rb(tm, W3, 0), pl.BlockSpec((8, W3), lambda i: (jnp.maximum(i * nhalo - 1, 0), 0)),
                  _rb(tm, W3), pl.BlockSpec((8, W3), lambda i: (jnp.minimum((i + 1) * nhalo, S // 8 - 1), 0)),
                  _fs((CONV_K, W3))],
        out_specs=[_rb(tm, W3), _fs((CONV_K, W3))],
        out_shape=[_sds((S, W3), _MXU_DTYPE), _sds((CONV_K, W3), F32)], compiler_params=_cp(1),
    )(p, p, dc, dc, conv_w)


def _gate(o_att, o_dn, p, gn, tm=256):
    S = p.shape[0]

    def body(oa_ref, zs_ref, od_ref, zd_ref, gn_ref, osb_ref, odn_ref):
        osb_ref[...] = (oa_ref[...] * _silu(zs_ref[...])).astype(osb_ref.dtype)
        for h in range(DN_HEADS):
            sl = slice(h * LANES, (h + 1) * LANES)
            o = od_ref[:, sl]
            r = lax.rsqrt(jnp.mean(o * o, axis=1, keepdims=True) + EPS)
            odn_ref[:, sl] = (o * r * gn_ref[...] * _silu(zd_ref[:, sl])).astype(odn_ref.dtype)

    return pl.pallas_call(
        body, name="gate", grid=(S // tm,),
        in_specs=[_rb(tm, SB_W), _rb(tm, SB_W, C_SB_Z // SB_W), _rb(tm, DN_W), _rb(tm, DN_W, C_DN_Z // DN_W),
                  _fs((1, LANES))],
        out_specs=[_rb(tm, SB_W), _rb(tm, DN_W)],
        out_shape=[_sds((S, SB_W), _MXU_DTYPE), _sds((S, DN_W), _MXU_DTYPE)], compiler_params=_cp(1),
    )(o_att, p, o_dn, p, gn)


def _gate_bwd(db_sb, db_dn, wb_sb, wb_dn, o_att, o_dn, p, gn, tm=256):
    S = p.shape[0]
    D = db_sb.shape[1]

    def body(dbs_ref, dbd_ref, ws_ref, wd_ref, oa_ref, zs_ref, od_ref, zd_ref, gn_ref,
             doa_ref, dzs_ref, dod_ref, dzd_ref, dgn_ref):
        @pl.when(pl.program_id(0) == 0)
        def _():
            dgn_ref[...] = jnp.zeros_like(dgn_ref)

        do_sb = _mm(dbs_ref[...], ws_ref[...], _NT)
        zs = zs_ref[...]
        doa_ref[...] = do_sb * _silu(zs)
        dzs_ref[...] = (do_sb * oa_ref[...] * _dsilu(zs)).astype(dzs_ref.dtype)
        do_dnn = _mm(dbd_ref[...], wd_ref[...], _NT)
        gnv = gn_ref[...]
        for h in range(DN_HEADS):
            sl = slice(h * LANES, (h + 1) * LANES)
            o, z, dout = od_ref[:, sl], zd_ref[:, sl], do_dnn[:, sl]
            r = lax.rsqrt(jnp.mean(o * o, axis=1, keepdims=True) + EPS)
            oh = o * r
            sz = _silu(z)
            dzd_ref[:, sl] = (dout * oh * gnv * _dsilu(z)).astype(dzd_ref.dtype)
            dgn_ref[...] += jnp.sum(dout * sz * oh, axis=0, keepdims=True)
            doh = dout * gnv * sz
            dod_ref[:, sl] = r * (doh - oh * jnp.mean(doh * oh, axis=1, keepdims=True))

    return pl.pallas_call(
        body, name="gate_bwd", grid=(S // tm,),
        in_specs=[_rb(tm, D), _rb(tm, D), _fs((SB_W, D)), _fs((DN_W, D)), _rb(tm, SB_W),
                  _rb(tm, SB_W, C_SB_Z // SB_W), _rb(tm, DN_W), _rb(tm, DN_W, C_DN_Z // DN_W), _fs((1, LANES))],
        out_specs=[_rb(tm, SB_W), _rb(tm, SB_W), _rb(tm, DN_W), _rb(tm, DN_W), _fs((1, LANES))],
        out_shape=[_sds((S, SB_W), F32), _sds((S, SB_W), _MXU_DTYPE), _sds((S, DN_W), F32),
                   _sds((S, DN_W), _MXU_DTYPE), _sds((1, LANES), F32)],
        compiler_params=_cp(1),
    )(db_sb, db_dn, wb_sb, wb_dn, o_att, p, o_dn, p, gn)


def _branch(o_sb, o_dnn, wb_sb, wb_dn, p, D, tm=256):
    S = p.shape[0]

    def body(os_ref, od_ref, ws_ref, wd_ref, ms_ref, md_ref, y_ref, bs_ref, bd_ref):
        bs = _mm(os_ref[...], ws_ref[...])
        bdn = _mm(od_ref[...], wd_ref[...])
        bs_ref[...] = bs
        bd_ref[...] = bdn
        y_ref[...] = (_sigmoid(ms_ref[...]) * bs + _sigmoid(md_ref[...]) * bdn).astype(y_ref.dtype)

    return pl.pallas_call(
        body, name="branch", grid=(S // tm,),
        in_specs=[_rb(tm, SB_W), _rb(tm, DN_W), _fs((SB_W, D)), _fs((DN_W, D)),
                  _rb(tm, D, C_MG // D), _rb(tm, D, C_MG // D + 1)],
        out_specs=[_rb(tm, D), _rb(tm, D), _rb(tm, D)],
        out_shape=[_sds((S, D), _MXU_DTYPE), _sds((S, D), F32), _sds((S, D), F32)], compiler_params=_cp(1),
    )(o_sb, o_dnn, wb_sb, wb_dn, p, p)


def _out_proj(x, y, w_out, gate, tm=256):
    S, D = x.shape

    def body(x_ref, y_ref, w_ref, g_ref, xn_ref, out_ref):
        out = _mm(y_ref[...], w_ref[...])
        out_ref[...] = out
        xn_ref[...] = x_ref[...] + g_ref[...] * out

    return pl.pallas_call(
        body, name="out_proj", grid=(S // tm,),
        in_specs=[_rb(tm, D), _rb(tm, D), _fs((D, D)), _fs((1, D))],
        out_specs=[_rb(tm, D), _rb(tm, D)],
        out_shape=[_sds((S, D), F32), _sds((S, D), F32)], compiler_params=_cp(1),
    )(x, y, w_out, gate)


def _out_bwd(dxn, out, gate, w_out, p, b_sb, b_dn, tm=256):
    S, D = dxn.shape

    def body(dxn_ref, out_ref, g_ref, w_ref, ms_ref, md_ref, bs_ref, bd_ref,
             dout_ref, dbs_ref, dbd_ref, dm_ref, dgate_ref):
        @pl.when(pl.program_id(0) == 0)
        def _():
            dgate_ref[...] = jnp.zeros_like(dgate_ref)

        dxv = dxn_ref[...]
        dgate_ref[...] += jnp.sum(dxv * out_ref[...], axis=0, keepdims=True)
        dout = (g_ref[...] * dxv).astype(dout_ref.dtype)
        dout_ref[...] = dout
        dy = _mm(dout, w_ref[...], _NT)
        s1, s2 = _sigmoid(ms_ref[...]), _sigmoid(md_ref[...])
        dbs_ref[...] = (dy * s1).astype(dbs_ref.dtype)
        dbd_ref[...] = (dy * s2).astype(dbd_ref.dtype)
        dm_ref[:, :D] = (dy * bs_ref[...] * s1 * (1.0 - s1)).astype(dm_ref.dtype)
        dm_ref[:, D:] = (dy * bd_ref[...] * s2 * (1.0 - s2)).astype(dm_ref.dtype)

    return pl.pallas_call(
        body, name="out_bwd", grid=(S // tm,),
        in_specs=[_rb(tm, D), _rb(tm, D), _fs((1, D)), _fs((D, D)), _rb(tm, D, C_MG // D),
                  _rb(tm, D, C_MG // D + 1), _rb(tm, D), _rb(tm, D)],
        out_specs=[_rb(tm, D), _rb(tm, D), _rb(tm, D), _rb(tm, 2 * D), _fs((1, D))],
        out_shape=[_sds((S, D), _MXU_DTYPE)] * 3 + [_sds((S, 2 * D), _MXU_DTYPE), _sds((1, D), F32)],
        compiler_params=_cp(1),
    )(dxn, out, gate, w_out, p, p, b_sb, b_dn)


def _loss_head(xf, target, tm=256):
    S, D = xf.shape

    def body(x_ref, t_ref, dy_ref, loss_ref):
        @pl.when(pl.program_id(0) == 0)
        def _():
            loss_ref[...] = jnp.zeros_like(loss_ref)

        e = x_ref[...] - t_ref[...]
        dy_ref[...] = e * (1.0 / D)
        row = jnp.sum(e * e, axis=1, keepdims=True) * (1.0 / D)
        loss_ref[...] += 0.5 * jnp.sum(row, axis=0, keepdims=True)

    return pl.pallas_call(
        body, name="loss_head", grid=(S // tm,),
        in_specs=[_rb(tm, D), _rb(tm, D)], out_specs=[_rb(tm, D), _fs((1, LANES))],
        out_shape=[_sds((S, D), F32), _sds((1, LANES), F32)], compiler_params=_cp(1),
    )(xf, target)


def _ada_fwd(c_all, ada_w, ada_b_sh):
    L, D, n = ada_w.shape
    B = c_all.shape[0]

    def body(c_ref, w_ref, b_ref, o_ref):
        sc = _silu(c_ref[...])
        o_ref[0] = _mm(sc, w_ref[0]) + b_ref[0]

    return pl.pallas_call(
        body, name="ada_fwd", grid=(L,),
        in_specs=[_fs((B, D)), pl.BlockSpec((1, D, n), lambda l: (l, 0, 0)), pl.BlockSpec((1, 1, n), lambda l: (l, 0, 0))],
        out_specs=pl.BlockSpec((1, B, n), lambda l: (l, 0, 0)),
        out_shape=_sds((L, B, n), F32), compiler_params=_cp(1),
    )(c_all, ada_w, ada_b_sh)


def _ada_bwd(c_all_t, dmod_sh):
    D, B = c_all_t.shape
    L, _, n = dmod_sh.shape

    def body(c_ref, d_ref, o_ref):
        acc = jnp.zeros((D, n), F32)
        for b in range(B):
            acc = acc + _silu(c_ref[:, b:b + 1]) * d_ref[0, b:b + 1, :]
        o_ref[0] = acc

    return pl.pallas_call(
        body, name="ada_bwd", grid=(L,),
        in_specs=[_fs((D, B)), pl.BlockSpec((1, B, n), lambda l: (l, 0, 0))],
        out_specs=pl.BlockSpec((1, D, n), lambda l: (l, 0, 0)),
        out_shape=_sds((L, D, n), F32), compiler_params=_cp(1),
    )(c_all_t, dmod_sh)


def _sum_parts(name, parts):
    P, R, C = parts.shape
    tr = _pick(R, max(16, min(512, (1 << 19) // (P * C))), 16) if R % 16 == 0 else R

    def body(p_ref, o_ref):
        acc = p_ref[0].astype(F32)
        for k in range(1, P):
            acc = acc + p_ref[k].astype(F32)
        o_ref[...] = acc

    return pl.pallas_call(
        body, name=name, grid=(R // tr,),
        in_specs=[pl.BlockSpec((P, tr, C), lambda i: (0, i, 0))], out_specs=_rb(tr, C),
        out_shape=_sds((R, C), F32), compiler_params=_cp(1),
    )(parts)


def _adamw(name, w, g, m, v):
    R, C = w.shape
    tr = _pick(R, 256, 8) if R % 8 == 0 else R
    c1 = 1.0 - ADAM_B1 ** ADAM_STEP
    c2 = 1.0 - ADAM_B2 ** ADAM_STEP

    def body(w_ref, g_ref, m_ref, v_ref, d_ref, mo_ref, vo_ref):
        gv = g_ref[...]
        mn = ADAM_B1 * m_ref[...] + (1.0 - ADAM_B1) * gv
        vn = ADAM_B2 * v_ref[...] + (1.0 - ADAM_B2) * (gv * gv)
        mo_ref[...] = mn
        vo_ref[...] = vn
        d_ref[...] = -ADAM_LR * ((mn / c1) / (jnp.sqrt(vn / c2) + ADAM_EPS) + ADAM_WD * w_ref[...])

    spec = _rb(tr, C)
    return pl.pallas_call(
        body, name=name, grid=(R // tr,),
        in_specs=[spec] * 4, out_specs=[spec] * 3, out_shape=[_sds((R, C), F32)] * 3, compiler_params=_cp(1),
    )(w, g, m, v)


def _ag_small(name, blk):
    R, C = blk.shape

    def body(x_ref, out_ref, send_sems, recv_sems, local_sem):
        x, y, c = lax.axis_index("x"), lax.axis_index("y"), lax.axis_index("c")
        me, sibling = (x, y, c), (x, y, 1 - c)
        chips = [(1 - x, y), (x, 1 - y), (1 - x, 1 - y)]

        def rows(px, py, pc):
            return out_ref.at[pl.ds((4 * px + 2 * py + pc) * R, R), :]

        def copy(k, block, to, src=None):
            return pltpu.make_async_remote_copy(
                src_ref=rows(*block) if src is None else src, dst_ref=rows(*block),
                send_sem=send_sems.at[k], recv_sem=recv_sems.at[k], device_id=to, device_id_type=MESH)

        mine = pltpu.make_async_copy(x_ref, rows(*me), local_sem)
        mine.start()
        first = [copy(0, me, sibling, src=x_ref)]
        first += [copy(1 + j, me, (*chip, c), src=x_ref) for j, chip in enumerate(chips)]
        for cp in first:
            cp.start()
        passed = [copy(4 + j, (*chip, c), sibling) for j, chip in enumerate(chips)]
        for j, chip in enumerate(chips):
            copy(1 + j, (*chip, c), me).wait_recv()
            passed[j].start()
        copy(0, sibling, me).wait_recv()
        for j, chip in enumerate(chips):
            copy(4 + j, (*chip, 1 - c), me).wait_recv()
        for cp in first + passed:
            cp.wait_send()
        mine.wait()

    return pl.pallas_call(
        body, name=name, out_shape=_sds((8 * R, C), blk.dtype),
        in_specs=[pl.BlockSpec(memory_space=pltpu.VMEM)], out_specs=pl.BlockSpec(memory_space=pltpu.VMEM),
        scratch_shapes=[pltpu.SemaphoreType.DMA((7,)), pltpu.SemaphoreType.DMA((7,)), pltpu.SemaphoreType.DMA],
    )(blk)


def _row_chunks(ts, row_axis):
    pieces = []
    for t, a in enumerate(ts):
        rows = a.shape[row_axis]
        n = 4 if rows >= 1024 else 1
        pieces += [(t, i * (rows // n), rows // n) for i in range(n)]
    return pieces


def _ag_weights(ts):
    nt = len(ts)
    pieces = _row_chunks(ts, 1)
    NP = len(pieces)

    def body(*refs):
        w, out = refs[:nt], refs[nt:2 * nt]
        send_sems, recv_sems, local_sems = refs[2 * nt:]
        x, y, c = lax.axis_index("x"), lax.axis_index("y"), lax.axis_index("c")
        me, sibling = (x, y, c), (x, y, 1 - c)
        mine = 2 * x + y
        chips = [(1 - x, y), (x, 1 - y), (1 - x, 1 - y)]

        def blk(t, shard, layer, r0, nr):
            return out[t].at[shard, layer, r0:r0 + nr, :]

        def copy(k, dst, to, src=None):
            return pltpu.make_async_remote_copy(
                src_ref=dst if src is None else src, dst_ref=dst, send_sem=send_sems.at[k], recv_sem=recv_sems.at[k],
                device_id=to, device_id_type=MESH)

        own = [pltpu.make_async_copy(w[t], out[t].at[mine], local_sems.at[t]) for t in range(nt)]
        for cp in own:
            cp.start()
        sent = []
        for j, chip in enumerate(chips):
            for pi, (t, r0, nr) in enumerate(pieces):
                sent.append(copy(j * NP + pi, blk(t, mine, c, r0, nr), (*chip, c), src=w[t].at[c, r0:r0 + nr, :]))
                sent[-1].start()
        for j, chip in enumerate(chips):
            theirs = 2 * chip[0] + chip[1]
            for pi, (t, r0, nr) in enumerate(pieces):
                copy(j * NP + pi, blk(t, theirs, c, r0, nr), me).wait_recv()
                sent.append(copy((3 + j) * NP + pi, blk(t, theirs, c, r0, nr), sibling))
                sent[-1].start()
        for j, chip in enumerate(chips):
            theirs = 2 * chip[0] + chip[1]
            for pi, (t, r0, nr) in enumerate(pieces):
                copy((3 + j) * NP + pi, blk(t, theirs, 1 - c, r0, nr), me).wait_recv()
        for cp in sent:
            cp.wait_send()
        for cp in own:
            cp.wait()

    hbm = pl.BlockSpec(memory_space=pl.ANY)
    return pl.pallas_call(
        body, name="ag_weights", out_shape=[_sds((4,) + a.shape, a.dtype) for a in ts],
        in_specs=[hbm] * nt, out_specs=[hbm] * nt,
        scratch_shapes=[pltpu.SemaphoreType.DMA((6 * NP,)), pltpu.SemaphoreType.DMA((6 * NP,)),
                        pltpu.SemaphoreType.DMA((nt,))],
    )(*ts)


def _grad_exchange(ts):
    nt = len(ts)
    pieces = _row_chunks(ts, 2)
    NP = len(pieces)

    def body(*refs):
        src, out = refs[:nt], refs[nt:2 * nt]
        send_sems, recv_sems, local_sems = refs[2 * nt:]
        x, y, c = lax.axis_index("x"), lax.axis_index("y"), lax.axis_index("c")
        me = 4 * x + 2 * y + c
        own = [pltpu.make_async_copy(src[t].at[c, 2 * x + y], out[t].at[me], local_sems.at[t]) for t in range(nt)]
        for cp in own:
            cp.start()
        sent, peers = [], []
        for k in range(1, 8):
            px = 1 - x if k & 4 else x
            py = 1 - y if k & 2 else y
            pc = 1 - c if k & 1 else c
            peers.append(4 * px + 2 * py + pc)
            for pi, (t, r0, nr) in enumerate(pieces):
                idx = (k - 1) * NP + pi
                sent.append(pltpu.make_async_remote_copy(
                    src_ref=src[t].at[pc, 2 * px + py, r0:r0 + nr, :], dst_ref=out[t].at[me, r0:r0 + nr, :],
                    send_sem=send_sems.at[idx], recv_sem=recv_sems.at[idx], device_id=(px, py, pc),
                    device_id_type=MESH))
                sent[-1].start()
        for k, peer in enumerate(peers):
            for pi, (t, r0, nr) in enumerate(pieces):
                idx = k * NP + pi
                pltpu.make_async_remote_copy(
                    src_ref=out[t].at[peer, r0:r0 + nr, :], dst_ref=out[t].at[peer, r0:r0 + nr, :],
                    send_sem=send_sems.at[idx], recv_sem=recv_sems.at[idx], device_id=(x, y, c),
                    device_id_type=MESH).wait_recv()
        for cp in sent:
            cp.wait_send()
        for cp in own:
            cp.wait()

    hbm = pl.BlockSpec(memory_space=pl.ANY)
    return pl.pallas_call(
        body, name="grad_exchange", out_shape=[_sds((8,) + a.shape[2:], a.dtype) for a in ts],
        in_specs=[hbm] * nt, out_specs=[hbm] * nt,
        scratch_shapes=[pltpu.SemaphoreType.DMA((7 * NP,)), pltpu.SemaphoreType.DMA((7 * NP,)),
                        pltpu.SemaphoreType.DMA((nt,))],
    )(*ts)


def _sibling_join(ts):
    nt = len(ts)
    pieces = _row_chunks(ts, 0)
    NP = len(pieces)

    def body(*refs):
        src, out = refs[:nt], refs[nt:2 * nt]
        send_sems, recv_sems, local_sems = refs[2 * nt:]
        x, y, c = lax.axis_index("x"), lax.axis_index("y"), lax.axis_index("c")
        own = [pltpu.make_async_copy(src[t], out[t].at[c], local_sems.at[t]) for t in range(nt)]
        for cp in own:
            cp.start()
        sent = []
        for pi, (t, r0, nr) in enumerate(pieces):
            sent.append(pltpu.make_async_remote_copy(
                src_ref=src[t].at[r0:r0 + nr, :], dst_ref=out[t].at[c, r0:r0 + nr, :], send_sem=send_sems.at[pi],
                recv_sem=recv_sems.at[pi], device_id=(x, y, 1 - c), device_id_type=MESH))
            sent[-1].start()
        for pi, (t, r0, nr) in enumerate(pieces):
            pltpu.make_async_remote_copy(
                src_ref=src[t].at[r0:r0 + nr, :], dst_ref=out[t].at[1 - c, r0:r0 + nr, :], send_sem=send_sems.at[pi],
                recv_sem=recv_sems.at[pi], device_id=(x, y, c), device_id_type=MESH).wait_recv()
        for cp in sent:
            cp.wait_send()
        for cp in own:
            cp.wait()

    vmem = pl.BlockSpec(memory_space=pltpu.VMEM)
    return pl.pallas_call(
        body, name="sibling_join", out_shape=[_sds((2,) + a.shape, a.dtype) for a in ts],
        in_specs=[vmem] * nt, out_specs=[vmem] * nt,
        scratch_shapes=[pltpu.SemaphoreType.DMA((NP,)), pltpu.SemaphoreType.DMA((NP,)),
                        pltpu.SemaphoreType.DMA((nt,))],
        compiler_params=pltpu.CompilerParams(vmem_limit_bytes=_VMEM_LIMIT),
    )(*ts)


def _layer_fwd(x, shift, scale, gate, lw):
    D = x.shape[1]
    h = _norm_mod(x, lw["norm_g"], scale, shift)
    p = _matmul("in_proj", h, lw["w_cat"], "nn", F32, tn_cap=896)
    qn, kn = _sb_prep(p, lw["gq_t"], lw["gk_t"])
    o_att = _sb_fwd(qn, kn, p)
    qkv, bb, gcb, glb = _dn_prep(p, lw["conv_w"], lw["a_row"], lw["dtb_row"])
    o_dn, tinv, sall = _dn_fwd(qkv, bb, gcb, glb)
    o_sb, o_dnn = _gate(o_att, o_dn, p, lw["gn"])
    y, b_sb, b_dn = _branch(o_sb, o_dnn, lw["wb_sb"], lw["wb_dn"], p, D)
    x_next, out = _out_proj(x, y, lw["w_out"], gate)
    res = dict(x=x, h=h, p=p, qn=qn, kn=kn, o_att=o_att, qkv=qkv, bb=bb, gcb=gcb, glb=glb, o_dn=o_dn,
               tinv=tinv, sall=sall, o_sb=o_sb, o_dnn=o_dnn, y=y, b_sb=b_sb, b_dn=b_dn, out=out,
               shift=shift, scale=scale, gate=gate)
    return x_next, res


def _layer_bwd(dxn, res, lw):
    p = res["p"]
    dout, db_sb, db_dn, dm, dgate = _out_bwd(dxn, res["out"], res["gate"], lw["w_out"], p, res["b_sb"], res["b_dn"])
    dw_out = _matmul("dw_out", res["y"], dout, "tn", _MXU_DTYPE)
    dwb_sb = _matmul("dwb_sb", res["o_sb"], db_sb, "tn", _MXU_DTYPE)
    dwb_dn = _matmul("dwb_dn", res["o_dnn"], db_dn, "tn", _MXU_DTYPE)
    do_att, dz_sb, do_dn, dz_dn, dgn = _gate_bwd(db_sb, db_dn, lw["wb_sb"], lw["wb_dn"], res["o_att"], res["o_dn"],
                                                  p, lw["gn"])
    dqn, dkn, dv = _sb_bwd(res["qn"], res["kn"], p, do_att)
    dq_sb, dk_sb, dgq, dgk = _sb_prep_bwd(p, dqn, dkn, lw["gq_t"], lw["gk_t"])
    dqkv, dbb, dgb = _dn_bwd(res["qkv"], res["bb"], res["gcb"], res["glb"], res["tinv"], res["sall"], do_dn)
    dc, dp_ba, dal, ddt = _dn_prep_bwd_a(p, dqkv, dbb, dgb, lw["conv_w"], lw["a_row"], lw["dtb_row"])
    dp_dn, dconv = _dn_prep_bwd_b(p, dc, lw["conv_w"])
    dp = jnp.concatenate([dp_dn, dz_dn, dq_sb, dk_sb, dv.astype(_MXU_DTYPE), dz_sb, dm, dp_ba], axis=1)
    dh = _matmul("dh", dp, lw["w_cat"], "nt", F32, tk_cap=896)
    dw_cat = _matmul("dw_cat", res["h"], dp, "tn", _MXU_DTYPE, tm_cap=1024, tn_cap=896, tk_cap=512)
    dx, dshift, dscale, dnorm_g = _norm_mod_bwd(res["x"], dh, dxn, lw["norm_g"], res["scale"])
    small = dict(dmod=jnp.concatenate([dshift, dscale, dgate], axis=1)[0], norm_g=dnorm_g[0],
                 sb_q_g=dgq.reshape(SB_HEADS, SB_HD).sum(0), sb_k_g=dgk.reshape(SB_HEADS, SB_HD).sum(0),
                 conv_w=dconv, dn_a_log=dal[0, DN_HEADS:2 * DN_HEADS], dn_dt_bias=ddt[0, DN_HEADS:2 * DN_HEADS],
                 dn_norm_g=dgn[0])
    big = dict(w_cat=dw_cat, w_branch_sb=dwb_sb, w_branch_dn=dwb_dn, w_out=dw_out)
    return dx, small, big


def _cat_cols(w, D):
    return jnp.concatenate([w[:, 2048:4096], w[:, 0:2048], w[:, 4104:4104 + 2 * D], w[:, 4096:4104],
                            jnp.zeros((w.shape[0], LANES - 8), w.dtype)], axis=1)


def _uncat_cols(g, D):
    return jnp.concatenate([g[:, 2048:4096], g[:, 0:2048], g[:, 4096 + 2 * D:4096 + 2 * D + 8],
                            g[:, 4096:4096 + 2 * D]], axis=1)


def _flat_pack(arrs, mult):
    flat = jnp.concatenate([a.reshape(-1) for a in arrs])
    n = flat.shape[0]
    pad = (-n) % mult
    if pad:
        flat = jnp.concatenate([flat, jnp.zeros((pad,), flat.dtype)])
    return flat.reshape(-1, LANES)


def _flat_unpack(flat, shapes):
    flat = flat.reshape(-1)
    out, off = [], 0
    for s in shapes:
        n = math.prod(s)
        out.append(flat[off:off + n].reshape(s))
        off += n
    return out


BIG = ("w_in", "w_branch_sb", "w_branch_dn", "w_out")
SMALL = ("ada_b", "norm_g", "sb_q_g", "sb_k_g", "conv_w", "dn_a_log", "dn_dt_bias", "dn_norm_g")


def kernel(x, c, ada_w, ada_b, norm_g, w_in, sb_q_g, sb_k_g, conv_w, dn_a_log, dn_dt_bias, dn_norm_g, w_branch_sb, w_branch_dn, w_out, loss_target, m_ada_w, m_ada_b, m_norm_g, m_w_in, m_sb_q_g, m_sb_k_g, m_conv_w, m_dn_a_log, m_dn_dt_bias, m_dn_norm_g, m_w_branch_sb, m_w_branch_dn, m_w_out, v_ada_w, v_ada_b, v_norm_g, v_w_in, v_sb_q_g, v_sb_k_g, v_conv_w, v_dn_a_log, v_dn_dt_bias, v_dn_norm_g, v_w_branch_sb, v_w_branch_dn, v_w_out):
    W = dict(ada_w=ada_w, ada_b=ada_b, norm_g=norm_g, w_in=w_in, sb_q_g=sb_q_g, sb_k_g=sb_k_g, conv_w=conv_w,
             dn_a_log=dn_a_log, dn_dt_bias=dn_dt_bias, dn_norm_g=dn_norm_g, w_branch_sb=w_branch_sb,
             w_branch_dn=w_branch_dn, w_out=w_out)
    M = dict(ada_w=m_ada_w, ada_b=m_ada_b, norm_g=m_norm_g, w_in=m_w_in, sb_q_g=m_sb_q_g, sb_k_g=m_sb_k_g,
             conv_w=m_conv_w, dn_a_log=m_dn_a_log, dn_dt_bias=m_dn_dt_bias, dn_norm_g=m_dn_norm_g,
             w_branch_sb=m_w_branch_sb, w_branch_dn=m_w_branch_dn, w_out=m_w_out)
    V = dict(ada_w=v_ada_w, ada_b=v_ada_b, norm_g=v_norm_g, w_in=v_w_in, sb_q_g=v_sb_q_g, sb_k_g=v_sb_k_g,
             conv_w=v_conv_w, dn_a_log=v_dn_a_log, dn_dt_bias=v_dn_dt_bias, dn_norm_g=v_dn_norm_g,
             w_branch_sb=v_w_branch_sb, w_branch_dn=v_w_branch_dn, w_out=v_w_out)
    L = ada_w.shape[0]
    S, D = x.shape[1], x.shape[2]
    ix, iy, ic = lax.axis_index("x"), lax.axis_index("y"), lax.axis_index("c")
    shard = 2 * ix + iy
    me = 2 * shard + ic
    n_ada = ada_w.shape[2]
    n_in = w_in.shape[2]
    n_conv = conv_w.shape[2]
    n_br = w_branch_sb.shape[2]
    n_out = w_out.shape[1]

    g_in, g_bs, g_bd, g_out = _ag_weights([W[n].astype(_MXU_DTYPE) for n in BIG])
    w_in_f = jnp.concatenate([g_in[s] for s in range(4)], axis=2)
    wb_sb_f = jnp.concatenate([g_bs[s] for s in range(4)], axis=2)
    wb_dn_f = jnp.concatenate([g_bd[s] for s in range(4)], axis=2)
    w_out_f = jnp.concatenate([g_out[s] for s in range(4)], axis=1)

    g1 = _ag_small("ag_c_conv", _flat_pack([c, conv_w], LANES * 8))
    g1 = g1.reshape(8, -1)
    c_all = g1[:, :D]
    conv_parts = g1[:, D:D + L * CONV_K * n_conv].reshape(4, 2, L, CONV_K, n_conv)[:, 0]
    conv_full = jnp.concatenate([conv_parts[s] for s in range(4)], axis=2)
    ada_b_sh = lax.dynamic_slice_in_dim(ada_b, shard * n_ada, n_ada, axis=1)[:, None, :]
    mod_sh = _ada_fwd(c_all, ada_w, ada_b_sh)
    g2 = _ag_small("ag_mod", _flat_pack([mod_sh], LANES * 8)).reshape(8, -1)
    mod_parts = g2[:, :L * 8 * n_ada].reshape(4, 2, L, 8, n_ada)[:, 0]
    mod_all = jnp.concatenate([mod_parts[s] for s in range(4)], axis=2)
    mod = lax.dynamic_index_in_dim(mod_all, me, axis=1, keepdims=False)

    def layer_weights(l):
        pad_lo = jnp.zeros((DN_HEADS,), F32)
        pad_hi = jnp.zeros((LANES - 2 * DN_HEADS,), F32)
        return dict(
            norm_g=norm_g[l][None, :], w_cat=_cat_cols(w_in_f[l], D),
            gq_t=jnp.tile(sb_q_g[l], SB_HEADS)[None, :], gk_t=jnp.tile(sb_k_g[l], SB_HEADS)[None, :],
            conv_w=conv_full[l],
            a_row=jnp.concatenate([pad_lo, dn_a_log[l], pad_hi])[None, :],
            dtb_row=jnp.concatenate([pad_lo, dn_dt_bias[l], pad_hi])[None, :],
            gn=dn_norm_g[l][None, :], wb_sb=wb_sb_f[l], wb_dn=wb_dn_f[l], w_out=w_out_f[l])

    xs = x[0]
    lws, ress = [], []
    for l in range(L):
        lw = layer_weights(l)
        xs, res = _layer_fwd(xs, mod[l, None, 0:D], mod[l, None, D:2 * D], mod[l, None, 2 * D:3 * D], lw)
        lws.append(lw)
        ress.append(res)
    dxs, loss_row = _loss_head(xs, loss_target[0])
    loss = lax.psum(loss_row[0, 0], ("x", "y", "c"))
    smalls, bigs = [None] * L, [None] * L
    for l in reversed(range(L)):
        dxs, smalls[l], bigs[l] = _layer_bwd(dxs, ress[l], lws[l])
    grad_x = dxs[None]

    small_names = ("dmod",) + SMALL[1:]
    small_pack = _flat_pack([jnp.stack([smalls[l][n] for l in range(L)]) for n in small_names], LANES * 8)
    g3 = _ag_small("ag_small_grads", small_pack)
    R3 = small_pack.shape[0]
    g3 = g3.reshape(8, R3, LANES)
    small_sum = _sum_parts("sum_small", g3)
    small_shapes = [(L, 3 * D), (L, D), (L, SB_HD), (L, SB_HD), (L, CONV_K, 3 * DN_W), (L, DN_HEADS), (L, DN_HEADS),
                    (L, DN_HD)]
    sg = dict(zip(small_names, _flat_unpack(small_sum, small_shapes)))
    G = dict(ada_b=sg["dmod"], norm_g=sg["norm_g"], sb_q_g=sg["sb_q_g"], sb_k_g=sg["sb_k_g"],
             conv_w=lax.dynamic_slice_in_dim(sg["conv_w"], shard * n_conv, n_conv, axis=2),
             dn_a_log=sg["dn_a_log"], dn_dt_bias=sg["dn_dt_bias"], dn_norm_g=sg["dn_norm_g"])
    dmod_all = g3.reshape(8, -1)[:, :L * 3 * D].reshape(8, L, 3 * D)
    dmod_sh = lax.dynamic_slice_in_dim(dmod_all, shard * n_ada, n_ada, axis=2).transpose(1, 0, 2)
    G["ada_w"] = _ada_bwd(c_all.T, dmod_sh)

    def by_col_shard(name, n_sh):
        g = jnp.stack([bigs[l][name] for l in range(L)])
        return g.reshape(L, g.shape[1], 4, n_sh).transpose(0, 2, 1, 3)

    send = [jnp.stack([_uncat_cols(bigs[l]["w_cat"], D) for l in range(L)]).reshape(L, D, 4, n_in).transpose(0, 2, 1, 3),
            by_col_shard("w_branch_sb", n_br), by_col_shard("w_branch_dn", n_br),
            jnp.stack([bigs[l]["w_out"] for l in range(L)]).reshape(L, 4, n_out, D)]
    got = _grad_exchange(send)
    mine = [_sum_parts("sum_" + n, g) for n, g in zip(BIG, got)]
    for n, g in zip(BIG, _sibling_join(mine)):
        G[n] = g

    delta, new_m, new_v = {}, {}, {}
    for n in ("ada_w",) + BIG:
        sh = W[n].shape
        two = (sh[0] * sh[1], sh[2])
        d, mo, vo = _adamw("adamw_" + n, W[n].reshape(two), G[n].reshape(two), M[n].reshape(two), V[n].reshape(two))
        delta[n], new_m[n], new_v[n] = d.reshape(sh), mo.reshape(sh), vo.reshape(sh)
    sm_shapes = [W[n].shape for n in SMALL]
    d, mo, vo = _adamw("adamw_small", _flat_pack([W[n] for n in SMALL], LANES * 8),
                       _flat_pack([G[n] for n in SMALL], LANES * 8), _flat_pack([M[n] for n in SMALL], LANES * 8),
                       _flat_pack([V[n] for n in SMALL], LANES * 8))
    for n, dd, mm, vv in zip(SMALL, _flat_unpack(d, sm_shapes), _flat_unpack(mo, sm_shapes),
                             _flat_unpack(vo, sm_shapes)):
        delta[n], new_m[n], new_v[n] = dd, mm, vv

    order = ("ada_w", "ada_b", "norm_g", "w_in", "sb_q_g", "sb_k_g", "conv_w", "dn_a_log", "dn_dt_bias", "dn_norm_g",
             "w_branch_sb", "w_branch_dn", "w_out")
    return (loss, grad_x, *[G[n] for n in order], *[delta[n] for n in order], *[new_m[n] for n in order],
            *[new_v[n] for n in order])
```

```python
import math

import jax
import jax.numpy as jnp
from jax import lax
from jax.experimental import pallas as pl
from jax.experimental.pallas import tpu as pltpu

F32 = jnp.float32
BF16 = jnp.bfloat16
_MXU_DTYPE = BF16
_VMEM_LIMIT = 48 * 1024 * 1024
LANES = 128

EPS = 1e-6
SB_HEADS, SB_HD, SB_W = 8, 64, 512
DN_HEADS, DN_HD, DN_W = 4, 128, 512
CONV_K = 4
CHUNK = 64
QB = 256
_SB_DEAD = 104.0
ADAM_LR, ADAM_B1, ADAM_B2, ADAM_EPS, ADAM_WD, ADAM_STEP = 0.001, 0.9, 0.999, 1e-08, 0.01, 10

C_DN_QKV, C_DN_Z, C_SB_Q, C_SB_K, C_SB_V, C_SB_Z, C_MG = 0, 1536, 2048, 2560, 3072, 3584, 4096

_NN = (((1,), (0,)), ((), ()))
_NT = (((1,), (1,)), ((), ()))
_TN = (((0,), (0,)), ((), ()))
_BNN = (((2,), (1,)), ((0,), (0,)))
_BNT = (((2,), (2,)), ((0,), (0,)))
_BTN = (((1,), (1,)), ((0,), (0,)))
MESH = pl.DeviceIdType.MESH


def _sds(shape, dtype):
    return jax.ShapeDtypeStruct(shape, dtype)


def _cp(n):
    return pltpu.CompilerParams(dimension_semantics=("arbitrary",) * n, vmem_limit_bytes=_VMEM_LIMIT)


def _rb(tm, w, cb=0):
    return pl.BlockSpec((tm, w), lambda i: (i, cb))


def _fs(shape):
    nd = len(shape)
    return pl.BlockSpec(shape, lambda i: (0,) * nd)


def _dg(a, b, dims):
    return lax.dot_general(a, b, dims, preferred_element_type=F32)


def _mm(a, b, dims=_NN):
    return _dg(a.astype(_MXU_DTYPE), b.astype(_MXU_DTYPE), dims)


def _split3(x):
    hi = x.astype(BF16)
    r = x - hi.astype(F32)
    mid = r.astype(BF16)
    lo = (r - mid.astype(F32)).astype(BF16)
    return hi, mid, lo


def _mm_xl(x, const, dims=_NN):
    cb = const.astype(BF16)
    hi, mid, lo = _split3(x)
    return _dg(hi, cb, dims) + _dg(mid, cb, dims) + _dg(lo, cb, dims)


def _mm_xl2(x, const, dims=_NN):
    cb = const.astype(BF16)
    hi = x.astype(BF16)
    lo = (x - hi.astype(F32)).astype(BF16)
    return _dg(hi, cb, dims) + _dg(lo, cb, dims)


def _mm_xr(const, x, dims=_NN):
    cb = const.astype(BF16)
    hi, mid, lo = _split3(x)
    return _dg(cb, hi, dims) + _dg(cb, mid, dims) + _dg(cb, lo, dims)


def _mm3(a, b, dims=_NN):
    ah, am, _ = _split3(a)
    bh, bm, _ = _split3(b)
    return _dg(ah, bh, dims) + (_dg(ah, bm, dims) + _dg(am, bh, dims))


def _sigmoid(z):
    return 1.0 / (1.0 + jnp.exp(-z))


def _silu(z):
    return z * _sigmoid(z)


def _dsilu(z):
    s = _sigmoid(z)
    return s * (1.0 + z * (1.0 - s))


def _softplus(z):
    return jnp.maximum(z, 0.0) + jnp.log(1.0 + jnp.exp(-jnp.abs(z)))


def _iota2(shape, dim):
    return lax.broadcasted_iota(jnp.int32, shape, dim)


def _pick(n, cap, mult):
    best = None
    for t in range(mult, min(n, cap) + 1, mult):
        if n % t == 0:
            best = t
    assert best is not None, (n, cap, mult)
    return best


def _matmul(name, a, b, form, out_dtype, tm_cap=512, tn_cap=1024, tk_cap=1024):
    if form == "nn":
        (M, K), (_, N) = a.shape, b.shape
    elif form == "nt":
        (M, K), (N, _) = a.shape, b.shape
    else:
        (K, M), (_, N) = a.shape, b.shape
    tm = _pick(M, tm_cap, 128 if form == "tn" else 8)
    tn = _pick(N, tn_cap, 128)
    tk = _pick(K, tk_cap, 128)
    nk = K // tk
    dims = {"nn": _NN, "nt": _NT, "tn": _TN}[form]
    if form == "nn":
        a_spec = pl.BlockSpec((tm, tk), lambda i, j, k: (i, k))
        b_spec = pl.BlockSpec((tk, tn), lambda i, j, k: (k, j))
    elif form == "nt":
        a_spec = pl.BlockSpec((tm, tk), lambda i, j, k: (i, k))
        b_spec = pl.BlockSpec((tn, tk), lambda i, j, k: (j, k))
    else:
        a_spec = pl.BlockSpec((tk, tm), lambda i, j, k: (k, i))
        b_spec = pl.BlockSpec((tk, tn), lambda i, j, k: (k, j))

    def body(a_ref, b_ref, o_ref, acc_ref):
        k = pl.program_id(2)

        @pl.when(k == 0)
        def _():
            acc_ref[...] = jnp.zeros_like(acc_ref)

        acc_ref[...] += _mm(a_ref[...], b_ref[...], dims)

        @pl.when(k == nk - 1)
        def _():
            o_ref[...] = acc_ref[...].astype(o_ref.dtype)

    return pl.pallas_call(
        body, name=name, grid=(M // tm, N // tn, nk),
        in_specs=[a_spec, b_spec],
        out_specs=pl.BlockSpec((tm, tn), lambda i, j, k: (i, j)),
        out_shape=_sds((M, N), out_dtype),
        scratch_shapes=[pltpu.VMEM((tm, tn), F32)],
        compiler_params=_cp(3),
    )(a, b)


def _norm_mod(x, g, scale, shift, tm=256):
    S, D = x.shape

    def body(x_ref, g_ref, sc_ref, sh_ref, h_ref):
        xv = x_ref[...]
        r = lax.rsqrt(jnp.mean(xv * xv, axis=1, keepdims=True) + EPS)
        h_ref[...] = ((xv * r * g_ref[...]) * (1.0 + sc_ref[...]) + sh_ref[...]).astype(h_ref.dtype)

    return pl.pallas_call(
        body, name="norm_mod", grid=(S // tm,),
        in_specs=[_rb(tm, D), _fs((1, D)), _fs((1, D)), _fs((1, D))],
        out_specs=_rb(tm, D), out_shape=_sds((S, D), _MXU_DTYPE), compiler_params=_cp(1),
    )(x, g, scale, shift)


def _norm_mod_bwd(x, dh, dxn, g, scale, tm=256):
    S, D = x.shape

    def body(x_ref, dh_ref, dxn_ref, g_ref, sc_ref, dx_ref, dsh_ref, dsc_ref, dg_ref):
        @pl.when(pl.program_id(0) == 0)
        def _():
            dsh_ref[...] = jnp.zeros_like(dsh_ref)
            dsc_ref[...] = jnp.zeros_like(dsc_ref)
            dg_ref[...] = jnp.zeros_like(dg_ref)

        xv, dhv, gv = x_ref[...], dh_ref[...], g_ref[...]
        r = lax.rsqrt(jnp.mean(xv * xv, axis=1, keepdims=True) + EPS)
        xh = xv * r
        one_sc = 1.0 + sc_ref[...]
        dsh_ref[...] += jnp.sum(dhv, axis=0, keepdims=True)
        dsc_ref[...] += jnp.sum(dhv * xh * gv, axis=0, keepdims=True)
        dg_ref[...] += jnp.sum(dhv * one_sc * xh, axis=0, keepdims=True)
        dxh = dhv * (gv * one_sc)
        dx_ref[...] = r * (dxh - xh * jnp.mean(dxh * xh, axis=1, keepdims=True)) + dxn_ref[...]

    return pl.pallas_call(
        body, name="norm_mod_bwd", grid=(S // tm,),
        in_specs=[_rb(tm, D), _rb(tm, D), _rb(tm, D), _fs((1, D)), _fs((1, D))],
        out_specs=[_rb(tm, D), _fs((1, D)), _fs((1, D)), _fs((1, D))],
        out_shape=[_sds((S, D), F32)] + [_sds((1, D), F32)] * 3, compiler_params=_cp(1),
    )(x, dh, dxn, g, scale)


def _head_sum_matrix():
    r = jnp.arange(SB_W)
    return (r[:, None] // SB_HD == r[None, :] // SB_HD).astype(BF16)


def _sb_prep(p, gq_t, gk_t, tm=256):
    S = p.shape[0]
    bd = _head_sum_matrix()

    def body(q_ref, k_ref, gq_ref, gk_ref, bd_ref, qn_ref, kn_ref):
        for src, g_ref, dst in ((q_ref, gq_ref, qn_ref), (k_ref, gk_ref, kn_ref)):
            v = src[...]
            ms = _mm_xl(v * v, bd_ref[...]) * (1.0 / SB_HD)
            dst[...] = (v * lax.rsqrt(ms + EPS) * g_ref[...]).astype(dst.dtype)

    return pl.pallas_call(
        body, name="sb_prep", grid=(S // tm,),
        in_specs=[_rb(tm, SB_W, C_SB_Q // SB_W), _rb(tm, SB_W, C_SB_K // SB_W),
                  _fs((1, SB_W)), _fs((1, SB_W)), _fs((SB_W, SB_W))],
        out_specs=[_rb(tm, SB_W), _rb(tm, SB_W)],
        out_shape=[_sds((S, SB_W), _MXU_DTYPE)] * 2, compiler_params=_cp(1),
    )(p, p, gq_t, gk_t, bd)


def _sb_prep_bwd(p, dqn, dkn, gq_t, gk_t, tm=256):
    S = p.shape[0]
    bd = _head_sum_matrix()

    def body(q_ref, k_ref, dqn_ref, dkn_ref, gq_ref, gk_ref, bd_ref, dq_ref, dk_ref, dgq_ref, dgk_ref):
        @pl.when(pl.program_id(0) == 0)
        def _():
            dgq_ref[...] = jnp.zeros_like(dgq_ref)
            dgk_ref[...] = jnp.zeros_like(dgk_ref)

        for src, dn_ref, g_ref, dst, dg_ref in ((q_ref, dqn_ref, gq_ref, dq_ref, dgq_ref),
                                                (k_ref, dkn_ref, gk_ref, dk_ref, dgk_ref)):
            v, dn = src[...], dn_ref[...]
            r = lax.rsqrt(_mm_xl(v * v, bd_ref[...]) * (1.0 / SB_HD) + EPS)
            vh = v * r
            dg_ref[...] += jnp.sum(dn * vh, axis=0, keepdims=True)
            dvh = dn * g_ref[...]
            m = _mm_xl(dvh * vh, bd_ref[...]) * (1.0 / SB_HD)
            dst[...] = (r * (dvh - vh * m)).astype(dst.dtype)

    return pl.pallas_call(
        body, name="sb_prep_bwd", grid=(S // tm,),
        in_specs=[_rb(tm, SB_W, C_SB_Q // SB_W), _rb(tm, SB_W, C_SB_K // SB_W), _rb(tm, SB_W), _rb(tm, SB_W),
                  _fs((1, SB_W)), _fs((1, SB_W)), _fs((SB_W, SB_W))],
        out_specs=[_rb(tm, SB_W), _rb(tm, SB_W), _fs((1, SB_W)), _fs((1, SB_W))],
        out_shape=[_sds((S, SB_W), _MXU_DTYPE)] * 2 + [_sds((1, SB_W), F32)] * 2, compiler_params=_cp(1),
    )(p, p, dqn, dkn, gq_t, gk_t, bd)


def _sb_consts():
    r, c = _iota2((QB, QB), 0), _iota2((QB, QB), 1)
    lane = _iota2((1, LANES), 1)
    return r, c, lane


def _sb_fwd(qn, kn, p):
    S = qn.shape[0]
    scale = 1.0 / math.sqrt(SB_HD)

    def body(q_ref, k_ref, v_ref, o_ref):
        i = pl.program_id(1)
        r, c, lane = _sb_consts()
        u_gt = (r > c).astype(BF16)
        strict = jnp.concatenate([c < r, c < r], axis=0)
        q = q_ref[...]
        mask0 = (lane // SB_HD) == 0
        zero = jnp.zeros_like(q)
        qh = jnp.concatenate([jnp.where(mask0, q, zero), jnp.where(mask0, zero, q)], axis=0)

        def block(off, carry, diagonal):
            o, run = carry
            kj = k_ref[pl.ds(off, QB), :]
            vj = v_ref[pl.ds(off, QB), :].astype(_MXU_DTYPE)
            z = _mm(qh, kj, _NT) * scale
            sp = _softplus(z)
            sp_m = jnp.where(strict, sp, 0.0) if diagonal else sp
            later = _mm_xl2(sp_m, u_gt)
            w = jnp.exp((z - sp) - later - run)
            if diagonal:
                w = jnp.where(strict, w, 0.0)
            return o + _mm(w, vj), run + jnp.sum(sp_m, axis=1, keepdims=True)

        init = (jnp.zeros((2 * QB, LANES), F32), jnp.zeros((2 * QB, 1), F32))
        carry = block(pl.multiple_of(i * QB, QB), init, True)
        st = lax.while_loop(
            lambda st: jnp.logical_and(st[0] <= i, jnp.min(st[2]) < _SB_DEAD),
            lambda st: (st[0] + 1,) + block(pl.multiple_of((i - st[0]) * QB, QB), st[1:], False),
            (jnp.int32(1),) + carry)
        o_ref[...] = jnp.where(mask0, st[1][:QB], st[1][QB:])

    return pl.pallas_call(
        body, name="sb_fwd", grid=(SB_W // LANES, S // QB),
        in_specs=[pl.BlockSpec((QB, LANES), lambda hp, i: (i, hp)),
                  pl.BlockSpec((S, LANES), lambda hp, i: (0, hp)),
                  pl.BlockSpec((S, LANES), lambda hp, i: (0, C_SB_V // LANES + hp))],
        out_specs=pl.BlockSpec((QB, LANES), lambda hp, i: (i, hp)),
        out_shape=_sds((S, SB_W), F32), compiler_params=_cp(2),
    )(qn, kn, p)


def _sb_bwd(qn, kn, p, do):
    S = qn.shape[0]
    scale = 1.0 / math.sqrt(SB_HD)

    def body(q_ref, k_ref, v_ref, do_ref, dq_ref, dk_ref, dv_ref):
        i = pl.program_id(1)

        @pl.when(i == 0)
        def _():
            dk_ref[...] = jnp.zeros_like(dk_ref)
            dv_ref[...] = jnp.zeros_like(dv_ref)

        r, c, lane = _sb_consts()
        u_le = (r <= c).astype(BF16)
        u_lt = (r < c).astype(BF16)
        strict = jnp.concatenate([c < r, c < r], axis=0)
        q = q_ref[...]
        do = do_ref[...].astype(_MXU_DTYPE)
        mask0 = (lane // SB_HD) == 0
        zero, zero_do = jnp.zeros_like(q), jnp.zeros_like(do)
        qh = jnp.concatenate([jnp.where(mask0, q, zero), jnp.where(mask0, zero, q)], axis=0)
        doh = jnp.concatenate([jnp.where(mask0, do, zero_do), jnp.where(mask0, zero_do, do)], axis=0)

        def sums(off, run, diagonal):
            sp = _softplus(_mm(qh, k_ref[pl.ds(off, QB), :], _NT) * scale)
            if diagonal:
                sp = jnp.where(strict, sp, 0.0)
            return run + jnp.sum(sp, axis=1, keepdims=True)

        run = sums(pl.multiple_of(i * QB, QB), jnp.zeros((2 * QB, 1), F32), True)
        nb, tot = lax.while_loop(
            lambda st: jnp.logical_and(st[0] <= i, jnp.min(st[1]) < _SB_DEAD),
            lambda st: (st[0] + 1, sums(pl.multiple_of((i - st[0]) * QB, QB), st[1], False)),
            (jnp.int32(1), run))
        first = i + 1 - nb

        def block(off, carry, diagonal):
            dq, pre_sp, pre_e = carry
            kj = k_ref[pl.ds(off, QB), :]
            vj = v_ref[pl.ds(off, QB), :].astype(_MXU_DTYPE)
            z = _mm(qh, kj, _NT) * scale
            sp = _softplus(z)
            a = z - sp
            sp_m = jnp.where(strict, sp, 0.0) if diagonal else sp
            incl = _mm_xl2(sp_m, u_le)
            w = jnp.exp(a - ((tot - pre_sp) - incl))
            if diagonal:
                w = jnp.where(strict, w, 0.0)
            e = w * _mm(doh, vj, _NT)
            db = pre_e + _mm_xl2(e, u_lt)
            dz = (e - jnp.exp(a) * (e + db)) * scale
            if diagonal:
                dz = jnp.where(strict, dz, 0.0)
            dk_ref[pl.ds(off, QB), :] += _mm(dz, qh, _TN)
            dv_ref[pl.ds(off, QB), :] += _mm(w, doh, _TN)
            return (dq + _mm(dz, kj), pre_sp + jnp.sum(sp_m, axis=1, keepdims=True),
                    pre_e + jnp.sum(e, axis=1, keepdims=True))

        zero_col = jnp.zeros((2 * QB, 1), F32)
        init = (jnp.zeros((2 * QB, LANES), F32), zero_col, zero_col)
        carry = lax.fori_loop(first, i, lambda j, cr: block(pl.multiple_of(j * QB, QB), cr, False), init)
        carry = block(pl.multiple_of(i * QB, QB), carry, True)
        dq_ref[...] = jnp.where(mask0, carry[0][:QB], carry[0][QB:])

    blk = pl.BlockSpec((QB, LANES), lambda hp, i: (i, hp))
    full = pl.BlockSpec((S, LANES), lambda hp, i: (0, hp))
    return pl.pallas_call(
        body, name="sb_bwd", grid=(SB_W // LANES, S // QB),
        in_specs=[blk, full, pl.BlockSpec((S, LANES), lambda hp, i: (0, C_SB_V // LANES + hp)), blk],
        out_specs=[blk, full, full],
        out_shape=[_sds((S, SB_W), F32)] * 3, compiler_params=_cp(2),
    )(qn, kn, p, do)


def _dn_prep(p, conv_w, a_row, dtb_row, tm=256):
    S = p.shape[0]
    W3 = 3 * DN_W
    nhalo = tm // 8

    def body(x_ref, halo_ref, w_ref, ba_ref, a_ref, dtb_ref, qkv_ref, bb_ref, gc_ref, gl_ref):
        i = pl.program_id(0)
        halo = jnp.where(i > 0, halo_ref[...], 0.0)
        xf = jnp.concatenate([halo, x_ref[...]], axis=0)
        acc = jnp.zeros((tm, W3), F32)
        for k in range(CONV_K):
            sh = CONV_K - 1 - k
            xs = xf if sh == 0 else pltpu.roll(xf, sh, 0)
            acc = acc + xs[8:, :] * w_ref[k:k + 1, :]
        s = _silu(acc)
        for gi in range(2 * DN_HEADS):
            sl = slice(gi * LANES, (gi + 1) * LANES)
            sg = s[:, sl]
            rinv = lax.rsqrt(jnp.sum(sg * sg, axis=1, keepdims=True) + EPS)
            qkv_ref[:, sl] = sg * rinv * (DN_HD ** -0.5 if gi < DN_HEADS else 1.0)
        qkv_ref[:, 2 * DN_W:] = s[:, 2 * DN_W:]

        ba = ba_ref[...]
        beta = _sigmoid(ba)
        g = -jnp.exp(a_ref[...]) * _softplus(ba + dtb_ref[...])
        lr, lc = _iota2((LANES, DN_W), 0), _iota2((LANES, DN_W), 1)
        sel_b = (lr == lc // LANES).astype(BF16)
        sel_g = (lr == lc // LANES + DN_HEADS).astype(BF16)
        bb_ref[...] = _mm_xl(beta, sel_b)
        graw = _mm_xl(g, sel_g)
        rr, cc = _iota2((tm, tm), 0), _iota2((tm, tm), 1)
        tri = jnp.logical_and(rr >= cc, rr // CHUNK == cc // CHUNK).astype(BF16)
        gc = _mm_xr(tri, graw)
        last = (cc == (rr // CHUNK) * CHUNK + (CHUNK - 1)).astype(BF16)
        gc_ref[...] = gc
        gl_ref[...] = _mm_xr(last, gc)

    return pl.pallas_call(
        body, name="dn_prep", grid=(S // tm,),
        in_specs=[_rb(tm, W3, 0), pl.BlockSpec((8, W3), lambda i: (jnp.maximum(i * nhalo - 1, 0), 0)),
                  _fs((CONV_K, W3)), _rb(tm, LANES, (p.shape[1] - LANES) // LANES),
                  _fs((1, LANES)), _fs((1, LANES))],
        out_specs=[_rb(tm, W3), _rb(tm, DN_W), _rb(tm, DN_W), _rb(tm, DN_W)],
        out_shape=[_sds((S, W3), F32)] + [_sds((S, DN_W), F32)] * 3, compiler_params=_cp(1),
    )(p, p, conv_w, p, a_row, dtb_row)


def _heads(ref, base=0):
    return jnp.stack([ref[:, base + h * LANES:base + (h + 1) * LANES] for h in range(DN_HEADS)])


def _per_head(const):
    return jnp.broadcast_to(const[None], (DN_HEADS,) + const.shape)


def _dn_chunk_terms(q, k, v, beta, gc, gl):
    r, c = _iota2((CHUNK, CHUNK), 0), _iota2((CHUNK, CHUNK), 1)
    tril, strict = r >= c, r > c
    gcol = _mm_xl(gc, _per_head(jnp.full((LANES, CHUNK), 1.0 / LANES, F32)), _BNN)
    grow = _mm_xr(_per_head(jnp.full((CHUNK, LANES), 1.0 / LANES, F32)), gc, _BNT)
    dec = jnp.where(tril, jnp.exp(jnp.where(tril, gcol - grow, 0.0)), 0.0)
    gam = jnp.exp(gc)
    dlt = jnp.exp(gl - gc)
    kb, vb = k * beta, v * beta
    pm = _mm(kb, k, _BNT)
    qk = _mm(q, k, _BNT)
    m = jnp.where(strict, pm * dec, 0.0)
    a = jnp.where(tril, qk * dec, 0.0)
    return dict(tril=tril, strict=strict, dec=dec, gam=gam, dlt=dlt, kb=kb, vb=vb, m=m, a=a)


def _dn_fwd(qkv, bb, gcb, glb):
    S = qkv.shape[0]
    N = S // CHUNK

    def body(qkv_ref, bb_ref, gc_ref, gl_ref, o_ref, t_ref, sall_ref, s_scr):
        @pl.when(pl.program_id(0) == 0)
        def _():
            s_scr[...] = jnp.zeros_like(s_scr)

        r, c = _iota2((CHUNK, CHUNK), 0), _iota2((CHUNK, CHUNK), 1)
        eye = (r == c).astype(F32)
        q, k, v = _heads(qkv_ref), _heads(qkv_ref, DN_W), _heads(qkv_ref, 2 * DN_W)
        beta, gc, gl = _heads(bb_ref), _heads(gc_ref), _heads(gl_ref)
        s_prev = s_scr[...]
        sall_ref[0] = s_prev.astype(sall_ref.dtype)
        s0 = s_prev.astype(sall_ref.dtype).astype(F32)
        t = _dn_chunk_terms(q, k, v, beta, gc, gl)
        pw = -t["m"]
        tinv = eye + pw
        for _ in range(5):
            pw = _mm3(pw, pw, _BNN)
            tinv = tinv + _mm3(tinv, pw, _BNN)
        t_ref[...] = tinv
        u = _mm3(tinv, t["vb"], _BNN)
        w = _mm3(tinv, t["kb"] * t["gam"], _BNN)
        vn = u - _mm(w, s0, _BNN)
        o = _mm(q * t["gam"], s0, _BNN) + _mm(t["a"], vn, _BNN)
        for h in range(DN_HEADS):
            o_ref[:, h * LANES:(h + 1) * LANES] = o[h]
        egl = jnp.exp(jnp.concatenate([gl, gl], axis=1))
        s_scr[...] = s_prev * egl + _mm(k * t["dlt"], vn, _BTN)

    return pl.pallas_call(
        body, name="dn_fwd", grid=(N,),
        in_specs=[_rb(CHUNK, 3 * DN_W), _rb(CHUNK, DN_W), _rb(CHUNK, DN_W), _rb(CHUNK, DN_W)],
        out_specs=[_rb(CHUNK, DN_W), pl.BlockSpec((DN_HEADS, CHUNK, CHUNK), lambda n: (0, n, 0)),
                   pl.BlockSpec((1, DN_HEADS, DN_HD, DN_HD), lambda n: (n, 0, 0, 0))],
        out_shape=[_sds((S, DN_W), F32), _sds((DN_HEADS, S, CHUNK), F32),
                   _sds((N, DN_HEADS, DN_HD, DN_HD), _MXU_DTYPE)],
        scratch_shapes=[pltpu.VMEM((DN_HEADS, DN_HD, DN_HD), F32)],
        compiler_params=_cp(1),
    )(qkv, bb, gcb, glb)


def _dn_bwd(qkv, bb, gcb, glb, tinv_all, sall, do):
    S = qkv.shape[0]
    N = S // CHUNK

    def body(qkv_ref, bb_ref, gc_ref, gl_ref, t_ref, sall_ref, do_ref, dqkv_ref, dbb_ref, dg_ref, ds_scr):
        @pl.when(pl.program_id(0) == 0)
        def _():
            ds_scr[...] = jnp.zeros_like(ds_scr)

        r, c = _iota2((CHUNK, CHUNK), 0), _iota2((CHUNK, CHUNK), 1)
        eye = (r == c).astype(F32)
        u_ge = (c >= r).astype(F32)
        last_row = _iota2((CHUNK, LANES), 0) == CHUNK - 1
        eye_h, u_ge_h = _per_head(eye), _per_head(u_ge)
        q, k, v = _heads(qkv_ref), _heads(qkv_ref, DN_W), _heads(qkv_ref, 2 * DN_W)
        beta, gc, gl = _heads(bb_ref), _heads(gc_ref), _heads(gl_ref)
        tinv = t_ref[...]
        s0 = sall_ref[0].astype(F32)
        do = _heads(do_ref)
        ds1 = ds_scr[...]
        t = _dn_chunk_terms(q, k, v, beta, gc, gl)
        gam, dlt, kb, vb, dec = t["gam"], t["dlt"], t["kb"], t["vb"], t["dec"]
        kbg = kb * gam
        u = _mm3(tinv, vb, _BNN)
        w = _mm3(tinv, kbg, _BNN)
        vn = u - _mm(w, s0, _BNN)
        qg, kd = q * gam, k * dlt
        egl = jnp.exp(gl)
        egl2 = jnp.concatenate([egl, egl], axis=1)

        dvn = _mm(t["a"], do, _BTN) + _mm(kd, ds1, _BNN)
        da = jnp.where(t["tril"], _mm(do, vn, _BNT), 0.0)
        dqg = _mm(do, s0, _BNT)
        dkd = _mm(vn, ds1, _BNT)
        dw = -_mm(dvn, s0, _BNT)
        ds_scr[...] = _mm(qg, do, _BTN) + egl2 * ds1 - _mm(w, dvn, _BTN)
        tt = _mm_xr(eye_h, tinv, _BNT)
        dvb = _mm3(tt, dvn, _BNN)
        dkbg = _mm3(tt, dw, _BNN)
        dm = -jnp.where(t["strict"], _mm(dvb, u, _BNT) + _mm(dkbg, w, _BNT), 0.0)
        dpm = dm * dec
        dqk = da * dec
        dkb = dkbg * gam + _mm(dpm, k, _BNN)
        dk = dkd * dlt + _mm(dpm, kb, _BTN) + _mm(dqk, q, _BTN) + dkb * beta
        dq = dqg * gam + _mm(dqk, k, _BNN)
        dv = dvb * beta
        dbeta = jnp.sum(dkb * k, axis=2, keepdims=True) + jnp.sum(dvb * v, axis=2, keepdims=True)
        dgam = jnp.sum(dqg * q, axis=2, keepdims=True) + jnp.sum(dkbg * kb, axis=2, keepdims=True)
        ddlt = jnp.sum(dkd * k, axis=2, keepdims=True)
        xm = dm * t["m"] + da * t["a"]
        xt = _mm_xr(eye_h, xm, _BNT)
        dgc = (dgam * gam - ddlt * dlt + jnp.sum(xm, axis=2, keepdims=True) - jnp.sum(xt, axis=2, keepdims=True))
        dgl = jnp.sum(ddlt * dlt, axis=1, keepdims=True) + jnp.sum(
            jnp.sum(ds1 * s0, axis=2, keepdims=True), axis=1, keepdims=True) * jnp.max(egl, axis=1, keepdims=True)
        dgc = dgc + jnp.where(last_row, dgl, 0.0)
        dg = _mm_xr(u_ge_h, dgc, _BNN)
        for h in range(DN_HEADS):
            sl = slice(h * LANES, (h + 1) * LANES)
            dqkv_ref[:, sl] = dq[h]
            dqkv_ref[:, DN_W + h * LANES:DN_W + (h + 1) * LANES] = dk[h]
            dqkv_ref[:, 2 * DN_W + h * LANES:2 * DN_W + (h + 1) * LANES] = dv[h]
            dbb_ref[:, sl] = jnp.broadcast_to(dbeta[h], (CHUNK, LANES))
            dg_ref[:, sl] = dg[h]

    rev = lambda w: pl.BlockSpec((CHUNK, w), lambda n: (N - 1 - n, 0))
    return pl.pallas_call(
        body, name="dn_bwd", grid=(N,),
        in_specs=[rev(3 * DN_W), rev(DN_W), rev(DN_W), rev(DN_W),
                  pl.BlockSpec((DN_HEADS, CHUNK, CHUNK), lambda n: (0, N - 1 - n, 0)),
                  pl.BlockSpec((1, DN_HEADS, DN_HD, DN_HD), lambda n: (N - 1 - n, 0, 0, 0)), rev(DN_W)],
        out_specs=[rev(3 * DN_W), rev(DN_W), rev(DN_W)],
        out_shape=[_sds((S, 3 * DN_W), F32), _sds((S, DN_W), F32), _sds((S, DN_W), F32)],
        scratch_shapes=[pltpu.VMEM((DN_HEADS, DN_HD, DN_HD), F32)],
        compiler_params=_cp(1),
    )(qkv, bb, gcb, glb, tinv_all, sall, do)


def _dn_prep_bwd_a(p, dqkv, dbb, dgb, conv_w, a_row, dtb_row, tm=256):
    S, PC = p.shape
    W3 = 3 * DN_W
    nhalo = tm // 8

    def body(x_ref, halo_ref, w_ref, ba_ref, a_ref, dtb_ref, dqkv_ref, dbb_ref, dgb_ref,
             dc_ref, dba_ref, dal_ref, ddt_ref):
        i = pl.program_id(0)

        @pl.when(i == 0)
        def _():
            dal_ref[...] = jnp.zeros_like(dal_ref)
            ddt_ref[...] = jnp.zeros_like(ddt_ref)

        halo = jnp.where(i > 0, halo_ref[...], 0.0)
        xf = jnp.concatenate([halo, x_ref[...]], axis=0)
        acc = jnp.zeros((tm, W3), F32)
        for k in range(CONV_K):
            sh = CONV_K - 1 - k
            xs = xf if sh == 0 else pltpu.roll(xf, sh, 0)
            acc = acc + xs[8:, :] * w_ref[k:k + 1, :]
        s = _silu(acc)
        ds_act = _dsilu(acc)
        for gi in range(2 * DN_HEADS):
            sl = slice(gi * LANES, (gi + 1) * LANES)
            sg = s[:, sl]
            rinv = lax.rsqrt(jnp.sum(sg * sg, axis=1, keepdims=True) + EPS)
            nh = sg * rinv
            dn = dqkv_ref[:, sl] * (DN_HD ** -0.5 if gi < DN_HEADS else 1.0)
            dsg = rinv * (dn - nh * jnp.sum(dn * nh, axis=1, keepdims=True))
            dc_ref[:, sl] = dsg * ds_act[:, sl]
        dc_ref[:, 2 * DN_W:] = dqkv_ref[:, 2 * DN_W:] * ds_act[:, 2 * DN_W:]

        ba = ba_ref[...]
        beta = _sigmoid(ba)
        ea = jnp.exp(a_ref[...])
        pre = ba + dtb_ref[...]
        g = -ea * _softplus(pre)
        lr, lc = _iota2((DN_W, LANES), 0), _iota2((DN_W, LANES), 1)
        pick_b = jnp.where(lc == lr // LANES, 1.0 / LANES, 0.0)
        pick_g = jnp.where(lc == lr // LANES + DN_HEADS, 1.0 / LANES, 0.0)
        dbeta = _mm_xl(dbb_ref[...], pick_b)
        dg = _mm_xl(dgb_ref[...], pick_g)
        lane = _iota2((1, LANES), 1)
        da = dg * (-ea) * _sigmoid(pre)
        dba_ref[...] = jnp.where(lane < DN_HEADS, dbeta * beta * (1.0 - beta),
                                 jnp.where(lane < 2 * DN_HEADS, da, 0.0)).astype(dba_ref.dtype)
        dal_ref[...] += jnp.sum(dg * g, axis=0, keepdims=True)
        ddt_ref[...] += jnp.sum(da, axis=0, keepdims=True)

    return pl.pallas_call(
        body, name="dn_prep_bwd_a", grid=(S // tm,),
        in_specs=[_rb(tm, W3, 0), pl.BlockSpec((8, W3), lambda i: (jnp.maximum(i * nhalo - 1, 0), 0)),
                  _fs((CONV_K, W3)), _rb(tm, LANES, (PC - LANES) // LANES), _fs((1, LANES)), _fs((1, LANES)),
                  _rb(tm, W3), _rb(tm, DN_W), _rb(tm, DN_W)],
        out_specs=[_rb(tm, W3), _rb(tm, LANES), _fs((1, LANES)), _fs((1, LANES))],
        out_shape=[_sds((S, W3), F32), _sds((S, LANES), _MXU_DTYPE), _sds((1, LANES), F32), _sds((1, LANES), F32)],
        compiler_params=_cp(1),
    )(p, p, conv_w, p, a_row, dtb_row, dqkv, dbb, dgb)


def _dn_prep_bwd_b(p, dc, conv_w, tm=256):
    S = p.shape[0]
    W3 = 3 * DN_W
    nhalo = tm // 8
    nblk = S // tm

    def body(x_ref, xh_ref, dc_ref, dch_ref, w_ref, dx_ref, dw_ref):
        i = pl.program_id(0)

        @pl.when(i == 0)
        def _():
            dw_ref[...] = jnp.zeros_like(dw_ref)

        dcv = dc_ref[...]
        xf = jnp.concatenate([jnp.where(i > 0, xh_ref[...], 0.0), x_ref[...]], axis=0)
        df = jnp.concatenate([dcv, jnp.where(i < nblk - 1, dch_ref[...], 0.0)], axis=0)
        acc = jnp.zeros((tm, W3), F32)
        for k in range(CONV_K):
            sh = CONV_K - 1 - k
            xs = xf if sh == 0 else pltpu.roll(xf, sh, 0)
            dw_ref[k:k + 1, :] += jnp.sum(dcv * xs[8:, :], axis=0, keepdims=True)
            ds = df if sh == 0 else pltpu.roll(df, tm + 8 - sh, 0)
            acc = acc + ds[:tm, :] * w_ref[k:k + 1, :]
        dx_ref[...] = acc.astype(dx_ref.dtype)

    return pl.pallas_call(
        body, name="dn_prep_bwd_b", grid=(nblk,),
        in_specs=[_rb(tm, W3, 0), pl.BlockSpec((8, W3), lambda i: (jnp.maximum(i * nhalo - 1, 0), 0)),
                  _rb(tm, W3), pl.BlockSpec((8, W3), lambda i: (jnp.minimum((i + 1) * nhalo, S // 8 - 1), 0)),
                  _fs((CONV_K, W3))],
        out_specs=[_rb(tm, W3), _fs((CONV_K, W3))],
        out_shape=[_sds((S, W3), _MXU_DTYPE), _sds((CONV_K, W3), F32)], compiler_params=_cp(1),
    )(p, p, dc, dc, conv_w)


def _gate(o_att, o_dn, p, gn, tm=256):
    S = p.shape[0]

    def body(oa_ref, zs_ref, od_ref, zd_ref, gn_ref, osb_ref, odn_ref):
        osb_ref[...] = (oa_ref[...] * _silu(zs_ref[...])).astype(osb_ref.dtype)
        for h in range(DN_HEADS):
            sl = slice(h * LANES, (h + 1) * LANES)
            o = od_ref[:, sl]
            r = lax.rsqrt(jnp.mean(o * o, axis=1, keepdims=True) + EPS)
            odn_ref[:, sl] = (o * r * gn_ref[...] * _silu(zd_ref[:, sl])).astype(odn_ref.dtype)

    return pl.pallas_call(
        body, name="gate", grid=(S // tm,),
        in_specs=[_rb(tm, SB_W), _rb(tm, SB_W, C_SB_Z // SB_W), _rb(tm, DN_W), _rb(tm, DN_W, C_DN_Z // DN_W),
                  _fs((1, LANES))],
        out_specs=[_rb(tm, SB_W), _rb(tm, DN_W)],
        out_shape=[_sds((S, SB_W), _MXU_DTYPE), _sds((S, DN_W), _MXU_DTYPE)], compiler_params=_cp(1),
    )(o_att, p, o_dn, p, gn)


def _gate_bwd(db_sb, db_dn, wb_sb, wb_dn, o_att, o_dn, p, gn, tm=256):
    S = p.shape[0]
    D = db_sb.shape[1]

    def body(dbs_ref, dbd_ref, ws_ref, wd_ref, oa_ref, zs_ref, od_ref, zd_ref, gn_ref,
             doa_ref, dzs_ref, dod_ref, dzd_ref, dgn_ref):
        @pl.when(pl.program_id(0) == 0)
        def _():
            dgn_ref[...] = jnp.zeros_like(dgn_ref)

        do_sb = _mm(dbs_ref[...], ws_ref[...], _NT)
        zs = zs_ref[...]
        doa_ref[...] = do_sb * _silu(zs)
        dzs_ref[...] = (do_sb * oa_ref[...] * _dsilu(zs)).astype(dzs_ref.dtype)
        do_dnn = _mm(dbd_ref[...], wd_ref[...], _NT)
        gnv = gn_ref[...]
        for h in range(DN_HEADS):
            sl = slice(h * LANES, (h + 1) * LANES)
            o, z, dout = od_ref[:, sl], zd_ref[:, sl], do_dnn[:, sl]
            r = lax.rsqrt(jnp.mean(o * o, axis=1, keepdims=True) + EPS)
            oh = o * r
            sz = _silu(z)
            dzd_ref[:, sl] = (dout * oh * gnv * _dsilu(z)).astype(dzd_ref.dtype)
            dgn_ref[...] += jnp.sum(dout * sz * oh, axis=0, keepdims=True)
            doh = dout * gnv * sz
            dod_ref[:, sl] = r * (doh - oh * jnp.mean(doh * oh, axis=1, keepdims=True))

    return pl.pallas_call(
        body, name="gate_bwd", grid=(S // tm,),
        in_specs=[_rb(tm, D), _rb(tm, D), _fs((SB_W, D)), _fs((DN_W, D)), _rb(tm, SB_W),
                  _rb(tm, SB_W, C_SB_Z // SB_W), _rb(tm, DN_W), _rb(tm, DN_W, C_DN_Z // DN_W), _fs((1, LANES))],
        out_specs=[_rb(tm, SB_W), _rb(tm, SB_W), _rb(tm, DN_W), _rb(tm, DN_W), _fs((1, LANES))],
        out_shape=[_sds((S, SB_W), F32), _sds((S, SB_W), _MXU_DTYPE), _sds((S, DN_W), F32),
                   _sds((S, DN_W), _MXU_DTYPE), _sds((1, LANES), F32)],
        compiler_params=_cp(1),
    )(db_sb, db_dn, wb_sb, wb_dn, o_att, p, o_dn, p, gn)


def _branch(o_sb, o_dnn, wb_sb, wb_dn, p, D, tm=256):
    S = p.shape[0]

    def body(os_ref, od_ref, ws_ref, wd_ref, ms_ref, md_ref, y_ref, bs_ref, bd_ref):
        bs = _mm(os_ref[...], ws_ref[...])
        bdn = _mm(od_ref[...], wd_ref[...])
        bs_ref[...] = bs
        bd_ref[...] = bdn
        y_ref[...] = (_sigmoid(ms_ref[...]) * bs + _sigmoid(md_ref[...]) * bdn).astype(y_ref.dtype)

    return pl.pallas_call(
        body, name="branch", grid=(S // tm,),
        in_specs=[_rb(tm, SB_W), _rb(tm, DN_W), _fs((SB_W, D)), _fs((DN_W, D)),
                  _rb(tm, D, C_MG // D), _rb(tm, D, C_MG // D + 1)],
        out_specs=[_rb(tm, D), _rb(tm, D), _rb(tm, D)],
        out_shape=[_sds((S, D), _MXU_DTYPE), _sds((S, D), F32), _sds((S, D), F32)], compiler_params=_cp(1),
    )(o_sb, o_dnn, wb_sb, wb_dn, p, p)


def _out_proj(x, y, w_out, gate, tm=256):
    S, D = x.shape

    def body(x_ref, y_ref, w_ref, g_ref, xn_ref, out_ref):
        out = _mm(y_ref[...], w_ref[...])
        out_ref[...] = out
        xn_ref[...] = x_ref[...] + g_ref[...] * out

    return pl.pallas_call(
        body, name="out_proj", grid=(S // tm,),
        in_specs=[_rb(tm, D), _rb(tm, D), _fs((D, D)), _fs((1, D))],
        out_specs=[_rb(tm, D), _rb(tm, D)],
        out_shape=[_sds((S, D), F32), _sds((S, D), F32)], compiler_params=_cp(1),
    )(x, y, w_out, gate)


def _out_bwd(dxn, out, gate, w_out, p, b_sb, b_dn, tm=256):
    S, D = dxn.shape

    def body(dxn_ref, out_ref, g_ref, w_ref, ms_ref, md_ref, bs_ref, bd_ref,
             dout_ref, dbs_ref, dbd_ref, dm_ref, dgate_ref):
        @pl.when(pl.program_id(0) == 0)
        def _():
            dgate_ref[...] = jnp.zeros_like(dgate_ref)

        dxv = dxn_ref[...]
        dgate_ref[...] += jnp.sum(dxv * out_ref[...], axis=0, keepdims=True)
        dout = (g_ref[...] * dxv).astype(dout_ref.dtype)
        dout_ref[...] = dout
        dy = _mm(dout, w_ref[...], _NT)
        s1, s2 = _sigmoid(ms_ref[...]), _sigmoid(md_ref[...])
        dbs_ref[...] = (dy * s1).astype(dbs_ref.dtype)
        dbd_ref[...] = (dy * s2).astype(dbd_ref.dtype)
        dm_ref[:, :D] = (dy * bs_ref[...] * s1 * (1.0 - s1)).astype(dm_ref.dtype)
        dm_ref[:, D:] = (dy * bd_ref[...] * s2 * (1.0 - s2)).astype(dm_ref.dtype)

    return pl.pallas_call(
        body, name="out_bwd", grid=(S // tm,),
        in_specs=[_rb(tm, D), _rb(tm, D), _fs((1, D)), _fs((D, D)), _rb(tm, D, C_MG // D),
                  _rb(tm, D, C_MG // D + 1), _rb(tm, D), _rb(tm, D)],
        out_specs=[_rb(tm, D), _rb(tm, D), _rb(tm, D), _rb(tm, 2 * D), _fs((1, D))],
        out_shape=[_sds((S, D), _MXU_DTYPE)] * 3 + [_sds((S, 2 * D), _MXU_DTYPE), _sds((1, D), F32)],
        compiler_params=_cp(1),
    )(dxn, out, gate, w_out, p, p, b_sb, b_dn)


def _loss_head(xf, target, tm=256):
    S, D = xf.shape

    def body(x_ref, t_ref, dy_ref, loss_ref):
        @pl.when(pl.program_id(0) == 0)
        def _():
            loss_ref[...] = jnp.zeros_like(loss_ref)

        e = x_ref[...] - t_ref[...]
        dy_ref[...] = e * (1.0 / D)
        row = jnp.sum(e * e, axis=1, keepdims=True) * (1.0 / D)
        loss_ref[...] += 0.5 * jnp.sum(row, axis=0, keepdims=True)

    return pl.pallas_call(
        body, name="loss_head", grid=(S // tm,),
        in_specs=[_rb(tm, D), _rb(tm, D)], out_specs=[_rb(tm, D), _fs((1, LANES))],
        out_shape=[_sds((S, D), F32), _sds((1, LANES), F32)], compiler_params=_cp(1),
    )(xf, target)


def _ada_fwd(c_all, ada_w, ada_b_sh):
    L, D, n = ada_w.shape
    B = c_all.shape[0]

    def body(c_ref, w_ref, b_ref, o_ref):
        sc = _silu(c_ref[...])
        o_ref[0] = _mm(sc, w_ref[0]) + b_ref[0]

    return pl.pallas_call(
        body, name="ada_fwd", grid=(L,),
        in_specs=[_fs((B, D)), pl.BlockSpec((1, D, n), lambda l: (l, 0, 0)), pl.BlockSpec((1, 1, n), lambda l: (l, 0, 0))],
        out_specs=pl.BlockSpec((1, B, n), lambda l: (l, 0, 0)),
        out_shape=_sds((L, B, n), F32), compiler_params=_cp(1),
    )(c_all, ada_w, ada_b_sh)


def _ada_bwd(c_all_t, dmod_sh):
    D, B = c_all_t.shape
    L, _, n = dmod_sh.shape

    def body(c_ref, d_ref, o_ref):
        acc = jnp.zeros((D, n), F32)
        for b in range(B):
            acc = acc + _silu(c_ref[:, b:b + 1]) * d_ref[0, b:b + 1, :]
        o_ref[0] = acc

    return pl.pallas_call(
        body, name="ada_bwd", grid=(L,),
        in_specs=[_fs((D, B)), pl.BlockSpec((1, B, n), lambda l: (l, 0, 0))],
        out_specs=pl.BlockSpec((1, D, n), lambda l: (l, 0, 0)),
        out_shape=_sds((L, D, n), F32), compiler_params=_cp(1),
    )(c_all_t, dmod_sh)


def _sum_parts(name, parts):
    P, R, C = parts.shape
    tr = _pick(R, max(16, min(512, (1 << 19) // (P * C))), 16) if R % 16 == 0 else R

    def body(p_ref, o_ref):
        acc = p_ref[0].astype(F32)
        for k in range(1, P):
            acc = acc + p_ref[k].astype(F32)
        o_ref[...] = acc

    return pl.pallas_call(
        body, name=name, grid=(R // tr,),
        in_specs=[pl.BlockSpec((P, tr, C), lambda i: (0, i, 0))], out_specs=_rb(tr, C),
        out_shape=_sds((R, C), F32), compiler_params=_cp(1),
    )(parts)


def _adamw(name, w, g, m, v):
    R, C = w.shape
    tr = _pick(R, 256, 8) if R % 8 == 0 else R
    c1 = 1.0 - ADAM_B1 ** ADAM_STEP
    c2 = 1.0 - ADAM_B2 ** ADAM_STEP

    def body(w_ref, g_ref, m_ref, v_ref, d_ref, mo_ref, vo_ref):
        gv = g_ref[...]
        mn = ADAM_B1 * m_ref[...] + (1.0 - ADAM_B1) * gv
        vn = ADAM_B2 * v_ref[...] + (1.0 - ADAM_B2) * (gv * gv)
        mo_ref[...] = mn
        vo_ref[...] = vn
        d_ref[...] = -ADAM_LR * ((mn / c1) / (jnp.sqrt(vn / c2) + ADAM_EPS) + ADAM_WD * w_ref[...])

    spec = _rb(tr, C)
    return pl.pallas_call(
        body, name=name, grid=(R // tr,),
        in_specs=[spec] * 4, out_specs=[spec] * 3, out_shape=[_sds((R, C), F32)] * 3, compiler_params=_cp(1),
    )(w, g, m, v)


def _ag_small(name, blk):
    R, C = blk.shape

    def body(x_ref, out_ref, send_sems, recv_sems, local_sem):
        x, y, c = lax.axis_index("x"), lax.axis_index("y"), lax.axis_index("c")
        me, sibling = (x, y, c), (x, y, 1 - c)
        chips = [(1 - x, y), (x, 1 - y), (1 - x, 1 - y)]

        def rows(px, py, pc):
            return out_ref.at[pl.ds((4 * px + 2 * py + pc) * R, R), :]

        def copy(k, block, to, src=None):
            return pltpu.make_async_remote_copy(
                src_ref=rows(*block) if src is None else src, dst_ref=rows(*block),
                send_sem=send_sems.at[k], recv_sem=recv_sems.at[k], device_id=to, device_id_type=MESH)

        mine = pltpu.make_async_copy(x_ref, rows(*me), local_sem)
        mine.start()
        first = [copy(0, me, sibling, src=x_ref)]
        first += [copy(1 + j, me, (*chip, c), src=x_ref) for j, chip in enumerate(chips)]
        for cp in first:
            cp.start()
        passed = [copy(4 + j, (*chip, c), sibling) for j, chip in enumerate(chips)]
        for j, chip in enumerate(chips):
            copy(1 + j, (*chip, c), me).wait_recv()
            passed[j].start()
        copy(0, sibling, me).wait_recv()
        for j, chip in enumerate(chips):
            copy(4 + j, (*chip, 1 - c), me).wait_recv()
        for cp in first + passed:
            cp.wait_send()
        mine.wait()

    return pl.pallas_call(
        body, name=name, out_shape=_sds((8 * R, C), blk.dtype),
        in_specs=[pl.BlockSpec(memory_space=pltpu.VMEM)], out_specs=pl.BlockSpec(memory_space=pltpu.VMEM),
        scratch_shapes=[pltpu.SemaphoreType.DMA((7,)), pltpu.SemaphoreType.DMA((7,)), pltpu.SemaphoreType.DMA],
    )(blk)


def _row_chunks(ts, row_axis):
    pieces = []
    for t, a in enumerate(ts):
        rows = a.shape[row_axis]
        n = 4 if rows >= 1024 else 1
        pieces += [(t, i * (rows // n), rows // n) for i in range(n)]
    return pieces


def _ag_weights(ts):
    nt = len(ts)
    pieces = _row_chunks(ts, 1)
    NP = len(pieces)

    def body(*refs):
        w, out = refs[:nt], refs[nt:2 * nt]
        send_sems, recv_sems, local_sems = refs[2 * nt:]
        x, y, c = lax.axis_index("x"), lax.axis_index("y"), lax.axis_index("c")
        me, sibling = (x, y, c), (x, y, 1 - c)
        mine = 2 * x + y
        chips = [(1 - x, y), (x, 1 - y), (1 - x, 1 - y)]

        def blk(t, shard, layer, r0, nr):
            return out[t].at[shard, layer, r0:r0 + nr, :]

        def copy(k, dst, to, src=None):
            return pltpu.make_async_remote_copy(
                src_ref=dst if src is None else src, dst_ref=dst, send_sem=send_sems.at[k], recv_sem=recv_sems.at[k],
                device_id=to, device_id_type=MESH)

        own = [pltpu.make_async_copy(w[t], out[t].at[mine], local_sems.at[t]) for t in range(nt)]
        for cp in own:
            cp.start()
        sent = []
        for j, chip in enumerate(chips):
            for pi, (t, r0, nr) in enumerate(pieces):
                sent.append(copy(j * NP + pi, blk(t, mine, c, r0, nr), (*chip, c), src=w[t].at[c, r0:r0 + nr, :]))
                sent[-1].start()
        for j, chip in enumerate(chips):
            theirs = 2 * chip[0] + chip[1]
            for pi, (t, r0, nr) in enumerate(pieces):
                copy(j * NP + pi, blk(t, theirs, c, r0, nr), me).wait_recv()
                sent.append(copy((3 + j) * NP + pi, blk(t, theirs, c, r0, nr), sibling))
                sent[-1].start()
        for j, chip in enumerate(chips):
            theirs = 2 * chip[0] + chip[1]
            for pi, (t, r0, nr) in enumerate(pieces):
                copy((3 + j) * NP + pi, blk(t, theirs, 1 - c, r0, nr), me).wait_recv()
        for cp in sent:
            cp.wait_send()
        for cp in own:
            cp.wait()

    hbm = pl.BlockSpec(memory_space=pl.ANY)
    return pl.pallas_call(
        body, name="ag_weights", out_shape=[_sds((4,) + a.shape, a.dtype) for a in ts],
        in_specs=[hbm] * nt, out_specs=[hbm] * nt,
        scratch_shapes=[pltpu.SemaphoreType.DMA((6 * NP,)), pltpu.SemaphoreType.DMA((6 * NP,)),
                        pltpu.SemaphoreType.DMA((nt,))],
    )(*ts)


def _grad_exchange(ts):
    nt = len(ts)
    pieces = _row_chunks(ts, 2)
    NP = len(pieces)

    def body(*refs):
        src, out = refs[:nt], refs[nt:2 * nt]
        send_sems, recv_sems, local_sems = refs[2 * nt:]
        x, y, c = lax.axis_index("x"), lax.axis_index("y"), lax.axis_index("c")
        me = 4 * x + 2 * y + c
        own = [pltpu.make_async_copy(src[t].at[c, 2 * x + y], out[t].at[me], local_sems.at[t]) for t in range(nt)]
        for cp in own:
            cp.start()
        sent, peers = [], []
        for k in range(1, 8):
            px = 1 - x if k & 4 else x
            py = 1 - y if k & 2 else y
            pc = 1 - c if k & 1 else c
            peers.append(4 * px + 2 * py + pc)
            for pi, (t, r0, nr) in enumerate(pieces):
                idx = (k - 1) * NP + pi
                sent.append(pltpu.make_async_remote_copy(
                    src_ref=src[t].at[pc, 2 * px + py, r0:r0 + nr, :], dst_ref=out[t].at[me, r0:r0 + nr, :],
                    send_sem=send_sems.at[idx], recv_sem=recv_sems.at[idx], device_id=(px, py, pc),
                    device_id_type=MESH))
                sent[-1].start()
        for k, peer in enumerate(peers):
            for pi, (t, r0, nr) in enumerate(pieces):
                idx = k * NP + pi
                pltpu.make_async_remote_copy(
                    src_ref=out[t].at[peer, r0:r0 + nr, :], dst_ref=out[t].at[peer, r0:r0 + nr, :],
                    send_sem=send_sems.at[idx], recv_sem=recv_sems.at[idx], device_id=(x, y, c),
                    device_id_type=MESH).wait_recv()
        for cp in sent:
            cp.wait_send()
        for cp in own:
            cp.wait()

    hbm = pl.BlockSpec(memory_space=pl.ANY)
    return pl.pallas_call(
        body, name="grad_exchange", out_shape=[_sds((8,) + a.shape[2:], a.dtype) for a in ts],
        in_specs=[hbm] * nt, out_specs=[hbm] * nt,
        scratch_shapes=[pltpu.SemaphoreType.DMA((7 * NP,)), pltpu.SemaphoreType.DMA((7 * NP,)),
                        pltpu.SemaphoreType.DMA((nt,))],
    )(*ts)


def _sibling_join(ts):
    nt = len(ts)
    pieces = _row_chunks(ts, 0)
    NP = len(pieces)

    def body(*refs):
        src, out = refs[:nt], refs[nt:2 * nt]
        send_sems, recv_sems, local_sems = refs[2 * nt:]
        x, y, c = lax.axis_index("x"), lax.axis_index("y"), lax.axis_index("c")
        own = [pltpu.make_async_copy(src[t], out[t].at[c], local_sems.at[t]) for t in range(nt)]
        for cp in own:
            cp.start()
        sent = []
        for pi, (t, r0, nr) in enumerate(pieces):
            sent.append(pltpu.make_async_remote_copy(
                src_ref=src[t].at[r0:r0 + nr, :], dst_ref=out[t].at[c, r0:r0 + nr, :], send_sem=send_sems.at[pi],
                recv_sem=recv_sems.at[pi], device_id=(x, y, 1 - c), device_id_type=MESH))
            sent[-1].start()
        for pi, (t, r0, nr) in enumerate(pieces):
            pltpu.make_async_remote_copy(
                src_ref=src[t].at[r0:r0 + nr, :], dst_ref=out[t].at[1 - c, r0:r0 + nr, :], send_sem=send_sems.at[pi],
                recv_sem=recv_sems.at[pi], device_id=(x, y, c), device_id_type=MESH).wait_recv()
        for cp in sent:
            cp.wait_send()
        for cp in own:
            cp.wait()

    vmem = pl.BlockSpec(memory_space=pltpu.VMEM)
    return pl.pallas_call(
        body, name="sibling_join", out_shape=[_sds((2,) + a.shape, a.dtype) for a in ts],
        in_specs=[vmem] * nt, out_specs=[vmem] * nt,
        scratch_shapes=[pltpu.SemaphoreType.DMA((NP,)), pltpu.SemaphoreType.DMA((NP,)),
                        pltpu.SemaphoreType.DMA((nt,))],
        compiler_params=pltpu.CompilerParams(vmem_limit_bytes=_VMEM_LIMIT),
    )(*ts)


def _layer_fwd(x, shift, scale, gate, lw):
    D = x.shape[1]
    h = _norm_mod(x, lw["norm_g"], scale, shift)
    p = _matmul("in_proj", h, lw["w_cat"], "nn", F32, tn_cap=896)
    qn, kn = _sb_prep(p, lw["gq_t"], lw["gk_t"])
    o_att = _sb_fwd(qn, kn, p)
    qkv, bb, gcb, glb = _dn_prep(p, lw["conv_w"], lw["a_row"], lw["dtb_row"])
    o_dn, tinv, sall = _dn_fwd(qkv, bb, gcb, glb)
    o_sb, o_dnn = _gate(o_att, o_dn, p, lw["gn"])
    y, b_sb, b_dn = _branch(o_sb, o_dnn, lw["wb_sb"], lw["wb_dn"], p, D)
    x_next, out = _out_proj(x, y, lw["w_out"], gate)
    res = dict(x=x, h=h, p=p, qn=qn, kn=kn, o_att=o_att, qkv=qkv, bb=bb, gcb=gcb, glb=glb, o_dn=o_dn,
               tinv=tinv, sall=sall, o_sb=o_sb, o_dnn=o_dnn, y=y, b_sb=b_sb, b_dn=b_dn, out=out,
               shift=shift, scale=scale, gate=gate)
    return x_next, res


def _layer_bwd(dxn, res, lw):
    p = res["p"]
    dout, db_sb, db_dn, dm, dgate = _out_bwd(dxn, res["out"], res["gate"], lw["w_out"], p, res["b_sb"], res["b_dn"])
    dw_out = _matmul("dw_out", res["y"], dout, "tn", _MXU_DTYPE)
    dwb_sb = _matmul("dwb_sb", res["o_sb"], db_sb, "tn", _MXU_DTYPE)
    dwb_dn = _matmul("dwb_dn", res["o_dnn"], db_dn, "tn", _MXU_DTYPE)
    do_att, dz_sb, do_dn, dz_dn, dgn = _gate_bwd(db_sb, db_dn, lw["wb_sb"], lw["wb_dn"], res["o_att"], res["o_dn"],
                                                  p, lw["gn"])
    dqn, dkn, dv = _sb_bwd(res["qn"], res["kn"], p, do_att)
    dq_sb, dk_sb, dgq, dgk = _sb_prep_bwd(p, dqn, dkn, lw["gq_t"], lw["gk_t"])
    dqkv, dbb, dgb = _dn_bwd(res["qkv"], res["bb"], res["gcb"], res["glb"], res["tinv"], res["sall"], do_dn)
    dc, dp_ba, dal, ddt = _dn_prep_bwd_a(p, dqkv, dbb, dgb, lw["conv_w"], lw["a_row"], lw["dtb_row"])
    dp_dn, dconv = _dn_prep_bwd_b(p, dc, lw["conv_w"])
    dp = jnp.concatenate([dp_dn, dz_dn, dq_sb, dk_sb, dv.astype(_MXU_DTYPE), dz_sb, dm, dp_ba], axis=1)
    dh = _matmul("dh", dp, lw["w_cat"], "nt", F32, tk_cap=896)
    dw_cat = _matmul("dw_cat", res["h"], dp, "tn", _MXU_DTYPE, tm_cap=1024, tn_cap=896, tk_cap=512)
    dx, dshift, dscale, dnorm_g = _norm_mod_bwd(res["x"], dh, dxn, lw["norm_g"], res["scale"])
    small = dict(dmod=jnp.concatenate([dshift, dscale, dgate], axis=1)[0], norm_g=dnorm_g[0],
                 sb_q_g=dgq.reshape(SB_HEADS, SB_HD).sum(0), sb_k_g=dgk.reshape(SB_HEADS, SB_HD).sum(0),
                 conv_w=dconv, dn_a_log=dal[0, DN_HEADS:2 * DN_HEADS], dn_dt_bias=ddt[0, DN_HEADS:2 * DN_HEADS],
                 dn_norm_g=dgn[0])
    big = dict(w_cat=dw_cat, w_branch_sb=dwb_sb, w_branch_dn=dwb_dn, w_out=dw_out)
    return dx, small, big


def _cat_cols(w, D):
    return jnp.concatenate([w[:, 2048:4096], w[:, 0:2048], w[:, 4104:4104 + 2 * D], w[:, 4096:4104],
                            jnp.zeros((w.shape[0], LANES - 8), w.dtype)], axis=1)


def _uncat_cols(g, D):
    return jnp.concatenate([g[:, 2048:4096], g[:, 0:2048], g[:, 4096 + 2 * D:4096 + 2 * D + 8],
                            g[:, 4096:4096 + 2 * D]], axis=1)


def _flat_pack(arrs, mult):
    flat = jnp.concatenate([a.reshape(-1) for a in arrs])
    n = flat.shape[0]
    pad = (-n) % mult
    if pad:
        flat = jnp.concatenate([flat, jnp.zeros((pad,), flat.dtype)])
    return flat.reshape(-1, LANES)


def _flat_unpack(flat, shapes):
    flat = flat.reshape(-1)
    out, off = [], 0
    for s in shapes:
        n = math.prod(s)
        out.append(flat[off:off + n].reshape(s))
        off += n
    return out


BIG = ("w_in", "w_branch_sb", "w_branch_dn", "w_out")
SMALL = ("ada_b", "norm_g", "sb_q_g", "sb_k_g", "conv_w", "dn_a_log", "dn_dt_bias", "dn_norm_g")


def kernel(x, c, ada_w, ada_b, norm_g, w_in, sb_q_g, sb_k_g, conv_w, dn_a_log, dn_dt_bias, dn_norm_g, w_branch_sb, w_branch_dn, w_out, loss_target, m_ada_w, m_ada_b, m_norm_g, m_w_in, m_sb_q_g, m_sb_k_g, m_conv_w, m_dn_a_log, m_dn_dt_bias, m_dn_norm_g, m_w_branch_sb, m_w_branch_dn, m_w_out, v_ada_w, v_ada_b, v_norm_g, v_w_in, v_sb_q_g, v_sb_k_g, v_conv_w, v_dn_a_log, v_dn_dt_bias, v_dn_norm_g, v_w_branch_sb, v_w_branch_dn, v_w_out):
    W = dict(ada_w=ada_w, ada_b=ada_b, norm_g=norm_g, w_in=w_in, sb_q_g=sb_q_g, sb_k_g=sb_k_g, conv_w=conv_w,
             dn_a_log=dn_a_log, dn_dt_bias=dn_dt_bias, dn_norm_g=dn_norm_g, w_branch_sb=w_branch_sb,
             w_branch_dn=w_branch_dn, w_out=w_out)
    M = dict(ada_w=m_ada_w, ada_b=m_ada_b, norm_g=m_norm_g, w_in=m_w_in, sb_q_g=m_sb_q_g, sb_k_g=m_sb_k_g,
             conv_w=m_conv_w, dn_a_log=m_dn_a_log, dn_dt_bias=m_dn_dt_bias, dn_norm_g=m_dn_norm_g,
             w_branch_sb=m_w_branch_sb, w_branch_dn=m_w_branch_dn, w_out=m_w_out)
    V = dict(ada_w=v_ada_w, ada_b=v_ada_b, norm_g=v_norm_g, w_in=v_w_in, sb_q_g=v_sb_q_g, sb_k_g=v_sb_k_g,
             conv_w=v_conv_w, dn_a_log=v_dn_a_log, dn_dt_bias=v_dn_dt_bias, dn_norm_g=v_dn_norm_g,
             w_branch_sb=v_w_branch_sb, w_branch_dn=v_w_branch_dn, w_out=v_w_out)
    L = ada_w.shape[0]
    S, D = x.shape[1], x.shape[2]
    ix, iy, ic = lax.axis_index("x"), lax.axis_index("y"), lax.axis_index("c")
    shard = 2 * ix + iy
    me = 2 * shard + ic
    n_ada = ada_w.shape[2]
    n_in = w_in.shape[2]
    n_conv = conv_w.shape[2]
    n_br = w_branch_sb.shape[2]
    n_out = w_out.shape[1]

    g_in, g_bs, g_bd, g_out = _ag_weights([W[n].astype(_MXU_DTYPE) for n in BIG])
    w_in_f = jnp.concatenate([g_in[s] for s in range(4)], axis=2)
    wb_sb_f = jnp.concatenate([g_bs[s] for s in range(4)], axis=2)
    wb_dn_f = jnp.concatenate([g_bd[s] for s in range(4)], axis=2)
    w_out_f = jnp.concatenate([g_out[s] for s in range(4)], axis=1)

    g1 = _ag_small("ag_c_conv", _flat_pack([c, conv_w], LANES * 8))
    g1 = g1.reshape(8, -1)
    c_all = g1[:, :D]
    conv_parts = g1[:, D:D + L * CONV_K * n_conv].reshape(4, 2, L, CONV_K, n_conv)[:, 0]
    conv_full = jnp.concatenate([conv_parts[s] for s in range(4)], axis=2)
    ada_b_sh = lax.dynamic_slice_in_dim(ada_b, shard * n_ada, n_ada, axis=1)[:, None, :]
    mod_sh = _ada_fwd(c_all, ada_w, ada_b_sh)
    g2 = _ag_small("ag_mod", _flat_pack([mod_sh], LANES * 8)).reshape(8, -1)
    mod_parts = g2[:, :L * 8 * n_ada].reshape(4, 2, L, 8, n_ada)[:, 0]
    mod_all = jnp.concatenate([mod_parts[s] for s in range(4)], axis=2)
    mod = lax.dynamic_index_in_dim(mod_all, me, axis=1, keepdims=False)

    def layer_weights(l):
        pad_lo = jnp.zeros((DN_HEADS,), F32)
        pad_hi = jnp.zeros((LANES - 2 * DN_HEADS,), F32)
        return dict(
            norm_g=norm_g[l][None, :], w_cat=_cat_cols(w_in_f[l], D),
            gq_t=jnp.tile(sb_q_g[l], SB_HEADS)[None, :], gk_t=jnp.tile(sb_k_g[l], SB_HEADS)[None, :],
            conv_w=conv_full[l],
            a_row=jnp.concatenate([pad_lo, dn_a_log[l], pad_hi])[None, :],
            dtb_row=jnp.concatenate([pad_lo, dn_dt_bias[l], pad_hi])[None, :],
            gn=dn_norm_g[l][None, :], wb_sb=wb_sb_f[l], wb_dn=wb_dn_f[l], w_out=w_out_f[l])

    xs = x[0]
    lws, ress = [], []
    for l in range(L):
        lw = layer_weights(l)
        xs, res = _layer_fwd(xs, mod[l, None, 0:D], mod[l, None, D:2 * D], mod[l, None, 2 * D:3 * D], lw)
        lws.append(lw)
        ress.append(res)
    dxs, loss_row = _loss_head(xs, loss_target[0])
    loss = lax.psum(loss_row[0, 0], ("x", "y", "c"))
    smalls, bigs = [None] * L, [None] * L
    for l in reversed(range(L)):
        dxs, smalls[l], bigs[l] = _layer_bwd(dxs, ress[l], lws[l])
    grad_x = dxs[None]

    small_names = ("dmod",) + SMALL[1:]
    small_pack = _flat_pack([jnp.stack([smalls[l][n] for l in range(L)]) for n in small_names], LANES * 8)
    g3 = _ag_small("ag_small_grads", small_pack)
    R3 = small_pack.shape[0]
    g3 = g3.reshape(8, R3, LANES)
    small_sum = _sum_parts("sum_small", g3)
    small_shapes = [(L, 3 * D), (L, D), (L, SB_HD), (L, SB_HD), (L, CONV_K, 3 * DN_W), (L, DN_HEADS), (L, DN_HEADS),
                    (L, DN_HD)]
    sg = dict(zip(small_names, _flat_unpack(small_sum, small_shapes)))
    G = dict(ada_b=sg["dmod"], norm_g=sg["norm_g"], sb_q_g=sg["sb_q_g"], sb_k_g=sg["sb_k_g"],
             conv_w=lax.dynamic_slice_in_dim(sg["conv_w"], shard * n_conv, n_conv, axis=2),
             dn_a_log=sg["dn_a_log"], dn_dt_bias=sg["dn_dt_bias"], dn_norm_g=sg["dn_norm_g"])
    dmod_all = g3.reshape(8, -1)[:, :L * 3 * D].reshape(8, L, 3 * D)
    dmod_sh = lax.dynamic_slice_in_dim(dmod_all, shard * n_ada, n_ada, axis=2).transpose(1, 0, 2)
    G["ada_w"] = _ada_bwd(c_all.T, dmod_sh)

    def by_col_shard(name, n_sh):
        g = jnp.stack([bigs[l][name] for l in range(L)])
        return g.reshape(L, g.shape[1], 4, n_sh).transpose(0, 2, 1, 3)

    send = [jnp.stack([_uncat_cols(bigs[l]["w_cat"], D) for l in range(L)]).reshape(L, D, 4, n_in).transpose(0, 2, 1, 3),
            by_col_shard("w_branch_sb", n_br), by_col_shard("w_branch_dn", n_br),
            jnp.stack([bigs[l]["w_out"] for l in range(L)]).reshape(L, 4, n_out, D)]
    got = _grad_exchange(send)
    mine = [_sum_parts("sum_" + n, g) for n, g in zip(BIG, got)]
    for n, g in zip(BIG, _sibling_join(mine)):
        G[n] = g

    delta, new_m, new_v = {}, {}, {}
    for n in ("ada_w",) + BIG:
        sh = W[n].shape
        two = (sh[0] * sh[1], sh[2])
        d, mo, vo = _adamw("adamw_" + n, W[n].reshape(two), G[n].reshape(two), M[n].reshape(two), V[n].reshape(two))
        delta[n], new_m[n], new_v[n] = d.reshape(sh), mo.reshape(sh), vo.reshape(sh)
    sm_shapes = [W[n].shape for n in SMALL]
    d, mo, vo = _adamw("adamw_small", _flat_pack([W[n] for n in SMALL], LANES * 8),
                       _flat_pack([G[n] for n in SMALL], LANES * 8), _flat_pack([M[n] for n in SMALL], LANES * 8),
                       _flat_pack([V[n] for n in SMALL], LANES * 8))
    for n, dd, mm, vv in zip(SMALL, _flat_unpack(d, sm_shapes), _flat_unpack(mo, sm_shapes),
                             _flat_unpack(vo, sm_shapes)):
        delta[n], new_m[n], new_v[n] = dd, mm, vv

    order = ("ada_w", "ada_b", "norm_g", "w_in", "sb_q_g", "sb_k_g", "conv_w", "dn_a_log", "dn_dt_bias", "dn_norm_g",
             "w_branch_sb", "w_branch_dn", "w_out")
    return (loss, grad_x, *[G[n] for n in order], *[delta[n] for n in order], *[new_m[n] for n in order],
            *[new_v[n] for n in order])
```

```python
import math

import jax
import jax.numpy as jnp
from jax import lax
from jax.experimental import pallas as pl
from jax.experimental.pallas import tpu as pltpu

F32 = jnp.float32
BF16 = jnp.bfloat16
_MXU_DTYPE = BF16
_VMEM_LIMIT = 48 * 1024 * 1024
LANES = 128

EPS = 1e-6
SB_HEADS, SB_HD, SB_W = 8, 64, 512
DN_HEADS, DN_HD, DN_W = 4, 128, 512
CONV_K = 4
CHUNK = 64
QB = 256
_SB_DEAD = 104.0
ADAM_LR, ADAM_B1, ADAM_B2, ADAM_EPS, ADAM_WD, ADAM_STEP = 0.001, 0.9, 0.999, 1e-08, 0.01, 10

C_DN_QKV, C_DN_Z, C_SB_Q, C_SB_K, C_SB_V, C_SB_Z, C_MG = 0, 1536, 2048, 2560, 3072, 3584, 4096

_NN = (((1,), (0,)), ((), ()))
_NT = (((1,), (1,)), ((), ()))
_TN = (((0,), (0,)), ((), ()))
_BNN = (((2,), (1,)), ((0,), (0,)))
_BNT = (((2,), (2,)), ((0,), (0,)))
_BTN = (((1,), (1,)), ((0,), (0,)))
MESH = pl.DeviceIdType.MESH


def _sds(shape, dtype):
    return jax.ShapeDtypeStruct(shape, dtype)


def _cp(n):
    return pltpu.CompilerParams(dimension_semantics=("arbitrary",) * n, vmem_limit_bytes=_VMEM_LIMIT)


def _rb(tm, w, cb=0):
    return pl.BlockSpec((tm, w), lambda i: (i, cb))


def _fs(shape):
    nd = len(shape)
    return pl.BlockSpec(shape, lambda i: (0,) * nd)


def _dg(a, b, dims):
    return lax.dot_general(a, b, dims, preferred_element_type=F32)


def _mm(a, b, dims=_NN):
    return _dg(a.astype(_MXU_DTYPE), b.astype(_MXU_DTYPE), dims)


def _split3(x):
    hi = x.astype(BF16)
    r = x - hi.astype(F32)
    mid = r.astype(BF16)
    lo = (r - mid.astype(F32)).astype(BF16)
    return hi, mid, lo


def _mm_xl(x, const, dims=_NN):
    cb = const.astype(BF16)
    hi, mid, lo = _split3(x)
    return _dg(hi, cb, dims) + _dg(mid, cb, dims) + _dg(lo, cb, dims)


def _mm_xl2(x, const, dims=_NN):
    cb = const.astype(BF16)
    hi = x.astype(BF16)
    lo = (x - hi.astype(F32)).astype(BF16)
    return _dg(hi, cb, dims) + _dg(lo, cb, dims)


def _mm_xr(const, x, dims=_NN):
    cb = const.astype(BF16)
    hi, mid, lo = _split3(x)
    return _dg(cb, hi, dims) + _dg(cb, mid, dims) + _dg(cb, lo, dims)


def _mm3(a, b, dims=_NN):
    ah, am, _ = _split3(a)
    bh, bm, _ = _split3(b)
    return _dg(ah, bh, dims) + (_dg(ah, bm, dims) + _dg(am, bh, dims))


def _sigmoid(z):
    return 1.0 / (1.0 + jnp.exp(-z))


def _silu(z):
    return z * _sigmoid(z)


def _dsilu(z):
    s = _sigmoid(z)
    return s * (1.0 + z * (1.0 - s))


def _softplus(z):
    return jnp.maximum(z, 0.0) + jnp.log(1.0 + jnp.exp(-jnp.abs(z)))


def _iota2(shape, dim):
    return lax.broadcasted_iota(jnp.int32, shape, dim)


def _pick(n, cap, mult):
    best = None
    for t in range(mult, min(n, cap) + 1, mult):
        if n % t == 0:
            best = t
    assert best is not None, (n, cap, mult)
    return best


def _matmul(name, a, b, form, out_dtype, tm_cap=512, tn_cap=1024, tk_cap=1024):
    if form == "nn":
        (M, K), (_, N) = a.shape, b.shape
    elif form == "nt":
        (M, K), (N, _) = a.shape, b.shape
    else:
        (K, M), (_, N) = a.shape, b.shape
    tm = _pick(M, tm_cap, 128 if form == "tn" else 8)
    tn = _pick(N, tn_cap, 128)
    tk = _pick(K, tk_cap, 128)
    nk = K // tk
    dims = {"nn": _NN, "nt": _NT, "tn": _TN}[form]
    if form == "nn":
        a_spec = pl.BlockSpec((tm, tk), lambda i, j, k: (i, k))
        b_spec = pl.BlockSpec((tk, tn), lambda i, j, k: (k, j))
    elif form == "nt":
        a_spec = pl.BlockSpec((tm, tk), lambda i, j, k: (i, k))
        b_spec = pl.BlockSpec((tn, tk), lambda i, j, k: (j, k))
    else:
        a_spec = pl.BlockSpec((tk, tm), lambda i, j, k: (k, i))
        b_spec = pl.BlockSpec((tk, tn), lambda i, j, k: (k, j))

    def body(a_ref, b_ref, o_ref, acc_ref):
        k = pl.program_id(2)

        @pl.when(k == 0)
        def _():
            acc_ref[...] = jnp.zeros_like(acc_ref)

        acc_ref[...] += _mm(a_ref[...], b_ref[...], dims)

        @pl.when(k == nk - 1)
        def _():
            o_ref[...] = acc_ref[...].astype(o_ref.dtype)

    return pl.pallas_call(
        body, name=name, grid=(M // tm, N // tn, nk),
        in_specs=[a_spec, b_spec],
        out_specs=pl.BlockSpec((tm, tn), lambda i, j, k: (i, j)),
        out_shape=_sds((M, N), out_dtype),
        scratch_shapes=[pltpu.VMEM((tm, tn), F32)],
        compiler_params=_cp(3),
    )(a, b)


def _norm_mod(x, g, scale, shift, tm=256):
    S, D = x.shape

    def body(x_ref, g_ref, sc_ref, sh_ref, h_ref):
        xv = x_ref[...]
        r = lax.rsqrt(jnp.mean(xv * xv, axis=1, keepdims=True) + EPS)
        h_ref[...] = ((xv * r * g_ref[...]) * (1.0 + sc_ref[...]) + sh_ref[...]).astype(h_ref.dtype)

    return pl.pallas_call(
        body, name="norm_mod", grid=(S // tm,),
        in_specs=[_rb(tm, D), _fs((1, D)), _fs((1, D)), _fs((1, D))],
        out_specs=_rb(tm, D), out_shape=_sds((S, D), _MXU_DTYPE), compiler_params=_cp(1),
    )(x, g, scale, shift)


def _norm_mod_bwd(x, dh, dxn, g, scale, tm=256):
    S, D = x.shape

    def body(x_ref, dh_ref, dxn_ref, g_ref, sc_ref, dx_ref, dsh_ref, dsc_ref, dg_ref):
        @pl.when(pl.program_id(0) == 0)
        def _():
            dsh_ref[...] = jnp.zeros_like(dsh_ref)
            dsc_ref[...] = jnp.zeros_like(dsc_ref)
            dg_ref[...] = jnp.zeros_like(dg_ref)

        xv, dhv, gv = x_ref[...], dh_ref[...], g_ref[...]
        r = lax.rsqrt(jnp.mean(xv * xv, axis=1, keepdims=True) + EPS)
        xh = xv * r
        one_sc = 1.0 + sc_ref[...]
        dsh_ref[...] += jnp.sum(dhv, axis=0, keepdims=True)
        dsc_ref[...] += jnp.sum(dhv * xh * gv, axis=0, keepdims=True)
        dg_ref[...] += jnp.sum(dhv * one_sc * xh, axis=0, keepdims=True)
        dxh = dhv * (gv * one_sc)
        dx_ref[...] = r * (dxh - xh * jnp.mean(dxh * xh, axis=1, keepdims=True)) + dxn_ref[...]

    return pl.pallas_call(
        body, name="norm_mod_bwd", grid=(S // tm,),
        in_specs=[_rb(tm, D), _rb(tm, D), _rb(tm, D), _fs((1, D)), _fs((1, D))],
        out_specs=[_rb(tm, D), _fs((1, D)), _fs((1, D)), _fs((1, D))],
        out_shape=[_sds((S, D), F32)] + [_sds((1, D), F32)] * 3, compiler_params=_cp(1),
    )(x, dh, dxn, g, scale)


def _head_sum_matrix():
    r = jnp.arange(SB_W)
    return (r[:, None] // SB_HD == r[None, :] // SB_HD).astype(BF16)


def _sb_prep(p, gq_t, gk_t, tm=256):
    S = p.shape[0]
    bd = _head_sum_matrix()

    def body(q_ref, k_ref, gq_ref, gk_ref, bd_ref, qn_ref, kn_ref):
        for src, g_ref, dst in ((q_ref, gq_ref, qn_ref), (k_ref, gk_ref, kn_ref)):
            v = src[...]
            ms = _mm_xl(v * v, bd_ref[...]) * (1.0 / SB_HD)
            dst[...] = (v * lax.rsqrt(ms + EPS) * g_ref[...]).astype(dst.dtype)

    return pl.pallas_call(
        body, name="sb_prep", grid=(S // tm,),
        in_specs=[_rb(tm, SB_W, C_SB_Q // SB_W), _rb(tm, SB_W, C_SB_K // SB_W),
                  _fs((1, SB_W)), _fs((1, SB_W)), _fs((SB_W, SB_W))],
        out_specs=[_rb(tm, SB_W), _rb(tm, SB_W)],
        out_shape=[_sds((S, SB_W), _MXU_DTYPE)] * 2, compiler_params=_cp(1),
    )(p, p, gq_t, gk_t, bd)


def _sb_prep_bwd(p, dqn, dkn, gq_t, gk_t, tm=256):
    S = p.shape[0]
    bd = _head_sum_matrix()

    def body(q_ref, k_ref, dqn_ref, dkn_ref, gq_ref, gk_ref, bd_ref, dq_ref, dk_ref, dgq_ref, dgk_ref):
        @pl.when(pl.program_id(0) == 0)
        def _():
            dgq_ref[...] = jnp.zeros_like(dgq_ref)
            dgk_ref[...] = jnp.zeros_like(dgk_ref)

        for src, dn_ref, g_ref, dst, dg_ref in ((q_ref, dqn_ref, gq_ref, dq_ref, dgq_ref),
                                                (k_ref, dkn_ref, gk_ref, dk_ref, dgk_ref)):
            v, dn = src[...], dn_ref[...]
            r = lax.rsqrt(_mm_xl(v * v, bd_ref[...]) * (1.0 / SB_HD) + EPS)
            vh = v * r
            dg_ref[...] += jnp.sum(dn * vh, axis=0, keepdims=True)
            dvh = dn * g_ref[...]
            m = _mm_xl(dvh * vh, bd_ref[...]) * (1.0 / SB_HD)
            dst[...] = (r * (dvh - vh * m)).astype(dst.dtype)

    return pl.pallas_call(
        body, name="sb_prep_bwd", grid=(S // tm,),
        in_specs=[_rb(tm, SB_W, C_SB_Q // SB_W), _rb(tm, SB_W, C_SB_K // SB_W), _rb(tm, SB_W), _rb(tm, SB_W),
                  _fs((1, SB_W)), _fs((1, SB_W)), _fs((SB_W, SB_W))],
        out_specs=[_rb(tm, SB_W), _rb(tm, SB_W), _fs((1, SB_W)), _fs((1, SB_W))],
        out_shape=[_sds((S, SB_W), _MXU_DTYPE)] * 2 + [_sds((1, SB_W), F32)] * 2, compiler_params=_cp(1),
    )(p, p, dqn, dkn, gq_t, gk_t, bd)


def _sb_consts():
    r, c = _iota2((QB, QB), 0), _iota2((QB, QB), 1)
    lane = _iota2((1, LANES), 1)
    return r, c, lane


def _sb_fwd(qn, kn, p):
    S = qn.shape[0]
    scale = 1.0 / math.sqrt(SB_HD)

    def body(q_ref, k_ref, v_ref, o_ref):
        i = pl.program_id(1)
        r, c, lane = _sb_consts()
        u_gt = (r > c).astype(BF16)
        strict = jnp.concatenate([c < r, c < r], axis=0)
        q = q_ref[...]
        mask0 = (lane // SB_HD) == 0
        zero = jnp.zeros_like(q)
        qh = jnp.concatenate([jnp.where(mask0, q, zero), jnp.where(mask0, zero, q)], axis=0)

        def block(off, carry, diagonal):
            o, run = carry
            kj = k_ref[pl.ds(off, QB), :]
            vj = v_ref[pl.ds(off, QB), :].astype(_MXU_DTYPE)
            z = _mm(qh, kj, _NT) * scale
            sp = _softplus(z)
            sp_m = jnp.where(strict, sp, 0.0) if diagonal else sp
            later = _mm_xl2(sp_m, u_gt)
            w = jnp.exp((z - sp) - later - run)
            if diagonal:
                w = jnp.where(strict, w, 0.0)
            return o + _mm(w, vj), run + jnp.sum(sp_m, axis=1, keepdims=True)

        init = (jnp.zeros((2 * QB, LANES), F32), jnp.zeros((2 * QB, 1), F32))
        carry = block(pl.multiple_of(i * QB, QB), init, True)
        st = lax.while_loop(
            lambda st: jnp.logical_and(st[0] <= i, jnp.min(st[2]) < _SB_DEAD),
            lambda st: (st[0] + 1,) + block(pl.multiple_of((i - st[0]) * QB, QB), st[1:], False),
            (jnp.int32(1),) + carry)
        o_ref[...] = jnp.where(mask0, st[1][:QB], st[1][QB:])

    return pl.pallas_call(
        body, name="sb_fwd", grid=(SB_W // LANES, S // QB),
        in_specs=[pl.BlockSpec((QB, LANES), lambda hp, i: (i, hp)),
                  pl.BlockSpec((S, LANES), lambda hp, i: (0, hp)),
                  pl.BlockSpec((S, LANES), lambda hp, i: (0, C_SB_V // LANES + hp))],
        out_specs=pl.BlockSpec((QB, LANES), lambda hp, i: (i, hp)),
        out_shape=_sds((S, SB_W), F32), compiler_params=_cp(2),
    )(qn, kn, p)


def _sb_bwd(qn, kn, p, do, exchange=None, ex_src=()):
    S = qn.shape[0]
    scale = 1.0 / math.sqrt(SB_HD)
    grid = (SB_W // LANES, S // QB)
    nx = len(ex_src)

    def body(*refs):
        if exchange is None:
            compute(*refs)
            return
        src, xout, sems = refs[4:4 + nx], refs[7 + nx:7 + 2 * nx], refs[7 + 2 * nx:]
        hp, i = pl.program_id(0), pl.program_id(1)

        @pl.when(jnp.logical_and(hp == 0, i == 0))
        def _():
            exchange.start(src, xout, sems)

        compute(*refs[:4], *refs[4 + nx:7 + nx])

        @pl.when(jnp.logical_and(hp == grid[0] - 1, i == grid[1] - 1))
        def _():
            exchange.wait(src, xout, sems)

    def compute(q_ref, k_ref, v_ref, do_ref, dq_ref, dk_ref, dv_ref):
        i = pl.program_id(1)

        @pl.when(i == 0)
        def _():
            dk_ref[...] = jnp.zeros_like(dk_ref)
            dv_ref[...] = jnp.zeros_like(dv_ref)

        r, c, lane = _sb_consts()
        u_le = (r <= c).astype(BF16)
        u_lt = (r < c).astype(BF16)
        strict = jnp.concatenate([c < r, c < r], axis=0)
        q = q_ref[...]
        do = do_ref[...].astype(_MXU_DTYPE)
        mask0 = (lane // SB_HD) == 0
        zero, zero_do = jnp.zeros_like(q), jnp.zeros_like(do)
        qh = jnp.concatenate([jnp.where(mask0, q, zero), jnp.where(mask0, zero, q)], axis=0)
        doh = jnp.concatenate([jnp.where(mask0, do, zero_do), jnp.where(mask0, zero_do, do)], axis=0)

        def sums(off, run, diagonal):
            sp = _softplus(_mm(qh, k_ref[pl.ds(off, QB), :], _NT) * scale)
            if diagonal:
                sp = jnp.where(strict, sp, 0.0)
            return run + jnp.sum(sp, axis=1, keepdims=True)

        run = sums(pl.multiple_of(i * QB, QB), jnp.zeros((2 * QB, 1), F32), True)
        nb, tot = lax.while_loop(
            lambda st: jnp.logical_and(st[0] <= i, jnp.min(st[1]) < _SB_DEAD),
            lambda st: (st[0] + 1, sums(pl.multiple_of((i - st[0]) * QB, QB), st[1], False)),
            (jnp.int32(1), run))
        first = i + 1 - nb

        def block(off, carry, diagonal):
            dq, pre_sp, pre_e = carry
            kj = k_ref[pl.ds(off, QB), :]
            vj = v_ref[pl.ds(off, QB), :].astype(_MXU_DTYPE)
            z = _mm(qh, kj, _NT) * scale
            sp = _softplus(z)
            a = z - sp
            sp_m = jnp.where(strict, sp, 0.0) if diagonal else sp
            incl = _mm_xl2(sp_m, u_le)
            w = jnp.exp(a - ((tot - pre_sp) - incl))
            if diagonal:
                w = jnp.where(strict, w, 0.0)
            e = w * _mm(doh, vj, _NT)
            db = pre_e + _mm_xl2(e, u_lt)
            dz = (e - jnp.exp(a) * (e + db)) * scale
            if diagonal:
                dz = jnp.where(strict, dz, 0.0)
            dk_ref[pl.ds(off, QB), :] += _mm(dz, qh, _TN)
            dv_ref[pl.ds(off, QB), :] += _mm(w, doh, _TN)
            return (dq + _mm(dz, kj), pre_sp + jnp.sum(sp_m, axis=1, keepdims=True),
                    pre_e + jnp.sum(e, axis=1, keepdims=True))

        zero_col = jnp.zeros((2 * QB, 1), F32)
        init = (jnp.zeros((2 * QB, LANES), F32), zero_col, zero_col)
        carry = lax.fori_loop(first, i, lambda j, cr: block(pl.multiple_of(j * QB, QB), cr, False), init)
        carry = block(pl.multiple_of(i * QB, QB), carry, True)
        dq_ref[...] = jnp.where(mask0, carry[0][:QB], carry[0][QB:])

    blk = pl.BlockSpec((QB, LANES), lambda hp, i: (i, hp))
    full = pl.BlockSpec((S, LANES), lambda hp, i: (0, hp))
    hbm = pl.BlockSpec(memory_space=pl.ANY)
    outs = pl.pallas_call(
        body, name="sb_bwd", grid=grid,
        in_specs=[blk, full, pl.BlockSpec((S, LANES), lambda hp, i: (0, C_SB_V // LANES + hp)), blk] + [hbm] * nx,
        out_specs=[blk, full, full] + [hbm] * nx,
        out_shape=[_sds((S, SB_W), F32)] * 3 + (exchange.out_shape if exchange else []),
        scratch_shapes=exchange.scratch if exchange else [], compiler_params=_cp(2),
    )(qn, kn, p, do, *ex_src)
    return outs[:3], outs[3:]


def _dn_prep(p, conv_w, a_row, dtb_row, tm=256):
    S = p.shape[0]
    W3 = 3 * DN_W
    nhalo = tm // 8

    def body(x_ref, halo_ref, w_ref, ba_ref, a_ref, dtb_ref, qkv_ref, bb_ref, gc_ref, gl_ref):
        i = pl.program_id(0)
        halo = jnp.where(i > 0, halo_ref[...], 0.0)
        xf = jnp.concatenate([halo, x_ref[...]], axis=0)
        acc = jnp.zeros((tm, W3), F32)
        for k in range(CONV_K):
            sh = CONV_K - 1 - k
            xs = xf if sh == 0 else pltpu.roll(xf, sh, 0)
            acc = acc + xs[8:, :] * w_ref[k:k + 1, :]
        s = _silu(acc)
        for gi in range(2 * DN_HEADS):
            sl = slice(gi * LANES, (gi + 1) * LANES)
            sg = s[:, sl]
            rinv = lax.rsqrt(jnp.sum(sg * sg, axis=1, keepdims=True) + EPS)
            qkv_ref[:, sl] = sg * rinv * (DN_HD ** -0.5 if gi < DN_HEADS else 1.0)
        qkv_ref[:, 2 * DN_W:] = s[:, 2 * DN_W:]

        ba = ba_ref[...]
        beta = _sigmoid(ba)
        g = -jnp.exp(a_ref[...]) * _softplus(ba + dtb_ref[...])
        lr, lc = _iota2((LANES, DN_W), 0), _iota2((LANES, DN_W), 1)
        sel_b = (lr == lc // LANES).astype(BF16)
        sel_g = (lr == lc // LANES + DN_HEADS).astype(BF16)
        bb_ref[...] = _mm_xl(beta, sel_b)
        graw = _mm_xl(g, sel_g)
        rr, cc = _iota2((tm, tm), 0), _iota2((tm, tm), 1)
        tri = jnp.logical_and(rr >= cc, rr // CHUNK == cc // CHUNK).astype(BF16)
        gc = _mm_xr(tri, graw)
        last = (cc == (rr // CHUNK) * CHUNK + (CHUNK - 1)).astype(BF16)
        gc_ref[...] = gc
        gl_ref[...] = _mm_xr(last, gc)

    return pl.pallas_call(
        body, name="dn_prep", grid=(S // tm,),
        in_specs=[_rb(tm, W3, 0), pl.BlockSpec((8, W3), lambda i: (jnp.maximum(i * nhalo - 1, 0), 0)),
                  _fs((CONV_K, W3)), _rb(tm, LANES, (p.shape[1] - LANES) // LANES),
                  _fs((1, LANES)), _fs((1, LANES))],
        out_specs=[_rb(tm, W3), _rb(tm, DN_W), _rb(tm, DN_W), _rb(tm, DN_W)],
        out_shape=[_sds((S, W3), F32)] + [_sds((S, DN_W), F32)] * 3, compiler_params=_cp(1),
    )(p, p, conv_w, p, a_row, dtb_row)


def _heads(ref, base=0):
    return jnp.stack([ref[:, base + h * LANES:base + (h + 1) * LANES] for h in range(DN_HEADS)])


def _per_head(const):
    return jnp.broadcast_to(const[None], (DN_HEADS,) + const.shape)


def _dn_chunk_terms(q, k, v, beta, gc, gl):
    r, c = _iota2((CHUNK, CHUNK), 0), _iota2((CHUNK, CHUNK), 1)
    tril, strict = r >= c, r > c
    gcol = _mm_xl(gc, _per_head(jnp.full((LANES, CHUNK), 1.0 / LANES, F32)), _BNN)
    grow = _mm_xr(_per_head(jnp.full((CHUNK, LANES), 1.0 / LANES, F32)), gc, _BNT)
    dec = jnp.where(tril, jnp.exp(jnp.where(tril, gcol - grow, 0.0)), 0.0)
    gam = jnp.exp(gc)
    dlt = jnp.exp(gl - gc)
    kb, vb = k * beta, v * beta
    pm = _mm(kb, k, _BNT)
    qk = _mm(q, k, _BNT)
    m = jnp.where(strict, pm * dec, 0.0)
    a = jnp.where(tril, qk * dec, 0.0)
    return dict(tril=tril, strict=strict, dec=dec, gam=gam, dlt=dlt, kb=kb, vb=vb, m=m, a=a)


def _dn_fwd(qkv, bb, gcb, glb):
    S = qkv.shape[0]
    N = S // CHUNK

    def body(qkv_ref, bb_ref, gc_ref, gl_ref, o_ref, t_ref, sall_ref, s_scr):
        @pl.when(pl.program_id(0) == 0)
        def _():
            s_scr[...] = jnp.zeros_like(s_scr)

        r, c = _iota2((CHUNK, CHUNK), 0), _iota2((CHUNK, CHUNK), 1)
        eye = (r == c).astype(F32)
        q, k, v = _heads(qkv_ref), _heads(qkv_ref, DN_W), _heads(qkv_ref, 2 * DN_W)
        beta, gc, gl = _heads(bb_ref), _heads(gc_ref), _heads(gl_ref)
        s_prev = s_scr[...]
        sall_ref[0] = s_prev.astype(sall_ref.dtype)
        s0 = s_prev.astype(sall_ref.dtype).astype(F32)
        t = _dn_chunk_terms(q, k, v, beta, gc, gl)
        pw = -t["m"]
        tinv = eye + pw
        for _ in range(5):
            pw = _mm3(pw, pw, _BNN)
            tinv = tinv + _mm3(tinv, pw, _BNN)
        t_ref[...] = tinv
        u = _mm3(tinv, t["vb"], _BNN)
        w = _mm3(tinv, t["kb"] * t["gam"], _BNN)
        vn = u - _mm(w, s0, _BNN)
        o = _mm(q * t["gam"], s0, _BNN) + _mm(t["a"], vn, _BNN)
        for h in range(DN_HEADS):
            o_ref[:, h * LANES:(h + 1) * LANES] = o[h]
        egl = jnp.exp(jnp.concatenate([gl, gl], axis=1))
        s_scr[...] = s_prev * egl + _mm(k * t["dlt"], vn, _BTN)

    return pl.pallas_call(
        body, name="dn_fwd", grid=(N,),
        in_specs=[_rb(CHUNK, 3 * DN_W), _rb(CHUNK, DN_W), _rb(CHUNK, DN_W), _rb(CHUNK, DN_W)],
        out_specs=[_rb(CHUNK, DN_W), pl.BlockSpec((DN_HEADS, CHUNK, CHUNK), lambda n: (0, n, 0)),
                   pl.BlockSpec((1, DN_HEADS, DN_HD, DN_HD), lambda n: (n, 0, 0, 0))],
        out_shape=[_sds((S, DN_W), F32), _sds((DN_HEADS, S, CHUNK), F32),
                   _sds((N, DN_HEADS, DN_HD, DN_HD), _MXU_DTYPE)],
        scratch_shapes=[pltpu.VMEM((DN_HEADS, DN_HD, DN_HD), F32)],
        compiler_params=_cp(1),
    )(qkv, bb, gcb, glb)


def _dn_bwd(qkv, bb, gcb, glb, tinv_all, sall, do):
    S = qkv.shape[0]
    N = S // CHUNK

    def body(qkv_ref, bb_ref, gc_ref, gl_ref, t_ref, sall_ref, do_ref, dqkv_ref, dbb_ref, dg_ref, ds_scr):
        @pl.when(pl.program_id(0) == 0)
        def _():
            ds_scr[...] = jnp.zeros_like(ds_scr)

        r, c = _iota2((CHUNK, CHUNK), 0), _iota2((CHUNK, CHUNK), 1)
        eye = (r == c).astype(F32)
        u_ge = (c >= r).astype(F32)
        last_row = _iota2((CHUNK, LANES), 0) == CHUNK - 1
        eye_h, u_ge_h = _per_head(eye), _per_head(u_ge)
        q, k, v = _heads(qkv_ref), _heads(qkv_ref, DN_W), _heads(qkv_ref, 2 * DN_W)
        beta, gc, gl = _heads(bb_ref), _heads(gc_ref), _heads(gl_ref)
        tinv = t_ref[...]
        s0 = sall_ref[0].astype(F32)
        do = _heads(do_ref)
        ds1 = ds_scr[...]
        t = _dn_chunk_terms(q, k, v, beta, gc, gl)
        gam, dlt, kb, vb, dec = t["gam"], t["dlt"], t["kb"], t["vb"], t["dec"]
        kbg = kb * gam
        u = _mm3(tinv, vb, _BNN)
        w = _mm3(tinv, kbg, _BNN)
        vn = u - _mm(w, s0, _BNN)
        qg, kd = q * gam, k * dlt
        egl = jnp.exp(gl)
        egl2 = jnp.concatenate([egl, egl], axis=1)

        dvn = _mm(t["a"], do, _BTN) + _mm(kd, ds1, _BNN)
        da = jnp.where(t["tril"], _mm(do, vn, _BNT), 0.0)
        dqg = _mm(do, s0, _BNT)
        dkd = _mm(vn, ds1, _BNT)
        dw = -_mm(dvn, s0, _BNT)
        ds_scr[...] = _mm(qg, do, _BTN) + egl2 * ds1 - _mm(w, dvn, _BTN)
        tt = _mm_xr(eye_h, tinv, _BNT)
        dvb = _mm3(tt, dvn, _BNN)
        dkbg = _mm3(tt, dw, _BNN)
        dm = -jnp.where(t["strict"], _mm(dvb, u, _BNT) + _mm(dkbg, w, _BNT), 0.0)
        dpm = dm * dec
        dqk = da * dec
        dkb = dkbg * gam + _mm(dpm, k, _BNN)
        dk = dkd * dlt + _mm(dpm, kb, _BTN) + _mm(dqk, q, _BTN) + dkb * beta
        dq = dqg * gam + _mm(dqk, k, _BNN)
        dv = dvb * beta
        dbeta = jnp.sum(dkb * k, axis=2, keepdims=True) + jnp.sum(dvb * v, axis=2, keepdims=True)
        dgam = jnp.sum(dqg * q, axis=2, keepdims=True) + jnp.sum(dkbg * kb, axis=2, keepdims=True)
        ddlt = jnp.sum(dkd * k, axis=2, keepdims=True)
        xm = dm * t["m"] + da * t["a"]
        xt = _mm_xr(eye_h, xm, _BNT)
        dgc = (dgam * gam - ddlt * dlt + jnp.sum(xm, axis=2, keepdims=True) - jnp.sum(xt, axis=2, keepdims=True))
        dgl = jnp.sum(ddlt * dlt, axis=1, keepdims=True) + jnp.sum(
            jnp.sum(ds1 * s0, axis=2, keepdims=True), axis=1, keepdims=True) * jnp.max(egl, axis=1, keepdims=True)
        dgc = dgc + jnp.where(last_row, dgl, 0.0)
        dg = _mm_xr(u_ge_h, dgc, _BNN)
        for h in range(DN_HEADS):
            sl = slice(h * LANES, (h + 1) * LANES)
            dqkv_ref[:, sl] = dq[h]
            dqkv_ref[:, DN_W + h * LANES:DN_W + (h + 1) * LANES] = dk[h]
            dqkv_ref[:, 2 * DN_W + h * LANES:2 * DN_W + (h + 1) * LANES] = dv[h]
            dbb_ref[:, sl] = jnp.broadcast_to(dbeta[h], (CHUNK, LANES))
            dg_ref[:, sl] = dg[h]

    rev = lambda w: pl.BlockSpec((CHUNK, w), lambda n: (N - 1 - n, 0))
    return pl.pallas_call(
        body, name="dn_bwd", grid=(N,),
        in_specs=[rev(3 * DN_W), rev(DN_W), rev(DN_W), rev(DN_W),
                  pl.BlockSpec((DN_HEADS, CHUNK, CHUNK), lambda n: (0, N - 1 - n, 0)),
                  pl.BlockSpec((1, DN_HEADS, DN_HD, DN_HD), lambda n: (N - 1 - n, 0, 0, 0)), rev(DN_W)],
        out_specs=[rev(3 * DN_W), rev(DN_W), rev(DN_W)],
        out_shape=[_sds((S, 3 * DN_W), F32), _sds((S, DN_W), F32), _sds((S, DN_W), F32)],
        scratch_shapes=[pltpu.VMEM((DN_HEADS, DN_HD, DN_HD), F32)],
        compiler_params=_cp(1),
    )(qkv, bb, gcb, glb, tinv_all, sall, do)


def _dn_prep_bwd_a(p, dqkv, dbb, dgb, conv_w, a_row, dtb_row, tm=256):
    S, PC = p.shape
    W3 = 3 * DN_W
    nhalo = tm // 8

    def body(x_ref, halo_ref, w_ref, ba_ref, a_ref, dtb_ref, dqkv_ref, dbb_ref, dgb_ref,
             dc_ref, dba_ref, dal_ref, ddt_ref):
        i = pl.program_id(0)

        @pl.when(i == 0)
        def _():
            dal_ref[...] = jnp.zeros_like(dal_ref)
            ddt_ref[...] = jnp.zeros_like(ddt_ref)

        halo = jnp.where(i > 0, halo_ref[...], 0.0)
        xf = jnp.concatenate([halo, x_ref[...]], axis=0)
        acc = jnp.zeros((tm, W3), F32)
        for k in range(CONV_K):
            sh = CONV_K - 1 - k
            xs = xf if sh == 0 else pltpu.roll(xf, sh, 0)
            acc = acc + xs[8:, :] * w_ref[k:k + 1, :]
        s = _silu(acc)
        ds_act = _dsilu(acc)
        for gi in range(2 * DN_HEADS):
            sl = slice(gi * LANES, (gi + 1) * LANES)
            sg = s[:, sl]
            rinv = lax.rsqrt(jnp.sum(sg * sg, axis=1, keepdims=True) + EPS)
            nh = sg * rinv
            dn = dqkv_ref[:, sl] * (DN_HD ** -0.5 if gi < DN_HEADS else 1.0)
            dsg = rinv * (dn - nh * jnp.sum(dn * nh, axis=1, keepdims=True))
            dc_ref[:, sl] = dsg * ds_act[:, sl]
        dc_ref[:, 2 * DN_W:] = dqkv_ref[:, 2 * DN_W:] * ds_act[:, 2 * DN_W:]

        ba = ba_ref[...]
        beta = _sigmoid(ba)
        ea = jnp.exp(a_ref[...])
        pre = ba + dtb_ref[...]
        g = -ea * _softplus(pre)
        lr, lc = _iota2((DN_W, LANES), 0), _iota2((DN_W, LANES), 1)
        pick_b = jnp.where(lc == lr // LANES, 1.0 / LANES, 0.0)
        pick_g = jnp.where(lc == lr // LANES + DN_HEADS, 1.0 / LANES, 0.0)
        dbeta = _mm_xl(dbb_ref[...], pick_b)
        dg = _mm_xl(dgb_ref[...], pick_g)
        lane = _iota2((1, LANES), 1)
        da = dg * (-ea) * _sigmoid(pre)
        dba_ref[...] = jnp.where(lane < DN_HEADS, dbeta * beta * (1.0 - beta),
                                 jnp.where(lane < 2 * DN_HEADS, da, 0.0)).astype(dba_ref.dtype)
        dal_ref[...] += jnp.sum(dg * g, axis=0, keepdims=True)
        ddt_ref[...] += jnp.sum(da, axis=0, keepdims=True)

    return pl.pallas_call(
        body, name="dn_prep_bwd_a", grid=(S // tm,),
        in_specs=[_rb(tm, W3, 0), pl.BlockSpec((8, W3), lambda i: (jnp.maximum(i * nhalo - 1, 0), 0)),
                  _fs((CONV_K, W3)), _rb(tm, LANES, (PC - LANES) // LANES), _fs((1, LANES)), _fs((1, LANES)),
                  _rb(tm, W3), _rb(tm, DN_W), _rb(tm, DN_W)],
        out_specs=[_rb(tm, W3), _rb(tm, LANES), _fs((1, LANES)), _fs((1, LANES))],
        out_shape=[_sds((S, W3), F32), _sds((S, LANES), _MXU_DTYPE), _sds((1, LANES), F32), _sds((1, LANES), F32)],
        compiler_params=_cp(1),
    )(p, p, conv_w, p, a_row, dtb_row, dqkv, dbb, dgb)


def _dn_prep_bwd_b(p, dc, conv_w, tm=256):
    S = p.shape[0]
    W3 = 3 * DN_W
    nhalo = tm // 8
    nblk = S // tm

    def body(x_ref, xh_ref, dc_ref, dch_ref, w_ref, dx_ref, dw_ref):
        i = pl.program_id(0)

        @pl.when(i == 0)
        def _():
            dw_ref[...] = jnp.zeros_like(dw_ref)

        dcv = dc_ref[...]
        xf = jnp.concatenate([jnp.where(i > 0, xh_ref[...], 0.0), x_ref[...]], axis=0)
        df = jnp.concatenate([dcv, jnp.where(i < nblk - 1, dch_ref[...], 0.0)], axis=0)
        acc = jnp.zeros((tm, W3), F32)
        for k in range(CONV_K):
            sh = CONV_K - 1 - k
            xs = xf if sh == 0 else pltpu.roll(xf, sh, 0)
            dw_ref[k:k + 1, :] += jnp.sum(dcv * xs[8:, :], axis=0, keepdims=True)
            ds = df if sh == 0 else pltpu.roll(df, tm + 8 - sh, 0)
            acc = acc + ds[:tm, :] * w_ref[k:k + 1, :]
        dx_ref[...] = acc.astype(dx_ref.dtype)

    return pl.pallas_call(
        body, name="dn_prep_bwd_b", grid=(nblk,),
        in_specs=[_rb(tm, W3, 0), pl.BlockSpec((8, W3), lambda i: (jnp.maximum(i * nhalo - 1, 0), 0)),
                  _rb(tm, W3), pl.BlockSpec((8, W3), lambda i: (jnp.minimum((i + 1) * nhalo, S // 8 - 1), 0)),
                  _fs((CONV_K, W3))],
        out_specs=[_rb(tm, W3), _fs((CONV_K, W3))],
        out_shape=[_sds((S, W3), _MXU_DTYPE), _sds((CONV_K, W3), F32)], compiler_params=_cp(1),
    )(p, p, dc, dc, conv_w)


def _gate(o_att, o_dn, p, gn, tm=256):
    S = p.shape[0]

    def body(oa_ref, zs_ref, od_ref, zd_ref, gn_ref, osb_ref, odn_ref):
        osb_ref[...] = (oa_ref[...] * _silu(zs_ref[...])).astype(osb_ref.dtype)
        for h in range(DN_HEADS):
            sl = slice(h * LANES, (h + 1) * LANES)
            o = od_ref[:, sl]
            r = lax.rsqrt(jnp.mean(o * o, axis=1, keepdims=True) + EPS)
            odn_ref[:, sl] = (o * r * gn_ref[...] * _silu(zd_ref[:, sl])).astype(odn_ref.dtype)

    return pl.pallas_call(
        body, name="gate", grid=(S // tm,),
        in_specs=[_rb(tm, SB_W), _rb(tm, SB_W, C_SB_Z // SB_W), _rb(tm, DN_W), _rb(tm, DN_W, C_DN_Z // DN_W),
                  _fs((1, LANES))],
        out_specs=[_rb(tm, SB_W), _rb(tm, DN_W)],
        out_shape=[_sds((S, SB_W), _MXU_DTYPE), _sds((S, DN_W), _MXU_DTYPE)], compiler_params=_cp(1),
    )(o_att, p, o_dn, p, gn)


def _gate_bwd(db_sb, db_dn, wb_sb, wb_dn, o_att, o_dn, p, gn, tm=256):
    S = p.shape[0]
    D = db_sb.shape[1]

    def body(dbs_ref, dbd_ref, ws_ref, wd_ref, oa_ref, zs_ref, od_ref, zd_ref, gn_ref,
             doa_ref, dzs_ref, dod_ref, dzd_ref, dgn_ref):
        @pl.when(pl.program_id(0) == 0)
        def _():
            dgn_ref[...] = jnp.zeros_like(dgn_ref)

        do_sb = _mm(dbs_ref[...], ws_ref[...], _NT)
        zs = zs_ref[...]
        doa_ref[...] = do_sb * _silu(zs)
        dzs_ref[...] = (do_sb * oa_ref[...] * _dsilu(zs)).astype(dzs_ref.dtype)
        do_dnn = _mm(dbd_ref[...], wd_ref[...], _NT)
        gnv = gn_ref[...]
        for h in range(DN_HEADS):
            sl = slice(h * LANES, (h + 1) * LANES)
            o, z, dout = od_ref[:, sl], zd_ref[:, sl], do_dnn[:, sl]
            r = lax.rsqrt(jnp.mean(o * o, axis=1, keepdims=True) + EPS)
            oh = o * r
            sz = _silu(z)
            dzd_ref[:, sl] = (dout * oh * gnv * _dsilu(z)).astype(dzd_ref.dtype)
            dgn_ref[...] += jnp.sum(dout * sz * oh, axis=0, keepdims=True)
            doh = dout * gnv * sz
            dod_ref[:, sl] = r * (doh - oh * jnp.mean(doh * oh, axis=1, keepdims=True))

    return pl.pallas_call(
        body, name="gate_bwd", grid=(S // tm,),
        in_specs=[_rb(tm, D), _rb(tm, D), _fs((SB_W, D)), _fs((DN_W, D)), _rb(tm, SB_W),
                  _rb(tm, SB_W, C_SB_Z // SB_W), _rb(tm, DN_W), _rb(tm, DN_W, C_DN_Z // DN_W), _fs((1, LANES))],
        out_specs=[_rb(tm, SB_W), _rb(tm, SB_W), _rb(tm, DN_W), _rb(tm, DN_W), _fs((1, LANES))],
        out_shape=[_sds((S, SB_W), F32), _sds((S, SB_W), _MXU_DTYPE), _sds((S, DN_W), F32),
                   _sds((S, DN_W), _MXU_DTYPE), _sds((1, LANES), F32)],
        compiler_params=_cp(1),
    )(db_sb, db_dn, wb_sb, wb_dn, o_att, p, o_dn, p, gn)


def _branch(o_sb, o_dnn, wb_sb, wb_dn, p, D, tm=256):
    S = p.shape[0]

    def body(os_ref, od_ref, ws_ref, wd_ref, ms_ref, md_ref, y_ref, bs_ref, bd_ref):
        bs = _mm(os_ref[...], ws_ref[...])
        bdn = _mm(od_ref[...], wd_ref[...])
        bs_ref[...] = bs
        bd_ref[...] = bdn
        y_ref[...] = (_sigmoid(ms_ref[...]) * bs + _sigmoid(md_ref[...]) * bdn).astype(y_ref.dtype)

    return pl.pallas_call(
        body, name="branch", grid=(S // tm,),
        in_specs=[_rb(tm, SB_W), _rb(tm, DN_W), _fs((SB_W, D)), _fs((DN_W, D)),
                  _rb(tm, D, C_MG // D), _rb(tm, D, C_MG // D + 1)],
        out_specs=[_rb(tm, D), _rb(tm, D), _rb(tm, D)],
        out_shape=[_sds((S, D), _MXU_DTYPE), _sds((S, D), F32), _sds((S, D), F32)], compiler_params=_cp(1),
    )(o_sb, o_dnn, wb_sb, wb_dn, p, p)


def _out_proj(x, y, w_out, gate, tm=256):
    S, D = x.shape

    def body(x_ref, y_ref, w_ref, g_ref, xn_ref, out_ref):
        out = _mm(y_ref[...], w_ref[...])
        out_ref[...] = out
        xn_ref[...] = x_ref[...] + g_ref[...] * out

    return pl.pallas_call(
        body, name="out_proj", grid=(S // tm,),
        in_specs=[_rb(tm, D), _rb(tm, D), _fs((D, D)), _fs((1, D))],
        out_specs=[_rb(tm, D), _rb(tm, D)],
        out_shape=[_sds((S, D), F32), _sds((S, D), F32)], compiler_params=_cp(1),
    )(x, y, w_out, gate)


def _out_bwd(dxn, out, gate, w_out, p, b_sb, b_dn, tm=256):
    S, D = dxn.shape

    def body(dxn_ref, out_ref, g_ref, w_ref, ms_ref, md_ref, bs_ref, bd_ref,
             dout_ref, dbs_ref, dbd_ref, dm_ref, dgate_ref):
        @pl.when(pl.program_id(0) == 0)
        def _():
            dgate_ref[...] = jnp.zeros_like(dgate_ref)

        dxv = dxn_ref[...]
        dgate_ref[...] += jnp.sum(dxv * out_ref[...], axis=0, keepdims=True)
        dout = (g_ref[...] * dxv).astype(dout_ref.dtype)
        dout_ref[...] = dout
        dy = _mm(dout, w_ref[...], _NT)
        s1, s2 = _sigmoid(ms_ref[...]), _sigmoid(md_ref[...])
        dbs_ref[...] = (dy * s1).astype(dbs_ref.dtype)
        dbd_ref[...] = (dy * s2).astype(dbd_ref.dtype)
        dm_ref[:, :D] = (dy * bs_ref[...] * s1 * (1.0 - s1)).astype(dm_ref.dtype)
        dm_ref[:, D:] = (dy * bd_ref[...] * s2 * (1.0 - s2)).astype(dm_ref.dtype)

    return pl.pallas_call(
        body, name="out_bwd", grid=(S // tm,),
        in_specs=[_rb(tm, D), _rb(tm, D), _fs((1, D)), _fs((D, D)), _rb(tm, D, C_MG // D),
                  _rb(tm, D, C_MG // D + 1), _rb(tm, D), _rb(tm, D)],
        out_specs=[_rb(tm, D), _rb(tm, D), _rb(tm, D), _rb(tm, 2 * D), _fs((1, D))],
        out_shape=[_sds((S, D), _MXU_DTYPE)] * 3 + [_sds((S, 2 * D), _MXU_DTYPE), _sds((1, D), F32)],
        compiler_params=_cp(1),
    )(dxn, out, gate, w_out, p, p, b_sb, b_dn)


def _loss_head(xf, target, tm=256):
    S, D = xf.shape

    def body(x_ref, t_ref, dy_ref, loss_ref):
        @pl.when(pl.program_id(0) == 0)
        def _():
            loss_ref[...] = jnp.zeros_like(loss_ref)

        e = x_ref[...] - t_ref[...]
        dy_ref[...] = e * (1.0 / D)
        row = jnp.sum(e * e, axis=1, keepdims=True) * (1.0 / D)
        loss_ref[...] += 0.5 * jnp.sum(row, axis=0, keepdims=True)

    return pl.pallas_call(
        body, name="loss_head", grid=(S // tm,),
        in_specs=[_rb(tm, D), _rb(tm, D)], out_specs=[_rb(tm, D), _fs((1, LANES))],
        out_shape=[_sds((S, D), F32), _sds((1, LANES), F32)], compiler_params=_cp(1),
    )(xf, target)


def _ada_fwd(c_all, ada_w, ada_b_sh):
    L, D, n = ada_w.shape
    B = c_all.shape[0]

    def body(c_ref, w_ref, b_ref, o_ref):
        sc = _silu(c_ref[...])
        o_ref[0] = _mm(sc, w_ref[0]) + b_ref[0]

    return pl.pallas_call(
        body, name="ada_fwd", grid=(L,),
        in_specs=[_fs((B, D)), pl.BlockSpec((1, D, n), lambda l: (l, 0, 0)), pl.BlockSpec((1, 1, n), lambda l: (l, 0, 0))],
        out_specs=pl.BlockSpec((1, B, n), lambda l: (l, 0, 0)),
        out_shape=_sds((L, B, n), F32), compiler_params=_cp(1),
    )(c_all, ada_w, ada_b_sh)


def _ada_bwd(c_all_t, dmod_sh):
    D, B = c_all_t.shape
    L, _, n = dmod_sh.shape

    def body(c_ref, d_ref, o_ref):
        acc = jnp.zeros((D, n), F32)
        for b in range(B):
            acc = acc + _silu(c_ref[:, b:b + 1]) * d_ref[0, b:b + 1, :]
        o_ref[0] = acc

    return pl.pallas_call(
        body, name="ada_bwd", grid=(L,),
        in_specs=[_fs((D, B)), pl.BlockSpec((1, B, n), lambda l: (l, 0, 0))],
        out_specs=pl.BlockSpec((1, D, n), lambda l: (l, 0, 0)),
        out_shape=_sds((L, D, n), F32), compiler_params=_cp(1),
    )(c_all_t, dmod_sh)


def _sum_parts(name, parts):
    P, R, C = parts.shape
    tr = _pick(R, max(16, min(512, (1 << 19) // (P * C))), 16) if R % 16 == 0 else R

    def body(p_ref, o_ref):
        acc = p_ref[0].astype(F32)
        for k in range(1, P):
            acc = acc + p_ref[k].astype(F32)
        o_ref[...] = acc

    return pl.pallas_call(
        body, name=name, grid=(R // tr,),
        in_specs=[pl.BlockSpec((P, tr, C), lambda i: (0, i, 0))], out_specs=_rb(tr, C),
        out_shape=_sds((R, C), F32), compiler_params=_cp(1),
    )(parts)


def _adamw(name, w, g, m, v):
    R, C = w.shape
    tr = _pick(R, 256, 8) if R % 8 == 0 else R
    c1 = 1.0 - ADAM_B1 ** ADAM_STEP
    c2 = 1.0 - ADAM_B2 ** ADAM_STEP

    def body(w_ref, g_ref, m_ref, v_ref, d_ref, mo_ref, vo_ref):
        gv = g_ref[...]
        mn = ADAM_B1 * m_ref[...] + (1.0 - ADAM_B1) * gv
        vn = ADAM_B2 * v_ref[...] + (1.0 - ADAM_B2) * (gv * gv)
        mo_ref[...] = mn
        vo_ref[...] = vn
        d_ref[...] = -ADAM_LR * ((mn / c1) / (jnp.sqrt(vn / c2) + ADAM_EPS) + ADAM_WD * w_ref[...])

    spec = _rb(tr, C)
    return pl.pallas_call(
        body, name=name, grid=(R // tr,),
        in_specs=[spec] * 4, out_specs=[spec] * 3, out_shape=[_sds((R, C), F32)] * 3, compiler_params=_cp(1),
    )(w, g, m, v)


def _ag_small(name, blk):
    R, C = blk.shape

    def body(x_ref, out_ref, send_sems, recv_sems, local_sem):
        x, y, c = lax.axis_index("x"), lax.axis_index("y"), lax.axis_index("c")
        me, sibling = (x, y, c), (x, y, 1 - c)
        chips = [(1 - x, y), (x, 1 - y), (1 - x, 1 - y)]

        def rows(px, py, pc):
            return out_ref.at[pl.ds((4 * px + 2 * py + pc) * R, R), :]

        def copy(k, block, to, src=None):
            return pltpu.make_async_remote_copy(
                src_ref=rows(*block) if src is None else src, dst_ref=rows(*block),
                send_sem=send_sems.at[k], recv_sem=recv_sems.at[k], device_id=to, device_id_type=MESH)

        mine = pltpu.make_async_copy(x_ref, rows(*me), local_sem)
        mine.start()
        first = [copy(0, me, sibling, src=x_ref)]
        first += [copy(1 + j, me, (*chip, c), src=x_ref) for j, chip in enumerate(chips)]
        for cp in first:
            cp.start()
        passed = [copy(4 + j, (*chip, c), sibling) for j, chip in enumerate(chips)]
        for j, chip in enumerate(chips):
            copy(1 + j, (*chip, c), me).wait_recv()
            passed[j].start()
        copy(0, sibling, me).wait_recv()
        for j, chip in enumerate(chips):
            copy(4 + j, (*chip, 1 - c), me).wait_recv()
        for cp in first + passed:
            cp.wait_send()
        mine.wait()

    return pl.pallas_call(
        body, name=name, out_shape=_sds((8 * R, C), blk.dtype),
        in_specs=[pl.BlockSpec(memory_space=pltpu.VMEM)], out_specs=pl.BlockSpec(memory_space=pltpu.VMEM),
        scratch_shapes=[pltpu.SemaphoreType.DMA((7,)), pltpu.SemaphoreType.DMA((7,)), pltpu.SemaphoreType.DMA],
    )(blk)


def _row_chunks(ts, row_axis):
    pieces = []
    for t, a in enumerate(ts):
        rows = a.shape[row_axis]
        n = 4 if rows >= 1024 else 1
        pieces += [(t, i * (rows // n), rows // n) for i in range(n)]
    return pieces


def _ag_weights(ts):
    nt = len(ts)
    pieces = _row_chunks(ts, 1)
    NP = len(pieces)

    def body(*refs):
        w, out = refs[:nt], refs[nt:2 * nt]
        send_sems, recv_sems, local_sems = refs[2 * nt:]
        x, y, c = lax.axis_index("x"), lax.axis_index("y"), lax.axis_index("c")
        me, sibling = (x, y, c), (x, y, 1 - c)
        mine = 2 * x + y
        chips = [(1 - x, y), (x, 1 - y), (1 - x, 1 - y)]

        def blk(t, shard, layer, r0, nr):
            return out[t].at[shard, layer, r0:r0 + nr, :]

        def copy(k, dst, to, src=None):
            return pltpu.make_async_remote_copy(
                src_ref=dst if src is None else src, dst_ref=dst, send_sem=send_sems.at[k], recv_sem=recv_sems.at[k],
                device_id=to, device_id_type=MESH)

        own = [pltpu.make_async_copy(w[t], out[t].at[mine], local_sems.at[t]) for t in range(nt)]
        for cp in own:
            cp.start()
        sent = []
        for j, chip in enumerate(chips):
            for pi, (t, r0, nr) in enumerate(pieces):
                sent.append(copy(j * NP + pi, blk(t, mine, c, r0, nr), (*chip, c), src=w[t].at[c, r0:r0 + nr, :]))
                sent[-1].start()
        for j, chip in enumerate(chips):
            theirs = 2 * chip[0] + chip[1]
            for pi, (t, r0, nr) in enumerate(pieces):
                copy(j * NP + pi, blk(t, theirs, c, r0, nr), me).wait_recv()
                sent.append(copy((3 + j) * NP + pi, blk(t, theirs, c, r0, nr), sibling))
                sent[-1].start()
        for j, chip in enumerate(chips):
            theirs = 2 * chip[0] + chip[1]
            for pi, (t, r0, nr) in enumerate(pieces):
                copy((3 + j) * NP + pi, blk(t, theirs, 1 - c, r0, nr), me).wait_recv()
        for cp in sent:
            cp.wait_send()
        for cp in own:
            cp.wait()

    hbm = pl.BlockSpec(memory_space=pl.ANY)
    return pl.pallas_call(
        body, name="ag_weights", out_shape=[_sds((4,) + a.shape, a.dtype) for a in ts],
        in_specs=[hbm] * nt, out_specs=[hbm] * nt,
        scratch_shapes=[pltpu.SemaphoreType.DMA((6 * NP,)), pltpu.SemaphoreType.DMA((6 * NP,)),
                        pltpu.SemaphoreType.DMA((nt,))],
    )(*ts)


class _GradExchange:
    def __init__(self, ts, layer):
        self.nt, self.layer = len(ts), layer
        self.pieces = _row_chunks(ts, 1)
        NP = len(self.pieces)
        self.out_shape = [_sds((8,) + a.shape[1:], a.dtype) for a in ts]
        self.scratch = [pltpu.SemaphoreType.DMA((7 * NP,)), pltpu.SemaphoreType.DMA((7 * NP,)),
                        pltpu.SemaphoreType.DMA((self.nt,))]

    def _copies(self, src, out, sems):
        send_sems, recv_sems, local_sems = sems
        NP = len(self.pieces)
        x, y, c = lax.axis_index("x"), lax.axis_index("y"), lax.axis_index("c")
        me = 4 * x + 2 * y + c
        owner = c == self.layer
        own = [pltpu.make_async_copy(src[t].at[2 * x + y], out[t].at[me], local_sems.at[t]) for t in range(self.nt)]
        rel = []
        for k in range(1, 8):
            px = 1 - x if k & 4 else x
            py = 1 - y if k & 2 else y
            source = 4 * px + 2 * py + (1 - c if k & 1 else c)
            sends, recvs = [], []
            for pi, (t, r0, nr) in enumerate(self.pieces):
                idx = (k - 1) * NP + pi
                sends.append(pltpu.make_async_remote_copy(
                    src_ref=src[t].at[2 * px + py, r0:r0 + nr, :], dst_ref=out[t].at[me, r0:r0 + nr, :],
                    send_sem=send_sems.at[idx], recv_sem=recv_sems.at[idx], device_id=(px, py, self.layer),
                    device_id_type=MESH))
                recvs.append(pltpu.make_async_remote_copy(
                    src_ref=out[t].at[source, r0:r0 + nr, :], dst_ref=out[t].at[source, r0:r0 + nr, :],
                    send_sem=send_sems.at[idx], recv_sem=recv_sems.at[idx], device_id=(x, y, c),
                    device_id_type=MESH))
            rel.append((jnp.logical_not(owner) if k & 1 else owner, sends, recvs))
        return owner, own, rel

    def start(self, src, out, sems):
        owner, own, rel = self._copies(src, out, sems)

        @pl.when(owner)
        def _():
            for cp in own:
                cp.start()

        for sending, sends, _ in rel:
            @pl.when(sending)
            def _(sends=sends):
                for cp in sends:
                    cp.start()

    def wait(self, src, out, sems):
        owner, own, rel = self._copies(src, out, sems)

        @pl.when(owner)
        def _():
            for _, _, recvs in rel:
                for cp in recvs:
                    cp.wait_recv()
            for cp in own:
                cp.wait()

        for sending, sends, _ in rel:
            @pl.when(sending)
            def _(sends=sends):
                for cp in sends:
                    cp.wait_send()


def _grad_exchange_alone(ts, layer, prev):
    ex = _GradExchange(ts, layer)
    nt = ex.nt

    def body(*refs):
        src, out, sems = refs[:nt], refs[2 * nt:3 * nt], refs[3 * nt:]
        ex.start(src, out, sems)
        ex.wait(src, out, sems)

    hbm = pl.BlockSpec(memory_space=pl.ANY)
    return pl.pallas_call(
        body, name="grad_exchange", out_shape=ex.out_shape, in_specs=[hbm] * (2 * nt), out_specs=[hbm] * nt,
        input_output_aliases={nt + t: t for t in range(nt)}, scratch_shapes=ex.scratch,
    )(*ts, *prev)


def _sibling_join(ts):
    nt = len(ts)
    pieces = _row_chunks(ts, 0)
    NP = len(pieces)

    def body(*refs):
        src, out = refs[:nt], refs[nt:2 * nt]
        send_sems, recv_sems, local_sems = refs[2 * nt:]
        x, y, c = lax.axis_index("x"), lax.axis_index("y"), lax.axis_index("c")
        own = [pltpu.make_async_copy(src[t], out[t].at[c], local_sems.at[t]) for t in range(nt)]
        for cp in own:
            cp.start()
        sent = []
        for pi, (t, r0, nr) in enumerate(pieces):
            sent.append(pltpu.make_async_remote_copy(
                src_ref=src[t].at[r0:r0 + nr, :], dst_ref=out[t].at[c, r0:r0 + nr, :], send_sem=send_sems.at[pi],
                recv_sem=recv_sems.at[pi], device_id=(x, y, 1 - c), device_id_type=MESH))
            sent[-1].start()
        for pi, (t, r0, nr) in enumerate(pieces):
            pltpu.make_async_remote_copy(
                src_ref=src[t].at[r0:r0 + nr, :], dst_ref=out[t].at[1 - c, r0:r0 + nr, :], send_sem=send_sems.at[pi],
                recv_sem=recv_sems.at[pi], device_id=(x, y, c), device_id_type=MESH).wait_recv()
        for cp in sent:
            cp.wait_send()
        for cp in own:
            cp.wait()

    vmem = pl.BlockSpec(memory_space=pltpu.VMEM)
    return pl.pallas_call(
        body, name="sibling_join", out_shape=[_sds((2,) + a.shape, a.dtype) for a in ts],
        in_specs=[vmem] * nt, out_specs=[vmem] * nt,
        scratch_shapes=[pltpu.SemaphoreType.DMA((NP,)), pltpu.SemaphoreType.DMA((NP,)),
                        pltpu.SemaphoreType.DMA((nt,))],
        compiler_params=pltpu.CompilerParams(vmem_limit_bytes=_VMEM_LIMIT),
    )(*ts)


def _layer_fwd(x, shift, scale, gate, lw):
    D = x.shape[1]
    h = _norm_mod(x, lw["norm_g"], scale, shift)
    p = _matmul("in_proj", h, lw["w_cat"], "nn", F32, tn_cap=896)
    qn, kn = _sb_prep(p, lw["gq_t"], lw["gk_t"])
    o_att = _sb_fwd(qn, kn, p)
    qkv, bb, gcb, glb = _dn_prep(p, lw["conv_w"], lw["a_row"], lw["dtb_row"])
    o_dn, tinv, sall = _dn_fwd(qkv, bb, gcb, glb)
    o_sb, o_dnn = _gate(o_att, o_dn, p, lw["gn"])
    y, b_sb, b_dn = _branch(o_sb, o_dnn, lw["wb_sb"], lw["wb_dn"], p, D)
    x_next, out = _out_proj(x, y, lw["w_out"], gate)
    res = dict(x=x, h=h, p=p, qn=qn, kn=kn, o_att=o_att, qkv=qkv, bb=bb, gcb=gcb, glb=glb, o_dn=o_dn,
               tinv=tinv, sall=sall, o_sb=o_sb, o_dnn=o_dnn, y=y, b_sb=b_sb, b_dn=b_dn, out=out,
               shift=shift, scale=scale, gate=gate)
    return x_next, res


def _layer_bwd(dxn, res, lw, pending=None):
    p = res["p"]
    dout, db_sb, db_dn, dm, dgate = _out_bwd(dxn, res["out"], res["gate"], lw["w_out"], p, res["b_sb"], res["b_dn"])
    dw_out = _matmul("dw_out", res["y"], dout, "tn", _MXU_DTYPE)
    dwb_sb = _matmul("dwb_sb", res["o_sb"], db_sb, "tn", _MXU_DTYPE)
    dwb_dn = _matmul("dwb_dn", res["o_dnn"], db_dn, "tn", _MXU_DTYPE)
    do_att, dz_sb, do_dn, dz_dn, dgn = _gate_bwd(db_sb, db_dn, lw["wb_sb"], lw["wb_dn"], res["o_att"], res["o_dn"],
                                                  p, lw["gn"])
    if pending is None:
        (dqn, dkn, dv), received = _sb_bwd(res["qn"], res["kn"], p, do_att)
    else:
        (dqn, dkn, dv), received = _sb_bwd(res["qn"], res["kn"], p, do_att,
                                           _GradExchange(pending[1], pending[0]), pending[1])
    dq_sb, dk_sb, dgq, dgk = _sb_prep_bwd(p, dqn, dkn, lw["gq_t"], lw["gk_t"])
    dqkv, dbb, dgb = _dn_bwd(res["qkv"], res["bb"], res["gcb"], res["glb"], res["tinv"], res["sall"], do_dn)
    dc, dp_ba, dal, ddt = _dn_prep_bwd_a(p, dqkv, dbb, dgb, lw["conv_w"], lw["a_row"], lw["dtb_row"])
    dp_dn, dconv = _dn_prep_bwd_b(p, dc, lw["conv_w"])
    dp = jnp.concatenate([dp_dn, dz_dn, dq_sb, dk_sb, dv.astype(_MXU_DTYPE), dz_sb, dm, dp_ba], axis=1)
    dh = _matmul("dh", dp, lw["w_cat"], "nt", F32, tk_cap=896)
    dw_cat = _matmul("dw_cat", res["h"], dp, "tn", _MXU_DTYPE, tm_cap=1024, tn_cap=896, tk_cap=512)
    dx, dshift, dscale, dnorm_g = _norm_mod_bwd(res["x"], dh, dxn, lw["norm_g"], res["scale"])
    small = dict(dmod=jnp.concatenate([dshift, dscale, dgate], axis=1)[0], norm_g=dnorm_g[0],
                 sb_q_g=dgq.reshape(SB_HEADS, SB_HD).sum(0), sb_k_g=dgk.reshape(SB_HEADS, SB_HD).sum(0),
                 conv_w=dconv, dn_a_log=dal[0, DN_HEADS:2 * DN_HEADS], dn_dt_bias=ddt[0, DN_HEADS:2 * DN_HEADS],
                 dn_norm_g=dgn[0])
    D = dxn.shape[1]
    by_shard = lambda g: g.reshape(g.shape[0], 4, g.shape[1] // 4).transpose(1, 0, 2)
    send = [by_shard(_uncat_cols(dw_cat, D)), by_shard(dwb_sb), by_shard(dwb_dn), dw_out.reshape(4, D // 4, D)]
    return dx, small, send, received


def _cat_cols(w, D):
    return jnp.concatenate([w[:, 2048:4096], w[:, 0:2048], w[:, 4104:4104 + 2 * D], w[:, 4096:4104],
                            jnp.zeros((w.shape[0], LANES - 8), w.dtype)], axis=1)


def _uncat_cols(g, D):
    return jnp.concatenate([g[:, 2048:4096], g[:, 0:2048], g[:, 4096 + 2 * D:4096 + 2 * D + 8],
                            g[:, 4096:4096 + 2 * D]], axis=1)


def _flat_pack(arrs, mult):
    flat = jnp.concatenate([a.reshape(-1) for a in arrs])
    n = flat.shape[0]
    pad = (-n) % mult
    if pad:
        flat = jnp.concatenate([flat, jnp.zeros((pad,), flat.dtype)])
    return flat.reshape(-1, LANES)


def _flat_unpack(flat, shapes):
    flat = flat.reshape(-1)
    out, off = [], 0
    for s in shapes:
        n = math.prod(s)
        out.append(flat[off:off + n].reshape(s))
        off += n
    return out


BIG = ("w_in", "w_branch_sb", "w_branch_dn", "w_out")
SMALL = ("ada_b", "norm_g", "sb_q_g", "sb_k_g", "conv_w", "dn_a_log", "dn_dt_bias", "dn_norm_g")


def kernel(x, c, ada_w, ada_b, norm_g, w_in, sb_q_g, sb_k_g, conv_w, dn_a_log, dn_dt_bias, dn_norm_g, w_branch_sb, w_branch_dn, w_out, loss_target, m_ada_w, m_ada_b, m_norm_g, m_w_in, m_sb_q_g, m_sb_k_g, m_conv_w, m_dn_a_log, m_dn_dt_bias, m_dn_norm_g, m_w_branch_sb, m_w_branch_dn, m_w_out, v_ada_w, v_ada_b, v_norm_g, v_w_in, v_sb_q_g, v_sb_k_g, v_conv_w, v_dn_a_log, v_dn_dt_bias, v_dn_norm_g, v_w_branch_sb, v_w_branch_dn, v_w_out):
    W = dict(ada_w=ada_w, ada_b=ada_b, norm_g=norm_g, w_in=w_in, sb_q_g=sb_q_g, sb_k_g=sb_k_g, conv_w=conv_w,
             dn_a_log=dn_a_log, dn_dt_bias=dn_dt_bias, dn_norm_g=dn_norm_g, w_branch_sb=w_branch_sb,
             w_branch_dn=w_branch_dn, w_out=w_out)
    M = dict(ada_w=m_ada_w, ada_b=m_ada_b, norm_g=m_norm_g, w_in=m_w_in, sb_q_g=m_sb_q_g, sb_k_g=m_sb_k_g,
             conv_w=m_conv_w, dn_a_log=m_dn_a_log, dn_dt_bias=m_dn_dt_bias, dn_norm_g=m_dn_norm_g,
             w_branch_sb=m_w_branch_sb, w_branch_dn=m_w_branch_dn, w_out=m_w_out)
    V = dict(ada_w=v_ada_w, ada_b=v_ada_b, norm_g=v_norm_g, w_in=v_w_in, sb_q_g=v_sb_q_g, sb_k_g=v_sb_k_g,
             conv_w=v_conv_w, dn_a_log=v_dn_a_log, dn_dt_bias=v_dn_dt_bias, dn_norm_g=v_dn_norm_g,
             w_branch_sb=v_w_branch_sb, w_branch_dn=v_w_branch_dn, w_out=v_w_out)
    L = ada_w.shape[0]
    S, D = x.shape[1], x.shape[2]
    ix, iy, ic = lax.axis_index("x"), lax.axis_index("y"), lax.axis_index("c")
    shard = 2 * ix + iy
    me = 2 * shard + ic
    n_ada = ada_w.shape[2]
    n_in = w_in.shape[2]
    n_conv = conv_w.shape[2]
    n_br = w_branch_sb.shape[2]
    n_out = w_out.shape[1]

    g_in, g_bs, g_bd, g_out = _ag_weights([W[n].astype(_MXU_DTYPE) for n in BIG])
    w_in_f = jnp.concatenate([g_in[s] for s in range(4)], axis=2)
    wb_sb_f = jnp.concatenate([g_bs[s] for s in range(4)], axis=2)
    wb_dn_f = jnp.concatenate([g_bd[s] for s in range(4)], axis=2)
    w_out_f = jnp.concatenate([g_out[s] for s in range(4)], axis=1)

    g1 = _ag_small("ag_c_conv", _flat_pack([c, conv_w], LANES * 8))
    g1 = g1.reshape(8, -1)
    c_all = g1[:, :D]
    conv_parts = g1[:, D:D + L * CONV_K * n_conv].reshape(4, 2, L, CONV_K, n_conv)[:, 0]
    conv_full = jnp.concatenate([conv_parts[s] for s in range(4)], axis=2)
    ada_b_sh = lax.dynamic_slice_in_dim(ada_b, shard * n_ada, n_ada, axis=1)[:, None, :]
    mod_sh = _ada_fwd(c_all, ada_w, ada_b_sh)
    g2 = _ag_small("ag_mod", _flat_pack([mod_sh], LANES * 8)).reshape(8, -1)
    mod_parts = g2[:, :L * 8 * n_ada].reshape(4, 2, L, 8, n_ada)[:, 0]
    mod_all = jnp.concatenate([mod_parts[s] for s in range(4)], axis=2)
    mod = lax.dynamic_index_in_dim(mod_all, me, axis=1, keepdims=False)

    def layer_weights(l):
        pad_lo = jnp.zeros((DN_HEADS,), F32)
        pad_hi = jnp.zeros((LANES - 2 * DN_HEADS,), F32)
        return dict(
            norm_g=norm_g[l][None, :], w_cat=_cat_cols(w_in_f[l], D),
            gq_t=jnp.tile(sb_q_g[l], SB_HEADS)[None, :], gk_t=jnp.tile(sb_k_g[l], SB_HEADS)[None, :],
            conv_w=conv_full[l],
            a_row=jnp.concatenate([pad_lo, dn_a_log[l], pad_hi])[None, :],
            dtb_row=jnp.concatenate([pad_lo, dn_dt_bias[l], pad_hi])[None, :],
            gn=dn_norm_g[l][None, :], wb_sb=wb_sb_f[l], wb_dn=wb_dn_f[l], w_out=w_out_f[l])

    xs = x[0]
    lws, ress = [], []
    for l in range(L):
        lw = layer_weights(l)
        xs, res = _layer_fwd(xs, mod[l, None, 0:D], mod[l, None, D:2 * D], mod[l, None, 2 * D:3 * D], lw)
        lws.append(lw)
        ress.append(res)
    dxs, loss_row = _loss_head(xs, loss_target[0])
    loss = lax.psum(loss_row[0, 0], ("x", "y", "c"))
    assert L == 2, "the owner of a layer's gradients is the core with the layer's number"
    smalls = [None] * L
    dxs, smalls[1], send1, _ = _layer_bwd(dxs, ress[1], lws[1])
    dxs, smalls[0], send0, got = _layer_bwd(dxs, ress[0], lws[0], (1, send1))
    grad_x = dxs[None]

    small_names = ("dmod",) + SMALL[1:]
    small_pack = _flat_pack([jnp.stack([smalls[l][n] for l in range(L)]) for n in small_names], LANES * 8)
    g3 = _ag_small("ag_small_grads", small_pack)
    R3 = small_pack.shape[0]
    g3 = g3.reshape(8, R3, LANES)
    small_sum = _sum_parts("sum_small", g3)
    small_shapes = [(L, 3 * D), (L, D), (L, SB_HD), (L, SB_HD), (L, CONV_K, 3 * DN_W), (L, DN_HEADS), (L, DN_HEADS),
                    (L, DN_HD)]
    sg = dict(zip(small_names, _flat_unpack(small_sum, small_shapes)))
    G = dict(ada_b=sg["dmod"], norm_g=sg["norm_g"], sb_q_g=sg["sb_q_g"], sb_k_g=sg["sb_k_g"],
             conv_w=lax.dynamic_slice_in_dim(sg["conv_w"], shard * n_conv, n_conv, axis=2),
             dn_a_log=sg["dn_a_log"], dn_dt_bias=sg["dn_dt_bias"], dn_norm_g=sg["dn_norm_g"])
    dmod_all = g3.reshape(8, -1)[:, :L * 3 * D].reshape(8, L, 3 * D)
    dmod_sh = lax.dynamic_slice_in_dim(dmod_all, shard * n_ada, n_ada, axis=2).transpose(1, 0, 2)
    G["ada_w"] = _ada_bwd(c_all.T, dmod_sh)

    got = _grad_exchange_alone(send0, 0, got)
    mine = [_sum_parts("sum_" + n, g) for n, g in zip(BIG, got)]
    for n, g in zip(BIG, _sibling_join(mine)):
        G[n] = g

    delta, new_m, new_v = {}, {}, {}
    for n in ("ada_w",) + BIG:
        sh = W[n].shape
        two = (sh[0] * sh[1], sh[2])
        d, mo, vo = _adamw("adamw_" + n, W[n].reshape(two), G[n].reshape(two), M[n].reshape(two), V[n].reshape(two))
        delta[n], new_m[n], new_v[n] = d.reshape(sh), mo.reshape(sh), vo.reshape(sh)
    sm_shapes = [W[n].shape for n in SMALL]
    d, mo, vo = _adamw("adamw_small", _flat_pack([W[n] for n in SMALL], LANES * 8),
                       _flat_pack([G[n] for n in SMALL], LANES * 8), _flat_pack([M[n] for n in SMALL], LANES * 8),
                       _flat_pack([V[n] for n in SMALL], LANES * 8))
    for n, dd, mm, vv in zip(SMALL, _flat_unpack(d, sm_shapes), _flat_unpack(mo, sm_shapes),
                             _flat_unpack(vo, sm_shapes)):
        delta[n], new_m[n], new_v[n] = dd, mm, vv

    order = ("ada_w", "ada_b", "norm_g", "w_in", "sb_q_g", "sb_k_g", "conv_w", "dn_a_log", "dn_dt_bias", "dn_norm_g",
             "w_branch_sb", "w_branch_dn", "w_out")
    return (loss, grad_x, *[G[n] for n in order], *[delta[n] for n in order], *[new_m[n] for n in order],
            *[new_v[n] for n in order])
```

```python
import math

import jax
import jax.numpy as jnp
from jax import lax
from jax.experimental import pallas as pl
from jax.experimental.pallas import tpu as pltpu

F32 = jnp.float32
BF16 = jnp.bfloat16
_MXU_DTYPE = BF16
_VMEM_LIMIT = 48 * 1024 * 1024
LANES = 128

EPS = 1e-6
SB_HEADS, SB_HD, SB_W = 8, 64, 512
DN_HEADS, DN_HD, DN_W = 4, 128, 512
CONV_K = 4
CHUNK = 64
QB = 256
_SB_DEAD = 104.0
ADAM_LR, ADAM_B1, ADAM_B2, ADAM_EPS, ADAM_WD, ADAM_STEP = 0.001, 0.9, 0.999, 1e-08, 0.01, 10

C_DN_QKV, C_DN_Z, C_SB_Q, C_SB_K, C_SB_V, C_SB_Z, C_MG = 0, 1536, 2048, 2560, 3072, 3584, 4096

_NN = (((1,), (0,)), ((), ()))
_NT = (((1,), (1,)), ((), ()))
_TN = (((0,), (0,)), ((), ()))
_BNN = (((2,), (1,)), ((0,), (0,)))
_BNT = (((2,), (2,)), ((0,), (0,)))
_BTN = (((1,), (1,)), ((0,), (0,)))
MESH = pl.DeviceIdType.MESH


def _sds(shape, dtype):
    return jax.ShapeDtypeStruct(shape, dtype)


def _cp(n):
    return pltpu.CompilerParams(dimension_semantics=("arbitrary",) * n, vmem_limit_bytes=_VMEM_LIMIT)


def _rb(tm, w, cb=0):
    return pl.BlockSpec((tm, w), lambda i: (i, cb))


def _fs(shape):
    nd = len(shape)
    return pl.BlockSpec(shape, lambda i: (0,) * nd)


def _dg(a, b, dims):
    return lax.dot_general(a, b, dims, preferred_element_type=F32)


def _mm(a, b, dims=_NN):
    return _dg(a.astype(_MXU_DTYPE), b.astype(_MXU_DTYPE), dims)


def _split3(x):
    hi = x.astype(BF16)
    r = x - hi.astype(F32)
    mid = r.astype(BF16)
    lo = (r - mid.astype(F32)).astype(BF16)
    return hi, mid, lo


def _mm_xl(x, const, dims=_NN):
    cb = const.astype(BF16)
    hi, mid, lo = _split3(x)
    return _dg(hi, cb, dims) + _dg(mid, cb, dims) + _dg(lo, cb, dims)


def _mm_xl2(x, const, dims=_NN):
    cb = const.astype(BF16)
    hi = x.astype(BF16)
    lo = (x - hi.astype(F32)).astype(BF16)
    return _dg(hi, cb, dims) + _dg(lo, cb, dims)


def _mm_xr(const, x, dims=_NN):
    cb = const.astype(BF16)
    hi, mid, lo = _split3(x)
    return _dg(cb, hi, dims) + _dg(cb, mid, dims) + _dg(cb, lo, dims)


def _mm3(a, b, dims=_NN):
    ah, am, _ = _split3(a)
    bh, bm, _ = _split3(b)
    return _dg(ah, bh, dims) + (_dg(ah, bm, dims) + _dg(am, bh, dims))


def _sigmoid(z):
    return 1.0 / (1.0 + jnp.exp(-z))


def _silu(z):
    return z * _sigmoid(z)


def _dsilu(z):
    s = _sigmoid(z)
    return s * (1.0 + z * (1.0 - s))


def _softplus(z):
    return jnp.maximum(z, 0.0) + jnp.log(1.0 + jnp.exp(-jnp.abs(z)))


def _iota2(shape, dim):
    return lax.broadcasted_iota(jnp.int32, shape, dim)


def _pick(n, cap, mult):
    best = None
    for t in range(mult, min(n, cap) + 1, mult):
        if n % t == 0:
            best = t
    assert best is not None, (n, cap, mult)
    return best


def _matmul(name, a, b, form, out_dtype, tm_cap=512, tn_cap=1024, tk_cap=1024):
    if form == "nn":
        (M, K), (_, N) = a.shape, b.shape
    elif form == "nt":
        (M, K), (N, _) = a.shape, b.shape
    else:
        (K, M), (_, N) = a.shape, b.shape
    tm = _pick(M, tm_cap, 128 if form == "tn" else 8)
    tn = _pick(N, tn_cap, 128)
    tk = _pick(K, tk_cap, 128)
    nk = K // tk
    dims = {"nn": _NN, "nt": _NT, "tn": _TN}[form]
    if form == "nn":
        a_spec = pl.BlockSpec((tm, tk), lambda i, j, k: (i, k))
        b_spec = pl.BlockSpec((tk, tn), lambda i, j, k: (k, j))
    elif form == "nt":
        a_spec = pl.BlockSpec((tm, tk), lambda i, j, k: (i, k))
        b_spec = pl.BlockSpec((tn, tk), lambda i, j, k: (j, k))
    else:
        a_spec = pl.BlockSpec((tk, tm), lambda i, j, k: (k, i))
        b_spec = pl.BlockSpec((tk, tn), lambda i, j, k: (k, j))

    def body(a_ref, b_ref, o_ref, acc_ref):
        k = pl.program_id(2)

        @pl.when(k == 0)
        def _():
            acc_ref[...] = jnp.zeros_like(acc_ref)

        acc_ref[...] += _mm(a_ref[...], b_ref[...], dims)

        @pl.when(k == nk - 1)
        def _():
            o_ref[...] = acc_ref[...].astype(o_ref.dtype)

    return pl.pallas_call(
        body, name=name, grid=(M // tm, N // tn, nk),
        in_specs=[a_spec, b_spec],
        out_specs=pl.BlockSpec((tm, tn), lambda i, j, k: (i, j)),
        out_shape=_sds((M, N), out_dtype),
        scratch_shapes=[pltpu.VMEM((tm, tn), F32)],
        compiler_params=_cp(3),
    )(a, b)


def _norm_mod(x, g, scale, shift, tm=256):
    S, D = x.shape

    def body(x_ref, g_ref, sc_ref, sh_ref, h_ref):
        xv = x_ref[...]
        r = lax.rsqrt(jnp.mean(xv * xv, axis=1, keepdims=True) + EPS)
        h_ref[...] = ((xv * r * g_ref[...]) * (1.0 + sc_ref[...]) + sh_ref[...]).astype(h_ref.dtype)

    return pl.pallas_call(
        body, name="norm_mod", grid=(S // tm,),
        in_specs=[_rb(tm, D), _fs((1, D)), _fs((1, D)), _fs((1, D))],
        out_specs=_rb(tm, D), out_shape=_sds((S, D), _MXU_DTYPE), compiler_params=_cp(1),
    )(x, g, scale, shift)


def _norm_mod_bwd(x, dh, dxn, g, scale, tm=256):
    S, D = x.shape

    def body(x_ref, dh_ref, dxn_ref, g_ref, sc_ref, dx_ref, dsh_ref, dsc_ref, dg_ref):
        @pl.when(pl.program_id(0) == 0)
        def _():
            dsh_ref[...] = jnp.zeros_like(dsh_ref)
            dsc_ref[...] = jnp.zeros_like(dsc_ref)
            dg_ref[...] = jnp.zeros_like(dg_ref)

        xv, dhv, gv = x_ref[...], dh_ref[...], g_ref[...]
        r = lax.rsqrt(jnp.mean(xv * xv, axis=1, keepdims=True) + EPS)
        xh = xv * r
        one_sc = 1.0 + sc_ref[...]
        dsh_ref[...] += jnp.sum(dhv, axis=0, keepdims=True)
        dsc_ref[...] += jnp.sum(dhv * xh * gv, axis=0, keepdims=True)
        dg_ref[...] += jnp.sum(dhv * one_sc * xh, axis=0, keepdims=True)
        dxh = dhv * (gv * one_sc)
        dx_ref[...] = r * (dxh - xh * jnp.mean(dxh * xh, axis=1, keepdims=True)) + dxn_ref[...]

    return pl.pallas_call(
        body, name="norm_mod_bwd", grid=(S // tm,),
        in_specs=[_rb(tm, D), _rb(tm, D), _rb(tm, D), _fs((1, D)), _fs((1, D))],
        out_specs=[_rb(tm, D), _fs((1, D)), _fs((1, D)), _fs((1, D))],
        out_shape=[_sds((S, D), F32)] + [_sds((1, D), F32)] * 3, compiler_params=_cp(1),
    )(x, dh, dxn, g, scale)


def _head_sum_matrix():
    r = jnp.arange(SB_W)
    return (r[:, None] // SB_HD == r[None, :] // SB_HD).astype(BF16)


def _sb_prep(p, gq_t, gk_t, tm=256):
    S = p.shape[0]
    bd = _head_sum_matrix()

    def body(q_ref, k_ref, gq_ref, gk_ref, bd_ref, qn_ref, kn_ref):
        for src, g_ref, dst in ((q_ref, gq_ref, qn_ref), (k_ref, gk_ref, kn_ref)):
            v = src[...]
            ms = _mm_xl(v * v, bd_ref[...]) * (1.0 / SB_HD)
            dst[...] = (v * lax.rsqrt(ms + EPS) * g_ref[...]).astype(dst.dtype)

    return pl.pallas_call(
        body, name="sb_prep", grid=(S // tm,),
        in_specs=[_rb(tm, SB_W, C_SB_Q // SB_W), _rb(tm, SB_W, C_SB_K // SB_W),
                  _fs((1, SB_W)), _fs((1, SB_W)), _fs((SB_W, SB_W))],
        out_specs=[_rb(tm, SB_W), _rb(tm, SB_W)],
        out_shape=[_sds((S, SB_W), _MXU_DTYPE)] * 2, compiler_params=_cp(1),
    )(p, p, gq_t, gk_t, bd)


def _sb_prep_bwd(p, dqn, dkn, gq_t, gk_t, tm=256):
    S = p.shape[0]
    bd = _head_sum_matrix()

    def body(q_ref, k_ref, dqn_ref, dkn_ref, gq_ref, gk_ref, bd_ref, dq_ref, dk_ref, dgq_ref, dgk_ref):
        @pl.when(pl.program_id(0) == 0)
        def _():
            dgq_ref[...] = jnp.zeros_like(dgq_ref)
            dgk_ref[...] = jnp.zeros_like(dgk_ref)

        for src, dn_ref, g_ref, dst, dg_ref in ((q_ref, dqn_ref, gq_ref, dq_ref, dgq_ref),
                                                (k_ref, dkn_ref, gk_ref, dk_ref, dgk_ref)):
            v, dn = src[...], dn_ref[...]
            r = lax.rsqrt(_mm_xl(v * v, bd_ref[...]) * (1.0 / SB_HD) + EPS)
            vh = v * r
            dg_ref[...] += jnp.sum(dn * vh, axis=0, keepdims=True)
            dvh = dn * g_ref[...]
            m = _mm_xl(dvh * vh, bd_ref[...]) * (1.0 / SB_HD)
            dst[...] = (r * (dvh - vh * m)).astype(dst.dtype)

    return pl.pallas_call(
        body, name="sb_prep_bwd", grid=(S // tm,),
        in_specs=[_rb(tm, SB_W, C_SB_Q // SB_W), _rb(tm, SB_W, C_SB_K // SB_W), _rb(tm, SB_W), _rb(tm, SB_W),
                  _fs((1, SB_W)), _fs((1, SB_W)), _fs((SB_W, SB_W))],
        out_specs=[_rb(tm, SB_W), _rb(tm, SB_W), _fs((1, SB_W)), _fs((1, SB_W))],
        out_shape=[_sds((S, SB_W), _MXU_DTYPE)] * 2 + [_sds((1, SB_W), F32)] * 2, compiler_params=_cp(1),
    )(p, p, dqn, dkn, gq_t, gk_t, bd)


def _sb_consts():
    r, c = _iota2((QB, QB), 0), _iota2((QB, QB), 1)
    lane = _iota2((1, LANES), 1)
    return r, c, lane


def _sb_fwd(qn, kn, p):
    S = qn.shape[0]
    scale = 1.0 / math.sqrt(SB_HD)

    def body(q_ref, k_ref, v_ref, o_ref):
        i = pl.program_id(1)
        r, c, lane = _sb_consts()
        u_gt = (r > c).astype(BF16)
        strict = jnp.concatenate([c < r, c < r], axis=0)
        q = q_ref[...]
        mask0 = (lane // SB_HD) == 0
        zero = jnp.zeros_like(q)
        qh = jnp.concatenate([jnp.where(mask0, q, zero), jnp.where(mask0, zero, q)], axis=0)

        def block(off, carry, diagonal):
            o, run = carry
            kj = k_ref[pl.ds(off, QB), :]
            vj = v_ref[pl.ds(off, QB), :].astype(_MXU_DTYPE)
            z = _mm(qh, kj, _NT) * scale
            sp = _softplus(z)
            sp_m = jnp.where(strict, sp, 0.0) if diagonal else sp
            later = _mm_xl2(sp_m, u_gt)
            w = jnp.exp((z - sp) - later - run)
            if diagonal:
                w = jnp.where(strict, w, 0.0)
            return o + _mm(w, vj), run + jnp.sum(sp_m, axis=1, keepdims=True)

        init = (jnp.zeros((2 * QB, LANES), F32), jnp.zeros((2 * QB, 1), F32))
        carry = block(pl.multiple_of(i * QB, QB), init, True)
        st = lax.while_loop(
            lambda st: jnp.logical_and(st[0] <= i, jnp.min(st[2]) < _SB_DEAD),
            lambda st: (st[0] + 1,) + block(pl.multiple_of((i - st[0]) * QB, QB), st[1:], False),
            (jnp.int32(1),) + carry)
        o_ref[...] = jnp.where(mask0, st[1][:QB], st[1][QB:])

    return pl.pallas_call(
        body, name="sb_fwd", grid=(SB_W // LANES, S // QB),
        in_specs=[pl.BlockSpec((QB, LANES), lambda hp, i: (i, hp)),
                  pl.BlockSpec((S, LANES), lambda hp, i: (0, hp)),
                  pl.BlockSpec((S, LANES), lambda hp, i: (0, C_SB_V // LANES + hp))],
        out_specs=pl.BlockSpec((QB, LANES), lambda hp, i: (i, hp)),
        out_shape=_sds((S, SB_W), F32), compiler_params=_cp(2),
    )(qn, kn, p)


def _sb_bwd(qn, kn, p, do, exchange=None, ex_src=()):
    S = qn.shape[0]
    scale = 1.0 / math.sqrt(SB_HD)
    grid = (SB_W // LANES, S // QB)
    nx = len(ex_src)

    def body(*refs):
        if exchange is None:
            compute(*refs)
            return
        src, xout, sems = refs[4:4 + nx], refs[7 + nx:7 + 2 * nx], refs[7 + 2 * nx:]
        hp, i = pl.program_id(0), pl.program_id(1)

        @pl.when(jnp.logical_and(hp == 0, i == 0))
        def _():
            exchange.start(src, xout, sems)

        compute(*refs[:4], *refs[4 + nx:7 + nx])

        @pl.when(jnp.logical_and(hp == grid[0] - 1, i == grid[1] - 1))
        def _():
            exchange.wait(src, xout, sems)

    def compute(q_ref, k_ref, v_ref, do_ref, dq_ref, dk_ref, dv_ref):
        i = pl.program_id(1)

        @pl.when(i == 0)
        def _():
            dk_ref[...] = jnp.zeros_like(dk_ref)
            dv_ref[...] = jnp.zeros_like(dv_ref)

        r, c, lane = _sb_consts()
        u_le = (r <= c).astype(BF16)
        u_lt = (r < c).astype(BF16)
        strict = jnp.concatenate([c < r, c < r], axis=0)
        q = q_ref[...]
        do = do_ref[...].astype(_MXU_DTYPE)
        mask0 = (lane // SB_HD) == 0
        zero, zero_do = jnp.zeros_like(q), jnp.zeros_like(do)
        qh = jnp.concatenate([jnp.where(mask0, q, zero), jnp.where(mask0, zero, q)], axis=0)
        doh = jnp.concatenate([jnp.where(mask0, do, zero_do), jnp.where(mask0, zero_do, do)], axis=0)

        def sums(off, run, diagonal):
            sp = _softplus(_mm(qh, k_ref[pl.ds(off, QB), :], _NT) * scale)
            if diagonal:
                sp = jnp.where(strict, sp, 0.0)
            return run + jnp.sum(sp, axis=1, keepdims=True)

        run = sums(pl.multiple_of(i * QB, QB), jnp.zeros((2 * QB, 1), F32), True)
        nb, tot = lax.while_loop(
            lambda st: jnp.logical_and(st[0] <= i, jnp.min(st[1]) < _SB_DEAD),
            lambda st: (st[0] + 1, sums(pl.multiple_of((i - st[0]) * QB, QB), st[1], False)),
            (jnp.int32(1), run))
        first = i + 1 - nb

        def block(off, carry, diagonal):
            dq, pre_sp, pre_e = carry
            kj = k_ref[pl.ds(off, QB), :]
            vj = v_ref[pl.ds(off, QB), :].astype(_MXU_DTYPE)
            z = _mm(qh, kj, _NT) * scale
            sp = _softplus(z)
            a = z - sp
            sp_m = jnp.where(strict, sp, 0.0) if diagonal else sp
            incl = _mm_xl2(sp_m, u_le)
            w = jnp.exp(a - ((tot - pre_sp) - incl))
            if diagonal:
                w = jnp.where(strict, w, 0.0)
            e = w * _mm(doh, vj, _NT)
            db = pre_e + _mm_xl2(e, u_lt)
            dz = (e - jnp.exp(a) * (e + db)) * scale
            if diagonal:
                dz = jnp.where(strict, dz, 0.0)
            dk_ref[pl.ds(off, QB), :] += _mm(dz, qh, _TN)
            dv_ref[pl.ds(off, QB), :] += _mm(w, doh, _TN)
            return (dq + _mm(dz, kj), pre_sp + jnp.sum(sp_m, axis=1, keepdims=True),
                    pre_e + jnp.sum(e, axis=1, keepdims=True))

        zero_col = jnp.zeros((2 * QB, 1), F32)
        init = (jnp.zeros((2 * QB, LANES), F32), zero_col, zero_col)
        carry = lax.fori_loop(first, i, lambda j, cr: block(pl.multiple_of(j * QB, QB), cr, False), init)
        carry = block(pl.multiple_of(i * QB, QB), carry, True)
        dq_ref[...] = jnp.where(mask0, carry[0][:QB], carry[0][QB:])

    blk = pl.BlockSpec((QB, LANES), lambda hp, i: (i, hp))
    full = pl.BlockSpec((S, LANES), lambda hp, i: (0, hp))
    hbm = pl.BlockSpec(memory_space=pl.ANY)
    outs = pl.pallas_call(
        body, name="sb_bwd", grid=grid,
        in_specs=[blk, full, pl.BlockSpec((S, LANES), lambda hp, i: (0, C_SB_V // LANES + hp)), blk] + [hbm] * nx,
        out_specs=[blk, full, full] + [hbm] * nx,
        out_shape=[_sds((S, SB_W), F32)] * 3 + (exchange.out_shape if exchange else []),
        scratch_shapes=exchange.scratch if exchange else [], compiler_params=_cp(2),
    )(qn, kn, p, do, *ex_src)
    return outs[:3], outs[3:]


def _dn_prep(p, conv_w, a_row, dtb_row, tm=256):
    S = p.shape[0]
    W3 = 3 * DN_W
    nhalo = tm // 8

    def body(x_ref, halo_ref, w_ref, ba_ref, a_ref, dtb_ref, qkv_ref, bb_ref, gc_ref, gl_ref):
        i = pl.program_id(0)
        halo = jnp.where(i > 0, halo_ref[...], 0.0)
        xf = jnp.concatenate([halo, x_ref[...]], axis=0)
        acc = jnp.zeros((tm, W3), F32)
        for k in range(CONV_K):
            sh = CONV_K - 1 - k
            xs = xf if sh == 0 else pltpu.roll(xf, sh, 0)
            acc = acc + xs[8:, :] * w_ref[k:k + 1, :]
        s = _silu(acc)
        for gi in range(2 * DN_HEADS):
            sl = slice(gi * LANES, (gi + 1) * LANES)
            sg = s[:, sl]
            rinv = lax.rsqrt(jnp.sum(sg * sg, axis=1, keepdims=True) + EPS)
            qkv_ref[:, sl] = sg * rinv * (DN_HD ** -0.5 if gi < DN_HEADS else 1.0)
        qkv_ref[:, 2 * DN_W:] = s[:, 2 * DN_W:]

        ba = ba_ref[...]
        beta = _sigmoid(ba)
        g = -jnp.exp(a_ref[...]) * _softplus(ba + dtb_ref[...])
        lr, lc = _iota2((LANES, DN_W), 0), _iota2((LANES, DN_W), 1)
        sel_b = (lr == lc // LANES).astype(BF16)
        sel_g = (lr == lc // LANES + DN_HEADS).astype(BF16)
        bb_ref[...] = _mm_xl(beta, sel_b)
        graw = _mm_xl(g, sel_g)
        rr, cc = _iota2((tm, tm), 0), _iota2((tm, tm), 1)
        tri = jnp.logical_and(rr >= cc, rr // CHUNK == cc // CHUNK).astype(BF16)
        gc = _mm_xr(tri, graw)
        last = (cc == (rr // CHUNK) * CHUNK + (CHUNK - 1)).astype(BF16)
        gc_ref[...] = gc
        gl_ref[...] = _mm_xr(last, gc)

    return pl.pallas_call(
        body, name="dn_prep", grid=(S // tm,),
        in_specs=[_rb(tm, W3, 0), pl.BlockSpec((8, W3), lambda i: (jnp.maximum(i * nhalo - 1, 0), 0)),
                  _fs((CONV_K, W3)), _rb(tm, LANES, (p.shape[1] - LANES) // LANES),
                  _fs((1, LANES)), _fs((1, LANES))],
        out_specs=[_rb(tm, W3), _rb(tm, DN_W), _rb(tm, DN_W), _rb(tm, DN_W)],
        out_shape=[_sds((S, W3), F32)] + [_sds((S, DN_W), F32)] * 3, compiler_params=_cp(1),
    )(p, p, conv_w, p, a_row, dtb_row)


def _heads(ref, base=0):
    return jnp.stack([ref[:, base + h * LANES:base + (h + 1) * LANES] for h in range(DN_HEADS)])


def _per_head(const):
    return jnp.broadcast_to(const[None], (DN_HEADS,) + const.shape)


def _dn_chunk_terms(q, k, v, beta, gc, gl):
    r, c = _iota2((CHUNK, CHUNK), 0), _iota2((CHUNK, CHUNK), 1)
    tril, strict = r >= c, r > c
    gcol = _mm_xl(gc, _per_head(jnp.full((LANES, CHUNK), 1.0 / LANES, F32)), _BNN)
    grow = _mm_xr(_per_head(jnp.full((CHUNK, LANES), 1.0 / LANES, F32)), gc, _BNT)
    dec = jnp.where(tril, jnp.exp(jnp.where(tril, gcol - grow, 0.0)), 0.0)
    gam = jnp.exp(gc)
    dlt = jnp.exp(gl - gc)
    kb, vb = k * beta, v * beta
    pm = _mm(kb, k, _BNT)
    qk = _mm(q, k, _BNT)
    m = jnp.where(strict, pm * dec, 0.0)
    a = jnp.where(tril, qk * dec, 0.0)
    return dict(tril=tril, strict=strict, dec=dec, gam=gam, dlt=dlt, kb=kb, vb=vb, m=m, a=a)


def _dn_fwd(qkv, bb, gcb, glb, gather=None, g_src=()):
    S = qkv.shape[0]
    N = S // CHUNK
    nx = len(g_src)

    def body(*refs):
        if gather is None:
            compute(*refs)
            return
        src, gout, sems = refs[4:4 + nx], refs[7 + nx:7 + 2 * nx], refs[8 + 2 * nx:]

        @pl.when(pl.program_id(0) == 0)
        def _():
            gather.start(src, gout, sems)

        compute(*refs[:4], *refs[4 + nx:7 + nx], refs[7 + 2 * nx])

        @pl.when(pl.program_id(0) == N - 1)
        def _():
            gather.wait(src, gout, sems)

    def compute(qkv_ref, bb_ref, gc_ref, gl_ref, o_ref, t_ref, sall_ref, s_scr):
        @pl.when(pl.program_id(0) == 0)
        def _():
            s_scr[...] = jnp.zeros_like(s_scr)

        r, c = _iota2((CHUNK, CHUNK), 0), _iota2((CHUNK, CHUNK), 1)
        eye = (r == c).astype(F32)
        q, k, v = _heads(qkv_ref), _heads(qkv_ref, DN_W), _heads(qkv_ref, 2 * DN_W)
        beta, gc, gl = _heads(bb_ref), _heads(gc_ref), _heads(gl_ref)
        s_prev = s_scr[...]
        sall_ref[0] = s_prev.astype(sall_ref.dtype)
        s0 = s_prev.astype(sall_ref.dtype).astype(F32)
        t = _dn_chunk_terms(q, k, v, beta, gc, gl)
        pw = -t["m"]
        tinv = eye + pw
        for _ in range(5):
            pw = _mm3(pw, pw, _BNN)
            tinv = tinv + _mm3(tinv, pw, _BNN)
        t_ref[...] = tinv
        u = _mm3(tinv, t["vb"], _BNN)
        w = _mm3(tinv, t["kb"] * t["gam"], _BNN)
        vn = u - _mm(w, s0, _BNN)
        o = _mm(q * t["gam"], s0, _BNN) + _mm(t["a"], vn, _BNN)
        for h in range(DN_HEADS):
            o_ref[:, h * LANES:(h + 1) * LANES] = o[h]
        egl = jnp.exp(jnp.concatenate([gl, gl], axis=1))
        s_scr[...] = s_prev * egl + _mm(k * t["dlt"], vn, _BTN)

    hbm = pl.BlockSpec(memory_space=pl.ANY)
    outs = pl.pallas_call(
        body, name="dn_fwd", grid=(N,),
        in_specs=[_rb(CHUNK, 3 * DN_W), _rb(CHUNK, DN_W), _rb(CHUNK, DN_W), _rb(CHUNK, DN_W)] + [hbm] * nx,
        out_specs=[_rb(CHUNK, DN_W), pl.BlockSpec((DN_HEADS, CHUNK, CHUNK), lambda n: (0, n, 0)),
                   pl.BlockSpec((1, DN_HEADS, DN_HD, DN_HD), lambda n: (n, 0, 0, 0))] + [hbm] * nx,
        out_shape=[_sds((S, DN_W), F32), _sds((DN_HEADS, S, CHUNK), F32),
                   _sds((N, DN_HEADS, DN_HD, DN_HD), _MXU_DTYPE)] + (gather.out_shape if gather else []),
        scratch_shapes=[pltpu.VMEM((DN_HEADS, DN_HD, DN_HD), F32)] + (gather.scratch if gather else []),
        compiler_params=_cp(1),
    )(qkv, bb, gcb, glb, *g_src)
    return outs[:3], outs[3:]


def _dn_bwd(qkv, bb, gcb, glb, tinv_all, sall, do):
    S = qkv.shape[0]
    N = S // CHUNK

    def body(qkv_ref, bb_ref, gc_ref, gl_ref, t_ref, sall_ref, do_ref, dqkv_ref, dbb_ref, dg_ref, ds_scr):
        @pl.when(pl.program_id(0) == 0)
        def _():
            ds_scr[...] = jnp.zeros_like(ds_scr)

        r, c = _iota2((CHUNK, CHUNK), 0), _iota2((CHUNK, CHUNK), 1)
        eye = (r == c).astype(F32)
        u_ge = (c >= r).astype(F32)
        last_row = _iota2((CHUNK, LANES), 0) == CHUNK - 1
        eye_h, u_ge_h = _per_head(eye), _per_head(u_ge)
        q, k, v = _heads(qkv_ref), _heads(qkv_ref, DN_W), _heads(qkv_ref, 2 * DN_W)
        beta, gc, gl = _heads(bb_ref), _heads(gc_ref), _heads(gl_ref)
        tinv = t_ref[...]
        s0 = sall_ref[0].astype(F32)
        do = _heads(do_ref)
        ds1 = ds_scr[...]
        t = _dn_chunk_terms(q, k, v, beta, gc, gl)
        gam, dlt, kb, vb, dec = t["gam"], t["dlt"], t["kb"], t["vb"], t["dec"]
        kbg = kb * gam
        u = _mm3(tinv, vb, _BNN)
        w = _mm3(tinv, kbg, _BNN)
        vn = u - _mm(w, s0, _BNN)
        qg, kd = q * gam, k * dlt
        egl = jnp.exp(gl)
        egl2 = jnp.concatenate([egl, egl], axis=1)

        dvn = _mm(t["a"], do, _BTN) + _mm(kd, ds1, _BNN)
        da = jnp.where(t["tril"], _mm(do, vn, _BNT), 0.0)
        dqg = _mm(do, s0, _BNT)
        dkd = _mm(vn, ds1, _BNT)
        dw = -_mm(dvn, s0, _BNT)
        ds_scr[...] = _mm(qg, do, _BTN) + egl2 * ds1 - _mm(w, dvn, _BTN)
        tt = _mm_xr(eye_h, tinv, _BNT)
        dvb = _mm3(tt, dvn, _BNN)
        dkbg = _mm3(tt, dw, _BNN)
        dm = -jnp.where(t["strict"], _mm(dvb, u, _BNT) + _mm(dkbg, w, _BNT), 0.0)
        dpm = dm * dec
        dqk = da * dec
        dkb = dkbg * gam + _mm(dpm, k, _BNN)
        dk = dkd * dlt + _mm(dpm, kb, _BTN) + _mm(dqk, q, _BTN) + dkb * beta
        dq = dqg * gam + _mm(dqk, k, _BNN)
        dv = dvb * beta
        dbeta = jnp.sum(dkb * k, axis=2, keepdims=True) + jnp.sum(dvb * v, axis=2, keepdims=True)
        dgam = jnp.sum(dqg * q, axis=2, keepdims=True) + jnp.sum(dkbg * kb, axis=2, keepdims=True)
        ddlt = jnp.sum(dkd * k, axis=2, keepdims=True)
        xm = dm * t["m"] + da * t["a"]
        xt = _mm_xr(eye_h, xm, _BNT)
        dgc = (dgam * gam - ddlt * dlt + jnp.sum(xm, axis=2, keepdims=True) - jnp.sum(xt, axis=2, keepdims=True))
        dgl = jnp.sum(ddlt * dlt, axis=1, keepdims=True) + jnp.sum(
            jnp.sum(ds1 * s0, axis=2, keepdims=True), axis=1, keepdims=True) * jnp.max(egl, axis=1, keepdims=True)
        dgc = dgc + jnp.where(last_row, dgl, 0.0)
        dg = _mm_xr(u_ge_h, dgc, _BNN)
        for h in range(DN_HEADS):
            sl = slice(h * LANES, (h + 1) * LANES)
            dqkv_ref[:, sl] = dq[h]
            dqkv_ref[:, DN_W + h * LANES:DN_W + (h + 1) * LANES] = dk[h]
            dqkv_ref[:, 2 * DN_W + h * LANES:2 * DN_W + (h + 1) * LANES] = dv[h]
            dbb_ref[:, sl] = jnp.broadcast_to(dbeta[h], (CHUNK, LANES))
            dg_ref[:, sl] = dg[h]

    rev = lambda w: pl.BlockSpec((CHUNK, w), lambda n: (N - 1 - n, 0))
    return pl.pallas_call(
        body, name="dn_bwd", grid=(N,),
        in_specs=[rev(3 * DN_W), rev(DN_W), rev(DN_W), rev(DN_W),
                  pl.BlockSpec((DN_HEADS, CHUNK, CHUNK), lambda n: (0, N - 1 - n, 0)),
                  pl.BlockSpec((1, DN_HEADS, DN_HD, DN_HD), lambda n: (N - 1 - n, 0, 0, 0)), rev(DN_W)],
        out_specs=[rev(3 * DN_W), rev(DN_W), rev(DN_W)],
        out_shape=[_sds((S, 3 * DN_W), F32), _sds((S, DN_W), F32), _sds((S, DN_W), F32)],
        scratch_shapes=[pltpu.VMEM((DN_HEADS, DN_HD, DN_HD), F32)],
        compiler_params=_cp(1),
    )(qkv, bb, gcb, glb, tinv_all, sall, do)


def _dn_prep_bwd_a(p, dqkv, dbb, dgb, conv_w, a_row, dtb_row, tm=256):
    S, PC = p.shape
    W3 = 3 * DN_W
    nhalo = tm // 8

    def body(x_ref, halo_ref, w_ref, ba_ref, a_ref, dtb_ref, dqkv_ref, dbb_ref, dgb_ref,
             dc_ref, dba_ref, dal_ref, ddt_ref):
        i = pl.program_id(0)

        @pl.when(i == 0)
        def _():
            dal_ref[...] = jnp.zeros_like(dal_ref)
            ddt_ref[...] = jnp.zeros_like(ddt_ref)

        halo = jnp.where(i > 0, halo_ref[...], 0.0)
        xf = jnp.concatenate([halo, x_ref[...]], axis=0)
        acc = jnp.zeros((tm, W3), F32)
        for k in range(CONV_K):
            sh = CONV_K - 1 - k
            xs = xf if sh == 0 else pltpu.roll(xf, sh, 0)
            acc = acc + xs[8:, :] * w_ref[k:k + 1, :]
        s = _silu(acc)
        ds_act = _dsilu(acc)
        for gi in range(2 * DN_HEADS):
            sl = slice(gi * LANES, (gi + 1) * LANES)
            sg = s[:, sl]
            rinv = lax.rsqrt(jnp.sum(sg * sg, axis=1, keepdims=True) + EPS)
            nh = sg * rinv
            dn = dqkv_ref[:, sl] * (DN_HD ** -0.5 if gi < DN_HEADS else 1.0)
            dsg = rinv * (dn - nh * jnp.sum(dn * nh, axis=1, keepdims=True))
            dc_ref[:, sl] = dsg * ds_act[:, sl]
        dc_ref[:, 2 * DN_W:] = dqkv_ref[:, 2 * DN_W:] * ds_act[:, 2 * DN_W:]

        ba = ba_ref[...]
        beta = _sigmoid(ba)
        ea = jnp.exp(a_ref[...])
        pre = ba + dtb_ref[...]
        g = -ea * _softplus(pre)
        lr, lc = _iota2((DN_W, LANES), 0), _iota2((DN_W, LANES), 1)
        pick_b = jnp.where(lc == lr // LANES, 1.0 / LANES, 0.0)
        pick_g = jnp.where(lc == lr // LANES + DN_HEADS, 1.0 / LANES, 0.0)
        dbeta = _mm_xl(dbb_ref[...], pick_b)
        dg = _mm_xl(dgb_ref[...], pick_g)
        lane = _iota2((1, LANES), 1)
        da = dg * (-ea) * _sigmoid(pre)
        dba_ref[...] = jnp.where(lane < DN_HEADS, dbeta * beta * (1.0 - beta),
                                 jnp.where(lane < 2 * DN_HEADS, da, 0.0)).astype(dba_ref.dtype)
        dal_ref[...] += jnp.sum(dg * g, axis=0, keepdims=True)
        ddt_ref[...] += jnp.sum(da, axis=0, keepdims=True)

    return pl.pallas_call(
        body, name="dn_prep_bwd_a", grid=(S // tm,),
        in_specs=[_rb(tm, W3, 0), pl.BlockSpec((8, W3), lambda i: (jnp.maximum(i * nhalo - 1, 0), 0)),
                  _fs((CONV_K, W3)), _rb(tm, LANES, (PC - LANES) // LANES), _fs((1, LANES)), _fs((1, LANES)),
                  _rb(tm, W3), _rb(tm, DN_W), _rb(tm, DN_W)],
        out_specs=[_rb(tm, W3), _rb(tm, LANES), _fs((1, LANES)), _fs((1, LANES))],
        out_shape=[_sds((S, W3), F32), _sds((S, LANES), _MXU_DTYPE), _sds((1, LANES), F32), _sds((1, LANES), F32)],
        compiler_params=_cp(1),
    )(p, p, conv_w, p, a_row, dtb_row, dqkv, dbb, dgb)


def _dn_prep_bwd_b(p, dc, conv_w, tm=256):
    S = p.shape[0]
    W3 = 3 * DN_W
    nhalo = tm // 8
    nblk = S // tm

    def body(x_ref, xh_ref, dc_ref, dch_ref, w_ref, dx_ref, dw_ref):
        i = pl.program_id(0)

        @pl.when(i == 0)
        def _():
            dw_ref[...] = jnp.zeros_like(dw_ref)

        dcv = dc_ref[...]
        xf = jnp.concatenate([jnp.where(i > 0, xh_ref[...], 0.0), x_ref[...]], axis=0)
        df = jnp.concatenate([dcv, jnp.where(i < nblk - 1, dch_ref[...], 0.0)], axis=0)
        acc = jnp.zeros((tm, W3), F32)
        for k in range(CONV_K):
            sh = CONV_K - 1 - k
            xs = xf if sh == 0 else pltpu.roll(xf, sh, 0)
            dw_ref[k:k + 1, :] += jnp.sum(dcv * xs[8:, :], axis=0, keepdims=True)
            ds = df if sh == 0 else pltpu.roll(df, tm + 8 - sh, 0)
            acc = acc + ds[:tm, :] * w_ref[k:k + 1, :]
        dx_ref[...] = acc.astype(dx_ref.dtype)

    return pl.pallas_call(
        body, name="dn_prep_bwd_b", grid=(nblk,),
        in_specs=[_rb(tm, W3, 0), pl.BlockSpec((8, W3), lambda i: (jnp.maximum(i * nhalo - 1, 0), 0)),
                  _rb(tm, W3), pl.BlockSpec((8, W3), lambda i: (jnp.minimum((i + 1) * nhalo, S // 8 - 1), 0)),
                  _fs((CONV_K, W3))],
        out_specs=[_rb(tm, W3), _fs((CONV_K, W3))],
        out_shape=[_sds((S, W3), _MXU_DTYPE), _sds((CONV_K, W3), F32)], compiler_params=_cp(1),
    )(p, p, dc, dc, conv_w)


def _gate(o_att, o_dn, p, gn, tm=256):
    S = p.shape[0]

    def body(oa_ref, zs_ref, od_ref, zd_ref, gn_ref, osb_ref, odn_ref):
        osb_ref[...] = (oa_ref[...] * _silu(zs_ref[...])).astype(osb_ref.dtype)
        for h in range(DN_HEADS):
            sl = slice(h * LANES, (h + 1) * LANES)
            o = od_ref[:, sl]
            r = lax.rsqrt(jnp.mean(o * o, axis=1, keepdims=True) + EPS)
            odn_ref[:, sl] = (o * r * gn_ref[...] * _silu(zd_ref[:, sl])).astype(odn_ref.dtype)

    return pl.pallas_call(
        body, name="gate", grid=(S // tm,),
        in_specs=[_rb(tm, SB_W), _rb(tm, SB_W, C_SB_Z // SB_W), _rb(tm, DN_W), _rb(tm, DN_W, C_DN_Z // DN_W),
                  _fs((1, LANES))],
        out_specs=[_rb(tm, SB_W), _rb(tm, DN_W)],
        out_shape=[_sds((S, SB_W), _MXU_DTYPE), _sds((S, DN_W), _MXU_DTYPE)], compiler_params=_cp(1),
    )(o_att, p, o_dn, p, gn)


def _gate_bwd(db_sb, db_dn, wb_sb, wb_dn, o_att, o_dn, p, gn, tm=256):
    S = p.shape[0]
    D = db_sb.shape[1]

    def body(dbs_ref, dbd_ref, ws_ref, wd_ref, oa_ref, zs_ref, od_ref, zd_ref, gn_ref,
             doa_ref, dzs_ref, dod_ref, dzd_ref, dgn_ref):
        @pl.when(pl.program_id(0) == 0)
        def _():
            dgn_ref[...] = jnp.zeros_like(dgn_ref)

        do_sb = _mm(dbs_ref[...], ws_ref[...], _NT)
        zs = zs_ref[...]
        doa_ref[...] = do_sb * _silu(zs)
        dzs_ref[...] = (do_sb * oa_ref[...] * _dsilu(zs)).astype(dzs_ref.dtype)
        do_dnn = _mm(dbd_ref[...], wd_ref[...], _NT)
        gnv = gn_ref[...]
        for h in range(DN_HEADS):
            sl = slice(h * LANES, (h + 1) * LANES)
            o, z, dout = od_ref[:, sl], zd_ref[:, sl], do_dnn[:, sl]
            r = lax.rsqrt(jnp.mean(o * o, axis=1, keepdims=True) + EPS)
            oh = o * r
            sz = _silu(z)
            dzd_ref[:, sl] = (dout * oh * gnv * _dsilu(z)).astype(dzd_ref.dtype)
            dgn_ref[...] += jnp.sum(dout * sz * oh, axis=0, keepdims=True)
            doh = dout * gnv * sz
            dod_ref[:, sl] = r * (doh - oh * jnp.mean(doh * oh, axis=1, keepdims=True))

    return pl.pallas_call(
        body, name="gate_bwd", grid=(S // tm,),
        in_specs=[_rb(tm, D), _rb(tm, D), _fs((SB_W, D)), _fs((DN_W, D)), _rb(tm, SB_W),
                  _rb(tm, SB_W, C_SB_Z // SB_W), _rb(tm, DN_W), _rb(tm, DN_W, C_DN_Z // DN_W), _fs((1, LANES))],
        out_specs=[_rb(tm, SB_W), _rb(tm, SB_W), _rb(tm, DN_W), _rb(tm, DN_W), _fs((1, LANES))],
        out_shape=[_sds((S, SB_W), F32), _sds((S, SB_W), _MXU_DTYPE), _sds((S, DN_W), F32),
                   _sds((S, DN_W), _MXU_DTYPE), _sds((1, LANES), F32)],
        compiler_params=_cp(1),
    )(db_sb, db_dn, wb_sb, wb_dn, o_att, p, o_dn, p, gn)


def _branch(o_sb, o_dnn, wb_sb, wb_dn, p, D, tm=256):
    S = p.shape[0]

    def body(os_ref, od_ref, ws_ref, wd_ref, ms_ref, md_ref, y_ref, bs_ref, bd_ref):
        bs = _mm(os_ref[...], ws_ref[...])
        bdn = _mm(od_ref[...], wd_ref[...])
        bs_ref[...] = bs
        bd_ref[...] = bdn
        y_ref[...] = (_sigmoid(ms_ref[...]) * bs + _sigmoid(md_ref[...]) * bdn).astype(y_ref.dtype)

    return pl.pallas_call(
        body, name="branch", grid=(S // tm,),
        in_specs=[_rb(tm, SB_W), _rb(tm, DN_W), _fs((SB_W, D)), _fs((DN_W, D)),
                  _rb(tm, D, C_MG // D), _rb(tm, D, C_MG // D + 1)],
        out_specs=[_rb(tm, D), _rb(tm, D), _rb(tm, D)],
        out_shape=[_sds((S, D), _MXU_DTYPE), _sds((S, D), F32), _sds((S, D), F32)], compiler_params=_cp(1),
    )(o_sb, o_dnn, wb_sb, wb_dn, p, p)


def _out_proj(x, y, w_out, gate, tm=256):
    S, D = x.shape

    def body(x_ref, y_ref, w_ref, g_ref, xn_ref, out_ref):
        out = _mm(y_ref[...], w_ref[...])
        out_ref[...] = out
        xn_ref[...] = x_ref[...] + g_ref[...] * out

    return pl.pallas_call(
        body, name="out_proj", grid=(S // tm,),
        in_specs=[_rb(tm, D), _rb(tm, D), _fs((D, D)), _fs((1, D))],
        out_specs=[_rb(tm, D), _rb(tm, D)],
        out_shape=[_sds((S, D), F32), _sds((S, D), F32)], compiler_params=_cp(1),
    )(x, y, w_out, gate)


def _out_bwd(dxn, out, gate, w_out, p, b_sb, b_dn, tm=256):
    S, D = dxn.shape

    def body(dxn_ref, out_ref, g_ref, w_ref, ms_ref, md_ref, bs_ref, bd_ref,
             dout_ref, dbs_ref, dbd_ref, dm_ref, dgate_ref):
        @pl.when(pl.program_id(0) == 0)
        def _():
            dgate_ref[...] = jnp.zeros_like(dgate_ref)

        dxv = dxn_ref[...]
        dgate_ref[...] += jnp.sum(dxv * out_ref[...], axis=0, keepdims=True)
        dout = (g_ref[...] * dxv).astype(dout_ref.dtype)
        dout_ref[...] = dout
        dy = _mm(dout, w_ref[...], _NT)
        s1, s2 = _sigmoid(ms_ref[...]), _sigmoid(md_ref[...])
        dbs_ref[...] = (dy * s1).astype(dbs_ref.dtype)
        dbd_ref[...] = (dy * s2).astype(dbd_ref.dtype)
        dm_ref[:, :D] = (dy * bs_ref[...] * s1 * (1.0 - s1)).astype(dm_ref.dtype)
        dm_ref[:, D:] = (dy * bd_ref[...] * s2 * (1.0 - s2)).astype(dm_ref.dtype)

    return pl.pallas_call(
        body, name="out_bwd", grid=(S // tm,),
        in_specs=[_rb(tm, D), _rb(tm, D), _fs((1, D)), _fs((D, D)), _rb(tm, D, C_MG // D),
                  _rb(tm, D, C_MG // D + 1), _rb(tm, D), _rb(tm, D)],
        out_specs=[_rb(tm, D), _rb(tm, D), _rb(tm, D), _rb(tm, 2 * D), _fs((1, D))],
        out_shape=[_sds((S, D), _MXU_DTYPE)] * 3 + [_sds((S, 2 * D), _MXU_DTYPE), _sds((1, D), F32)],
        compiler_params=_cp(1),
    )(dxn, out, gate, w_out, p, p, b_sb, b_dn)


def _loss_head(xf, target, tm=256):
    S, D = xf.shape

    def body(x_ref, t_ref, dy_ref, loss_ref):
        @pl.when(pl.program_id(0) == 0)
        def _():
            loss_ref[...] = jnp.zeros_like(loss_ref)

        e = x_ref[...] - t_ref[...]
        dy_ref[...] = e * (1.0 / D)
        row = jnp.sum(e * e, axis=1, keepdims=True) * (1.0 / D)
        loss_ref[...] += 0.5 * jnp.sum(row, axis=0, keepdims=True)

    return pl.pallas_call(
        body, name="loss_head", grid=(S // tm,),
        in_specs=[_rb(tm, D), _rb(tm, D)], out_specs=[_rb(tm, D), _fs((1, LANES))],
        out_shape=[_sds((S, D), F32), _sds((1, LANES), F32)], compiler_params=_cp(1),
    )(xf, target)


def _ada_fwd(c_all, ada_w, ada_b_sh):
    L, D, n = ada_w.shape
    B = c_all.shape[0]

    def body(c_ref, w_ref, b_ref, o_ref):
        sc = _silu(c_ref[...])
        o_ref[0] = _mm(sc, w_ref[0]) + b_ref[0]

    return pl.pallas_call(
        body, name="ada_fwd", grid=(L,),
        in_specs=[_fs((B, D)), pl.BlockSpec((1, D, n), lambda l: (l, 0, 0)), pl.BlockSpec((1, 1, n), lambda l: (l, 0, 0))],
        out_specs=pl.BlockSpec((1, B, n), lambda l: (l, 0, 0)),
        out_shape=_sds((L, B, n), F32), compiler_params=_cp(1),
    )(c_all, ada_w, ada_b_sh)


def _ada_bwd(c_all_t, dmod_sh):
    D, B = c_all_t.shape
    L, _, n = dmod_sh.shape

    def body(c_ref, d_ref, o_ref):
        acc = jnp.zeros((D, n), F32)
        for b in range(B):
            acc = acc + _silu(c_ref[:, b:b + 1]) * d_ref[0, b:b + 1, :]
        o_ref[0] = acc

    return pl.pallas_call(
        body, name="ada_bwd", grid=(L,),
        in_specs=[_fs((D, B)), pl.BlockSpec((1, B, n), lambda l: (l, 0, 0))],
        out_specs=pl.BlockSpec((1, D, n), lambda l: (l, 0, 0)),
        out_shape=_sds((L, D, n), F32), compiler_params=_cp(1),
    )(c_all_t, dmod_sh)


def _sum_parts(name, parts):
    P, R, C = parts.shape
    tr = _pick(R, max(16, min(512, (1 << 19) // (P * C))), 16) if R % 16 == 0 else R

    def body(p_ref, o_ref):
        acc = p_ref[0].astype(F32)
        for k in range(1, P):
            acc = acc + p_ref[k].astype(F32)
        o_ref[...] = acc

    return pl.pallas_call(
        body, name=name, grid=(R // tr,),
        in_specs=[pl.BlockSpec((P, tr, C), lambda i: (0, i, 0))], out_specs=_rb(tr, C),
        out_shape=_sds((R, C), F32), compiler_params=_cp(1),
    )(parts)


def _adamw(name, w, g, m, v):
    R, C = w.shape
    tr = _pick(R, 256, 8) if R % 8 == 0 else R
    c1 = 1.0 - ADAM_B1 ** ADAM_STEP
    c2 = 1.0 - ADAM_B2 ** ADAM_STEP

    def body(w_ref, g_ref, m_ref, v_ref, d_ref, mo_ref, vo_ref):
        gv = g_ref[...]
        mn = ADAM_B1 * m_ref[...] + (1.0 - ADAM_B1) * gv
        vn = ADAM_B2 * v_ref[...] + (1.0 - ADAM_B2) * (gv * gv)
        mo_ref[...] = mn
        vo_ref[...] = vn
        d_ref[...] = -ADAM_LR * ((mn / c1) / (jnp.sqrt(vn / c2) + ADAM_EPS) + ADAM_WD * w_ref[...])

    spec = _rb(tr, C)
    return pl.pallas_call(
        body, name=name, grid=(R // tr,),
        in_specs=[spec] * 4, out_specs=[spec] * 3, out_shape=[_sds((R, C), F32)] * 3, compiler_params=_cp(1),
    )(w, g, m, v)


def _ag_small(name, blk):
    R, C = blk.shape

    def body(x_ref, out_ref, send_sems, recv_sems, local_sem):
        x, y, c = lax.axis_index("x"), lax.axis_index("y"), lax.axis_index("c")
        me, sibling = (x, y, c), (x, y, 1 - c)
        chips = [(1 - x, y), (x, 1 - y), (1 - x, 1 - y)]

        def rows(px, py, pc):
            return out_ref.at[pl.ds((4 * px + 2 * py + pc) * R, R), :]

        def copy(k, block, to, src=None):
            return pltpu.make_async_remote_copy(
                src_ref=rows(*block) if src is None else src, dst_ref=rows(*block),
                send_sem=send_sems.at[k], recv_sem=recv_sems.at[k], device_id=to, device_id_type=MESH)

        mine = pltpu.make_async_copy(x_ref, rows(*me), local_sem)
        mine.start()
        first = [copy(0, me, sibling, src=x_ref)]
        first += [copy(1 + j, me, (*chip, c), src=x_ref) for j, chip in enumerate(chips)]
        for cp in first:
            cp.start()
        passed = [copy(4 + j, (*chip, c), sibling) for j, chip in enumerate(chips)]
        for j, chip in enumerate(chips):
            copy(1 + j, (*chip, c), me).wait_recv()
            passed[j].start()
        copy(0, sibling, me).wait_recv()
        for j, chip in enumerate(chips):
            copy(4 + j, (*chip, 1 - c), me).wait_recv()
        for cp in first + passed:
            cp.wait_send()
        mine.wait()

    return pl.pallas_call(
        body, name=name, out_shape=_sds((8 * R, C), blk.dtype),
        in_specs=[pl.BlockSpec(memory_space=pltpu.VMEM)], out_specs=pl.BlockSpec(memory_space=pltpu.VMEM),
        scratch_shapes=[pltpu.SemaphoreType.DMA((7,)), pltpu.SemaphoreType.DMA((7,)), pltpu.SemaphoreType.DMA],
    )(blk)


def _row_chunks(ts, row_axis):
    pieces = []
    for t, a in enumerate(ts):
        rows = a.shape[row_axis]
        n = 4 if rows >= 1024 else 1
        pieces += [(t, i * (rows // n), rows // n) for i in range(n)]
    return pieces


def _ag_weights_first(ts):
    nt = len(ts)
    pieces = _row_chunks(ts, 0)
    NP = len(pieces)
    sizes = [nr * ts[t].shape[1] for t, _, nr in pieces]
    split = next(pi for pi in range(NP + 1) if 2 * sum(sizes[:pi]) >= sum(sizes))

    def body(*refs):
        w, out = refs[:nt], refs[nt:2 * nt]
        send_sems, recv_sems, local_sems = refs[2 * nt:]
        x, y, c = lax.axis_index("x"), lax.axis_index("y"), lax.axis_index("c")
        me, sibling = (x, y, c), (x, y, 1 - c)
        mine = 2 * x + y
        chips = [(1 - x, y), (x, 1 - y), (1 - x, 1 - y)]

        def blk(t, shard, r0, nr):
            return out[t].at[shard, r0:r0 + nr, :]

        def copy(k, dst, to, src=None):
            return pltpu.make_async_remote_copy(
                src_ref=dst if src is None else src, dst_ref=dst, send_sem=send_sems.at[k], recv_sem=recv_sems.at[k],
                device_id=to, device_id_type=MESH)

        own = [pltpu.make_async_copy(w[t], out[t].at[mine], local_sems.at[t]) for t in range(nt)]
        for cp in own:
            cp.start()
        for fetcher, lo, hi in ((0, 0, split), (1, split, NP)):
            @pl.when(c == fetcher)
            def _(lo=lo, hi=hi):
                sent = []
                for j, chip in enumerate(chips):
                    for pi in range(lo, hi):
                        t, r0, nr = pieces[pi]
                        sent.append(copy(j * NP + pi, blk(t, mine, r0, nr), (*chip, c), src=w[t].at[r0:r0 + nr, :]))
                        sent[-1].start()
                for j, chip in enumerate(chips):
                    theirs = 2 * chip[0] + chip[1]
                    for pi in range(lo, hi):
                        t, r0, nr = pieces[pi]
                        copy(j * NP + pi, blk(t, theirs, r0, nr), me).wait_recv()
                        sent.append(copy((3 + j) * NP + pi, blk(t, theirs, r0, nr), sibling))
                        sent[-1].start()
                for cp in sent:
                    cp.wait_send()

            @pl.when(c != fetcher)
            def _(lo=lo, hi=hi):
                for j, chip in enumerate(chips):
                    theirs = 2 * chip[0] + chip[1]
                    for pi in range(lo, hi):
                        t, r0, nr = pieces[pi]
                        copy((3 + j) * NP + pi, blk(t, theirs, r0, nr), me).wait_recv()

        for cp in own:
            cp.wait()

    return pl.pallas_call(
        body, name="ag_weights_first", out_shape=[_sds((4,) + a.shape, a.dtype) for a in ts],
        in_specs=[pl.BlockSpec(memory_space=pl.ANY)] * nt, out_specs=[pl.BlockSpec(memory_space=pltpu.VMEM)] * nt,
        scratch_shapes=[pltpu.SemaphoreType.DMA((6 * NP,)), pltpu.SemaphoreType.DMA((6 * NP,)),
                        pltpu.SemaphoreType.DMA((nt,))],
        compiler_params=pltpu.CompilerParams(vmem_limit_bytes=_VMEM_LIMIT),
    )(*ts)


class _WeightGather:
    def __init__(self, ts):
        self.nt = len(ts)
        self.pieces = _row_chunks(ts, 0)
        NP = len(self.pieces)
        self.out_shape = [_sds((4,) + a.shape, a.dtype) for a in ts]
        self.scratch = [pltpu.SemaphoreType.DMA((3 * NP,)), pltpu.SemaphoreType.DMA((3 * NP,)),
                        pltpu.SemaphoreType.DMA((self.nt,))]

    def _copies(self, src, out, sems):
        send_sems, recv_sems, local_sems = sems
        NP = len(self.pieces)
        x, y, c = lax.axis_index("x"), lax.axis_index("y"), lax.axis_index("c")
        mine = 2 * x + y
        own = [pltpu.make_async_copy(src[t], out[t].at[mine], local_sems.at[t]) for t in range(self.nt)]
        sends, recvs = [], []
        for j, chip in enumerate([(1 - x, y), (x, 1 - y), (1 - x, 1 - y)]):
            theirs = 2 * chip[0] + chip[1]
            for pi, (t, r0, nr) in enumerate(self.pieces):
                idx = j * NP + pi
                sends.append(pltpu.make_async_remote_copy(
                    src_ref=src[t].at[r0:r0 + nr, :], dst_ref=out[t].at[mine, r0:r0 + nr, :],
                    send_sem=send_sems.at[idx], recv_sem=recv_sems.at[idx], device_id=(*chip, c), device_id_type=MESH))
                recvs.append(pltpu.make_async_remote_copy(
                    src_ref=out[t].at[theirs, r0:r0 + nr, :], dst_ref=out[t].at[theirs, r0:r0 + nr, :],
                    send_sem=send_sems.at[idx], recv_sem=recv_sems.at[idx], device_id=(x, y, c), device_id_type=MESH))
        return own, sends, recvs

    def start(self, src, out, sems):
        own, sends, _ = self._copies(src, out, sems)
        for cp in own + sends:
            cp.start()

    def wait(self, src, out, sems):
        own, sends, recvs = self._copies(src, out, sems)
        for cp in recvs:
            cp.wait_recv()
        for cp in sends:
            cp.wait_send()
        for cp in own:
            cp.wait()


class _GradExchange:
    def __init__(self, ts, layer):
        self.nt, self.layer = len(ts), layer
        self.pieces = _row_chunks(ts, 1)
        NP = len(self.pieces)
        self.out_shape = [_sds((8,) + a.shape[1:], a.dtype) for a in ts]
        self.scratch = [pltpu.SemaphoreType.DMA((7 * NP,)), pltpu.SemaphoreType.DMA((7 * NP,)),
                        pltpu.SemaphoreType.DMA((self.nt,))]

    def _copies(self, src, out, sems):
        send_sems, recv_sems, local_sems = sems
        NP = len(self.pieces)
        x, y, c = lax.axis_index("x"), lax.axis_index("y"), lax.axis_index("c")
        me = 4 * x + 2 * y + c
        owner = c == self.layer
        own = [pltpu.make_async_copy(src[t].at[2 * x + y], out[t].at[me], local_sems.at[t]) for t in range(self.nt)]
        rel = []
        for k in range(1, 8):
            px = 1 - x if k & 4 else x
            py = 1 - y if k & 2 else y
            source = 4 * px + 2 * py + (1 - c if k & 1 else c)
            sends, recvs = [], []
            for pi, (t, r0, nr) in enumerate(self.pieces):
                idx = (k - 1) * NP + pi
                sends.append(pltpu.make_async_remote_copy(
                    src_ref=src[t].at[2 * px + py, r0:r0 + nr, :], dst_ref=out[t].at[me, r0:r0 + nr, :],
                    send_sem=send_sems.at[idx], recv_sem=recv_sems.at[idx], device_id=(px, py, self.layer),
                    device_id_type=MESH))
                recvs.append(pltpu.make_async_remote_copy(
                    src_ref=out[t].at[source, r0:r0 + nr, :], dst_ref=out[t].at[source, r0:r0 + nr, :],
                    send_sem=send_sems.at[idx], recv_sem=recv_sems.at[idx], device_id=(x, y, c),
                    device_id_type=MESH))
            rel.append((jnp.logical_not(owner) if k & 1 else owner, sends, recvs))
        return owner, own, rel

    def start(self, src, out, sems):
        owner, own, rel = self._copies(src, out, sems)

        @pl.when(owner)
        def _():
            for cp in own:
                cp.start()

        for sending, sends, _ in rel:
            @pl.when(sending)
            def _(sends=sends):
                for cp in sends:
                    cp.start()

    def wait(self, src, out, sems):
        owner, own, rel = self._copies(src, out, sems)

        @pl.when(owner)
        def _():
            for _, _, recvs in rel:
                for cp in recvs:
                    cp.wait_recv()
            for cp in own:
                cp.wait()

        for sending, sends, _ in rel:
            @pl.when(sending)
            def _(sends=sends):
                for cp in sends:
                    cp.wait_send()


def _grad_exchange_alone(ts, layer, prev):
    ex = _GradExchange(ts, layer)
    nt = ex.nt

    def body(*refs):
        src, out, sems = refs[:nt], refs[2 * nt:3 * nt], refs[3 * nt:]
        ex.start(src, out, sems)
        ex.wait(src, out, sems)

    hbm = pl.BlockSpec(memory_space=pl.ANY)
    return pl.pallas_call(
        body, name="grad_exchange", out_shape=ex.out_shape, in_specs=[hbm] * (2 * nt), out_specs=[hbm] * nt,
        input_output_aliases={nt + t: t for t in range(nt)}, scratch_shapes=ex.scratch,
    )(*ts, *prev)


def _sibling_join(ts):
    nt = len(ts)
    pieces = _row_chunks(ts, 0)
    NP = len(pieces)

    def body(*refs):
        src, out = refs[:nt], refs[nt:2 * nt]
        send_sems, recv_sems, local_sems = refs[2 * nt:]
        x, y, c = lax.axis_index("x"), lax.axis_index("y"), lax.axis_index("c")
        own = [pltpu.make_async_copy(src[t], out[t].at[c], local_sems.at[t]) for t in range(nt)]
        for cp in own:
            cp.start()
        sent = []
        for pi, (t, r0, nr) in enumerate(pieces):
            sent.append(pltpu.make_async_remote_copy(
                src_ref=src[t].at[r0:r0 + nr, :], dst_ref=out[t].at[c, r0:r0 + nr, :], send_sem=send_sems.at[pi],
                recv_sem=recv_sems.at[pi], device_id=(x, y, 1 - c), device_id_type=MESH))
            sent[-1].start()
        for pi, (t, r0, nr) in enumerate(pieces):
            pltpu.make_async_remote_copy(
                src_ref=src[t].at[r0:r0 + nr, :], dst_ref=out[t].at[1 - c, r0:r0 + nr, :], send_sem=send_sems.at[pi],
                recv_sem=recv_sems.at[pi], device_id=(x, y, c), device_id_type=MESH).wait_recv()
        for cp in sent:
            cp.wait_send()
        for cp in own:
            cp.wait()

    vmem = pl.BlockSpec(memory_space=pltpu.VMEM)
    return pl.pallas_call(
        body, name="sibling_join", out_shape=[_sds((2,) + a.shape, a.dtype) for a in ts],
        in_specs=[vmem] * nt, out_specs=[vmem] * nt,
        scratch_shapes=[pltpu.SemaphoreType.DMA((NP,)), pltpu.SemaphoreType.DMA((NP,)),
                        pltpu.SemaphoreType.DMA((nt,))],
        compiler_params=pltpu.CompilerParams(vmem_limit_bytes=_VMEM_LIMIT),
    )(*ts)


def _layer_fwd(x, shift, scale, gate, lw, next_shards=None):
    D = x.shape[1]
    h = _norm_mod(x, lw["norm_g"], scale, shift)
    p = _matmul("in_proj", h, lw["w_cat"], "nn", F32, tm_cap=1024, tn_cap=896)
    qn, kn = _sb_prep(p, lw["gq_t"], lw["gk_t"])
    o_att = _sb_fwd(qn, kn, p)
    qkv, bb, gcb, glb = _dn_prep(p, lw["conv_w"], lw["a_row"], lw["dtb_row"])
    if next_shards is None:
        (o_dn, tinv, sall), gathered = _dn_fwd(qkv, bb, gcb, glb)
    else:
        (o_dn, tinv, sall), gathered = _dn_fwd(qkv, bb, gcb, glb, _WeightGather(next_shards), next_shards)
    o_sb, o_dnn = _gate(o_att, o_dn, p, lw["gn"])
    y, b_sb, b_dn = _branch(o_sb, o_dnn, lw["wb_sb"], lw["wb_dn"], p, D)
    x_next, out = _out_proj(x, y, lw["w_out"], gate)
    res = dict(x=x, h=h, p=p, qn=qn, kn=kn, o_att=o_att, qkv=qkv, bb=bb, gcb=gcb, glb=glb, o_dn=o_dn,
               tinv=tinv, sall=sall, o_sb=o_sb, o_dnn=o_dnn, y=y, b_sb=b_sb, b_dn=b_dn, out=out,
               shift=shift, scale=scale, gate=gate)
    return x_next, res, gathered


def _layer_bwd(dxn, res, lw, pending=None):
    p = res["p"]
    dout, db_sb, db_dn, dm, dgate = _out_bwd(dxn, res["out"], res["gate"], lw["w_out"], p, res["b_sb"], res["b_dn"])
    dw_out = _matmul("dw_out", res["y"], dout, "tn", _MXU_DTYPE)
    dwb_sb = _matmul("dwb_sb", res["o_sb"], db_sb, "tn", _MXU_DTYPE)
    dwb_dn = _matmul("dwb_dn", res["o_dnn"], db_dn, "tn", _MXU_DTYPE)
    do_att, dz_sb, do_dn, dz_dn, dgn = _gate_bwd(db_sb, db_dn, lw["wb_sb"], lw["wb_dn"], res["o_att"], res["o_dn"],
                                                  p, lw["gn"])
    if pending is None:
        (dqn, dkn, dv), received = _sb_bwd(res["qn"], res["kn"], p, do_att)
    else:
        (dqn, dkn, dv), received = _sb_bwd(res["qn"], res["kn"], p, do_att,
                                           _GradExchange(pending[1], pending[0]), pending[1])
    dq_sb, dk_sb, dgq, dgk = _sb_prep_bwd(p, dqn, dkn, lw["gq_t"], lw["gk_t"])
    dqkv, dbb, dgb = _dn_bwd(res["qkv"], res["bb"], res["gcb"], res["glb"], res["tinv"], res["sall"], do_dn)
    dc, dp_ba, dal, ddt = _dn_prep_bwd_a(p, dqkv, dbb, dgb, lw["conv_w"], lw["a_row"], lw["dtb_row"])
    dp_dn, dconv = _dn_prep_bwd_b(p, dc, lw["conv_w"])
    dp = jnp.concatenate([dp_dn, dz_dn, dq_sb, dk_sb, dv.astype(_MXU_DTYPE), dz_sb, dm, dp_ba], axis=1)
    dh = _matmul("dh", dp, lw["w_cat"], "nt", F32, tm_cap=1024, tk_cap=896)
    dw_cat = _matmul("dw_cat", res["h"], dp, "tn", _MXU_DTYPE, tm_cap=1024, tn_cap=896, tk_cap=512)
    dx, dshift, dscale, dnorm_g = _norm_mod_bwd(res["x"], dh, dxn, lw["norm_g"], res["scale"])
    small = dict(dmod=jnp.concatenate([dshift, dscale, dgate], axis=1)[0], norm_g=dnorm_g[0],
                 sb_q_g=dgq.reshape(SB_HEADS, SB_HD).sum(0), sb_k_g=dgk.reshape(SB_HEADS, SB_HD).sum(0),
                 conv_w=dconv, dn_a_log=dal[0, DN_HEADS:2 * DN_HEADS], dn_dt_bias=ddt[0, DN_HEADS:2 * DN_HEADS],
                 dn_norm_g=dgn[0])
    D = dxn.shape[1]
    by_shard = lambda g: g.reshape(g.shape[0], 4, g.shape[1] // 4).transpose(1, 0, 2)
    send = [by_shard(_uncat_cols(dw_cat, D)), by_shard(dwb_sb), by_shard(dwb_dn), dw_out.reshape(4, D // 4, D)]
    return dx, small, send, received


def _cat_cols(w, D):
    return jnp.concatenate([w[:, 2048:4096], w[:, 0:2048], w[:, 4104:4104 + 2 * D], w[:, 4096:4104],
                            jnp.zeros((w.shape[0], LANES - 8), w.dtype)], axis=1)


def _uncat_cols(g, D):
    return jnp.concatenate([g[:, 2048:4096], g[:, 0:2048], g[:, 4096 + 2 * D:4096 + 2 * D + 8],
                            g[:, 4096:4096 + 2 * D]], axis=1)


def _flat_pack(arrs, mult):
    flat = jnp.concatenate([a.reshape(-1) for a in arrs])
    n = flat.shape[0]
    pad = (-n) % mult
    if pad:
        flat = jnp.concatenate([flat, jnp.zeros((pad,), flat.dtype)])
    return flat.reshape(-1, LANES)


def _flat_unpack(flat, shapes):
    flat = flat.reshape(-1)
    out, off = [], 0
    for s in shapes:
        n = math.prod(s)
        out.append(flat[off:off + n].reshape(s))
        off += n
    return out


BIG = ("w_in", "w_branch_sb", "w_branch_dn", "w_out")
SMALL = ("ada_b", "norm_g", "sb_q_g", "sb_k_g", "conv_w", "dn_a_log", "dn_dt_bias", "dn_norm_g")


def kernel(x, c, ada_w, ada_b, norm_g, w_in, sb_q_g, sb_k_g, conv_w, dn_a_log, dn_dt_bias, dn_norm_g, w_branch_sb, w_branch_dn, w_out, loss_target, m_ada_w, m_ada_b, m_norm_g, m_w_in, m_sb_q_g, m_sb_k_g, m_conv_w, m_dn_a_log, m_dn_dt_bias, m_dn_norm_g, m_w_branch_sb, m_w_branch_dn, m_w_out, v_ada_w, v_ada_b, v_norm_g, v_w_in, v_sb_q_g, v_sb_k_g, v_conv_w, v_dn_a_log, v_dn_dt_bias, v_dn_norm_g, v_w_branch_sb, v_w_branch_dn, v_w_out):
    W = dict(ada_w=ada_w, ada_b=ada_b, norm_g=norm_g, w_in=w_in, sb_q_g=sb_q_g, sb_k_g=sb_k_g, conv_w=conv_w,
             dn_a_log=dn_a_log, dn_dt_bias=dn_dt_bias, dn_norm_g=dn_norm_g, w_branch_sb=w_branch_sb,
             w_branch_dn=w_branch_dn, w_out=w_out)
    M = dict(ada_w=m_ada_w, ada_b=m_ada_b, norm_g=m_norm_g, w_in=m_w_in, sb_q_g=m_sb_q_g, sb_k_g=m_sb_k_g,
             conv_w=m_conv_w, dn_a_log=m_dn_a_log, dn_dt_bias=m_dn_dt_bias, dn_norm_g=m_dn_norm_g,
             w_branch_sb=m_w_branch_sb, w_branch_dn=m_w_branch_dn, w_out=m_w_out)
    V = dict(ada_w=v_ada_w, ada_b=v_ada_b, norm_g=v_norm_g, w_in=v_w_in, sb_q_g=v_sb_q_g, sb_k_g=v_sb_k_g,
             conv_w=v_conv_w, dn_a_log=v_dn_a_log, dn_dt_bias=v_dn_dt_bias, dn_norm_g=v_dn_norm_g,
             w_branch_sb=v_w_branch_sb, w_branch_dn=v_w_branch_dn, w_out=v_w_out)
    L = ada_w.shape[0]
    S, D = x.shape[1], x.shape[2]
    ix, iy, ic = lax.axis_index("x"), lax.axis_index("y"), lax.axis_index("c")
    shard = 2 * ix + iy
    me = 2 * shard + ic
    n_ada = ada_w.shape[2]
    n_in = w_in.shape[2]
    n_conv = conv_w.shape[2]
    n_br = w_branch_sb.shape[2]
    n_out = w_out.shape[1]

    assert L == 2, "the owner of a layer's gradients is the core with the layer's number"
    shards = [W[n].astype(_MXU_DTYPE) for n in BIG]
    gathered0 = _ag_weights_first([a[0] for a in shards])

    g1 = _ag_small("ag_c_conv", _flat_pack([c, conv_w], LANES * 8))
    g1 = g1.reshape(8, -1)
    c_all = g1[:, :D]
    conv_parts = g1[:, D:D + L * CONV_K * n_conv].reshape(4, 2, L, CONV_K, n_conv)[:, 0]
    conv_full = jnp.concatenate([conv_parts[s] for s in range(4)], axis=2)
    ada_b_sh = lax.dynamic_slice_in_dim(ada_b, shard * n_ada, n_ada, axis=1)[:, None, :]
    mod_sh = _ada_fwd(c_all, ada_w, ada_b_sh)
    g2 = _ag_small("ag_mod", _flat_pack([mod_sh], LANES * 8)).reshape(8, -1)
    mod_parts = g2[:, :L * 8 * n_ada].reshape(4, 2, L, 8, n_ada)[:, 0]
    mod_all = jnp.concatenate([mod_parts[s] for s in range(4)], axis=2)
    mod = lax.dynamic_index_in_dim(mod_all, me, axis=1, keepdims=False)

    def layer_weights(l, gathered):
        g_in, g_bs, g_bd, g_out = gathered
        cat = lambda g, axis: jnp.concatenate([g[s] for s in range(4)], axis=axis)
        pad_lo = jnp.zeros((DN_HEADS,), F32)
        pad_hi = jnp.zeros((LANES - 2 * DN_HEADS,), F32)
        return dict(
            norm_g=norm_g[l][None, :], w_cat=_cat_cols(cat(g_in, 1), D),
            gq_t=jnp.tile(sb_q_g[l], SB_HEADS)[None, :], gk_t=jnp.tile(sb_k_g[l], SB_HEADS)[None, :],
            conv_w=conv_full[l],
            a_row=jnp.concatenate([pad_lo, dn_a_log[l], pad_hi])[None, :],
            dtb_row=jnp.concatenate([pad_lo, dn_dt_bias[l], pad_hi])[None, :],
            gn=dn_norm_g[l][None, :], wb_sb=cat(g_bs, 1), wb_dn=cat(g_bd, 1), w_out=cat(g_out, 0))

    mods = lambda l: (mod[l, None, 0:D], mod[l, None, D:2 * D], mod[l, None, 2 * D:3 * D])
    lws, ress = [None] * L, [None] * L
    lws[0] = layer_weights(0, gathered0)
    xs, ress[0], gathered1 = _layer_fwd(x[0], *mods(0), lws[0], [a[1] for a in shards])
    lws[1] = layer_weights(1, gathered1)
    xs, ress[1], _ = _layer_fwd(xs, *mods(1), lws[1])
    dxs, loss_row = _loss_head(xs, loss_target[0])
    loss = lax.psum(loss_row[0, 0], ("x", "y", "c"))
    smalls = [None] * L
    dxs, smalls[1], send1, _ = _layer_bwd(dxs, ress[1], lws[1])
    dxs, smalls[0], send0, got = _layer_bwd(dxs, ress[0], lws[0], (1, send1))
    grad_x = dxs[None]

    small_names = ("dmod",) + SMALL[1:]
    small_pack = _flat_pack([jnp.stack([smalls[l][n] for l in range(L)]) for n in small_names], LANES * 8)
    g3 = _ag_small("ag_small_grads", small_pack)
    R3 = small_pack.shape[0]
    g3 = g3.reshape(8, R3, LANES)
    small_sum = _sum_parts("sum_small", g3)
    small_shapes = [(L, 3 * D), (L, D), (L, SB_HD), (L, SB_HD), (L, CONV_K, 3 * DN_W), (L, DN_HEADS), (L, DN_HEADS),
                    (L, DN_HD)]
    sg = dict(zip(small_names, _flat_unpack(small_sum, small_shapes)))
    G = dict(ada_b=sg["dmod"], norm_g=sg["norm_g"], sb_q_g=sg["sb_q_g"], sb_k_g=sg["sb_k_g"],
             conv_w=lax.dynamic_slice_in_dim(sg["conv_w"], shard * n_conv, n_conv, axis=2),
             dn_a_log=sg["dn_a_log"], dn_dt_bias=sg["dn_dt_bias"], dn_norm_g=sg["dn_norm_g"])
    dmod_all = g3.reshape(8, -1)[:, :L * 3 * D].reshape(8, L, 3 * D)
    dmod_sh = lax.dynamic_slice_in_dim(dmod_all, shard * n_ada, n_ada, axis=2).transpose(1, 0, 2)
    G["ada_w"] = _ada_bwd(c_all.T, dmod_sh)

    got = _grad_exchange_alone(send0, 0, got)
    mine = [_sum_parts("sum_" + n, g) for n, g in zip(BIG, got)]
    for n, g in zip(BIG, _sibling_join(mine)):
        G[n] = g

    delta, new_m, new_v = {}, {}, {}
    for n in ("ada_w",) + BIG:
        sh = W[n].shape
        two = (sh[0] * sh[1], sh[2])
        d, mo, vo = _adamw("adamw_" + n, W[n].reshape(two), G[n].reshape(two), M[n].reshape(two), V[n].reshape(two))
        delta[n], new_m[n], new_v[n] = d.reshape(sh), mo.reshape(sh), vo.reshape(sh)
    sm_shapes = [W[n].shape for n in SMALL]
    d, mo, vo = _adamw("adamw_small", _flat_pack([W[n] for n in SMALL], LANES * 8),
                       _flat_pack([G[n] for n in SMALL], LANES * 8), _flat_pack([M[n] for n in SMALL], LANES * 8),
                       _flat_pack([V[n] for n in SMALL], LANES * 8))
    for n, dd, mm, vv in zip(SMALL, _flat_unpack(d, sm_shapes), _flat_unpack(mo, sm_shapes),
                             _flat_unpack(vo, sm_shapes)):
        delta[n], new_m[n], new_v[n] = dd, mm, vv

    order = ("ada_w", "ada_b", "norm_g", "w_in", "sb_q_g", "sb_k_g", "conv_w", "dn_a_log", "dn_dt_bias", "dn_norm_g",
             "w_branch_sb", "w_branch_dn", "w_out")
    return (loss, grad_x, *[G[n] for n in order], *[delta[n] for n in order], *[new_m[n] for n in order],
            *[new_v[n] for n in order])
```

```python
import math

import jax
import jax.numpy as jnp
from jax import lax
from jax.experimental import pallas as pl
from jax.experimental.pallas import tpu as pltpu

F32 = jnp.float32
BF16 = jnp.bfloat16
_MXU_DTYPE = BF16
_VMEM_LIMIT = 48 * 1024 * 1024
LANES = 128

EPS = 1e-6
SB_HEADS, SB_HD, SB_W = 8, 64, 512
DN_HEADS, DN_HD, DN_W = 4, 128, 512
CONV_K = 4
CHUNK = 64
QB = 256
_SB_DEAD = 104.0
ADAM_LR, ADAM_B1, ADAM_B2, ADAM_EPS, ADAM_WD, ADAM_STEP = 0.001, 0.9, 0.999, 1e-08, 0.01, 10

C_DN_QKV, C_DN_Z, C_SB_Q, C_SB_K, C_SB_V, C_SB_Z, C_MG = 0, 1536, 2048, 2560, 3072, 3584, 4096

_NN = (((1,), (0,)), ((), ()))
_NT = (((1,), (1,)), ((), ()))
_TN = (((0,), (0,)), ((), ()))
_BNN = (((2,), (1,)), ((0,), (0,)))
_BNT = (((2,), (2,)), ((0,), (0,)))
_BTN = (((1,), (1,)), ((0,), (0,)))
MESH = pl.DeviceIdType.MESH


def _sds(shape, dtype):
    return jax.ShapeDtypeStruct(shape, dtype)


def _cp(n):
    return pltpu.CompilerParams(dimension_semantics=("arbitrary",) * n, vmem_limit_bytes=_VMEM_LIMIT)


def _rb(tm, w, cb=0):
    return pl.BlockSpec((tm, w), lambda i: (i, cb))


def _fs(shape):
    nd = len(shape)
    return pl.BlockSpec(shape, lambda i: (0,) * nd)


def _dg(a, b, dims):
    return lax.dot_general(a, b, dims, preferred_element_type=F32)


def _mm(a, b, dims=_NN):
    return _dg(a.astype(_MXU_DTYPE), b.astype(_MXU_DTYPE), dims)


def _split3(x):
    hi = x.astype(BF16)
    r = x - hi.astype(F32)
    mid = r.astype(BF16)
    lo = (r - mid.astype(F32)).astype(BF16)
    return hi, mid, lo


def _mm_xl(x, const, dims=_NN):
    cb = const.astype(BF16)
    hi, mid, lo = _split3(x)
    return _dg(hi, cb, dims) + _dg(mid, cb, dims) + _dg(lo, cb, dims)


def _mm_xl2(x, const, dims=_NN):
    cb = const.astype(BF16)
    hi = x.astype(BF16)
    lo = (x - hi.astype(F32)).astype(BF16)
    return _dg(hi, cb, dims) + _dg(lo, cb, dims)


def _mm_xr(const, x, dims=_NN):
    cb = const.astype(BF16)
    hi, mid, lo = _split3(x)
    return _dg(cb, hi, dims) + _dg(cb, mid, dims) + _dg(cb, lo, dims)


def _mm3(a, b, dims=_NN):
    ah, am, _ = _split3(a)
    bh, bm, _ = _split3(b)
    return _dg(ah, bh, dims) + (_dg(ah, bm, dims) + _dg(am, bh, dims))


def _sigmoid(z):
    return 1.0 / (1.0 + jnp.exp(-z))


def _silu(z):
    return z * _sigmoid(z)


def _dsilu(z):
    s = _sigmoid(z)
    return s * (1.0 + z * (1.0 - s))


def _softplus(z):
    return jnp.maximum(z, 0.0) + jnp.log(1.0 + jnp.exp(-jnp.abs(z)))


def _iota2(shape, dim):
    return lax.broadcasted_iota(jnp.int32, shape, dim)


def _pick(n, cap, mult):
    best = None
    for t in range(mult, min(n, cap) + 1, mult):
        if n % t == 0:
            best = t
    assert best is not None, (n, cap, mult)
    return best


def _matmul(name, a, b, form, out_dtype, tm_cap=512, tn_cap=1024, tk_cap=1024):
    if form == "nn":
        (M, K), (_, N) = a.shape, b.shape
    elif form == "nt":
        (M, K), (N, _) = a.shape, b.shape
    else:
        (K, M), (_, N) = a.shape, b.shape
    tm = _pick(M, tm_cap, 128 if form == "tn" else 8)
    tn = _pick(N, tn_cap, 128)
    tk = _pick(K, tk_cap, 128)
    nk = K // tk
    dims = {"nn": _NN, "nt": _NT, "tn": _TN}[form]
    if form == "nn":
        a_spec = pl.BlockSpec((tm, tk), lambda i, j, k: (i, k))
        b_spec = pl.BlockSpec((tk, tn), lambda i, j, k: (k, j))
    elif form == "nt":
        a_spec = pl.BlockSpec((tm, tk), lambda i, j, k: (i, k))
        b_spec = pl.BlockSpec((tn, tk), lambda i, j, k: (j, k))
    else:
        a_spec = pl.BlockSpec((tk, tm), lambda i, j, k: (k, i))
        b_spec = pl.BlockSpec((tk, tn), lambda i, j, k: (k, j))

    def body(a_ref, b_ref, o_ref, acc_ref):
        k = pl.program_id(2)

        @pl.when(k == 0)
        def _():
            acc_ref[...] = jnp.zeros_like(acc_ref)

        acc_ref[...] += _mm(a_ref[...], b_ref[...], dims)

        @pl.when(k == nk - 1)
        def _():
            o_ref[...] = acc_ref[...].astype(o_ref.dtype)

    return pl.pallas_call(
        body, name=name, grid=(M // tm, N // tn, nk),
        in_specs=[a_spec, b_spec],
        out_specs=pl.BlockSpec((tm, tn), lambda i, j, k: (i, j)),
        out_shape=_sds((M, N), out_dtype),
        scratch_shapes=[pltpu.VMEM((tm, tn), F32)],
        compiler_params=_cp(3),
    )(a, b)


def _norm_mod(x, g, scale, shift, tm=256):
    S, D = x.shape

    def body(x_ref, g_ref, sc_ref, sh_ref, h_ref):
        xv = x_ref[...]
        r = lax.rsqrt(jnp.mean(xv * xv, axis=1, keepdims=True) + EPS)
        h_ref[...] = ((xv * r * g_ref[...]) * (1.0 + sc_ref[...]) + sh_ref[...]).astype(h_ref.dtype)

    return pl.pallas_call(
        body, name="norm_mod", grid=(S // tm,),
        in_specs=[_rb(tm, D), _fs((1, D)), _fs((1, D)), _fs((1, D))],
        out_specs=_rb(tm, D), out_shape=_sds((S, D), _MXU_DTYPE), compiler_params=_cp(1),
    )(x, g, scale, shift)


def _norm_mod_bwd(x, dh, dxn, g, scale, tm=256):
    S, D = x.shape

    def body(x_ref, dh_ref, dxn_ref, g_ref, sc_ref, dx_ref, dsh_ref, dsc_ref, dg_ref):
        @pl.when(pl.program_id(0) == 0)
        def _():
            dsh_ref[...] = jnp.zeros_like(dsh_ref)
            dsc_ref[...] = jnp.zeros_like(dsc_ref)
            dg_ref[...] = jnp.zeros_like(dg_ref)

        xv, dhv, gv = x_ref[...], dh_ref[...], g_ref[...]
        r = lax.rsqrt(jnp.mean(xv * xv, axis=1, keepdims=True) + EPS)
        xh = xv * r
        one_sc = 1.0 + sc_ref[...]
        dsh_ref[...] += jnp.sum(dhv, axis=0, keepdims=True)
        dsc_ref[...] += jnp.sum(dhv * xh * gv, axis=0, keepdims=True)
        dg_ref[...] += jnp.sum(dhv * one_sc * xh, axis=0, keepdims=True)
        dxh = dhv * (gv * one_sc)
        dx_ref[...] = r * (dxh - xh * jnp.mean(dxh * xh, axis=1, keepdims=True)) + dxn_ref[...]

    return pl.pallas_call(
        body, name="norm_mod_bwd", grid=(S // tm,),
        in_specs=[_rb(tm, D), _rb(tm, D), _rb(tm, D), _fs((1, D)), _fs((1, D))],
        out_specs=[_rb(tm, D), _fs((1, D)), _fs((1, D)), _fs((1, D))],
        out_shape=[_sds((S, D), F32)] + [_sds((1, D), F32)] * 3, compiler_params=_cp(1),
    )(x, dh, dxn, g, scale)


def _head_sum_matrix():
    r = jnp.arange(SB_W)
    return (r[:, None] // SB_HD == r[None, :] // SB_HD).astype(BF16)


def _sb_prep(p, gq_t, gk_t, tm=256):
    S = p.shape[0]
    bd = _head_sum_matrix()

    def body(q_ref, k_ref, gq_ref, gk_ref, bd_ref, qn_ref, kn_ref):
        for src, g_ref, dst in ((q_ref, gq_ref, qn_ref), (k_ref, gk_ref, kn_ref)):
            v = src[...]
            ms = _mm_xl(v * v, bd_ref[...]) * (1.0 / SB_HD)
            dst[...] = (v * lax.rsqrt(ms + EPS) * g_ref[...]).astype(dst.dtype)

    return pl.pallas_call(
        body, name="sb_prep", grid=(S // tm,),
        in_specs=[_rb(tm, SB_W, C_SB_Q // SB_W), _rb(tm, SB_W, C_SB_K // SB_W),
                  _fs((1, SB_W)), _fs((1, SB_W)), _fs((SB_W, SB_W))],
        out_specs=[_rb(tm, SB_W), _rb(tm, SB_W)],
        out_shape=[_sds((S, SB_W), _MXU_DTYPE)] * 2, compiler_params=_cp(1),
    )(p, p, gq_t, gk_t, bd)


def _sb_prep_bwd(p, dqn, dkn, gq_t, gk_t, tm=256):
    S = p.shape[0]
    bd = _head_sum_matrix()

    def body(q_ref, k_ref, dqn_ref, dkn_ref, gq_ref, gk_ref, bd_ref, dq_ref, dk_ref, dgq_ref, dgk_ref):
        @pl.when(pl.program_id(0) == 0)
        def _():
            dgq_ref[...] = jnp.zeros_like(dgq_ref)
            dgk_ref[...] = jnp.zeros_like(dgk_ref)

        for src, dn_ref, g_ref, dst, dg_ref in ((q_ref, dqn_ref, gq_ref, dq_ref, dgq_ref),
                                                (k_ref, dkn_ref, gk_ref, dk_ref, dgk_ref)):
            v, dn = src[...], dn_ref[...]
            r = lax.rsqrt(_mm_xl(v * v, bd_ref[...]) * (1.0 / SB_HD) + EPS)
            vh = v * r
            dg_ref[...] += jnp.sum(dn * vh, axis=0, keepdims=True)
            dvh = dn * g_ref[...]
            m = _mm_xl(dvh * vh, bd_ref[...]) * (1.0 / SB_HD)
            dst[...] = (r * (dvh - vh * m)).astype(dst.dtype)

    return pl.pallas_call(
        body, name="sb_prep_bwd", grid=(S // tm,),
        in_specs=[_rb(tm, SB_W, C_SB_Q // SB_W), _rb(tm, SB_W, C_SB_K // SB_W), _rb(tm, SB_W), _rb(tm, SB_W),
                  _fs((1, SB_W)), _fs((1, SB_W)), _fs((SB_W, SB_W))],
        out_specs=[_rb(tm, SB_W), _rb(tm, SB_W), _fs((1, SB_W)), _fs((1, SB_W))],
        out_shape=[_sds((S, SB_W), _MXU_DTYPE)] * 2 + [_sds((1, SB_W), F32)] * 2, compiler_params=_cp(1),
    )(p, p, dqn, dkn, gq_t, gk_t, bd)


def _sb_consts():
    r, c = _iota2((QB, QB), 0), _iota2((QB, QB), 1)
    lane = _iota2((1, LANES), 1)
    return r, c, lane


def _sb_fwd(qn, kn, p):
    S = qn.shape[0]
    scale = 1.0 / math.sqrt(SB_HD)

    def body(q_ref, k_ref, v_ref, o_ref):
        i = pl.program_id(1)
        r, c, lane = _sb_consts()
        u_gt = (r > c).astype(BF16)
        strict = jnp.concatenate([c < r, c < r], axis=0)
        q = q_ref[...]
        mask0 = (lane // SB_HD) == 0
        zero = jnp.zeros_like(q)
        qh = jnp.concatenate([jnp.where(mask0, q, zero), jnp.where(mask0, zero, q)], axis=0)

        def block(off, carry, diagonal):
            o, run = carry
            kj = k_ref[pl.ds(off, QB), :]
            vj = v_ref[pl.ds(off, QB), :].astype(_MXU_DTYPE)
            z = _mm(qh, kj, _NT) * scale
            sp = _softplus(z)
            sp_m = jnp.where(strict, sp, 0.0) if diagonal else sp
            later = _mm_xl2(sp_m, u_gt)
            w = jnp.exp((z - sp) - later - run)
            if diagonal:
                w = jnp.where(strict, w, 0.0)
            return o + _mm(w, vj), run + jnp.sum(sp_m, axis=1, keepdims=True)

        init = (jnp.zeros((2 * QB, LANES), F32), jnp.zeros((2 * QB, 1), F32))
        carry = block(pl.multiple_of(i * QB, QB), init, True)
        st = lax.while_loop(
            lambda st: jnp.logical_and(st[0] <= i, jnp.min(st[2]) < _SB_DEAD),
            lambda st: (st[0] + 1,) + block(pl.multiple_of((i - st[0]) * QB, QB), st[1:], False),
            (jnp.int32(1),) + carry)
        o_ref[...] = jnp.where(mask0, st[1][:QB], st[1][QB:])

    return pl.pallas_call(
        body, name="sb_fwd", grid=(SB_W // LANES, S // QB),
        in_specs=[pl.BlockSpec((QB, LANES), lambda hp, i: (i, hp)),
                  pl.BlockSpec((S, LANES), lambda hp, i: (0, hp)),
                  pl.BlockSpec((S, LANES), lambda hp, i: (0, C_SB_V // LANES + hp))],
        out_specs=pl.BlockSpec((QB, LANES), lambda hp, i: (i, hp)),
        out_shape=_sds((S, SB_W), F32), compiler_params=_cp(2),
    )(qn, kn, p)


def _sb_bwd(qn, kn, p, do, exchange=None, ex_src=(), early=None, early_src=()):
    S = qn.shape[0]
    scale = 1.0 / math.sqrt(SB_HD)
    grid = (SB_W // LANES, S // QB)
    nx, ne = len(ex_src), len(early_src)

    def body(*refs):
        if exchange is None:
            compute(*refs)
            return
        src, src2 = refs[4:4 + nx], refs[4 + nx:4 + nx + ne]
        o0 = 4 + nx + ne
        xout, sems, sems2 = refs[o0 + 3:o0 + 3 + nx], refs[o0 + 3 + nx:o0 + 6 + nx], refs[o0 + 6 + nx:]
        hp, i = pl.program_id(0), pl.program_id(1)

        @pl.when(jnp.logical_and(hp == 0, i == 0))
        def _():
            exchange.start(src, xout, sems)
            if early is not None:
                early.start(src2, xout[nx - ne:], sems2)

        compute(*refs[:4], *refs[o0:o0 + 3])

        @pl.when(jnp.logical_and(hp == grid[0] - 1, i == grid[1] - 1))
        def _():
            exchange.wait(src, xout, sems)
            if early is not None:
                early.wait(src2, xout[nx - ne:], sems2)

    def compute(q_ref, k_ref, v_ref, do_ref, dq_ref, dk_ref, dv_ref):
        i = pl.program_id(1)

        @pl.when(i == 0)
        def _():
            dk_ref[...] = jnp.zeros_like(dk_ref)
            dv_ref[...] = jnp.zeros_like(dv_ref)

        r, c, lane = _sb_consts()
        u_le = (r <= c).astype(BF16)
        u_lt = (r < c).astype(BF16)
        strict = jnp.concatenate([c < r, c < r], axis=0)
        q = q_ref[...]
        do = do_ref[...].astype(_MXU_DTYPE)
        mask0 = (lane // SB_HD) == 0
        zero, zero_do = jnp.zeros_like(q), jnp.zeros_like(do)
        qh = jnp.concatenate([jnp.where(mask0, q, zero), jnp.where(mask0, zero, q)], axis=0)
        doh = jnp.concatenate([jnp.where(mask0, do, zero_do), jnp.where(mask0, zero_do, do)], axis=0)

        def sums(off, run, diagonal):
            sp = _softplus(_mm(qh, k_ref[pl.ds(off, QB), :], _NT) * scale)
            if diagonal:
                sp = jnp.where(strict, sp, 0.0)
            return run + jnp.sum(sp, axis=1, keepdims=True)

        run = sums(pl.multiple_of(i * QB, QB), jnp.zeros((2 * QB, 1), F32), True)
        nb, tot = lax.while_loop(
            lambda st: jnp.logical_and(st[0] <= i, jnp.min(st[1]) < _SB_DEAD),
            lambda st: (st[0] + 1, sums(pl.multiple_of((i - st[0]) * QB, QB), st[1], False)),
            (jnp.int32(1), run))
        first = i + 1 - nb

        def block(off, carry, diagonal):
            dq, pre_sp, pre_e = carry
            kj = k_ref[pl.ds(off, QB), :]
            vj = v_ref[pl.ds(off, QB), :].astype(_MXU_DTYPE)
            z = _mm(qh, kj, _NT) * scale
            sp = _softplus(z)
            a = z - sp
            sp_m = jnp.where(strict, sp, 0.0) if diagonal else sp
            incl = _mm_xl2(sp_m, u_le)
            w = jnp.exp(a - ((tot - pre_sp) - incl))
            if diagonal:
                w = jnp.where(strict, w, 0.0)
            e = w * _mm(doh, vj, _NT)
            db = pre_e + _mm_xl2(e, u_lt)
            dz = (e - jnp.exp(a) * (e + db)) * scale
            if diagonal:
                dz = jnp.where(strict, dz, 0.0)
            dk_ref[pl.ds(off, QB), :] += _mm(dz, qh, _TN)
            dv_ref[pl.ds(off, QB), :] += _mm(w, doh, _TN)
            return (dq + _mm(dz, kj), pre_sp + jnp.sum(sp_m, axis=1, keepdims=True),
                    pre_e + jnp.sum(e, axis=1, keepdims=True))

        zero_col = jnp.zeros((2 * QB, 1), F32)
        init = (jnp.zeros((2 * QB, LANES), F32), zero_col, zero_col)
        carry = lax.fori_loop(first, i, lambda j, cr: block(pl.multiple_of(j * QB, QB), cr, False), init)
        carry = block(pl.multiple_of(i * QB, QB), carry, True)
        dq_ref[...] = jnp.where(mask0, carry[0][:QB], carry[0][QB:])

    blk = pl.BlockSpec((QB, LANES), lambda hp, i: (i, hp))
    full = pl.BlockSpec((S, LANES), lambda hp, i: (0, hp))
    hbm = pl.BlockSpec(memory_space=pl.ANY)
    outs = pl.pallas_call(
        body, name="sb_bwd", grid=grid,
        in_specs=[blk, full, pl.BlockSpec((S, LANES), lambda hp, i: (0, C_SB_V // LANES + hp)), blk]
        + [hbm] * (nx + ne),
        out_specs=[blk, full, full] + [hbm] * nx,
        out_shape=[_sds((S, SB_W), F32)] * 3 + (exchange.out_shape if exchange else []),
        scratch_shapes=(exchange.scratch if exchange else []) + (early.scratch if early else []),
        compiler_params=_cp(2),
    )(qn, kn, p, do, *ex_src, *early_src)
    return outs[:3], outs[3:]


def _dn_prep(p, conv_w, a_row, dtb_row, tm=256):
    S = p.shape[0]
    W3 = 3 * DN_W
    nhalo = tm // 8

    def body(x_ref, halo_ref, w_ref, ba_ref, a_ref, dtb_ref, qkv_ref, bb_ref, gc_ref, gl_ref):
        i = pl.program_id(0)
        halo = jnp.where(i > 0, halo_ref[...], 0.0)
        xf = jnp.concatenate([halo, x_ref[...]], axis=0)
        acc = jnp.zeros((tm, W3), F32)
        for k in range(CONV_K):
            sh = CONV_K - 1 - k
            xs = xf if sh == 0 else pltpu.roll(xf, sh, 0)
            acc = acc + xs[8:, :] * w_ref[k:k + 1, :]
        s = _silu(acc)
        for gi in range(2 * DN_HEADS):
            sl = slice(gi * LANES, (gi + 1) * LANES)
            sg = s[:, sl]
            rinv = lax.rsqrt(jnp.sum(sg * sg, axis=1, keepdims=True) + EPS)
            qkv_ref[:, sl] = sg * rinv * (DN_HD ** -0.5 if gi < DN_HEADS else 1.0)
        qkv_ref[:, 2 * DN_W:] = s[:, 2 * DN_W:]

        ba = ba_ref[...]
        beta = _sigmoid(ba)
        g = -jnp.exp(a_ref[...]) * _softplus(ba + dtb_ref[...])
        lr, lc = _iota2((LANES, DN_W), 0), _iota2((LANES, DN_W), 1)
        sel_b = (lr == lc // LANES).astype(BF16)
        sel_g = (lr == lc // LANES + DN_HEADS).astype(BF16)
        bb_ref[...] = _mm_xl(beta, sel_b)
        graw = _mm_xl(g, sel_g)
        rr, cc = _iota2((tm, tm), 0), _iota2((tm, tm), 1)
        tri = jnp.logical_and(rr >= cc, rr // CHUNK == cc // CHUNK).astype(BF16)
        gc = _mm_xr(tri, graw)
        last = (cc == (rr // CHUNK) * CHUNK + (CHUNK - 1)).astype(BF16)
        gc_ref[...] = gc
        gl_ref[...] = _mm_xr(last, gc)

    return pl.pallas_call(
        body, name="dn_prep", grid=(S // tm,),
        in_specs=[_rb(tm, W3, 0), pl.BlockSpec((8, W3), lambda i: (jnp.maximum(i * nhalo - 1, 0), 0)),
                  _fs((CONV_K, W3)), _rb(tm, LANES, (p.shape[1] - LANES) // LANES),
                  _fs((1, LANES)), _fs((1, LANES))],
        out_specs=[_rb(tm, W3), _rb(tm, DN_W), _rb(tm, DN_W), _rb(tm, DN_W)],
        out_shape=[_sds((S, W3), F32)] + [_sds((S, DN_W), F32)] * 3, compiler_params=_cp(1),
    )(p, p, conv_w, p, a_row, dtb_row)


def _heads(ref, base=0):
    return jnp.stack([ref[:, base + h * LANES:base + (h + 1) * LANES] for h in range(DN_HEADS)])


def _per_head(const):
    return jnp.broadcast_to(const[None], (DN_HEADS,) + const.shape)


def _dn_chunk_terms(q, k, v, beta, gc, gl):
    r, c = _iota2((CHUNK, CHUNK), 0), _iota2((CHUNK, CHUNK), 1)
    tril, strict = r >= c, r > c
    gcol = _mm_xl(gc, _per_head(jnp.full((LANES, CHUNK), 1.0 / LANES, F32)), _BNN)
    grow = _mm_xr(_per_head(jnp.full((CHUNK, LANES), 1.0 / LANES, F32)), gc, _BNT)
    dec = jnp.where(tril, jnp.exp(jnp.where(tril, gcol - grow, 0.0)), 0.0)
    gam = jnp.exp(gc)
    dlt = jnp.exp(gl - gc)
    kb, vb = k * beta, v * beta
    pm = _mm(kb, k, _BNT)
    qk = _mm(q, k, _BNT)
    m = jnp.where(strict, pm * dec, 0.0)
    a = jnp.where(tril, qk * dec, 0.0)
    return dict(tril=tril, strict=strict, dec=dec, gam=gam, dlt=dlt, kb=kb, vb=vb, m=m, a=a)


def _dn_fwd(qkv, bb, gcb, glb, gather=None, g_src=()):
    S = qkv.shape[0]
    N = S // CHUNK
    nx = len(g_src)

    def body(*refs):
        if gather is None:
            compute(*refs)
            return
        src, gout, sems = refs[4:4 + nx], refs[7 + nx:7 + 2 * nx], refs[8 + 2 * nx:]

        @pl.when(pl.program_id(0) == 0)
        def _():
            gather.start(src, gout, sems)

        compute(*refs[:4], *refs[4 + nx:7 + nx], refs[7 + 2 * nx])

        @pl.when(pl.program_id(0) == N - 1)
        def _():
            gather.wait(src, gout, sems)

    def compute(qkv_ref, bb_ref, gc_ref, gl_ref, o_ref, t_ref, sall_ref, s_scr):
        @pl.when(pl.program_id(0) == 0)
        def _():
            s_scr[...] = jnp.zeros_like(s_scr)

        r, c = _iota2((CHUNK, CHUNK), 0), _iota2((CHUNK, CHUNK), 1)
        eye = (r == c).astype(F32)
        q, k, v = _heads(qkv_ref), _heads(qkv_ref, DN_W), _heads(qkv_ref, 2 * DN_W)
        beta, gc, gl = _heads(bb_ref), _heads(gc_ref), _heads(gl_ref)
        s_prev = s_scr[...]
        sall_ref[0] = s_prev.astype(sall_ref.dtype)
        s0 = s_prev.astype(sall_ref.dtype).astype(F32)
        t = _dn_chunk_terms(q, k, v, beta, gc, gl)
        pw = -t["m"]
        tinv = eye + pw
        for _ in range(5):
            pw = _mm3(pw, pw, _BNN)
            tinv = tinv + _mm3(tinv, pw, _BNN)
        t_ref[...] = tinv
        u = _mm3(tinv, t["vb"], _BNN)
        w = _mm3(tinv, t["kb"] * t["gam"], _BNN)
        vn = u - _mm(w, s0, _BNN)
        o = _mm(q * t["gam"], s0, _BNN) + _mm(t["a"], vn, _BNN)
        for h in range(DN_HEADS):
            o_ref[:, h * LANES:(h + 1) * LANES] = o[h]
        egl = jnp.exp(jnp.concatenate([gl, gl], axis=1))
        s_scr[...] = s_prev * egl + _mm(k * t["dlt"], vn, _BTN)

    hbm = pl.BlockSpec(memory_space=pl.ANY)
    outs = pl.pallas_call(
        body, name="dn_fwd", grid=(N,),
        in_specs=[_rb(CHUNK, 3 * DN_W), _rb(CHUNK, DN_W), _rb(CHUNK, DN_W), _rb(CHUNK, DN_W)] + [hbm] * nx,
        out_specs=[_rb(CHUNK, DN_W), pl.BlockSpec((DN_HEADS, CHUNK, CHUNK), lambda n: (0, n, 0)),
                   pl.BlockSpec((1, DN_HEADS, DN_HD, DN_HD), lambda n: (n, 0, 0, 0))] + [hbm] * nx,
        out_shape=[_sds((S, DN_W), F32), _sds((DN_HEADS, S, CHUNK), F32),
                   _sds((N, DN_HEADS, DN_HD, DN_HD), _MXU_DTYPE)] + (gather.out_shape if gather else []),
        scratch_shapes=[pltpu.VMEM((DN_HEADS, DN_HD, DN_HD), F32)] + (gather.scratch if gather else []),
        compiler_params=_cp(1),
    )(qkv, bb, gcb, glb, *g_src)
    return outs[:3], outs[3:]


def _dn_bwd(qkv, bb, gcb, glb, tinv_all, sall, do):
    S = qkv.shape[0]
    N = S // CHUNK

    def body(qkv_ref, bb_ref, gc_ref, gl_ref, t_ref, sall_ref, do_ref, dqkv_ref, dbb_ref, dg_ref, ds_scr):
        @pl.when(pl.program_id(0) == 0)
        def _():
            ds_scr[...] = jnp.zeros_like(ds_scr)

        r, c = _iota2((CHUNK, CHUNK), 0), _iota2((CHUNK, CHUNK), 1)
        eye = (r == c).astype(F32)
        u_ge = (c >= r).astype(F32)
        last_row = _iota2((CHUNK, LANES), 0) == CHUNK - 1
        eye_h, u_ge_h = _per_head(eye), _per_head(u_ge)
        q, k, v = _heads(qkv_ref), _heads(qkv_ref, DN_W), _heads(qkv_ref, 2 * DN_W)
        beta, gc, gl = _heads(bb_ref), _heads(gc_ref), _heads(gl_ref)
        tinv = t_ref[...]
        s0 = sall_ref[0].astype(F32)
        do = _heads(do_ref)
        ds1 = ds_scr[...]
        t = _dn_chunk_terms(q, k, v, beta, gc, gl)
        gam, dlt, kb, vb, dec = t["gam"], t["dlt"], t["kb"], t["vb"], t["dec"]
        kbg = kb * gam
        u = _mm3(tinv, vb, _BNN)
        w = _mm3(tinv, kbg, _BNN)
        vn = u - _mm(w, s0, _BNN)
        qg, kd = q * gam, k * dlt
        egl = jnp.exp(gl)
        egl2 = jnp.concatenate([egl, egl], axis=1)

        dvn = _mm(t["a"], do, _BTN) + _mm(kd, ds1, _BNN)
        da = jnp.where(t["tril"], _mm(do, vn, _BNT), 0.0)
        dqg = _mm(do, s0, _BNT)
        dkd = _mm(vn, ds1, _BNT)
        dw = -_mm(dvn, s0, _BNT)
        ds_scr[...] = _mm(qg, do, _BTN) + egl2 * ds1 - _mm(w, dvn, _BTN)
        tt = _mm_xr(eye_h, tinv, _BNT)
        dvb = _mm3(tt, dvn, _BNN)
        dkbg = _mm3(tt, dw, _BNN)
        dm = -jnp.where(t["strict"], _mm(dvb, u, _BNT) + _mm(dkbg, w, _BNT), 0.0)
        dpm = dm * dec
        dqk = da * dec
        dkb = dkbg * gam + _mm(dpm, k, _BNN)
        dk = dkd * dlt + _mm(dpm, kb, _BTN) + _mm(dqk, q, _BTN) + dkb * beta
        dq = dqg * gam + _mm(dqk, k, _BNN)
        dv = dvb * beta
        dbeta = jnp.sum(dkb * k, axis=2, keepdims=True) + jnp.sum(dvb * v, axis=2, keepdims=True)
        dgam = jnp.sum(dqg * q, axis=2, keepdims=True) + jnp.sum(dkbg * kb, axis=2, keepdims=True)
        ddlt = jnp.sum(dkd * k, axis=2, keepdims=True)
        xm = dm * t["m"] + da * t["a"]
        xt = _mm_xr(eye_h, xm, _BNT)
        dgc = (dgam * gam - ddlt * dlt + jnp.sum(xm, axis=2, keepdims=True) - jnp.sum(xt, axis=2, keepdims=True))
        dgl = jnp.sum(ddlt * dlt, axis=1, keepdims=True) + jnp.sum(
            jnp.sum(ds1 * s0, axis=2, keepdims=True), axis=1, keepdims=True) * jnp.max(egl, axis=1, keepdims=True)
        dgc = dgc + jnp.where(last_row, dgl, 0.0)
        dg = _mm_xr(u_ge_h, dgc, _BNN)
        for h in range(DN_HEADS):
            sl = slice(h * LANES, (h + 1) * LANES)
            dqkv_ref[:, sl] = dq[h]
            dqkv_ref[:, DN_W + h * LANES:DN_W + (h + 1) * LANES] = dk[h]
            dqkv_ref[:, 2 * DN_W + h * LANES:2 * DN_W + (h + 1) * LANES] = dv[h]
            dbb_ref[:, sl] = jnp.broadcast_to(dbeta[h], (CHUNK, LANES))
            dg_ref[:, sl] = dg[h]

    rev = lambda w: pl.BlockSpec((CHUNK, w), lambda n: (N - 1 - n, 0))
    return pl.pallas_call(
        body, name="dn_bwd", grid=(N,),
        in_specs=[rev(3 * DN_W), rev(DN_W), rev(DN_W), rev(DN_W),
                  pl.BlockSpec((DN_HEADS, CHUNK, CHUNK), lambda n: (0, N - 1 - n, 0)),
                  pl.BlockSpec((1, DN_HEADS, DN_HD, DN_HD), lambda n: (N - 1 - n, 0, 0, 0)), rev(DN_W)],
        out_specs=[rev(3 * DN_W), rev(DN_W), rev(DN_W)],
        out_shape=[_sds((S, 3 * DN_W), F32), _sds((S, DN_W), F32), _sds((S, DN_W), F32)],
        scratch_shapes=[pltpu.VMEM((DN_HEADS, DN_HD, DN_HD), F32)],
        compiler_params=_cp(1),
    )(qkv, bb, gcb, glb, tinv_all, sall, do)


def _dn_prep_bwd_a(p, dqkv, dbb, dgb, conv_w, a_row, dtb_row, tm=256):
    S, PC = p.shape
    W3 = 3 * DN_W
    nhalo = tm // 8

    def body(x_ref, halo_ref, w_ref, ba_ref, a_ref, dtb_ref, dqkv_ref, dbb_ref, dgb_ref,
             dc_ref, dba_ref, dal_ref, ddt_ref):
        i = pl.program_id(0)

        @pl.when(i == 0)
        def _():
            dal_ref[...] = jnp.zeros_like(dal_ref)
            ddt_ref[...] = jnp.zeros_like(ddt_ref)

        halo = jnp.where(i > 0, halo_ref[...], 0.0)
        xf = jnp.concatenate([halo, x_ref[...]], axis=0)
        acc = jnp.zeros((tm, W3), F32)
        for k in range(CONV_K):
            sh = CONV_K - 1 - k
            xs = xf if sh == 0 else pltpu.roll(xf, sh, 0)
            acc = acc + xs[8:, :] * w_ref[k:k + 1, :]
        s = _silu(acc)
        ds_act = _dsilu(acc)
        for gi in range(2 * DN_HEADS):
            sl = slice(gi * LANES, (gi + 1) * LANES)
            sg = s[:, sl]
            rinv = lax.rsqrt(jnp.sum(sg * sg, axis=1, keepdims=True) + EPS)
            nh = sg * rinv
            dn = dqkv_ref[:, sl] * (DN_HD ** -0.5 if gi < DN_HEADS else 1.0)
            dsg = rinv * (dn - nh * jnp.sum(dn * nh, axis=1, keepdims=True))
            dc_ref[:, sl] = dsg * ds_act[:, sl]
        dc_ref[:, 2 * DN_W:] = dqkv_ref[:, 2 * DN_W:] * ds_act[:, 2 * DN_W:]

        ba = ba_ref[...]
        beta = _sigmoid(ba)
        ea = jnp.exp(a_ref[...])
        pre = ba + dtb_ref[...]
        g = -ea * _softplus(pre)
        lr, lc = _iota2((DN_W, LANES), 0), _iota2((DN_W, LANES), 1)
        pick_b = jnp.where(lc == lr // LANES, 1.0 / LANES, 0.0)
        pick_g = jnp.where(lc == lr // LANES + DN_HEADS, 1.0 / LANES, 0.0)
        dbeta = _mm_xl(dbb_ref[...], pick_b)
        dg = _mm_xl(dgb_ref[...], pick_g)
        lane = _iota2((1, LANES), 1)
        da = dg * (-ea) * _sigmoid(pre)
        dba_ref[...] = jnp.where(lane < DN_HEADS, dbeta * beta * (1.0 - beta),
                                 jnp.where(lane < 2 * DN_HEADS, da, 0.0)).astype(dba_ref.dtype)
        dal_ref[...] += jnp.sum(dg * g, axis=0, keepdims=True)
        ddt_ref[...] += jnp.sum(da, axis=0, keepdims=True)

    return pl.pallas_call(
        body, name="dn_prep_bwd_a", grid=(S // tm,),
        in_specs=[_rb(tm, W3, 0), pl.BlockSpec((8, W3), lambda i: (jnp.maximum(i * nhalo - 1, 0), 0)),
                  _fs((CONV_K, W3)), _rb(tm, LANES, (PC - LANES) // LANES), _fs((1, LANES)), _fs((1, LANES)),
                  _rb(tm, W3), _rb(tm, DN_W), _rb(tm, DN_W)],
        out_specs=[_rb(tm, W3), _rb(tm, LANES), _fs((1, LANES)), _fs((1, LANES))],
        out_shape=[_sds((S, W3), F32), _sds((S, LANES), _MXU_DTYPE), _sds((1, LANES), F32), _sds((1, LANES), F32)],
        compiler_params=_cp(1),
    )(p, p, conv_w, p, a_row, dtb_row, dqkv, dbb, dgb)


def _dn_prep_bwd_b(p, dc, conv_w, tm=256):
    S = p.shape[0]
    W3 = 3 * DN_W
    nhalo = tm // 8
    nblk = S // tm

    def body(x_ref, xh_ref, dc_ref, dch_ref, w_ref, dx_ref, dw_ref):
        i = pl.program_id(0)

        @pl.when(i == 0)
        def _():
            dw_ref[...] = jnp.zeros_like(dw_ref)

        dcv = dc_ref[...]
        xf = jnp.concatenate([jnp.where(i > 0, xh_ref[...], 0.0), x_ref[...]], axis=0)
        df = jnp.concatenate([dcv, jnp.where(i < nblk - 1, dch_ref[...], 0.0)], axis=0)
        acc = jnp.zeros((tm, W3), F32)
        for k in range(CONV_K):
            sh = CONV_K - 1 - k
            xs = xf if sh == 0 else pltpu.roll(xf, sh, 0)
            dw_ref[k:k + 1, :] += jnp.sum(dcv * xs[8:, :], axis=0, keepdims=True)
            ds = df if sh == 0 else pltpu.roll(df, tm + 8 - sh, 0)
            acc = acc + ds[:tm, :] * w_ref[k:k + 1, :]
        dx_ref[...] = acc.astype(dx_ref.dtype)

    return pl.pallas_call(
        body, name="dn_prep_bwd_b", grid=(nblk,),
        in_specs=[_rb(tm, W3, 0), pl.BlockSpec((8, W3), lambda i: (jnp.maximum(i * nhalo - 1, 0), 0)),
                  _rb(tm, W3), pl.BlockSpec((8, W3), lambda i: (jnp.minimum((i + 1) * nhalo, S // 8 - 1), 0)),
                  _fs((CONV_K, W3))],
        out_specs=[_rb(tm, W3), _fs((CONV_K, W3))],
        out_shape=[_sds((S, W3), _MXU_DTYPE), _sds((CONV_K, W3), F32)], compiler_params=_cp(1),
    )(p, p, dc, dc, conv_w)


def _gate(o_att, o_dn, p, gn, tm=256):
    S = p.shape[0]

    def body(oa_ref, zs_ref, od_ref, zd_ref, gn_ref, osb_ref, odn_ref):
        osb_ref[...] = (oa_ref[...] * _silu(zs_ref[...])).astype(osb_ref.dtype)
        for h in range(DN_HEADS):
            sl = slice(h * LANES, (h + 1) * LANES)
            o = od_ref[:, sl]
            r = lax.rsqrt(jnp.mean(o * o, axis=1, keepdims=True) + EPS)
            odn_ref[:, sl] = (o * r * gn_ref[...] * _silu(zd_ref[:, sl])).astype(odn_ref.dtype)

    return pl.pallas_call(
        body, name="gate", grid=(S // tm,),
        in_specs=[_rb(tm, SB_W), _rb(tm, SB_W, C_SB_Z // SB_W), _rb(tm, DN_W), _rb(tm, DN_W, C_DN_Z // DN_W),
                  _fs((1, LANES))],
        out_specs=[_rb(tm, SB_W), _rb(tm, DN_W)],
        out_shape=[_sds((S, SB_W), _MXU_DTYPE), _sds((S, DN_W), _MXU_DTYPE)], compiler_params=_cp(1),
    )(o_att, p, o_dn, p, gn)


def _gate_bwd(db_sb, db_dn, wb_sb, wb_dn, o_att, o_dn, p, gn, tm=256):
    S = p.shape[0]
    D = db_sb.shape[1]

    def body(dbs_ref, dbd_ref, ws_ref, wd_ref, oa_ref, zs_ref, od_ref, zd_ref, gn_ref,
             doa_ref, dzs_ref, dod_ref, dzd_ref, dgn_ref):
        @pl.when(pl.program_id(0) == 0)
        def _():
            dgn_ref[...] = jnp.zeros_like(dgn_ref)

        do_sb = _mm(dbs_ref[...], ws_ref[...], _NT)
        zs = zs_ref[...]
        doa_ref[...] = do_sb * _silu(zs)
        dzs_ref[...] = (do_sb * oa_ref[...] * _dsilu(zs)).astype(dzs_ref.dtype)
        do_dnn = _mm(dbd_ref[...], wd_ref[...], _NT)
        gnv = gn_ref[...]
        for h in range(DN_HEADS):
            sl = slice(h * LANES, (h + 1) * LANES)
            o, z, dout = od_ref[:, sl], zd_ref[:, sl], do_dnn[:, sl]
            r = lax.rsqrt(jnp.mean(o * o, axis=1, keepdims=True) + EPS)
            oh = o * r
            sz = _silu(z)
            dzd_ref[:, sl] = (dout * oh * gnv * _dsilu(z)).astype(dzd_ref.dtype)
            dgn_ref[...] += jnp.sum(dout * sz * oh, axis=0, keepdims=True)
            doh = dout * gnv * sz
            dod_ref[:, sl] = r * (doh - oh * jnp.mean(doh * oh, axis=1, keepdims=True))

    return pl.pallas_call(
        body, name="gate_bwd", grid=(S // tm,),
        in_specs=[_rb(tm, D), _rb(tm, D), _fs((SB_W, D)), _fs((DN_W, D)), _rb(tm, SB_W),
                  _rb(tm, SB_W, C_SB_Z // SB_W), _rb(tm, DN_W), _rb(tm, DN_W, C_DN_Z // DN_W), _fs((1, LANES))],
        out_specs=[_rb(tm, SB_W), _rb(tm, SB_W), _rb(tm, DN_W), _rb(tm, DN_W), _fs((1, LANES))],
        out_shape=[_sds((S, SB_W), F32), _sds((S, SB_W), _MXU_DTYPE), _sds((S, DN_W), F32),
                   _sds((S, DN_W), _MXU_DTYPE), _sds((1, LANES), F32)],
        compiler_params=_cp(1),
    )(db_sb, db_dn, wb_sb, wb_dn, o_att, p, o_dn, p, gn)


def _branch(o_sb, o_dnn, wb_sb, wb_dn, p, D, tm=256):
    S = p.shape[0]

    def body(os_ref, od_ref, ws_ref, wd_ref, ms_ref, md_ref, y_ref, bs_ref, bd_ref):
        bs = _mm(os_ref[...], ws_ref[...])
        bdn = _mm(od_ref[...], wd_ref[...])
        bs_ref[...] = bs
        bd_ref[...] = bdn
        y_ref[...] = (_sigmoid(ms_ref[...]) * bs + _sigmoid(md_ref[...]) * bdn).astype(y_ref.dtype)

    return pl.pallas_call(
        body, name="branch", grid=(S // tm,),
        in_specs=[_rb(tm, SB_W), _rb(tm, DN_W), _fs((SB_W, D)), _fs((DN_W, D)),
                  _rb(tm, D, C_MG // D), _rb(tm, D, C_MG // D + 1)],
        out_specs=[_rb(tm, D), _rb(tm, D), _rb(tm, D)],
        out_shape=[_sds((S, D), _MXU_DTYPE), _sds((S, D), F32), _sds((S, D), F32)], compiler_params=_cp(1),
    )(o_sb, o_dnn, wb_sb, wb_dn, p, p)


def _out_proj(x, y, w_out, gate, tm=256):
    S, D = x.shape

    def body(x_ref, y_ref, w_ref, g_ref, xn_ref, out_ref):
        out = _mm(y_ref[...], w_ref[...])
        out_ref[...] = out
        xn_ref[...] = x_ref[...] + g_ref[...] * out

    return pl.pallas_call(
        body, name="out_proj", grid=(S // tm,),
        in_specs=[_rb(tm, D), _rb(tm, D), _fs((D, D)), _fs((1, D))],
        out_specs=[_rb(tm, D), _rb(tm, D)],
        out_shape=[_sds((S, D), F32), _sds((S, D), F32)], compiler_params=_cp(1),
    )(x, y, w_out, gate)


def _out_bwd(dxn, out, gate, w_out, p, b_sb, b_dn, tm=256):
    S, D = dxn.shape

    def body(dxn_ref, out_ref, g_ref, w_ref, ms_ref, md_ref, bs_ref, bd_ref,
             dout_ref, dbs_ref, dbd_ref, dm_ref, dgate_ref):
        @pl.when(pl.program_id(0) == 0)
        def _():
            dgate_ref[...] = jnp.zeros_like(dgate_ref)

        dxv = dxn_ref[...]
        dgate_ref[...] += jnp.sum(dxv * out_ref[...], axis=0, keepdims=True)
        dout = (g_ref[...] * dxv).astype(dout_ref.dtype)
        dout_ref[...] = dout
        dy = _mm(dout, w_ref[...], _NT)
        s1, s2 = _sigmoid(ms_ref[...]), _sigmoid(md_ref[...])
        dbs_ref[...] = (dy * s1).astype(dbs_ref.dtype)
        dbd_ref[...] = (dy * s2).astype(dbd_ref.dtype)
        dm_ref[:, :D] = (dy * bs_ref[...] * s1 * (1.0 - s1)).astype(dm_ref.dtype)
        dm_ref[:, D:] = (dy * bd_ref[...] * s2 * (1.0 - s2)).astype(dm_ref.dtype)

    return pl.pallas_call(
        body, name="out_bwd", grid=(S // tm,),
        in_specs=[_rb(tm, D), _rb(tm, D), _fs((1, D)), _fs((D, D)), _rb(tm, D, C_MG // D),
                  _rb(tm, D, C_MG // D + 1), _rb(tm, D), _rb(tm, D)],
        out_specs=[_rb(tm, D), _rb(tm, D), _rb(tm, D), _rb(tm, 2 * D), _fs((1, D))],
        out_shape=[_sds((S, D), _MXU_DTYPE)] * 3 + [_sds((S, 2 * D), _MXU_DTYPE), _sds((1, D), F32)],
        compiler_params=_cp(1),
    )(dxn, out, gate, w_out, p, p, b_sb, b_dn)


def _loss_head(xf, target, tm=256):
    S, D = xf.shape

    def body(x_ref, t_ref, dy_ref, loss_ref):
        @pl.when(pl.program_id(0) == 0)
        def _():
            loss_ref[...] = jnp.zeros_like(loss_ref)

        e = x_ref[...] - t_ref[...]
        dy_ref[...] = e * (1.0 / D)
        row = jnp.sum(e * e, axis=1, keepdims=True) * (1.0 / D)
        loss_ref[...] += 0.5 * jnp.sum(row, axis=0, keepdims=True)

    return pl.pallas_call(
        body, name="loss_head", grid=(S // tm,),
        in_specs=[_rb(tm, D), _rb(tm, D)], out_specs=[_rb(tm, D), _fs((1, LANES))],
        out_shape=[_sds((S, D), F32), _sds((1, LANES), F32)], compiler_params=_cp(1),
    )(xf, target)


def _ada_fwd(c_all, ada_w, ada_b_sh):
    L, D, n = ada_w.shape
    B = c_all.shape[0]

    def body(c_ref, w_ref, b_ref, o_ref):
        sc = _silu(c_ref[...])
        o_ref[0] = _mm(sc, w_ref[0]) + b_ref[0]

    return pl.pallas_call(
        body, name="ada_fwd", grid=(L,),
        in_specs=[_fs((B, D)), pl.BlockSpec((1, D, n), lambda l: (l, 0, 0)), pl.BlockSpec((1, 1, n), lambda l: (l, 0, 0))],
        out_specs=pl.BlockSpec((1, B, n), lambda l: (l, 0, 0)),
        out_shape=_sds((L, B, n), F32), compiler_params=_cp(1),
    )(c_all, ada_w, ada_b_sh)


def _ada_bwd(c_all_t, dmod_sh):
    D, B = c_all_t.shape
    L, _, n = dmod_sh.shape

    def body(c_ref, d_ref, o_ref):
        acc = jnp.zeros((D, n), F32)
        for b in range(B):
            acc = acc + _silu(c_ref[:, b:b + 1]) * d_ref[0, b:b + 1, :]
        o_ref[0] = acc

    return pl.pallas_call(
        body, name="ada_bwd", grid=(L,),
        in_specs=[_fs((D, B)), pl.BlockSpec((1, B, n), lambda l: (l, 0, 0))],
        out_specs=pl.BlockSpec((1, D, n), lambda l: (l, 0, 0)),
        out_shape=_sds((L, D, n), F32), compiler_params=_cp(1),
    )(c_all_t, dmod_sh)


def _sum_parts(name, parts):
    P, R, C = parts.shape
    tr = _pick(R, max(16, min(512, (1 << 19) // (P * C))), 16) if R % 16 == 0 else R

    def body(p_ref, o_ref):
        acc = p_ref[0].astype(F32)
        for k in range(1, P):
            acc = acc + p_ref[k].astype(F32)
        o_ref[...] = acc

    return pl.pallas_call(
        body, name=name, grid=(R // tr,),
        in_specs=[pl.BlockSpec((P, tr, C), lambda i: (0, i, 0))], out_specs=_rb(tr, C),
        out_shape=_sds((R, C), F32), compiler_params=_cp(1),
    )(parts)


def _adamw(name, w, g, m, v):
    L, R, C = w.shape
    tr = _pick(R, 256, 8) if R % 8 == 0 else R
    c1 = 1.0 - ADAM_B1 ** ADAM_STEP
    c2 = 1.0 - ADAM_B2 ** ADAM_STEP

    def body(w_ref, g_ref, m_ref, v_ref, d_ref, mo_ref, vo_ref):
        gv = g_ref[...]
        mn = ADAM_B1 * m_ref[...] + (1.0 - ADAM_B1) * gv
        vn = ADAM_B2 * v_ref[...] + (1.0 - ADAM_B2) * (gv * gv)
        mo_ref[...] = mn
        vo_ref[...] = vn
        d_ref[...] = -ADAM_LR * ((mn / c1) / (jnp.sqrt(vn / c2) + ADAM_EPS) + ADAM_WD * w_ref[...])

    spec = pl.BlockSpec((1, tr, C), lambda l, i: (l, i, 0))
    return pl.pallas_call(
        body, name=name, grid=(L, R // tr),
        in_specs=[spec] * 4, out_specs=[spec] * 3, out_shape=[_sds((L, R, C), F32)] * 3, compiler_params=_cp(2),
    )(w, g, m, v)


def _ag_small(name, blk):
    R, C = blk.shape

    def body(x_ref, out_ref, send_sems, recv_sems, local_sem):
        x, y, c = lax.axis_index("x"), lax.axis_index("y"), lax.axis_index("c")
        me, sibling = (x, y, c), (x, y, 1 - c)
        chips = [(1 - x, y), (x, 1 - y), (1 - x, 1 - y)]

        def rows(px, py, pc):
            return out_ref.at[pl.ds((4 * px + 2 * py + pc) * R, R), :]

        def copy(k, block, to, src=None):
            return pltpu.make_async_remote_copy(
                src_ref=rows(*block) if src is None else src, dst_ref=rows(*block),
                send_sem=send_sems.at[k], recv_sem=recv_sems.at[k], device_id=to, device_id_type=MESH)

        mine = pltpu.make_async_copy(x_ref, rows(*me), local_sem)
        mine.start()
        first = [copy(0, me, sibling, src=x_ref)]
        first += [copy(1 + j, me, (*chip, c), src=x_ref) for j, chip in enumerate(chips)]
        for cp in first:
            cp.start()
        passed = [copy(4 + j, (*chip, c), sibling) for j, chip in enumerate(chips)]
        for j, chip in enumerate(chips):
            copy(1 + j, (*chip, c), me).wait_recv()
            passed[j].start()
        copy(0, sibling, me).wait_recv()
        for j, chip in enumerate(chips):
            copy(4 + j, (*chip, 1 - c), me).wait_recv()
        for cp in first + passed:
            cp.wait_send()
        mine.wait()

    return pl.pallas_call(
        body, name=name, out_shape=_sds((8 * R, C), blk.dtype),
        in_specs=[pl.BlockSpec(memory_space=pltpu.VMEM)], out_specs=pl.BlockSpec(memory_space=pltpu.VMEM),
        scratch_shapes=[pltpu.SemaphoreType.DMA((7,)), pltpu.SemaphoreType.DMA((7,)), pltpu.SemaphoreType.DMA],
    )(blk)


def _row_chunks(ts, row_axis):
    pieces = []
    for t, a in enumerate(ts):
        rows = a.shape[row_axis]
        n = 4 if rows >= 1024 else 1
        pieces += [(t, i * (rows // n), rows // n) for i in range(n)]
    return pieces


def _ag_weights_first(ts):
    nt = len(ts)
    pieces = _row_chunks(ts, 0)
    NP = len(pieces)
    sizes = [nr * ts[t].shape[1] for t, _, nr in pieces]
    split = next(pi for pi in range(NP + 1) if 2 * sum(sizes[:pi]) >= sum(sizes))

    def body(*refs):
        w, out = refs[:nt], refs[nt:2 * nt]
        send_sems, recv_sems, local_sems = refs[2 * nt:]
        x, y, c = lax.axis_index("x"), lax.axis_index("y"), lax.axis_index("c")
        me, sibling = (x, y, c), (x, y, 1 - c)
        mine = 2 * x + y
        chips = [(1 - x, y), (x, 1 - y), (1 - x, 1 - y)]

        def blk(t, shard, r0, nr):
            return out[t].at[shard, r0:r0 + nr, :]

        def copy(k, dst, to, src=None):
            return pltpu.make_async_remote_copy(
                src_ref=dst if src is None else src, dst_ref=dst, send_sem=send_sems.at[k], recv_sem=recv_sems.at[k],
                device_id=to, device_id_type=MESH)

        own = [pltpu.make_async_copy(w[t], out[t].at[mine], local_sems.at[t]) for t in range(nt)]
        for cp in own:
            cp.start()
        for fetcher, lo, hi in ((0, 0, split), (1, split, NP)):
            @pl.when(c == fetcher)
            def _(lo=lo, hi=hi):
                sent = []
                for j, chip in enumerate(chips):
                    for pi in range(lo, hi):
                        t, r0, nr = pieces[pi]
                        sent.append(copy(j * NP + pi, blk(t, mine, r0, nr), (*chip, c), src=w[t].at[r0:r0 + nr, :]))
                        sent[-1].start()
                for j, chip in enumerate(chips):
                    theirs = 2 * chip[0] + chip[1]
                    for pi in range(lo, hi):
                        t, r0, nr = pieces[pi]
                        copy(j * NP + pi, blk(t, theirs, r0, nr), me).wait_recv()
                        sent.append(copy((3 + j) * NP + pi, blk(t, theirs, r0, nr), sibling))
                        sent[-1].start()
                for cp in sent:
                    cp.wait_send()

            @pl.when(c != fetcher)
            def _(lo=lo, hi=hi):
                for j, chip in enumerate(chips):
                    theirs = 2 * chip[0] + chip[1]
                    for pi in range(lo, hi):
                        t, r0, nr = pieces[pi]
                        copy((3 + j) * NP + pi, blk(t, theirs, r0, nr), me).wait_recv()

        for cp in own:
            cp.wait()

    return pl.pallas_call(
        body, name="ag_weights_first", out_shape=[_sds((4,) + a.shape, a.dtype) for a in ts],
        in_specs=[pl.BlockSpec(memory_space=pl.ANY)] * nt, out_specs=[pl.BlockSpec(memory_space=pltpu.VMEM)] * nt,
        scratch_shapes=[pltpu.SemaphoreType.DMA((6 * NP,)), pltpu.SemaphoreType.DMA((6 * NP,)),
                        pltpu.SemaphoreType.DMA((nt,))],
        compiler_params=pltpu.CompilerParams(vmem_limit_bytes=_VMEM_LIMIT),
    )(*ts)


class _WeightGather:
    def __init__(self, ts):
        self.nt = len(ts)
        self.pieces = _row_chunks(ts, 0)
        NP = len(self.pieces)
        self.out_shape = [_sds((4,) + a.shape, a.dtype) for a in ts]
        self.scratch = [pltpu.SemaphoreType.DMA((3 * NP,)), pltpu.SemaphoreType.DMA((3 * NP,)),
                        pltpu.SemaphoreType.DMA((self.nt,))]

    def _copies(self, src, out, sems):
        send_sems, recv_sems, local_sems = sems
        NP = len(self.pieces)
        x, y, c = lax.axis_index("x"), lax.axis_index("y"), lax.axis_index("c")
        mine = 2 * x + y
        own = [pltpu.make_async_copy(src[t], out[t].at[mine], local_sems.at[t]) for t in range(self.nt)]
        sends, recvs = [], []
        for j, chip in enumerate([(1 - x, y), (x, 1 - y), (1 - x, 1 - y)]):
            theirs = 2 * chip[0] + chip[1]
            for pi, (t, r0, nr) in enumerate(self.pieces):
                idx = j * NP + pi
                sends.append(pltpu.make_async_remote_copy(
                    src_ref=src[t].at[r0:r0 + nr, :], dst_ref=out[t].at[mine, r0:r0 + nr, :],
                    send_sem=send_sems.at[idx], recv_sem=recv_sems.at[idx], device_id=(*chip, c), device_id_type=MESH))
                recvs.append(pltpu.make_async_remote_copy(
                    src_ref=out[t].at[theirs, r0:r0 + nr, :], dst_ref=out[t].at[theirs, r0:r0 + nr, :],
                    send_sem=send_sems.at[idx], recv_sem=recv_sems.at[idx], device_id=(x, y, c), device_id_type=MESH))
        return own, sends, recvs

    def start(self, src, out, sems):
        own, sends, _ = self._copies(src, out, sems)
        for cp in own + sends:
            cp.start()

    def wait(self, src, out, sems):
        own, sends, recvs = self._copies(src, out, sems)
        for cp in recvs:
            cp.wait_recv()
        for cp in sends:
            cp.wait_send()
        for cp in own:
            cp.wait()


class _GradExchange:
    def __init__(self, ts, layer):
        self.nt, self.layer = len(ts), layer
        self.pieces = _row_chunks(ts, 1)
        NP = len(self.pieces)
        self.out_shape = [_sds((8,) + a.shape[1:], a.dtype) for a in ts]
        self.scratch = [pltpu.SemaphoreType.DMA((7 * NP,)), pltpu.SemaphoreType.DMA((7 * NP,)),
                        pltpu.SemaphoreType.DMA((self.nt,))]

    def _copies(self, src, out, sems):
        send_sems, recv_sems, local_sems = sems
        NP = len(self.pieces)
        x, y, c = lax.axis_index("x"), lax.axis_index("y"), lax.axis_index("c")
        me = 4 * x + 2 * y + c
        owner = c == self.layer
        own = [pltpu.make_async_copy(src[t].at[2 * x + y], out[t].at[me], local_sems.at[t]) for t in range(self.nt)]
        rel = []
        for k in range(1, 8):
            px = 1 - x if k & 4 else x
            py = 1 - y if k & 2 else y
            source = 4 * px + 2 * py + (1 - c if k & 1 else c)
            sends, recvs = [], []
            for pi, (t, r0, nr) in enumerate(self.pieces):
                idx = (k - 1) * NP + pi
                sends.append(pltpu.make_async_remote_copy(
                    src_ref=src[t].at[2 * px + py, r0:r0 + nr, :], dst_ref=out[t].at[me, r0:r0 + nr, :],
                    send_sem=send_sems.at[idx], recv_sem=recv_sems.at[idx], device_id=(px, py, self.layer),
                    device_id_type=MESH))
                recvs.append(pltpu.make_async_remote_copy(
                    src_ref=out[t].at[source, r0:r0 + nr, :], dst_ref=out[t].at[source, r0:r0 + nr, :],
                    send_sem=send_sems.at[idx], recv_sem=recv_sems.at[idx], device_id=(x, y, c),
                    device_id_type=MESH))
            rel.append((jnp.logical_not(owner) if k & 1 else owner, sends, recvs))
        return owner, own, rel

    def start(self, src, out, sems):
        owner, own, rel = self._copies(src, out, sems)

        @pl.when(owner)
        def _():
            for cp in own:
                cp.start()

        for sending, sends, _ in rel:
            @pl.when(sending)
            def _(sends=sends):
                for cp in sends:
                    cp.start()

    def wait(self, src, out, sems):
        owner, own, rel = self._copies(src, out, sems)

        @pl.when(owner)
        def _():
            for _, _, recvs in rel:
                for cp in recvs:
                    cp.wait_recv()
            for cp in own:
                cp.wait()

        for sending, sends, _ in rel:
            @pl.when(sending)
            def _(sends=sends):
                for cp in sends:
                    cp.wait_send()


def _grad_exchange_alone(ts, layer, prev):
    ex = _GradExchange(ts, layer)
    nt = ex.nt

    def body(*refs):
        src, out, sems = refs[:nt], refs[2 * nt:3 * nt], refs[3 * nt:]
        ex.start(src, out, sems)
        ex.wait(src, out, sems)

    hbm = pl.BlockSpec(memory_space=pl.ANY)
    return pl.pallas_call(
        body, name="grad_exchange", out_shape=ex.out_shape, in_specs=[hbm] * (2 * nt), out_specs=[hbm] * nt,
        input_output_aliases={nt + t: t for t in range(nt)}, scratch_shapes=ex.scratch,
    )(*ts, *prev)


def _sibling_join(ts):
    nt = len(ts)
    pieces = _row_chunks(ts, 0)
    NP = len(pieces)

    def body(*refs):
        src, out = refs[:nt], refs[nt:2 * nt]
        send_sems, recv_sems, local_sems = refs[2 * nt:]
        x, y, c = lax.axis_index("x"), lax.axis_index("y"), lax.axis_index("c")
        own = [pltpu.make_async_copy(src[t], out[t].at[c], local_sems.at[t]) for t in range(nt)]
        for cp in own:
            cp.start()
        sent = []
        for pi, (t, r0, nr) in enumerate(pieces):
            sent.append(pltpu.make_async_remote_copy(
                src_ref=src[t].at[r0:r0 + nr, :], dst_ref=out[t].at[c, r0:r0 + nr, :], send_sem=send_sems.at[pi],
                recv_sem=recv_sems.at[pi], device_id=(x, y, 1 - c), device_id_type=MESH))
            sent[-1].start()
        for pi, (t, r0, nr) in enumerate(pieces):
            pltpu.make_async_remote_copy(
                src_ref=src[t].at[r0:r0 + nr, :], dst_ref=out[t].at[1 - c, r0:r0 + nr, :], send_sem=send_sems.at[pi],
                recv_sem=recv_sems.at[pi], device_id=(x, y, c), device_id_type=MESH).wait_recv()
        for cp in sent:
            cp.wait_send()
        for cp in own:
            cp.wait()

    vmem = pl.BlockSpec(memory_space=pltpu.VMEM)
    return pl.pallas_call(
        body, name="sibling_join", out_shape=[_sds((2,) + a.shape, a.dtype) for a in ts],
        in_specs=[vmem] * nt, out_specs=[vmem] * nt,
        scratch_shapes=[pltpu.SemaphoreType.DMA((NP,)), pltpu.SemaphoreType.DMA((NP,)),
                        pltpu.SemaphoreType.DMA((nt,))],
        compiler_params=pltpu.CompilerParams(vmem_limit_bytes=_VMEM_LIMIT),
    )(*ts)


def _layer_fwd(x, shift, scale, gate, lw, next_shards=None):
    D = x.shape[1]
    h = _norm_mod(x, lw["norm_g"], scale, shift)
    p = _matmul("in_proj", h, lw["w_cat"], "nn", F32, tm_cap=1024, tn_cap=896)
    qn, kn = _sb_prep(p, lw["gq_t"], lw["gk_t"])
    o_att = _sb_fwd(qn, kn, p)
    qkv, bb, gcb, glb = _dn_prep(p, lw["conv_w"], lw["a_row"], lw["dtb_row"])
    if next_shards is None:
        (o_dn, tinv, sall), gathered = _dn_fwd(qkv, bb, gcb, glb)
    else:
        (o_dn, tinv, sall), gathered = _dn_fwd(qkv, bb, gcb, glb, _WeightGather(next_shards), next_shards)
    o_sb, o_dnn = _gate(o_att, o_dn, p, lw["gn"])
    y, b_sb, b_dn = _branch(o_sb, o_dnn, lw["wb_sb"], lw["wb_dn"], p, D)
    x_next, out = _out_proj(x, y, lw["w_out"], gate)
    res = dict(x=x, h=h, p=p, qn=qn, kn=kn, o_att=o_att, qkv=qkv, bb=bb, gcb=gcb, glb=glb, o_dn=o_dn,
               tinv=tinv, sall=sall, o_sb=o_sb, o_dnn=o_dnn, y=y, b_sb=b_sb, b_dn=b_dn, out=out,
               shift=shift, scale=scale, gate=gate)
    return x_next, res, gathered


def _layer_bwd(dxn, res, lw, pending=None):
    p = res["p"]
    dout, db_sb, db_dn, dm, dgate = _out_bwd(dxn, res["out"], res["gate"], lw["w_out"], p, res["b_sb"], res["b_dn"])
    dw_out = _matmul("dw_out", res["y"], dout, "tn", _MXU_DTYPE)
    dwb_sb = _matmul("dwb_sb", res["o_sb"], db_sb, "tn", _MXU_DTYPE)
    dwb_dn = _matmul("dwb_dn", res["o_dnn"], db_dn, "tn", _MXU_DTYPE)
    do_att, dz_sb, do_dn, dz_dn, dgn = _gate_bwd(db_sb, db_dn, lw["wb_sb"], lw["wb_dn"], res["o_att"], res["o_dn"],
                                                  p, lw["gn"])
    D = dxn.shape[1]
    by_shard = lambda g: g.reshape(g.shape[0], 4, g.shape[1] // 4).transpose(1, 0, 2)
    send = [by_shard(dwb_sb), by_shard(dwb_dn), dw_out.reshape(4, D // 4, D)]
    if pending is None:
        (dqn, dkn, dv), received = _sb_bwd(res["qn"], res["kn"], p, do_att)
    else:
        (dqn, dkn, dv), received = _sb_bwd(res["qn"], res["kn"], p, do_att,
                                           _GradExchange(pending[1], pending[0]), pending[1],
                                           _GradExchange(send, pending[0] - 1), send)
        send = []
    dq_sb, dk_sb, dgq, dgk = _sb_prep_bwd(p, dqn, dkn, lw["gq_t"], lw["gk_t"])
    dqkv, dbb, dgb = _dn_bwd(res["qkv"], res["bb"], res["gcb"], res["glb"], res["tinv"], res["sall"], do_dn)
    dc, dp_ba, dal, ddt = _dn_prep_bwd_a(p, dqkv, dbb, dgb, lw["conv_w"], lw["a_row"], lw["dtb_row"])
    dp_dn, dconv = _dn_prep_bwd_b(p, dc, lw["conv_w"])
    dp = jnp.concatenate([dp_dn, dz_dn, dq_sb, dk_sb, dv.astype(_MXU_DTYPE), dz_sb, dm, dp_ba], axis=1)
    dh = _matmul("dh", dp, lw["w_cat"], "nt", F32, tm_cap=1024, tk_cap=896)
    dw_cat = _matmul("dw_cat", res["h"], dp, "tn", _MXU_DTYPE, tm_cap=1024, tn_cap=896, tk_cap=512)
    dx, dshift, dscale, dnorm_g = _norm_mod_bwd(res["x"], dh, dxn, lw["norm_g"], res["scale"])
    small = dict(dmod=jnp.concatenate([dshift, dscale, dgate], axis=1)[0], norm_g=dnorm_g[0],
                 sb_q_g=dgq.reshape(SB_HEADS, SB_HD).sum(0), sb_k_g=dgk.reshape(SB_HEADS, SB_HD).sum(0),
                 conv_w=dconv, dn_a_log=dal[0, DN_HEADS:2 * DN_HEADS], dn_dt_bias=ddt[0, DN_HEADS:2 * DN_HEADS],
                 dn_norm_g=dgn[0])
    return dx, small, [_shards_from_cat(dw_cat, D)] + send, received


def _cat_cols(w, D):
    return jnp.concatenate([w[:, 2048:4096], w[:, 0:2048], w[:, 4104:4104 + 2 * D], w[:, 4096:4104],
                            jnp.zeros((w.shape[0], LANES - 8), w.dtype)], axis=1)


def _shards_from_cat(g, D):
    n = (4104 + 2 * D) // 4
    segments = ((0, 2048, 2048), (2048, 4096, 0), (4096, 4104, 4096 + 2 * D), (4104, 4104 + 2 * D, 4096))

    def shard(lo, hi):
        cuts = [(c0 + max(lo, s0) - s0, c0 + min(hi, s1) - s0) for s0, s1, c0 in segments if max(lo, s0) < min(hi, s1)]
        return jnp.concatenate([g[:, a:b] for a, b in cuts], axis=1)

    return jnp.stack([shard(s * n, (s + 1) * n) for s in range(4)])


def _flat_pack(arrs, mult):
    flat = jnp.concatenate([a.reshape(-1) for a in arrs])
    n = flat.shape[0]
    pad = (-n) % mult
    if pad:
        flat = jnp.concatenate([flat, jnp.zeros((pad,), flat.dtype)])
    return flat.reshape(-1, LANES)


def _flat_unpack(flat, shapes):
    flat = flat.reshape(-1)
    out, off = [], 0
    for s in shapes:
        n = math.prod(s)
        out.append(flat[off:off + n].reshape(s))
        off += n
    return out


BIG = ("w_in", "w_branch_sb", "w_branch_dn", "w_out")
SMALL = ("ada_b", "norm_g", "sb_q_g", "sb_k_g", "conv_w", "dn_a_log", "dn_dt_bias", "dn_norm_g")


def kernel(x, c, ada_w, ada_b, norm_g, w_in, sb_q_g, sb_k_g, conv_w, dn_a_log, dn_dt_bias, dn_norm_g, w_branch_sb, w_branch_dn, w_out, loss_target, m_ada_w, m_ada_b, m_norm_g, m_w_in, m_sb_q_g, m_sb_k_g, m_conv_w, m_dn_a_log, m_dn_dt_bias, m_dn_norm_g, m_w_branch_sb, m_w_branch_dn, m_w_out, v_ada_w, v_ada_b, v_norm_g, v_w_in, v_sb_q_g, v_sb_k_g, v_conv_w, v_dn_a_log, v_dn_dt_bias, v_dn_norm_g, v_w_branch_sb, v_w_branch_dn, v_w_out):
    W = dict(ada_w=ada_w, ada_b=ada_b, norm_g=norm_g, w_in=w_in, sb_q_g=sb_q_g, sb_k_g=sb_k_g, conv_w=conv_w,
             dn_a_log=dn_a_log, dn_dt_bias=dn_dt_bias, dn_norm_g=dn_norm_g, w_branch_sb=w_branch_sb,
             w_branch_dn=w_branch_dn, w_out=w_out)
    M = dict(ada_w=m_ada_w, ada_b=m_ada_b, norm_g=m_norm_g, w_in=m_w_in, sb_q_g=m_sb_q_g, sb_k_g=m_sb_k_g,
             conv_w=m_conv_w, dn_a_log=m_dn_a_log, dn_dt_bias=m_dn_dt_bias, dn_norm_g=m_dn_norm_g,
             w_branch_sb=m_w_branch_sb, w_branch_dn=m_w_branch_dn, w_out=m_w_out)
    V = dict(ada_w=v_ada_w, ada_b=v_ada_b, norm_g=v_norm_g, w_in=v_w_in, sb_q_g=v_sb_q_g, sb_k_g=v_sb_k_g,
             conv_w=v_conv_w, dn_a_log=v_dn_a_log, dn_dt_bias=v_dn_dt_bias, dn_norm_g=v_dn_norm_g,
             w_branch_sb=v_w_branch_sb, w_branch_dn=v_w_branch_dn, w_out=v_w_out)
    L = ada_w.shape[0]
    S, D = x.shape[1], x.shape[2]
    ix, iy, ic = lax.axis_index("x"), lax.axis_index("y"), lax.axis_index("c")
    shard = 2 * ix + iy
    me = 2 * shard + ic
    n_ada = ada_w.shape[2]
    n_in = w_in.shape[2]
    n_conv = conv_w.shape[2]
    n_br = w_branch_sb.shape[2]
    n_out = w_out.shape[1]

    assert L == 2, "the owner of a layer's gradients is the core with the layer's number"
    shards = [W[n].astype(_MXU_DTYPE) for n in BIG]
    gathered0 = _ag_weights_first([a[0] for a in shards])

    g1 = _ag_small("ag_c_conv", _flat_pack([c, conv_w], LANES * 8))
    g1 = g1.reshape(8, -1)
    c_all = g1[:, :D]
    conv_parts = g1[:, D:D + L * CONV_K * n_conv].reshape(4, 2, L, CONV_K, n_conv)[:, 0]
    conv_full = jnp.concatenate([conv_parts[s] for s in range(4)], axis=2)
    ada_b_sh = lax.dynamic_slice_in_dim(ada_b, shard * n_ada, n_ada, axis=1)[:, None, :]
    mod_sh = _ada_fwd(c_all, ada_w, ada_b_sh)
    g2 = _ag_small("ag_mod", _flat_pack([mod_sh], LANES * 8)).reshape(8, -1)
    mod_parts = g2[:, :L * 8 * n_ada].reshape(4, 2, L, 8, n_ada)[:, 0]
    mod_all = jnp.concatenate([mod_parts[s] for s in range(4)], axis=2)
    mod = lax.dynamic_index_in_dim(mod_all, me, axis=1, keepdims=False)

    def layer_weights(l, gathered):
        g_in, g_bs, g_bd, g_out = gathered
        cat = lambda g, axis: jnp.concatenate([g[s] for s in range(4)], axis=axis)
        pad_lo = jnp.zeros((DN_HEADS,), F32)
        pad_hi = jnp.zeros((LANES - 2 * DN_HEADS,), F32)
        return dict(
            norm_g=norm_g[l][None, :], w_cat=_cat_cols(cat(g_in, 1), D),
            gq_t=jnp.tile(sb_q_g[l], SB_HEADS)[None, :], gk_t=jnp.tile(sb_k_g[l], SB_HEADS)[None, :],
            conv_w=conv_full[l],
            a_row=jnp.concatenate([pad_lo, dn_a_log[l], pad_hi])[None, :],
            dtb_row=jnp.concatenate([pad_lo, dn_dt_bias[l], pad_hi])[None, :],
            gn=dn_norm_g[l][None, :], wb_sb=cat(g_bs, 1), wb_dn=cat(g_bd, 1), w_out=cat(g_out, 0))

    mods = lambda l: (mod[l, None, 0:D], mod[l, None, D:2 * D], mod[l, None, 2 * D:3 * D])
    lws, ress = [None] * L, [None] * L
    lws[0] = layer_weights(0, gathered0)
    xs, ress[0], gathered1 = _layer_fwd(x[0], *mods(0), lws[0], [a[1] for a in shards])
    lws[1] = layer_weights(1, gathered1)
    xs, ress[1], _ = _layer_fwd(xs, *mods(1), lws[1])
    dxs, loss_row = _loss_head(xs, loss_target[0])
    loss = lax.psum(loss_row[0, 0], ("x", "y", "c"))
    smalls = [None] * L
    dxs, smalls[1], send1, _ = _layer_bwd(dxs, ress[1], lws[1])
    dxs, smalls[0], send0, got = _layer_bwd(dxs, ress[0], lws[0], (1, send1))
    grad_x = dxs[None]

    small_names = ("dmod",) + SMALL[1:]
    small_pack = _flat_pack([jnp.stack([smalls[l][n] for l in range(L)]) for n in small_names], LANES * 8)
    g3 = _ag_small("ag_small_grads", small_pack)
    R3 = small_pack.shape[0]
    g3 = g3.reshape(8, R3, LANES)
    small_sum = _sum_parts("sum_small", g3)
    small_shapes = [(L, 3 * D), (L, D), (L, SB_HD), (L, SB_HD), (L, CONV_K, 3 * DN_W), (L, DN_HEADS), (L, DN_HEADS),
                    (L, DN_HD)]
    sg = dict(zip(small_names, _flat_unpack(small_sum, small_shapes)))
    G = dict(ada_b=sg["dmod"], norm_g=sg["norm_g"], sb_q_g=sg["sb_q_g"], sb_k_g=sg["sb_k_g"],
             conv_w=lax.dynamic_slice_in_dim(sg["conv_w"], shard * n_conv, n_conv, axis=2),
             dn_a_log=sg["dn_a_log"], dn_dt_bias=sg["dn_dt_bias"], dn_norm_g=sg["dn_norm_g"])
    dmod_all = g3.reshape(8, -1)[:, :L * 3 * D].reshape(8, L, 3 * D)
    dmod_sh = lax.dynamic_slice_in_dim(dmod_all, shard * n_ada, n_ada, axis=2).transpose(1, 0, 2)
    G["ada_w"] = _ada_bwd(c_all.T, dmod_sh)

    got = list(_grad_exchange_alone(send0, 0, got[:len(send0)])) + list(got[len(send0):])
    mine = [_sum_parts("sum_" + n, g) for n, g in zip(BIG, got)]
    for n, g in zip(BIG, _sibling_join(mine)):
        G[n] = g

    delta, new_m, new_v = {}, {}, {}
    for n in ("ada_w",) + BIG:
        delta[n], new_m[n], new_v[n] = _adamw("adamw_" + n, W[n], G[n], M[n], V[n])
    sm_shapes = [W[n].shape for n in SMALL]
    d, mo, vo = _adamw("adamw_small", *[_flat_pack([T[n] for n in SMALL], LANES * 8)[None] for T in (W, G, M, V)])
    for n, dd, mm, vv in zip(SMALL, _flat_unpack(d, sm_shapes), _flat_unpack(mo, sm_shapes),
                             _flat_unpack(vo, sm_shapes)):
        delta[n], new_m[n], new_v[n] = dd, mm, vv

    order = ("ada_w", "ada_b", "norm_g", "w_in", "sb_q_g", "sb_k_g", "conv_w", "dn_a_log", "dn_dt_bias", "dn_norm_g",
             "w_branch_sb", "w_branch_dn", "w_out")
    return (loss, grad_x, *[G[n] for n in order], *[delta[n] for n in order], *[new_m[n] for n in order],
            *[new_v[n] for n in order])
```

```python
import math

import jax
import jax.numpy as jnp
from jax import lax
from jax.experimental import pallas as pl
from jax.experimental.pallas import tpu as pltpu

F32 = jnp.float32
BF16 = jnp.bfloat16
_MXU_DTYPE = BF16
_VMEM_LIMIT = 48 * 1024 * 1024
LANES = 128

EPS = 1e-6
SB_HEADS, SB_HD, SB_W = 8, 64, 512
DN_HEADS, DN_HD, DN_W = 4, 128, 512
CONV_K = 4
CHUNK = 64
QB = 256
_SB_DEAD = 104.0
ADAM_LR, ADAM_B1, ADAM_B2, ADAM_EPS, ADAM_WD, ADAM_STEP = 0.001, 0.9, 0.999, 1e-08, 0.01, 10

C_DN_QKV, C_DN_Z, C_SB_Q, C_SB_K, C_SB_V, C_SB_Z, C_MG = 0, 1536, 2048, 2560, 3072, 3584, 4096

_NN = (((1,), (0,)), ((), ()))
_NT = (((1,), (1,)), ((), ()))
_TN = (((0,), (0,)), ((), ()))
_BNN = (((2,), (1,)), ((0,), (0,)))
_BNT = (((2,), (2,)), ((0,), (0,)))
_BTN = (((1,), (1,)), ((0,), (0,)))
MESH = pl.DeviceIdType.MESH


def _sds(shape, dtype):
    return jax.ShapeDtypeStruct(shape, dtype)


def _cp(n):
    return pltpu.CompilerParams(dimension_semantics=("arbitrary",) * n, vmem_limit_bytes=_VMEM_LIMIT)


def _rb(tm, w, cb=0):
    return pl.BlockSpec((tm, w), lambda i: (i, cb))


def _fs(shape):
    nd = len(shape)
    return pl.BlockSpec(shape, lambda i: (0,) * nd)


def _dg(a, b, dims):
    return lax.dot_general(a, b, dims, preferred_element_type=F32)


def _mm(a, b, dims=_NN):
    return _dg(a.astype(_MXU_DTYPE), b.astype(_MXU_DTYPE), dims)


def _split3(x):
    hi = x.astype(BF16)
    r = x - hi.astype(F32)
    mid = r.astype(BF16)
    lo = (r - mid.astype(F32)).astype(BF16)
    return hi, mid, lo


def _mm_xl(x, const, dims=_NN):
    cb = const.astype(BF16)
    hi, mid, lo = _split3(x)
    return _dg(hi, cb, dims) + _dg(mid, cb, dims) + _dg(lo, cb, dims)


def _mm_xl2(x, const, dims=_NN):
    cb = const.astype(BF16)
    hi = x.astype(BF16)
    lo = (x - hi.astype(F32)).astype(BF16)
    return _dg(hi, cb, dims) + _dg(lo, cb, dims)


def _mm_xr(const, x, dims=_NN):
    cb = const.astype(BF16)
    hi, mid, lo = _split3(x)
    return _dg(cb, hi, dims) + _dg(cb, mid, dims) + _dg(cb, lo, dims)


def _mm3(a, b, dims=_NN):
    ah, am, _ = _split3(a)
    bh, bm, _ = _split3(b)
    return _dg(ah, bh, dims) + (_dg(ah, bm, dims) + _dg(am, bh, dims))


def _sigmoid(z):
    return 1.0 / (1.0 + jnp.exp(-z))


def _silu(z):
    return z * _sigmoid(z)


def _dsilu(z):
    s = _sigmoid(z)
    return s * (1.0 + z * (1.0 - s))


def _softplus(z):
    return jnp.maximum(z, 0.0) + jnp.log(1.0 + jnp.exp(-jnp.abs(z)))


def _iota2(shape, dim):
    return lax.broadcasted_iota(jnp.int32, shape, dim)


def _pick(n, cap, mult):
    best = None
    for t in range(mult, min(n, cap) + 1, mult):
        if n % t == 0:
            best = t
    assert best is not None, (n, cap, mult)
    return best


def _matmul(name, a, b, form, out_dtype, tm_cap=512, tn_cap=1024, tk_cap=1024, exchange=None, ex_src=(), ex_prev=()):
    if form == "nn":
        (M, K), (_, N) = a.shape, b.shape
    elif form == "nt":
        (M, K), (N, _) = a.shape, b.shape
    else:
        (K, M), (_, N) = a.shape, b.shape
    tm = _pick(M, tm_cap, 128 if form == "tn" else 8)
    tn = _pick(N, tn_cap, 128)
    tk = _pick(K, tk_cap, 128)
    nk = K // tk
    dims = {"nn": _NN, "nt": _NT, "tn": _TN}[form]
    if form == "nn":
        a_spec = pl.BlockSpec((tm, tk), lambda i, j, k: (i, k))
        b_spec = pl.BlockSpec((tk, tn), lambda i, j, k: (k, j))
    elif form == "nt":
        a_spec = pl.BlockSpec((tm, tk), lambda i, j, k: (i, k))
        b_spec = pl.BlockSpec((tn, tk), lambda i, j, k: (j, k))
    else:
        a_spec = pl.BlockSpec((tk, tm), lambda i, j, k: (k, i))
        b_spec = pl.BlockSpec((tk, tn), lambda i, j, k: (k, j))

    grid = (M // tm, N // tn, nk)
    nx = len(ex_src)

    def body(*refs):
        if exchange is None:
            compute(*refs)
            return
        src, xout, sems = refs[2:2 + nx], refs[3 + 2 * nx:3 + 3 * nx], refs[4 + 3 * nx:]
        at = [pl.program_id(d) for d in range(3)]

        @pl.when(jnp.logical_and(jnp.logical_and(at[0] == 0, at[1] == 0), at[2] == 0))
        def _():
            exchange.start(src, xout, sems)

        compute(refs[0], refs[1], refs[2 + 2 * nx], refs[3 + 3 * nx])

        @pl.when(jnp.logical_and(jnp.logical_and(at[0] == grid[0] - 1, at[1] == grid[1] - 1), at[2] == nk - 1))
        def _():
            exchange.wait(src, xout, sems)

    def compute(a_ref, b_ref, o_ref, acc_ref):
        k = pl.program_id(2)

        @pl.when(k == 0)
        def _():
            acc_ref[...] = jnp.zeros_like(acc_ref)

        acc_ref[...] += _mm(a_ref[...], b_ref[...], dims)

        @pl.when(k == nk - 1)
        def _():
            o_ref[...] = acc_ref[...].astype(o_ref.dtype)

    hbm = pl.BlockSpec(memory_space=pl.ANY)
    outs = pl.pallas_call(
        body, name=name, grid=grid,
        in_specs=[a_spec, b_spec] + [hbm] * (2 * nx),
        out_specs=[pl.BlockSpec((tm, tn), lambda i, j, k: (i, j))] + [hbm] * nx,
        out_shape=[_sds((M, N), out_dtype)] + (exchange.out_shape if exchange else []),
        input_output_aliases={2 + nx + t: 1 + t for t in range(nx)},
        scratch_shapes=[pltpu.VMEM((tm, tn), F32)] + (exchange.scratch if exchange else []),
        compiler_params=_cp(3),
    )(a, b, *ex_src, *ex_prev)
    return outs[0] if exchange is None else (outs[0], list(outs[1:]))


def _norm_mod(x, g, scale, shift, tm=256):
    S, D = x.shape

    def body(x_ref, g_ref, sc_ref, sh_ref, h_ref):
        xv = x_ref[...]
        r = lax.rsqrt(jnp.mean(xv * xv, axis=1, keepdims=True) + EPS)
        h_ref[...] = ((xv * r * g_ref[...]) * (1.0 + sc_ref[...]) + sh_ref[...]).astype(h_ref.dtype)

    return pl.pallas_call(
        body, name="norm_mod", grid=(S // tm,),
        in_specs=[_rb(tm, D), _fs((1, D)), _fs((1, D)), _fs((1, D))],
        out_specs=_rb(tm, D), out_shape=_sds((S, D), _MXU_DTYPE), compiler_params=_cp(1),
    )(x, g, scale, shift)


def _norm_mod_bwd(x, dh, dxn, g, scale, tm=256):
    S, D = x.shape

    def body(x_ref, dh_ref, dxn_ref, g_ref, sc_ref, dx_ref, dsh_ref, dsc_ref, dg_ref):
        @pl.when(pl.program_id(0) == 0)
        def _():
            dsh_ref[...] = jnp.zeros_like(dsh_ref)
            dsc_ref[...] = jnp.zeros_like(dsc_ref)
            dg_ref[...] = jnp.zeros_like(dg_ref)

        xv, dhv, gv = x_ref[...], dh_ref[...], g_ref[...]
        r = lax.rsqrt(jnp.mean(xv * xv, axis=1, keepdims=True) + EPS)
        xh = xv * r
        one_sc = 1.0 + sc_ref[...]
        dsh_ref[...] += jnp.sum(dhv, axis=0, keepdims=True)
        dsc_ref[...] += jnp.sum(dhv * xh * gv, axis=0, keepdims=True)
        dg_ref[...] += jnp.sum(dhv * one_sc * xh, axis=0, keepdims=True)
        dxh = dhv * (gv * one_sc)
        dx_ref[...] = r * (dxh - xh * jnp.mean(dxh * xh, axis=1, keepdims=True)) + dxn_ref[...]

    return pl.pallas_call(
        body, name="norm_mod_bwd", grid=(S // tm,),
        in_specs=[_rb(tm, D), _rb(tm, D), _rb(tm, D), _fs((1, D)), _fs((1, D))],
        out_specs=[_rb(tm, D), _fs((1, D)), _fs((1, D)), _fs((1, D))],
        out_shape=[_sds((S, D), F32)] + [_sds((1, D), F32)] * 3, compiler_params=_cp(1),
    )(x, dh, dxn, g, scale)


def _head_sum_matrix():
    r = jnp.arange(SB_W)
    return (r[:, None] // SB_HD == r[None, :] // SB_HD).astype(BF16)


def _sb_prep(p, gq_t, gk_t, tm=256):
    S = p.shape[0]
    bd = _head_sum_matrix()

    def body(q_ref, k_ref, gq_ref, gk_ref, bd_ref, qn_ref, kn_ref):
        for src, g_ref, dst in ((q_ref, gq_ref, qn_ref), (k_ref, gk_ref, kn_ref)):
            v = src[...]
            ms = _mm_xl(v * v, bd_ref[...]) * (1.0 / SB_HD)
            dst[...] = (v * lax.rsqrt(ms + EPS) * g_ref[...]).astype(dst.dtype)

    return pl.pallas_call(
        body, name="sb_prep", grid=(S // tm,),
        in_specs=[_rb(tm, SB_W, C_SB_Q // SB_W), _rb(tm, SB_W, C_SB_K // SB_W),
                  _fs((1, SB_W)), _fs((1, SB_W)), _fs((SB_W, SB_W))],
        out_specs=[_rb(tm, SB_W), _rb(tm, SB_W)],
        out_shape=[_sds((S, SB_W), _MXU_DTYPE)] * 2, compiler_params=_cp(1),
    )(p, p, gq_t, gk_t, bd)


def _sb_prep_bwd(p, dqn, dkn, gq_t, gk_t, tm=256):
    S = p.shape[0]
    bd = _head_sum_matrix()

    def body(q_ref, k_ref, dqn_ref, dkn_ref, gq_ref, gk_ref, bd_ref, dq_ref, dk_ref, dgq_ref, dgk_ref):
        @pl.when(pl.program_id(0) == 0)
        def _():
            dgq_ref[...] = jnp.zeros_like(dgq_ref)
            dgk_ref[...] = jnp.zeros_like(dgk_ref)

        for src, dn_ref, g_ref, dst, dg_ref in ((q_ref, dqn_ref, gq_ref, dq_ref, dgq_ref),
                                                (k_ref, dkn_ref, gk_ref, dk_ref, dgk_ref)):
            v, dn = src[...], dn_ref[...]
            r = lax.rsqrt(_mm_xl(v * v, bd_ref[...]) * (1.0 / SB_HD) + EPS)
            vh = v * r
            dg_ref[...] += jnp.sum(dn * vh, axis=0, keepdims=True)
            dvh = dn * g_ref[...]
            m = _mm_xl(dvh * vh, bd_ref[...]) * (1.0 / SB_HD)
            dst[...] = (r * (dvh - vh * m)).astype(dst.dtype)

    return pl.pallas_call(
        body, name="sb_prep_bwd", grid=(S // tm,),
        in_specs=[_rb(tm, SB_W, C_SB_Q // SB_W), _rb(tm, SB_W, C_SB_K // SB_W), _rb(tm, SB_W), _rb(tm, SB_W),
                  _fs((1, SB_W)), _fs((1, SB_W)), _fs((SB_W, SB_W))],
        out_specs=[_rb(tm, SB_W), _rb(tm, SB_W), _fs((1, SB_W)), _fs((1, SB_W))],
        out_shape=[_sds((S, SB_W), _MXU_DTYPE)] * 2 + [_sds((1, SB_W), F32)] * 2, compiler_params=_cp(1),
    )(p, p, dqn, dkn, gq_t, gk_t, bd)


def _sb_consts():
    r, c = _iota2((QB, QB), 0), _iota2((QB, QB), 1)
    lane = _iota2((1, LANES), 1)
    return r, c, lane


def _sb_fwd(qn, kn, p):
    S = qn.shape[0]
    scale = 1.0 / math.sqrt(SB_HD)

    def body(q_ref, k_ref, v_ref, o_ref):
        i = pl.program_id(1)
        r, c, lane = _sb_consts()
        u_gt = (r > c).astype(BF16)
        strict = jnp.concatenate([c < r, c < r], axis=0)
        q = q_ref[...]
        mask0 = (lane // SB_HD) == 0
        zero = jnp.zeros_like(q)
        qh = jnp.concatenate([jnp.where(mask0, q, zero), jnp.where(mask0, zero, q)], axis=0)

        def block(off, carry, diagonal):
            o, run = carry
            kj = k_ref[pl.ds(off, QB), :]
            vj = v_ref[pl.ds(off, QB), :].astype(_MXU_DTYPE)
            z = _mm(qh, kj, _NT) * scale
            sp = _softplus(z)
            sp_m = jnp.where(strict, sp, 0.0) if diagonal else sp
            later = _mm_xl2(sp_m, u_gt)
            w = jnp.exp((z - sp) - later - run)
            if diagonal:
                w = jnp.where(strict, w, 0.0)
            return o + _mm(w, vj), run + jnp.sum(sp_m, axis=1, keepdims=True)

        init = (jnp.zeros((2 * QB, LANES), F32), jnp.zeros((2 * QB, 1), F32))
        carry = block(pl.multiple_of(i * QB, QB), init, True)
        st = lax.while_loop(
            lambda st: jnp.logical_and(st[0] <= i, jnp.min(st[2]) < _SB_DEAD),
            lambda st: (st[0] + 1,) + block(pl.multiple_of((i - st[0]) * QB, QB), st[1:], False),
            (jnp.int32(1),) + carry)
        o_ref[...] = jnp.where(mask0, st[1][:QB], st[1][QB:])

    return pl.pallas_call(
        body, name="sb_fwd", grid=(SB_W // LANES, S // QB),
        in_specs=[pl.BlockSpec((QB, LANES), lambda hp, i: (i, hp)),
                  pl.BlockSpec((S, LANES), lambda hp, i: (0, hp)),
                  pl.BlockSpec((S, LANES), lambda hp, i: (0, C_SB_V // LANES + hp))],
        out_specs=pl.BlockSpec((QB, LANES), lambda hp, i: (i, hp)),
        out_shape=_sds((S, SB_W), F32), compiler_params=_cp(2),
    )(qn, kn, p)


def _sb_bwd(qn, kn, p, do, exchange=None, ex_src=(), early=None, early_src=()):
    S = qn.shape[0]
    scale = 1.0 / math.sqrt(SB_HD)
    grid = (SB_W // LANES, S // QB)
    nx, ne = len(ex_src), len(early_src)

    def body(*refs):
        if exchange is None:
            compute(*refs)
            return
        src, src2 = refs[4:4 + nx], refs[4 + nx:4 + nx + ne]
        o0 = 4 + nx + ne
        xout, sems, sems2 = refs[o0 + 3:o0 + 3 + nx], refs[o0 + 3 + nx:o0 + 6 + nx], refs[o0 + 6 + nx:]
        hp, i = pl.program_id(0), pl.program_id(1)

        @pl.when(jnp.logical_and(hp == 0, i == 0))
        def _():
            exchange.start(src, xout, sems)
            if early is not None:
                early.start(src2, xout[nx - ne:], sems2)

        compute(*refs[:4], *refs[o0:o0 + 3])

        @pl.when(jnp.logical_and(hp == grid[0] - 1, i == grid[1] - 1))
        def _():
            exchange.wait(src, xout, sems)
            if early is not None:
                early.wait(src2, xout[nx - ne:], sems2)

    def compute(q_ref, k_ref, v_ref, do_ref, dq_ref, dk_ref, dv_ref):
        i = pl.program_id(1)

        @pl.when(i == 0)
        def _():
            dk_ref[...] = jnp.zeros_like(dk_ref)
            dv_ref[...] = jnp.zeros_like(dv_ref)

        r, c, lane = _sb_consts()
        u_le = (r <= c).astype(BF16)
        u_lt = (r < c).astype(BF16)
        strict = jnp.concatenate([c < r, c < r], axis=0)
        q = q_ref[...]
        do = do_ref[...].astype(_MXU_DTYPE)
        mask0 = (lane // SB_HD) == 0
        zero, zero_do = jnp.zeros_like(q), jnp.zeros_like(do)
        qh = jnp.concatenate([jnp.where(mask0, q, zero), jnp.where(mask0, zero, q)], axis=0)
        doh = jnp.concatenate([jnp.where(mask0, do, zero_do), jnp.where(mask0, zero_do, do)], axis=0)

        def sums(off, run, diagonal):
            sp = _softplus(_mm(qh, k_ref[pl.ds(off, QB), :], _NT) * scale)
            if diagonal:
                sp = jnp.where(strict, sp, 0.0)
            return run + jnp.sum(sp, axis=1, keepdims=True)

        run = sums(pl.multiple_of(i * QB, QB), jnp.zeros((2 * QB, 1), F32), True)
        nb, tot = lax.while_loop(
            lambda st: jnp.logical_and(st[0] <= i, jnp.min(st[1]) < _SB_DEAD),
            lambda st: (st[0] + 1, sums(pl.multiple_of((i - st[0]) * QB, QB), st[1], False)),
            (jnp.int32(1), run))
        first = i + 1 - nb

        def block(off, carry, diagonal):
            dq, pre_sp, pre_e = carry
            kj = k_ref[pl.ds(off, QB), :]
            vj = v_ref[pl.ds(off, QB), :].astype(_MXU_DTYPE)
            z = _mm(qh, kj, _NT) * scale
            sp = _softplus(z)
            a = z - sp
            sp_m = jnp.where(strict, sp, 0.0) if diagonal else sp
            incl = _mm_xl2(sp_m, u_le)
            w = jnp.exp(a - ((tot - pre_sp) - incl))
            if diagonal:
                w = jnp.where(strict, w, 0.0)
            e = w * _mm(doh, vj, _NT)
            db = pre_e + _mm_xl2(e, u_lt)
            dz = (e - jnp.exp(a) * (e + db)) * scale
            if diagonal:
                dz = jnp.where(strict, dz, 0.0)
            dk_ref[pl.ds(off, QB), :] += _mm(dz, qh, _TN)
            dv_ref[pl.ds(off, QB), :] += _mm(w, doh, _TN)
            return (dq + _mm(dz, kj), pre_sp + jnp.sum(sp_m, axis=1, keepdims=True),
                    pre_e + jnp.sum(e, axis=1, keepdims=True))

        zero_col = jnp.zeros((2 * QB, 1), F32)
        init = (jnp.zeros((2 * QB, LANES), F32), zero_col, zero_col)
        carry = lax.fori_loop(first, i, lambda j, cr: block(pl.multiple_of(j * QB, QB), cr, False), init)
        carry = block(pl.multiple_of(i * QB, QB), carry, True)
        dq_ref[...] = jnp.where(mask0, carry[0][:QB], carry[0][QB:])

    blk = pl.BlockSpec((QB, LANES), lambda hp, i: (i, hp))
    full = pl.BlockSpec((S, LANES), lambda hp, i: (0, hp))
    hbm = pl.BlockSpec(memory_space=pl.ANY)
    outs = pl.pallas_call(
        body, name="sb_bwd", grid=grid,
        in_specs=[blk, full, pl.BlockSpec((S, LANES), lambda hp, i: (0, C_SB_V // LANES + hp)), blk]
        + [hbm] * (nx + ne),
        out_specs=[blk, full, full] + [hbm] * nx,
        out_shape=[_sds((S, SB_W), F32)] * 3 + (exchange.out_shape if exchange else []),
        scratch_shapes=(exchange.scratch if exchange else []) + (early.scratch if early else []),
        compiler_params=_cp(2),
    )(qn, kn, p, do, *ex_src, *early_src)
    return outs[:3], outs[3:]


def _dn_prep(p, conv_w, a_row, dtb_row, tm=256):
    S = p.shape[0]
    W3 = 3 * DN_W
    nhalo = tm // 8

    def body(x_ref, halo_ref, w_ref, ba_ref, a_ref, dtb_ref, qkv_ref, bb_ref, gc_ref, gl_ref):
        i = pl.program_id(0)
        halo = jnp.where(i > 0, halo_ref[...], 0.0)
        xf = jnp.concatenate([halo, x_ref[...]], axis=0)
        acc = jnp.zeros((tm, W3), F32)
        for k in range(CONV_K):
            sh = CONV_K - 1 - k
            xs = xf if sh == 0 else pltpu.roll(xf, sh, 0)
            acc = acc + xs[8:, :] * w_ref[k:k + 1, :]
        s = _silu(acc)
        for gi in range(2 * DN_HEADS):
            sl = slice(gi * LANES, (gi + 1) * LANES)
            sg = s[:, sl]
            rinv = lax.rsqrt(jnp.sum(sg * sg, axis=1, keepdims=True) + EPS)
            qkv_ref[:, sl] = sg * rinv * (DN_HD ** -0.5 if gi < DN_HEADS else 1.0)
        qkv_ref[:, 2 * DN_W:] = s[:, 2 * DN_W:]

        ba = ba_ref[...]
        beta = _sigmoid(ba)
        g = -jnp.exp(a_ref[...]) * _softplus(ba + dtb_ref[...])
        lr, lc = _iota2((LANES, DN_W), 0), _iota2((LANES, DN_W), 1)
        sel_b = (lr == lc // LANES).astype(BF16)
        sel_g = (lr == lc // LANES + DN_HEADS).astype(BF16)
        bb_ref[...] = _mm_xl(beta, sel_b)
        graw = _mm_xl(g, sel_g)
        rr, cc = _iota2((tm, tm), 0), _iota2((tm, tm), 1)
        tri = jnp.logical_and(rr >= cc, rr // CHUNK == cc // CHUNK).astype(BF16)
        gc = _mm_xr(tri, graw)
        last = (cc == (rr // CHUNK) * CHUNK + (CHUNK - 1)).astype(BF16)
        gc_ref[...] = gc
        gl_ref[...] = _mm_xr(last, gc)

    return pl.pallas_call(
        body, name="dn_prep", grid=(S // tm,),
        in_specs=[_rb(tm, W3, 0), pl.BlockSpec((8, W3), lambda i: (jnp.maximum(i * nhalo - 1, 0), 0)),
                  _fs((CONV_K, W3)), _rb(tm, LANES, (p.shape[1] - LANES) // LANES),
                  _fs((1, LANES)), _fs((1, LANES))],
        out_specs=[_rb(tm, W3), _rb(tm, DN_W), _rb(tm, DN_W), _rb(tm, DN_W)],
        out_shape=[_sds((S, W3), F32)] + [_sds((S, DN_W), F32)] * 3, compiler_params=_cp(1),
    )(p, p, conv_w, p, a_row, dtb_row)


def _heads(ref, base=0):
    return jnp.stack([ref[:, base + h * LANES:base + (h + 1) * LANES] for h in range(DN_HEADS)])


def _per_head(const):
    return jnp.broadcast_to(const[None], (DN_HEADS,) + const.shape)


def _dn_chunk_terms(q, k, v, beta, gc, gl):
    r, c = _iota2((CHUNK, CHUNK), 0), _iota2((CHUNK, CHUNK), 1)
    tril, strict = r >= c, r > c
    gcol = _mm_xl(gc, _per_head(jnp.full((LANES, CHUNK), 1.0 / LANES, F32)), _BNN)
    grow = _mm_xr(_per_head(jnp.full((CHUNK, LANES), 1.0 / LANES, F32)), gc, _BNT)
    dec = jnp.where(tril, jnp.exp(jnp.where(tril, gcol - grow, 0.0)), 0.0)
    gam = jnp.exp(gc)
    dlt = jnp.exp(gl - gc)
    kb, vb = k * beta, v * beta
    pm = _mm(kb, k, _BNT)
    qk = _mm(q, k, _BNT)
    m = jnp.where(strict, pm * dec, 0.0)
    a = jnp.where(tril, qk * dec, 0.0)
    return dict(tril=tril, strict=strict, dec=dec, gam=gam, dlt=dlt, kb=kb, vb=vb, m=m, a=a)


def _dn_fwd(qkv, bb, gcb, glb, gather=None, g_src=()):
    S = qkv.shape[0]
    N = S // CHUNK
    nx = len(g_src)

    def body(*refs):
        if gather is None:
            compute(*refs)
            return
        src, gout, sems = refs[4:4 + nx], refs[7 + nx:7 + 2 * nx], refs[8 + 2 * nx:]

        @pl.when(pl.program_id(0) == 0)
        def _():
            gather.start(src, gout, sems)

        compute(*refs[:4], *refs[4 + nx:7 + nx], refs[7 + 2 * nx])

        @pl.when(pl.program_id(0) == N - 1)
        def _():
            gather.wait(src, gout, sems)

    def compute(qkv_ref, bb_ref, gc_ref, gl_ref, o_ref, t_ref, sall_ref, s_scr):
        @pl.when(pl.program_id(0) == 0)
        def _():
            s_scr[...] = jnp.zeros_like(s_scr)

        r, c = _iota2((CHUNK, CHUNK), 0), _iota2((CHUNK, CHUNK), 1)
        eye = (r == c).astype(F32)
        q, k, v = _heads(qkv_ref), _heads(qkv_ref, DN_W), _heads(qkv_ref, 2 * DN_W)
        beta, gc, gl = _heads(bb_ref), _heads(gc_ref), _heads(gl_ref)
        s_prev = s_scr[...]
        sall_ref[0] = s_prev.astype(sall_ref.dtype)
        s0 = s_prev.astype(sall_ref.dtype).astype(F32)
        t = _dn_chunk_terms(q, k, v, beta, gc, gl)
        pw = -t["m"]
        tinv = eye + pw
        for _ in range(5):
            pw = _mm3(pw, pw, _BNN)
            tinv = tinv + _mm3(tinv, pw, _BNN)
        t_ref[...] = tinv
        u = _mm3(tinv, t["vb"], _BNN)
        w = _mm3(tinv, t["kb"] * t["gam"], _BNN)
        vn = u - _mm(w, s0, _BNN)
        o = _mm(q * t["gam"], s0, _BNN) + _mm(t["a"], vn, _BNN)
        for h in range(DN_HEADS):
            o_ref[:, h * LANES:(h + 1) * LANES] = o[h]
        egl = jnp.exp(jnp.concatenate([gl, gl], axis=1))
        s_scr[...] = s_prev * egl + _mm(k * t["dlt"], vn, _BTN)

    hbm = pl.BlockSpec(memory_space=pl.ANY)
    outs = pl.pallas_call(
        body, name="dn_fwd", grid=(N,),
        in_specs=[_rb(CHUNK, 3 * DN_W), _rb(CHUNK, DN_W), _rb(CHUNK, DN_W), _rb(CHUNK, DN_W)] + [hbm] * nx,
        out_specs=[_rb(CHUNK, DN_W), pl.BlockSpec((DN_HEADS, CHUNK, CHUNK), lambda n: (0, n, 0)),
                   pl.BlockSpec((1, DN_HEADS, DN_HD, DN_HD), lambda n: (n, 0, 0, 0))] + [hbm] * nx,
        out_shape=[_sds((S, DN_W), F32), _sds((DN_HEADS, S, CHUNK), F32),
                   _sds((N, DN_HEADS, DN_HD, DN_HD), _MXU_DTYPE)] + (gather.out_shape if gather else []),
        scratch_shapes=[pltpu.VMEM((DN_HEADS, DN_HD, DN_HD), F32)] + (gather.scratch if gather else []),
        compiler_params=_cp(1),
    )(qkv, bb, gcb, glb, *g_src)
    return outs[:3], outs[3:]


def _dn_bwd(qkv, bb, gcb, glb, tinv_all, sall, do):
    S = qkv.shape[0]
    N = S // CHUNK

    def body(qkv_ref, bb_ref, gc_ref, gl_ref, t_ref, sall_ref, do_ref, dqkv_ref, dbb_ref, dg_ref, ds_scr):
        @pl.when(pl.program_id(0) == 0)
        def _():
            ds_scr[...] = jnp.zeros_like(ds_scr)

        r, c = _iota2((CHUNK, CHUNK), 0), _iota2((CHUNK, CHUNK), 1)
        eye = (r == c).astype(F32)
        u_ge = (c >= r).astype(F32)
        last_row = _iota2((CHUNK, LANES), 0) == CHUNK - 1
        eye_h, u_ge_h = _per_head(eye), _per_head(u_ge)
        q, k, v = _heads(qkv_ref), _heads(qkv_ref, DN_W), _heads(qkv_ref, 2 * DN_W)
        beta, gc, gl = _heads(bb_ref), _heads(gc_ref), _heads(gl_ref)
        tinv = t_ref[...]
        s0 = sall_ref[0].astype(F32)
        do = _heads(do_ref)
        ds1 = ds_scr[...]
        t = _dn_chunk_terms(q, k, v, beta, gc, gl)
        gam, dlt, kb, vb, dec = t["gam"], t["dlt"], t["kb"], t["vb"], t["dec"]
        kbg = kb * gam
        u = _mm3(tinv, vb, _BNN)
        w = _mm3(tinv, kbg, _BNN)
        vn = u - _mm(w, s0, _BNN)
        qg, kd = q * gam, k * dlt
        egl = jnp.exp(gl)
        egl2 = jnp.concatenate([egl, egl], axis=1)

        dvn = _mm(t["a"], do, _BTN) + _mm(kd, ds1, _BNN)
        da = jnp.where(t["tril"], _mm(do, vn, _BNT), 0.0)
        dqg = _mm(do, s0, _BNT)
        dkd = _mm(vn, ds1, _BNT)
        dw = -_mm(dvn, s0, _BNT)
        ds_scr[...] = _mm(qg, do, _BTN) + egl2 * ds1 - _mm(w, dvn, _BTN)
        tt = _mm_xr(eye_h, tinv, _BNT)
        dvb = _mm3(tt, dvn, _BNN)
        dkbg = _mm3(tt, dw, _BNN)
        dm = -jnp.where(t["strict"], _mm(dvb, u, _BNT) + _mm(dkbg, w, _BNT), 0.0)
        dpm = dm * dec
        dqk = da * dec
        dkb = dkbg * gam + _mm(dpm, k, _BNN)
        dk = dkd * dlt + _mm(dpm, kb, _BTN) + _mm(dqk, q, _BTN) + dkb * beta
        dq = dqg * gam + _mm(dqk, k, _BNN)
        dv = dvb * beta
        dbeta = jnp.sum(dkb * k, axis=2, keepdims=True) + jnp.sum(dvb * v, axis=2, keepdims=True)
        dgam = jnp.sum(dqg * q, axis=2, keepdims=True) + jnp.sum(dkbg * kb, axis=2, keepdims=True)
        ddlt = jnp.sum(dkd * k, axis=2, keepdims=True)
        xm = dm * t["m"] + da * t["a"]
        xt = _mm_xr(eye_h, xm, _BNT)
        dgc = (dgam * gam - ddlt * dlt + jnp.sum(xm, axis=2, keepdims=True) - jnp.sum(xt, axis=2, keepdims=True))
        dgl = jnp.sum(ddlt * dlt, axis=1, keepdims=True) + jnp.sum(
            jnp.sum(ds1 * s0, axis=2, keepdims=True), axis=1, keepdims=True) * jnp.max(egl, axis=1, keepdims=True)
        dgc = dgc + jnp.where(last_row, dgl, 0.0)
        dg = _mm_xr(u_ge_h, dgc, _BNN)
        for h in range(DN_HEADS):
            sl = slice(h * LANES, (h + 1) * LANES)
            dqkv_ref[:, sl] = dq[h]
            dqkv_ref[:, DN_W + h * LANES:DN_W + (h + 1) * LANES] = dk[h]
            dqkv_ref[:, 2 * DN_W + h * LANES:2 * DN_W + (h + 1) * LANES] = dv[h]
            dbb_ref[:, sl] = jnp.broadcast_to(dbeta[h], (CHUNK, LANES))
            dg_ref[:, sl] = dg[h]

    rev = lambda w: pl.BlockSpec((CHUNK, w), lambda n: (N - 1 - n, 0))
    return pl.pallas_call(
        body, name="dn_bwd", grid=(N,),
        in_specs=[rev(3 * DN_W), rev(DN_W), rev(DN_W), rev(DN_W),
                  pl.BlockSpec((DN_HEADS, CHUNK, CHUNK), lambda n: (0, N - 1 - n, 0)),
                  pl.BlockSpec((1, DN_HEADS, DN_HD, DN_HD), lambda n: (N - 1 - n, 0, 0, 0)), rev(DN_W)],
        out_specs=[rev(3 * DN_W), rev(DN_W), rev(DN_W)],
        out_shape=[_sds((S, 3 * DN_W), F32), _sds((S, DN_W), F32), _sds((S, DN_W), F32)],
        scratch_shapes=[pltpu.VMEM((DN_HEADS, DN_HD, DN_HD), F32)],
        compiler_params=_cp(1),
    )(qkv, bb, gcb, glb, tinv_all, sall, do)


def _dn_prep_bwd_a(p, dqkv, dbb, dgb, conv_w, a_row, dtb_row, tm=256):
    S, PC = p.shape
    W3 = 3 * DN_W
    nhalo = tm // 8

    def body(x_ref, halo_ref, w_ref, ba_ref, a_ref, dtb_ref, dqkv_ref, dbb_ref, dgb_ref,
             dc_ref, dba_ref, dal_ref, ddt_ref):
        i = pl.program_id(0)

        @pl.when(i == 0)
        def _():
            dal_ref[...] = jnp.zeros_like(dal_ref)
            ddt_ref[...] = jnp.zeros_like(ddt_ref)

        halo = jnp.where(i > 0, halo_ref[...], 0.0)
        xf = jnp.concatenate([halo, x_ref[...]], axis=0)
        acc = jnp.zeros((tm, W3), F32)
        for k in range(CONV_K):
            sh = CONV_K - 1 - k
            xs = xf if sh == 0 else pltpu.roll(xf, sh, 0)
            acc = acc + xs[8:, :] * w_ref[k:k + 1, :]
        s = _silu(acc)
        ds_act = _dsilu(acc)
        for gi in range(2 * DN_HEADS):
            sl = slice(gi * LANES, (gi + 1) * LANES)
            sg = s[:, sl]
            rinv = lax.rsqrt(jnp.sum(sg * sg, axis=1, keepdims=True) + EPS)
            nh = sg * rinv
            dn = dqkv_ref[:, sl] * (DN_HD ** -0.5 if gi < DN_HEADS else 1.0)
            dsg = rinv * (dn - nh * jnp.sum(dn * nh, axis=1, keepdims=True))
            dc_ref[:, sl] = dsg * ds_act[:, sl]
        dc_ref[:, 2 * DN_W:] = dqkv_ref[:, 2 * DN_W:] * ds_act[:, 2 * DN_W:]

        ba = ba_ref[...]
        beta = _sigmoid(ba)
        ea = jnp.exp(a_ref[...])
        pre = ba + dtb_ref[...]
        g = -ea * _softplus(pre)
        lr, lc = _iota2((DN_W, LANES), 0), _iota2((DN_W, LANES), 1)
        pick_b = jnp.where(lc == lr // LANES, 1.0 / LANES, 0.0)
        pick_g = jnp.where(lc == lr // LANES + DN_HEADS, 1.0 / LANES, 0.0)
        dbeta = _mm_xl(dbb_ref[...], pick_b)
        dg = _mm_xl(dgb_ref[...], pick_g)
        lane = _iota2((1, LANES), 1)
        da = dg * (-ea) * _sigmoid(pre)
        dba_ref[...] = jnp.where(lane < DN_HEADS, dbeta * beta * (1.0 - beta),
                                 jnp.where(lane < 2 * DN_HEADS, da, 0.0)).astype(dba_ref.dtype)
        dal_ref[...] += jnp.sum(dg * g, axis=0, keepdims=True)
        ddt_ref[...] += jnp.sum(da, axis=0, keepdims=True)

    return pl.pallas_call(
        body, name="dn_prep_bwd_a", grid=(S // tm,),
        in_specs=[_rb(tm, W3, 0), pl.BlockSpec((8, W3), lambda i: (jnp.maximum(i * nhalo - 1, 0), 0)),
                  _fs((CONV_K, W3)), _rb(tm, LANES, (PC - LANES) // LANES), _fs((1, LANES)), _fs((1, LANES)),
                  _rb(tm, W3), _rb(tm, DN_W), _rb(tm, DN_W)],
        out_specs=[_rb(tm, W3), _rb(tm, LANES), _fs((1, LANES)), _fs((1, LANES))],
        out_shape=[_sds((S, W3), F32), _sds((S, LANES), _MXU_DTYPE), _sds((1, LANES), F32), _sds((1, LANES), F32)],
        compiler_params=_cp(1),
    )(p, p, conv_w, p, a_row, dtb_row, dqkv, dbb, dgb)


def _dn_prep_bwd_b(p, dc, conv_w, tm=256):
    S = p.shape[0]
    W3 = 3 * DN_W
    nhalo = tm // 8
    nblk = S // tm

    def body(x_ref, xh_ref, dc_ref, dch_ref, w_ref, dx_ref, dw_ref):
        i = pl.program_id(0)

        @pl.when(i == 0)
        def _():
            dw_ref[...] = jnp.zeros_like(dw_ref)

        dcv = dc_ref[...]
        xf = jnp.concatenate([jnp.where(i > 0, xh_ref[...], 0.0), x_ref[...]], axis=0)
        df = jnp.concatenate([dcv, jnp.where(i < nblk - 1, dch_ref[...], 0.0)], axis=0)
        acc = jnp.zeros((tm, W3), F32)
        for k in range(CONV_K):
            sh = CONV_K - 1 - k
            xs = xf if sh == 0 else pltpu.roll(xf, sh, 0)
            dw_ref[k:k + 1, :] += jnp.sum(dcv * xs[8:, :], axis=0, keepdims=True)
            ds = df if sh == 0 else pltpu.roll(df, tm + 8 - sh, 0)
            acc = acc + ds[:tm, :] * w_ref[k:k + 1, :]
        dx_ref[...] = acc.astype(dx_ref.dtype)

    return pl.pallas_call(
        body, name="dn_prep_bwd_b", grid=(nblk,),
        in_specs=[_rb(tm, W3, 0), pl.BlockSpec((8, W3), lambda i: (jnp.maximum(i * nhalo - 1, 0), 0)),
                  _rb(tm, W3), pl.BlockSpec((8, W3), lambda i: (jnp.minimum((i + 1) * nhalo, S // 8 - 1), 0)),
                  _fs((CONV_K, W3))],
        out_specs=[_rb(tm, W3), _fs((CONV_K, W3))],
        out_shape=[_sds((S, W3), _MXU_DTYPE), _sds((CONV_K, W3), F32)], compiler_params=_cp(1),
    )(p, p, dc, dc, conv_w)


def _gate(o_att, o_dn, p, gn, tm=256):
    S = p.shape[0]

    def body(oa_ref, zs_ref, od_ref, zd_ref, gn_ref, osb_ref, odn_ref):
        osb_ref[...] = (oa_ref[...] * _silu(zs_ref[...])).astype(osb_ref.dtype)
        for h in range(DN_HEADS):
            sl = slice(h * LANES, (h + 1) * LANES)
            o = od_ref[:, sl]
            r = lax.rsqrt(jnp.mean(o * o, axis=1, keepdims=True) + EPS)
            odn_ref[:, sl] = (o * r * gn_ref[...] * _silu(zd_ref[:, sl])).astype(odn_ref.dtype)

    return pl.pallas_call(
        body, name="gate", grid=(S // tm,),
        in_specs=[_rb(tm, SB_W), _rb(tm, SB_W, C_SB_Z // SB_W), _rb(tm, DN_W), _rb(tm, DN_W, C_DN_Z // DN_W),
                  _fs((1, LANES))],
        out_specs=[_rb(tm, SB_W), _rb(tm, DN_W)],
        out_shape=[_sds((S, SB_W), _MXU_DTYPE), _sds((S, DN_W), _MXU_DTYPE)], compiler_params=_cp(1),
    )(o_att, p, o_dn, p, gn)


def _gate_bwd(db_sb, db_dn, wb_sb, wb_dn, o_att, o_dn, p, gn, tm=256):
    S = p.shape[0]
    D = db_sb.shape[1]

    def body(dbs_ref, dbd_ref, ws_ref, wd_ref, oa_ref, zs_ref, od_ref, zd_ref, gn_ref,
             doa_ref, dzs_ref, dod_ref, dzd_ref, dgn_ref):
        @pl.when(pl.program_id(0) == 0)
        def _():
            dgn_ref[...] = jnp.zeros_like(dgn_ref)

        do_sb = _mm(dbs_ref[...], ws_ref[...], _NT)
        zs = zs_ref[...]
        doa_ref[...] = do_sb * _silu(zs)
        dzs_ref[...] = (do_sb * oa_ref[...] * _dsilu(zs)).astype(dzs_ref.dtype)
        do_dnn = _mm(dbd_ref[...], wd_ref[...], _NT)
        gnv = gn_ref[...]
        for h in range(DN_HEADS):
            sl = slice(h * LANES, (h + 1) * LANES)
            o, z, dout = od_ref[:, sl], zd_ref[:, sl], do_dnn[:, sl]
            r = lax.rsqrt(jnp.mean(o * o, axis=1, keepdims=True) + EPS)
            oh = o * r
            sz = _silu(z)
            dzd_ref[:, sl] = (dout * oh * gnv * _dsilu(z)).astype(dzd_ref.dtype)
            dgn_ref[...] += jnp.sum(dout * sz * oh, axis=0, keepdims=True)
            doh = dout * gnv * sz
            dod_ref[:, sl] = r * (doh - oh * jnp.mean(doh * oh, axis=1, keepdims=True))

    return pl.pallas_call(
        body, name="gate_bwd", grid=(S // tm,),
        in_specs=[_rb(tm, D), _rb(tm, D), _fs((SB_W, D)), _fs((DN_W, D)), _rb(tm, SB_W),
                  _rb(tm, SB_W, C_SB_Z // SB_W), _rb(tm, DN_W), _rb(tm, DN_W, C_DN_Z // DN_W), _fs((1, LANES))],
        out_specs=[_rb(tm, SB_W), _rb(tm, SB_W), _rb(tm, DN_W), _rb(tm, DN_W), _fs((1, LANES))],
        out_shape=[_sds((S, SB_W), F32), _sds((S, SB_W), _MXU_DTYPE), _sds((S, DN_W), F32),
                   _sds((S, DN_W), _MXU_DTYPE), _sds((1, LANES), F32)],
        compiler_params=_cp(1),
    )(db_sb, db_dn, wb_sb, wb_dn, o_att, p, o_dn, p, gn)


def _branch(o_sb, o_dnn, wb_sb, wb_dn, p, D, tm=256):
    S = p.shape[0]

    def body(os_ref, od_ref, ws_ref, wd_ref, ms_ref, md_ref, y_ref, bs_ref, bd_ref):
        bs = _mm(os_ref[...], ws_ref[...])
        bdn = _mm(od_ref[...], wd_ref[...])
        bs_ref[...] = bs
        bd_ref[...] = bdn
        y_ref[...] = (_sigmoid(ms_ref[...]) * bs + _sigmoid(md_ref[...]) * bdn).astype(y_ref.dtype)

    return pl.pallas_call(
        body, name="branch", grid=(S // tm,),
        in_specs=[_rb(tm, SB_W), _rb(tm, DN_W), _fs((SB_W, D)), _fs((DN_W, D)),
                  _rb(tm, D, C_MG // D), _rb(tm, D, C_MG // D + 1)],
        out_specs=[_rb(tm, D), _rb(tm, D), _rb(tm, D)],
        out_shape=[_sds((S, D), _MXU_DTYPE), _sds((S, D), F32), _sds((S, D), F32)], compiler_params=_cp(1),
    )(o_sb, o_dnn, wb_sb, wb_dn, p, p)


def _out_proj(x, y, w_out, gate, tm=256):
    S, D = x.shape

    def body(x_ref, y_ref, w_ref, g_ref, xn_ref, out_ref):
        out = _mm(y_ref[...], w_ref[...])
        out_ref[...] = out
        xn_ref[...] = x_ref[...] + g_ref[...] * out

    return pl.pallas_call(
        body, name="out_proj", grid=(S // tm,),
        in_specs=[_rb(tm, D), _rb(tm, D), _fs((D, D)), _fs((1, D))],
        out_specs=[_rb(tm, D), _rb(tm, D)],
        out_shape=[_sds((S, D), F32), _sds((S, D), F32)], compiler_params=_cp(1),
    )(x, y, w_out, gate)


def _out_bwd(dxn, out, gate, w_out, p, b_sb, b_dn, tm=256):
    S, D = dxn.shape

    def body(dxn_ref, out_ref, g_ref, w_ref, ms_ref, md_ref, bs_ref, bd_ref,
             dout_ref, dbs_ref, dbd_ref, dm_ref, dgate_ref):
        @pl.when(pl.program_id(0) == 0)
        def _():
            dgate_ref[...] = jnp.zeros_like(dgate_ref)

        dxv = dxn_ref[...]
        dgate_ref[...] += jnp.sum(dxv * out_ref[...], axis=0, keepdims=True)
        dout = (g_ref[...] * dxv).astype(dout_ref.dtype)
        dout_ref[...] = dout
        dy = _mm(dout, w_ref[...], _NT)
        s1, s2 = _sigmoid(ms_ref[...]), _sigmoid(md_ref[...])
        dbs_ref[...] = (dy * s1).astype(dbs_ref.dtype)
        dbd_ref[...] = (dy * s2).astype(dbd_ref.dtype)
        dm_ref[:, :D] = (dy * bs_ref[...] * s1 * (1.0 - s1)).astype(dm_ref.dtype)
        dm_ref[:, D:] = (dy * bd_ref[...] * s2 * (1.0 - s2)).astype(dm_ref.dtype)

    return pl.pallas_call(
        body, name="out_bwd", grid=(S // tm,),
        in_specs=[_rb(tm, D), _rb(tm, D), _fs((1, D)), _fs((D, D)), _rb(tm, D, C_MG // D),
                  _rb(tm, D, C_MG // D + 1), _rb(tm, D), _rb(tm, D)],
        out_specs=[_rb(tm, D), _rb(tm, D), _rb(tm, D), _rb(tm, 2 * D), _fs((1, D))],
        out_shape=[_sds((S, D), _MXU_DTYPE)] * 3 + [_sds((S, 2 * D), _MXU_DTYPE), _sds((1, D), F32)],
        compiler_params=_cp(1),
    )(dxn, out, gate, w_out, p, p, b_sb, b_dn)


def _loss_head(xf, target, tm=256):
    S, D = xf.shape

    def body(x_ref, t_ref, dy_ref, loss_ref):
        @pl.when(pl.program_id(0) == 0)
        def _():
            loss_ref[...] = jnp.zeros_like(loss_ref)

        e = x_ref[...] - t_ref[...]
        dy_ref[...] = e * (1.0 / D)
        row = jnp.sum(e * e, axis=1, keepdims=True) * (1.0 / D)
        loss_ref[...] += 0.5 * jnp.sum(row, axis=0, keepdims=True)

    return pl.pallas_call(
        body, name="loss_head", grid=(S // tm,),
        in_specs=[_rb(tm, D), _rb(tm, D)], out_specs=[_rb(tm, D), _fs((1, LANES))],
        out_shape=[_sds((S, D), F32), _sds((1, LANES), F32)], compiler_params=_cp(1),
    )(xf, target)


def _ada_fwd(c_all, ada_w, ada_b_sh):
    L, D, n = ada_w.shape
    B = c_all.shape[0]

    def body(c_ref, w_ref, b_ref, o_ref):
        sc = _silu(c_ref[...])
        o_ref[0] = _mm(sc, w_ref[0]) + b_ref[0]

    return pl.pallas_call(
        body, name="ada_fwd", grid=(L,),
        in_specs=[_fs((B, D)), pl.BlockSpec((1, D, n), lambda l: (l, 0, 0)), pl.BlockSpec((1, 1, n), lambda l: (l, 0, 0))],
        out_specs=pl.BlockSpec((1, B, n), lambda l: (l, 0, 0)),
        out_shape=_sds((L, B, n), F32), compiler_params=_cp(1),
    )(c_all, ada_w, ada_b_sh)


def _ada_bwd(c_all_t, dmod_sh):
    D, B = c_all_t.shape
    L, _, n = dmod_sh.shape

    def body(c_ref, d_ref, o_ref):
        acc = jnp.zeros((D, n), F32)
        for b in range(B):
            acc = acc + _silu(c_ref[:, b:b + 1]) * d_ref[0, b:b + 1, :]
        o_ref[0] = acc

    return pl.pallas_call(
        body, name="ada_bwd", grid=(L,),
        in_specs=[_fs((D, B)), pl.BlockSpec((1, B, n), lambda l: (l, 0, 0))],
        out_specs=pl.BlockSpec((1, D, n), lambda l: (l, 0, 0)),
        out_shape=_sds((L, D, n), F32), compiler_params=_cp(1),
    )(c_all_t, dmod_sh)


def _sum_parts(name, parts):
    P, R, C = parts.shape
    tr = _pick(R, max(16, min(512, (1 << 19) // (P * C))), 16) if R % 16 == 0 else R

    def body(p_ref, o_ref):
        acc = p_ref[0].astype(F32)
        for k in range(1, P):
            acc = acc + p_ref[k].astype(F32)
        o_ref[...] = acc

    return pl.pallas_call(
        body, name=name, grid=(R // tr,),
        in_specs=[pl.BlockSpec((P, tr, C), lambda i: (0, i, 0))], out_specs=_rb(tr, C),
        out_shape=_sds((R, C), F32), compiler_params=_cp(1),
    )(parts)


def _adamw(name, w, g, m, v):
    L, R, C = w.shape
    tr = _pick(R, 256, 8) if R % 8 == 0 else R
    c1 = 1.0 - ADAM_B1 ** ADAM_STEP
    c2 = 1.0 - ADAM_B2 ** ADAM_STEP

    def body(w_ref, g_ref, m_ref, v_ref, d_ref, mo_ref, vo_ref):
        gv = g_ref[...]
        mn = ADAM_B1 * m_ref[...] + (1.0 - ADAM_B1) * gv
        vn = ADAM_B2 * v_ref[...] + (1.0 - ADAM_B2) * (gv * gv)
        mo_ref[...] = mn
        vo_ref[...] = vn
        d_ref[...] = -ADAM_LR * ((mn / c1) / (jnp.sqrt(vn / c2) + ADAM_EPS) + ADAM_WD * w_ref[...])

    spec = pl.BlockSpec((1, tr, C), lambda l, i: (l, i, 0))
    return pl.pallas_call(
        body, name=name, grid=(L, R // tr),
        in_specs=[spec] * 4, out_specs=[spec] * 3, out_shape=[_sds((L, R, C), F32)] * 3, compiler_params=_cp(2),
    )(w, g, m, v)


def _ag_small(name, blk):
    R, C = blk.shape

    def body(x_ref, out_ref, send_sems, recv_sems, local_sem):
        x, y, c = lax.axis_index("x"), lax.axis_index("y"), lax.axis_index("c")
        me, sibling = (x, y, c), (x, y, 1 - c)
        chips = [(1 - x, y), (x, 1 - y), (1 - x, 1 - y)]

        def rows(px, py, pc):
            return out_ref.at[pl.ds((4 * px + 2 * py + pc) * R, R), :]

        def copy(k, block, to, src=None):
            return pltpu.make_async_remote_copy(
                src_ref=rows(*block) if src is None else src, dst_ref=rows(*block),
                send_sem=send_sems.at[k], recv_sem=recv_sems.at[k], device_id=to, device_id_type=MESH)

        mine = pltpu.make_async_copy(x_ref, rows(*me), local_sem)
        mine.start()
        first = [copy(0, me, sibling, src=x_ref)]
        first += [copy(1 + j, me, (*chip, c), src=x_ref) for j, chip in enumerate(chips)]
        for cp in first:
            cp.start()
        passed = [copy(4 + j, (*chip, c), sibling) for j, chip in enumerate(chips)]
        for j, chip in enumerate(chips):
            copy(1 + j, (*chip, c), me).wait_recv()
            passed[j].start()
        copy(0, sibling, me).wait_recv()
        for j, chip in enumerate(chips):
            copy(4 + j, (*chip, 1 - c), me).wait_recv()
        for cp in first + passed:
            cp.wait_send()
        mine.wait()

    return pl.pallas_call(
        body, name=name, out_shape=_sds((8 * R, C), blk.dtype),
        in_specs=[pl.BlockSpec(memory_space=pltpu.VMEM)], out_specs=pl.BlockSpec(memory_space=pltpu.VMEM),
        scratch_shapes=[pltpu.SemaphoreType.DMA((7,)), pltpu.SemaphoreType.DMA((7,)), pltpu.SemaphoreType.DMA],
    )(blk)


def _row_chunks(ts, row_axis):
    pieces = []
    for t, a in enumerate(ts):
        rows = a.shape[row_axis]
        n = 4 if rows >= 1024 else 1
        pieces += [(t, i * (rows // n), rows // n) for i in range(n)]
    return pieces


def _ag_weights_first(ts):
    nt = len(ts)
    pieces = _row_chunks(ts, 0)
    NP = len(pieces)
    sizes = [nr * ts[t].shape[1] for t, _, nr in pieces]
    split = next(pi for pi in range(NP + 1) if 2 * sum(sizes[:pi]) >= sum(sizes))

    def body(*refs):
        w, out = refs[:nt], refs[nt:2 * nt]
        send_sems, recv_sems, local_sems = refs[2 * nt:]
        x, y, c = lax.axis_index("x"), lax.axis_index("y"), lax.axis_index("c")
        me, sibling = (x, y, c), (x, y, 1 - c)
        mine = 2 * x + y
        chips = [(1 - x, y), (x, 1 - y), (1 - x, 1 - y)]

        def blk(t, shard, r0, nr):
            return out[t].at[shard, r0:r0 + nr, :]

        def copy(k, dst, to, src=None):
            return pltpu.make_async_remote_copy(
                src_ref=dst if src is None else src, dst_ref=dst, send_sem=send_sems.at[k], recv_sem=recv_sems.at[k],
                device_id=to, device_id_type=MESH)

        own = [pltpu.make_async_copy(w[t], out[t].at[mine], local_sems.at[t]) for t in range(nt)]
        for cp in own:
            cp.start()
        for fetcher, lo, hi in ((0, 0, split), (1, split, NP)):
            @pl.when(c == fetcher)
            def _(lo=lo, hi=hi):
                sent = []
                for j, chip in enumerate(chips):
                    for pi in range(lo, hi):
                        t, r0, nr = pieces[pi]
                        sent.append(copy(j * NP + pi, blk(t, mine, r0, nr), (*chip, c), src=w[t].at[r0:r0 + nr, :]))
                        sent[-1].start()
                for j, chip in enumerate(chips):
                    theirs = 2 * chip[0] + chip[1]
                    for pi in range(lo, hi):
                        t, r0, nr = pieces[pi]
                        copy(j * NP + pi, blk(t, theirs, r0, nr), me).wait_recv()
                        sent.append(copy((3 + j) * NP + pi, blk(t, theirs, r0, nr), sibling))
                        sent[-1].start()
                for cp in sent:
                    cp.wait_send()

            @pl.when(c != fetcher)
            def _(lo=lo, hi=hi):
                for j, chip in enumerate(chips):
                    theirs = 2 * chip[0] + chip[1]
                    for pi in range(lo, hi):
                        t, r0, nr = pieces[pi]
                        copy((3 + j) * NP + pi, blk(t, theirs, r0, nr), me).wait_recv()

        for cp in own:
            cp.wait()

    return pl.pallas_call(
        body, name="ag_weights_first", out_shape=[_sds((4,) + a.shape, a.dtype) for a in ts],
        in_specs=[pl.BlockSpec(memory_space=pl.ANY)] * nt, out_specs=[pl.BlockSpec(memory_space=pltpu.VMEM)] * nt,
        scratch_shapes=[pltpu.SemaphoreType.DMA((6 * NP,)), pltpu.SemaphoreType.DMA((6 * NP,)),
                        pltpu.SemaphoreType.DMA((nt,))],
        compiler_params=pltpu.CompilerParams(vmem_limit_bytes=_VMEM_LIMIT),
    )(*ts)


class _WeightGather:
    def __init__(self, ts):
        self.nt = len(ts)
        self.pieces = _row_chunks(ts, 0)
        NP = len(self.pieces)
        self.out_shape = [_sds((4,) + a.shape, a.dtype) for a in ts]
        self.scratch = [pltpu.SemaphoreType.DMA((3 * NP,)), pltpu.SemaphoreType.DMA((3 * NP,)),
                        pltpu.SemaphoreType.DMA((self.nt,))]

    def _copies(self, src, out, sems):
        send_sems, recv_sems, local_sems = sems
        NP = len(self.pieces)
        x, y, c = lax.axis_index("x"), lax.axis_index("y"), lax.axis_index("c")
        mine = 2 * x + y
        own = [pltpu.make_async_copy(src[t], out[t].at[mine], local_sems.at[t]) for t in range(self.nt)]
        sends, recvs = [], []
        for j, chip in enumerate([(1 - x, y), (x, 1 - y), (1 - x, 1 - y)]):
            theirs = 2 * chip[0] + chip[1]
            for pi, (t, r0, nr) in enumerate(self.pieces):
                idx = j * NP + pi
                sends.append(pltpu.make_async_remote_copy(
                    src_ref=src[t].at[r0:r0 + nr, :], dst_ref=out[t].at[mine, r0:r0 + nr, :],
                    send_sem=send_sems.at[idx], recv_sem=recv_sems.at[idx], device_id=(*chip, c), device_id_type=MESH))
                recvs.append(pltpu.make_async_remote_copy(
                    src_ref=out[t].at[theirs, r0:r0 + nr, :], dst_ref=out[t].at[theirs, r0:r0 + nr, :],
                    send_sem=send_sems.at[idx], recv_sem=recv_sems.at[idx], device_id=(x, y, c), device_id_type=MESH))
        return own, sends, recvs

    def start(self, src, out, sems):
        own, sends, _ = self._copies(src, out, sems)
        for cp in own + sends:
            cp.start()

    def wait(self, src, out, sems):
        own, sends, recvs = self._copies(src, out, sems)
        for cp in recvs:
            cp.wait_recv()
        for cp in sends:
            cp.wait_send()
        for cp in own:
            cp.wait()


class _GradExchange:
    def __init__(self, ts, layer):
        self.nt, self.layer = len(ts), layer
        self.pieces = _row_chunks(ts, 1)
        NP = len(self.pieces)
        self.out_shape = [_sds((8,) + a.shape[1:], a.dtype) for a in ts]
        self.scratch = [pltpu.SemaphoreType.DMA((7 * NP,)), pltpu.SemaphoreType.DMA((7 * NP,)),
                        pltpu.SemaphoreType.DMA((self.nt,))]

    def _copies(self, src, out, sems):
        send_sems, recv_sems, local_sems = sems
        NP = len(self.pieces)
        x, y, c = lax.axis_index("x"), lax.axis_index("y"), lax.axis_index("c")
        me = 4 * x + 2 * y + c
        owner = c == self.layer
        own = [pltpu.make_async_copy(src[t].at[2 * x + y], out[t].at[me], local_sems.at[t]) for t in range(self.nt)]
        rel = []
        for k in range(1, 8):
            px = 1 - x if k & 4 else x
            py = 1 - y if k & 2 else y
            source = 4 * px + 2 * py + (1 - c if k & 1 else c)
            sends, recvs = [], []
            for pi, (t, r0, nr) in enumerate(self.pieces):
                idx = (k - 1) * NP + pi
                sends.append(pltpu.make_async_remote_copy(
                    src_ref=src[t].at[2 * px + py, r0:r0 + nr, :], dst_ref=out[t].at[me, r0:r0 + nr, :],
                    send_sem=send_sems.at[idx], recv_sem=recv_sems.at[idx], device_id=(px, py, self.layer),
                    device_id_type=MESH))
                recvs.append(pltpu.make_async_remote_copy(
                    src_ref=out[t].at[source, r0:r0 + nr, :], dst_ref=out[t].at[source, r0:r0 + nr, :],
                    send_sem=send_sems.at[idx], recv_sem=recv_sems.at[idx], device_id=(x, y, c),
                    device_id_type=MESH))
            rel.append((jnp.logical_not(owner) if k & 1 else owner, sends, recvs))
        return owner, own, rel

    def start(self, src, out, sems):
        owner, own, rel = self._copies(src, out, sems)

        @pl.when(owner)
        def _():
            for cp in own:
                cp.start()

        for sending, sends, _ in rel:
            @pl.when(sending)
            def _(sends=sends):
                for cp in sends:
                    cp.start()

    def wait(self, src, out, sems):
        owner, own, rel = self._copies(src, out, sems)

        @pl.when(owner)
        def _():
            for _, _, recvs in rel:
                for cp in recvs:
                    cp.wait_recv()
            for cp in own:
                cp.wait()

        for sending, sends, _ in rel:
            @pl.when(sending)
            def _(sends=sends):
                for cp in sends:
                    cp.wait_send()


def _sibling_join(ts):
    nt = len(ts)
    pieces = _row_chunks(ts, 0)
    NP = len(pieces)

    def body(*refs):
        src, out = refs[:nt], refs[nt:2 * nt]
        send_sems, recv_sems, local_sems = refs[2 * nt:]
        x, y, c = lax.axis_index("x"), lax.axis_index("y"), lax.axis_index("c")
        own = [pltpu.make_async_copy(src[t], out[t].at[c], local_sems.at[t]) for t in range(nt)]
        for cp in own:
            cp.start()
        sent = []
        for pi, (t, r0, nr) in enumerate(pieces):
            sent.append(pltpu.make_async_remote_copy(
                src_ref=src[t].at[r0:r0 + nr, :], dst_ref=out[t].at[c, r0:r0 + nr, :], send_sem=send_sems.at[pi],
                recv_sem=recv_sems.at[pi], device_id=(x, y, 1 - c), device_id_type=MESH))
            sent[-1].start()
        for pi, (t, r0, nr) in enumerate(pieces):
            pltpu.make_async_remote_copy(
                src_ref=src[t].at[r0:r0 + nr, :], dst_ref=out[t].at[1 - c, r0:r0 + nr, :], send_sem=send_sems.at[pi],
                recv_sem=recv_sems.at[pi], device_id=(x, y, c), device_id_type=MESH).wait_recv()
        for cp in sent:
            cp.wait_send()
        for cp in own:
            cp.wait()

    vmem = pl.BlockSpec(memory_space=pltpu.VMEM)
    return pl.pallas_call(
        body, name="sibling_join", out_shape=[_sds((2,) + a.shape, a.dtype) for a in ts],
        in_specs=[vmem] * nt, out_specs=[vmem] * nt,
        scratch_shapes=[pltpu.SemaphoreType.DMA((NP,)), pltpu.SemaphoreType.DMA((NP,)),
                        pltpu.SemaphoreType.DMA((nt,))],
        compiler_params=pltpu.CompilerParams(vmem_limit_bytes=_VMEM_LIMIT),
    )(*ts)


def _layer_fwd(x, shift, scale, gate, lw, next_shards=None):
    D = x.shape[1]
    h = _norm_mod(x, lw["norm_g"], scale, shift)
    p = _matmul("in_proj", h, lw["w_cat"], "nn", F32, tm_cap=1024, tn_cap=896)
    qn, kn = _sb_prep(p, lw["gq_t"], lw["gk_t"])
    o_att = _sb_fwd(qn, kn, p)
    qkv, bb, gcb, glb = _dn_prep(p, lw["conv_w"], lw["a_row"], lw["dtb_row"])
    if next_shards is None:
        (o_dn, tinv, sall), gathered = _dn_fwd(qkv, bb, gcb, glb)
    else:
        (o_dn, tinv, sall), gathered = _dn_fwd(qkv, bb, gcb, glb, _WeightGather(next_shards), next_shards)
    o_sb, o_dnn = _gate(o_att, o_dn, p, lw["gn"])
    y, b_sb, b_dn = _branch(o_sb, o_dnn, lw["wb_sb"], lw["wb_dn"], p, D)
    x_next, out = _out_proj(x, y, lw["w_out"], gate)
    res = dict(x=x, h=h, p=p, qn=qn, kn=kn, o_att=o_att, qkv=qkv, bb=bb, gcb=gcb, glb=glb, o_dn=o_dn,
               tinv=tinv, sall=sall, o_sb=o_sb, o_dnn=o_dnn, y=y, b_sb=b_sb, b_dn=b_dn, out=out,
               shift=shift, scale=scale, gate=gate)
    return x_next, res, gathered


def _layer_bwd(dxn, res, lw, pending=None):
    p = res["p"]
    dout, db_sb, db_dn, dm, dgate = _out_bwd(dxn, res["out"], res["gate"], lw["w_out"], p, res["b_sb"], res["b_dn"])
    dw_out = _matmul("dw_out", res["y"], dout, "tn", _MXU_DTYPE)
    dwb_sb = _matmul("dwb_sb", res["o_sb"], db_sb, "tn", _MXU_DTYPE)
    dwb_dn = _matmul("dwb_dn", res["o_dnn"], db_dn, "tn", _MXU_DTYPE)
    do_att, dz_sb, do_dn, dz_dn, dgn = _gate_bwd(db_sb, db_dn, lw["wb_sb"], lw["wb_dn"], res["o_att"], res["o_dn"],
                                                  p, lw["gn"])
    D = dxn.shape[1]
    by_shard = lambda g: g.reshape(g.shape[0], 4, g.shape[1] // 4).transpose(1, 0, 2)
    send = [by_shard(dwb_sb), by_shard(dwb_dn), dw_out.reshape(4, D // 4, D)]
    if pending is None:
        (dqn, dkn, dv), received = _sb_bwd(res["qn"], res["kn"], p, do_att)
    else:
        (dqn, dkn, dv), received = _sb_bwd(res["qn"], res["kn"], p, do_att,
                                           _GradExchange(pending[1], pending[0]), pending[1],
                                           _GradExchange(send, pending[0] - 1), send)
        send = []
    dq_sb, dk_sb, dgq, dgk = _sb_prep_bwd(p, dqn, dkn, lw["gq_t"], lw["gk_t"])
    dqkv, dbb, dgb = _dn_bwd(res["qkv"], res["bb"], res["gcb"], res["glb"], res["tinv"], res["sall"], do_dn)
    dc, dp_ba, dal, ddt = _dn_prep_bwd_a(p, dqkv, dbb, dgb, lw["conv_w"], lw["a_row"], lw["dtb_row"])
    dp_dn, dconv = _dn_prep_bwd_b(p, dc, lw["conv_w"])
    dp = jnp.concatenate([dp_dn, dz_dn, dq_sb, dk_sb, dv.astype(_MXU_DTYPE), dz_sb, dm, dp_ba], axis=1)
    dw_cat = _matmul("dw_cat", res["h"], dp, "tn", _MXU_DTYPE, tm_cap=1024, tn_cap=896, tk_cap=512)
    send = [_shards_from_cat(dw_cat, D)] + send
    if pending is None:
        dh = _matmul("dh", dp, lw["w_cat"], "nt", F32, tm_cap=1024, tk_cap=896)
    else:
        dh, arrived = _matmul("dh", dp, lw["w_cat"], "nt", F32, tm_cap=1024, tk_cap=896,
                              exchange=_GradExchange(send, pending[0] - 1), ex_src=send, ex_prev=received[:1])
        received, send = arrived + list(received[1:]), []
    dx, dshift, dscale, dnorm_g = _norm_mod_bwd(res["x"], dh, dxn, lw["norm_g"], res["scale"])
    small = dict(dmod=jnp.concatenate([dshift, dscale, dgate], axis=1)[0], norm_g=dnorm_g[0],
                 sb_q_g=dgq.reshape(SB_HEADS, SB_HD).sum(0), sb_k_g=dgk.reshape(SB_HEADS, SB_HD).sum(0),
                 conv_w=dconv, dn_a_log=dal[0, DN_HEADS:2 * DN_HEADS], dn_dt_bias=ddt[0, DN_HEADS:2 * DN_HEADS],
                 dn_norm_g=dgn[0])
    return dx, small, send, received


def _cat_cols(w, D):
    return jnp.concatenate([w[:, 2048:4096], w[:, 0:2048], w[:, 4104:4104 + 2 * D], w[:, 4096:4104],
                            jnp.zeros((w.shape[0], LANES - 8), w.dtype)], axis=1)


def _shards_from_cat(g, D):
    n = (4104 + 2 * D) // 4
    segments = ((0, 2048, 2048), (2048, 4096, 0), (4096, 4104, 4096 + 2 * D), (4104, 4104 + 2 * D, 4096))

    def shard(lo, hi):
        cuts = [(c0 + max(lo, s0) - s0, c0 + min(hi, s1) - s0) for s0, s1, c0 in segments if max(lo, s0) < min(hi, s1)]
        return jnp.concatenate([g[:, a:b] for a, b in cuts], axis=1)

    return jnp.stack([shard(s * n, (s + 1) * n) for s in range(4)])


def _flat_pack(arrs, mult):
    flat = jnp.concatenate([a.reshape(-1) for a in arrs])
    n = flat.shape[0]
    pad = (-n) % mult
    if pad:
        flat = jnp.concatenate([flat, jnp.zeros((pad,), flat.dtype)])
    return flat.reshape(-1, LANES)


def _flat_unpack(flat, shapes):
    flat = flat.reshape(-1)
    out, off = [], 0
    for s in shapes:
        n = math.prod(s)
        out.append(flat[off:off + n].reshape(s))
        off += n
    return out


BIG = ("w_in", "w_branch_sb", "w_branch_dn", "w_out")
SMALL = ("ada_b", "norm_g", "sb_q_g", "sb_k_g", "conv_w", "dn_a_log", "dn_dt_bias", "dn_norm_g")


def kernel(x, c, ada_w, ada_b, norm_g, w_in, sb_q_g, sb_k_g, conv_w, dn_a_log, dn_dt_bias, dn_norm_g, w_branch_sb, w_branch_dn, w_out, loss_target, m_ada_w, m_ada_b, m_norm_g, m_w_in, m_sb_q_g, m_sb_k_g, m_conv_w, m_dn_a_log, m_dn_dt_bias, m_dn_norm_g, m_w_branch_sb, m_w_branch_dn, m_w_out, v_ada_w, v_ada_b, v_norm_g, v_w_in, v_sb_q_g, v_sb_k_g, v_conv_w, v_dn_a_log, v_dn_dt_bias, v_dn_norm_g, v_w_branch_sb, v_w_branch_dn, v_w_out):
    W = dict(ada_w=ada_w, ada_b=ada_b, norm_g=norm_g, w_in=w_in, sb_q_g=sb_q_g, sb_k_g=sb_k_g, conv_w=conv_w,
             dn_a_log=dn_a_log, dn_dt_bias=dn_dt_bias, dn_norm_g=dn_norm_g, w_branch_sb=w_branch_sb,
             w_branch_dn=w_branch_dn, w_out=w_out)
    M = dict(ada_w=m_ada_w, ada_b=m_ada_b, norm_g=m_norm_g, w_in=m_w_in, sb_q_g=m_sb_q_g, sb_k_g=m_sb_k_g,
             conv_w=m_conv_w, dn_a_log=m_dn_a_log, dn_dt_bias=m_dn_dt_bias, dn_norm_g=m_dn_norm_g,
             w_branch_sb=m_w_branch_sb, w_branch_dn=m_w_branch_dn, w_out=m_w_out)
    V = dict(ada_w=v_ada_w, ada_b=v_ada_b, norm_g=v_norm_g, w_in=v_w_in, sb_q_g=v_sb_q_g, sb_k_g=v_sb_k_g,
             conv_w=v_conv_w, dn_a_log=v_dn_a_log, dn_dt_bias=v_dn_dt_bias, dn_norm_g=v_dn_norm_g,
             w_branch_sb=v_w_branch_sb, w_branch_dn=v_w_branch_dn, w_out=v_w_out)
    L = ada_w.shape[0]
    S, D = x.shape[1], x.shape[2]
    ix, iy, ic = lax.axis_index("x"), lax.axis_index("y"), lax.axis_index("c")
    shard = 2 * ix + iy
    me = 2 * shard + ic
    n_ada = ada_w.shape[2]
    n_in = w_in.shape[2]
    n_conv = conv_w.shape[2]
    n_br = w_branch_sb.shape[2]
    n_out = w_out.shape[1]

    assert L == 2, "the owner of a layer's gradients is the core with the layer's number"
    shards = [W[n].astype(_MXU_DTYPE) for n in BIG]
    gathered0 = _ag_weights_first([a[0] for a in shards])

    g1 = _ag_small("ag_c_conv", _flat_pack([c, conv_w], LANES * 8))
    g1 = g1.reshape(8, -1)
    c_all = g1[:, :D]
    conv_parts = g1[:, D:D + L * CONV_K * n_conv].reshape(4, 2, L, CONV_K, n_conv)[:, 0]
    conv_full = jnp.concatenate([conv_parts[s] for s in range(4)], axis=2)
    ada_b_sh = lax.dynamic_slice_in_dim(ada_b, shard * n_ada, n_ada, axis=1)[:, None, :]
    mod_sh = _ada_fwd(c_all, ada_w, ada_b_sh)
    g2 = _ag_small("ag_mod", _flat_pack([mod_sh], LANES * 8)).reshape(8, -1)
    mod_parts = g2[:, :L * 8 * n_ada].reshape(4, 2, L, 8, n_ada)[:, 0]
    mod_all = jnp.concatenate([mod_parts[s] for s in range(4)], axis=2)
    mod = lax.dynamic_index_in_dim(mod_all, me, axis=1, keepdims=False)

    def layer_weights(l, gathered):
        g_in, g_bs, g_bd, g_out = gathered
        cat = lambda g, axis: jnp.concatenate([g[s] for s in range(4)], axis=axis)
        pad_lo = jnp.zeros((DN_HEADS,), F32)
        pad_hi = jnp.zeros((LANES - 2 * DN_HEADS,), F32)
        return dict(
            norm_g=norm_g[l][None, :], w_cat=_cat_cols(cat(g_in, 1), D),
            gq_t=jnp.tile(sb_q_g[l], SB_HEADS)[None, :], gk_t=jnp.tile(sb_k_g[l], SB_HEADS)[None, :],
            conv_w=conv_full[l],
            a_row=jnp.concatenate([pad_lo, dn_a_log[l], pad_hi])[None, :],
            dtb_row=jnp.concatenate([pad_lo, dn_dt_bias[l], pad_hi])[None, :],
            gn=dn_norm_g[l][None, :], wb_sb=cat(g_bs, 1), wb_dn=cat(g_bd, 1), w_out=cat(g_out, 0))

    mods = lambda l: (mod[l, None, 0:D], mod[l, None, D:2 * D], mod[l, None, 2 * D:3 * D])
    lws, ress = [None] * L, [None] * L
    lws[0] = layer_weights(0, gathered0)
    xs, ress[0], gathered1 = _layer_fwd(x[0], *mods(0), lws[0], [a[1] for a in shards])
    lws[1] = layer_weights(1, gathered1)
    xs, ress[1], _ = _layer_fwd(xs, *mods(1), lws[1])
    dxs, loss_row = _loss_head(xs, loss_target[0])
    loss = lax.psum(loss_row[0, 0], ("x", "y", "c"))
    smalls = [None] * L
    dxs, smalls[1], send1, _ = _layer_bwd(dxs, ress[1], lws[1])
    dxs, smalls[0], send0, got = _layer_bwd(dxs, ress[0], lws[0], (1, send1))
    grad_x = dxs[None]

    small_names = ("dmod",) + SMALL[1:]
    small_pack = _flat_pack([jnp.stack([smalls[l][n] for l in range(L)]) for n in small_names], LANES * 8)
    g3 = _ag_small("ag_small_grads", small_pack)
    R3 = small_pack.shape[0]
    g3 = g3.reshape(8, R3, LANES)
    small_sum = _sum_parts("sum_small", g3)
    small_shapes = [(L, 3 * D), (L, D), (L, SB_HD), (L, SB_HD), (L, CONV_K, 3 * DN_W), (L, DN_HEADS), (L, DN_HEADS),
                    (L, DN_HD)]
    sg = dict(zip(small_names, _flat_unpack(small_sum, small_shapes)))
    G = dict(ada_b=sg["dmod"], norm_g=sg["norm_g"], sb_q_g=sg["sb_q_g"], sb_k_g=sg["sb_k_g"],
             conv_w=lax.dynamic_slice_in_dim(sg["conv_w"], shard * n_conv, n_conv, axis=2),
             dn_a_log=sg["dn_a_log"], dn_dt_bias=sg["dn_dt_bias"], dn_norm_g=sg["dn_norm_g"])
    dmod_all = g3.reshape(8, -1)[:, :L * 3 * D].reshape(8, L, 3 * D)
    dmod_sh = lax.dynamic_slice_in_dim(dmod_all, shard * n_ada, n_ada, axis=2).transpose(1, 0, 2)
    G["ada_w"] = _ada_bwd(c_all.T, dmod_sh)

    assert not send0
    mine = [_sum_parts("sum_" + n, g) for n, g in zip(BIG, got)]
    for n, g in zip(BIG, _sibling_join(mine)):
        G[n] = g

    delta, new_m, new_v = {}, {}, {}
    for n in ("ada_w",) + BIG:
        delta[n], new_m[n], new_v[n] = _adamw("adamw_" + n, W[n], G[n], M[n], V[n])
    sm_shapes = [W[n].shape for n in SMALL]
    d, mo, vo = _adamw("adamw_small", *[_flat_pack([T[n] for n in SMALL], LANES * 8)[None] for T in (W, G, M, V)])
    for n, dd, mm, vv in zip(SMALL, _flat_unpack(d, sm_shapes), _flat_unpack(mo, sm_shapes),
                             _flat_unpack(vo, sm_shapes)):
        delta[n], new_m[n], new_v[n] = dd, mm, vv

    order = ("ada_w", "ada_b", "norm_g", "w_in", "sb_q_g", "sb_k_g", "conv_w", "dn_a_log", "dn_dt_bias", "dn_norm_g",
             "w_branch_sb", "w_branch_dn", "w_out")
    return (loss, grad_x, *[G[n] for n in order], *[delta[n] for n in order], *[new_m[n] for n in order],
            *[new_v[n] for n in order])
```

```python
import math

import jax
import jax.numpy as jnp
from jax import lax
from jax.experimental import pallas as pl
from jax.experimental.pallas import tpu as pltpu

F32 = jnp.float32
BF16 = jnp.bfloat16
_MXU_DTYPE = BF16
_VMEM_LIMIT = 48 * 1024 * 1024
LANES = 128

EPS = 1e-6
SB_HEADS, SB_HD, SB_W = 8, 64, 512
DN_HEADS, DN_HD, DN_W = 4, 128, 512
CONV_K = 4
CHUNK = 64
QB = 256
_SB_DEAD = 104.0
ADAM_LR, ADAM_B1, ADAM_B2, ADAM_EPS, ADAM_WD, ADAM_STEP = 0.001, 0.9, 0.999, 1e-08, 0.01, 10

C_DN_QKV, C_DN_Z, C_SB_Q, C_SB_K, C_SB_V, C_SB_Z, C_MG = 0, 1536, 2048, 2560, 3072, 3584, 4096

_NN = (((1,), (0,)), ((), ()))
_NT = (((1,), (1,)), ((), ()))
_TN = (((0,), (0,)), ((), ()))
_BNN = (((2,), (1,)), ((0,), (0,)))
_BNT = (((2,), (2,)), ((0,), (0,)))
_BTN = (((1,), (1,)), ((0,), (0,)))
MESH = pl.DeviceIdType.MESH


def _sds(shape, dtype):
    return jax.ShapeDtypeStruct(shape, dtype)


def _cp(n):
    return pltpu.CompilerParams(dimension_semantics=("arbitrary",) * n, vmem_limit_bytes=_VMEM_LIMIT)


def _rb(tm, w, cb=0):
    return pl.BlockSpec((tm, w), lambda i: (i, cb))


def _fs(shape):
    nd = len(shape)
    return pl.BlockSpec(shape, lambda i: (0,) * nd)


def _dg(a, b, dims):
    return lax.dot_general(a, b, dims, preferred_element_type=F32)


def _mm(a, b, dims=_NN):
    return _dg(a.astype(_MXU_DTYPE), b.astype(_MXU_DTYPE), dims)


def _split3(x):
    hi = x.astype(BF16)
    r = x - hi.astype(F32)
    mid = r.astype(BF16)
    lo = (r - mid.astype(F32)).astype(BF16)
    return hi, mid, lo


def _mm_xl(x, const, dims=_NN):
    cb = const.astype(BF16)
    hi, mid, lo = _split3(x)
    return _dg(hi, cb, dims) + _dg(mid, cb, dims) + _dg(lo, cb, dims)


def _mm_xl2(x, const, dims=_NN):
    cb = const.astype(BF16)
    hi = x.astype(BF16)
    lo = (x - hi.astype(F32)).astype(BF16)
    return _dg(hi, cb, dims) + _dg(lo, cb, dims)


def _mm_xr(const, x, dims=_NN):
    cb = const.astype(BF16)
    hi, mid, lo = _split3(x)
    return _dg(cb, hi, dims) + _dg(cb, mid, dims) + _dg(cb, lo, dims)


def _mm3(a, b, dims=_NN):
    ah, am, _ = _split3(a)
    bh, bm, _ = _split3(b)
    return _dg(ah, bh, dims) + (_dg(ah, bm, dims) + _dg(am, bh, dims))


def _sigmoid(z):
    return 1.0 / (1.0 + jnp.exp(-z))


def _silu(z):
    return z * _sigmoid(z)


def _dsilu(z):
    s = _sigmoid(z)
    return s * (1.0 + z * (1.0 - s))


def _softplus(z):
    return jnp.maximum(z, 0.0) + jnp.log(1.0 + jnp.exp(-jnp.abs(z)))


def _iota2(shape, dim):
    return lax.broadcasted_iota(jnp.int32, shape, dim)


def _pick(n, cap, mult):
    best = None
    for t in range(mult, min(n, cap) + 1, mult):
        if n % t == 0:
            best = t
    assert best is not None, (n, cap, mult)
    return best


def _matmul(name, a, b, form, out_dtype, tm_cap=512, tn_cap=1024, tk_cap=1024, exchange=None, ex_src=(), ex_prev=()):
    if form == "nn":
        (M, K), (_, N) = a.shape, b.shape
    elif form == "nt":
        (M, K), (N, _) = a.shape, b.shape
    else:
        (K, M), (_, N) = a.shape, b.shape
    tm = _pick(M, tm_cap, 128 if form == "tn" else 8)
    tn = _pick(N, tn_cap, 128)
    tk = _pick(K, tk_cap, 128)
    nk = K // tk
    dims = {"nn": _NN, "nt": _NT, "tn": _TN}[form]
    if form == "nn":
        a_spec = pl.BlockSpec((tm, tk), lambda i, j, k: (i, k))
        b_spec = pl.BlockSpec((tk, tn), lambda i, j, k: (k, j))
    elif form == "nt":
        a_spec = pl.BlockSpec((tm, tk), lambda i, j, k: (i, k))
        b_spec = pl.BlockSpec((tn, tk), lambda i, j, k: (j, k))
    else:
        a_spec = pl.BlockSpec((tk, tm), lambda i, j, k: (k, i))
        b_spec = pl.BlockSpec((tk, tn), lambda i, j, k: (k, j))

    grid = (M // tm, N // tn, nk)
    nx = len(ex_src)

    def body(*refs):
        if exchange is None:
            compute(*refs)
            return
        src, xout, sems = refs[2:2 + nx], refs[3 + 2 * nx:3 + 3 * nx], refs[4 + 3 * nx:]
        at = [pl.program_id(d) for d in range(3)]

        @pl.when(jnp.logical_and(jnp.logical_and(at[0] == 0, at[1] == 0), at[2] == 0))
        def _():
            exchange.start(src, xout, sems)

        compute(refs[0], refs[1], refs[2 + 2 * nx], refs[3 + 3 * nx])

        @pl.when(jnp.logical_and(jnp.logical_and(at[0] == grid[0] - 1, at[1] == grid[1] - 1), at[2] == nk - 1))
        def _():
            exchange.wait(src, xout, sems)

    def compute(a_ref, b_ref, o_ref, acc_ref):
        k = pl.program_id(2)

        @pl.when(k == 0)
        def _():
            acc_ref[...] = jnp.zeros_like(acc_ref)

        acc_ref[...] += _mm(a_ref[...], b_ref[...], dims)

        @pl.when(k == nk - 1)
        def _():
            o_ref[...] = acc_ref[...].astype(o_ref.dtype)

    hbm = pl.BlockSpec(memory_space=pl.ANY)
    outs = pl.pallas_call(
        body, name=name, grid=grid,
        in_specs=[a_spec, b_spec] + [hbm] * (2 * nx),
        out_specs=[pl.BlockSpec((tm, tn), lambda i, j, k: (i, j))] + [hbm] * nx,
        out_shape=[_sds((M, N), out_dtype)] + (exchange.out_shape if exchange else []),
        input_output_aliases={2 + nx + t: 1 + t for t in range(nx)},
        scratch_shapes=[pltpu.VMEM((tm, tn), F32)] + (exchange.scratch if exchange else []),
        compiler_params=_cp(3),
    )(a, b, *ex_src, *ex_prev)
    return outs[0] if exchange is None else (outs[0], list(outs[1:]))


def _norm_mod(x, g, scale, shift, tm=256):
    S, D = x.shape

    def body(x_ref, g_ref, sc_ref, sh_ref, h_ref):
        xv = x_ref[...]
        r = lax.rsqrt(jnp.mean(xv * xv, axis=1, keepdims=True) + EPS)
        h_ref[...] = ((xv * r * g_ref[...]) * (1.0 + sc_ref[...]) + sh_ref[...]).astype(h_ref.dtype)

    return pl.pallas_call(
        body, name="norm_mod", grid=(S // tm,),
        in_specs=[_rb(tm, D), _fs((1, D)), _fs((1, D)), _fs((1, D))],
        out_specs=_rb(tm, D), out_shape=_sds((S, D), _MXU_DTYPE), compiler_params=_cp(1),
    )(x, g, scale, shift)


def _norm_mod_bwd(x, dh, dxn, g, scale, tm=256):
    S, D = x.shape

    def body(x_ref, dh_ref, dxn_ref, g_ref, sc_ref, dx_ref, dsh_ref, dsc_ref, dg_ref):
        @pl.when(pl.program_id(0) == 0)
        def _():
            dsh_ref[...] = jnp.zeros_like(dsh_ref)
            dsc_ref[...] = jnp.zeros_like(dsc_ref)
            dg_ref[...] = jnp.zeros_like(dg_ref)

        xv, dhv, gv = x_ref[...], dh_ref[...], g_ref[...]
        r = lax.rsqrt(jnp.mean(xv * xv, axis=1, keepdims=True) + EPS)
        xh = xv * r
        one_sc = 1.0 + sc_ref[...]
        dsh_ref[...] += jnp.sum(dhv, axis=0, keepdims=True)
        dsc_ref[...] += jnp.sum(dhv * xh * gv, axis=0, keepdims=True)
        dg_ref[...] += jnp.sum(dhv * one_sc * xh, axis=0, keepdims=True)
        dxh = dhv * (gv * one_sc)
        dx_ref[...] = r * (dxh - xh * jnp.mean(dxh * xh, axis=1, keepdims=True)) + dxn_ref[...]

    return pl.pallas_call(
        body, name="norm_mod_bwd", grid=(S // tm,),
        in_specs=[_rb(tm, D), _rb(tm, D), _rb(tm, D), _fs((1, D)), _fs((1, D))],
        out_specs=[_rb(tm, D), _fs((1, D)), _fs((1, D)), _fs((1, D))],
        out_shape=[_sds((S, D), F32)] + [_sds((1, D), F32)] * 3, compiler_params=_cp(1),
    )(x, dh, dxn, g, scale)


def _head_sum_matrix():
    r = jnp.arange(SB_W)
    return (r[:, None] // SB_HD == r[None, :] // SB_HD).astype(BF16)


def _sb_prep(p, gq_t, gk_t, tm=256):
    S = p.shape[0]
    bd = _head_sum_matrix()

    def body(q_ref, k_ref, gq_ref, gk_ref, bd_ref, qn_ref, kn_ref):
        for src, g_ref, dst in ((q_ref, gq_ref, qn_ref), (k_ref, gk_ref, kn_ref)):
            v = src[...]
            ms = _mm_xl(v * v, bd_ref[...]) * (1.0 / SB_HD)
            dst[...] = (v * lax.rsqrt(ms + EPS) * g_ref[...]).astype(dst.dtype)

    return pl.pallas_call(
        body, name="sb_prep", grid=(S // tm,),
        in_specs=[_rb(tm, SB_W, C_SB_Q // SB_W), _rb(tm, SB_W, C_SB_K // SB_W),
                  _fs((1, SB_W)), _fs((1, SB_W)), _fs((SB_W, SB_W))],
        out_specs=[_rb(tm, SB_W), _rb(tm, SB_W)],
        out_shape=[_sds((S, SB_W), _MXU_DTYPE)] * 2, compiler_params=_cp(1),
    )(p, p, gq_t, gk_t, bd)


def _sb_prep_bwd(p, dqn, dkn, gq_t, gk_t, tm=256):
    S = p.shape[0]
    bd = _head_sum_matrix()

    def body(q_ref, k_ref, dqn_ref, dkn_ref, gq_ref, gk_ref, bd_ref, dq_ref, dk_ref, dgq_ref, dgk_ref):
        @pl.when(pl.program_id(0) == 0)
        def _():
            dgq_ref[...] = jnp.zeros_like(dgq_ref)
            dgk_ref[...] = jnp.zeros_like(dgk_ref)

        for src, dn_ref, g_ref, dst, dg_ref in ((q_ref, dqn_ref, gq_ref, dq_ref, dgq_ref),
                                                (k_ref, dkn_ref, gk_ref, dk_ref, dgk_ref)):
            v, dn = src[...], dn_ref[...]
            r = lax.rsqrt(_mm_xl(v * v, bd_ref[...]) * (1.0 / SB_HD) + EPS)
            vh = v * r
            dg_ref[...] += jnp.sum(dn * vh, axis=0, keepdims=True)
            dvh = dn * g_ref[...]
            m = _mm_xl(dvh * vh, bd_ref[...]) * (1.0 / SB_HD)
            dst[...] = (r * (dvh - vh * m)).astype(dst.dtype)

    return pl.pallas_call(
        body, name="sb_prep_bwd", grid=(S // tm,),
        in_specs=[_rb(tm, SB_W, C_SB_Q // SB_W), _rb(tm, SB_W, C_SB_K // SB_W), _rb(tm, SB_W), _rb(tm, SB_W),
                  _fs((1, SB_W)), _fs((1, SB_W)), _fs((SB_W, SB_W))],
        out_specs=[_rb(tm, SB_W), _rb(tm, SB_W), _fs((1, SB_W)), _fs((1, SB_W))],
        out_shape=[_sds((S, SB_W), _MXU_DTYPE)] * 2 + [_sds((1, SB_W), F32)] * 2, compiler_params=_cp(1),
    )(p, p, dqn, dkn, gq_t, gk_t, bd)


def _sb_consts():
    r, c = _iota2((QB, QB), 0), _iota2((QB, QB), 1)
    lane = _iota2((1, LANES), 1)
    return r, c, lane


def _sb_fwd(qn, kn, p):
    S = qn.shape[0]
    scale = 1.0 / math.sqrt(SB_HD)

    def body(q_ref, k_ref, v_ref, o_ref, tot_ref, nb_ref):
        i = pl.program_id(1)
        r, c, lane = _sb_consts()
        u_gt = (r > c).astype(BF16)
        strict = jnp.concatenate([c < r, c < r], axis=0)
        q = q_ref[...]
        mask0 = (lane // SB_HD) == 0
        zero = jnp.zeros_like(q)
        qh = jnp.concatenate([jnp.where(mask0, q, zero), jnp.where(mask0, zero, q)], axis=0)

        def block(off, carry, diagonal):
            o, run = carry
            kj = k_ref[pl.ds(off, QB), :]
            vj = v_ref[pl.ds(off, QB), :].astype(_MXU_DTYPE)
            z = _mm(qh, kj, _NT) * scale
            sp = _softplus(z)
            sp_m = jnp.where(strict, sp, 0.0) if diagonal else sp
            later = _mm_xl2(sp_m, u_gt)
            w = jnp.exp((z - sp) - later - run)
            if diagonal:
                w = jnp.where(strict, w, 0.0)
            return o + _mm(w, vj), run + jnp.sum(sp_m, axis=1, keepdims=True)

        init = (jnp.zeros((2 * QB, LANES), F32), jnp.zeros((2 * QB, 1), F32))
        carry = block(pl.multiple_of(i * QB, QB), init, True)
        st = lax.while_loop(
            lambda st: jnp.logical_and(st[0] <= i, jnp.min(st[2]) < _SB_DEAD),
            lambda st: (st[0] + 1,) + block(pl.multiple_of((i - st[0]) * QB, QB), st[1:], False),
            (jnp.int32(1),) + carry)
        o_ref[...] = jnp.where(mask0, st[1][:QB], st[1][QB:])
        tot_ref[...] = jnp.where(mask0, st[2][:QB], st[2][QB:])
        nb_ref[...] = jnp.zeros((8, LANES), F32) + st[0].astype(F32)

    blk = pl.BlockSpec((QB, LANES), lambda hp, i: (i, hp))
    return pl.pallas_call(
        body, name="sb_fwd", grid=(SB_W // LANES, S // QB),
        in_specs=[blk, pl.BlockSpec((S, LANES), lambda hp, i: (0, hp)),
                  pl.BlockSpec((S, LANES), lambda hp, i: (0, C_SB_V // LANES + hp))],
        out_specs=[blk, blk, pl.BlockSpec((8, LANES), lambda hp, i: (i, hp))],
        out_shape=[_sds((S, SB_W), F32), _sds((S, SB_W), F32), _sds((8 * S // QB, SB_W), F32)],
        compiler_params=_cp(2),
    )(qn, kn, p)


def _sb_bwd(qn, kn, p, do, tot, nblocks, exchange=None, ex_src=(), early=None, early_src=()):
    S = qn.shape[0]
    scale = 1.0 / math.sqrt(SB_HD)
    grid = (SB_W // LANES, S // QB)
    nx, ne = len(ex_src), len(early_src)
    NI = 6

    def body(*refs):
        if exchange is None:
            compute(*refs)
            return
        src, src2 = refs[NI:NI + nx], refs[NI + nx:NI + nx + ne]
        o0 = NI + nx + ne
        xout, sems, sems2 = refs[o0 + 3:o0 + 3 + nx], refs[o0 + 3 + nx:o0 + 6 + nx], refs[o0 + 6 + nx:]
        hp, i = pl.program_id(0), pl.program_id(1)

        @pl.when(jnp.logical_and(hp == 0, i == 0))
        def _():
            exchange.start(src, xout, sems)
            if early is not None:
                early.start(src2, xout[nx - ne:], sems2)

        compute(*refs[:NI], *refs[o0:o0 + 3])

        @pl.when(jnp.logical_and(hp == grid[0] - 1, i == grid[1] - 1))
        def _():
            exchange.wait(src, xout, sems)
            if early is not None:
                early.wait(src2, xout[nx - ne:], sems2)

    def compute(q_ref, k_ref, v_ref, do_ref, tot_ref, nb_ref, dq_ref, dk_ref, dv_ref):
        i = pl.program_id(1)

        @pl.when(i == 0)
        def _():
            dk_ref[...] = jnp.zeros_like(dk_ref)
            dv_ref[...] = jnp.zeros_like(dv_ref)

        r, c, lane = _sb_consts()
        u_le = (r <= c).astype(BF16)
        u_lt = (r < c).astype(BF16)
        strict = jnp.concatenate([c < r, c < r], axis=0)
        q = q_ref[...]
        do = do_ref[...].astype(_MXU_DTYPE)
        mask0 = (lane // SB_HD) == 0
        zero, zero_do = jnp.zeros_like(q), jnp.zeros_like(do)
        qh = jnp.concatenate([jnp.where(mask0, q, zero), jnp.where(mask0, zero, q)], axis=0)
        doh = jnp.concatenate([jnp.where(mask0, do, zero_do), jnp.where(mask0, zero_do, do)], axis=0)

        tot_pair = tot_ref[...]
        tot = jnp.concatenate([jnp.max(jnp.where(mask0, tot_pair, 0.0), axis=1, keepdims=True),
                               jnp.max(jnp.where(mask0, 0.0, tot_pair), axis=1, keepdims=True)], axis=0)
        nb = jnp.clip(jnp.max(nb_ref[...]).astype(jnp.int32), 1, i + 1)
        first = i + 1 - nb

        def block(off, carry, diagonal):
            dq, pre_sp, pre_e = carry
            kj = k_ref[pl.ds(off, QB), :]
            vj = v_ref[pl.ds(off, QB), :].astype(_MXU_DTYPE)
            z = _mm(qh, kj, _NT) * scale
            sp = _softplus(z)
            a = z - sp
            sp_m = jnp.where(strict, sp, 0.0) if diagonal else sp
            incl = _mm_xl2(sp_m, u_le)
            w = jnp.exp(a - ((tot - pre_sp) - incl))
            if diagonal:
                w = jnp.where(strict, w, 0.0)
            e = w * _mm(doh, vj, _NT)
            db = pre_e + _mm_xl2(e, u_lt)
            dz = (e - jnp.exp(a) * (e + db)) * scale
            if diagonal:
                dz = jnp.where(strict, dz, 0.0)
            dk_ref[pl.ds(off, QB), :] += _mm(dz, qh, _TN)
            dv_ref[pl.ds(off, QB), :] += _mm(w, doh, _TN)
            return (dq + _mm(dz, kj), pre_sp + jnp.sum(sp_m, axis=1, keepdims=True),
                    pre_e + jnp.sum(e, axis=1, keepdims=True))

        zero_col = jnp.zeros((2 * QB, 1), F32)
        init = (jnp.zeros((2 * QB, LANES), F32), zero_col, zero_col)
        carry = lax.fori_loop(first, i, lambda j, cr: block(pl.multiple_of(j * QB, QB), cr, False), init)
        carry = block(pl.multiple_of(i * QB, QB), carry, True)
        dq_ref[...] = jnp.where(mask0, carry[0][:QB], carry[0][QB:])

    blk = pl.BlockSpec((QB, LANES), lambda hp, i: (i, hp))
    full = pl.BlockSpec((S, LANES), lambda hp, i: (0, hp))
    hbm = pl.BlockSpec(memory_space=pl.ANY)
    outs = pl.pallas_call(
        body, name="sb_bwd", grid=grid,
        in_specs=[blk, full, pl.BlockSpec((S, LANES), lambda hp, i: (0, C_SB_V // LANES + hp)), blk, blk,
                  pl.BlockSpec((8, LANES), lambda hp, i: (i, hp))] + [hbm] * (nx + ne),
        out_specs=[blk, full, full] + [hbm] * nx,
        out_shape=[_sds((S, SB_W), F32)] * 3 + (exchange.out_shape if exchange else []),
        scratch_shapes=(exchange.scratch if exchange else []) + (early.scratch if early else []),
        compiler_params=_cp(2),
    )(qn, kn, p, do, tot, nblocks, *ex_src, *early_src)
    return outs[:3], outs[3:]


def _dn_prep(p, conv_w, a_row, dtb_row, tm=256):
    S = p.shape[0]
    W3 = 3 * DN_W
    nhalo = tm // 8

    def body(x_ref, halo_ref, w_ref, ba_ref, a_ref, dtb_ref, qkv_ref, bb_ref, gc_ref, gl_ref):
        i = pl.program_id(0)
        halo = jnp.where(i > 0, halo_ref[...], 0.0)
        xf = jnp.concatenate([halo, x_ref[...]], axis=0)
        acc = jnp.zeros((tm, W3), F32)
        for k in range(CONV_K):
            sh = CONV_K - 1 - k
            xs = xf if sh == 0 else pltpu.roll(xf, sh, 0)
            acc = acc + xs[8:, :] * w_ref[k:k + 1, :]
        s = _silu(acc)
        for gi in range(2 * DN_HEADS):
            sl = slice(gi * LANES, (gi + 1) * LANES)
            sg = s[:, sl]
            rinv = lax.rsqrt(jnp.sum(sg * sg, axis=1, keepdims=True) + EPS)
            qkv_ref[:, sl] = sg * rinv * (DN_HD ** -0.5 if gi < DN_HEADS else 1.0)
        qkv_ref[:, 2 * DN_W:] = s[:, 2 * DN_W:]

        ba = ba_ref[...]
        beta = _sigmoid(ba)
        g = -jnp.exp(a_ref[...]) * _softplus(ba + dtb_ref[...])
        lr, lc = _iota2((LANES, DN_W), 0), _iota2((LANES, DN_W), 1)
        sel_b = (lr == lc // LANES).astype(BF16)
        sel_g = (lr == lc // LANES + DN_HEADS).astype(BF16)
        bb_ref[...] = _mm_xl(beta, sel_b)
        graw = _mm_xl(g, sel_g)
        rr, cc = _iota2((tm, tm), 0), _iota2((tm, tm), 1)
        tri = jnp.logical_and(rr >= cc, rr // CHUNK == cc // CHUNK).astype(BF16)
        gc = _mm_xr(tri, graw)
        last = (cc == (rr // CHUNK) * CHUNK + (CHUNK - 1)).astype(BF16)
        gc_ref[...] = gc
        gl_ref[...] = _mm_xr(last, gc)

    return pl.pallas_call(
        body, name="dn_prep", grid=(S // tm,),
        in_specs=[_rb(tm, W3, 0), pl.BlockSpec((8, W3), lambda i: (jnp.maximum(i * nhalo - 1, 0), 0)),
                  _fs((CONV_K, W3)), _rb(tm, LANES, (p.shape[1] - LANES) // LANES),
                  _fs((1, LANES)), _fs((1, LANES))],
        out_specs=[_rb(tm, W3), _rb(tm, DN_W), _rb(tm, DN_W), _rb(tm, DN_W)],
        out_shape=[_sds((S, W3), F32)] + [_sds((S, DN_W), F32)] * 3, compiler_params=_cp(1),
    )(p, p, conv_w, p, a_row, dtb_row)


def _heads(ref, base=0):
    return jnp.stack([ref[:, base + h * LANES:base + (h + 1) * LANES] for h in range(DN_HEADS)])


def _per_head(const):
    return jnp.broadcast_to(const[None], (DN_HEADS,) + const.shape)


def _dn_chunk_terms(q, k, v, beta, gc, gl):
    r, c = _iota2((CHUNK, CHUNK), 0), _iota2((CHUNK, CHUNK), 1)
    tril, strict = r >= c, r > c
    gcol = _mm_xl(gc, _per_head(jnp.full((LANES, CHUNK), 1.0 / LANES, F32)), _BNN)
    grow = _mm_xr(_per_head(jnp.full((CHUNK, LANES), 1.0 / LANES, F32)), gc, _BNT)
    dec = jnp.where(tril, jnp.exp(jnp.where(tril, gcol - grow, 0.0)), 0.0)
    gam = jnp.exp(gc)
    dlt = jnp.exp(gl - gc)
    kb, vb = k * beta, v * beta
    pm = _mm(kb, k, _BNT)
    qk = _mm(q, k, _BNT)
    m = jnp.where(strict, pm * dec, 0.0)
    a = jnp.where(tril, qk * dec, 0.0)
    return dict(tril=tril, strict=strict, dec=dec, gam=gam, dlt=dlt, kb=kb, vb=vb, m=m, a=a)


def _dn_fwd(qkv, bb, gcb, glb, gather=None, g_src=()):
    S = qkv.shape[0]
    N = S // CHUNK
    nx = len(g_src)

    def body(*refs):
        if gather is None:
            compute(*refs)
            return
        src, gout, sems = refs[4:4 + nx], refs[7 + nx:7 + 2 * nx], refs[8 + 2 * nx:]

        @pl.when(pl.program_id(0) == 0)
        def _():
            gather.start(src, gout, sems)

        compute(*refs[:4], *refs[4 + nx:7 + nx], refs[7 + 2 * nx])

        @pl.when(pl.program_id(0) == N - 1)
        def _():
            gather.wait(src, gout, sems)

    def compute(qkv_ref, bb_ref, gc_ref, gl_ref, o_ref, t_ref, sall_ref, s_scr):
        @pl.when(pl.program_id(0) == 0)
        def _():
            s_scr[...] = jnp.zeros_like(s_scr)

        r, c = _iota2((CHUNK, CHUNK), 0), _iota2((CHUNK, CHUNK), 1)
        eye = (r == c).astype(F32)
        q, k, v = _heads(qkv_ref), _heads(qkv_ref, DN_W), _heads(qkv_ref, 2 * DN_W)
        beta, gc, gl = _heads(bb_ref), _heads(gc_ref), _heads(gl_ref)
        s_prev = s_scr[...]
        sall_ref[0] = s_prev.astype(sall_ref.dtype)
        s0 = s_prev.astype(sall_ref.dtype).astype(F32)
        t = _dn_chunk_terms(q, k, v, beta, gc, gl)
        pw = -t["m"]
        tinv = eye + pw
        for _ in range(5):
            pw = _mm3(pw, pw, _BNN)
            tinv = tinv + _mm3(tinv, pw, _BNN)
        t_ref[...] = tinv
        u = _mm3(tinv, t["vb"], _BNN)
        w = _mm3(tinv, t["kb"] * t["gam"], _BNN)
        vn = u - _mm(w, s0, _BNN)
        o = _mm(q * t["gam"], s0, _BNN) + _mm(t["a"], vn, _BNN)
        for h in range(DN_HEADS):
            o_ref[:, h * LANES:(h + 1) * LANES] = o[h]
        egl = jnp.exp(jnp.concatenate([gl, gl], axis=1))
        s_scr[...] = s_prev * egl + _mm(k * t["dlt"], vn, _BTN)

    hbm = pl.BlockSpec(memory_space=pl.ANY)
    outs = pl.pallas_call(
        body, name="dn_fwd", grid=(N,),
        in_specs=[_rb(CHUNK, 3 * DN_W), _rb(CHUNK, DN_W), _rb(CHUNK, DN_W), _rb(CHUNK, DN_W)] + [hbm] * nx,
        out_specs=[_rb(CHUNK, DN_W), pl.BlockSpec((DN_HEADS, CHUNK, CHUNK), lambda n: (0, n, 0)),
                   pl.BlockSpec((1, DN_HEADS, DN_HD, DN_HD), lambda n: (n, 0, 0, 0))] + [hbm] * nx,
        out_shape=[_sds((S, DN_W), F32), _sds((DN_HEADS, S, CHUNK), F32),
                   _sds((N, DN_HEADS, DN_HD, DN_HD), _MXU_DTYPE)] + (gather.out_shape if gather else []),
        scratch_shapes=[pltpu.VMEM((DN_HEADS, DN_HD, DN_HD), F32)] + (gather.scratch if gather else []),
        compiler_params=_cp(1),
    )(qkv, bb, gcb, glb, *g_src)
    return outs[:3], outs[3:]


def _dn_bwd(qkv, bb, gcb, glb, tinv_all, sall, do):
    S = qkv.shape[0]
    N = S // CHUNK

    def body(qkv_ref, bb_ref, gc_ref, gl_ref, t_ref, sall_ref, do_ref, dqkv_ref, dbb_ref, dg_ref, ds_scr):
        @pl.when(pl.program_id(0) == 0)
        def _():
            ds_scr[...] = jnp.zeros_like(ds_scr)

        r, c = _iota2((CHUNK, CHUNK), 0), _iota2((CHUNK, CHUNK), 1)
        eye = (r == c).astype(F32)
        u_ge = (c >= r).astype(F32)
        last_row = _iota2((CHUNK, LANES), 0) == CHUNK - 1
        eye_h, u_ge_h = _per_head(eye), _per_head(u_ge)
        q, k, v = _heads(qkv_ref), _heads(qkv_ref, DN_W), _heads(qkv_ref, 2 * DN_W)
        beta, gc, gl = _heads(bb_ref), _heads(gc_ref), _heads(gl_ref)
        tinv = t_ref[...]
        s0 = sall_ref[0].astype(F32)
        do = _heads(do_ref)
        ds1 = ds_scr[...]
        t = _dn_chunk_terms(q, k, v, beta, gc, gl)
        gam, dlt, kb, vb, dec = t["gam"], t["dlt"], t["kb"], t["vb"], t["dec"]
        kbg = kb * gam
        u = _mm3(tinv, vb, _BNN)
        w = _mm3(tinv, kbg, _BNN)
        vn = u - _mm(w, s0, _BNN)
        qg, kd = q * gam, k * dlt
        egl = jnp.exp(gl)
        egl2 = jnp.concatenate([egl, egl], axis=1)

        dvn = _mm(t["a"], do, _BTN) + _mm(kd, ds1, _BNN)
        da = jnp.where(t["tril"], _mm(do, vn, _BNT), 0.0)
        dqg = _mm(do, s0, _BNT)
        dkd = _mm(vn, ds1, _BNT)
        dw = -_mm(dvn, s0, _BNT)
        ds_scr[...] = _mm(qg, do, _BTN) + egl2 * ds1 - _mm(w, dvn, _BTN)
        tt = _mm_xr(eye_h, tinv, _BNT)
        dvb = _mm3(tt, dvn, _BNN)
        dkbg = _mm3(tt, dw, _BNN)
        dm = -jnp.where(t["strict"], _mm(dvb, u, _BNT) + _mm(dkbg, w, _BNT), 0.0)
        dpm = dm * dec
        dqk = da * dec
        dkb = dkbg * gam + _mm(dpm, k, _BNN)
        dk = dkd * dlt + _mm(dpm, kb, _BTN) + _mm(dqk, q, _BTN) + dkb * beta
        dq = dqg * gam + _mm(dqk, k, _BNN)
        dv = dvb * beta
        dbeta = jnp.sum(dkb * k, axis=2, keepdims=True) + jnp.sum(dvb * v, axis=2, keepdims=True)
        dgam = jnp.sum(dqg * q, axis=2, keepdims=True) + jnp.sum(dkbg * kb, axis=2, keepdims=True)
        ddlt = jnp.sum(dkd * k, axis=2, keepdims=True)
        xm = dm * t["m"] + da * t["a"]
        xt = _mm_xr(eye_h, xm, _BNT)
        dgc = (dgam * gam - ddlt * dlt + jnp.sum(xm, axis=2, keepdims=True) - jnp.sum(xt, axis=2, keepdims=True))
        dgl = jnp.sum(ddlt * dlt, axis=1, keepdims=True) + jnp.sum(
            jnp.sum(ds1 * s0, axis=2, keepdims=True), axis=1, keepdims=True) * jnp.max(egl, axis=1, keepdims=True)
        dgc = dgc + jnp.where(last_row, dgl, 0.0)
        dg = _mm_xr(u_ge_h, dgc, _BNN)
        for h in range(DN_HEADS):
            sl = slice(h * LANES, (h + 1) * LANES)
            dqkv_ref[:, sl] = dq[h]
            dqkv_ref[:, DN_W + h * LANES:DN_W + (h + 1) * LANES] = dk[h]
            dqkv_ref[:, 2 * DN_W + h * LANES:2 * DN_W + (h + 1) * LANES] = dv[h]
            dbb_ref[:, sl] = jnp.broadcast_to(dbeta[h], (CHUNK, LANES))
            dg_ref[:, sl] = dg[h]

    rev = lambda w: pl.BlockSpec((CHUNK, w), lambda n: (N - 1 - n, 0))
    return pl.pallas_call(
        body, name="dn_bwd", grid=(N,),
        in_specs=[rev(3 * DN_W), rev(DN_W), rev(DN_W), rev(DN_W),
                  pl.BlockSpec((DN_HEADS, CHUNK, CHUNK), lambda n: (0, N - 1 - n, 0)),
                  pl.BlockSpec((1, DN_HEADS, DN_HD, DN_HD), lambda n: (N - 1 - n, 0, 0, 0)), rev(DN_W)],
        out_specs=[rev(3 * DN_W), rev(DN_W), rev(DN_W)],
        out_shape=[_sds((S, 3 * DN_W), F32), _sds((S, DN_W), F32), _sds((S, DN_W), F32)],
        scratch_shapes=[pltpu.VMEM((DN_HEADS, DN_HD, DN_HD), F32)],
        compiler_params=_cp(1),
    )(qkv, bb, gcb, glb, tinv_all, sall, do)


def _dn_prep_bwd_a(p, dqkv, dbb, dgb, conv_w, a_row, dtb_row, tm=256):
    S, PC = p.shape
    W3 = 3 * DN_W
    nhalo = tm // 8

    def body(x_ref, halo_ref, w_ref, ba_ref, a_ref, dtb_ref, dqkv_ref, dbb_ref, dgb_ref,
             dc_ref, dba_ref, dal_ref, ddt_ref):
        i = pl.program_id(0)

        @pl.when(i == 0)
        def _():
            dal_ref[...] = jnp.zeros_like(dal_ref)
            ddt_ref[...] = jnp.zeros_like(ddt_ref)

        halo = jnp.where(i > 0, halo_ref[...], 0.0)
        xf = jnp.concatenate([halo, x_ref[...]], axis=0)
        acc = jnp.zeros((tm, W3), F32)
        for k in range(CONV_K):
            sh = CONV_K - 1 - k
            xs = xf if sh == 0 else pltpu.roll(xf, sh, 0)
            acc = acc + xs[8:, :] * w_ref[k:k + 1, :]
        s = _silu(acc)
        ds_act = _dsilu(acc)
        for gi in range(2 * DN_HEADS):
            sl = slice(gi * LANES, (gi + 1) * LANES)
            sg = s[:, sl]
            rinv = lax.rsqrt(jnp.sum(sg * sg, axis=1, keepdims=True) + EPS)
            nh = sg * rinv
            dn = dqkv_ref[:, sl] * (DN_HD ** -0.5 if gi < DN_HEADS else 1.0)
            dsg = rinv * (dn - nh * jnp.sum(dn * nh, axis=1, keepdims=True))
            dc_ref[:, sl] = dsg * ds_act[:, sl]
        dc_ref[:, 2 * DN_W:] = dqkv_ref[:, 2 * DN_W:] * ds_act[:, 2 * DN_W:]

        ba = ba_ref[...]
        beta = _sigmoid(ba)
        ea = jnp.exp(a_ref[...])
        pre = ba + dtb_ref[...]
        g = -ea * _softplus(pre)
        lr, lc = _iota2((DN_W, LANES), 0), _iota2((DN_W, LANES), 1)
        pick_b = jnp.where(lc == lr // LANES, 1.0 / LANES, 0.0)
        pick_g = jnp.where(lc == lr // LANES + DN_HEADS, 1.0 / LANES, 0.0)
        dbeta = _mm_xl(dbb_ref[...], pick_b)
        dg = _mm_xl(dgb_ref[...], pick_g)
        lane = _iota2((1, LANES), 1)
        da = dg * (-ea) * _sigmoid(pre)
        dba_ref[...] = jnp.where(lane < DN_HEADS, dbeta * beta * (1.0 - beta),
                                 jnp.where(lane < 2 * DN_HEADS, da, 0.0)).astype(dba_ref.dtype)
        dal_ref[...] += jnp.sum(dg * g, axis=0, keepdims=True)
        ddt_ref[...] += jnp.sum(da, axis=0, keepdims=True)

    return pl.pallas_call(
        body, name="dn_prep_bwd_a", grid=(S // tm,),
        in_specs=[_rb(tm, W3, 0), pl.BlockSpec((8, W3), lambda i: (jnp.maximum(i * nhalo - 1, 0), 0)),
                  _fs((CONV_K, W3)), _rb(tm, LANES, (PC - LANES) // LANES), _fs((1, LANES)), _fs((1, LANES)),
                  _rb(tm, W3), _rb(tm, DN_W), _rb(tm, DN_W)],
        out_specs=[_rb(tm, W3), _rb(tm, LANES), _fs((1, LANES)), _fs((1, LANES))],
        out_shape=[_sds((S, W3), F32), _sds((S, LANES), _MXU_DTYPE), _sds((1, LANES), F32), _sds((1, LANES), F32)],
        compiler_params=_cp(1),
    )(p, p, conv_w, p, a_row, dtb_row, dqkv, dbb, dgb)


def _dn_prep_bwd_b(p, dc, conv_w, tm=256):
    S = p.shape[0]
    W3 = 3 * DN_W
    nhalo = tm // 8
    nblk = S // tm

    def body(x_ref, xh_ref, dc_ref, dch_ref, w_ref, dx_ref, dw_ref):
        i = pl.program_id(0)

        @pl.when(i == 0)
        def _():
            dw_ref[...] = jnp.zeros_like(dw_ref)

        dcv = dc_ref[...]
        xf = jnp.concatenate([jnp.where(i > 0, xh_ref[...], 0.0), x_ref[...]], axis=0)
        df = jnp.concatenate([dcv, jnp.where(i < nblk - 1, dch_ref[...], 0.0)], axis=0)
        acc = jnp.zeros((tm, W3), F32)
        for k in range(CONV_K):
            sh = CONV_K - 1 - k
            xs = xf if sh == 0 else pltpu.roll(xf, sh, 0)
            dw_ref[k:k + 1, :] += jnp.sum(dcv * xs[8:, :], axis=0, keepdims=True)
            ds = df if sh == 0 else pltpu.roll(df, tm + 8 - sh, 0)
            acc = acc + ds[:tm, :] * w_ref[k:k + 1, :]
        dx_ref[...] = acc.astype(dx_ref.dtype)

    return pl.pallas_call(
        body, name="dn_prep_bwd_b", grid=(nblk,),
        in_specs=[_rb(tm, W3, 0), pl.BlockSpec((8, W3), lambda i: (jnp.maximum(i * nhalo - 1, 0), 0)),
                  _rb(tm, W3), pl.BlockSpec((8, W3), lambda i: (jnp.minimum((i + 1) * nhalo, S // 8 - 1), 0)),
                  _fs((CONV_K, W3))],
        out_specs=[_rb(tm, W3), _fs((CONV_K, W3))],
        out_shape=[_sds((S, W3), _MXU_DTYPE), _sds((CONV_K, W3), F32)], compiler_params=_cp(1),
    )(p, p, dc, dc, conv_w)


def _gate(o_att, o_dn, p, gn, tm=256):
    S = p.shape[0]

    def body(oa_ref, zs_ref, od_ref, zd_ref, gn_ref, osb_ref, odn_ref):
        osb_ref[...] = (oa_ref[...] * _silu(zs_ref[...])).astype(osb_ref.dtype)
        for h in range(DN_HEADS):
            sl = slice(h * LANES, (h + 1) * LANES)
            o = od_ref[:, sl]
            r = lax.rsqrt(jnp.mean(o * o, axis=1, keepdims=True) + EPS)
            odn_ref[:, sl] = (o * r * gn_ref[...] * _silu(zd_ref[:, sl])).astype(odn_ref.dtype)

    return pl.pallas_call(
        body, name="gate", grid=(S // tm,),
        in_specs=[_rb(tm, SB_W), _rb(tm, SB_W, C_SB_Z // SB_W), _rb(tm, DN_W), _rb(tm, DN_W, C_DN_Z // DN_W),
                  _fs((1, LANES))],
        out_specs=[_rb(tm, SB_W), _rb(tm, DN_W)],
        out_shape=[_sds((S, SB_W), _MXU_DTYPE), _sds((S, DN_W), _MXU_DTYPE)], compiler_params=_cp(1),
    )(o_att, p, o_dn, p, gn)


def _gate_bwd(db_sb, db_dn, wb_sb, wb_dn, o_att, o_dn, p, gn, tm=256):
    S = p.shape[0]
    D = db_sb.shape[1]

    def body(dbs_ref, dbd_ref, ws_ref, wd_ref, oa_ref, zs_ref, od_ref, zd_ref, gn_ref,
             doa_ref, dzs_ref, dod_ref, dzd_ref, dgn_ref):
        @pl.when(pl.program_id(0) == 0)
        def _():
            dgn_ref[...] = jnp.zeros_like(dgn_ref)

        do_sb = _mm(dbs_ref[...], ws_ref[...], _NT)
        zs = zs_ref[...]
        doa_ref[...] = do_sb * _silu(zs)
        dzs_ref[...] = (do_sb * oa_ref[...] * _dsilu(zs)).astype(dzs_ref.dtype)
        do_dnn = _mm(dbd_ref[...], wd_ref[...], _NT)
        gnv = gn_ref[...]
        for h in range(DN_HEADS):
            sl = slice(h * LANES, (h + 1) * LANES)
            o, z, dout = od_ref[:, sl], zd_ref[:, sl], do_dnn[:, sl]
            r = lax.rsqrt(jnp.mean(o * o, axis=1, keepdims=True) + EPS)
            oh = o * r
            sz = _silu(z)
            dzd_ref[:, sl] = (dout * oh * gnv * _dsilu(z)).astype(dzd_ref.dtype)
            dgn_ref[...] += jnp.sum(dout * sz * oh, axis=0, keepdims=True)
            doh = dout * gnv * sz
            dod_ref[:, sl] = r * (doh - oh * jnp.mean(doh * oh, axis=1, keepdims=True))

    return pl.pallas_call(
        body, name="gate_bwd", grid=(S // tm,),
        in_specs=[_rb(tm, D), _rb(tm, D), _fs((SB_W, D)), _fs((DN_W, D)), _rb(tm, SB_W),
                  _rb(tm, SB_W, C_SB_Z // SB_W), _rb(tm, DN_W), _rb(tm, DN_W, C_DN_Z // DN_W), _fs((1, LANES))],
        out_specs=[_rb(tm, SB_W), _rb(tm, SB_W), _rb(tm, DN_W), _rb(tm, DN_W), _fs((1, LANES))],
        out_shape=[_sds((S, SB_W), F32), _sds((S, SB_W), _MXU_DTYPE), _sds((S, DN_W), F32),
                   _sds((S, DN_W), _MXU_DTYPE), _sds((1, LANES), F32)],
        compiler_params=_cp(1),
    )(db_sb, db_dn, wb_sb, wb_dn, o_att, p, o_dn, p, gn)


def _branch(o_sb, o_dnn, wb_sb, wb_dn, p, D, tm=256):
    S = p.shape[0]

    def body(os_ref, od_ref, ws_ref, wd_ref, ms_ref, md_ref, y_ref, bs_ref, bd_ref):
        bs = _mm(os_ref[...], ws_ref[...])
        bdn = _mm(od_ref[...], wd_ref[...])
        bs_ref[...] = bs
        bd_ref[...] = bdn
        y_ref[...] = (_sigmoid(ms_ref[...]) * bs + _sigmoid(md_ref[...]) * bdn).astype(y_ref.dtype)

    return pl.pallas_call(
        body, name="branch", grid=(S // tm,),
        in_specs=[_rb(tm, SB_W), _rb(tm, DN_W), _fs((SB_W, D)), _fs((DN_W, D)),
                  _rb(tm, D, C_MG // D), _rb(tm, D, C_MG // D + 1)],
        out_specs=[_rb(tm, D), _rb(tm, D), _rb(tm, D)],
        out_shape=[_sds((S, D), _MXU_DTYPE), _sds((S, D), F32), _sds((S, D), F32)], compiler_params=_cp(1),
    )(o_sb, o_dnn, wb_sb, wb_dn, p, p)


def _out_proj(x, y, w_out, gate, tm=256):
    S, D = x.shape

    def body(x_ref, y_ref, w_ref, g_ref, xn_ref, out_ref):
        out = _mm(y_ref[...], w_ref[...])
        out_ref[...] = out
        xn_ref[...] = x_ref[...] + g_ref[...] * out

    return pl.pallas_call(
        body, name="out_proj", grid=(S // tm,),
        in_specs=[_rb(tm, D), _rb(tm, D), _fs((D, D)), _fs((1, D))],
        out_specs=[_rb(tm, D), _rb(tm, D)],
        out_shape=[_sds((S, D), F32), _sds((S, D), F32)], compiler_params=_cp(1),
    )(x, y, w_out, gate)


def _out_bwd(dxn, out, gate, w_out, p, b_sb, b_dn, tm=256):
    S, D = dxn.shape

    def body(dxn_ref, out_ref, g_ref, w_ref, ms_ref, md_ref, bs_ref, bd_ref,
             dout_ref, dbs_ref, dbd_ref, dm_ref, dgate_ref):
        @pl.when(pl.program_id(0) == 0)
        def _():
            dgate_ref[...] = jnp.zeros_like(dgate_ref)

        dxv = dxn_ref[...]
        dgate_ref[...] += jnp.sum(dxv * out_ref[...], axis=0, keepdims=True)
        dout = (g_ref[...] * dxv).astype(dout_ref.dtype)
        dout_ref[...] = dout
        dy = _mm(dout, w_ref[...], _NT)
        s1, s2 = _sigmoid(ms_ref[...]), _sigmoid(md_ref[...])
        dbs_ref[...] = (dy * s1).astype(dbs_ref.dtype)
        dbd_ref[...] = (dy * s2).astype(dbd_ref.dtype)
        dm_ref[:, :D] = (dy * bs_ref[...] * s1 * (1.0 - s1)).astype(dm_ref.dtype)
        dm_ref[:, D:] = (dy * bd_ref[...] * s2 * (1.0 - s2)).astype(dm_ref.dtype)

    return pl.pallas_call(
        body, name="out_bwd", grid=(S // tm,),
        in_specs=[_rb(tm, D), _rb(tm, D), _fs((1, D)), _fs((D, D)), _rb(tm, D, C_MG // D),
                  _rb(tm, D, C_MG // D + 1), _rb(tm, D), _rb(tm, D)],
        out_specs=[_rb(tm, D), _rb(tm, D), _rb(tm, D), _rb(tm, 2 * D), _fs((1, D))],
        out_shape=[_sds((S, D), _MXU_DTYPE)] * 3 + [_sds((S, 2 * D), _MXU_DTYPE), _sds((1, D), F32)],
        compiler_params=_cp(1),
    )(dxn, out, gate, w_out, p, p, b_sb, b_dn)


def _loss_head(xf, target, tm=256):
    S, D = xf.shape

    def body(x_ref, t_ref, dy_ref, loss_ref):
        @pl.when(pl.program_id(0) == 0)
        def _():
            loss_ref[...] = jnp.zeros_like(loss_ref)

        e = x_ref[...] - t_ref[...]
        dy_ref[...] = e * (1.0 / D)
        row = jnp.sum(e * e, axis=1, keepdims=True) * (1.0 / D)
        loss_ref[...] += 0.5 * jnp.sum(row, axis=0, keepdims=True)

    return pl.pallas_call(
        body, name="loss_head", grid=(S // tm,),
        in_specs=[_rb(tm, D), _rb(tm, D)], out_specs=[_rb(tm, D), _fs((1, LANES))],
        out_shape=[_sds((S, D), F32), _sds((1, LANES), F32)], compiler_params=_cp(1),
    )(xf, target)


def _ada_fwd(c_all, ada_w, ada_b_sh):
    L, D, n = ada_w.shape
    B = c_all.shape[0]

    def body(c_ref, w_ref, b_ref, o_ref):
        sc = _silu(c_ref[...])
        o_ref[0] = _mm(sc, w_ref[0]) + b_ref[0]

    return pl.pallas_call(
        body, name="ada_fwd", grid=(L,),
        in_specs=[_fs((B, D)), pl.BlockSpec((1, D, n), lambda l: (l, 0, 0)), pl.BlockSpec((1, 1, n), lambda l: (l, 0, 0))],
        out_specs=pl.BlockSpec((1, B, n), lambda l: (l, 0, 0)),
        out_shape=_sds((L, B, n), F32), compiler_params=_cp(1),
    )(c_all, ada_w, ada_b_sh)


def _ada_bwd(c_all_t, dmod_sh):
    D, B = c_all_t.shape
    L, _, n = dmod_sh.shape

    def body(c_ref, d_ref, o_ref):
        acc = jnp.zeros((D, n), F32)
        for b in range(B):
            acc = acc + _silu(c_ref[:, b:b + 1]) * d_ref[0, b:b + 1, :]
        o_ref[0] = acc

    return pl.pallas_call(
        body, name="ada_bwd", grid=(L,),
        in_specs=[_fs((D, B)), pl.BlockSpec((1, B, n), lambda l: (l, 0, 0))],
        out_specs=pl.BlockSpec((1, D, n), lambda l: (l, 0, 0)),
        out_shape=_sds((L, D, n), F32), compiler_params=_cp(1),
    )(c_all_t, dmod_sh)


def _sum_parts(name, parts):
    P, R, C = parts.shape
    tr = _pick(R, max(16, min(512, (1 << 19) // (P * C))), 16) if R % 16 == 0 else R

    def body(p_ref, o_ref):
        acc = p_ref[0].astype(F32)
        for k in range(1, P):
            acc = acc + p_ref[k].astype(F32)
        o_ref[...] = acc

    return pl.pallas_call(
        body, name=name, grid=(R // tr,),
        in_specs=[pl.BlockSpec((P, tr, C), lambda i: (0, i, 0))], out_specs=_rb(tr, C),
        out_shape=_sds((R, C), F32), compiler_params=_cp(1),
    )(parts)


def _adamw(name, w, g, m, v):
    L, R, C = w.shape
    tr = _pick(R, 256, 8) if R % 8 == 0 else R
    c1 = 1.0 - ADAM_B1 ** ADAM_STEP
    c2 = 1.0 - ADAM_B2 ** ADAM_STEP

    def body(w_ref, g_ref, m_ref, v_ref, d_ref, mo_ref, vo_ref):
        gv = g_ref[...]
        mn = ADAM_B1 * m_ref[...] + (1.0 - ADAM_B1) * gv
        vn = ADAM_B2 * v_ref[...] + (1.0 - ADAM_B2) * (gv * gv)
        mo_ref[...] = mn
        vo_ref[...] = vn
        d_ref[...] = -ADAM_LR * ((mn / c1) / (jnp.sqrt(vn / c2) + ADAM_EPS) + ADAM_WD * w_ref[...])

    spec = pl.BlockSpec((1, tr, C), lambda l, i: (l, i, 0))
    return pl.pallas_call(
        body, name=name, grid=(L, R // tr),
        in_specs=[spec] * 4, out_specs=[spec] * 3, out_shape=[_sds((L, R, C), F32)] * 3, compiler_params=_cp(2),
    )(w, g, m, v)


def _ag_small(name, blk):
    R, C = blk.shape

    def body(x_ref, out_ref, send_sems, recv_sems, local_sem):
        x, y, c = lax.axis_index("x"), lax.axis_index("y"), lax.axis_index("c")
        me, sibling = (x, y, c), (x, y, 1 - c)
        chips = [(1 - x, y), (x, 1 - y), (1 - x, 1 - y)]

        def rows(px, py, pc):
            return out_ref.at[pl.ds((4 * px + 2 * py + pc) * R, R), :]

        def copy(k, block, to, src=None):
            return pltpu.make_async_remote_copy(
                src_ref=rows(*block) if src is None else src, dst_ref=rows(*block),
                send_sem=send_sems.at[k], recv_sem=recv_sems.at[k], device_id=to, device_id_type=MESH)

        mine = pltpu.make_async_copy(x_ref, rows(*me), local_sem)
        mine.start()
        first = [copy(0, me, sibling, src=x_ref)]
        first += [copy(1 + j, me, (*chip, c), src=x_ref) for j, chip in enumerate(chips)]
        for cp in first:
            cp.start()
        passed = [copy(4 + j, (*chip, c), sibling) for j, chip in enumerate(chips)]
        for j, chip in enumerate(chips):
            copy(1 + j, (*chip, c), me).wait_recv()
            passed[j].start()
        copy(0, sibling, me).wait_recv()
        for j, chip in enumerate(chips):
            copy(4 + j, (*chip, 1 - c), me).wait_recv()
        for cp in first + passed:
            cp.wait_send()
        mine.wait()

    return pl.pallas_call(
        body, name=name, out_shape=_sds((8 * R, C), blk.dtype),
        in_specs=[pl.BlockSpec(memory_space=pltpu.VMEM)], out_specs=pl.BlockSpec(memory_space=pltpu.VMEM),
        scratch_shapes=[pltpu.SemaphoreType.DMA((7,)), pltpu.SemaphoreType.DMA((7,)), pltpu.SemaphoreType.DMA],
    )(blk)


def _row_chunks(ts, row_axis):
    pieces = []
    for t, a in enumerate(ts):
        rows = a.shape[row_axis]
        n = 4 if rows >= 1024 else 1
        pieces += [(t, i * (rows // n), rows // n) for i in range(n)]
    return pieces


def _ag_weights_first(ts):
    nt = len(ts)
    pieces = _row_chunks(ts, 0)
    NP = len(pieces)
    sizes = [nr * ts[t].shape[1] for t, _, nr in pieces]
    split = next(pi for pi in range(NP + 1) if 2 * sum(sizes[:pi]) >= sum(sizes))

    def body(*refs):
        w, out = refs[:nt], refs[nt:2 * nt]
        send_sems, recv_sems, local_sems = refs[2 * nt:]
        x, y, c = lax.axis_index("x"), lax.axis_index("y"), lax.axis_index("c")
        me, sibling = (x, y, c), (x, y, 1 - c)
        mine = 2 * x + y
        chips = [(1 - x, y), (x, 1 - y), (1 - x, 1 - y)]

        def blk(t, shard, r0, nr):
            return out[t].at[shard, r0:r0 + nr, :]

        def copy(k, dst, to, src=None):
            return pltpu.make_async_remote_copy(
                src_ref=dst if src is None else src, dst_ref=dst, send_sem=send_sems.at[k], recv_sem=recv_sems.at[k],
                device_id=to, device_id_type=MESH)

        own = [pltpu.make_async_copy(w[t], out[t].at[mine], local_sems.at[t]) for t in range(nt)]
        for cp in own:
            cp.start()
        for fetcher, lo, hi in ((0, 0, split), (1, split, NP)):
            @pl.when(c == fetcher)
            def _(lo=lo, hi=hi):
                sent = []
                for j, chip in enumerate(chips):
                    for pi in range(lo, hi):
                        t, r0, nr = pieces[pi]
                        sent.append(copy(j * NP + pi, blk(t, mine, r0, nr), (*chip, c), src=w[t].at[r0:r0 + nr, :]))
                        sent[-1].start()
                for j, chip in enumerate(chips):
                    theirs = 2 * chip[0] + chip[1]
                    for pi in range(lo, hi):
                        t, r0, nr = pieces[pi]
                        copy(j * NP + pi, blk(t, theirs, r0, nr), me).wait_recv()
                        sent.append(copy((3 + j) * NP + pi, blk(t, theirs, r0, nr), sibling))
                        sent[-1].start()
                for cp in sent:
                    cp.wait_send()

            @pl.when(c != fetcher)
            def _(lo=lo, hi=hi):
                for j, chip in enumerate(chips):
                    theirs = 2 * chip[0] + chip[1]
                    for pi in range(lo, hi):
                        t, r0, nr = pieces[pi]
                        copy((3 + j) * NP + pi, blk(t, theirs, r0, nr), me).wait_recv()

        for cp in own:
            cp.wait()

    return pl.pallas_call(
        body, name="ag_weights_first", out_shape=[_sds((4,) + a.shape, a.dtype) for a in ts],
        in_specs=[pl.BlockSpec(memory_space=pl.ANY)] * nt, out_specs=[pl.BlockSpec(memory_space=pltpu.VMEM)] * nt,
        scratch_shapes=[pltpu.SemaphoreType.DMA((6 * NP,)), pltpu.SemaphoreType.DMA((6 * NP,)),
                        pltpu.SemaphoreType.DMA((nt,))],
        compiler_params=pltpu.CompilerParams(vmem_limit_bytes=_VMEM_LIMIT),
    )(*ts)


class _WeightGather:
    def __init__(self, ts):
        self.nt = len(ts)
        self.pieces = _row_chunks(ts, 0)
        NP = len(self.pieces)
        self.out_shape = [_sds((4,) + a.shape, a.dtype) for a in ts]
        self.scratch = [pltpu.SemaphoreType.DMA((3 * NP,)), pltpu.SemaphoreType.DMA((3 * NP,)),
                        pltpu.SemaphoreType.DMA((self.nt,))]

    def _copies(self, src, out, sems):
        send_sems, recv_sems, local_sems = sems
        NP = len(self.pieces)
        x, y, c = lax.axis_index("x"), lax.axis_index("y"), lax.axis_index("c")
        mine = 2 * x + y
        own = [pltpu.make_async_copy(src[t], out[t].at[mine], local_sems.at[t]) for t in range(self.nt)]
        sends, recvs = [], []
        for j, chip in enumerate([(1 - x, y), (x, 1 - y), (1 - x, 1 - y)]):
            theirs = 2 * chip[0] + chip[1]
            for pi, (t, r0, nr) in enumerate(self.pieces):
                idx = j * NP + pi
                sends.append(pltpu.make_async_remote_copy(
                    src_ref=src[t].at[r0:r0 + nr, :], dst_ref=out[t].at[mine, r0:r0 + nr, :],
                    send_sem=send_sems.at[idx], recv_sem=recv_sems.at[idx], device_id=(*chip, c), device_id_type=MESH))
                recvs.append(pltpu.make_async_remote_copy(
                    src_ref=out[t].at[theirs, r0:r0 + nr, :], dst_ref=out[t].at[theirs, r0:r0 + nr, :],
                    send_sem=send_sems.at[idx], recv_sem=recv_sems.at[idx], device_id=(x, y, c), device_id_type=MESH))
        return own, sends, recvs

    def start(self, src, out, sems):
        own, sends, _ = self._copies(src, out, sems)
        for cp in own + sends:
            cp.start()

    def wait(self, src, out, sems):
        own, sends, recvs = self._copies(src, out, sems)
        for cp in recvs:
            cp.wait_recv()
        for cp in sends:
            cp.wait_send()
        for cp in own:
            cp.wait()


class _GradExchange:
    def __init__(self, ts, layer):
        self.nt, self.layer = len(ts), layer
        self.pieces = _row_chunks(ts, 1)
        NP = len(self.pieces)
        self.out_shape = [_sds((8,) + a.shape[1:], a.dtype) for a in ts]
        self.scratch = [pltpu.SemaphoreType.DMA((7 * NP,)), pltpu.SemaphoreType.DMA((7 * NP,)),
                        pltpu.SemaphoreType.DMA((self.nt,))]

    def _copies(self, src, out, sems):
        send_sems, recv_sems, local_sems = sems
        NP = len(self.pieces)
        x, y, c = lax.axis_index("x"), lax.axis_index("y"), lax.axis_index("c")
        me = 4 * x + 2 * y + c
        owner = c == self.layer
        own = [pltpu.make_async_copy(src[t].at[2 * x + y], out[t].at[me], local_sems.at[t]) for t in range(self.nt)]
        rel = []
        for k in range(1, 8):
            px = 1 - x if k & 4 else x
            py = 1 - y if k & 2 else y
            source = 4 * px + 2 * py + (1 - c if k & 1 else c)
            sends, recvs = [], []
            for pi, (t, r0, nr) in enumerate(self.pieces):
                idx = (k - 1) * NP + pi
                sends.append(pltpu.make_async_remote_copy(
                    src_ref=src[t].at[2 * px + py, r0:r0 + nr, :], dst_ref=out[t].at[me, r0:r0 + nr, :],
                    send_sem=send_sems.at[idx], recv_sem=recv_sems.at[idx], device_id=(px, py, self.layer),
                    device_id_type=MESH))
                recvs.append(pltpu.make_async_remote_copy(
                    src_ref=out[t].at[source, r0:r0 + nr, :], dst_ref=out[t].at[source, r0:r0 + nr, :],
                    send_sem=send_sems.at[idx], recv_sem=recv_sems.at[idx], device_id=(x, y, c),
                    device_id_type=MESH))
            rel.append((jnp.logical_not(owner) if k & 1 else owner, sends, recvs))
        return owner, own, rel

    def start(self, src, out, sems):
        owner, own, rel = self._copies(src, out, sems)

        @pl.when(owner)
        def _():
            for cp in own:
                cp.start()

        for sending, sends, _ in rel:
            @pl.when(sending)
            def _(sends=sends):
                for cp in sends:
                    cp.start()

    def wait(self, src, out, sems):
        owner, own, rel = self._copies(src, out, sems)

        @pl.when(owner)
        def _():
            for _, _, recvs in rel:
                for cp in recvs:
                    cp.wait_recv()
            for cp in own:
                cp.wait()

        for sending, sends, _ in rel:
            @pl.when(sending)
            def _(sends=sends):
                for cp in sends:
                    cp.wait_send()


def _sibling_join(ts):
    nt = len(ts)
    pieces = _row_chunks(ts, 0)
    NP = len(pieces)

    def body(*refs):
        src, out = refs[:nt], refs[nt:2 * nt]
        send_sems, recv_sems, local_sems = refs[2 * nt:]
        x, y, c = lax.axis_index("x"), lax.axis_index("y"), lax.axis_index("c")
        own = [pltpu.make_async_copy(src[t], out[t].at[c], local_sems.at[t]) for t in range(nt)]
        for cp in own:
            cp.start()
        sent = []
        for pi, (t, r0, nr) in enumerate(pieces):
            sent.append(pltpu.make_async_remote_copy(
                src_ref=src[t].at[r0:r0 + nr, :], dst_ref=out[t].at[c, r0:r0 + nr, :], send_sem=send_sems.at[pi],
                recv_sem=recv_sems.at[pi], device_id=(x, y, 1 - c), device_id_type=MESH))
            sent[-1].start()
        for pi, (t, r0, nr) in enumerate(pieces):
            pltpu.make_async_remote_copy(
                src_ref=src[t].at[r0:r0 + nr, :], dst_ref=out[t].at[1 - c, r0:r0 + nr, :], send_sem=send_sems.at[pi],
                recv_sem=recv_sems.at[pi], device_id=(x, y, c), device_id_type=MESH).wait_recv()
        for cp in sent:
            cp.wait_send()
        for cp in own:
            cp.wait()

    vmem = pl.BlockSpec(memory_space=pltpu.VMEM)
    return pl.pallas_call(
        body, name="sibling_join", out_shape=[_sds((2,) + a.shape, a.dtype) for a in ts],
        in_specs=[vmem] * nt, out_specs=[vmem] * nt,
        scratch_shapes=[pltpu.SemaphoreType.DMA((NP,)), pltpu.SemaphoreType.DMA((NP,)),
                        pltpu.SemaphoreType.DMA((nt,))],
        compiler_params=pltpu.CompilerParams(vmem_limit_bytes=_VMEM_LIMIT),
    )(*ts)


def _layer_fwd(x, shift, scale, gate, lw, next_shards=None):
    D = x.shape[1]
    h = _norm_mod(x, lw["norm_g"], scale, shift)
    p = _matmul("in_proj", h, lw["w_cat"], "nn", F32, tm_cap=1024, tn_cap=896)
    qn, kn = _sb_prep(p, lw["gq_t"], lw["gk_t"])
    o_att, tot, nblocks = _sb_fwd(qn, kn, p)
    qkv, bb, gcb, glb = _dn_prep(p, lw["conv_w"], lw["a_row"], lw["dtb_row"])
    if next_shards is None:
        (o_dn, tinv, sall), gathered = _dn_fwd(qkv, bb, gcb, glb)
    else:
        (o_dn, tinv, sall), gathered = _dn_fwd(qkv, bb, gcb, glb, _WeightGather(next_shards), next_shards)
    o_sb, o_dnn = _gate(o_att, o_dn, p, lw["gn"])
    y, b_sb, b_dn = _branch(o_sb, o_dnn, lw["wb_sb"], lw["wb_dn"], p, D)
    x_next, out = _out_proj(x, y, lw["w_out"], gate)
    res = dict(x=x, h=h, p=p, qn=qn, kn=kn, o_att=o_att, tot=tot, nblocks=nblocks, qkv=qkv, bb=bb, gcb=gcb, glb=glb, o_dn=o_dn,
               tinv=tinv, sall=sall, o_sb=o_sb, o_dnn=o_dnn, y=y, b_sb=b_sb, b_dn=b_dn, out=out,
               shift=shift, scale=scale, gate=gate)
    return x_next, res, gathered


def _layer_bwd(dxn, res, lw, pending=None):
    p = res["p"]
    dout, db_sb, db_dn, dm, dgate = _out_bwd(dxn, res["out"], res["gate"], lw["w_out"], p, res["b_sb"], res["b_dn"])
    dw_out = _matmul("dw_out", res["y"], dout, "tn", _MXU_DTYPE)
    dwb_sb = _matmul("dwb_sb", res["o_sb"], db_sb, "tn", _MXU_DTYPE)
    dwb_dn = _matmul("dwb_dn", res["o_dnn"], db_dn, "tn", _MXU_DTYPE)
    do_att, dz_sb, do_dn, dz_dn, dgn = _gate_bwd(db_sb, db_dn, lw["wb_sb"], lw["wb_dn"], res["o_att"], res["o_dn"],
                                                  p, lw["gn"])
    D = dxn.shape[1]
    by_shard = lambda g: g.reshape(g.shape[0], 4, g.shape[1] // 4).transpose(1, 0, 2)
    send = [by_shard(dwb_sb), by_shard(dwb_dn), dw_out.reshape(4, D // 4, D)]
    if pending is None:
        (dqn, dkn, dv), received = _sb_bwd(res["qn"], res["kn"], p, do_att, res["tot"], res["nblocks"])
    else:
        (dqn, dkn, dv), received = _sb_bwd(res["qn"], res["kn"], p, do_att, res["tot"], res["nblocks"],
                                           _GradExchange(pending[1], pending[0]), pending[1],
                                           _GradExchange(send, pending[0] - 1), send)
        send = []
    dq_sb, dk_sb, dgq, dgk = _sb_prep_bwd(p, dqn, dkn, lw["gq_t"], lw["gk_t"])
    dqkv, dbb, dgb = _dn_bwd(res["qkv"], res["bb"], res["gcb"], res["glb"], res["tinv"], res["sall"], do_dn)
    dc, dp_ba, dal, ddt = _dn_prep_bwd_a(p, dqkv, dbb, dgb, lw["conv_w"], lw["a_row"], lw["dtb_row"])
    dp_dn, dconv = _dn_prep_bwd_b(p, dc, lw["conv_w"])
    dp = jnp.concatenate([dp_dn, dz_dn, dq_sb, dk_sb, dv.astype(_MXU_DTYPE), dz_sb, dm, dp_ba], axis=1)
    dw_cat = _matmul("dw_cat", res["h"], dp, "tn", _MXU_DTYPE, tm_cap=1024, tn_cap=896, tk_cap=512)
    send = [_shards_from_cat(dw_cat, D)] + send
    if pending is None:
        dh = _matmul("dh", dp, lw["w_cat"], "nt", F32, tm_cap=1024, tk_cap=896)
    else:
        dh, arrived = _matmul("dh", dp, lw["w_cat"], "nt", F32, tm_cap=1024, tk_cap=896,
                              exchange=_GradExchange(send, pending[0] - 1), ex_src=send, ex_prev=received[:1])
        received, send = arrived + list(received[1:]), []
    dx, dshift, dscale, dnorm_g = _norm_mod_bwd(res["x"], dh, dxn, lw["norm_g"], res["scale"])
    small = dict(dmod=jnp.concatenate([dshift, dscale, dgate], axis=1)[0], norm_g=dnorm_g[0],
                 sb_q_g=dgq.reshape(SB_HEADS, SB_HD).sum(0), sb_k_g=dgk.reshape(SB_HEADS, SB_HD).sum(0),
                 conv_w=dconv, dn_a_log=dal[0, DN_HEADS:2 * DN_HEADS], dn_dt_bias=ddt[0, DN_HEADS:2 * DN_HEADS],
                 dn_norm_g=dgn[0])
    return dx, small, send, received


def _cat_cols(w, D):
    return jnp.concatenate([w[:, 2048:4096], w[:, 0:2048], w[:, 4104:4104 + 2 * D], w[:, 4096:4104],
                            jnp.zeros((w.shape[0], LANES - 8), w.dtype)], axis=1)


def _shards_from_cat(g, D):
    n = (4104 + 2 * D) // 4
    segments = ((0, 2048, 2048), (2048, 4096, 0), (4096, 4104, 4096 + 2 * D), (4104, 4104 + 2 * D, 4096))

    def shard(lo, hi):
        cuts = [(c0 + max(lo, s0) - s0, c0 + min(hi, s1) - s0) for s0, s1, c0 in segments if max(lo, s0) < min(hi, s1)]
        return jnp.concatenate([g[:, a:b] for a, b in cuts], axis=1)

    return jnp.stack([shard(s * n, (s + 1) * n) for s in range(4)])


def _flat_pack(arrs, mult):
    flat = jnp.concatenate([a.reshape(-1) for a in arrs])
    n = flat.shape[0]
    pad = (-n) % mult
    if pad:
        flat = jnp.concatenate([flat, jnp.zeros((pad,), flat.dtype)])
    return flat.reshape(-1, LANES)


def _flat_unpack(flat, shapes):
    flat = flat.reshape(-1)
    out, off = [], 0
    for s in shapes:
        n = math.prod(s)
        out.append(flat[off:off + n].reshape(s))
        off += n
    return out


BIG = ("w_in", "w_branch_sb", "w_branch_dn", "w_out")
SMALL = ("ada_b", "norm_g", "sb_q_g", "sb_k_g", "conv_w", "dn_a_log", "dn_dt_bias", "dn_norm_g")


def kernel(x, c, ada_w, ada_b, norm_g, w_in, sb_q_g, sb_k_g, conv_w, dn_a_log, dn_dt_bias, dn_norm_g, w_branch_sb, w_branch_dn, w_out, loss_target, m_ada_w, m_ada_b, m_norm_g, m_w_in, m_sb_q_g, m_sb_k_g, m_conv_w, m_dn_a_log, m_dn_dt_bias, m_dn_norm_g, m_w_branch_sb, m_w_branch_dn, m_w_out, v_ada_w, v_ada_b, v_norm_g, v_w_in, v_sb_q_g, v_sb_k_g, v_conv_w, v_dn_a_log, v_dn_dt_bias, v_dn_norm_g, v_w_branch_sb, v_w_branch_dn, v_w_out):
    W = dict(ada_w=ada_w, ada_b=ada_b, norm_g=norm_g, w_in=w_in, sb_q_g=sb_q_g, sb_k_g=sb_k_g, conv_w=conv_w,
             dn_a_log=dn_a_log, dn_dt_bias=dn_dt_bias, dn_norm_g=dn_norm_g, w_branch_sb=w_branch_sb,
             w_branch_dn=w_branch_dn, w_out=w_out)
    M = dict(ada_w=m_ada_w, ada_b=m_ada_b, norm_g=m_norm_g, w_in=m_w_in, sb_q_g=m_sb_q_g, sb_k_g=m_sb_k_g,
             conv_w=m_conv_w, dn_a_log=m_dn_a_log, dn_dt_bias=m_dn_dt_bias, dn_norm_g=m_dn_norm_g,
             w_branch_sb=m_w_branch_sb, w_branch_dn=m_w_branch_dn, w_out=m_w_out)
    V = dict(ada_w=v_ada_w, ada_b=v_ada_b, norm_g=v_norm_g, w_in=v_w_in, sb_q_g=v_sb_q_g, sb_k_g=v_sb_k_g,
             conv_w=v_conv_w, dn_a_log=v_dn_a_log, dn_dt_bias=v_dn_dt_bias, dn_norm_g=v_dn_norm_g,
             w_branch_sb=v_w_branch_sb, w_branch_dn=v_w_branch_dn, w_out=v_w_out)
    L = ada_w.shape[0]
    S, D = x.shape[1], x.shape[2]
    ix, iy, ic = lax.axis_index("x"), lax.axis_index("y"), lax.axis_index("c")
    shard = 2 * ix + iy
    me = 2 * shard + ic
    n_ada = ada_w.shape[2]
    n_in = w_in.shape[2]
    n_conv = conv_w.shape[2]
    n_br = w_branch_sb.shape[2]
    n_out = w_out.shape[1]

    assert L == 2, "the owner of a layer's gradients is the core with the layer's number"
    shards = [W[n].astype(_MXU_DTYPE) for n in BIG]
    gathered0 = _ag_weights_first([a[0] for a in shards])

    g1 = _ag_small("ag_c_conv", _flat_pack([c, conv_w], LANES * 8))
    g1 = g1.reshape(8, -1)
    c_all = g1[:, :D]
    conv_parts = g1[:, D:D + L * CONV_K * n_conv].reshape(4, 2, L, CONV_K, n_conv)[:, 0]
    conv_full = jnp.concatenate([conv_parts[s] for s in range(4)], axis=2)
    ada_b_sh = lax.dynamic_slice_in_dim(ada_b, shard * n_ada, n_ada, axis=1)[:, None, :]
    mod_sh = _ada_fwd(c_all, ada_w, ada_b_sh)
    g2 = _ag_small("ag_mod", _flat_pack([mod_sh], LANES * 8)).reshape(8, -1)
    mod_parts = g2[:, :L * 8 * n_ada].reshape(4, 2, L, 8, n_ada)[:, 0]
    mod_all = jnp.concatenate([mod_parts[s] for s in range(4)], axis=2)
    mod = lax.dynamic_index_in_dim(mod_all, me, axis=1, keepdims=False)

    def layer_weights(l, gathered):
        g_in, g_bs, g_bd, g_out = gathered
        cat = lambda g, axis: jnp.concatenate([g[s] for s in range(4)], axis=axis)
        pad_lo = jnp.zeros((DN_HEADS,), F32)
        pad_hi = jnp.zeros((LANES - 2 * DN_HEADS,), F32)
        return dict(
            norm_g=norm_g[l][None, :], w_cat=_cat_cols(cat(g_in, 1), D),
            gq_t=jnp.tile(sb_q_g[l], SB_HEADS)[None, :], gk_t=jnp.tile(sb_k_g[l], SB_HEADS)[None, :],
            conv_w=conv_full[l],
            a_row=jnp.concatenate([pad_lo, dn_a_log[l], pad_hi])[None, :],
            dtb_row=jnp.concatenate([pad_lo, dn_dt_bias[l], pad_hi])[None, :],
            gn=dn_norm_g[l][None, :], wb_sb=cat(g_bs, 1), wb_dn=cat(g_bd, 1), w_out=cat(g_out, 0))

    mods = lambda l: (mod[l, None, 0:D], mod[l, None, D:2 * D], mod[l, None, 2 * D:3 * D])
    lws, ress = [None] * L, [None] * L
    lws[0] = layer_weights(0, gathered0)
    xs, ress[0], gathered1 = _layer_fwd(x[0], *mods(0), lws[0], [a[1] for a in shards])
    lws[1] = layer_weights(1, gathered1)
    xs, ress[1], _ = _layer_fwd(xs, *mods(1), lws[1])
    dxs, loss_row = _loss_head(xs, loss_target[0])
    loss = lax.psum(loss_row[0, 0], ("x", "y", "c"))
    smalls = [None] * L
    dxs, smalls[1], send1, _ = _layer_bwd(dxs, ress[1], lws[1])
    dxs, smalls[0], send0, got = _layer_bwd(dxs, ress[0], lws[0], (1, send1))
    grad_x = dxs[None]

    small_names = ("dmod",) + SMALL[1:]
    small_pack = _flat_pack([jnp.stack([smalls[l][n] for l in range(L)]) for n in small_names], LANES * 8)
    g3 = _ag_small("ag_small_grads", small_pack)
    R3 = small_pack.shape[0]
    g3 = g3.reshape(8, R3, LANES)
    small_sum = _sum_parts("sum_small", g3)
    small_shapes = [(L, 3 * D), (L, D), (L, SB_HD), (L, SB_HD), (L, CONV_K, 3 * DN_W), (L, DN_HEADS), (L, DN_HEADS),
                    (L, DN_HD)]
    sg = dict(zip(small_names, _flat_unpack(small_sum, small_shapes)))
    G = dict(ada_b=sg["dmod"], norm_g=sg["norm_g"], sb_q_g=sg["sb_q_g"], sb_k_g=sg["sb_k_g"],
             conv_w=lax.dynamic_slice_in_dim(sg["conv_w"], shard * n_conv, n_conv, axis=2),
             dn_a_log=sg["dn_a_log"], dn_dt_bias=sg["dn_dt_bias"], dn_norm_g=sg["dn_norm_g"])
    dmod_all = g3.reshape(8, -1)[:, :L * 3 * D].reshape(8, L, 3 * D)
    dmod_sh = lax.dynamic_slice_in_dim(dmod_all, shard * n_ada, n_ada, axis=2).transpose(1, 0, 2)
    G["ada_w"] = _ada_bwd(c_all.T, dmod_sh)

    assert not send0
    mine = [_sum_parts("sum_" + n, g) for n, g in zip(BIG, got)]
    for n, g in zip(BIG, _sibling_join(mine)):
        G[n] = g

    delta, new_m, new_v = {}, {}, {}
    for n in ("ada_w",) + BIG:
        delta[n], new_m[n], new_v[n] = _adamw("adamw_" + n, W[n], G[n], M[n], V[n])
    sm_shapes = [W[n].shape for n in SMALL]
    d, mo, vo = _adamw("adamw_small", *[_flat_pack([T[n] for n in SMALL], LANES * 8)[None] for T in (W, G, M, V)])
    for n, dd, mm, vv in zip(SMALL, _flat_unpack(d, sm_shapes), _flat_unpack(mo, sm_shapes),
                             _flat_unpack(vo, sm_shapes)):
        delta[n], new_m[n], new_v[n] = dd, mm, vv

    order = ("ada_w", "ada_b", "norm_g", "w_in", "sb_q_g", "sb_k_g", "conv_w", "dn_a_log", "dn_dt_bias", "dn_norm_g",
             "w_branch_sb", "w_branch_dn", "w_out")
    return (loss, grad_x, *[G[n] for n in order], *[delta[n] for n in order], *[new_m[n] for n in order],
            *[new_v[n] for n in order])
```

```python
import math

import jax
import jax.numpy as jnp
from jax import lax
from jax.experimental import pallas as pl
from jax.experimental.pallas import tpu as pltpu

F32 = jnp.float32
BF16 = jnp.bfloat16
_MXU_DTYPE = BF16
_VMEM_LIMIT = 48 * 1024 * 1024
LANES = 128

EPS = 1e-6
SB_HEADS, SB_HD, SB_W = 8, 64, 512
DN_HEADS, DN_HD, DN_W = 4, 128, 512
CONV_K = 4
CHUNK = 64
QB = 256
_SB_DEAD = 104.0
ADAM_LR, ADAM_B1, ADAM_B2, ADAM_EPS, ADAM_WD, ADAM_STEP = 0.001, 0.9, 0.999, 1e-08, 0.01, 10

C_DN_QKV, C_DN_Z, C_SB_Q, C_SB_K, C_SB_V, C_SB_Z, C_MG = 0, 1536, 2048, 2560, 3072, 3584, 4096

_NN = (((1,), (0,)), ((), ()))
_NT = (((1,), (1,)), ((), ()))
_TN = (((0,), (0,)), ((), ()))
_BNN = (((2,), (1,)), ((0,), (0,)))
_BNT = (((2,), (2,)), ((0,), (0,)))
_BTN = (((1,), (1,)), ((0,), (0,)))
MESH = pl.DeviceIdType.MESH


def _sds(shape, dtype):
    return jax.ShapeDtypeStruct(shape, dtype)


def _cp(n):
    return pltpu.CompilerParams(dimension_semantics=("arbitrary",) * n, vmem_limit_bytes=_VMEM_LIMIT)


def _rb(tm, w, cb=0):
    return pl.BlockSpec((tm, w), lambda i: (i, cb))


def _fs(shape):
    nd = len(shape)
    return pl.BlockSpec(shape, lambda i: (0,) * nd)


def _dg(a, b, dims):
    return lax.dot_general(a, b, dims, preferred_element_type=F32)


def _mm(a, b, dims=_NN):
    return _dg(a.astype(_MXU_DTYPE), b.astype(_MXU_DTYPE), dims)


def _split3(x):
    hi = x.astype(BF16)
    r = x - hi.astype(F32)
    mid = r.astype(BF16)
    lo = (r - mid.astype(F32)).astype(BF16)
    return hi, mid, lo


def _mm_xl(x, const, dims=_NN):
    cb = const.astype(BF16)
    hi, mid, lo = _split3(x)
    return _dg(hi, cb, dims) + _dg(mid, cb, dims) + _dg(lo, cb, dims)


def _mm_xl2(x, const, dims=_NN):
    cb = const.astype(BF16)
    hi = x.astype(BF16)
    lo = (x - hi.astype(F32)).astype(BF16)
    return _dg(hi, cb, dims) + _dg(lo, cb, dims)


def _mm_xr(const, x, dims=_NN):
    cb = const.astype(BF16)
    hi, mid, lo = _split3(x)
    return _dg(cb, hi, dims) + _dg(cb, mid, dims) + _dg(cb, lo, dims)


def _mm3(a, b, dims=_NN):
    ah, am, _ = _split3(a)
    bh, bm, _ = _split3(b)
    return _dg(ah, bh, dims) + (_dg(ah, bm, dims) + _dg(am, bh, dims))


def _sigmoid(z):
    return 1.0 / (1.0 + jnp.exp(-z))


def _silu(z):
    return z * _sigmoid(z)


def _dsilu(z):
    s = _sigmoid(z)
    return s * (1.0 + z * (1.0 - s))


def _softplus(z):
    return jnp.maximum(z, 0.0) + jnp.log(1.0 + jnp.exp(-jnp.abs(z)))


def _iota2(shape, dim):
    return lax.broadcasted_iota(jnp.int32, shape, dim)


def _pick(n, cap, mult):
    best = None
    for t in range(mult, min(n, cap) + 1, mult):
        if n % t == 0:
            best = t
    assert best is not None, (n, cap, mult)
    return best


def _matmul(name, a, b, form, out_dtype, tm_cap=512, tn_cap=1024, tk_cap=1024, exchange=None, ex_src=(), ex_prev=()):
    if form == "nn":
        (M, K), (_, N) = a.shape, b.shape
    elif form == "nt":
        (M, K), (N, _) = a.shape, b.shape
    else:
        (K, M), (_, N) = a.shape, b.shape
    tm = _pick(M, tm_cap, 128 if form == "tn" else 8)
    tn = _pick(N, tn_cap, 128)
    tk = _pick(K, tk_cap, 128)
    nk = K // tk
    dims = {"nn": _NN, "nt": _NT, "tn": _TN}[form]
    if form == "nn":
        a_spec = pl.BlockSpec((tm, tk), lambda i, j, k: (i, k))
        b_spec = pl.BlockSpec((tk, tn), lambda i, j, k: (k, j))
    elif form == "nt":
        a_spec = pl.BlockSpec((tm, tk), lambda i, j, k: (i, k))
        b_spec = pl.BlockSpec((tn, tk), lambda i, j, k: (j, k))
    else:
        a_spec = pl.BlockSpec((tk, tm), lambda i, j, k: (k, i))
        b_spec = pl.BlockSpec((tk, tn), lambda i, j, k: (k, j))

    grid = (M // tm, N // tn, nk)
    nx, npv = len(ex_src), len(ex_prev)
    o0 = 2 + nx + npv

    def body(*refs):
        if exchange is None:
            compute(*refs)
            return
        src, xout, sems = refs[2:2 + nx], refs[o0 + 1:o0 + 1 + nx], refs[o0 + 2 + nx:]
        at = [pl.program_id(d) for d in range(3)]

        @pl.when(jnp.logical_and(jnp.logical_and(at[0] == 0, at[1] == 0), at[2] == 0))
        def _():
            exchange.start(src, xout, sems)

        compute(refs[0], refs[1], refs[o0], refs[o0 + 1 + nx])

        @pl.when(jnp.logical_and(jnp.logical_and(at[0] == grid[0] - 1, at[1] == grid[1] - 1), at[2] == nk - 1))
        def _():
            exchange.wait(src, xout, sems)

    def compute(a_ref, b_ref, o_ref, acc_ref):
        k = pl.program_id(2)

        @pl.when(k == 0)
        def _():
            acc_ref[...] = jnp.zeros_like(acc_ref)

        acc_ref[...] += _mm(a_ref[...], b_ref[...], dims)

        @pl.when(k == nk - 1)
        def _():
            o_ref[...] = acc_ref[...].astype(o_ref.dtype)

    hbm = pl.BlockSpec(memory_space=pl.ANY)
    outs = pl.pallas_call(
        body, name=name, grid=grid,
        in_specs=[a_spec, b_spec] + [hbm] * (nx + npv),
        out_specs=[pl.BlockSpec((tm, tn), lambda i, j, k: (i, j))] + [hbm] * nx,
        out_shape=[_sds((M, N), out_dtype)] + (exchange.out_shape if exchange else []),
        input_output_aliases={2 + nx + t: 1 + t for t in range(npv)},
        scratch_shapes=[pltpu.VMEM((tm, tn), F32)] + (exchange.scratch if exchange else []),
        compiler_params=_cp(3),
    )(a, b, *ex_src, *ex_prev)
    return outs[0] if exchange is None else (outs[0], list(outs[1:]))


def _norm_mod(x, g, scale, shift, tm=256):
    S, D = x.shape

    def body(x_ref, g_ref, sc_ref, sh_ref, h_ref):
        xv = x_ref[...]
        r = lax.rsqrt(jnp.mean(xv * xv, axis=1, keepdims=True) + EPS)
        h_ref[...] = ((xv * r * g_ref[...]) * (1.0 + sc_ref[...]) + sh_ref[...]).astype(h_ref.dtype)

    return pl.pallas_call(
        body, name="norm_mod", grid=(S // tm,),
        in_specs=[_rb(tm, D), _fs((1, D)), _fs((1, D)), _fs((1, D))],
        out_specs=_rb(tm, D), out_shape=_sds((S, D), _MXU_DTYPE), compiler_params=_cp(1),
    )(x, g, scale, shift)


def _norm_mod_bwd(x, dh, dxn, g, scale, tm=256):
    S, D = x.shape

    def body(x_ref, dh_ref, dxn_ref, g_ref, sc_ref, dx_ref, dsh_ref, dsc_ref, dg_ref):
        @pl.when(pl.program_id(0) == 0)
        def _():
            dsh_ref[...] = jnp.zeros_like(dsh_ref)
            dsc_ref[...] = jnp.zeros_like(dsc_ref)
            dg_ref[...] = jnp.zeros_like(dg_ref)

        xv, dhv, gv = x_ref[...], dh_ref[...], g_ref[...]
        r = lax.rsqrt(jnp.mean(xv * xv, axis=1, keepdims=True) + EPS)
        xh = xv * r
        one_sc = 1.0 + sc_ref[...]
        dsh_ref[...] += jnp.sum(dhv, axis=0, keepdims=True)
        dsc_ref[...] += jnp.sum(dhv * xh * gv, axis=0, keepdims=True)
        dg_ref[...] += jnp.sum(dhv * one_sc * xh, axis=0, keepdims=True)
        dxh = dhv * (gv * one_sc)
        dx_ref[...] = r * (dxh - xh * jnp.mean(dxh * xh, axis=1, keepdims=True)) + dxn_ref[...]

    return pl.pallas_call(
        body, name="norm_mod_bwd", grid=(S // tm,),
        in_specs=[_rb(tm, D), _rb(tm, D), _rb(tm, D), _fs((1, D)), _fs((1, D))],
        out_specs=[_rb(tm, D), _fs((1, D)), _fs((1, D)), _fs((1, D))],
        out_shape=[_sds((S, D), F32)] + [_sds((1, D), F32)] * 3, compiler_params=_cp(1),
    )(x, dh, dxn, g, scale)


def _head_sum_matrix():
    r = jnp.arange(SB_W)
    return (r[:, None] // SB_HD == r[None, :] // SB_HD).astype(BF16)


def _sb_prep(p, gq_t, gk_t, tm=256):
    S = p.shape[0]
    bd = _head_sum_matrix()

    def body(q_ref, k_ref, gq_ref, gk_ref, bd_ref, qn_ref, kn_ref):
        for src, g_ref, dst in ((q_ref, gq_ref, qn_ref), (k_ref, gk_ref, kn_ref)):
            v = src[...]
            ms = _mm_xl(v * v, bd_ref[...]) * (1.0 / SB_HD)
            dst[...] = (v * lax.rsqrt(ms + EPS) * g_ref[...]).astype(dst.dtype)

    return pl.pallas_call(
        body, name="sb_prep", grid=(S // tm,),
        in_specs=[_rb(tm, SB_W, C_SB_Q // SB_W), _rb(tm, SB_W, C_SB_K // SB_W),
                  _fs((1, SB_W)), _fs((1, SB_W)), _fs((SB_W, SB_W))],
        out_specs=[_rb(tm, SB_W), _rb(tm, SB_W)],
        out_shape=[_sds((S, SB_W), _MXU_DTYPE)] * 2, compiler_params=_cp(1),
    )(p, p, gq_t, gk_t, bd)


def _sb_prep_bwd(p, dqn, dkn, gq_t, gk_t, tm=256):
    S = p.shape[0]
    bd = _head_sum_matrix()

    def body(q_ref, k_ref, dqn_ref, dkn_ref, gq_ref, gk_ref, bd_ref, dq_ref, dk_ref, dgq_ref, dgk_ref):
        @pl.when(pl.program_id(0) == 0)
        def _():
            dgq_ref[...] = jnp.zeros_like(dgq_ref)
            dgk_ref[...] = jnp.zeros_like(dgk_ref)

        for src, dn_ref, g_ref, dst, dg_ref in ((q_ref, dqn_ref, gq_ref, dq_ref, dgq_ref),
                                                (k_ref, dkn_ref, gk_ref, dk_ref, dgk_ref)):
            v, dn = src[...], dn_ref[...]
            r = lax.rsqrt(_mm_xl(v * v, bd_ref[...]) * (1.0 / SB_HD) + EPS)
            vh = v * r
            dg_ref[...] += jnp.sum(dn * vh, axis=0, keepdims=True)
            dvh = dn * g_ref[...]
            m = _mm_xl(dvh * vh, bd_ref[...]) * (1.0 / SB_HD)
            dst[...] = (r * (dvh - vh * m)).astype(dst.dtype)

    return pl.pallas_call(
        body, name="sb_prep_bwd", grid=(S // tm,),
        in_specs=[_rb(tm, SB_W, C_SB_Q // SB_W), _rb(tm, SB_W, C_SB_K // SB_W), _rb(tm, SB_W), _rb(tm, SB_W),
                  _fs((1, SB_W)), _fs((1, SB_W)), _fs((SB_W, SB_W))],
        out_specs=[_rb(tm, SB_W), _rb(tm, SB_W), _fs((1, SB_W)), _fs((1, SB_W))],
        out_shape=[_sds((S, SB_W), _MXU_DTYPE)] * 2 + [_sds((1, SB_W), F32)] * 2, compiler_params=_cp(1),
    )(p, p, dqn, dkn, gq_t, gk_t, bd)


def _sb_consts():
    r, c = _iota2((QB, QB), 0), _iota2((QB, QB), 1)
    lane = _iota2((1, LANES), 1)
    return r, c, lane


def _sb_fwd(qn, kn, p):
    S = qn.shape[0]
    scale = 1.0 / math.sqrt(SB_HD)

    def body(q_ref, k_ref, v_ref, o_ref, tot_ref, nb_ref):
        i = pl.program_id(1)
        r, c, lane = _sb_consts()
        u_gt = (r > c).astype(BF16)
        strict = jnp.concatenate([c < r, c < r], axis=0)
        q = q_ref[...]
        mask0 = (lane // SB_HD) == 0
        zero = jnp.zeros_like(q)
        qh = jnp.concatenate([jnp.where(mask0, q, zero), jnp.where(mask0, zero, q)], axis=0)

        def block(off, carry, diagonal):
            o, run = carry
            kj = k_ref[pl.ds(off, QB), :]
            vj = v_ref[pl.ds(off, QB), :].astype(_MXU_DTYPE)
            z = _mm(qh, kj, _NT) * scale
            sp = _softplus(z)
            sp_m = jnp.where(strict, sp, 0.0) if diagonal else sp
            later = _mm_xl2(sp_m, u_gt)
            w = jnp.exp((z - sp) - later - run)
            if diagonal:
                w = jnp.where(strict, w, 0.0)
            return o + _mm(w, vj), run + jnp.sum(sp_m, axis=1, keepdims=True)

        init = (jnp.zeros((2 * QB, LANES), F32), jnp.zeros((2 * QB, 1), F32))
        carry = block(pl.multiple_of(i * QB, QB), init, True)
        st = lax.while_loop(
            lambda st: jnp.logical_and(st[0] <= i, jnp.min(st[2]) < _SB_DEAD),
            lambda st: (st[0] + 1,) + block(pl.multiple_of((i - st[0]) * QB, QB), st[1:], False),
            (jnp.int32(1),) + carry)
        o_ref[...] = jnp.where(mask0, st[1][:QB], st[1][QB:])
        tot_ref[...] = jnp.where(mask0, st[2][:QB], st[2][QB:])
        nb_ref[...] = jnp.zeros((8, LANES), F32) + st[0].astype(F32)

    blk = pl.BlockSpec((QB, LANES), lambda hp, i: (i, hp))
    return pl.pallas_call(
        body, name="sb_fwd", grid=(SB_W // LANES, S // QB),
        in_specs=[blk, pl.BlockSpec((S, LANES), lambda hp, i: (0, hp)),
                  pl.BlockSpec((S, LANES), lambda hp, i: (0, C_SB_V // LANES + hp))],
        out_specs=[blk, blk, pl.BlockSpec((8, LANES), lambda hp, i: (i, hp))],
        out_shape=[_sds((S, SB_W), F32), _sds((S, SB_W), F32), _sds((8 * S // QB, SB_W), F32)],
        compiler_params=_cp(2),
    )(qn, kn, p)


def _sb_bwd(qn, kn, p, do, tot, nblocks, exchange=None, ex_src=(), early=None, early_src=()):
    S = qn.shape[0]
    scale = 1.0 / math.sqrt(SB_HD)
    grid = (SB_W // LANES, S // QB)
    nx, ne = len(ex_src), len(early_src)
    NI = 6

    def body(*refs):
        if exchange is None:
            compute(*refs)
            return
        src, src2 = refs[NI:NI + nx], refs[NI + nx:NI + nx + ne]
        o0 = NI + nx + ne
        xout, sems, sems2 = refs[o0 + 3:o0 + 3 + nx], refs[o0 + 3 + nx:o0 + 6 + nx], refs[o0 + 6 + nx:]
        hp, i = pl.program_id(0), pl.program_id(1)

        @pl.when(jnp.logical_and(hp == 0, i == 0))
        def _():
            exchange.start(src, xout, sems)
            if early is not None:
                early.start(src2, xout[nx - ne:], sems2)

        compute(*refs[:NI], *refs[o0:o0 + 3])

        @pl.when(jnp.logical_and(hp == grid[0] - 1, i == grid[1] - 1))
        def _():
            exchange.wait(src, xout, sems)
            if early is not None:
                early.wait(src2, xout[nx - ne:], sems2)

    def compute(q_ref, k_ref, v_ref, do_ref, tot_ref, nb_ref, dq_ref, dk_ref, dv_ref):
        i = pl.program_id(1)

        @pl.when(i == 0)
        def _():
            dk_ref[...] = jnp.zeros_like(dk_ref)
            dv_ref[...] = jnp.zeros_like(dv_ref)

        r, c, lane = _sb_consts()
        u_le = (r <= c).astype(BF16)
        u_lt = (r < c).astype(BF16)
        strict = jnp.concatenate([c < r, c < r], axis=0)
        q = q_ref[...]
        do = do_ref[...].astype(_MXU_DTYPE)
        mask0 = (lane // SB_HD) == 0
        zero, zero_do = jnp.zeros_like(q), jnp.zeros_like(do)
        qh = jnp.concatenate([jnp.where(mask0, q, zero), jnp.where(mask0, zero, q)], axis=0)
        doh = jnp.concatenate([jnp.where(mask0, do, zero_do), jnp.where(mask0, zero_do, do)], axis=0)

        tot_pair = tot_ref[...]
        tot = jnp.concatenate([jnp.max(jnp.where(mask0, tot_pair, 0.0), axis=1, keepdims=True),
                               jnp.max(jnp.where(mask0, 0.0, tot_pair), axis=1, keepdims=True)], axis=0)
        nb = jnp.clip(jnp.max(nb_ref[...]).astype(jnp.int32), 1, i + 1)
        first = i + 1 - nb

        def block(off, carry, diagonal):
            dq, pre_sp, pre_e = carry
            kj = k_ref[pl.ds(off, QB), :]
            vj = v_ref[pl.ds(off, QB), :].astype(_MXU_DTYPE)
            z = _mm(qh, kj, _NT) * scale
            sp = _softplus(z)
            a = z - sp
            sp_m = jnp.where(strict, sp, 0.0) if diagonal else sp
            incl = _mm_xl2(sp_m, u_le)
            w = jnp.exp(a - ((tot - pre_sp) - incl))
            if diagonal:
                w = jnp.where(strict, w, 0.0)
            e = w * _mm(doh, vj, _NT)
            db = pre_e + _mm_xl2(e, u_lt)
            dz = (e - jnp.exp(a) * (e + db)) * scale
            if diagonal:
                dz = jnp.where(strict, dz, 0.0)
            dk_ref[pl.ds(off, QB), :] += _mm(dz, qh, _TN)
            dv_ref[pl.ds(off, QB), :] += _mm(w, doh, _TN)
            return (dq + _mm(dz, kj), pre_sp + jnp.sum(sp_m, axis=1, keepdims=True),
                    pre_e + jnp.sum(e, axis=1, keepdims=True))

        zero_col = jnp.zeros((2 * QB, 1), F32)
        init = (jnp.zeros((2 * QB, LANES), F32), zero_col, zero_col)
        carry = lax.fori_loop(first, i, lambda j, cr: block(pl.multiple_of(j * QB, QB), cr, False), init)
        carry = block(pl.multiple_of(i * QB, QB), carry, True)
        dq_ref[...] = jnp.where(mask0, carry[0][:QB], carry[0][QB:])

    blk = pl.BlockSpec((QB, LANES), lambda hp, i: (i, hp))
    full = pl.BlockSpec((S, LANES), lambda hp, i: (0, hp))
    hbm = pl.BlockSpec(memory_space=pl.ANY)
    outs = pl.pallas_call(
        body, name="sb_bwd", grid=grid,
        in_specs=[blk, full, pl.BlockSpec((S, LANES), lambda hp, i: (0, C_SB_V // LANES + hp)), blk, blk,
                  pl.BlockSpec((8, LANES), lambda hp, i: (i, hp))] + [hbm] * (nx + ne),
        out_specs=[blk, full, full] + [hbm] * nx,
        out_shape=[_sds((S, SB_W), F32)] * 3 + (exchange.out_shape if exchange else []),
        scratch_shapes=(exchange.scratch if exchange else []) + (early.scratch if early else []),
        compiler_params=_cp(2),
    )(qn, kn, p, do, tot, nblocks, *ex_src, *early_src)
    return outs[:3], outs[3:]


def _dn_prep(p, conv_w, a_row, dtb_row, tm=256):
    S = p.shape[0]
    W3 = 3 * DN_W
    nhalo = tm // 8

    def body(x_ref, halo_ref, w_ref, ba_ref, a_ref, dtb_ref, qkv_ref, bb_ref, gc_ref, gl_ref):
        i = pl.program_id(0)
        halo = jnp.where(i > 0, halo_ref[...], 0.0)
        xf = jnp.concatenate([halo, x_ref[...]], axis=0)
        acc = jnp.zeros((tm, W3), F32)
        for k in range(CONV_K):
            sh = CONV_K - 1 - k
            xs = xf if sh == 0 else pltpu.roll(xf, sh, 0)
            acc = acc + xs[8:, :] * w_ref[k:k + 1, :]
        s = _silu(acc)
        for gi in range(2 * DN_HEADS):
            sl = slice(gi * LANES, (gi + 1) * LANES)
            sg = s[:, sl]
            rinv = lax.rsqrt(jnp.sum(sg * sg, axis=1, keepdims=True) + EPS)
            qkv_ref[:, sl] = sg * rinv * (DN_HD ** -0.5 if gi < DN_HEADS else 1.0)
        qkv_ref[:, 2 * DN_W:] = s[:, 2 * DN_W:]

        ba = ba_ref[...]
        beta = _sigmoid(ba)
        g = -jnp.exp(a_ref[...]) * _softplus(ba + dtb_ref[...])
        lr, lc = _iota2((LANES, DN_W), 0), _iota2((LANES, DN_W), 1)
        sel_b = (lr == lc // LANES).astype(BF16)
        sel_g = (lr == lc // LANES + DN_HEADS).astype(BF16)
        bb_ref[...] = _mm_xl(beta, sel_b)
        graw = _mm_xl(g, sel_g)
        rr, cc = _iota2((tm, tm), 0), _iota2((tm, tm), 1)
        tri = jnp.logical_and(rr >= cc, rr // CHUNK == cc // CHUNK).astype(BF16)
        gc = _mm_xr(tri, graw)
        last = (cc == (rr // CHUNK) * CHUNK + (CHUNK - 1)).astype(BF16)
        gc_ref[...] = gc
        gl_ref[...] = _mm_xr(last, gc)

    return pl.pallas_call(
        body, name="dn_prep", grid=(S // tm,),
        in_specs=[_rb(tm, W3, 0), pl.BlockSpec((8, W3), lambda i: (jnp.maximum(i * nhalo - 1, 0), 0)),
                  _fs((CONV_K, W3)), _rb(tm, LANES, (p.shape[1] - LANES) // LANES),
                  _fs((1, LANES)), _fs((1, LANES))],
        out_specs=[_rb(tm, W3), _rb(tm, DN_W), _rb(tm, DN_W), _rb(tm, DN_W)],
        out_shape=[_sds((S, W3), F32)] + [_sds((S, DN_W), F32)] * 3, compiler_params=_cp(1),
    )(p, p, conv_w, p, a_row, dtb_row)


def _heads(ref, base=0):
    return jnp.stack([ref[:, base + h * LANES:base + (h + 1) * LANES] for h in range(DN_HEADS)])


def _per_head(const):
    return jnp.broadcast_to(const[None], (DN_HEADS,) + const.shape)


def _dn_chunk_terms(q, k, v, beta, gc, gl):
    r, c = _iota2((CHUNK, CHUNK), 0), _iota2((CHUNK, CHUNK), 1)
    tril, strict = r >= c, r > c
    gcol = _mm_xl(gc, _per_head(jnp.full((LANES, CHUNK), 1.0 / LANES, F32)), _BNN)
    grow = _mm_xr(_per_head(jnp.full((CHUNK, LANES), 1.0 / LANES, F32)), gc, _BNT)
    dec = jnp.where(tril, jnp.exp(jnp.where(tril, gcol - grow, 0.0)), 0.0)
    gam = jnp.exp(gc)
    dlt = jnp.exp(gl - gc)
    kb, vb = k * beta, v * beta
    pm = _mm(kb, k, _BNT)
    qk = _mm(q, k, _BNT)
    m = jnp.where(strict, pm * dec, 0.0)
    a = jnp.where(tril, qk * dec, 0.0)
    return dict(tril=tril, strict=strict, dec=dec, gam=gam, dlt=dlt, kb=kb, vb=vb, m=m, a=a)


def _dn_fwd(qkv, bb, gcb, glb, gather=None, g_src=()):
    S = qkv.shape[0]
    N = S // CHUNK
    nx = len(g_src)

    def body(*refs):
        if gather is None:
            compute(*refs)
            return
        src, gout, sems = refs[4:4 + nx], refs[7 + nx:7 + 2 * nx], refs[8 + 2 * nx:]

        @pl.when(pl.program_id(0) == 0)
        def _():
            gather.start(src, gout, sems)

        compute(*refs[:4], *refs[4 + nx:7 + nx], refs[7 + 2 * nx])

        @pl.when(pl.program_id(0) == N - 1)
        def _():
            gather.wait(src, gout, sems)

    def compute(qkv_ref, bb_ref, gc_ref, gl_ref, o_ref, t_ref, sall_ref, s_scr):
        @pl.when(pl.program_id(0) == 0)
        def _():
            s_scr[...] = jnp.zeros_like(s_scr)

        r, c = _iota2((CHUNK, CHUNK), 0), _iota2((CHUNK, CHUNK), 1)
        eye = (r == c).astype(F32)
        q, k, v = _heads(qkv_ref), _heads(qkv_ref, DN_W), _heads(qkv_ref, 2 * DN_W)
        beta, gc, gl = _heads(bb_ref), _heads(gc_ref), _heads(gl_ref)
        s_prev = s_scr[...]
        sall_ref[0] = s_prev.astype(sall_ref.dtype)
        s0 = s_prev.astype(sall_ref.dtype).astype(F32)
        t = _dn_chunk_terms(q, k, v, beta, gc, gl)
        pw = -t["m"]
        tinv = eye + pw
        for _ in range(5):
            pw = _mm3(pw, pw, _BNN)
            tinv = tinv + _mm3(tinv, pw, _BNN)
        t_ref[...] = tinv
        u = _mm3(tinv, t["vb"], _BNN)
        w = _mm3(tinv, t["kb"] * t["gam"], _BNN)
        vn = u - _mm(w, s0, _BNN)
        o = _mm(q * t["gam"], s0, _BNN) + _mm(t["a"], vn, _BNN)
        for h in range(DN_HEADS):
            o_ref[:, h * LANES:(h + 1) * LANES] = o[h]
        egl = jnp.exp(jnp.concatenate([gl, gl], axis=1))
        s_scr[...] = s_prev * egl + _mm(k * t["dlt"], vn, _BTN)

    hbm = pl.BlockSpec(memory_space=pl.ANY)
    outs = pl.pallas_call(
        body, name="dn_fwd", grid=(N,),
        in_specs=[_rb(CHUNK, 3 * DN_W), _rb(CHUNK, DN_W), _rb(CHUNK, DN_W), _rb(CHUNK, DN_W)] + [hbm] * nx,
        out_specs=[_rb(CHUNK, DN_W), pl.BlockSpec((DN_HEADS, CHUNK, CHUNK), lambda n: (0, n, 0)),
                   pl.BlockSpec((1, DN_HEADS, DN_HD, DN_HD), lambda n: (n, 0, 0, 0))] + [hbm] * nx,
        out_shape=[_sds((S, DN_W), F32), _sds((DN_HEADS, S, CHUNK), F32),
                   _sds((N, DN_HEADS, DN_HD, DN_HD), _MXU_DTYPE)] + (gather.out_shape if gather else []),
        scratch_shapes=[pltpu.VMEM((DN_HEADS, DN_HD, DN_HD), F32)] + (gather.scratch if gather else []),
        compiler_params=_cp(1),
    )(qkv, bb, gcb, glb, *g_src)
    return outs[:3], outs[3:]


def _dn_bwd(qkv, bb, gcb, glb, tinv_all, sall, do, exchange=None, ex_src=(), early=None, early_src=()):
    S = qkv.shape[0]
    N = S // CHUNK
    nx = len(ex_src)
    NI = 7

    def body(*refs):
        if exchange is None:
            compute(*refs)
            return
        src, src2 = refs[NI:NI + nx], refs[NI + nx:NI + 2 * nx]
        o0 = NI + 2 * nx
        xout, ds_scr = refs[o0 + 3:o0 + 3 + nx], refs[o0 + 3 + nx]
        sems, sems2 = refs[o0 + 4 + nx:o0 + 7 + nx], refs[o0 + 7 + nx:]

        @pl.when(pl.program_id(0) == 0)
        def _():
            exchange.start(src, xout, sems)
            early.start(src2, xout, sems2)

        compute(*refs[:NI], *refs[o0:o0 + 3], ds_scr)

        @pl.when(pl.program_id(0) == N - 1)
        def _():
            exchange.wait(src, xout, sems)
            early.wait(src2, xout, sems2)

    def compute(qkv_ref, bb_ref, gc_ref, gl_ref, t_ref, sall_ref, do_ref, dqkv_ref, dbb_ref, dg_ref, ds_scr):
        @pl.when(pl.program_id(0) == 0)
        def _():
            ds_scr[...] = jnp.zeros_like(ds_scr)

        r, c = _iota2((CHUNK, CHUNK), 0), _iota2((CHUNK, CHUNK), 1)
        eye = (r == c).astype(F32)
        u_ge = (c >= r).astype(F32)
        last_row = _iota2((CHUNK, LANES), 0) == CHUNK - 1
        eye_h, u_ge_h = _per_head(eye), _per_head(u_ge)
        q, k, v = _heads(qkv_ref), _heads(qkv_ref, DN_W), _heads(qkv_ref, 2 * DN_W)
        beta, gc, gl = _heads(bb_ref), _heads(gc_ref), _heads(gl_ref)
        tinv = t_ref[...]
        s0 = sall_ref[0].astype(F32)
        do = _heads(do_ref)
        ds1 = ds_scr[...]
        t = _dn_chunk_terms(q, k, v, beta, gc, gl)
        gam, dlt, kb, vb, dec = t["gam"], t["dlt"], t["kb"], t["vb"], t["dec"]
        kbg = kb * gam
        u = _mm3(tinv, vb, _BNN)
        w = _mm3(tinv, kbg, _BNN)
        vn = u - _mm(w, s0, _BNN)
        qg, kd = q * gam, k * dlt
        egl = jnp.exp(gl)
        egl2 = jnp.concatenate([egl, egl], axis=1)

        dvn = _mm(t["a"], do, _BTN) + _mm(kd, ds1, _BNN)
        da = jnp.where(t["tril"], _mm(do, vn, _BNT), 0.0)
        dqg = _mm(do, s0, _BNT)
        dkd = _mm(vn, ds1, _BNT)
        dw = -_mm(dvn, s0, _BNT)
        ds_scr[...] = _mm(qg, do, _BTN) + egl2 * ds1 - _mm(w, dvn, _BTN)
        tt = _mm_xr(eye_h, tinv, _BNT)
        dvb = _mm3(tt, dvn, _BNN)
        dkbg = _mm3(tt, dw, _BNN)
        dm = -jnp.where(t["strict"], _mm(dvb, u, _BNT) + _mm(dkbg, w, _BNT), 0.0)
        dpm = dm * dec
        dqk = da * dec
        dkb = dkbg * gam + _mm(dpm, k, _BNN)
        dk = dkd * dlt + _mm(dpm, kb, _BTN) + _mm(dqk, q, _BTN) + dkb * beta
        dq = dqg * gam + _mm(dqk, k, _BNN)
        dv = dvb * beta
        dbeta = jnp.sum(dkb * k, axis=2, keepdims=True) + jnp.sum(dvb * v, axis=2, keepdims=True)
        dgam = jnp.sum(dqg * q, axis=2, keepdims=True) + jnp.sum(dkbg * kb, axis=2, keepdims=True)
        ddlt = jnp.sum(dkd * k, axis=2, keepdims=True)
        xm = dm * t["m"] + da * t["a"]
        xt = _mm_xr(eye_h, xm, _BNT)
        dgc = (dgam * gam - ddlt * dlt + jnp.sum(xm, axis=2, keepdims=True) - jnp.sum(xt, axis=2, keepdims=True))
        dgl = jnp.sum(ddlt * dlt, axis=1, keepdims=True) + jnp.sum(
            jnp.sum(ds1 * s0, axis=2, keepdims=True), axis=1, keepdims=True) * jnp.max(egl, axis=1, keepdims=True)
        dgc = dgc + jnp.where(last_row, dgl, 0.0)
        dg = _mm_xr(u_ge_h, dgc, _BNN)
        for h in range(DN_HEADS):
            sl = slice(h * LANES, (h + 1) * LANES)
            dqkv_ref[:, sl] = dq[h]
            dqkv_ref[:, DN_W + h * LANES:DN_W + (h + 1) * LANES] = dk[h]
            dqkv_ref[:, 2 * DN_W + h * LANES:2 * DN_W + (h + 1) * LANES] = dv[h]
            dbb_ref[:, sl] = jnp.broadcast_to(dbeta[h], (CHUNK, LANES))
            dg_ref[:, sl] = dg[h]

    rev = lambda w: pl.BlockSpec((CHUNK, w), lambda n: (N - 1 - n, 0))
    hbm = pl.BlockSpec(memory_space=pl.ANY)
    outs = pl.pallas_call(
        body, name="dn_bwd", grid=(N,),
        in_specs=[rev(3 * DN_W), rev(DN_W), rev(DN_W), rev(DN_W),
                  pl.BlockSpec((DN_HEADS, CHUNK, CHUNK), lambda n: (0, N - 1 - n, 0)),
                  pl.BlockSpec((1, DN_HEADS, DN_HD, DN_HD), lambda n: (N - 1 - n, 0, 0, 0)), rev(DN_W)]
        + [hbm] * (2 * nx),
        out_specs=[rev(3 * DN_W), rev(DN_W), rev(DN_W)] + [hbm] * nx,
        out_shape=[_sds((S, 3 * DN_W), F32), _sds((S, DN_W), F32), _sds((S, DN_W), F32)]
        + (exchange.out_shape if exchange else []),
        scratch_shapes=[pltpu.VMEM((DN_HEADS, DN_HD, DN_HD), F32)]
        + (exchange.scratch + early.scratch if exchange else []),
        compiler_params=_cp(1),
    )(qkv, bb, gcb, glb, tinv_all, sall, do, *ex_src, *early_src)
    return outs[:3], list(outs[3:])


def _dn_prep_bwd_a(p, dqkv, dbb, dgb, conv_w, a_row, dtb_row, tm=256):
    S, PC = p.shape
    W3 = 3 * DN_W
    nhalo = tm // 8

    def body(x_ref, halo_ref, w_ref, ba_ref, a_ref, dtb_ref, dqkv_ref, dbb_ref, dgb_ref,
             dc_ref, dba_ref, dal_ref, ddt_ref):
        i = pl.program_id(0)

        @pl.when(i == 0)
        def _():
            dal_ref[...] = jnp.zeros_like(dal_ref)
            ddt_ref[...] = jnp.zeros_like(ddt_ref)

        halo = jnp.where(i > 0, halo_ref[...], 0.0)
        xf = jnp.concatenate([halo, x_ref[...]], axis=0)
        acc = jnp.zeros((tm, W3), F32)
        for k in range(CONV_K):
            sh = CONV_K - 1 - k
            xs = xf if sh == 0 else pltpu.roll(xf, sh, 0)
            acc = acc + xs[8:, :] * w_ref[k:k + 1, :]
        s = _silu(acc)
        ds_act = _dsilu(acc)
        for gi in range(2 * DN_HEADS):
            sl = slice(gi * LANES, (gi + 1) * LANES)
            sg = s[:, sl]
            rinv = lax.rsqrt(jnp.sum(sg * sg, axis=1, keepdims=True) + EPS)
            nh = sg * rinv
            dn = dqkv_ref[:, sl] * (DN_HD ** -0.5 if gi < DN_HEADS else 1.0)
            dsg = rinv * (dn - nh * jnp.sum(dn * nh, axis=1, keepdims=True))
            dc_ref[:, sl] = dsg * ds_act[:, sl]
        dc_ref[:, 2 * DN_W:] = dqkv_ref[:, 2 * DN_W:] * ds_act[:, 2 * DN_W:]

        ba = ba_ref[...]
        beta = _sigmoid(ba)
        ea = jnp.exp(a_ref[...])
        pre = ba + dtb_ref[...]
        g = -ea * _softplus(pre)
        lr, lc = _iota2((DN_W, LANES), 0), _iota2((DN_W, LANES), 1)
        pick_b = jnp.where(lc == lr // LANES, 1.0 / LANES, 0.0)
        pick_g = jnp.where(lc == lr // LANES + DN_HEADS, 1.0 / LANES, 0.0)
        dbeta = _mm_xl(dbb_ref[...], pick_b)
        dg = _mm_xl(dgb_ref[...], pick_g)
        lane = _iota2((1, LANES), 1)
        da = dg * (-ea) * _sigmoid(pre)
        dba_ref[...] = jnp.where(lane < DN_HEADS, dbeta * beta * (1.0 - beta),
                                 jnp.where(lane < 2 * DN_HEADS, da, 0.0)).astype(dba_ref.dtype)
        dal_ref[...] += jnp.sum(dg * g, axis=0, keepdims=True)
        ddt_ref[...] += jnp.sum(da, axis=0, keepdims=True)

    return pl.pallas_call(
        body, name="dn_prep_bwd_a", grid=(S // tm,),
        in_specs=[_rb(tm, W3, 0), pl.BlockSpec((8, W3), lambda i: (jnp.maximum(i * nhalo - 1, 0), 0)),
                  _fs((CONV_K, W3)), _rb(tm, LANES, (PC - LANES) // LANES), _fs((1, LANES)), _fs((1, LANES)),
                  _rb(tm, W3), _rb(tm, DN_W), _rb(tm, DN_W)],
        out_specs=[_rb(tm, W3), _rb(tm, LANES), _fs((1, LANES)), _fs((1, LANES))],
        out_shape=[_sds((S, W3), F32), _sds((S, LANES), _MXU_DTYPE), _sds((1, LANES), F32), _sds((1, LANES), F32)],
        compiler_params=_cp(1),
    )(p, p, conv_w, p, a_row, dtb_row, dqkv, dbb, dgb)


def _dn_prep_bwd_b(p, dc, conv_w, tm=256):
    S = p.shape[0]
    W3 = 3 * DN_W
    nhalo = tm // 8
    nblk = S // tm

    def body(x_ref, xh_ref, dc_ref, dch_ref, w_ref, dx_ref, dw_ref):
        i = pl.program_id(0)

        @pl.when(i == 0)
        def _():
            dw_ref[...] = jnp.zeros_like(dw_ref)

        dcv = dc_ref[...]
        xf = jnp.concatenate([jnp.where(i > 0, xh_ref[...], 0.0), x_ref[...]], axis=0)
        df = jnp.concatenate([dcv, jnp.where(i < nblk - 1, dch_ref[...], 0.0)], axis=0)
        acc = jnp.zeros((tm, W3), F32)
        for k in range(CONV_K):
            sh = CONV_K - 1 - k
            xs = xf if sh == 0 else pltpu.roll(xf, sh, 0)
            dw_ref[k:k + 1, :] += jnp.sum(dcv * xs[8:, :], axis=0, keepdims=True)
            ds = df if sh == 0 else pltpu.roll(df, tm + 8 - sh, 0)
            acc = acc + ds[:tm, :] * w_ref[k:k + 1, :]
        dx_ref[...] = acc.astype(dx_ref.dtype)

    return pl.pallas_call(
        body, name="dn_prep_bwd_b", grid=(nblk,),
        in_specs=[_rb(tm, W3, 0), pl.BlockSpec((8, W3), lambda i: (jnp.maximum(i * nhalo - 1, 0), 0)),
                  _rb(tm, W3), pl.BlockSpec((8, W3), lambda i: (jnp.minimum((i + 1) * nhalo, S // 8 - 1), 0)),
                  _fs((CONV_K, W3))],
        out_specs=[_rb(tm, W3), _fs((CONV_K, W3))],
        out_shape=[_sds((S, W3), _MXU_DTYPE), _sds((CONV_K, W3), F32)], compiler_params=_cp(1),
    )(p, p, dc, dc, conv_w)


def _gate(o_att, o_dn, p, gn, tm=256):
    S = p.shape[0]

    def body(oa_ref, zs_ref, od_ref, zd_ref, gn_ref, osb_ref, odn_ref):
        osb_ref[...] = (oa_ref[...] * _silu(zs_ref[...])).astype(osb_ref.dtype)
        for h in range(DN_HEADS):
            sl = slice(h * LANES, (h + 1) * LANES)
            o = od_ref[:, sl]
            r = lax.rsqrt(jnp.mean(o * o, axis=1, keepdims=True) + EPS)
            odn_ref[:, sl] = (o * r * gn_ref[...] * _silu(zd_ref[:, sl])).astype(odn_ref.dtype)

    return pl.pallas_call(
        body, name="gate", grid=(S // tm,),
        in_specs=[_rb(tm, SB_W), _rb(tm, SB_W, C_SB_Z // SB_W), _rb(tm, DN_W), _rb(tm, DN_W, C_DN_Z // DN_W),
                  _fs((1, LANES))],
        out_specs=[_rb(tm, SB_W), _rb(tm, DN_W)],
        out_shape=[_sds((S, SB_W), _MXU_DTYPE), _sds((S, DN_W), _MXU_DTYPE)], compiler_params=_cp(1),
    )(o_att, p, o_dn, p, gn)


def _gate_bwd(db_sb, db_dn, wb_sb, wb_dn, o_att, o_dn, p, gn, tm=256):
    S = p.shape[0]
    D = db_sb.shape[1]

    def body(dbs_ref, dbd_ref, ws_ref, wd_ref, oa_ref, zs_ref, od_ref, zd_ref, gn_ref,
             doa_ref, dzs_ref, dod_ref, dzd_ref, dgn_ref):
        @pl.when(pl.program_id(0) == 0)
        def _():
            dgn_ref[...] = jnp.zeros_like(dgn_ref)

        do_sb = _mm(dbs_ref[...], ws_ref[...], _NT)
        zs = zs_ref[...]
        doa_ref[...] = do_sb * _silu(zs)
        dzs_ref[...] = (do_sb * oa_ref[...] * _dsilu(zs)).astype(dzs_ref.dtype)
        do_dnn = _mm(dbd_ref[...], wd_ref[...], _NT)
        gnv = gn_ref[...]
        for h in range(DN_HEADS):
            sl = slice(h * LANES, (h + 1) * LANES)
            o, z, dout = od_ref[:, sl], zd_ref[:, sl], do_dnn[:, sl]
            r = lax.rsqrt(jnp.mean(o * o, axis=1, keepdims=True) + EPS)
            oh = o * r
            sz = _silu(z)
            dzd_ref[:, sl] = (dout * oh * gnv * _dsilu(z)).astype(dzd_ref.dtype)
            dgn_ref[...] += jnp.sum(dout * sz * oh, axis=0, keepdims=True)
            doh = dout * gnv * sz
            dod_ref[:, sl] = r * (doh - oh * jnp.mean(doh * oh, axis=1, keepdims=True))

    return pl.pallas_call(
        body, name="gate_bwd", grid=(S // tm,),
        in_specs=[_rb(tm, D), _rb(tm, D), _fs((SB_W, D)), _fs((DN_W, D)), _rb(tm, SB_W),
                  _rb(tm, SB_W, C_SB_Z // SB_W), _rb(tm, DN_W), _rb(tm, DN_W, C_DN_Z // DN_W), _fs((1, LANES))],
        out_specs=[_rb(tm, SB_W), _rb(tm, SB_W), _rb(tm, DN_W), _rb(tm, DN_W), _fs((1, LANES))],
        out_shape=[_sds((S, SB_W), F32), _sds((S, SB_W), _MXU_DTYPE), _sds((S, DN_W), F32),
                   _sds((S, DN_W), _MXU_DTYPE), _sds((1, LANES), F32)],
        compiler_params=_cp(1),
    )(db_sb, db_dn, wb_sb, wb_dn, o_att, p, o_dn, p, gn)


def _branch(o_sb, o_dnn, wb_sb, wb_dn, p, D, tm=256):
    S = p.shape[0]

    def body(os_ref, od_ref, ws_ref, wd_ref, ms_ref, md_ref, y_ref, bs_ref, bd_ref):
        bs = _mm(os_ref[...], ws_ref[...])
        bdn = _mm(od_ref[...], wd_ref[...])
        bs_ref[...] = bs
        bd_ref[...] = bdn
        y_ref[...] = (_sigmoid(ms_ref[...]) * bs + _sigmoid(md_ref[...]) * bdn).astype(y_ref.dtype)

    return pl.pallas_call(
        body, name="branch", grid=(S // tm,),
        in_specs=[_rb(tm, SB_W), _rb(tm, DN_W), _fs((SB_W, D)), _fs((DN_W, D)),
                  _rb(tm, D, C_MG // D), _rb(tm, D, C_MG // D + 1)],
        out_specs=[_rb(tm, D), _rb(tm, D), _rb(tm, D)],
        out_shape=[_sds((S, D), _MXU_DTYPE), _sds((S, D), F32), _sds((S, D), F32)], compiler_params=_cp(1),
    )(o_sb, o_dnn, wb_sb, wb_dn, p, p)


def _out_proj(x, y, w_out, gate, tm=256):
    S, D = x.shape

    def body(x_ref, y_ref, w_ref, g_ref, xn_ref, out_ref):
        out = _mm(y_ref[...], w_ref[...])
        out_ref[...] = out
        xn_ref[...] = x_ref[...] + g_ref[...] * out

    return pl.pallas_call(
        body, name="out_proj", grid=(S // tm,),
        in_specs=[_rb(tm, D), _rb(tm, D), _fs((D, D)), _fs((1, D))],
        out_specs=[_rb(tm, D), _rb(tm, D)],
        out_shape=[_sds((S, D), F32), _sds((S, D), F32)], compiler_params=_cp(1),
    )(x, y, w_out, gate)


def _out_bwd(dxn, out, gate, w_out, p, b_sb, b_dn, tm=256):
    S, D = dxn.shape

    def body(dxn_ref, out_ref, g_ref, w_ref, ms_ref, md_ref, bs_ref, bd_ref,
             dout_ref, dbs_ref, dbd_ref, dm_ref, dgate_ref):
        @pl.when(pl.program_id(0) == 0)
        def _():
            dgate_ref[...] = jnp.zeros_like(dgate_ref)

        dxv = dxn_ref[...]
        dgate_ref[...] += jnp.sum(dxv * out_ref[...], axis=0, keepdims=True)
        dout = (g_ref[...] * dxv).astype(dout_ref.dtype)
        dout_ref[...] = dout
        dy = _mm(dout, w_ref[...], _NT)
        s1, s2 = _sigmoid(ms_ref[...]), _sigmoid(md_ref[...])
        dbs_ref[...] = (dy * s1).astype(dbs_ref.dtype)
        dbd_ref[...] = (dy * s2).astype(dbd_ref.dtype)
        dm_ref[:, :D] = (dy * bs_ref[...] * s1 * (1.0 - s1)).astype(dm_ref.dtype)
        dm_ref[:, D:] = (dy * bd_ref[...] * s2 * (1.0 - s2)).astype(dm_ref.dtype)

    return pl.pallas_call(
        body, name="out_bwd", grid=(S // tm,),
        in_specs=[_rb(tm, D), _rb(tm, D), _fs((1, D)), _fs((D, D)), _rb(tm, D, C_MG // D),
                  _rb(tm, D, C_MG // D + 1), _rb(tm, D), _rb(tm, D)],
        out_specs=[_rb(tm, D), _rb(tm, D), _rb(tm, D), _rb(tm, 2 * D), _fs((1, D))],
        out_shape=[_sds((S, D), _MXU_DTYPE)] * 3 + [_sds((S, 2 * D), _MXU_DTYPE), _sds((1, D), F32)],
        compiler_params=_cp(1),
    )(dxn, out, gate, w_out, p, p, b_sb, b_dn)


def _loss_head(xf, target, tm=256):
    S, D = xf.shape

    def body(x_ref, t_ref, dy_ref, loss_ref):
        @pl.when(pl.program_id(0) == 0)
        def _():
            loss_ref[...] = jnp.zeros_like(loss_ref)

        e = x_ref[...] - t_ref[...]
        dy_ref[...] = e * (1.0 / D)
        row = jnp.sum(e * e, axis=1, keepdims=True) * (1.0 / D)
        loss_ref[...] += 0.5 * jnp.sum(row, axis=0, keepdims=True)

    return pl.pallas_call(
        body, name="loss_head", grid=(S // tm,),
        in_specs=[_rb(tm, D), _rb(tm, D)], out_specs=[_rb(tm, D), _fs((1, LANES))],
        out_shape=[_sds((S, D), F32), _sds((1, LANES), F32)], compiler_params=_cp(1),
    )(xf, target)


def _ada_fwd(c_all, ada_w, ada_b_sh):
    L, D, n = ada_w.shape
    B = c_all.shape[0]

    def body(c_ref, w_ref, b_ref, o_ref):
        sc = _silu(c_ref[...])
        o_ref[0] = _mm(sc, w_ref[0]) + b_ref[0]

    return pl.pallas_call(
        body, name="ada_fwd", grid=(L,),
        in_specs=[_fs((B, D)), pl.BlockSpec((1, D, n), lambda l: (l, 0, 0)), pl.BlockSpec((1, 1, n), lambda l: (l, 0, 0))],
        out_specs=pl.BlockSpec((1, B, n), lambda l: (l, 0, 0)),
        out_shape=_sds((L, B, n), F32), compiler_params=_cp(1),
    )(c_all, ada_w, ada_b_sh)


def _ada_bwd(c_all_t, dmod_sh):
    D, B = c_all_t.shape
    L, _, n = dmod_sh.shape

    def body(c_ref, d_ref, o_ref):
        acc = jnp.zeros((D, n), F32)
        for b in range(B):
            acc = acc + _silu(c_ref[:, b:b + 1]) * d_ref[0, b:b + 1, :]
        o_ref[0] = acc

    return pl.pallas_call(
        body, name="ada_bwd", grid=(L,),
        in_specs=[_fs((D, B)), pl.BlockSpec((1, B, n), lambda l: (l, 0, 0))],
        out_specs=pl.BlockSpec((1, D, n), lambda l: (l, 0, 0)),
        out_shape=_sds((L, D, n), F32), compiler_params=_cp(1),
    )(c_all_t, dmod_sh)


def _sum_parts(name, parts):
    P, R, C = parts.shape
    tr = _pick(R, max(16, min(512, (1 << 19) // (P * C))), 16) if R % 16 == 0 else R

    def body(p_ref, o_ref):
        acc = p_ref[0].astype(F32)
        for k in range(1, P):
            acc = acc + p_ref[k].astype(F32)
        o_ref[...] = acc

    return pl.pallas_call(
        body, name=name, grid=(R // tr,),
        in_specs=[pl.BlockSpec((P, tr, C), lambda i: (0, i, 0))], out_specs=_rb(tr, C),
        out_shape=_sds((R, C), F32), compiler_params=_cp(1),
    )(parts)


def _adamw(name, w, g, m, v):
    L, R, C = w.shape
    tr = _pick(R, 256, 8) if R % 8 == 0 else R
    c1 = 1.0 - ADAM_B1 ** ADAM_STEP
    c2 = 1.0 - ADAM_B2 ** ADAM_STEP

    def body(w_ref, g_ref, m_ref, v_ref, d_ref, mo_ref, vo_ref):
        gv = g_ref[...]
        mn = ADAM_B1 * m_ref[...] + (1.0 - ADAM_B1) * gv
        vn = ADAM_B2 * v_ref[...] + (1.0 - ADAM_B2) * (gv * gv)
        mo_ref[...] = mn
        vo_ref[...] = vn
        d_ref[...] = -ADAM_LR * ((mn / c1) / (jnp.sqrt(vn / c2) + ADAM_EPS) + ADAM_WD * w_ref[...])

    spec = pl.BlockSpec((1, tr, C), lambda l, i: (l, i, 0))
    return pl.pallas_call(
        body, name=name, grid=(L, R // tr),
        in_specs=[spec] * 4, out_specs=[spec] * 3, out_shape=[_sds((L, R, C), F32)] * 3, compiler_params=_cp(2),
    )(w, g, m, v)


def _ag_small(name, blk):
    R, C = blk.shape

    def body(x_ref, out_ref, send_sems, recv_sems, local_sem):
        x, y, c = lax.axis_index("x"), lax.axis_index("y"), lax.axis_index("c")
        me, sibling = (x, y, c), (x, y, 1 - c)
        chips = [(1 - x, y), (x, 1 - y), (1 - x, 1 - y)]

        def rows(px, py, pc):
            return out_ref.at[pl.ds((4 * px + 2 * py + pc) * R, R), :]

        def copy(k, block, to, src=None):
            return pltpu.make_async_remote_copy(
                src_ref=rows(*block) if src is None else src, dst_ref=rows(*block),
                send_sem=send_sems.at[k], recv_sem=recv_sems.at[k], device_id=to, device_id_type=MESH)

        mine = pltpu.make_async_copy(x_ref, rows(*me), local_sem)
        mine.start()
        first = [copy(0, me, sibling, src=x_ref)]
        first += [copy(1 + j, me, (*chip, c), src=x_ref) for j, chip in enumerate(chips)]
        for cp in first:
            cp.start()
        passed = [copy(4 + j, (*chip, c), sibling) for j, chip in enumerate(chips)]
        for j, chip in enumerate(chips):
            copy(1 + j, (*chip, c), me).wait_recv()
            passed[j].start()
        copy(0, sibling, me).wait_recv()
        for j, chip in enumerate(chips):
            copy(4 + j, (*chip, 1 - c), me).wait_recv()
        for cp in first + passed:
            cp.wait_send()
        mine.wait()

    return pl.pallas_call(
        body, name=name, out_shape=_sds((8 * R, C), blk.dtype),
        in_specs=[pl.BlockSpec(memory_space=pltpu.VMEM)], out_specs=pl.BlockSpec(memory_space=pltpu.VMEM),
        scratch_shapes=[pltpu.SemaphoreType.DMA((7,)), pltpu.SemaphoreType.DMA((7,)), pltpu.SemaphoreType.DMA],
    )(blk)


def _row_chunks(ts, row_axis):
    pieces = []
    for t, a in enumerate(ts):
        rows = a.shape[row_axis]
        n = 4 if rows >= 1024 else 1
        pieces += [(t, i * (rows // n), rows // n) for i in range(n)]
    return pieces


def _ag_weights_first(ts):
    nt = len(ts)
    pieces = _row_chunks(ts, 0)
    NP = len(pieces)
    sizes = [nr * ts[t].shape[1] for t, _, nr in pieces]
    split = next(pi for pi in range(NP + 1) if 2 * sum(sizes[:pi]) >= sum(sizes))

    def body(*refs):
        w, out = refs[:nt], refs[nt:2 * nt]
        send_sems, recv_sems, local_sems = refs[2 * nt:]
        x, y, c = lax.axis_index("x"), lax.axis_index("y"), lax.axis_index("c")
        me, sibling = (x, y, c), (x, y, 1 - c)
        mine = 2 * x + y
        chips = [(1 - x, y), (x, 1 - y), (1 - x, 1 - y)]

        def blk(t, shard, r0, nr):
            return out[t].at[shard, r0:r0 + nr, :]

        def copy(k, dst, to, src=None):
            return pltpu.make_async_remote_copy(
                src_ref=dst if src is None else src, dst_ref=dst, send_sem=send_sems.at[k], recv_sem=recv_sems.at[k],
                device_id=to, device_id_type=MESH)

        own = [pltpu.make_async_copy(w[t], out[t].at[mine], local_sems.at[t]) for t in range(nt)]
        for cp in own:
            cp.start()
        for fetcher, lo, hi in ((0, 0, split), (1, split, NP)):
            @pl.when(c == fetcher)
            def _(lo=lo, hi=hi):
                sent = []
                for j, chip in enumerate(chips):
                    for pi in range(lo, hi):
                        t, r0, nr = pieces[pi]
                        sent.append(copy(j * NP + pi, blk(t, mine, r0, nr), (*chip, c), src=w[t].at[r0:r0 + nr, :]))
                        sent[-1].start()
                for j, chip in enumerate(chips):
                    theirs = 2 * chip[0] + chip[1]
                    for pi in range(lo, hi):
                        t, r0, nr = pieces[pi]
                        copy(j * NP + pi, blk(t, theirs, r0, nr), me).wait_recv()
                        sent.append(copy((3 + j) * NP + pi, blk(t, theirs, r0, nr), sibling))
                        sent[-1].start()
                for cp in sent:
                    cp.wait_send()

            @pl.when(c != fetcher)
            def _(lo=lo, hi=hi):
                for j, chip in enumerate(chips):
                    theirs = 2 * chip[0] + chip[1]
                    for pi in range(lo, hi):
                        t, r0, nr = pieces[pi]
                        copy((3 + j) * NP + pi, blk(t, theirs, r0, nr), me).wait_recv()

        for cp in own:
            cp.wait()

    return pl.pallas_call(
        body, name="ag_weights_first", out_shape=[_sds((4,) + a.shape, a.dtype) for a in ts],
        in_specs=[pl.BlockSpec(memory_space=pl.ANY)] * nt, out_specs=[pl.BlockSpec(memory_space=pltpu.VMEM)] * nt,
        scratch_shapes=[pltpu.SemaphoreType.DMA((6 * NP,)), pltpu.SemaphoreType.DMA((6 * NP,)),
                        pltpu.SemaphoreType.DMA((nt,))],
        compiler_params=pltpu.CompilerParams(vmem_limit_bytes=_VMEM_LIMIT),
    )(*ts)


class _WeightGather:
    def __init__(self, ts):
        self.nt = len(ts)
        self.pieces = _row_chunks(ts, 0)
        NP = len(self.pieces)
        self.out_shape = [_sds((4,) + a.shape, a.dtype) for a in ts]
        self.scratch = [pltpu.SemaphoreType.DMA((3 * NP,)), pltpu.SemaphoreType.DMA((3 * NP,)),
                        pltpu.SemaphoreType.DMA((self.nt,))]

    def _copies(self, src, out, sems):
        send_sems, recv_sems, local_sems = sems
        NP = len(self.pieces)
        x, y, c = lax.axis_index("x"), lax.axis_index("y"), lax.axis_index("c")
        mine = 2 * x + y
        own = [pltpu.make_async_copy(src[t], out[t].at[mine], local_sems.at[t]) for t in range(self.nt)]
        sends, recvs = [], []
        for j, chip in enumerate([(1 - x, y), (x, 1 - y), (1 - x, 1 - y)]):
            theirs = 2 * chip[0] + chip[1]
            for pi, (t, r0, nr) in enumerate(self.pieces):
                idx = j * NP + pi
                sends.append(pltpu.make_async_remote_copy(
                    src_ref=src[t].at[r0:r0 + nr, :], dst_ref=out[t].at[mine, r0:r0 + nr, :],
                    send_sem=send_sems.at[idx], recv_sem=recv_sems.at[idx], device_id=(*chip, c), device_id_type=MESH))
                recvs.append(pltpu.make_async_remote_copy(
                    src_ref=out[t].at[theirs, r0:r0 + nr, :], dst_ref=out[t].at[theirs, r0:r0 + nr, :],
                    send_sem=send_sems.at[idx], recv_sem=recv_sems.at[idx], device_id=(x, y, c), device_id_type=MESH))
        return own, sends, recvs

    def start(self, src, out, sems):
        own, sends, _ = self._copies(src, out, sems)
        for cp in own + sends:
            cp.start()

    def wait(self, src, out, sems):
        own, sends, recvs = self._copies(src, out, sems)
        for cp in recvs:
            cp.wait_recv()
        for cp in sends:
            cp.wait_send()
        for cp in own:
            cp.wait()


class _GradExchange:
    def __init__(self, ts, layer):
        self.nt, self.layer = len(ts), layer
        self.pieces = _row_chunks(ts, 1)
        NP = len(self.pieces)
        self.out_shape = [_sds((8,) + a.shape[1:], a.dtype) for a in ts]
        self.scratch = [pltpu.SemaphoreType.DMA((7 * NP,)), pltpu.SemaphoreType.DMA((7 * NP,)),
                        pltpu.SemaphoreType.DMA((self.nt,))]

    def _copies(self, src, out, sems):
        send_sems, recv_sems, local_sems = sems
        NP = len(self.pieces)
        x, y, c = lax.axis_index("x"), lax.axis_index("y"), lax.axis_index("c")
        me = 4 * x + 2 * y + c
        owner = c == self.layer
        own = [pltpu.make_async_copy(src[t].at[2 * x + y], out[t].at[me], local_sems.at[t]) for t in range(self.nt)]
        rel = []
        for k in range(1, 8):
            px = 1 - x if k & 4 else x
            py = 1 - y if k & 2 else y
            source = 4 * px + 2 * py + (1 - c if k & 1 else c)
            sends, recvs = [], []
            for pi, (t, r0, nr) in enumerate(self.pieces):
                idx = (k - 1) * NP + pi
                sends.append(pltpu.make_async_remote_copy(
                    src_ref=src[t].at[2 * px + py, r0:r0 + nr, :], dst_ref=out[t].at[me, r0:r0 + nr, :],
                    send_sem=send_sems.at[idx], recv_sem=recv_sems.at[idx], device_id=(px, py, self.layer),
                    device_id_type=MESH))
                recvs.append(pltpu.make_async_remote_copy(
                    src_ref=out[t].at[source, r0:r0 + nr, :], dst_ref=out[t].at[source, r0:r0 + nr, :],
                    send_sem=send_sems.at[idx], recv_sem=recv_sems.at[idx], device_id=(x, y, c),
                    device_id_type=MESH))
            rel.append((jnp.logical_not(owner) if k & 1 else owner, sends, recvs))
        return owner, own, rel

    def start(self, src, out, sems):
        owner, own, rel = self._copies(src, out, sems)

        @pl.when(owner)
        def _():
            for cp in own:
                cp.start()

        for sending, sends, _ in rel:
            @pl.when(sending)
            def _(sends=sends):
                for cp in sends:
                    cp.start()

    def wait(self, src, out, sems):
        owner, own, rel = self._copies(src, out, sems)

        @pl.when(owner)
        def _():
            for _, _, recvs in rel:
                for cp in recvs:
                    cp.wait_recv()
            for cp in own:
                cp.wait()

        for sending, sends, _ in rel:
            @pl.when(sending)
            def _(sends=sends):
                for cp in sends:
                    cp.wait_send()


def _sibling_join(ts):
    nt = len(ts)
    pieces = _row_chunks(ts, 0)
    NP = len(pieces)

    def body(*refs):
        src, out = refs[:nt], refs[nt:2 * nt]
        send_sems, recv_sems, local_sems = refs[2 * nt:]
        x, y, c = lax.axis_index("x"), lax.axis_index("y"), lax.axis_index("c")
        own = [pltpu.make_async_copy(src[t], out[t].at[c], local_sems.at[t]) for t in range(nt)]
        for cp in own:
            cp.start()
        sent = []
        for pi, (t, r0, nr) in enumerate(pieces):
            sent.append(pltpu.make_async_remote_copy(
                src_ref=src[t].at[r0:r0 + nr, :], dst_ref=out[t].at[c, r0:r0 + nr, :], send_sem=send_sems.at[pi],
                recv_sem=recv_sems.at[pi], device_id=(x, y, 1 - c), device_id_type=MESH))
            sent[-1].start()
        for pi, (t, r0, nr) in enumerate(pieces):
            pltpu.make_async_remote_copy(
                src_ref=src[t].at[r0:r0 + nr, :], dst_ref=out[t].at[1 - c, r0:r0 + nr, :], send_sem=send_sems.at[pi],
                recv_sem=recv_sems.at[pi], device_id=(x, y, c), device_id_type=MESH).wait_recv()
        for cp in sent:
            cp.wait_send()
        for cp in own:
            cp.wait()

    vmem = pl.BlockSpec(memory_space=pltpu.VMEM)
    return pl.pallas_call(
        body, name="sibling_join", out_shape=[_sds((2,) + a.shape, a.dtype) for a in ts],
        in_specs=[vmem] * nt, out_specs=[vmem] * nt,
        scratch_shapes=[pltpu.SemaphoreType.DMA((NP,)), pltpu.SemaphoreType.DMA((NP,)),
                        pltpu.SemaphoreType.DMA((nt,))],
        compiler_params=pltpu.CompilerParams(vmem_limit_bytes=_VMEM_LIMIT),
    )(*ts)


def _late_weights(gathered):
    g_bs, g_bd, g_out = gathered
    cat = lambda g, axis: jnp.concatenate([g[s] for s in range(4)], axis=axis)
    return dict(wb_sb=cat(g_bs, 1), wb_dn=cat(g_bd, 1), w_out=cat(g_out, 0))


def _layer_fwd(x, shift, scale, gate, lw, next_shards=None, late_shards=None):
    D = x.shape[1]
    h = _norm_mod(x, lw["norm_g"], scale, shift)
    if late_shards is None:
        p = _matmul("in_proj", h, lw["w_cat"], "nn", F32, tm_cap=1024, tn_cap=896)
    else:
        p, late = _matmul("in_proj", h, lw["w_cat"], "nn", F32, tm_cap=1024, tn_cap=896,
                          exchange=_WeightGather(late_shards), ex_src=late_shards)
        lw = {**lw, **_late_weights(late)}
    qn, kn = _sb_prep(p, lw["gq_t"], lw["gk_t"])
    o_att, tot, nblocks = _sb_fwd(qn, kn, p)
    qkv, bb, gcb, glb = _dn_prep(p, lw["conv_w"], lw["a_row"], lw["dtb_row"])
    if next_shards is None:
        (o_dn, tinv, sall), gathered = _dn_fwd(qkv, bb, gcb, glb)
    else:
        (o_dn, tinv, sall), gathered = _dn_fwd(qkv, bb, gcb, glb, _WeightGather(next_shards), next_shards)
    o_sb, o_dnn = _gate(o_att, o_dn, p, lw["gn"])
    y, b_sb, b_dn = _branch(o_sb, o_dnn, lw["wb_sb"], lw["wb_dn"], p, D)
    x_next, out = _out_proj(x, y, lw["w_out"], gate)
    res = dict(x=x, h=h, p=p, qn=qn, kn=kn, o_att=o_att, tot=tot, nblocks=nblocks, qkv=qkv, bb=bb, gcb=gcb, glb=glb, o_dn=o_dn,
               tinv=tinv, sall=sall, o_sb=o_sb, o_dnn=o_dnn, y=y, b_sb=b_sb, b_dn=b_dn, out=out,
               shift=shift, scale=scale, gate=gate)
    return x_next, res, lw, gathered


def _layer_bwd(dxn, res, lw, pending=None):
    p = res["p"]
    dout, db_sb, db_dn, dm, dgate = _out_bwd(dxn, res["out"], res["gate"], lw["w_out"], p, res["b_sb"], res["b_dn"])
    dw_out = _matmul("dw_out", res["y"], dout, "tn", _MXU_DTYPE)
    dwb_sb = _matmul("dwb_sb", res["o_sb"], db_sb, "tn", _MXU_DTYPE)
    dwb_dn = _matmul("dwb_dn", res["o_dnn"], db_dn, "tn", _MXU_DTYPE)
    do_att, dz_sb, do_dn, dz_dn, dgn = _gate_bwd(db_sb, db_dn, lw["wb_sb"], lw["wb_dn"], res["o_att"], res["o_dn"],
                                                  p, lw["gn"])
    D = dxn.shape[1]
    by_shard = lambda g: g.reshape(g.shape[0], 4, g.shape[1] // 4).transpose(1, 0, 2)
    send = [by_shard(dwb_sb), by_shard(dwb_dn), dw_out.reshape(4, D // 4, D)]
    dn_args = (res["qkv"], res["bb"], res["gcb"], res["glb"], res["tinv"], res["sall"], do_dn)
    if pending is None:
        (dqn, dkn, dv), _ = _sb_bwd(res["qn"], res["kn"], p, do_att, res["tot"], res["nblocks"])
        (dqkv, dbb, dgb), received = _dn_bwd(*dn_args)
    else:
        above, above_send = pending
        (dqn, dkn, dv), got_in = _sb_bwd(res["qn"], res["kn"], p, do_att, res["tot"], res["nblocks"],
                                         _GradExchange(above_send[:1], above), above_send[:1])
        (dqkv, dbb, dgb), got_rest = _dn_bwd(*dn_args, _GradExchange(above_send[1:], above), above_send[1:],
                                             _GradExchange(send, above - 1), send)
        received, send = list(got_in) + got_rest, []
    dq_sb, dk_sb, dgq, dgk = _sb_prep_bwd(p, dqn, dkn, lw["gq_t"], lw["gk_t"])
    dc, dp_ba, dal, ddt = _dn_prep_bwd_a(p, dqkv, dbb, dgb, lw["conv_w"], lw["a_row"], lw["dtb_row"])
    dp_dn, dconv = _dn_prep_bwd_b(p, dc, lw["conv_w"])
    dp = jnp.concatenate([dp_dn, dz_dn, dq_sb, dk_sb, dv.astype(_MXU_DTYPE), dz_sb, dm, dp_ba], axis=1)
    dw_cat = _matmul("dw_cat", res["h"], dp, "tn", _MXU_DTYPE, tm_cap=1024, tn_cap=896, tk_cap=512)
    send = [_shards_from_cat(dw_cat, D)] + send
    if pending is None:
        dh = _matmul("dh", dp, lw["w_cat"], "nt", F32, tm_cap=1024, tk_cap=896)
    else:
        dh, arrived = _matmul("dh", dp, lw["w_cat"], "nt", F32, tm_cap=1024, tk_cap=896,
                              exchange=_GradExchange(send, pending[0] - 1), ex_src=send, ex_prev=received[:1])
        received, send = arrived + list(received[1:]), []
    dx, dshift, dscale, dnorm_g = _norm_mod_bwd(res["x"], dh, dxn, lw["norm_g"], res["scale"])
    small = dict(dmod=jnp.concatenate([dshift, dscale, dgate], axis=1)[0], norm_g=dnorm_g[0],
                 sb_q_g=dgq.reshape(SB_HEADS, SB_HD).sum(0), sb_k_g=dgk.reshape(SB_HEADS, SB_HD).sum(0),
                 conv_w=dconv, dn_a_log=dal[0, DN_HEADS:2 * DN_HEADS], dn_dt_bias=ddt[0, DN_HEADS:2 * DN_HEADS],
                 dn_norm_g=dgn[0])
    return dx, small, send, received


def _cat_cols(w, D):
    return jnp.concatenate([w[:, 2048:4096], w[:, 0:2048], w[:, 4104:4104 + 2 * D], w[:, 4096:4104],
                            jnp.zeros((w.shape[0], LANES - 8), w.dtype)], axis=1)


def _shards_from_cat(g, D):
    n = (4104 + 2 * D) // 4
    segments = ((0, 2048, 2048), (2048, 4096, 0), (4096, 4104, 4096 + 2 * D), (4104, 4104 + 2 * D, 4096))

    def shard(lo, hi):
        cuts = [(c0 + max(lo, s0) - s0, c0 + min(hi, s1) - s0) for s0, s1, c0 in segments if max(lo, s0) < min(hi, s1)]
        return jnp.concatenate([g[:, a:b] for a, b in cuts], axis=1)

    return jnp.stack([shard(s * n, (s + 1) * n) for s in range(4)])


def _flat_pack(arrs, mult):
    flat = jnp.concatenate([a.reshape(-1) for a in arrs])
    n = flat.shape[0]
    pad = (-n) % mult
    if pad:
        flat = jnp.concatenate([flat, jnp.zeros((pad,), flat.dtype)])
    return flat.reshape(-1, LANES)


def _flat_unpack(flat, shapes):
    flat = flat.reshape(-1)
    out, off = [], 0
    for s in shapes:
        n = math.prod(s)
        out.append(flat[off:off + n].reshape(s))
        off += n
    return out


BIG = ("w_in", "w_branch_sb", "w_branch_dn", "w_out")
SMALL = ("ada_b", "norm_g", "sb_q_g", "sb_k_g", "conv_w", "dn_a_log", "dn_dt_bias", "dn_norm_g")


def kernel(x, c, ada_w, ada_b, norm_g, w_in, sb_q_g, sb_k_g, conv_w, dn_a_log, dn_dt_bias, dn_norm_g, w_branch_sb, w_branch_dn, w_out, loss_target, m_ada_w, m_ada_b, m_norm_g, m_w_in, m_sb_q_g, m_sb_k_g, m_conv_w, m_dn_a_log, m_dn_dt_bias, m_dn_norm_g, m_w_branch_sb, m_w_branch_dn, m_w_out, v_ada_w, v_ada_b, v_norm_g, v_w_in, v_sb_q_g, v_sb_k_g, v_conv_w, v_dn_a_log, v_dn_dt_bias, v_dn_norm_g, v_w_branch_sb, v_w_branch_dn, v_w_out):
    W = dict(ada_w=ada_w, ada_b=ada_b, norm_g=norm_g, w_in=w_in, sb_q_g=sb_q_g, sb_k_g=sb_k_g, conv_w=conv_w,
             dn_a_log=dn_a_log, dn_dt_bias=dn_dt_bias, dn_norm_g=dn_norm_g, w_branch_sb=w_branch_sb,
             w_branch_dn=w_branch_dn, w_out=w_out)
    M = dict(ada_w=m_ada_w, ada_b=m_ada_b, norm_g=m_norm_g, w_in=m_w_in, sb_q_g=m_sb_q_g, sb_k_g=m_sb_k_g,
             conv_w=m_conv_w, dn_a_log=m_dn_a_log, dn_dt_bias=m_dn_dt_bias, dn_norm_g=m_dn_norm_g,
             w_branch_sb=m_w_branch_sb, w_branch_dn=m_w_branch_dn, w_out=m_w_out)
    V = dict(ada_w=v_ada_w, ada_b=v_ada_b, norm_g=v_norm_g, w_in=v_w_in, sb_q_g=v_sb_q_g, sb_k_g=v_sb_k_g,
             conv_w=v_conv_w, dn_a_log=v_dn_a_log, dn_dt_bias=v_dn_dt_bias, dn_norm_g=v_dn_norm_g,
             w_branch_sb=v_w_branch_sb, w_branch_dn=v_w_branch_dn, w_out=v_w_out)
    L = ada_w.shape[0]
    S, D = x.shape[1], x.shape[2]
    ix, iy, ic = lax.axis_index("x"), lax.axis_index("y"), lax.axis_index("c")
    shard = 2 * ix + iy
    me = 2 * shard + ic
    n_ada = ada_w.shape[2]
    n_in = w_in.shape[2]
    n_conv = conv_w.shape[2]
    n_br = w_branch_sb.shape[2]
    n_out = w_out.shape[1]

    assert L == 2, "the owner of a layer's gradients is the core with the layer's number"
    shards = [W[n].astype(_MXU_DTYPE) for n in BIG]
    gathered0 = _ag_weights_first([shards[0][0]])

    g1 = _ag_small("ag_c_conv", _flat_pack([c, conv_w], LANES * 8))
    g1 = g1.reshape(8, -1)
    c_all = g1[:, :D]
    conv_parts = g1[:, D:D + L * CONV_K * n_conv].reshape(4, 2, L, CONV_K, n_conv)[:, 0]
    conv_full = jnp.concatenate([conv_parts[s] for s in range(4)], axis=2)
    ada_b_sh = lax.dynamic_slice_in_dim(ada_b, shard * n_ada, n_ada, axis=1)[:, None, :]
    mod_sh = _ada_fwd(c_all, ada_w, ada_b_sh)
    g2 = _ag_small("ag_mod", _flat_pack([mod_sh], LANES * 8)).reshape(8, -1)
    mod_parts = g2[:, :L * 8 * n_ada].reshape(4, 2, L, 8, n_ada)[:, 0]
    mod_all = jnp.concatenate([mod_parts[s] for s in range(4)], axis=2)
    mod = lax.dynamic_index_in_dim(mod_all, me, axis=1, keepdims=False)

    def layer_weights(l, g_in, late=None):
        pad_lo = jnp.zeros((DN_HEADS,), F32)
        pad_hi = jnp.zeros((LANES - 2 * DN_HEADS,), F32)
        lw = dict(
            norm_g=norm_g[l][None, :], w_cat=_cat_cols(jnp.concatenate([g_in[s] for s in range(4)], axis=1), D),
            gq_t=jnp.tile(sb_q_g[l], SB_HEADS)[None, :], gk_t=jnp.tile(sb_k_g[l], SB_HEADS)[None, :],
            conv_w=conv_full[l],
            a_row=jnp.concatenate([pad_lo, dn_a_log[l], pad_hi])[None, :],
            dtb_row=jnp.concatenate([pad_lo, dn_dt_bias[l], pad_hi])[None, :], gn=dn_norm_g[l][None, :])
        return lw if late is None else {**lw, **_late_weights(late)}

    mods = lambda l: (mod[l, None, 0:D], mod[l, None, D:2 * D], mod[l, None, 2 * D:3 * D])
    lws, ress = [None] * L, [None] * L
    xs, ress[0], lws[0], gathered1 = _layer_fwd(x[0], *mods(0), layer_weights(0, gathered0[0]),
                                                next_shards=[a[1] for a in shards],
                                                late_shards=[a[0] for a in shards[1:]])
    xs, ress[1], lws[1], _ = _layer_fwd(xs, *mods(1), layer_weights(1, gathered1[0], gathered1[1:]))
    dxs, loss_row = _loss_head(xs, loss_target[0])
    loss = lax.psum(loss_row[0, 0], ("x", "y", "c"))
    smalls = [None] * L
    dxs, smalls[1], send1, _ = _layer_bwd(dxs, ress[1], lws[1])
    dxs, smalls[0], send0, got = _layer_bwd(dxs, ress[0], lws[0], (1, send1))
    grad_x = dxs[None]

    small_names = ("dmod",) + SMALL[1:]
    small_pack = _flat_pack([jnp.stack([smalls[l][n] for l in range(L)]) for n in small_names], LANES * 8)
    g3 = _ag_small("ag_small_grads", small_pack)
    R3 = small_pack.shape[0]
    g3 = g3.reshape(8, R3, LANES)
    small_sum = _sum_parts("sum_small", g3)
    small_shapes = [(L, 3 * D), (L, D), (L, SB_HD), (L, SB_HD), (L, CONV_K, 3 * DN_W), (L, DN_HEADS), (L, DN_HEADS),
                    (L, DN_HD)]
    sg = dict(zip(small_names, _flat_unpack(small_sum, small_shapes)))
    G = dict(ada_b=sg["dmod"], norm_g=sg["norm_g"], sb_q_g=sg["sb_q_g"], sb_k_g=sg["sb_k_g"],
             conv_w=lax.dynamic_slice_in_dim(sg["conv_w"], shard * n_conv, n_conv, axis=2),
             dn_a_log=sg["dn_a_log"], dn_dt_bias=sg["dn_dt_bias"], dn_norm_g=sg["dn_norm_g"])
    dmod_all = g3.reshape(8, -1)[:, :L * 3 * D].reshape(8, L, 3 * D)
    dmod_sh = lax.dynamic_slice_in_dim(dmod_all, shard * n_ada, n_ada, axis=2).transpose(1, 0, 2)
    G["ada_w"] = _ada_bwd(c_all.T, dmod_sh)

    assert not send0
    mine = [_sum_parts("sum_" + n, g) for n, g in zip(BIG, got)]
    for n, g in zip(BIG, _sibling_join(mine)):
        G[n] = g

    delta, new_m, new_v = {}, {}, {}
    for n in ("ada_w",) + BIG:
        delta[n], new_m[n], new_v[n] = _adamw("adamw_" + n, W[n], G[n], M[n], V[n])
    sm_shapes = [W[n].shape for n in SMALL]
    d, mo, vo = _adamw("adamw_small", *[_flat_pack([T[n] for n in SMALL], LANES * 8)[None] for T in (W, G, M, V)])
    for n, dd, mm, vv in zip(SMALL, _flat_unpack(d, sm_shapes), _flat_unpack(mo, sm_shapes),
                             _flat_unpack(vo, sm_shapes)):
        delta[n], new_m[n], new_v[n] = dd, mm, vv

    order = ("ada_w", "ada_b", "norm_g", "w_in", "sb_q_g", "sb_k_g", "conv_w", "dn_a_log", "dn_dt_bias", "dn_norm_g",
             "w_branch_sb", "w_branch_dn", "w_out")
    return (loss, grad_x, *[G[n] for n in order], *[delta[n] for n in order], *[new_m[n] for n in order],
            *[new_v[n] for n in order])
```

```python
import math

import jax
import jax.numpy as jnp
from jax import lax
from jax.experimental import pallas as pl
from jax.experimental.pallas import tpu as pltpu

F32 = jnp.float32
BF16 = jnp.bfloat16
_MXU_DTYPE = BF16
_VMEM_LIMIT = 48 * 1024 * 1024
LANES = 128

EPS = 1e-6
SB_HEADS, SB_HD, SB_W = 8, 64, 512
DN_HEADS, DN_HD, DN_W = 4, 128, 512
CONV_K = 4
CHUNK = 64
QB = 256
_SB_DEAD = 104.0
ADAM_LR, ADAM_B1, ADAM_B2, ADAM_EPS, ADAM_WD, ADAM_STEP = 0.001, 0.9, 0.999, 1e-08, 0.01, 10

C_DN_QKV, C_DN_Z, C_SB_Q, C_SB_K, C_SB_V, C_SB_Z, C_MG = 0, 1536, 2048, 2560, 3072, 3584, 4096

_NN = (((1,), (0,)), ((), ()))
_NT = (((1,), (1,)), ((), ()))
_TN = (((0,), (0,)), ((), ()))
_BNN = (((2,), (1,)), ((0,), (0,)))
_BNT = (((2,), (2,)), ((0,), (0,)))
_BTN = (((1,), (1,)), ((0,), (0,)))
MESH = pl.DeviceIdType.MESH


def _sds(shape, dtype):
    return jax.ShapeDtypeStruct(shape, dtype)


def _cp(n):
    return pltpu.CompilerParams(dimension_semantics=("arbitrary",) * n, vmem_limit_bytes=_VMEM_LIMIT)


def _rb(tm, w, cb=0):
    return pl.BlockSpec((tm, w), lambda i: (i, cb))


def _fs(shape):
    nd = len(shape)
    return pl.BlockSpec(shape, lambda i: (0,) * nd)


def _dg(a, b, dims):
    return lax.dot_general(a, b, dims, preferred_element_type=F32)


def _mm(a, b, dims=_NN):
    return _dg(a.astype(_MXU_DTYPE), b.astype(_MXU_DTYPE), dims)


def _split3(x):
    hi = x.astype(BF16)
    r = x - hi.astype(F32)
    mid = r.astype(BF16)
    lo = (r - mid.astype(F32)).astype(BF16)
    return hi, mid, lo


def _mm_xl(x, const, dims=_NN):
    cb = const.astype(BF16)
    hi, mid, lo = _split3(x)
    return _dg(hi, cb, dims) + _dg(mid, cb, dims) + _dg(lo, cb, dims)


def _mm_xl2(x, const, dims=_NN):
    cb = const.astype(BF16)
    hi = x.astype(BF16)
    lo = (x - hi.astype(F32)).astype(BF16)
    return _dg(hi, cb, dims) + _dg(lo, cb, dims)


def _mm_xr(const, x, dims=_NN):
    cb = const.astype(BF16)
    hi, mid, lo = _split3(x)
    return _dg(cb, hi, dims) + _dg(cb, mid, dims) + _dg(cb, lo, dims)


def _mm3(a, b, dims=_NN):
    ah, am, _ = _split3(a)
    bh, bm, _ = _split3(b)
    return _dg(ah, bh, dims) + (_dg(ah, bm, dims) + _dg(am, bh, dims))


def _sigmoid(z):
    return 1.0 / (1.0 + jnp.exp(-z))


def _silu(z):
    return z * _sigmoid(z)


def _dsilu(z):
    s = _sigmoid(z)
    return s * (1.0 + z * (1.0 - s))


def _softplus(z):
    return jnp.maximum(z, 0.0) + jnp.log(1.0 + jnp.exp(-jnp.abs(z)))


def _iota2(shape, dim):
    return lax.broadcasted_iota(jnp.int32, shape, dim)


def _pick(n, cap, mult):
    best = None
    for t in range(mult, min(n, cap) + 1, mult):
        if n % t == 0:
            best = t
    assert best is not None, (n, cap, mult)
    return best


def _matmul(name, a, b, form, out_dtype, tm_cap=512, tn_cap=1024, tk_cap=1024, exchange=None, ex_src=(), ex_prev=()):
    if form == "nn":
        (M, K), (_, N) = a.shape, b.shape
    elif form == "nt":
        (M, K), (N, _) = a.shape, b.shape
    else:
        (K, M), (_, N) = a.shape, b.shape
    tm = _pick(M, tm_cap, 128 if form == "tn" else 8)
    tn = _pick(N, tn_cap, 128)
    tk = _pick(K, tk_cap, 128)
    nk = K // tk
    dims = {"nn": _NN, "nt": _NT, "tn": _TN}[form]
    if form == "nn":
        a_spec = pl.BlockSpec((tm, tk), lambda i, j, k: (i, k))
        b_spec = pl.BlockSpec((tk, tn), lambda i, j, k: (k, j))
    elif form == "nt":
        a_spec = pl.BlockSpec((tm, tk), lambda i, j, k: (i, k))
        b_spec = pl.BlockSpec((tn, tk), lambda i, j, k: (j, k))
    else:
        a_spec = pl.BlockSpec((tk, tm), lambda i, j, k: (k, i))
        b_spec = pl.BlockSpec((tk, tn), lambda i, j, k: (k, j))

    grid = (M // tm, N // tn, nk)
    nx, npv = len(ex_src), len(ex_prev)
    o0 = 2 + nx + npv

    def body(*refs):
        if exchange is None:
            compute(*refs)
            return
        src, xout, sems = refs[2:2 + nx], refs[o0 + 1:o0 + 1 + nx], refs[o0 + 2 + nx:]
        at = [pl.program_id(d) for d in range(3)]

        @pl.when(jnp.logical_and(jnp.logical_and(at[0] == 0, at[1] == 0), at[2] == 0))
        def _():
            exchange.start(src, xout, sems)

        compute(refs[0], refs[1], refs[o0], refs[o0 + 1 + nx])

        @pl.when(jnp.logical_and(jnp.logical_and(at[0] == grid[0] - 1, at[1] == grid[1] - 1), at[2] == nk - 1))
        def _():
            exchange.wait(src, xout, sems)

    def compute(a_ref, b_ref, o_ref, acc_ref):
        if nk == 1:
            o_ref[...] = _mm(a_ref[...], b_ref[...], dims).astype(o_ref.dtype)
            return
        k = pl.program_id(2)

        @pl.when(k == 0)
        def _():
            acc_ref[...] = _mm(a_ref[...], b_ref[...], dims)

        @pl.when(k > 0)
        def _():
            acc_ref[...] += _mm(a_ref[...], b_ref[...], dims)

        @pl.when(k == nk - 1)
        def _():
            o_ref[...] = acc_ref[...].astype(o_ref.dtype)

    hbm = pl.BlockSpec(memory_space=pl.ANY)
    outs = pl.pallas_call(
        body, name=name, grid=grid,
        in_specs=[a_spec, b_spec] + [hbm] * (nx + npv),
        out_specs=[pl.BlockSpec((tm, tn), lambda i, j, k: (i, j))] + [hbm] * nx,
        out_shape=[_sds((M, N), out_dtype)] + (exchange.out_shape if exchange else []),
        input_output_aliases={2 + nx + t: 1 + t for t in range(npv)},
        scratch_shapes=[pltpu.VMEM((tm, tn), F32)] + (exchange.scratch if exchange else []),
        compiler_params=_cp(3),
    )(a, b, *ex_src, *ex_prev)
    return outs[0] if exchange is None else (outs[0], list(outs[1:]))


def _norm_mod(x, g, scale, shift, tm=256):
    S, D = x.shape

    def body(x_ref, g_ref, sc_ref, sh_ref, h_ref):
        xv = x_ref[...]
        r = lax.rsqrt(jnp.mean(xv * xv, axis=1, keepdims=True) + EPS)
        h_ref[...] = ((xv * r * g_ref[...]) * (1.0 + sc_ref[...]) + sh_ref[...]).astype(h_ref.dtype)

    return pl.pallas_call(
        body, name="norm_mod", grid=(S // tm,),
        in_specs=[_rb(tm, D), _fs((1, D)), _fs((1, D)), _fs((1, D))],
        out_specs=_rb(tm, D), out_shape=_sds((S, D), _MXU_DTYPE), compiler_params=_cp(1),
    )(x, g, scale, shift)


def _norm_mod_bwd(x, dh, dxn, g, scale, tm=256):
    S, D = x.shape

    def body(x_ref, dh_ref, dxn_ref, g_ref, sc_ref, dx_ref, dsh_ref, dsc_ref, dg_ref):
        @pl.when(pl.program_id(0) == 0)
        def _():
            dsh_ref[...] = jnp.zeros_like(dsh_ref)
            dsc_ref[...] = jnp.zeros_like(dsc_ref)
            dg_ref[...] = jnp.zeros_like(dg_ref)

        xv, dhv, gv = x_ref[...], dh_ref[...], g_ref[...]
        r = lax.rsqrt(jnp.mean(xv * xv, axis=1, keepdims=True) + EPS)
        xh = xv * r
        one_sc = 1.0 + sc_ref[...]
        dsh_ref[...] += jnp.sum(dhv, axis=0, keepdims=True)
        dsc_ref[...] += jnp.sum(dhv * xh * gv, axis=0, keepdims=True)
        dg_ref[...] += jnp.sum(dhv * one_sc * xh, axis=0, keepdims=True)
        dxh = dhv * (gv * one_sc)
        dx_ref[...] = r * (dxh - xh * jnp.mean(dxh * xh, axis=1, keepdims=True)) + dxn_ref[...]

    return pl.pallas_call(
        body, name="norm_mod_bwd", grid=(S // tm,),
        in_specs=[_rb(tm, D), _rb(tm, D), _rb(tm, D), _fs((1, D)), _fs((1, D))],
        out_specs=[_rb(tm, D), _fs((1, D)), _fs((1, D)), _fs((1, D))],
        out_shape=[_sds((S, D), F32)] + [_sds((1, D), F32)] * 3, compiler_params=_cp(1),
    )(x, dh, dxn, g, scale)


def _head_sum_matrix():
    r = jnp.arange(SB_W)
    return (r[:, None] // SB_HD == r[None, :] // SB_HD).astype(BF16)


def _sb_prep(p, gq_t, gk_t, tm=256):
    S = p.shape[0]
    bd = _head_sum_matrix()

    def body(q_ref, k_ref, gq_ref, gk_ref, bd_ref, qn_ref, kn_ref):
        for src, g_ref, dst in ((q_ref, gq_ref, qn_ref), (k_ref, gk_ref, kn_ref)):
            v = src[...]
            ms = _mm_xl(v * v, bd_ref[...]) * (1.0 / SB_HD)
            dst[...] = (v * lax.rsqrt(ms + EPS) * g_ref[...]).astype(dst.dtype)

    return pl.pallas_call(
        body, name="sb_prep", grid=(S // tm,),
        in_specs=[_rb(tm, SB_W, C_SB_Q // SB_W), _rb(tm, SB_W, C_SB_K // SB_W),
                  _fs((1, SB_W)), _fs((1, SB_W)), _fs((SB_W, SB_W))],
        out_specs=[_rb(tm, SB_W), _rb(tm, SB_W)],
        out_shape=[_sds((S, SB_W), _MXU_DTYPE)] * 2, compiler_params=_cp(1),
    )(p, p, gq_t, gk_t, bd)


def _sb_prep_bwd(p, dqn, dkn, gq_t, gk_t, tm=256):
    S = p.shape[0]
    bd = _head_sum_matrix()

    def body(q_ref, k_ref, dqn_ref, dkn_ref, gq_ref, gk_ref, bd_ref, dq_ref, dk_ref, dgq_ref, dgk_ref):
        @pl.when(pl.program_id(0) == 0)
        def _():
            dgq_ref[...] = jnp.zeros_like(dgq_ref)
            dgk_ref[...] = jnp.zeros_like(dgk_ref)

        for src, dn_ref, g_ref, dst, dg_ref in ((q_ref, dqn_ref, gq_ref, dq_ref, dgq_ref),
                                                (k_ref, dkn_ref, gk_ref, dk_ref, dgk_ref)):
            v, dn = src[...], dn_ref[...]
            r = lax.rsqrt(_mm_xl(v * v, bd_ref[...]) * (1.0 / SB_HD) + EPS)
            vh = v * r
            dg_ref[...] += jnp.sum(dn * vh, axis=0, keepdims=True)
            dvh = dn * g_ref[...]
            m = _mm_xl(dvh * vh, bd_ref[...]) * (1.0 / SB_HD)
            dst[...] = (r * (dvh - vh * m)).astype(dst.dtype)

    return pl.pallas_call(
        body, name="sb_prep_bwd", grid=(S // tm,),
        in_specs=[_rb(tm, SB_W, C_SB_Q // SB_W), _rb(tm, SB_W, C_SB_K // SB_W), _rb(tm, SB_W), _rb(tm, SB_W),
                  _fs((1, SB_W)), _fs((1, SB_W)), _fs((SB_W, SB_W))],
        out_specs=[_rb(tm, SB_W), _rb(tm, SB_W), _fs((1, SB_W)), _fs((1, SB_W))],
        out_shape=[_sds((S, SB_W), _MXU_DTYPE)] * 2 + [_sds((1, SB_W), F32)] * 2, compiler_params=_cp(1),
    )(p, p, dqn, dkn, gq_t, gk_t, bd)


def _sb_consts():
    r, c = _iota2((QB, QB), 0), _iota2((QB, QB), 1)
    lane = _iota2((1, LANES), 1)
    return r, c, lane


def _sb_fwd(qn, kn, p, gather=None, g_src=()):
    S = qn.shape[0]
    scale = 1.0 / math.sqrt(SB_HD)
    grid = (SB_W // LANES, S // QB)
    nx = len(g_src)

    def body(*refs):
        if gather is None:
            compute(*refs)
            return
        src, gout, sems = refs[3:3 + nx], refs[6 + nx:6 + 2 * nx], refs[6 + 2 * nx:]
        hp, i = pl.program_id(0), pl.program_id(1)

        @pl.when(jnp.logical_and(hp == 0, i == 0))
        def _():
            gather.start(src, gout, sems)

        compute(*refs[:3], *refs[3 + nx:6 + nx])

        @pl.when(jnp.logical_and(hp == grid[0] - 1, i == grid[1] - 1))
        def _():
            gather.wait(src, gout, sems)

    def compute(q_ref, k_ref, v_ref, o_ref, tot_ref, nb_ref):
        i = pl.program_id(1)
        r, c, lane = _sb_consts()
        u_gt = (r > c).astype(BF16)
        strict = jnp.concatenate([c < r, c < r], axis=0)
        q = q_ref[...]
        mask0 = (lane // SB_HD) == 0
        zero = jnp.zeros_like(q)
        qh = jnp.concatenate([jnp.where(mask0, q, zero), jnp.where(mask0, zero, q)], axis=0)

        def block(off, carry, diagonal):
            o, run = carry
            kj = k_ref[pl.ds(off, QB), :]
            vj = v_ref[pl.ds(off, QB), :].astype(_MXU_DTYPE)
            z = _mm(qh, kj, _NT) * scale
            sp = _softplus(z)
            sp_m = jnp.where(strict, sp, 0.0) if diagonal else sp
            later = _mm_xl2(sp_m, u_gt)
            w = jnp.exp((z - sp) - later - run)
            if diagonal:
                w = jnp.where(strict, w, 0.0)
            return o + _mm(w, vj), run + jnp.sum(sp_m, axis=1, keepdims=True)

        init = (jnp.zeros((2 * QB, LANES), F32), jnp.zeros((2 * QB, 1), F32))
        carry = block(pl.multiple_of(i * QB, QB), init, True)
        st = lax.while_loop(
            lambda st: jnp.logical_and(st[0] <= i, jnp.min(st[2]) < _SB_DEAD),
            lambda st: (st[0] + 1,) + block(pl.multiple_of((i - st[0]) * QB, QB), st[1:], False),
            (jnp.int32(1),) + carry)
        o_ref[...] = jnp.where(mask0, st[1][:QB], st[1][QB:])
        tot_ref[...] = jnp.where(mask0, st[2][:QB], st[2][QB:])
        nb_ref[...] = jnp.zeros((8, LANES), F32) + st[0].astype(F32)

    blk = pl.BlockSpec((QB, LANES), lambda hp, i: (i, hp))
    hbm = pl.BlockSpec(memory_space=pl.ANY)
    outs = pl.pallas_call(
        body, name="sb_fwd", grid=grid,
        in_specs=[blk, pl.BlockSpec((S, LANES), lambda hp, i: (0, hp)),
                  pl.BlockSpec((S, LANES), lambda hp, i: (0, C_SB_V // LANES + hp))] + [hbm] * nx,
        out_specs=[blk, blk, pl.BlockSpec((8, LANES), lambda hp, i: (i, hp))] + [hbm] * nx,
        out_shape=[_sds((S, SB_W), F32), _sds((S, SB_W), F32), _sds((8 * S // QB, SB_W), F32)]
        + (gather.out_shape if gather else []),
        scratch_shapes=gather.scratch if gather else [], compiler_params=_cp(2),
    )(qn, kn, p, *g_src)
    return outs[:3], list(outs[3:])


def _sb_bwd(qn, kn, p, do, tot, nblocks, exchange=None, ex_src=(), early=None, early_src=()):
    S = qn.shape[0]
    scale = 1.0 / math.sqrt(SB_HD)
    grid = (SB_W // LANES, S // QB)
    nx, ne = len(ex_src), len(early_src)
    NI = 6

    def body(*refs):
        if exchange is None:
            compute(*refs)
            return
        src, src2 = refs[NI:NI + nx], refs[NI + nx:NI + nx + ne]
        o0 = NI + nx + ne
        xout, sems, sems2 = refs[o0 + 3:o0 + 3 + nx], refs[o0 + 3 + nx:o0 + 6 + nx], refs[o0 + 6 + nx:]
        hp, i = pl.program_id(0), pl.program_id(1)

        @pl.when(jnp.logical_and(hp == 0, i == 0))
        def _():
            exchange.start(src, xout, sems)
            if early is not None:
                early.start(src2, xout[nx - ne:], sems2)

        compute(*refs[:NI], *refs[o0:o0 + 3])

        @pl.when(jnp.logical_and(hp == grid[0] - 1, i == grid[1] - 1))
        def _():
            exchange.wait(src, xout, sems)
            if early is not None:
                early.wait(src2, xout[nx - ne:], sems2)

    def compute(q_ref, k_ref, v_ref, do_ref, tot_ref, nb_ref, dq_ref, dk_ref, dv_ref):
        i = pl.program_id(1)

        @pl.when(i == 0)
        def _():
            dk_ref[...] = jnp.zeros_like(dk_ref)
            dv_ref[...] = jnp.zeros_like(dv_ref)

        r, c, lane = _sb_consts()
        u_le = (r <= c).astype(BF16)
        u_lt = (r < c).astype(BF16)
        strict = jnp.concatenate([c < r, c < r], axis=0)
        q = q_ref[...]
        do = do_ref[...].astype(_MXU_DTYPE)
        mask0 = (lane // SB_HD) == 0
        zero, zero_do = jnp.zeros_like(q), jnp.zeros_like(do)
        qh = jnp.concatenate([jnp.where(mask0, q, zero), jnp.where(mask0, zero, q)], axis=0)
        doh = jnp.concatenate([jnp.where(mask0, do, zero_do), jnp.where(mask0, zero_do, do)], axis=0)

        tot_pair = tot_ref[...]
        tot = jnp.concatenate([jnp.max(jnp.where(mask0, tot_pair, 0.0), axis=1, keepdims=True),
                               jnp.max(jnp.where(mask0, 0.0, tot_pair), axis=1, keepdims=True)], axis=0)
        nb = jnp.clip(jnp.max(nb_ref[...]).astype(jnp.int32), 1, i + 1)
        first = i + 1 - nb

        def block(off, carry, diagonal):
            dq, pre_sp, pre_e = carry
            kj = k_ref[pl.ds(off, QB), :]
            vj = v_ref[pl.ds(off, QB), :].astype(_MXU_DTYPE)
            z = _mm(qh, kj, _NT) * scale
            sp = _softplus(z)
            a = z - sp
            sp_m = jnp.where(strict, sp, 0.0) if diagonal else sp
            incl = _mm_xl2(sp_m, u_le)
            w = jnp.exp(a - ((tot - pre_sp) - incl))
            if diagonal:
                w = jnp.where(strict, w, 0.0)
            e = w * _mm(doh, vj, _NT)
            db = pre_e + _mm_xl2(e, u_lt)
            dz = (e - jnp.exp(a) * (e + db)) * scale
            if diagonal:
                dz = jnp.where(strict, dz, 0.0)
            dk_ref[pl.ds(off, QB), :] += _mm(dz, qh, _TN)
            dv_ref[pl.ds(off, QB), :] += _mm(w, doh, _TN)
            return (dq + _mm(dz, kj), pre_sp + jnp.sum(sp_m, axis=1, keepdims=True),
                    pre_e + jnp.sum(e, axis=1, keepdims=True))

        zero_col = jnp.zeros((2 * QB, 1), F32)
        init = (jnp.zeros((2 * QB, LANES), F32), zero_col, zero_col)
        carry = lax.fori_loop(first, i, lambda j, cr: block(pl.multiple_of(j * QB, QB), cr, False), init)
        carry = block(pl.multiple_of(i * QB, QB), carry, True)
        dq_ref[...] = jnp.where(mask0, carry[0][:QB], carry[0][QB:])

    blk = pl.BlockSpec((QB, LANES), lambda hp, i: (i, hp))
    full = pl.BlockSpec((S, LANES), lambda hp, i: (0, hp))
    hbm = pl.BlockSpec(memory_space=pl.ANY)
    outs = pl.pallas_call(
        body, name="sb_bwd", grid=grid,
        in_specs=[blk, full, pl.BlockSpec((S, LANES), lambda hp, i: (0, C_SB_V // LANES + hp)), blk, blk,
                  pl.BlockSpec((8, LANES), lambda hp, i: (i, hp))] + [hbm] * (nx + ne),
        out_specs=[blk, full, full] + [hbm] * nx,
        out_shape=[_sds((S, SB_W), F32)] * 3 + (exchange.out_shape if exchange else []),
        scratch_shapes=(exchange.scratch if exchange else []) + (early.scratch if early else []),
        compiler_params=_cp(2),
    )(qn, kn, p, do, tot, nblocks, *ex_src, *early_src)
    return outs[:3], outs[3:]


def _dn_prep(p, conv_w, a_row, dtb_row, tm=256):
    S = p.shape[0]
    W3 = 3 * DN_W
    nhalo = tm // 8

    def body(x_ref, halo_ref, w_ref, ba_ref, a_ref, dtb_ref, qkv_ref, bb_ref, gc_ref, gl_ref):
        i = pl.program_id(0)
        halo = jnp.where(i > 0, halo_ref[...], 0.0)
        xf = jnp.concatenate([halo, x_ref[...]], axis=0)
        acc = jnp.zeros((tm, W3), F32)
        for k in range(CONV_K):
            sh = CONV_K - 1 - k
            xs = xf if sh == 0 else pltpu.roll(xf, sh, 0)
            acc = acc + xs[8:, :] * w_ref[k:k + 1, :]
        s = _silu(acc)
        for gi in range(2 * DN_HEADS):
            sl = slice(gi * LANES, (gi + 1) * LANES)
            sg = s[:, sl]
            rinv = lax.rsqrt(jnp.sum(sg * sg, axis=1, keepdims=True) + EPS)
            qkv_ref[:, sl] = sg * rinv * (DN_HD ** -0.5 if gi < DN_HEADS else 1.0)
        qkv_ref[:, 2 * DN_W:] = s[:, 2 * DN_W:]

        ba = ba_ref[...]
        beta = _sigmoid(ba)
        g = -jnp.exp(a_ref[...]) * _softplus(ba + dtb_ref[...])
        lr, lc = _iota2((LANES, DN_W), 0), _iota2((LANES, DN_W), 1)
        sel_b = (lr == lc // LANES).astype(BF16)
        sel_g = (lr == lc // LANES + DN_HEADS).astype(BF16)
        bb_ref[...] = _mm_xl(beta, sel_b)
        graw = _mm_xl(g, sel_g)
        rr, cc = _iota2((tm, tm), 0), _iota2((tm, tm), 1)
        tri = jnp.logical_and(rr >= cc, rr // CHUNK == cc // CHUNK).astype(BF16)
        gc = _mm_xr(tri, graw)
        last = (cc == (rr // CHUNK) * CHUNK + (CHUNK - 1)).astype(BF16)
        gc_ref[...] = gc
        gl_ref[...] = _mm_xr(last, gc)

    return pl.pallas_call(
        body, name="dn_prep", grid=(S // tm,),
        in_specs=[_rb(tm, W3, 0), pl.BlockSpec((8, W3), lambda i: (jnp.maximum(i * nhalo - 1, 0), 0)),
                  _fs((CONV_K, W3)), _rb(tm, LANES, (p.shape[1] - LANES) // LANES),
                  _fs((1, LANES)), _fs((1, LANES))],
        out_specs=[_rb(tm, W3), _rb(tm, DN_W), _rb(tm, DN_W), _rb(tm, DN_W)],
        out_shape=[_sds((S, W3), F32)] + [_sds((S, DN_W), F32)] * 3, compiler_params=_cp(1),
    )(p, p, conv_w, p, a_row, dtb_row)


def _heads(ref, base=0):
    return jnp.stack([ref[:, base + h * LANES:base + (h + 1) * LANES] for h in range(DN_HEADS)])


def _per_head(const):
    return jnp.broadcast_to(const[None], (DN_HEADS,) + const.shape)


def _dn_chunk_terms(q, k, v, beta, gc, gl):
    r, c = _iota2((CHUNK, CHUNK), 0), _iota2((CHUNK, CHUNK), 1)
    tril, strict = r >= c, r > c
    gcol = _mm_xl(gc, _per_head(jnp.full((LANES, CHUNK), 1.0 / LANES, F32)), _BNN)
    grow = _mm_xr(_per_head(jnp.full((CHUNK, LANES), 1.0 / LANES, F32)), gc, _BNT)
    dec = jnp.where(tril, jnp.exp(jnp.where(tril, gcol - grow, 0.0)), 0.0)
    gam = jnp.exp(gc)
    dlt = jnp.exp(gl - gc)
    kb, vb = k * beta, v * beta
    pm = _mm(kb, k, _BNT)
    qk = _mm(q, k, _BNT)
    m = jnp.where(strict, pm * dec, 0.0)
    a = jnp.where(tril, qk * dec, 0.0)
    return dict(tril=tril, strict=strict, dec=dec, gam=gam, dlt=dlt, kb=kb, vb=vb, m=m, a=a)


def _dn_fwd(qkv, bb, gcb, glb, gather=None, g_src=()):
    S = qkv.shape[0]
    N = S // CHUNK
    nx = len(g_src)

    def body(*refs):
        if gather is None:
            compute(*refs)
            return
        src, gout, sems = refs[4:4 + nx], refs[7 + nx:7 + 2 * nx], refs[8 + 2 * nx:]

        @pl.when(pl.program_id(0) == 0)
        def _():
            gather.start(src, gout, sems)

        compute(*refs[:4], *refs[4 + nx:7 + nx], refs[7 + 2 * nx])

        @pl.when(pl.program_id(0) == N - 1)
        def _():
            gather.wait(src, gout, sems)

    def compute(qkv_ref, bb_ref, gc_ref, gl_ref, o_ref, t_ref, sall_ref, s_scr):
        @pl.when(pl.program_id(0) == 0)
        def _():
            s_scr[...] = jnp.zeros_like(s_scr)

        r, c = _iota2((CHUNK, CHUNK), 0), _iota2((CHUNK, CHUNK), 1)
        eye = (r == c).astype(F32)
        q, k, v = _heads(qkv_ref), _heads(qkv_ref, DN_W), _heads(qkv_ref, 2 * DN_W)
        beta, gc, gl = _heads(bb_ref), _heads(gc_ref), _heads(gl_ref)
        s_prev = s_scr[...]
        sall_ref[0] = s_prev.astype(sall_ref.dtype)
        s0 = s_prev.astype(sall_ref.dtype).astype(F32)
        t = _dn_chunk_terms(q, k, v, beta, gc, gl)
        pw = -t["m"]
        tinv = eye + pw
        for _ in range(5):
            pw = _mm3(pw, pw, _BNN)
            tinv = tinv + _mm3(tinv, pw, _BNN)
        t_ref[...] = tinv
        u = _mm3(tinv, t["vb"], _BNN)
        w = _mm3(tinv, t["kb"] * t["gam"], _BNN)
        vn = u - _mm(w, s0, _BNN)
        o = _mm(q * t["gam"], s0, _BNN) + _mm(t["a"], vn, _BNN)
        for h in range(DN_HEADS):
            o_ref[:, h * LANES:(h + 1) * LANES] = o[h]
        egl = jnp.exp(jnp.concatenate([gl, gl], axis=1))
        s_scr[...] = s_prev * egl + _mm(k * t["dlt"], vn, _BTN)

    hbm = pl.BlockSpec(memory_space=pl.ANY)
    outs = pl.pallas_call(
        body, name="dn_fwd", grid=(N,),
        in_specs=[_rb(CHUNK, 3 * DN_W), _rb(CHUNK, DN_W), _rb(CHUNK, DN_W), _rb(CHUNK, DN_W)] + [hbm] * nx,
        out_specs=[_rb(CHUNK, DN_W), pl.BlockSpec((DN_HEADS, CHUNK, CHUNK), lambda n: (0, n, 0)),
                   pl.BlockSpec((1, DN_HEADS, DN_HD, DN_HD), lambda n: (n, 0, 0, 0))] + [hbm] * nx,
        out_shape=[_sds((S, DN_W), F32), _sds((DN_HEADS, S, CHUNK), F32),
                   _sds((N, DN_HEADS, DN_HD, DN_HD), _MXU_DTYPE)] + (gather.out_shape if gather else []),
        scratch_shapes=[pltpu.VMEM((DN_HEADS, DN_HD, DN_HD), F32)] + (gather.scratch if gather else []),
        compiler_params=_cp(1),
    )(qkv, bb, gcb, glb, *g_src)
    return outs[:3], outs[3:]


def _dn_bwd(qkv, bb, gcb, glb, tinv_all, sall, do, exchange=None, ex_src=(), early=None, early_src=()):
    S = qkv.shape[0]
    N = S // CHUNK
    nx = len(ex_src)
    NI = 7

    def body(*refs):
        if exchange is None:
            compute(*refs)
            return
        src, src2 = refs[NI:NI + nx], refs[NI + nx:NI + 2 * nx]
        o0 = NI + 2 * nx
        xout, ds_scr = refs[o0 + 3:o0 + 3 + nx], refs[o0 + 3 + nx]
        sems, sems2 = refs[o0 + 4 + nx:o0 + 7 + nx], refs[o0 + 7 + nx:]

        @pl.when(pl.program_id(0) == 0)
        def _():
            exchange.start(src, xout, sems)
            early.start(src2, xout, sems2)

        compute(*refs[:NI], *refs[o0:o0 + 3], ds_scr)

        @pl.when(pl.program_id(0) == N - 1)
        def _():
            exchange.wait(src, xout, sems)
            early.wait(src2, xout, sems2)

    def compute(qkv_ref, bb_ref, gc_ref, gl_ref, t_ref, sall_ref, do_ref, dqkv_ref, dbb_ref, dg_ref, ds_scr):
        @pl.when(pl.program_id(0) == 0)
        def _():
            ds_scr[...] = jnp.zeros_like(ds_scr)

        r, c = _iota2((CHUNK, CHUNK), 0), _iota2((CHUNK, CHUNK), 1)
        eye = (r == c).astype(F32)
        u_ge = (c >= r).astype(F32)
        last_row = _iota2((CHUNK, LANES), 0) == CHUNK - 1
        eye_h, u_ge_h = _per_head(eye), _per_head(u_ge)
        q, k, v = _heads(qkv_ref), _heads(qkv_ref, DN_W), _heads(qkv_ref, 2 * DN_W)
        beta, gc, gl = _heads(bb_ref), _heads(gc_ref), _heads(gl_ref)
        tinv = t_ref[...]
        s0 = sall_ref[0].astype(F32)
        do = _heads(do_ref)
        ds1 = ds_scr[...]
        t = _dn_chunk_terms(q, k, v, beta, gc, gl)
        gam, dlt, kb, vb, dec = t["gam"], t["dlt"], t["kb"], t["vb"], t["dec"]
        kbg = kb * gam
        u = _mm3(tinv, vb, _BNN)
        w = _mm3(tinv, kbg, _BNN)
        vn = u - _mm(w, s0, _BNN)
        qg, kd = q * gam, k * dlt
        egl = jnp.exp(gl)
        egl2 = jnp.concatenate([egl, egl], axis=1)

        dvn = _mm(t["a"], do, _BTN) + _mm(kd, ds1, _BNN)
        da = jnp.where(t["tril"], _mm(do, vn, _BNT), 0.0)
        dqg = _mm(do, s0, _BNT)
        dkd = _mm(vn, ds1, _BNT)
        dw = -_mm(dvn, s0, _BNT)
        ds_scr[...] = _mm(qg, do, _BTN) + egl2 * ds1 - _mm(w, dvn, _BTN)
        tt = _mm_xr(eye_h, tinv, _BNT)
        dvb = _mm3(tt, dvn, _BNN)
        dkbg = _mm3(tt, dw, _BNN)
        dm = -jnp.where(t["strict"], _mm(dvb, u, _BNT) + _mm(dkbg, w, _BNT), 0.0)
        dpm = dm * dec
        dqk = da * dec
        dkb = dkbg * gam + _mm(dpm, k, _BNN)
        dk = dkd * dlt + _mm(dpm, kb, _BTN) + _mm(dqk, q, _BTN) + dkb * beta
        dq = dqg * gam + _mm(dqk, k, _BNN)
        dv = dvb * beta
        dbeta = jnp.sum(dkb * k, axis=2, keepdims=True) + jnp.sum(dvb * v, axis=2, keepdims=True)
        dgam = jnp.sum(dqg * q, axis=2, keepdims=True) + jnp.sum(dkbg * kb, axis=2, keepdims=True)
        ddlt = jnp.sum(dkd * k, axis=2, keepdims=True)
        xm = dm * t["m"] + da * t["a"]
        xt = _mm_xr(eye_h, xm, _BNT)
        dgc = (dgam * gam - ddlt * dlt + jnp.sum(xm, axis=2, keepdims=True) - jnp.sum(xt, axis=2, keepdims=True))
        dgl = jnp.sum(ddlt * dlt, axis=1, keepdims=True) + jnp.sum(
            jnp.sum(ds1 * s0, axis=2, keepdims=True), axis=1, keepdims=True) * jnp.max(egl, axis=1, keepdims=True)
        dgc = dgc + jnp.where(last_row, dgl, 0.0)
        dg = _mm_xr(u_ge_h, dgc, _BNN)
        for h in range(DN_HEADS):
            sl = slice(h * LANES, (h + 1) * LANES)
            dqkv_ref[:, sl] = dq[h]
            dqkv_ref[:, DN_W + h * LANES:DN_W + (h + 1) * LANES] = dk[h]
            dqkv_ref[:, 2 * DN_W + h * LANES:2 * DN_W + (h + 1) * LANES] = dv[h]
            dbb_ref[:, sl] = jnp.broadcast_to(dbeta[h], (CHUNK, LANES))
            dg_ref[:, sl] = dg[h]

    rev = lambda w: pl.BlockSpec((CHUNK, w), lambda n: (N - 1 - n, 0))
    hbm = pl.BlockSpec(memory_space=pl.ANY)
    outs = pl.pallas_call(
        body, name="dn_bwd", grid=(N,),
        in_specs=[rev(3 * DN_W), rev(DN_W), rev(DN_W), rev(DN_W),
                  pl.BlockSpec((DN_HEADS, CHUNK, CHUNK), lambda n: (0, N - 1 - n, 0)),
                  pl.BlockSpec((1, DN_HEADS, DN_HD, DN_HD), lambda n: (N - 1 - n, 0, 0, 0)), rev(DN_W)]
        + [hbm] * (2 * nx),
        out_specs=[rev(3 * DN_W), rev(DN_W), rev(DN_W)] + [hbm] * nx,
        out_shape=[_sds((S, 3 * DN_W), F32), _sds((S, DN_W), F32), _sds((S, DN_W), F32)]
        + (exchange.out_shape if exchange else []),
        scratch_shapes=[pltpu.VMEM((DN_HEADS, DN_HD, DN_HD), F32)]
        + (exchange.scratch + early.scratch if exchange else []),
        compiler_params=_cp(1),
    )(qkv, bb, gcb, glb, tinv_all, sall, do, *ex_src, *early_src)
    return outs[:3], list(outs[3:])


def _dn_prep_bwd_a(p, dqkv, dbb, dgb, conv_w, a_row, dtb_row, tm=256):
    S, PC = p.shape
    W3 = 3 * DN_W
    nhalo = tm // 8

    def body(x_ref, halo_ref, w_ref, ba_ref, a_ref, dtb_ref, dqkv_ref, dbb_ref, dgb_ref,
             dc_ref, dba_ref, dal_ref, ddt_ref):
        i = pl.program_id(0)

        @pl.when(i == 0)
        def _():
            dal_ref[...] = jnp.zeros_like(dal_ref)
            ddt_ref[...] = jnp.zeros_like(ddt_ref)

        halo = jnp.where(i > 0, halo_ref[...], 0.0)
        xf = jnp.concatenate([halo, x_ref[...]], axis=0)
        acc = jnp.zeros((tm, W3), F32)
        for k in range(CONV_K):
            sh = CONV_K - 1 - k
            xs = xf if sh == 0 else pltpu.roll(xf, sh, 0)
            acc = acc + xs[8:, :] * w_ref[k:k + 1, :]
        s = _silu(acc)
        ds_act = _dsilu(acc)
        for gi in range(2 * DN_HEADS):
            sl = slice(gi * LANES, (gi + 1) * LANES)
            sg = s[:, sl]
            rinv = lax.rsqrt(jnp.sum(sg * sg, axis=1, keepdims=True) + EPS)
            nh = sg * rinv
            dn = dqkv_ref[:, sl] * (DN_HD ** -0.5 if gi < DN_HEADS else 1.0)
            dsg = rinv * (dn - nh * jnp.sum(dn * nh, axis=1, keepdims=True))
            dc_ref[:, sl] = dsg * ds_act[:, sl]
        dc_ref[:, 2 * DN_W:] = dqkv_ref[:, 2 * DN_W:] * ds_act[:, 2 * DN_W:]

        ba = ba_ref[...]
        beta = _sigmoid(ba)
        ea = jnp.exp(a_ref[...])
        pre = ba + dtb_ref[...]
        g = -ea * _softplus(pre)
        lr, lc = _iota2((DN_W, LANES), 0), _iota2((DN_W, LANES), 1)
        pick_b = jnp.where(lc == lr // LANES, 1.0 / LANES, 0.0)
        pick_g = jnp.where(lc == lr // LANES + DN_HEADS, 1.0 / LANES, 0.0)
        dbeta = _mm_xl(dbb_ref[...], pick_b)
        dg = _mm_xl(dgb_ref[...], pick_g)
        lane = _iota2((1, LANES), 1)
        da = dg * (-ea) * _sigmoid(pre)
        dba_ref[...] = jnp.where(lane < DN_HEADS, dbeta * beta * (1.0 - beta),
                                 jnp.where(lane < 2 * DN_HEADS, da, 0.0)).astype(dba_ref.dtype)
        dal_ref[...] += jnp.sum(dg * g, axis=0, keepdims=True)
        ddt_ref[...] += jnp.sum(da, axis=0, keepdims=True)

    return pl.pallas_call(
        body, name="dn_prep_bwd_a", grid=(S // tm,),
        in_specs=[_rb(tm, W3, 0), pl.BlockSpec((8, W3), lambda i: (jnp.maximum(i * nhalo - 1, 0), 0)),
                  _fs((CONV_K, W3)), _rb(tm, LANES, (PC - LANES) // LANES), _fs((1, LANES)), _fs((1, LANES)),
                  _rb(tm, W3), _rb(tm, DN_W), _rb(tm, DN_W)],
        out_specs=[_rb(tm, W3), _rb(tm, LANES), _fs((1, LANES)), _fs((1, LANES))],
        out_shape=[_sds((S, W3), F32), _sds((S, LANES), _MXU_DTYPE), _sds((1, LANES), F32), _sds((1, LANES), F32)],
        compiler_params=_cp(1),
    )(p, p, conv_w, p, a_row, dtb_row, dqkv, dbb, dgb)


def _dn_prep_bwd_b(p, dc, conv_w, tm=256):
    S = p.shape[0]
    W3 = 3 * DN_W
    nhalo = tm // 8
    nblk = S // tm

    def body(x_ref, xh_ref, dc_ref, dch_ref, w_ref, dx_ref, dw_ref):
        i = pl.program_id(0)

        @pl.when(i == 0)
        def _():
            dw_ref[...] = jnp.zeros_like(dw_ref)

        dcv = dc_ref[...]
        xf = jnp.concatenate([jnp.where(i > 0, xh_ref[...], 0.0), x_ref[...]], axis=0)
        df = jnp.concatenate([dcv, jnp.where(i < nblk - 1, dch_ref[...], 0.0)], axis=0)
        acc = jnp.zeros((tm, W3), F32)
        for k in range(CONV_K):
            sh = CONV_K - 1 - k
            xs = xf if sh == 0 else pltpu.roll(xf, sh, 0)
            dw_ref[k:k + 1, :] += jnp.sum(dcv * xs[8:, :], axis=0, keepdims=True)
            ds = df if sh == 0 else pltpu.roll(df, tm + 8 - sh, 0)
            acc = acc + ds[:tm, :] * w_ref[k:k + 1, :]
        dx_ref[...] = acc.astype(dx_ref.dtype)

    return pl.pallas_call(
        body, name="dn_prep_bwd_b", grid=(nblk,),
        in_specs=[_rb(tm, W3, 0), pl.BlockSpec((8, W3), lambda i: (jnp.maximum(i * nhalo - 1, 0), 0)),
                  _rb(tm, W3), pl.BlockSpec((8, W3), lambda i: (jnp.minimum((i + 1) * nhalo, S // 8 - 1), 0)),
                  _fs((CONV_K, W3))],
        out_specs=[_rb(tm, W3), _fs((CONV_K, W3))],
        out_shape=[_sds((S, W3), _MXU_DTYPE), _sds((CONV_K, W3), F32)], compiler_params=_cp(1),
    )(p, p, dc, dc, conv_w)


def _gate(o_att, o_dn, p, gn, tm=256):
    S = p.shape[0]

    def body(oa_ref, zs_ref, od_ref, zd_ref, gn_ref, osb_ref, odn_ref):
        osb_ref[...] = (oa_ref[...] * _silu(zs_ref[...])).astype(osb_ref.dtype)
        for h in range(DN_HEADS):
            sl = slice(h * LANES, (h + 1) * LANES)
            o = od_ref[:, sl]
            r = lax.rsqrt(jnp.mean(o * o, axis=1, keepdims=True) + EPS)
            odn_ref[:, sl] = (o * r * gn_ref[...] * _silu(zd_ref[:, sl])).astype(odn_ref.dtype)

    return pl.pallas_call(
        body, name="gate", grid=(S // tm,),
        in_specs=[_rb(tm, SB_W), _rb(tm, SB_W, C_SB_Z // SB_W), _rb(tm, DN_W), _rb(tm, DN_W, C_DN_Z // DN_W),
                  _fs((1, LANES))],
        out_specs=[_rb(tm, SB_W), _rb(tm, DN_W)],
        out_shape=[_sds((S, SB_W), _MXU_DTYPE), _sds((S, DN_W), _MXU_DTYPE)], compiler_params=_cp(1),
    )(o_att, p, o_dn, p, gn)


def _gate_bwd(db_sb, db_dn, wb_sb, wb_dn, o_att, o_dn, p, gn, tm=256):
    S = p.shape[0]
    D = db_sb.shape[1]

    def body(dbs_ref, dbd_ref, ws_ref, wd_ref, oa_ref, zs_ref, od_ref, zd_ref, gn_ref,
             doa_ref, dzs_ref, dod_ref, dzd_ref, dgn_ref):
        @pl.when(pl.program_id(0) == 0)
        def _():
            dgn_ref[...] = jnp.zeros_like(dgn_ref)

        do_sb = _mm(dbs_ref[...], ws_ref[...], _NT)
        zs = zs_ref[...]
        doa_ref[...] = do_sb * _silu(zs)
        dzs_ref[...] = (do_sb * oa_ref[...] * _dsilu(zs)).astype(dzs_ref.dtype)
        do_dnn = _mm(dbd_ref[...], wd_ref[...], _NT)
        gnv = gn_ref[...]
        for h in range(DN_HEADS):
            sl = slice(h * LANES, (h + 1) * LANES)
            o, z, dout = od_ref[:, sl], zd_ref[:, sl], do_dnn[:, sl]
            r = lax.rsqrt(jnp.mean(o * o, axis=1, keepdims=True) + EPS)
            oh = o * r
            sz = _silu(z)
            dzd_ref[:, sl] = (dout * oh * gnv * _dsilu(z)).astype(dzd_ref.dtype)
            dgn_ref[...] += jnp.sum(dout * sz * oh, axis=0, keepdims=True)
            doh = dout * gnv * sz
            dod_ref[:, sl] = r * (doh - oh * jnp.mean(doh * oh, axis=1, keepdims=True))

    return pl.pallas_call(
        body, name="gate_bwd", grid=(S // tm,),
        in_specs=[_rb(tm, D), _rb(tm, D), _fs((SB_W, D)), _fs((DN_W, D)), _rb(tm, SB_W),
                  _rb(tm, SB_W, C_SB_Z // SB_W), _rb(tm, DN_W), _rb(tm, DN_W, C_DN_Z // DN_W), _fs((1, LANES))],
        out_specs=[_rb(tm, SB_W), _rb(tm, SB_W), _rb(tm, DN_W), _rb(tm, DN_W), _fs((1, LANES))],
        out_shape=[_sds((S, SB_W), F32), _sds((S, SB_W), _MXU_DTYPE), _sds((S, DN_W), F32),
                   _sds((S, DN_W), _MXU_DTYPE), _sds((1, LANES), F32)],
        compiler_params=_cp(1),
    )(db_sb, db_dn, wb_sb, wb_dn, o_att, p, o_dn, p, gn)


def _branch(o_sb, o_dnn, wb_sb, wb_dn, p, D, tm=256):
    S = p.shape[0]

    def body(os_ref, od_ref, ws_ref, wd_ref, ms_ref, md_ref, y_ref, bs_ref, bd_ref):
        bs = _mm(os_ref[...], ws_ref[...])
        bdn = _mm(od_ref[...], wd_ref[...])
        bs_ref[...] = bs
        bd_ref[...] = bdn
        y_ref[...] = (_sigmoid(ms_ref[...]) * bs + _sigmoid(md_ref[...]) * bdn).astype(y_ref.dtype)

    return pl.pallas_call(
        body, name="branch", grid=(S // tm,),
        in_specs=[_rb(tm, SB_W), _rb(tm, DN_W), _fs((SB_W, D)), _fs((DN_W, D)),
                  _rb(tm, D, C_MG // D), _rb(tm, D, C_MG // D + 1)],
        out_specs=[_rb(tm, D), _rb(tm, D), _rb(tm, D)],
        out_shape=[_sds((S, D), _MXU_DTYPE), _sds((S, D), F32), _sds((S, D), F32)], compiler_params=_cp(1),
    )(o_sb, o_dnn, wb_sb, wb_dn, p, p)


def _out_proj(x, y, w_out, gate, tm=256):
    S, D = x.shape

    def body(x_ref, y_ref, w_ref, g_ref, xn_ref, out_ref):
        out = _mm(y_ref[...], w_ref[...])
        out_ref[...] = out
        xn_ref[...] = x_ref[...] + g_ref[...] * out

    return pl.pallas_call(
        body, name="out_proj", grid=(S // tm,),
        in_specs=[_rb(tm, D), _rb(tm, D), _fs((D, D)), _fs((1, D))],
        out_specs=[_rb(tm, D), _rb(tm, D)],
        out_shape=[_sds((S, D), F32), _sds((S, D), F32)], compiler_params=_cp(1),
    )(x, y, w_out, gate)


def _out_bwd(dxn, out, gate, w_out, p, b_sb, b_dn, tm=256):
    S, D = dxn.shape

    def body(dxn_ref, out_ref, g_ref, w_ref, ms_ref, md_ref, bs_ref, bd_ref,
             dout_ref, dbs_ref, dbd_ref, dm_ref, dgate_ref):
        @pl.when(pl.program_id(0) == 0)
        def _():
            dgate_ref[...] = jnp.zeros_like(dgate_ref)

        dxv = dxn_ref[...]
        dgate_ref[...] += jnp.sum(dxv * out_ref[...], axis=0, keepdims=True)
        dout = (g_ref[...] * dxv).astype(dout_ref.dtype)
        dout_ref[...] = dout
        dy = _mm(dout, w_ref[...], _NT)
        s1, s2 = _sigmoid(ms_ref[...]), _sigmoid(md_ref[...])
        dbs_ref[...] = (dy * s1).astype(dbs_ref.dtype)
        dbd_ref[...] = (dy * s2).astype(dbd_ref.dtype)
        dm_ref[:, :D] = (dy * bs_ref[...] * s1 * (1.0 - s1)).astype(dm_ref.dtype)
        dm_ref[:, D:] = (dy * bd_ref[...] * s2 * (1.0 - s2)).astype(dm_ref.dtype)

    return pl.pallas_call(
        body, name="out_bwd", grid=(S // tm,),
        in_specs=[_rb(tm, D), _rb(tm, D), _fs((1, D)), _fs((D, D)), _rb(tm, D, C_MG // D),
                  _rb(tm, D, C_MG // D + 1), _rb(tm, D), _rb(tm, D)],
        out_specs=[_rb(tm, D), _rb(tm, D), _rb(tm, D), _rb(tm, 2 * D), _fs((1, D))],
        out_shape=[_sds((S, D), _MXU_DTYPE)] * 3 + [_sds((S, 2 * D), _MXU_DTYPE), _sds((1, D), F32)],
        compiler_params=_cp(1),
    )(dxn, out, gate, w_out, p, p, b_sb, b_dn)


def _loss_head(xf, target, tm=256):
    S, D = xf.shape

    def body(x_ref, t_ref, dy_ref, loss_ref):
        @pl.when(pl.program_id(0) == 0)
        def _():
            loss_ref[...] = jnp.zeros_like(loss_ref)

        e = x_ref[...] - t_ref[...]
        dy_ref[...] = e * (1.0 / D)
        row = jnp.sum(e * e, axis=1, keepdims=True) * (1.0 / D)
        loss_ref[...] += 0.5 * jnp.sum(row, axis=0, keepdims=True)

    return pl.pallas_call(
        body, name="loss_head", grid=(S // tm,),
        in_specs=[_rb(tm, D), _rb(tm, D)], out_specs=[_rb(tm, D), _fs((1, LANES))],
        out_shape=[_sds((S, D), F32), _sds((1, LANES), F32)], compiler_params=_cp(1),
    )(xf, target)


def _ada_fwd(c_all, ada_w, ada_b_sh):
    L, D, n = ada_w.shape
    B = c_all.shape[0]

    def body(c_ref, w_ref, b_ref, o_ref):
        sc = _silu(c_ref[...])
        o_ref[0] = _mm(sc, w_ref[0]) + b_ref[0]

    return pl.pallas_call(
        body, name="ada_fwd", grid=(L,),
        in_specs=[_fs((B, D)), pl.BlockSpec((1, D, n), lambda l: (l, 0, 0)), pl.BlockSpec((1, 1, n), lambda l: (l, 0, 0))],
        out_specs=pl.BlockSpec((1, B, n), lambda l: (l, 0, 0)),
        out_shape=_sds((L, B, n), F32), compiler_params=_cp(1),
    )(c_all, ada_w, ada_b_sh)


def _ada_bwd(c_all_t, dmod_sh):
    D, B = c_all_t.shape
    L, _, n = dmod_sh.shape

    def body(c_ref, d_ref, o_ref):
        acc = jnp.zeros((D, n), F32)
        for b in range(B):
            acc = acc + _silu(c_ref[:, b:b + 1]) * d_ref[0, b:b + 1, :]
        o_ref[0] = acc

    return pl.pallas_call(
        body, name="ada_bwd", grid=(L,),
        in_specs=[_fs((D, B)), pl.BlockSpec((1, B, n), lambda l: (l, 0, 0))],
        out_specs=pl.BlockSpec((1, D, n), lambda l: (l, 0, 0)),
        out_shape=_sds((L, D, n), F32), compiler_params=_cp(1),
    )(c_all_t, dmod_sh)


def _sum_parts(name, parts):
    P, R, C = parts.shape
    tr = _pick(R, max(16, min(512, (1 << 19) // (P * C))), 16) if R % 16 == 0 else R

    def body(p_ref, o_ref):
        acc = p_ref[0].astype(F32)
        for k in range(1, P):
            acc = acc + p_ref[k].astype(F32)
        o_ref[...] = acc

    return pl.pallas_call(
        body, name=name, grid=(R // tr,),
        in_specs=[pl.BlockSpec((P, tr, C), lambda i: (0, i, 0))], out_specs=_rb(tr, C),
        out_shape=_sds((R, C), F32), compiler_params=_cp(1),
    )(parts)


def _adamw(name, w, g, m, v):
    L, R, C = w.shape
    tr = _pick(R, 256, 8) if R % 8 == 0 else R
    c1 = 1.0 - ADAM_B1 ** ADAM_STEP
    c2 = 1.0 - ADAM_B2 ** ADAM_STEP

    def body(w_ref, g_ref, m_ref, v_ref, d_ref, mo_ref, vo_ref):
        gv = g_ref[...]
        mn = ADAM_B1 * m_ref[...] + (1.0 - ADAM_B1) * gv
        vn = ADAM_B2 * v_ref[...] + (1.0 - ADAM_B2) * (gv * gv)
        mo_ref[...] = mn
        vo_ref[...] = vn
        d_ref[...] = -ADAM_LR * ((mn / c1) / (jnp.sqrt(vn / c2) + ADAM_EPS) + ADAM_WD * w_ref[...])

    spec = pl.BlockSpec((1, tr, C), lambda l, i: (l, i, 0))
    return pl.pallas_call(
        body, name=name, grid=(L, R // tr),
        in_specs=[spec] * 4, out_specs=[spec] * 3, out_shape=[_sds((L, R, C), F32)] * 3, compiler_params=_cp(2),
    )(w, g, m, v)


def _ag_small(name, blk):
    R, C = blk.shape

    def body(x_ref, out_ref, send_sems, recv_sems, local_sem):
        x, y, c = lax.axis_index("x"), lax.axis_index("y"), lax.axis_index("c")
        me, sibling = (x, y, c), (x, y, 1 - c)
        chips = [(1 - x, y), (x, 1 - y), (1 - x, 1 - y)]

        def rows(px, py, pc):
            return out_ref.at[pl.ds((4 * px + 2 * py + pc) * R, R), :]

        def copy(k, block, to, src=None):
            return pltpu.make_async_remote_copy(
                src_ref=rows(*block) if src is None else src, dst_ref=rows(*block),
                send_sem=send_sems.at[k], recv_sem=recv_sems.at[k], device_id=to, device_id_type=MESH)

        mine = pltpu.make_async_copy(x_ref, rows(*me), local_sem)
        mine.start()
        first = [copy(0, me, sibling, src=x_ref)]
        first += [copy(1 + j, me, (*chip, c), src=x_ref) for j, chip in enumerate(chips)]
        for cp in first:
            cp.start()
        passed = [copy(4 + j, (*chip, c), sibling) for j, chip in enumerate(chips)]
        for j, chip in enumerate(chips):
            copy(1 + j, (*chip, c), me).wait_recv()
            passed[j].start()
        copy(0, sibling, me).wait_recv()
        for j, chip in enumerate(chips):
            copy(4 + j, (*chip, 1 - c), me).wait_recv()
        for cp in first + passed:
            cp.wait_send()
        mine.wait()

    return pl.pallas_call(
        body, name=name, out_shape=_sds((8 * R, C), blk.dtype),
        in_specs=[pl.BlockSpec(memory_space=pltpu.VMEM)], out_specs=pl.BlockSpec(memory_space=pltpu.VMEM),
        scratch_shapes=[pltpu.SemaphoreType.DMA((7,)), pltpu.SemaphoreType.DMA((7,)), pltpu.SemaphoreType.DMA],
    )(blk)


def _row_chunks(ts, row_axis):
    pieces = []
    for t, a in enumerate(ts):
        rows = a.shape[row_axis]
        n = 4 if rows >= 1024 else 1
        pieces += [(t, i * (rows // n), rows // n) for i in range(n)]
    return pieces


def _ag_weights_first(ts):
    nt = len(ts)
    pieces = _row_chunks(ts, 0)
    NP = len(pieces)
    sizes = [nr * ts[t].shape[1] for t, _, nr in pieces]
    split = next(pi for pi in range(NP + 1) if 2 * sum(sizes[:pi]) >= sum(sizes))

    def body(*refs):
        w, out = refs[:nt], refs[nt:2 * nt]
        send_sems, recv_sems, local_sems = refs[2 * nt:]
        x, y, c = lax.axis_index("x"), lax.axis_index("y"), lax.axis_index("c")
        me, sibling = (x, y, c), (x, y, 1 - c)
        mine = 2 * x + y
        chips = [(1 - x, y), (x, 1 - y), (1 - x, 1 - y)]

        def blk(t, shard, r0, nr):
            return out[t].at[shard, r0:r0 + nr, :]

        def copy(k, dst, to, src=None):
            return pltpu.make_async_remote_copy(
                src_ref=dst if src is None else src, dst_ref=dst, send_sem=send_sems.at[k], recv_sem=recv_sems.at[k],
                device_id=to, device_id_type=MESH)

        own = [pltpu.make_async_copy(w[t], out[t].at[mine], local_sems.at[t]) for t in range(nt)]
        for cp in own:
            cp.start()
        for fetcher, lo, hi in ((0, 0, split), (1, split, NP)):
            @pl.when(c == fetcher)
            def _(lo=lo, hi=hi):
                sent = []
                for j, chip in enumerate(chips):
                    for pi in range(lo, hi):
                        t, r0, nr = pieces[pi]
                        sent.append(copy(j * NP + pi, blk(t, mine, r0, nr), (*chip, c), src=w[t].at[r0:r0 + nr, :]))
                        sent[-1].start()
                for j, chip in enumerate(chips):
                    theirs = 2 * chip[0] + chip[1]
                    for pi in range(lo, hi):
                        t, r0, nr = pieces[pi]
                        copy(j * NP + pi, blk(t, theirs, r0, nr), me).wait_recv()
                        sent.append(copy((3 + j) * NP + pi, blk(t, theirs, r0, nr), sibling))
                        sent[-1].start()
                for cp in sent:
                    cp.wait_send()

            @pl.when(c != fetcher)
            def _(lo=lo, hi=hi):
                for j, chip in enumerate(chips):
                    theirs = 2 * chip[0] + chip[1]
                    for pi in range(lo, hi):
                        t, r0, nr = pieces[pi]
                        copy((3 + j) * NP + pi, blk(t, theirs, r0, nr), me).wait_recv()

        for cp in own:
            cp.wait()

    return pl.pallas_call(
        body, name="ag_weights_first", out_shape=[_sds((4,) + a.shape, a.dtype) for a in ts],
        in_specs=[pl.BlockSpec(memory_space=pl.ANY)] * nt, out_specs=[pl.BlockSpec(memory_space=pltpu.VMEM)] * nt,
        scratch_shapes=[pltpu.SemaphoreType.DMA((6 * NP,)), pltpu.SemaphoreType.DMA((6 * NP,)),
                        pltpu.SemaphoreType.DMA((nt,))],
        compiler_params=pltpu.CompilerParams(vmem_limit_bytes=_VMEM_LIMIT),
    )(*ts)


class _WeightGather:
    def __init__(self, ts):
        self.nt = len(ts)
        self.pieces = _row_chunks(ts, 0)
        NP = len(self.pieces)
        self.out_shape = [_sds((4,) + a.shape, a.dtype) for a in ts]
        self.scratch = [pltpu.SemaphoreType.DMA((3 * NP,)), pltpu.SemaphoreType.DMA((3 * NP,)),
                        pltpu.SemaphoreType.DMA((self.nt,))]

    def _copies(self, src, out, sems):
        send_sems, recv_sems, local_sems = sems
        NP = len(self.pieces)
        x, y, c = lax.axis_index("x"), lax.axis_index("y"), lax.axis_index("c")
        mine = 2 * x + y
        own = [pltpu.make_async_copy(src[t], out[t].at[mine], local_sems.at[t]) for t in range(self.nt)]
        sends, recvs = [], []
        for j, chip in enumerate([(1 - x, y), (x, 1 - y), (1 - x, 1 - y)]):
            theirs = 2 * chip[0] + chip[1]
            for pi, (t, r0, nr) in enumerate(self.pieces):
                idx = j * NP + pi
                sends.append(pltpu.make_async_remote_copy(
                    src_ref=src[t].at[r0:r0 + nr, :], dst_ref=out[t].at[mine, r0:r0 + nr, :],
                    send_sem=send_sems.at[idx], recv_sem=recv_sems.at[idx], device_id=(*chip, c), device_id_type=MESH))
                recvs.append(pltpu.make_async_remote_copy(
                    src_ref=out[t].at[theirs, r0:r0 + nr, :], dst_ref=out[t].at[theirs, r0:r0 + nr, :],
                    send_sem=send_sems.at[idx], recv_sem=recv_sems.at[idx], device_id=(x, y, c), device_id_type=MESH))
        return own, sends, recvs

    def start(self, src, out, sems):
        own, sends, _ = self._copies(src, out, sems)
        for cp in own + sends:
            cp.start()

    def wait(self, src, out, sems):
        own, sends, recvs = self._copies(src, out, sems)
        for cp in recvs:
            cp.wait_recv()
        for cp in sends:
            cp.wait_send()
        for cp in own:
            cp.wait()


class _GradExchange:
    def __init__(self, ts, layer):
        self.nt, self.layer = len(ts), layer
        self.pieces = _row_chunks(ts, 1)
        NP = len(self.pieces)
        self.out_shape = [_sds((8,) + a.shape[1:], a.dtype) for a in ts]
        self.scratch = [pltpu.SemaphoreType.DMA((7 * NP,)), pltpu.SemaphoreType.DMA((7 * NP,)),
                        pltpu.SemaphoreType.DMA((self.nt,))]

    def _copies(self, src, out, sems):
        send_sems, recv_sems, local_sems = sems
        NP = len(self.pieces)
        x, y, c = lax.axis_index("x"), lax.axis_index("y"), lax.axis_index("c")
        me = 4 * x + 2 * y + c
        owner = c == self.layer
        own = [pltpu.make_async_copy(src[t].at[2 * x + y], out[t].at[me], local_sems.at[t]) for t in range(self.nt)]
        rel = []
        for k in range(1, 8):
            px = 1 - x if k & 4 else x
            py = 1 - y if k & 2 else y
            source = 4 * px + 2 * py + (1 - c if k & 1 else c)
            sends, recvs = [], []
            for pi, (t, r0, nr) in enumerate(self.pieces):
                idx = (k - 1) * NP + pi
                sends.append(pltpu.make_async_remote_copy(
                    src_ref=src[t].at[2 * px + py, r0:r0 + nr, :], dst_ref=out[t].at[me, r0:r0 + nr, :],
                    send_sem=send_sems.at[idx], recv_sem=recv_sems.at[idx], device_id=(px, py, self.layer),
                    device_id_type=MESH))
                recvs.append(pltpu.make_async_remote_copy(
                    src_ref=out[t].at[source, r0:r0 + nr, :], dst_ref=out[t].at[source, r0:r0 + nr, :],
                    send_sem=send_sems.at[idx], recv_sem=recv_sems.at[idx], device_id=(x, y, c),
                    device_id_type=MESH))
            rel.append((jnp.logical_not(owner) if k & 1 else owner, sends, recvs))
        return owner, own, rel

    def start(self, src, out, sems):
        owner, own, rel = self._copies(src, out, sems)

        @pl.when(owner)
        def _():
            for cp in own:
                cp.start()

        for sending, sends, _ in rel:
            @pl.when(sending)
            def _(sends=sends):
                for cp in sends:
                    cp.start()

    def wait(self, src, out, sems):
        owner, own, rel = self._copies(src, out, sems)

        @pl.when(owner)
        def _():
            for _, _, recvs in rel:
                for cp in recvs:
                    cp.wait_recv()
            for cp in own:
                cp.wait()

        for sending, sends, _ in rel:
            @pl.when(sending)
            def _(sends=sends):
                for cp in sends:
                    cp.wait_send()


def _sibling_join(ts):
    nt = len(ts)
    pieces = _row_chunks(ts, 0)
    NP = len(pieces)

    def body(*refs):
        src, out = refs[:nt], refs[nt:2 * nt]
        send_sems, recv_sems, local_sems = refs[2 * nt:]
        x, y, c = lax.axis_index("x"), lax.axis_index("y"), lax.axis_index("c")
        own = [pltpu.make_async_copy(src[t], out[t].at[c], local_sems.at[t]) for t in range(nt)]
        for cp in own:
            cp.start()
        sent = []
        for pi, (t, r0, nr) in enumerate(pieces):
            sent.append(pltpu.make_async_remote_copy(
                src_ref=src[t].at[r0:r0 + nr, :], dst_ref=out[t].at[c, r0:r0 + nr, :], send_sem=send_sems.at[pi],
                recv_sem=recv_sems.at[pi], device_id=(x, y, 1 - c), device_id_type=MESH))
            sent[-1].start()
        for pi, (t, r0, nr) in enumerate(pieces):
            pltpu.make_async_remote_copy(
                src_ref=src[t].at[r0:r0 + nr, :], dst_ref=out[t].at[1 - c, r0:r0 + nr, :], send_sem=send_sems.at[pi],
                recv_sem=recv_sems.at[pi], device_id=(x, y, c), device_id_type=MESH).wait_recv()
        for cp in sent:
            cp.wait_send()
        for cp in own:
            cp.wait()

    vmem = pl.BlockSpec(memory_space=pltpu.VMEM)
    return pl.pallas_call(
        body, name="sibling_join", out_shape=[_sds((2,) + a.shape, a.dtype) for a in ts],
        in_specs=[vmem] * nt, out_specs=[vmem] * nt,
        scratch_shapes=[pltpu.SemaphoreType.DMA((NP,)), pltpu.SemaphoreType.DMA((NP,)),
                        pltpu.SemaphoreType.DMA((nt,))],
        compiler_params=pltpu.CompilerParams(vmem_limit_bytes=_VMEM_LIMIT),
    )(*ts)


def _late_weights(gathered):
    g_bs, g_bd, g_out = gathered
    cat = lambda g, axis: jnp.concatenate([g[s] for s in range(4)], axis=axis)
    return dict(wb_sb=cat(g_bs, 1), wb_dn=cat(g_bd, 1), w_out=cat(g_out, 0))


def _layer_fwd(x, shift, scale, gate, lw, next_shards=None, late_shards=None):
    D = x.shape[1]
    h = _norm_mod(x, lw["norm_g"], scale, shift)
    if late_shards is None:
        p = _matmul("in_proj", h, lw["w_cat"], "nn", F32, tm_cap=1024, tn_cap=896)
    else:
        p, late = _matmul("in_proj", h, lw["w_cat"], "nn", F32, tm_cap=1024, tn_cap=896,
                          exchange=_WeightGather(late_shards), ex_src=late_shards)
        lw = {**lw, **_late_weights(late)}
    qn, kn = _sb_prep(p, lw["gq_t"], lw["gk_t"])
    qkv, bb, gcb, glb = _dn_prep(p, lw["conv_w"], lw["a_row"], lw["dtb_row"])
    if next_shards is None:
        (o_att, tot, nblocks), _ = _sb_fwd(qn, kn, p)
        (o_dn, tinv, sall), gathered = _dn_fwd(qkv, bb, gcb, glb)
    else:
        first, rest = next_shards[:1], next_shards[1:]
        (o_att, tot, nblocks), g_rest = _sb_fwd(qn, kn, p, _WeightGather(rest), rest)
        (o_dn, tinv, sall), g_first = _dn_fwd(qkv, bb, gcb, glb, _WeightGather(first), first)
        gathered = list(g_first) + g_rest
    o_sb, o_dnn = _gate(o_att, o_dn, p, lw["gn"])
    y, b_sb, b_dn = _branch(o_sb, o_dnn, lw["wb_sb"], lw["wb_dn"], p, D)
    x_next, out = _out_proj(x, y, lw["w_out"], gate)
    res = dict(x=x, h=h, p=p, qn=qn, kn=kn, o_att=o_att, tot=tot, nblocks=nblocks, qkv=qkv, bb=bb, gcb=gcb, glb=glb, o_dn=o_dn,
               tinv=tinv, sall=sall, o_sb=o_sb, o_dnn=o_dnn, y=y, b_sb=b_sb, b_dn=b_dn, out=out,
               shift=shift, scale=scale, gate=gate)
    return x_next, res, lw, gathered


def _layer_bwd(dxn, res, lw, pending=None):
    p = res["p"]
    dout, db_sb, db_dn, dm, dgate = _out_bwd(dxn, res["out"], res["gate"], lw["w_out"], p, res["b_sb"], res["b_dn"])
    dw_out = _matmul("dw_out", res["y"], dout, "tn", _MXU_DTYPE)
    dwb_sb = _matmul("dwb_sb", res["o_sb"], db_sb, "tn", _MXU_DTYPE)
    dwb_dn = _matmul("dwb_dn", res["o_dnn"], db_dn, "tn", _MXU_DTYPE)
    do_att, dz_sb, do_dn, dz_dn, dgn = _gate_bwd(db_sb, db_dn, lw["wb_sb"], lw["wb_dn"], res["o_att"], res["o_dn"],
                                                  p, lw["gn"])
    D = dxn.shape[1]
    by_shard = lambda g: g.reshape(g.shape[0], 4, g.shape[1] // 4).transpose(1, 0, 2)
    send = [by_shard(dwb_sb), by_shard(dwb_dn), dw_out.reshape(4, D // 4, D)]
    dn_args = (res["qkv"], res["bb"], res["gcb"], res["glb"], res["tinv"], res["sall"], do_dn)
    if pending is None:
        (dqn, dkn, dv), _ = _sb_bwd(res["qn"], res["kn"], p, do_att, res["tot"], res["nblocks"])
        (dqkv, dbb, dgb), received = _dn_bwd(*dn_args)
    else:
        above, above_send = pending
        (dqn, dkn, dv), got_in = _sb_bwd(res["qn"], res["kn"], p, do_att, res["tot"], res["nblocks"],
                                         _GradExchange(above_send[:1], above), above_send[:1])
        (dqkv, dbb, dgb), got_rest = _dn_bwd(*dn_args, _GradExchange(above_send[1:], above), above_send[1:],
                                             _GradExchange(send, above - 1), send)
        received, send = list(got_in) + got_rest, []
    dq_sb, dk_sb, dgq, dgk = _sb_prep_bwd(p, dqn, dkn, lw["gq_t"], lw["gk_t"])
    dc, dp_ba, dal, ddt = _dn_prep_bwd_a(p, dqkv, dbb, dgb, lw["conv_w"], lw["a_row"], lw["dtb_row"])
    dp_dn, dconv = _dn_prep_bwd_b(p, dc, lw["conv_w"])
    dp = jnp.concatenate([dp_dn, dz_dn, dq_sb, dk_sb, dv.astype(_MXU_DTYPE), dz_sb, dm, dp_ba], axis=1)
    dw_cat = _matmul("dw_cat", res["h"], dp, "tn", _MXU_DTYPE, tm_cap=1024, tn_cap=896, tk_cap=2048)
    send = [_shards_from_cat(dw_cat, D)] + send
    if pending is None:
        dh = _matmul("dh", dp, lw["w_cat"], "nt", F32, tm_cap=1024, tk_cap=896)
    else:
        dh, arrived = _matmul("dh", dp, lw["w_cat"], "nt", F32, tm_cap=1024, tk_cap=896,
                              exchange=_GradExchange(send, pending[0] - 1), ex_src=send, ex_prev=received[:1])
        received, send = arrived + list(received[1:]), []
    dx, dshift, dscale, dnorm_g = _norm_mod_bwd(res["x"], dh, dxn, lw["norm_g"], res["scale"])
    small = dict(dmod=jnp.concatenate([dshift, dscale, dgate], axis=1)[0], norm_g=dnorm_g[0],
                 sb_q_g=dgq.reshape(SB_HEADS, SB_HD).sum(0), sb_k_g=dgk.reshape(SB_HEADS, SB_HD).sum(0),
                 conv_w=dconv, dn_a_log=dal[0, DN_HEADS:2 * DN_HEADS], dn_dt_bias=ddt[0, DN_HEADS:2 * DN_HEADS],
                 dn_norm_g=dgn[0])
    return dx, small, send, received


def _cat_cols(w, D):
    return jnp.concatenate([w[:, 2048:4096], w[:, 0:2048], w[:, 4104:4104 + 2 * D], w[:, 4096:4104],
                            jnp.zeros((w.shape[0], LANES - 8), w.dtype)], axis=1)


def _shards_from_cat(g, D):
    n = (4104 + 2 * D) // 4
    segments = ((0, 2048, 2048), (2048, 4096, 0), (4096, 4104, 4096 + 2 * D), (4104, 4104 + 2 * D, 4096))

    def shard(lo, hi):
        cuts = [(c0 + max(lo, s0) - s0, c0 + min(hi, s1) - s0) for s0, s1, c0 in segments if max(lo, s0) < min(hi, s1)]
        return jnp.concatenate([g[:, a:b] for a, b in cuts], axis=1)

    return jnp.stack([shard(s * n, (s + 1) * n) for s in range(4)])


def _flat_pack(arrs, mult):
    flat = jnp.concatenate([a.reshape(-1) for a in arrs])
    n = flat.shape[0]
    pad = (-n) % mult
    if pad:
        flat = jnp.concatenate([flat, jnp.zeros((pad,), flat.dtype)])
    return flat.reshape(-1, LANES)


def _flat_unpack(flat, shapes):
    flat = flat.reshape(-1)
    out, off = [], 0
    for s in shapes:
        n = math.prod(s)
        out.append(flat[off:off + n].reshape(s))
        off += n
    return out


BIG = ("w_in", "w_branch_sb", "w_branch_dn", "w_out")
SMALL = ("ada_b", "norm_g", "sb_q_g", "sb_k_g", "conv_w", "dn_a_log", "dn_dt_bias", "dn_norm_g")


def kernel(x, c, ada_w, ada_b, norm_g, w_in, sb_q_g, sb_k_g, conv_w, dn_a_log, dn_dt_bias, dn_norm_g, w_branch_sb, w_branch_dn, w_out, loss_target, m_ada_w, m_ada_b, m_norm_g, m_w_in, m_sb_q_g, m_sb_k_g, m_conv_w, m_dn_a_log, m_dn_dt_bias, m_dn_norm_g, m_w_branch_sb, m_w_branch_dn, m_w_out, v_ada_w, v_ada_b, v_norm_g, v_w_in, v_sb_q_g, v_sb_k_g, v_conv_w, v_dn_a_log, v_dn_dt_bias, v_dn_norm_g, v_w_branch_sb, v_w_branch_dn, v_w_out):
    W = dict(ada_w=ada_w, ada_b=ada_b, norm_g=norm_g, w_in=w_in, sb_q_g=sb_q_g, sb_k_g=sb_k_g, conv_w=conv_w,
             dn_a_log=dn_a_log, dn_dt_bias=dn_dt_bias, dn_norm_g=dn_norm_g, w_branch_sb=w_branch_sb,
             w_branch_dn=w_branch_dn, w_out=w_out)
    M = dict(ada_w=m_ada_w, ada_b=m_ada_b, norm_g=m_norm_g, w_in=m_w_in, sb_q_g=m_sb_q_g, sb_k_g=m_sb_k_g,
             conv_w=m_conv_w, dn_a_log=m_dn_a_log, dn_dt_bias=m_dn_dt_bias, dn_norm_g=m_dn_norm_g,
             w_branch_sb=m_w_branch_sb, w_branch_dn=m_w_branch_dn, w_out=m_w_out)
    V = dict(ada_w=v_ada_w, ada_b=v_ada_b, norm_g=v_norm_g, w_in=v_w_in, sb_q_g=v_sb_q_g, sb_k_g=v_sb_k_g,
             conv_w=v_conv_w, dn_a_log=v_dn_a_log, dn_dt_bias=v_dn_dt_bias, dn_norm_g=v_dn_norm_g,
             w_branch_sb=v_w_branch_sb, w_branch_dn=v_w_branch_dn, w_out=v_w_out)
    L = ada_w.shape[0]
    S, D = x.shape[1], x.shape[2]
    ix, iy, ic = lax.axis_index("x"), lax.axis_index("y"), lax.axis_index("c")
    shard = 2 * ix + iy
    me = 2 * shard + ic
    n_ada = ada_w.shape[2]
    n_in = w_in.shape[2]
    n_conv = conv_w.shape[2]
    n_br = w_branch_sb.shape[2]
    n_out = w_out.shape[1]

    assert L == 2, "the owner of a layer's gradients is the core with the layer's number"
    shards = [W[n].astype(_MXU_DTYPE) for n in BIG]
    gathered0 = _ag_weights_first([shards[0][0]])

    g1 = _ag_small("ag_c_conv", _flat_pack([c, conv_w], LANES * 8))
    g1 = g1.reshape(8, -1)
    c_all = g1[:, :D]
    conv_parts = g1[:, D:D + L * CONV_K * n_conv].reshape(4, 2, L, CONV_K, n_conv)[:, 0]
    conv_full = jnp.concatenate([conv_parts[s] for s in range(4)], axis=2)
    ada_b_sh = lax.dynamic_slice_in_dim(ada_b, shard * n_ada, n_ada, axis=1)[:, None, :]
    mod_sh = _ada_fwd(c_all, ada_w, ada_b_sh)
    g2 = _ag_small("ag_mod", _flat_pack([mod_sh], LANES * 8)).reshape(8, -1)
    mod_parts = g2[:, :L * 8 * n_ada].reshape(4, 2, L, 8, n_ada)[:, 0]
    mod_all = jnp.concatenate([mod_parts[s] for s in range(4)], axis=2)
    mod = lax.dynamic_index_in_dim(mod_all, me, axis=1, keepdims=False)

    def layer_weights(l, g_in, late=None):
        pad_lo = jnp.zeros((DN_HEADS,), F32)
        pad_hi = jnp.zeros((LANES - 2 * DN_HEADS,), F32)
        lw = dict(
            norm_g=norm_g[l][None, :], w_cat=_cat_cols(jnp.concatenate([g_in[s] for s in range(4)], axis=1), D),
            gq_t=jnp.tile(sb_q_g[l], SB_HEADS)[None, :], gk_t=jnp.tile(sb_k_g[l], SB_HEADS)[None, :],
            conv_w=conv_full[l],
            a_row=jnp.concatenate([pad_lo, dn_a_log[l], pad_hi])[None, :],
            dtb_row=jnp.concatenate([pad_lo, dn_dt_bias[l], pad_hi])[None, :], gn=dn_norm_g[l][None, :])
        return lw if late is None else {**lw, **_late_weights(late)}

    mods = lambda l: (mod[l, None, 0:D], mod[l, None, D:2 * D], mod[l, None, 2 * D:3 * D])
    lws, ress = [None] * L, [None] * L
    xs, ress[0], lws[0], gathered1 = _layer_fwd(x[0], *mods(0), layer_weights(0, gathered0[0]),
                                                next_shards=[a[1] for a in shards],
                                                late_shards=[a[0] for a in shards[1:]])
    xs, ress[1], lws[1], _ = _layer_fwd(xs, *mods(1), layer_weights(1, gathered1[0], gathered1[1:]))
    dxs, loss_row = _loss_head(xs, loss_target[0])
    loss = lax.psum(loss_row[0, 0], ("x", "y", "c"))
    smalls = [None] * L
    dxs, smalls[1], send1, _ = _layer_bwd(dxs, ress[1], lws[1])
    dxs, smalls[0], send0, got = _layer_bwd(dxs, ress[0], lws[0], (1, send1))
    grad_x = dxs[None]

    small_names = ("dmod",) + SMALL[1:]
    small_pack = _flat_pack([jnp.stack([smalls[l][n] for l in range(L)]) for n in small_names], LANES * 8)
    g3 = _ag_small("ag_small_grads", small_pack)
    R3 = small_pack.shape[0]
    g3 = g3.reshape(8, R3, LANES)
    small_sum = _sum_parts("sum_small", g3)
    small_shapes = [(L, 3 * D), (L, D), (L, SB_HD), (L, SB_HD), (L, CONV_K, 3 * DN_W), (L, DN_HEADS), (L, DN_HEADS),
                    (L, DN_HD)]
    sg = dict(zip(small_names, _flat_unpack(small_sum, small_shapes)))
    G = dict(ada_b=sg["dmod"], norm_g=sg["norm_g"], sb_q_g=sg["sb_q_g"], sb_k_g=sg["sb_k_g"],
             conv_w=lax.dynamic_slice_in_dim(sg["conv_w"], shard * n_conv, n_conv, axis=2),
             dn_a_log=sg["dn_a_log"], dn_dt_bias=sg["dn_dt_bias"], dn_norm_g=sg["dn_norm_g"])
    dmod_all = g3.reshape(8, -1)[:, :L * 3 * D].reshape(8, L, 3 * D)
    dmod_sh = lax.dynamic_slice_in_dim(dmod_all, shard * n_ada, n_ada, axis=2).transpose(1, 0, 2)
    G["ada_w"] = _ada_bwd(c_all.T, dmod_sh)

    assert not send0
    mine = [_sum_parts("sum_" + n, g) for n, g in zip(BIG, got)]
    for n, g in zip(BIG, _sibling_join(mine)):
        G[n] = g

    delta, new_m, new_v = {}, {}, {}
    for n in ("ada_w",) + BIG:
        delta[n], new_m[n], new_v[n] = _adamw("adamw_" + n, W[n], G[n], M[n], V[n])
    sm_shapes = [W[n].shape for n in SMALL]
    d, mo, vo = _adamw("adamw_small", *[_flat_pack([T[n] for n in SMALL], LANES * 8)[None] for T in (W, G, M, V)])
    for n, dd, mm, vv in zip(SMALL, _flat_unpack(d, sm_shapes), _flat_unpack(mo, sm_shapes),
                             _flat_unpack(vo, sm_shapes)):
        delta[n], new_m[n], new_v[n] = dd, mm, vv

    order = ("ada_w", "ada_b", "norm_g", "w_in", "sb_q_g", "sb_k_g", "conv_w", "dn_a_log", "dn_dt_bias", "dn_norm_g",
             "w_branch_sb", "w_branch_dn", "w_out")
    return (loss, grad_x, *[G[n] for n in order], *[delta[n] for n in order], *[new_m[n] for n in order],
            *[new_v[n] for n in order])
```

```python
import math

import jax
import jax.numpy as jnp
from jax import lax
from jax.experimental import pallas as pl
from jax.experimental.pallas import tpu as pltpu

F32 = jnp.float32
BF16 = jnp.bfloat16
_MXU_DTYPE = BF16
_VMEM_LIMIT = 48 * 1024 * 1024
LANES = 128

EPS = 1e-6
SB_HEADS, SB_HD, SB_W = 8, 64, 512
DN_HEADS, DN_HD, DN_W = 4, 128, 512
CONV_K = 4
CHUNK = 64
QB = 256
_SB_DEAD = 104.0
ADAM_LR, ADAM_B1, ADAM_B2, ADAM_EPS, ADAM_WD, ADAM_STEP = 0.001, 0.9, 0.999, 1e-08, 0.01, 10

C_DN_QKV, C_DN_Z, C_SB_Q, C_SB_K, C_SB_V, C_SB_Z, C_MG = 0, 1536, 2048, 2560, 3072, 3584, 4096

_NN = (((1,), (0,)), ((), ()))
_NT = (((1,), (1,)), ((), ()))
_TN = (((0,), (0,)), ((), ()))
_BNN = (((2,), (1,)), ((0,), (0,)))
_BNT = (((2,), (2,)), ((0,), (0,)))
_BTN = (((1,), (1,)), ((0,), (0,)))
MESH = pl.DeviceIdType.MESH


def _sds(shape, dtype):
    return jax.ShapeDtypeStruct(shape, dtype)


def _cp(n):
    return pltpu.CompilerParams(dimension_semantics=("arbitrary",) * n, vmem_limit_bytes=_VMEM_LIMIT)


def _rb(tm, w, cb=0):
    return pl.BlockSpec((tm, w), lambda i: (i, cb))


def _fs(shape):
    nd = len(shape)
    return pl.BlockSpec(shape, lambda i: (0,) * nd)


def _dg(a, b, dims):
    return lax.dot_general(a, b, dims, preferred_element_type=F32)


def _mm(a, b, dims=_NN):
    return _dg(a.astype(_MXU_DTYPE), b.astype(_MXU_DTYPE), dims)


def _split3(x):
    hi = x.astype(BF16)
    r = x - hi.astype(F32)
    mid = r.astype(BF16)
    lo = (r - mid.astype(F32)).astype(BF16)
    return hi, mid, lo


def _mm_xl(x, const, dims=_NN):
    cb = const.astype(BF16)
    hi, mid, lo = _split3(x)
    return _dg(hi, cb, dims) + _dg(mid, cb, dims) + _dg(lo, cb, dims)


def _mm_xl2(x, const, dims=_NN):
    cb = const.astype(BF16)
    hi = x.astype(BF16)
    lo = (x - hi.astype(F32)).astype(BF16)
    return _dg(hi, cb, dims) + _dg(lo, cb, dims)


def _mm_xr(const, x, dims=_NN):
    cb = const.astype(BF16)
    hi, mid, lo = _split3(x)
    return _dg(cb, hi, dims) + _dg(cb, mid, dims) + _dg(cb, lo, dims)


def _mm3(a, b, dims=_NN):
    ah, am, _ = _split3(a)
    bh, bm, _ = _split3(b)
    return _dg(ah, bh, dims) + (_dg(ah, bm, dims) + _dg(am, bh, dims))


def _sigmoid(z):
    return 1.0 / (1.0 + jnp.exp(-z))


def _silu(z):
    return z * _sigmoid(z)


def _dsilu(z):
    s = _sigmoid(z)
    return s * (1.0 + z * (1.0 - s))


def _softplus(z):
    return jnp.maximum(z, 0.0) + jnp.log(1.0 + jnp.exp(-jnp.abs(z)))


def _iota2(shape, dim):
    return lax.broadcasted_iota(jnp.int32, shape, dim)


def _pick(n, cap, mult):
    best = None
    for t in range(mult, min(n, cap) + 1, mult):
        if n % t == 0:
            best = t
    assert best is not None, (n, cap, mult)
    return best


def _matmul(name, a, b, form, out_dtype, tm_cap=512, tn_cap=1024, tk_cap=1024, exchange=None, ex_src=(), ex_prev=()):
    if form == "nn":
        (M, K), (_, N) = a.shape, b.shape
    elif form == "nt":
        (M, K), (N, _) = a.shape, b.shape
    else:
        (K, M), (_, N) = a.shape, b.shape
    tm = _pick(M, tm_cap, 128 if form == "tn" else 8)
    tn = _pick(N, tn_cap, 128)
    tk = _pick(K, tk_cap, 128)
    nk = K // tk
    dims = {"nn": _NN, "nt": _NT, "tn": _TN}[form]
    if form == "nn":
        a_spec = pl.BlockSpec((tm, tk), lambda i, j, k: (i, k))
        b_spec = pl.BlockSpec((tk, tn), lambda i, j, k: (k, j))
    elif form == "nt":
        a_spec = pl.BlockSpec((tm, tk), lambda i, j, k: (i, k))
        b_spec = pl.BlockSpec((tn, tk), lambda i, j, k: (j, k))
    else:
        a_spec = pl.BlockSpec((tk, tm), lambda i, j, k: (k, i))
        b_spec = pl.BlockSpec((tk, tn), lambda i, j, k: (k, j))

    grid = (M // tm, N // tn, nk)
    nx, npv = len(ex_src), len(ex_prev)
    o0 = 2 + nx + npv

    def body(*refs):
        if exchange is None:
            compute(*refs)
            return
        src, xout, sems = refs[2:2 + nx], refs[o0 + 1:o0 + 1 + nx], refs[o0 + 2 + nx:]
        at = [pl.program_id(d) for d in range(3)]

        @pl.when(jnp.logical_and(jnp.logical_and(at[0] == 0, at[1] == 0), at[2] == 0))
        def _():
            exchange.start(src, xout, sems)

        compute(refs[0], refs[1], refs[o0], refs[o0 + 1 + nx])

        @pl.when(jnp.logical_and(jnp.logical_and(at[0] == grid[0] - 1, at[1] == grid[1] - 1), at[2] == nk - 1))
        def _():
            exchange.wait(src, xout, sems)

    def compute(a_ref, b_ref, o_ref, acc_ref):
        if nk == 1:
            o_ref[...] = _mm(a_ref[...], b_ref[...], dims).astype(o_ref.dtype)
            return
        k = pl.program_id(2)

        @pl.when(k == 0)
        def _():
            acc_ref[...] = _mm(a_ref[...], b_ref[...], dims)

        @pl.when(k > 0)
        def _():
            acc_ref[...] += _mm(a_ref[...], b_ref[...], dims)

        @pl.when(k == nk - 1)
        def _():
            o_ref[...] = acc_ref[...].astype(o_ref.dtype)

    hbm = pl.BlockSpec(memory_space=pl.ANY)
    outs = pl.pallas_call(
        body, name=name, grid=grid,
        in_specs=[a_spec, b_spec] + [hbm] * (nx + npv),
        out_specs=[pl.BlockSpec((tm, tn), lambda i, j, k: (i, j))] + [hbm] * nx,
        out_shape=[_sds((M, N), out_dtype)] + (exchange.out_shape if exchange else []),
        input_output_aliases={2 + nx + t: 1 + t for t in range(npv)},
        scratch_shapes=[pltpu.VMEM((tm, tn), F32)] + (exchange.scratch if exchange else []),
        compiler_params=_cp(3),
    )(a, b, *ex_src, *ex_prev)
    return outs[0] if exchange is None else (outs[0], list(outs[1:]))


def _norm_mod(x, g, scale, shift, tm=256):
    S, D = x.shape

    def body(x_ref, g_ref, sc_ref, sh_ref, h_ref):
        xv = x_ref[...]
        r = lax.rsqrt(jnp.mean(xv * xv, axis=1, keepdims=True) + EPS)
        h_ref[...] = ((xv * r * g_ref[...]) * (1.0 + sc_ref[...]) + sh_ref[...]).astype(h_ref.dtype)

    return pl.pallas_call(
        body, name="norm_mod", grid=(S // tm,),
        in_specs=[_rb(tm, D), _fs((1, D)), _fs((1, D)), _fs((1, D))],
        out_specs=_rb(tm, D), out_shape=_sds((S, D), _MXU_DTYPE), compiler_params=_cp(1),
    )(x, g, scale, shift)


def _norm_mod_bwd(x, dh, dxn, g, scale, tm=256, exchange=None, ex_src=(), ex_prev=()):
    S, D = x.shape
    nx = len(ex_src)

    def body(*refs):
        if exchange is None:
            compute(*refs)
            return
        src, xout, sems = refs[5:5 + nx], refs[9 + 2 * nx:9 + 3 * nx], refs[9 + 3 * nx:]

        @pl.when(pl.program_id(0) == 0)
        def _():
            exchange.start(src, xout, sems)

        compute(*refs[:5], *refs[5 + 2 * nx:9 + 2 * nx])

        @pl.when(pl.program_id(0) == S // tm - 1)
        def _():
            exchange.wait(src, xout, sems)

    def compute(x_ref, dh_ref, dxn_ref, g_ref, sc_ref, dx_ref, dsh_ref, dsc_ref, dg_ref):
        @pl.when(pl.program_id(0) == 0)
        def _():
            dsh_ref[...] = jnp.zeros_like(dsh_ref)
            dsc_ref[...] = jnp.zeros_like(dsc_ref)
            dg_ref[...] = jnp.zeros_like(dg_ref)

        xv, dhv, gv = x_ref[...], dh_ref[...], g_ref[...]
        r = lax.rsqrt(jnp.mean(xv * xv, axis=1, keepdims=True) + EPS)
        xh = xv * r
        one_sc = 1.0 + sc_ref[...]
        dsh_ref[...] += jnp.sum(dhv, axis=0, keepdims=True)
        dsc_ref[...] += jnp.sum(dhv * xh * gv, axis=0, keepdims=True)
        dg_ref[...] += jnp.sum(dhv * one_sc * xh, axis=0, keepdims=True)
        dxh = dhv * (gv * one_sc)
        dx_ref[...] = r * (dxh - xh * jnp.mean(dxh * xh, axis=1, keepdims=True)) + dxn_ref[...]

    hbm = pl.BlockSpec(memory_space=pl.ANY)
    outs = pl.pallas_call(
        body, name="norm_mod_bwd", grid=(S // tm,),
        in_specs=[_rb(tm, D), _rb(tm, D), _rb(tm, D), _fs((1, D)), _fs((1, D))] + [hbm] * (2 * nx),
        out_specs=[_rb(tm, D), _fs((1, D)), _fs((1, D)), _fs((1, D))] + [hbm] * nx,
        out_shape=[_sds((S, D), F32)] + [_sds((1, D), F32)] * 3 + (exchange.out_shape if exchange else []),
        input_output_aliases={5 + nx + t: 4 + t for t in range(nx)},
        scratch_shapes=exchange.scratch if exchange else [], compiler_params=_cp(1),
    )(x, dh, dxn, g, scale, *ex_src, *ex_prev)
    return outs[:4], list(outs[4:])


def _head_sum_matrix():
    r = jnp.arange(SB_W)
    return (r[:, None] // SB_HD == r[None, :] // SB_HD).astype(BF16)


def _sb_prep(p, gq_t, gk_t, tm=256):
    S = p.shape[0]
    bd = _head_sum_matrix()

    def body(q_ref, k_ref, gq_ref, gk_ref, bd_ref, qn_ref, kn_ref):
        for src, g_ref, dst in ((q_ref, gq_ref, qn_ref), (k_ref, gk_ref, kn_ref)):
            v = src[...]
            ms = _mm_xl(v * v, bd_ref[...]) * (1.0 / SB_HD)
            dst[...] = (v * lax.rsqrt(ms + EPS) * g_ref[...]).astype(dst.dtype)

    return pl.pallas_call(
        body, name="sb_prep", grid=(S // tm,),
        in_specs=[_rb(tm, SB_W, C_SB_Q // SB_W), _rb(tm, SB_W, C_SB_K // SB_W),
                  _fs((1, SB_W)), _fs((1, SB_W)), _fs((SB_W, SB_W))],
        out_specs=[_rb(tm, SB_W), _rb(tm, SB_W)],
        out_shape=[_sds((S, SB_W), _MXU_DTYPE)] * 2, compiler_params=_cp(1),
    )(p, p, gq_t, gk_t, bd)


def _sb_prep_bwd(p, dqn, dkn, gq_t, gk_t, tm=256):
    S = p.shape[0]
    bd = _head_sum_matrix()

    def body(q_ref, k_ref, dqn_ref, dkn_ref, gq_ref, gk_ref, bd_ref, dq_ref, dk_ref, dgq_ref, dgk_ref):
        @pl.when(pl.program_id(0) == 0)
        def _():
            dgq_ref[...] = jnp.zeros_like(dgq_ref)
            dgk_ref[...] = jnp.zeros_like(dgk_ref)

        for src, dn_ref, g_ref, dst, dg_ref in ((q_ref, dqn_ref, gq_ref, dq_ref, dgq_ref),
                                                (k_ref, dkn_ref, gk_ref, dk_ref, dgk_ref)):
            v, dn = src[...], dn_ref[...]
            r = lax.rsqrt(_mm_xl(v * v, bd_ref[...]) * (1.0 / SB_HD) + EPS)
            vh = v * r
            dg_ref[...] += jnp.sum(dn * vh, axis=0, keepdims=True)
            dvh = dn * g_ref[...]
            m = _mm_xl(dvh * vh, bd_ref[...]) * (1.0 / SB_HD)
            dst[...] = (r * (dvh - vh * m)).astype(dst.dtype)

    return pl.pallas_call(
        body, name="sb_prep_bwd", grid=(S // tm,),
        in_specs=[_rb(tm, SB_W, C_SB_Q // SB_W), _rb(tm, SB_W, C_SB_K // SB_W), _rb(tm, SB_W), _rb(tm, SB_W),
                  _fs((1, SB_W)), _fs((1, SB_W)), _fs((SB_W, SB_W))],
        out_specs=[_rb(tm, SB_W), _rb(tm, SB_W), _fs((1, SB_W)), _fs((1, SB_W))],
        out_shape=[_sds((S, SB_W), _MXU_DTYPE)] * 2 + [_sds((1, SB_W), F32)] * 2, compiler_params=_cp(1),
    )(p, p, dqn, dkn, gq_t, gk_t, bd)


def _sb_consts():
    r, c = _iota2((QB, QB), 0), _iota2((QB, QB), 1)
    lane = _iota2((1, LANES), 1)
    return r, c, lane


def _sb_fwd(qn, kn, p, gather=None, g_src=()):
    S = qn.shape[0]
    scale = 1.0 / math.sqrt(SB_HD)
    grid = (SB_W // LANES, S // QB)
    nx = len(g_src)

    def body(*refs):
        if gather is None:
            compute(*refs)
            return
        src, gout, sems = refs[3:3 + nx], refs[6 + nx:6 + 2 * nx], refs[6 + 2 * nx:]
        hp, i = pl.program_id(0), pl.program_id(1)

        @pl.when(jnp.logical_and(hp == 0, i == 0))
        def _():
            gather.start(src, gout, sems)

        compute(*refs[:3], *refs[3 + nx:6 + nx])

        @pl.when(jnp.logical_and(hp == grid[0] - 1, i == grid[1] - 1))
        def _():
            gather.wait(src, gout, sems)

    def compute(q_ref, k_ref, v_ref, o_ref, tot_ref, nb_ref):
        i = pl.program_id(1)
        r, c, lane = _sb_consts()
        u_gt = (r > c).astype(BF16)
        strict = jnp.concatenate([c < r, c < r], axis=0)
        q = q_ref[...]
        mask0 = (lane // SB_HD) == 0
        zero = jnp.zeros_like(q)
        qh = jnp.concatenate([jnp.where(mask0, q, zero), jnp.where(mask0, zero, q)], axis=0)

        def block(off, carry, diagonal):
            o, run = carry
            kj = k_ref[pl.ds(off, QB), :]
            vj = v_ref[pl.ds(off, QB), :].astype(_MXU_DTYPE)
            z = _mm(qh, kj, _NT) * scale
            sp = _softplus(z)
            sp_m = jnp.where(strict, sp, 0.0) if diagonal else sp
            later = _mm_xl2(sp_m, u_gt)
            w = jnp.exp((z - sp) - later - run)
            if diagonal:
                w = jnp.where(strict, w, 0.0)
            return o + _mm(w, vj), run + jnp.sum(sp_m, axis=1, keepdims=True)

        init = (jnp.zeros((2 * QB, LANES), F32), jnp.zeros((2 * QB, 1), F32))
        carry = block(pl.multiple_of(i * QB, QB), init, True)
        st = lax.while_loop(
            lambda st: jnp.logical_and(st[0] <= i, jnp.min(st[2]) < _SB_DEAD),
            lambda st: (st[0] + 1,) + block(pl.multiple_of((i - st[0]) * QB, QB), st[1:], False),
            (jnp.int32(1),) + carry)
        o_ref[...] = jnp.where(mask0, st[1][:QB], st[1][QB:])
        tot_ref[...] = jnp.where(mask0, st[2][:QB], st[2][QB:])
        nb_ref[...] = jnp.zeros((8, LANES), F32) + st[0].astype(F32)

    blk = pl.BlockSpec((QB, LANES), lambda hp, i: (i, hp))
    hbm = pl.BlockSpec(memory_space=pl.ANY)
    outs = pl.pallas_call(
        body, name="sb_fwd", grid=grid,
        in_specs=[blk, pl.BlockSpec((S, LANES), lambda hp, i: (0, hp)),
                  pl.BlockSpec((S, LANES), lambda hp, i: (0, C_SB_V // LANES + hp))] + [hbm] * nx,
        out_specs=[blk, blk, pl.BlockSpec((8, LANES), lambda hp, i: (i, hp))] + [hbm] * nx,
        out_shape=[_sds((S, SB_W), F32), _sds((S, SB_W), F32), _sds((8 * S // QB, SB_W), F32)]
        + (gather.out_shape if gather else []),
        scratch_shapes=gather.scratch if gather else [], compiler_params=_cp(2),
    )(qn, kn, p, *g_src)
    return outs[:3], list(outs[3:])


def _sb_bwd(qn, kn, p, do, tot, nblocks, exchange=None, ex_src=(), early=None, early_src=()):
    S = qn.shape[0]
    scale = 1.0 / math.sqrt(SB_HD)
    grid = (SB_W // LANES, S // QB)
    nx, ne = len(ex_src), len(early_src)
    NI = 6

    def body(*refs):
        if exchange is None:
            compute(*refs)
            return
        src, src2 = refs[NI:NI + nx], refs[NI + nx:NI + nx + ne]
        o0 = NI + nx + ne
        xout, sems, sems2 = refs[o0 + 3:o0 + 3 + nx], refs[o0 + 3 + nx:o0 + 6 + nx], refs[o0 + 6 + nx:]
        hp, i = pl.program_id(0), pl.program_id(1)

        @pl.when(jnp.logical_and(hp == 0, i == 0))
        def _():
            exchange.start(src, xout, sems)
            if early is not None:
                early.start(src2, xout[nx - ne:], sems2)

        compute(*refs[:NI], *refs[o0:o0 + 3])

        @pl.when(jnp.logical_and(hp == grid[0] - 1, i == grid[1] - 1))
        def _():
            exchange.wait(src, xout, sems)
            if early is not None:
                early.wait(src2, xout[nx - ne:], sems2)

    def compute(q_ref, k_ref, v_ref, do_ref, tot_ref, nb_ref, dq_ref, dk_ref, dv_ref):
        i = pl.program_id(1)

        @pl.when(i == 0)
        def _():
            dk_ref[...] = jnp.zeros_like(dk_ref)
            dv_ref[...] = jnp.zeros_like(dv_ref)

        r, c, lane = _sb_consts()
        u_le = (r <= c).astype(BF16)
        u_lt = (r < c).astype(BF16)
        strict = jnp.concatenate([c < r, c < r], axis=0)
        q = q_ref[...]
        do = do_ref[...].astype(_MXU_DTYPE)
        mask0 = (lane // SB_HD) == 0
        zero, zero_do = jnp.zeros_like(q), jnp.zeros_like(do)
        qh = jnp.concatenate([jnp.where(mask0, q, zero), jnp.where(mask0, zero, q)], axis=0)
        doh = jnp.concatenate([jnp.where(mask0, do, zero_do), jnp.where(mask0, zero_do, do)], axis=0)

        tot_pair = tot_ref[...]
        tot = jnp.concatenate([jnp.max(jnp.where(mask0, tot_pair, 0.0), axis=1, keepdims=True),
                               jnp.max(jnp.where(mask0, 0.0, tot_pair), axis=1, keepdims=True)], axis=0)
        nb = jnp.clip(jnp.max(nb_ref[...]).astype(jnp.int32), 1, i + 1)
        first = i + 1 - nb

        def block(off, carry, diagonal):
            dq, pre_sp, pre_e = carry
            kj = k_ref[pl.ds(off, QB), :]
            vj = v_ref[pl.ds(off, QB), :].astype(_MXU_DTYPE)
            z = _mm(qh, kj, _NT) * scale
            sp = _softplus(z)
            a = z - sp
            sp_m = jnp.where(strict, sp, 0.0) if diagonal else sp
            incl = _mm_xl2(sp_m, u_le)
            w = jnp.exp(a - ((tot - pre_sp) - incl))
            if diagonal:
                w = jnp.where(strict, w, 0.0)
            e = w * _mm(doh, vj, _NT)
            db = pre_e + _mm_xl2(e, u_lt)
            dz = (e - jnp.exp(a) * (e + db)) * scale
            if diagonal:
                dz = jnp.where(strict, dz, 0.0)
            dk_ref[pl.ds(off, QB), :] += _mm(dz, qh, _TN)
            dv_ref[pl.ds(off, QB), :] += _mm(w, doh, _TN)
            return (dq + _mm(dz, kj), pre_sp + jnp.sum(sp_m, axis=1, keepdims=True),
                    pre_e + jnp.sum(e, axis=1, keepdims=True))

        zero_col = jnp.zeros((2 * QB, 1), F32)
        init = (jnp.zeros((2 * QB, LANES), F32), zero_col, zero_col)
        carry = lax.fori_loop(first, i, lambda j, cr: block(pl.multiple_of(j * QB, QB), cr, False), init)
        carry = block(pl.multiple_of(i * QB, QB), carry, True)
        dq_ref[...] = jnp.where(mask0, carry[0][:QB], carry[0][QB:])

    blk = pl.BlockSpec((QB, LANES), lambda hp, i: (i, hp))
    full = pl.BlockSpec((S, LANES), lambda hp, i: (0, hp))
    hbm = pl.BlockSpec(memory_space=pl.ANY)
    outs = pl.pallas_call(
        body, name="sb_bwd", grid=grid,
        in_specs=[blk, full, pl.BlockSpec((S, LANES), lambda hp, i: (0, C_SB_V // LANES + hp)), blk, blk,
                  pl.BlockSpec((8, LANES), lambda hp, i: (i, hp))] + [hbm] * (nx + ne),
        out_specs=[blk, full, full] + [hbm] * nx,
        out_shape=[_sds((S, SB_W), F32)] * 3 + (exchange.out_shape if exchange else []),
        scratch_shapes=(exchange.scratch if exchange else []) + (early.scratch if early else []),
        compiler_params=_cp(2),
    )(qn, kn, p, do, tot, nblocks, *ex_src, *early_src)
    return outs[:3], outs[3:]


def _dn_prep(p, conv_w, a_row, dtb_row, tm=256):
    S = p.shape[0]
    W3 = 3 * DN_W
    nhalo = tm // 8

    def body(x_ref, halo_ref, w_ref, ba_ref, a_ref, dtb_ref, qkv_ref, bb_ref, gc_ref, gl_ref):
        i = pl.program_id(0)
        halo = jnp.where(i > 0, halo_ref[...], 0.0)
        xf = jnp.concatenate([halo, x_ref[...]], axis=0)
        acc = jnp.zeros((tm, W3), F32)
        for k in range(CONV_K):
            sh = CONV_K - 1 - k
            xs = xf if sh == 0 else pltpu.roll(xf, sh, 0)
            acc = acc + xs[8:, :] * w_ref[k:k + 1, :]
        s = _silu(acc)
        for gi in range(2 * DN_HEADS):
            sl = slice(gi * LANES, (gi + 1) * LANES)
            sg = s[:, sl]
            rinv = lax.rsqrt(jnp.sum(sg * sg, axis=1, keepdims=True) + EPS)
            qkv_ref[:, sl] = sg * rinv * (DN_HD ** -0.5 if gi < DN_HEADS else 1.0)
        qkv_ref[:, 2 * DN_W:] = s[:, 2 * DN_W:]

        ba = ba_ref[...]
        beta = _sigmoid(ba)
        g = -jnp.exp(a_ref[...]) * _softplus(ba + dtb_ref[...])
        lr, lc = _iota2((LANES, DN_W), 0), _iota2((LANES, DN_W), 1)
        sel_b = (lr == lc // LANES).astype(BF16)
        sel_g = (lr == lc // LANES + DN_HEADS).astype(BF16)
        bb_ref[...] = _mm_xl(beta, sel_b)
        graw = _mm_xl(g, sel_g)
        rr, cc = _iota2((tm, tm), 0), _iota2((tm, tm), 1)
        tri = jnp.logical_and(rr >= cc, rr // CHUNK == cc // CHUNK).astype(BF16)
        gc = _mm_xr(tri, graw)
        last = (cc == (rr // CHUNK) * CHUNK + (CHUNK - 1)).astype(BF16)
        gc_ref[...] = gc
        gl_ref[...] = _mm_xr(last, gc)

    return pl.pallas_call(
        body, name="dn_prep", grid=(S // tm,),
        in_specs=[_rb(tm, W3, 0), pl.BlockSpec((8, W3), lambda i: (jnp.maximum(i * nhalo - 1, 0), 0)),
                  _fs((CONV_K, W3)), _rb(tm, LANES, (p.shape[1] - LANES) // LANES),
                  _fs((1, LANES)), _fs((1, LANES))],
        out_specs=[_rb(tm, W3), _rb(tm, DN_W), _rb(tm, DN_W), _rb(tm, DN_W)],
        out_shape=[_sds((S, W3), F32)] + [_sds((S, DN_W), F32)] * 3, compiler_params=_cp(1),
    )(p, p, conv_w, p, a_row, dtb_row)


def _heads(ref, base=0):
    return jnp.stack([ref[:, base + h * LANES:base + (h + 1) * LANES] for h in range(DN_HEADS)])


def _per_head(const):
    return jnp.broadcast_to(const[None], (DN_HEADS,) + const.shape)


def _dn_chunk_terms(q, k, v, beta, gc, gl):
    r, c = _iota2((CHUNK, CHUNK), 0), _iota2((CHUNK, CHUNK), 1)
    tril, strict = r >= c, r > c
    gcol = _mm_xl(gc, _per_head(jnp.full((LANES, CHUNK), 1.0 / LANES, F32)), _BNN)
    grow = _mm_xr(_per_head(jnp.full((CHUNK, LANES), 1.0 / LANES, F32)), gc, _BNT)
    dec = jnp.where(tril, jnp.exp(jnp.where(tril, gcol - grow, 0.0)), 0.0)
    gam = jnp.exp(gc)
    dlt = jnp.exp(gl - gc)
    kb, vb = k * beta, v * beta
    pm = _mm(kb, k, _BNT)
    qk = _mm(q, k, _BNT)
    m = jnp.where(strict, pm * dec, 0.0)
    a = jnp.where(tril, qk * dec, 0.0)
    return dict(tril=tril, strict=strict, dec=dec, gam=gam, dlt=dlt, kb=kb, vb=vb, m=m, a=a)


def _dn_fwd(qkv, bb, gcb, glb, gather=None, g_src=()):
    S = qkv.shape[0]
    N = S // CHUNK
    nx = len(g_src)

    def body(*refs):
        if gather is None:
            compute(*refs)
            return
        src, gout, sems = refs[4:4 + nx], refs[7 + nx:7 + 2 * nx], refs[8 + 2 * nx:]

        @pl.when(pl.program_id(0) == 0)
        def _():
            gather.start(src, gout, sems)

        compute(*refs[:4], *refs[4 + nx:7 + nx], refs[7 + 2 * nx])

        @pl.when(pl.program_id(0) == N - 1)
        def _():
            gather.wait(src, gout, sems)

    def compute(qkv_ref, bb_ref, gc_ref, gl_ref, o_ref, t_ref, sall_ref, s_scr):
        @pl.when(pl.program_id(0) == 0)
        def _():
            s_scr[...] = jnp.zeros_like(s_scr)

        r, c = _iota2((CHUNK, CHUNK), 0), _iota2((CHUNK, CHUNK), 1)
        eye = (r == c).astype(F32)
        q, k, v = _heads(qkv_ref), _heads(qkv_ref, DN_W), _heads(qkv_ref, 2 * DN_W)
        beta, gc, gl = _heads(bb_ref), _heads(gc_ref), _heads(gl_ref)
        s_prev = s_scr[...]
        sall_ref[0] = s_prev.astype(sall_ref.dtype)
        s0 = s_prev.astype(sall_ref.dtype).astype(F32)
        t = _dn_chunk_terms(q, k, v, beta, gc, gl)
        pw = -t["m"]
        tinv = eye + pw
        for _ in range(5):
            pw = _mm3(pw, pw, _BNN)
            tinv = tinv + _mm3(tinv, pw, _BNN)
        t_ref[...] = tinv
        u = _mm3(tinv, t["vb"], _BNN)
        w = _mm3(tinv, t["kb"] * t["gam"], _BNN)
        vn = u - _mm(w, s0, _BNN)
        o = _mm(q * t["gam"], s0, _BNN) + _mm(t["a"], vn, _BNN)
        for h in range(DN_HEADS):
            o_ref[:, h * LANES:(h + 1) * LANES] = o[h]
        egl = jnp.exp(jnp.concatenate([gl, gl], axis=1))
        s_scr[...] = s_prev * egl + _mm(k * t["dlt"], vn, _BTN)

    hbm = pl.BlockSpec(memory_space=pl.ANY)
    outs = pl.pallas_call(
        body, name="dn_fwd", grid=(N,),
        in_specs=[_rb(CHUNK, 3 * DN_W), _rb(CHUNK, DN_W), _rb(CHUNK, DN_W), _rb(CHUNK, DN_W)] + [hbm] * nx,
        out_specs=[_rb(CHUNK, DN_W), pl.BlockSpec((DN_HEADS, CHUNK, CHUNK), lambda n: (0, n, 0)),
                   pl.BlockSpec((1, DN_HEADS, DN_HD, DN_HD), lambda n: (n, 0, 0, 0))] + [hbm] * nx,
        out_shape=[_sds((S, DN_W), F32), _sds((DN_HEADS, S, CHUNK), F32),
                   _sds((N, DN_HEADS, DN_HD, DN_HD), _MXU_DTYPE)] + (gather.out_shape if gather else []),
        scratch_shapes=[pltpu.VMEM((DN_HEADS, DN_HD, DN_HD), F32)] + (gather.scratch if gather else []),
        compiler_params=_cp(1),
    )(qkv, bb, gcb, glb, *g_src)
    return outs[:3], outs[3:]


def _dn_bwd(qkv, bb, gcb, glb, tinv_all, sall, do, exchange=None, ex_src=(), early=None, early_src=()):
    S = qkv.shape[0]
    N = S // CHUNK
    nx = len(ex_src)
    NI = 7

    def body(*refs):
        if exchange is None:
            compute(*refs)
            return
        src, src2 = refs[NI:NI + nx], refs[NI + nx:NI + 2 * nx]
        o0 = NI + 2 * nx
        xout, ds_scr = refs[o0 + 3:o0 + 3 + nx], refs[o0 + 3 + nx]
        sems, sems2 = refs[o0 + 4 + nx:o0 + 7 + nx], refs[o0 + 7 + nx:]

        @pl.when(pl.program_id(0) == 0)
        def _():
            exchange.start(src, xout, sems)
            early.start(src2, xout, sems2)

        compute(*refs[:NI], *refs[o0:o0 + 3], ds_scr)

        @pl.when(pl.program_id(0) == N - 1)
        def _():
            exchange.wait(src, xout, sems)
            early.wait(src2, xout, sems2)

    def compute(qkv_ref, bb_ref, gc_ref, gl_ref, t_ref, sall_ref, do_ref, dqkv_ref, dbb_ref, dg_ref, ds_scr):
        @pl.when(pl.program_id(0) == 0)
        def _():
            ds_scr[...] = jnp.zeros_like(ds_scr)

        r, c = _iota2((CHUNK, CHUNK), 0), _iota2((CHUNK, CHUNK), 1)
        eye = (r == c).astype(F32)
        u_ge = (c >= r).astype(F32)
        last_row = _iota2((CHUNK, LANES), 0) == CHUNK - 1
        eye_h, u_ge_h = _per_head(eye), _per_head(u_ge)
        q, k, v = _heads(qkv_ref), _heads(qkv_ref, DN_W), _heads(qkv_ref, 2 * DN_W)
        beta, gc, gl = _heads(bb_ref), _heads(gc_ref), _heads(gl_ref)
        tinv = t_ref[...]
        s0 = sall_ref[0].astype(F32)
        do = _heads(do_ref)
        ds1 = ds_scr[...]
        t = _dn_chunk_terms(q, k, v, beta, gc, gl)
        gam, dlt, kb, vb, dec = t["gam"], t["dlt"], t["kb"], t["vb"], t["dec"]
        kbg = kb * gam
        u = _mm3(tinv, vb, _BNN)
        w = _mm3(tinv, kbg, _BNN)
        vn = u - _mm(w, s0, _BNN)
        qg, kd = q * gam, k * dlt
        egl = jnp.exp(gl)
        egl2 = jnp.concatenate([egl, egl], axis=1)

        dvn = _mm(t["a"], do, _BTN) + _mm(kd, ds1, _BNN)
        da = jnp.where(t["tril"], _mm(do, vn, _BNT), 0.0)
        dqg = _mm(do, s0, _BNT)
        dkd = _mm(vn, ds1, _BNT)
        dw = -_mm(dvn, s0, _BNT)
        ds_scr[...] = _mm(qg, do, _BTN) + egl2 * ds1 - _mm(w, dvn, _BTN)
        tt = _mm_xr(eye_h, tinv, _BNT)
        dvb = _mm3(tt, dvn, _BNN)
        dkbg = _mm3(tt, dw, _BNN)
        dm = -jnp.where(t["strict"], _mm(dvb, u, _BNT) + _mm(dkbg, w, _BNT), 0.0)
        dpm = dm * dec
        dqk = da * dec
        dkb = dkbg * gam + _mm(dpm, k, _BNN)
        dk = dkd * dlt + _mm(dpm, kb, _BTN) + _mm(dqk, q, _BTN) + dkb * beta
        dq = dqg * gam + _mm(dqk, k, _BNN)
        dv = dvb * beta
        dbeta = jnp.sum(dkb * k, axis=2, keepdims=True) + jnp.sum(dvb * v, axis=2, keepdims=True)
        dgam = jnp.sum(dqg * q, axis=2, keepdims=True) + jnp.sum(dkbg * kb, axis=2, keepdims=True)
        ddlt = jnp.sum(dkd * k, axis=2, keepdims=True)
        xm = dm * t["m"] + da * t["a"]
        xt = _mm_xr(eye_h, xm, _BNT)
        dgc = (dgam * gam - ddlt * dlt + jnp.sum(xm, axis=2, keepdims=True) - jnp.sum(xt, axis=2, keepdims=True))
        dgl = jnp.sum(ddlt * dlt, axis=1, keepdims=True) + jnp.sum(
            jnp.sum(ds1 * s0, axis=2, keepdims=True), axis=1, keepdims=True) * jnp.max(egl, axis=1, keepdims=True)
        dgc = dgc + jnp.where(last_row, dgl, 0.0)
        dg = _mm_xr(u_ge_h, dgc, _BNN)
        for h in range(DN_HEADS):
            sl = slice(h * LANES, (h + 1) * LANES)
            dqkv_ref[:, sl] = dq[h]
            dqkv_ref[:, DN_W + h * LANES:DN_W + (h + 1) * LANES] = dk[h]
            dqkv_ref[:, 2 * DN_W + h * LANES:2 * DN_W + (h + 1) * LANES] = dv[h]
            dbb_ref[:, sl] = jnp.broadcast_to(dbeta[h], (CHUNK, LANES))
            dg_ref[:, sl] = dg[h]

    rev = lambda w: pl.BlockSpec((CHUNK, w), lambda n: (N - 1 - n, 0))
    hbm = pl.BlockSpec(memory_space=pl.ANY)
    outs = pl.pallas_call(
        body, name="dn_bwd", grid=(N,),
        in_specs=[rev(3 * DN_W), rev(DN_W), rev(DN_W), rev(DN_W),
                  pl.BlockSpec((DN_HEADS, CHUNK, CHUNK), lambda n: (0, N - 1 - n, 0)),
                  pl.BlockSpec((1, DN_HEADS, DN_HD, DN_HD), lambda n: (N - 1 - n, 0, 0, 0)), rev(DN_W)]
        + [hbm] * (2 * nx),
        out_specs=[rev(3 * DN_W), rev(DN_W), rev(DN_W)] + [hbm] * nx,
        out_shape=[_sds((S, 3 * DN_W), F32), _sds((S, DN_W), F32), _sds((S, DN_W), F32)]
        + (exchange.out_shape if exchange else []),
        scratch_shapes=[pltpu.VMEM((DN_HEADS, DN_HD, DN_HD), F32)]
        + (exchange.scratch + early.scratch if exchange else []),
        compiler_params=_cp(1),
    )(qkv, bb, gcb, glb, tinv_all, sall, do, *ex_src, *early_src)
    return outs[:3], list(outs[3:])


def _dn_prep_bwd_a(p, dqkv, dbb, dgb, conv_w, a_row, dtb_row, tm=256):
    S, PC = p.shape
    W3 = 3 * DN_W
    nhalo = tm // 8

    def body(x_ref, halo_ref, w_ref, ba_ref, a_ref, dtb_ref, dqkv_ref, dbb_ref, dgb_ref,
             dc_ref, dba_ref, dal_ref, ddt_ref):
        i = pl.program_id(0)

        @pl.when(i == 0)
        def _():
            dal_ref[...] = jnp.zeros_like(dal_ref)
            ddt_ref[...] = jnp.zeros_like(ddt_ref)

        halo = jnp.where(i > 0, halo_ref[...], 0.0)
        xf = jnp.concatenate([halo, x_ref[...]], axis=0)
        acc = jnp.zeros((tm, W3), F32)
        for k in range(CONV_K):
            sh = CONV_K - 1 - k
            xs = xf if sh == 0 else pltpu.roll(xf, sh, 0)
            acc = acc + xs[8:, :] * w_ref[k:k + 1, :]
        s = _silu(acc)
        ds_act = _dsilu(acc)
        for gi in range(2 * DN_HEADS):
            sl = slice(gi * LANES, (gi + 1) * LANES)
            sg = s[:, sl]
            rinv = lax.rsqrt(jnp.sum(sg * sg, axis=1, keepdims=True) + EPS)
            nh = sg * rinv
            dn = dqkv_ref[:, sl] * (DN_HD ** -0.5 if gi < DN_HEADS else 1.0)
            dsg = rinv * (dn - nh * jnp.sum(dn * nh, axis=1, keepdims=True))
            dc_ref[:, sl] = dsg * ds_act[:, sl]
        dc_ref[:, 2 * DN_W:] = dqkv_ref[:, 2 * DN_W:] * ds_act[:, 2 * DN_W:]

        ba = ba_ref[...]
        beta = _sigmoid(ba)
        ea = jnp.exp(a_ref[...])
        pre = ba + dtb_ref[...]
        g = -ea * _softplus(pre)
        lr, lc = _iota2((DN_W, LANES), 0), _iota2((DN_W, LANES), 1)
        pick_b = jnp.where(lc == lr // LANES, 1.0 / LANES, 0.0)
        pick_g = jnp.where(lc == lr // LANES + DN_HEADS, 1.0 / LANES, 0.0)
        dbeta = _mm_xl(dbb_ref[...], pick_b)
        dg = _mm_xl(dgb_ref[...], pick_g)
        lane = _iota2((1, LANES), 1)
        da = dg * (-ea) * _sigmoid(pre)
        dba_ref[...] = jnp.where(lane < DN_HEADS, dbeta * beta * (1.0 - beta),
                                 jnp.where(lane < 2 * DN_HEADS, da, 0.0)).astype(dba_ref.dtype)
        dal_ref[...] += jnp.sum(dg * g, axis=0, keepdims=True)
        ddt_ref[...] += jnp.sum(da, axis=0, keepdims=True)

    return pl.pallas_call(
        body, name="dn_prep_bwd_a", grid=(S // tm,),
        in_specs=[_rb(tm, W3, 0), pl.BlockSpec((8, W3), lambda i: (jnp.maximum(i * nhalo - 1, 0), 0)),
                  _fs((CONV_K, W3)), _rb(tm, LANES, (PC - LANES) // LANES), _fs((1, LANES)), _fs((1, LANES)),
                  _rb(tm, W3), _rb(tm, DN_W), _rb(tm, DN_W)],
        out_specs=[_rb(tm, W3), _rb(tm, LANES), _fs((1, LANES)), _fs((1, LANES))],
        out_shape=[_sds((S, W3), F32), _sds((S, LANES), _MXU_DTYPE), _sds((1, LANES), F32), _sds((1, LANES), F32)],
        compiler_params=_cp(1),
    )(p, p, conv_w, p, a_row, dtb_row, dqkv, dbb, dgb)


def _dn_prep_bwd_b(p, dc, conv_w, tm=256):
    S = p.shape[0]
    W3 = 3 * DN_W
    nhalo = tm // 8
    nblk = S // tm

    def body(x_ref, xh_ref, dc_ref, dch_ref, w_ref, dx_ref, dw_ref):
        i = pl.program_id(0)

        @pl.when(i == 0)
        def _():
            dw_ref[...] = jnp.zeros_like(dw_ref)

        dcv = dc_ref[...]
        xf = jnp.concatenate([jnp.where(i > 0, xh_ref[...], 0.0), x_ref[...]], axis=0)
        df = jnp.concatenate([dcv, jnp.where(i < nblk - 1, dch_ref[...], 0.0)], axis=0)
        acc = jnp.zeros((tm, W3), F32)
        for k in range(CONV_K):
            sh = CONV_K - 1 - k
            xs = xf if sh == 0 else pltpu.roll(xf, sh, 0)
            dw_ref[k:k + 1, :] += jnp.sum(dcv * xs[8:, :], axis=0, keepdims=True)
            ds = df if sh == 0 else pltpu.roll(df, tm + 8 - sh, 0)
            acc = acc + ds[:tm, :] * w_ref[k:k + 1, :]
        dx_ref[...] = acc.astype(dx_ref.dtype)

    return pl.pallas_call(
        body, name="dn_prep_bwd_b", grid=(nblk,),
        in_specs=[_rb(tm, W3, 0), pl.BlockSpec((8, W3), lambda i: (jnp.maximum(i * nhalo - 1, 0), 0)),
                  _rb(tm, W3), pl.BlockSpec((8, W3), lambda i: (jnp.minimum((i + 1) * nhalo, S // 8 - 1), 0)),
                  _fs((CONV_K, W3))],
        out_specs=[_rb(tm, W3), _fs((CONV_K, W3))],
        out_shape=[_sds((S, W3), _MXU_DTYPE), _sds((CONV_K, W3), F32)], compiler_params=_cp(1),
    )(p, p, dc, dc, conv_w)


def _gate(o_att, o_dn, p, gn, tm=256):
    S = p.shape[0]

    def body(oa_ref, zs_ref, od_ref, zd_ref, gn_ref, osb_ref, odn_ref):
        osb_ref[...] = (oa_ref[...] * _silu(zs_ref[...])).astype(osb_ref.dtype)
        for h in range(DN_HEADS):
            sl = slice(h * LANES, (h + 1) * LANES)
            o = od_ref[:, sl]
            r = lax.rsqrt(jnp.mean(o * o, axis=1, keepdims=True) + EPS)
            odn_ref[:, sl] = (o * r * gn_ref[...] * _silu(zd_ref[:, sl])).astype(odn_ref.dtype)

    return pl.pallas_call(
        body, name="gate", grid=(S // tm,),
        in_specs=[_rb(tm, SB_W), _rb(tm, SB_W, C_SB_Z // SB_W), _rb(tm, DN_W), _rb(tm, DN_W, C_DN_Z // DN_W),
                  _fs((1, LANES))],
        out_specs=[_rb(tm, SB_W), _rb(tm, DN_W)],
        out_shape=[_sds((S, SB_W), _MXU_DTYPE), _sds((S, DN_W), _MXU_DTYPE)], compiler_params=_cp(1),
    )(o_att, p, o_dn, p, gn)


def _gate_bwd(db_sb, db_dn, wb_sb, wb_dn, o_att, o_dn, p, gn, tm=256):
    S = p.shape[0]
    D = db_sb.shape[1]

    def body(dbs_ref, dbd_ref, ws_ref, wd_ref, oa_ref, zs_ref, od_ref, zd_ref, gn_ref,
             doa_ref, dzs_ref, dod_ref, dzd_ref, dgn_ref):
        @pl.when(pl.program_id(0) == 0)
        def _():
            dgn_ref[...] = jnp.zeros_like(dgn_ref)

        do_sb = _mm(dbs_ref[...], ws_ref[...], _NT)
        zs = zs_ref[...]
        doa_ref[...] = do_sb * _silu(zs)
        dzs_ref[...] = (do_sb * oa_ref[...] * _dsilu(zs)).astype(dzs_ref.dtype)
        do_dnn = _mm(dbd_ref[...], wd_ref[...], _NT)
        gnv = gn_ref[...]
        for h in range(DN_HEADS):
            sl = slice(h * LANES, (h + 1) * LANES)
            o, z, dout = od_ref[:, sl], zd_ref[:, sl], do_dnn[:, sl]
            r = lax.rsqrt(jnp.mean(o * o, axis=1, keepdims=True) + EPS)
            oh = o * r
            sz = _silu(z)
            dzd_ref[:, sl] = (dout * oh * gnv * _dsilu(z)).astype(dzd_ref.dtype)
            dgn_ref[...] += jnp.sum(dout * sz * oh, axis=0, keepdims=True)
            doh = dout * gnv * sz
            dod_ref[:, sl] = r * (doh - oh * jnp.mean(doh * oh, axis=1, keepdims=True))

    return pl.pallas_call(
        body, name="gate_bwd", grid=(S // tm,),
        in_specs=[_rb(tm, D), _rb(tm, D), _fs((SB_W, D)), _fs((DN_W, D)), _rb(tm, SB_W),
                  _rb(tm, SB_W, C_SB_Z // SB_W), _rb(tm, DN_W), _rb(tm, DN_W, C_DN_Z // DN_W), _fs((1, LANES))],
        out_specs=[_rb(tm, SB_W), _rb(tm, SB_W), _rb(tm, DN_W), _rb(tm, DN_W), _fs((1, LANES))],
        out_shape=[_sds((S, SB_W), F32), _sds((S, SB_W), _MXU_DTYPE), _sds((S, DN_W), F32),
                   _sds((S, DN_W), _MXU_DTYPE), _sds((1, LANES), F32)],
        compiler_params=_cp(1),
    )(db_sb, db_dn, wb_sb, wb_dn, o_att, p, o_dn, p, gn)


def _branch(o_sb, o_dnn, wb_sb, wb_dn, p, D, tm=256):
    S = p.shape[0]

    def body(os_ref, od_ref, ws_ref, wd_ref, ms_ref, md_ref, y_ref, bs_ref, bd_ref):
        bs = _mm(os_ref[...], ws_ref[...])
        bdn = _mm(od_ref[...], wd_ref[...])
        bs_ref[...] = bs
        bd_ref[...] = bdn
        y_ref[...] = (_sigmoid(ms_ref[...]) * bs + _sigmoid(md_ref[...]) * bdn).astype(y_ref.dtype)

    return pl.pallas_call(
        body, name="branch", grid=(S // tm,),
        in_specs=[_rb(tm, SB_W), _rb(tm, DN_W), _fs((SB_W, D)), _fs((DN_W, D)),
                  _rb(tm, D, C_MG // D), _rb(tm, D, C_MG // D + 1)],
        out_specs=[_rb(tm, D), _rb(tm, D), _rb(tm, D)],
        out_shape=[_sds((S, D), _MXU_DTYPE), _sds((S, D), F32), _sds((S, D), F32)], compiler_params=_cp(1),
    )(o_sb, o_dnn, wb_sb, wb_dn, p, p)


def _out_proj(x, y, w_out, gate, tm=256):
    S, D = x.shape

    def body(x_ref, y_ref, w_ref, g_ref, xn_ref, out_ref):
        out = _mm(y_ref[...], w_ref[...])
        out_ref[...] = out
        xn_ref[...] = x_ref[...] + g_ref[...] * out

    return pl.pallas_call(
        body, name="out_proj", grid=(S // tm,),
        in_specs=[_rb(tm, D), _rb(tm, D), _fs((D, D)), _fs((1, D))],
        out_specs=[_rb(tm, D), _rb(tm, D)],
        out_shape=[_sds((S, D), F32), _sds((S, D), F32)], compiler_params=_cp(1),
    )(x, y, w_out, gate)


def _out_bwd(dxn, out, gate, w_out, p, b_sb, b_dn, tm=256):
    S, D = dxn.shape

    def body(dxn_ref, out_ref, g_ref, w_ref, ms_ref, md_ref, bs_ref, bd_ref,
             dout_ref, dbs_ref, dbd_ref, dm_ref, dgate_ref):
        @pl.when(pl.program_id(0) == 0)
        def _():
            dgate_ref[...] = jnp.zeros_like(dgate_ref)

        dxv = dxn_ref[...]
        dgate_ref[...] += jnp.sum(dxv * out_ref[...], axis=0, keepdims=True)
        dout = (g_ref[...] * dxv).astype(dout_ref.dtype)
        dout_ref[...] = dout
        dy = _mm(dout, w_ref[...], _NT)
        s1, s2 = _sigmoid(ms_ref[...]), _sigmoid(md_ref[...])
        dbs_ref[...] = (dy * s1).astype(dbs_ref.dtype)
        dbd_ref[...] = (dy * s2).astype(dbd_ref.dtype)
        dm_ref[:, :D] = (dy * bs_ref[...] * s1 * (1.0 - s1)).astype(dm_ref.dtype)
        dm_ref[:, D:] = (dy * bd_ref[...] * s2 * (1.0 - s2)).astype(dm_ref.dtype)

    return pl.pallas_call(
        body, name="out_bwd", grid=(S // tm,),
        in_specs=[_rb(tm, D), _rb(tm, D), _fs((1, D)), _fs((D, D)), _rb(tm, D, C_MG // D),
                  _rb(tm, D, C_MG // D + 1), _rb(tm, D), _rb(tm, D)],
        out_specs=[_rb(tm, D), _rb(tm, D), _rb(tm, D), _rb(tm, 2 * D), _fs((1, D))],
        out_shape=[_sds((S, D), _MXU_DTYPE)] * 3 + [_sds((S, 2 * D), _MXU_DTYPE), _sds((1, D), F32)],
        compiler_params=_cp(1),
    )(dxn, out, gate, w_out, p, p, b_sb, b_dn)


def _loss_head(xf, target, tm=256):
    S, D = xf.shape

    def body(x_ref, t_ref, dy_ref, loss_ref):
        @pl.when(pl.program_id(0) == 0)
        def _():
            loss_ref[...] = jnp.zeros_like(loss_ref)

        e = x_ref[...] - t_ref[...]
        dy_ref[...] = e * (1.0 / D)
        row = jnp.sum(e * e, axis=1, keepdims=True) * (1.0 / D)
        loss_ref[...] += 0.5 * jnp.sum(row, axis=0, keepdims=True)

    return pl.pallas_call(
        body, name="loss_head", grid=(S // tm,),
        in_specs=[_rb(tm, D), _rb(tm, D)], out_specs=[_rb(tm, D), _fs((1, LANES))],
        out_shape=[_sds((S, D), F32), _sds((1, LANES), F32)], compiler_params=_cp(1),
    )(xf, target)


def _ada_fwd(c_all, ada_w, ada_b_sh):
    L, D, n = ada_w.shape
    B = c_all.shape[0]

    def body(c_ref, w_ref, b_ref, o_ref):
        sc = _silu(c_ref[...])
        o_ref[0] = _mm(sc, w_ref[0]) + b_ref[0]

    return pl.pallas_call(
        body, name="ada_fwd", grid=(L,),
        in_specs=[_fs((B, D)), pl.BlockSpec((1, D, n), lambda l: (l, 0, 0)), pl.BlockSpec((1, 1, n), lambda l: (l, 0, 0))],
        out_specs=pl.BlockSpec((1, B, n), lambda l: (l, 0, 0)),
        out_shape=_sds((L, B, n), F32), compiler_params=_cp(1),
    )(c_all, ada_w, ada_b_sh)


def _ada_bwd(c_all_t, dmod_sh):
    D, B = c_all_t.shape
    L, _, n = dmod_sh.shape

    def body(c_ref, d_ref, o_ref):
        acc = jnp.zeros((D, n), F32)
        for b in range(B):
            acc = acc + _silu(c_ref[:, b:b + 1]) * d_ref[0, b:b + 1, :]
        o_ref[0] = acc

    return pl.pallas_call(
        body, name="ada_bwd", grid=(L,),
        in_specs=[_fs((D, B)), pl.BlockSpec((1, B, n), lambda l: (l, 0, 0))],
        out_specs=pl.BlockSpec((1, D, n), lambda l: (l, 0, 0)),
        out_shape=_sds((L, D, n), F32), compiler_params=_cp(1),
    )(c_all_t, dmod_sh)


def _sum_parts(name, parts):
    P, R, C = parts.shape
    tr = _pick(R, max(16, min(512, (1 << 19) // (P * C))), 16) if R % 16 == 0 else R

    def body(p_ref, o_ref):
        acc = p_ref[0].astype(F32)
        for k in range(1, P):
            acc = acc + p_ref[k].astype(F32)
        o_ref[...] = acc

    return pl.pallas_call(
        body, name=name, grid=(R // tr,),
        in_specs=[pl.BlockSpec((P, tr, C), lambda i: (0, i, 0))], out_specs=_rb(tr, C),
        out_shape=_sds((R, C), F32), compiler_params=_cp(1),
    )(parts)


def _adamw(name, w, g, m, v):
    L, R, C = w.shape
    tr = _pick(R, 256, 8) if R % 8 == 0 else R
    c1 = 1.0 - ADAM_B1 ** ADAM_STEP
    c2 = 1.0 - ADAM_B2 ** ADAM_STEP

    def body(w_ref, g_ref, m_ref, v_ref, d_ref, mo_ref, vo_ref):
        gv = g_ref[...]
        mn = ADAM_B1 * m_ref[...] + (1.0 - ADAM_B1) * gv
        vn = ADAM_B2 * v_ref[...] + (1.0 - ADAM_B2) * (gv * gv)
        mo_ref[...] = mn
        vo_ref[...] = vn
        d_ref[...] = -ADAM_LR * ((mn / c1) / (jnp.sqrt(vn / c2) + ADAM_EPS) + ADAM_WD * w_ref[...])

    spec = pl.BlockSpec((1, tr, C), lambda l, i: (l, i, 0))
    return pl.pallas_call(
        body, name=name, grid=(L, R // tr),
        in_specs=[spec] * 4, out_specs=[spec] * 3, out_shape=[_sds((L, R, C), F32)] * 3, compiler_params=_cp(2),
    )(w, g, m, v)


def _ag_small(name, blk):
    R, C = blk.shape

    def body(x_ref, out_ref, send_sems, recv_sems, local_sem):
        x, y, c = lax.axis_index("x"), lax.axis_index("y"), lax.axis_index("c")
        me, sibling = (x, y, c), (x, y, 1 - c)
        chips = [(1 - x, y), (x, 1 - y), (1 - x, 1 - y)]

        def rows(px, py, pc):
            return out_ref.at[pl.ds((4 * px + 2 * py + pc) * R, R), :]

        def copy(k, block, to, src=None):
            return pltpu.make_async_remote_copy(
                src_ref=rows(*block) if src is None else src, dst_ref=rows(*block),
                send_sem=send_sems.at[k], recv_sem=recv_sems.at[k], device_id=to, device_id_type=MESH)

        mine = pltpu.make_async_copy(x_ref, rows(*me), local_sem)
        mine.start()
        first = [copy(0, me, sibling, src=x_ref)]
        first += [copy(1 + j, me, (*chip, c), src=x_ref) for j, chip in enumerate(chips)]
        for cp in first:
            cp.start()
        passed = [copy(4 + j, (*chip, c), sibling) for j, chip in enumerate(chips)]
        for j, chip in enumerate(chips):
            copy(1 + j, (*chip, c), me).wait_recv()
            passed[j].start()
        copy(0, sibling, me).wait_recv()
        for j, chip in enumerate(chips):
            copy(4 + j, (*chip, 1 - c), me).wait_recv()
        for cp in first + passed:
            cp.wait_send()
        mine.wait()

    return pl.pallas_call(
        body, name=name, out_shape=_sds((8 * R, C), blk.dtype),
        in_specs=[pl.BlockSpec(memory_space=pltpu.VMEM)], out_specs=pl.BlockSpec(memory_space=pltpu.VMEM),
        scratch_shapes=[pltpu.SemaphoreType.DMA((7,)), pltpu.SemaphoreType.DMA((7,)), pltpu.SemaphoreType.DMA],
    )(blk)


def _row_chunks(ts, row_axis):
    pieces = []
    for t, a in enumerate(ts):
        rows = a.shape[row_axis]
        n = 4 if rows >= 1024 else 1
        pieces += [(t, i * (rows // n), rows // n) for i in range(n)]
    return pieces


def _ag_weights_first(ts):
    nt = len(ts)
    pieces = _row_chunks(ts, 0)
    NP = len(pieces)
    sizes = [nr * ts[t].shape[1] for t, _, nr in pieces]
    split = next(pi for pi in range(NP + 1) if 2 * sum(sizes[:pi]) >= sum(sizes))

    def body(*refs):
        w, out = refs[:nt], refs[nt:2 * nt]
        send_sems, recv_sems, local_sems = refs[2 * nt:]
        x, y, c = lax.axis_index("x"), lax.axis_index("y"), lax.axis_index("c")
        me, sibling = (x, y, c), (x, y, 1 - c)
        mine = 2 * x + y
        chips = [(1 - x, y), (x, 1 - y), (1 - x, 1 - y)]

        def blk(t, shard, r0, nr):
            return out[t].at[shard, r0:r0 + nr, :]

        def copy(k, dst, to, src=None):
            return pltpu.make_async_remote_copy(
                src_ref=dst if src is None else src, dst_ref=dst, send_sem=send_sems.at[k], recv_sem=recv_sems.at[k],
                device_id=to, device_id_type=MESH)

        own = [pltpu.make_async_copy(w[t], out[t].at[mine], local_sems.at[t]) for t in range(nt)]
        for cp in own:
            cp.start()
        for fetcher, lo, hi in ((0, 0, split), (1, split, NP)):
            @pl.when(c == fetcher)
            def _(lo=lo, hi=hi):
                sent = []
                for j, chip in enumerate(chips):
                    for pi in range(lo, hi):
                        t, r0, nr = pieces[pi]
                        sent.append(copy(j * NP + pi, blk(t, mine, r0, nr), (*chip, c), src=w[t].at[r0:r0 + nr, :]))
                        sent[-1].start()
                for j, chip in enumerate(chips):
                    theirs = 2 * chip[0] + chip[1]
                    for pi in range(lo, hi):
                        t, r0, nr = pieces[pi]
                        copy(j * NP + pi, blk(t, theirs, r0, nr), me).wait_recv()
                        sent.append(copy((3 + j) * NP + pi, blk(t, theirs, r0, nr), sibling))
                        sent[-1].start()
                for cp in sent:
                    cp.wait_send()

            @pl.when(c != fetcher)
            def _(lo=lo, hi=hi):
                for j, chip in enumerate(chips):
                    theirs = 2 * chip[0] + chip[1]
                    for pi in range(lo, hi):
                        t, r0, nr = pieces[pi]
                        copy((3 + j) * NP + pi, blk(t, theirs, r0, nr), me).wait_recv()

        for cp in own:
            cp.wait()

    return pl.pallas_call(
        body, name="ag_weights_first", out_shape=[_sds((4,) + a.shape, a.dtype) for a in ts],
        in_specs=[pl.BlockSpec(memory_space=pl.ANY)] * nt, out_specs=[pl.BlockSpec(memory_space=pltpu.VMEM)] * nt,
        scratch_shapes=[pltpu.SemaphoreType.DMA((6 * NP,)), pltpu.SemaphoreType.DMA((6 * NP,)),
                        pltpu.SemaphoreType.DMA((nt,))],
        compiler_params=pltpu.CompilerParams(vmem_limit_bytes=_VMEM_LIMIT),
    )(*ts)


class _WeightGather:
    def __init__(self, ts):
        self.nt = len(ts)
        self.pieces = _row_chunks(ts, 0)
        NP = len(self.pieces)
        self.out_shape = [_sds((4,) + a.shape, a.dtype) for a in ts]
        self.scratch = [pltpu.SemaphoreType.DMA((3 * NP,)), pltpu.SemaphoreType.DMA((3 * NP,)),
                        pltpu.SemaphoreType.DMA((self.nt,))]

    def _copies(self, src, out, sems):
        send_sems, recv_sems, local_sems = sems
        NP = len(self.pieces)
        x, y, c = lax.axis_index("x"), lax.axis_index("y"), lax.axis_index("c")
        mine = 2 * x + y
        own = [pltpu.make_async_copy(src[t], out[t].at[mine], local_sems.at[t]) for t in range(self.nt)]
        sends, recvs = [], []
        for j, chip in enumerate([(1 - x, y), (x, 1 - y), (1 - x, 1 - y)]):
            theirs = 2 * chip[0] + chip[1]
            for pi, (t, r0, nr) in enumerate(self.pieces):
                idx = j * NP + pi
                sends.append(pltpu.make_async_remote_copy(
                    src_ref=src[t].at[r0:r0 + nr, :], dst_ref=out[t].at[mine, r0:r0 + nr, :],
                    send_sem=send_sems.at[idx], recv_sem=recv_sems.at[idx], device_id=(*chip, c), device_id_type=MESH))
                recvs.append(pltpu.make_async_remote_copy(
                    src_ref=out[t].at[theirs, r0:r0 + nr, :], dst_ref=out[t].at[theirs, r0:r0 + nr, :],
                    send_sem=send_sems.at[idx], recv_sem=recv_sems.at[idx], device_id=(x, y, c), device_id_type=MESH))
        return own, sends, recvs

    def start(self, src, out, sems):
        own, sends, _ = self._copies(src, out, sems)
        for cp in own + sends:
            cp.start()

    def wait(self, src, out, sems):
        own, sends, recvs = self._copies(src, out, sems)
        for cp in recvs:
            cp.wait_recv()
        for cp in sends:
            cp.wait_send()
        for cp in own:
            cp.wait()


class _GradExchange:
    def __init__(self, ts, layer, chunks=None):
        self.nt, self.layer = len(ts), layer
        self.pieces = [p for i, p in enumerate(_row_chunks(ts, 1)) if chunks is None or i in chunks]
        NP = len(self.pieces)
        self.out_shape = [_sds((8,) + a.shape[1:], a.dtype) for a in ts]
        self.scratch = [pltpu.SemaphoreType.DMA((7 * NP,)), pltpu.SemaphoreType.DMA((7 * NP,)),
                        pltpu.SemaphoreType.DMA((NP,))]

    def _copies(self, src, out, sems):
        send_sems, recv_sems, local_sems = sems
        NP = len(self.pieces)
        x, y, c = lax.axis_index("x"), lax.axis_index("y"), lax.axis_index("c")
        me = 4 * x + 2 * y + c
        owner = c == self.layer
        own = [pltpu.make_async_copy(src[t].at[2 * x + y, r0:r0 + nr, :], out[t].at[me, r0:r0 + nr, :],
                                     local_sems.at[pi]) for pi, (t, r0, nr) in enumerate(self.pieces)]
        rel = []
        for k in range(1, 8):
            px = 1 - x if k & 4 else x
            py = 1 - y if k & 2 else y
            source = 4 * px + 2 * py + (1 - c if k & 1 else c)
            sends, recvs = [], []
            for pi, (t, r0, nr) in enumerate(self.pieces):
                idx = (k - 1) * NP + pi
                sends.append(pltpu.make_async_remote_copy(
                    src_ref=src[t].at[2 * px + py, r0:r0 + nr, :], dst_ref=out[t].at[me, r0:r0 + nr, :],
                    send_sem=send_sems.at[idx], recv_sem=recv_sems.at[idx], device_id=(px, py, self.layer),
                    device_id_type=MESH))
                recvs.append(pltpu.make_async_remote_copy(
                    src_ref=out[t].at[source, r0:r0 + nr, :], dst_ref=out[t].at[source, r0:r0 + nr, :],
                    send_sem=send_sems.at[idx], recv_sem=recv_sems.at[idx], device_id=(x, y, c),
                    device_id_type=MESH))
            rel.append((jnp.logical_not(owner) if k & 1 else owner, sends, recvs))
        return owner, own, rel

    def start(self, src, out, sems):
        owner, own, rel = self._copies(src, out, sems)

        @pl.when(owner)
        def _():
            for cp in own:
                cp.start()

        for sending, sends, _ in rel:
            @pl.when(sending)
            def _(sends=sends):
                for cp in sends:
                    cp.start()

    def wait(self, src, out, sems):
        owner, own, rel = self._copies(src, out, sems)

        @pl.when(owner)
        def _():
            for _, _, recvs in rel:
                for cp in recvs:
                    cp.wait_recv()
            for cp in own:
                cp.wait()

        for sending, sends, _ in rel:
            @pl.when(sending)
            def _(sends=sends):
                for cp in sends:
                    cp.wait_send()


def _sibling_join(ts):
    nt = len(ts)
    pieces = _row_chunks(ts, 0)
    NP = len(pieces)

    def body(*refs):
        src, out = refs[:nt], refs[nt:2 * nt]
        send_sems, recv_sems, local_sems = refs[2 * nt:]
        x, y, c = lax.axis_index("x"), lax.axis_index("y"), lax.axis_index("c")
        own = [pltpu.make_async_copy(src[t], out[t].at[c], local_sems.at[t]) for t in range(nt)]
        for cp in own:
            cp.start()
        sent = []
        for pi, (t, r0, nr) in enumerate(pieces):
            sent.append(pltpu.make_async_remote_copy(
                src_ref=src[t].at[r0:r0 + nr, :], dst_ref=out[t].at[c, r0:r0 + nr, :], send_sem=send_sems.at[pi],
                recv_sem=recv_sems.at[pi], device_id=(x, y, 1 - c), device_id_type=MESH))
            sent[-1].start()
        for pi, (t, r0, nr) in enumerate(pieces):
            pltpu.make_async_remote_copy(
                src_ref=src[t].at[r0:r0 + nr, :], dst_ref=out[t].at[1 - c, r0:r0 + nr, :], send_sem=send_sems.at[pi],
                recv_sem=recv_sems.at[pi], device_id=(x, y, c), device_id_type=MESH).wait_recv()
        for cp in sent:
            cp.wait_send()
        for cp in own:
            cp.wait()

    vmem = pl.BlockSpec(memory_space=pltpu.VMEM)
    return pl.pallas_call(
        body, name="sibling_join", out_shape=[_sds((2,) + a.shape, a.dtype) for a in ts],
        in_specs=[vmem] * nt, out_specs=[vmem] * nt,
        scratch_shapes=[pltpu.SemaphoreType.DMA((NP,)), pltpu.SemaphoreType.DMA((NP,)),
                        pltpu.SemaphoreType.DMA((nt,))],
        compiler_params=pltpu.CompilerParams(vmem_limit_bytes=_VMEM_LIMIT),
    )(*ts)


def _late_weights(gathered):
    g_bs, g_bd, g_out = gathered
    cat = lambda g, axis: jnp.concatenate([g[s] for s in range(4)], axis=axis)
    return dict(wb_sb=cat(g_bs, 1), wb_dn=cat(g_bd, 1), w_out=cat(g_out, 0))


def _layer_fwd(x, shift, scale, gate, lw, next_shards=None, late_shards=None):
    D = x.shape[1]
    h = _norm_mod(x, lw["norm_g"], scale, shift)
    if late_shards is None:
        p = _matmul("in_proj", h, lw["w_cat"], "nn", F32, tm_cap=1024, tn_cap=896)
    else:
        p, late = _matmul("in_proj", h, lw["w_cat"], "nn", F32, tm_cap=1024, tn_cap=896,
                          exchange=_WeightGather(late_shards), ex_src=late_shards)
        lw = {**lw, **_late_weights(late)}
    qn, kn = _sb_prep(p, lw["gq_t"], lw["gk_t"])
    qkv, bb, gcb, glb = _dn_prep(p, lw["conv_w"], lw["a_row"], lw["dtb_row"])
    if next_shards is None:
        (o_att, tot, nblocks), _ = _sb_fwd(qn, kn, p)
        (o_dn, tinv, sall), gathered = _dn_fwd(qkv, bb, gcb, glb)
    else:
        first, rest = next_shards[:1], next_shards[1:]
        (o_att, tot, nblocks), g_rest = _sb_fwd(qn, kn, p, _WeightGather(rest), rest)
        (o_dn, tinv, sall), g_first = _dn_fwd(qkv, bb, gcb, glb, _WeightGather(first), first)
        gathered = list(g_first) + g_rest
    o_sb, o_dnn = _gate(o_att, o_dn, p, lw["gn"])
    y, b_sb, b_dn = _branch(o_sb, o_dnn, lw["wb_sb"], lw["wb_dn"], p, D)
    x_next, out = _out_proj(x, y, lw["w_out"], gate)
    res = dict(x=x, h=h, p=p, qn=qn, kn=kn, o_att=o_att, tot=tot, nblocks=nblocks, qkv=qkv, bb=bb, gcb=gcb, glb=glb, o_dn=o_dn,
               tinv=tinv, sall=sall, o_sb=o_sb, o_dnn=o_dnn, y=y, b_sb=b_sb, b_dn=b_dn, out=out,
               shift=shift, scale=scale, gate=gate)
    return x_next, res, lw, gathered


def _layer_bwd(dxn, res, lw, pending=None):
    p = res["p"]
    dout, db_sb, db_dn, dm, dgate = _out_bwd(dxn, res["out"], res["gate"], lw["w_out"], p, res["b_sb"], res["b_dn"])
    dw_out = _matmul("dw_out", res["y"], dout, "tn", _MXU_DTYPE)
    dwb_sb = _matmul("dwb_sb", res["o_sb"], db_sb, "tn", _MXU_DTYPE)
    dwb_dn = _matmul("dwb_dn", res["o_dnn"], db_dn, "tn", _MXU_DTYPE)
    do_att, dz_sb, do_dn, dz_dn, dgn = _gate_bwd(db_sb, db_dn, lw["wb_sb"], lw["wb_dn"], res["o_att"], res["o_dn"],
                                                  p, lw["gn"])
    D = dxn.shape[1]
    by_shard = lambda g: g.reshape(g.shape[0], 4, g.shape[1] // 4).transpose(1, 0, 2)
    send = [by_shard(dwb_sb), by_shard(dwb_dn), dw_out.reshape(4, D // 4, D)]
    dn_args = (res["qkv"], res["bb"], res["gcb"], res["glb"], res["tinv"], res["sall"], do_dn)
    if pending is None:
        (dqn, dkn, dv), _ = _sb_bwd(res["qn"], res["kn"], p, do_att, res["tot"], res["nblocks"])
        (dqkv, dbb, dgb), received = _dn_bwd(*dn_args)
    else:
        above, above_send = pending
        (dqn, dkn, dv), got_in = _sb_bwd(res["qn"], res["kn"], p, do_att, res["tot"], res["nblocks"],
                                         _GradExchange(above_send[:1], above), above_send[:1])
        (dqkv, dbb, dgb), got_rest = _dn_bwd(*dn_args, _GradExchange(above_send[1:], above), above_send[1:],
                                             _GradExchange(send, above - 1), send)
        received, send = list(got_in) + got_rest, []
    dq_sb, dk_sb, dgq, dgk = _sb_prep_bwd(p, dqn, dkn, lw["gq_t"], lw["gk_t"])
    dc, dp_ba, dal, ddt = _dn_prep_bwd_a(p, dqkv, dbb, dgb, lw["conv_w"], lw["a_row"], lw["dtb_row"])
    dp_dn, dconv = _dn_prep_bwd_b(p, dc, lw["conv_w"])
    dp = jnp.concatenate([dp_dn, dz_dn, dq_sb, dk_sb, dv.astype(_MXU_DTYPE), dz_sb, dm, dp_ba], axis=1)
    dw_cat = _matmul("dw_cat", res["h"], dp, "tn", _MXU_DTYPE, tm_cap=1024, tn_cap=896, tk_cap=2048)
    send = [_shards_from_cat(dw_cat, D)] + send
    if pending is None:
        dh = _matmul("dh", dp, lw["w_cat"], "nt", F32, tm_cap=1024, tk_cap=896)
        (dx, dshift, dscale, dnorm_g), _ = _norm_mod_bwd(res["x"], dh, dxn, lw["norm_g"], res["scale"])
    else:
        dh, arrived = _matmul("dh", dp, lw["w_cat"], "nt", F32, tm_cap=1024, tk_cap=896,
                              exchange=_GradExchange(send, pending[0] - 1, (0, 1, 2)), ex_src=send,
                              ex_prev=received[:1])
        tail = _GradExchange(send, pending[0] - 1, (3,))
        if tail.pieces:
            (dx, dshift, dscale, dnorm_g), arrived = _norm_mod_bwd(
                res["x"], dh, dxn, lw["norm_g"], res["scale"], exchange=tail, ex_src=send, ex_prev=arrived)
        else:
            (dx, dshift, dscale, dnorm_g), _ = _norm_mod_bwd(res["x"], dh, dxn, lw["norm_g"], res["scale"])
        received, send = arrived + list(received[1:]), []
    small = dict(dmod=jnp.concatenate([dshift, dscale, dgate], axis=1)[0], norm_g=dnorm_g[0],
                 sb_q_g=dgq.reshape(SB_HEADS, SB_HD).sum(0), sb_k_g=dgk.reshape(SB_HEADS, SB_HD).sum(0),
                 conv_w=dconv, dn_a_log=dal[0, DN_HEADS:2 * DN_HEADS], dn_dt_bias=ddt[0, DN_HEADS:2 * DN_HEADS],
                 dn_norm_g=dgn[0])
    return dx, small, send, received


def _cat_cols(w, D):
    return jnp.concatenate([w[:, 2048:4096], w[:, 0:2048], w[:, 4104:4104 + 2 * D], w[:, 4096:4104],
                            jnp.zeros((w.shape[0], LANES - 8), w.dtype)], axis=1)


def _shards_from_cat(g, D):
    n = (4104 + 2 * D) // 4
    segments = ((0, 2048, 2048), (2048, 4096, 0), (4096, 4104, 4096 + 2 * D), (4104, 4104 + 2 * D, 4096))

    def shard(lo, hi):
        cuts = [(c0 + max(lo, s0) - s0, c0 + min(hi, s1) - s0) for s0, s1, c0 in segments if max(lo, s0) < min(hi, s1)]
        return jnp.concatenate([g[:, a:b] for a, b in cuts], axis=1)

    return jnp.stack([shard(s * n, (s + 1) * n) for s in range(4)])


def _flat_pack(arrs, mult):
    flat = jnp.concatenate([a.reshape(-1) for a in arrs])
    n = flat.shape[0]
    pad = (-n) % mult
    if pad:
        flat = jnp.concatenate([flat, jnp.zeros((pad,), flat.dtype)])
    return flat.reshape(-1, LANES)


def _flat_unpack(flat, shapes):
    flat = flat.reshape(-1)
    out, off = [], 0
    for s in shapes:
        n = math.prod(s)
        out.append(flat[off:off + n].reshape(s))
        off += n
    return out


BIG = ("w_in", "w_branch_sb", "w_branch_dn", "w_out")
SMALL = ("ada_b", "norm_g", "sb_q_g", "sb_k_g", "conv_w", "dn_a_log", "dn_dt_bias", "dn_norm_g")


def kernel(x, c, ada_w, ada_b, norm_g, w_in, sb_q_g, sb_k_g, conv_w, dn_a_log, dn_dt_bias, dn_norm_g, w_branch_sb, w_branch_dn, w_out, loss_target, m_ada_w, m_ada_b, m_norm_g, m_w_in, m_sb_q_g, m_sb_k_g, m_conv_w, m_dn_a_log, m_dn_dt_bias, m_dn_norm_g, m_w_branch_sb, m_w_branch_dn, m_w_out, v_ada_w, v_ada_b, v_norm_g, v_w_in, v_sb_q_g, v_sb_k_g, v_conv_w, v_dn_a_log, v_dn_dt_bias, v_dn_norm_g, v_w_branch_sb, v_w_branch_dn, v_w_out):
    W = dict(ada_w=ada_w, ada_b=ada_b, norm_g=norm_g, w_in=w_in, sb_q_g=sb_q_g, sb_k_g=sb_k_g, conv_w=conv_w,
             dn_a_log=dn_a_log, dn_dt_bias=dn_dt_bias, dn_norm_g=dn_norm_g, w_branch_sb=w_branch_sb,
             w_branch_dn=w_branch_dn, w_out=w_out)
    M = dict(ada_w=m_ada_w, ada_b=m_ada_b, norm_g=m_norm_g, w_in=m_w_in, sb_q_g=m_sb_q_g, sb_k_g=m_sb_k_g,
             conv_w=m_conv_w, dn_a_log=m_dn_a_log, dn_dt_bias=m_dn_dt_bias, dn_norm_g=m_dn_norm_g,
             w_branch_sb=m_w_branch_sb, w_branch_dn=m_w_branch_dn, w_out=m_w_out)
    V = dict(ada_w=v_ada_w, ada_b=v_ada_b, norm_g=v_norm_g, w_in=v_w_in, sb_q_g=v_sb_q_g, sb_k_g=v_sb_k_g,
             conv_w=v_conv_w, dn_a_log=v_dn_a_log, dn_dt_bias=v_dn_dt_bias, dn_norm_g=v_dn_norm_g,
             w_branch_sb=v_w_branch_sb, w_branch_dn=v_w_branch_dn, w_out=v_w_out)
    L = ada_w.shape[0]
    S, D = x.shape[1], x.shape[2]
    ix, iy, ic = lax.axis_index("x"), lax.axis_index("y"), lax.axis_index("c")
    shard = 2 * ix + iy
    me = 2 * shard + ic
    n_ada = ada_w.shape[2]
    n_in = w_in.shape[2]
    n_conv = conv_w.shape[2]
    n_br = w_branch_sb.shape[2]
    n_out = w_out.shape[1]

    assert L == 2, "the owner of a layer's gradients is the core with the layer's number"
    shards = [W[n].astype(_MXU_DTYPE) for n in BIG]
    gathered0 = _ag_weights_first([shards[0][0]])

    g1 = _ag_small("ag_c_conv", _flat_pack([c, conv_w], LANES * 8))
    g1 = g1.reshape(8, -1)
    c_all = g1[:, :D]
    conv_parts = g1[:, D:D + L * CONV_K * n_conv].reshape(4, 2, L, CONV_K, n_conv)[:, 0]
    conv_full = jnp.concatenate([conv_parts[s] for s in range(4)], axis=2)
    ada_b_sh = lax.dynamic_slice_in_dim(ada_b, shard * n_ada, n_ada, axis=1)[:, None, :]
    mod_sh = _ada_fwd(c_all, ada_w, ada_b_sh)
    g2 = _ag_small("ag_mod", _flat_pack([mod_sh], LANES * 8)).reshape(8, -1)
    mod_parts = g2[:, :L * 8 * n_ada].reshape(4, 2, L, 8, n_ada)[:, 0]
    mod_all = jnp.concatenate([mod_parts[s] for s in range(4)], axis=2)
    mod = lax.dynamic_index_in_dim(mod_all, me, axis=1, keepdims=False)

    def layer_weights(l, g_in, late=None):
        pad_lo = jnp.zeros((DN_HEADS,), F32)
        pad_hi = jnp.zeros((LANES - 2 * DN_HEADS,), F32)
        lw = dict(
            norm_g=norm_g[l][None, :], w_cat=_cat_cols(jnp.concatenate([g_in[s] for s in range(4)], axis=1), D),
            gq_t=jnp.tile(sb_q_g[l], SB_HEADS)[None, :], gk_t=jnp.tile(sb_k_g[l], SB_HEADS)[None, :],
            conv_w=conv_full[l],
            a_row=jnp.concatenate([pad_lo, dn_a_log[l], pad_hi])[None, :],
            dtb_row=jnp.concatenate([pad_lo, dn_dt_bias[l], pad_hi])[None, :], gn=dn_norm_g[l][None, :])
        return lw if late is None else {**lw, **_late_weights(late)}

    mods = lambda l: (mod[l, None, 0:D], mod[l, None, D:2 * D], mod[l, None, 2 * D:3 * D])
    lws, ress = [None] * L, [None] * L
    xs, ress[0], lws[0], gathered1 = _layer_fwd(x[0], *mods(0), layer_weights(0, gathered0[0]),
                                                next_shards=[a[1] for a in shards],
                                                late_shards=[a[0] for a in shards[1:]])
    xs, ress[1], lws[1], _ = _layer_fwd(xs, *mods(1), layer_weights(1, gathered1[0], gathered1[1:]))
    dxs, loss_row = _loss_head(xs, loss_target[0])
    loss = lax.psum(loss_row[0, 0], ("x", "y", "c"))
    smalls = [None] * L
    dxs, smalls[1], send1, _ = _layer_bwd(dxs, ress[1], lws[1])
    dxs, smalls[0], send0, got = _layer_bwd(dxs, ress[0], lws[0], (1, send1))
    grad_x = dxs[None]

    small_names = ("dmod",) + SMALL[1:]
    small_pack = _flat_pack([jnp.stack([smalls[l][n] for l in range(L)]) for n in small_names], LANES * 8)
    g3 = _ag_small("ag_small_grads", small_pack)
    R3 = small_pack.shape[0]
    g3 = g3.reshape(8, R3, LANES)
    small_sum = _sum_parts("sum_small", g3)
    small_shapes = [(L, 3 * D), (L, D), (L, SB_HD), (L, SB_HD), (L, CONV_K, 3 * DN_W), (L, DN_HEADS), (L, DN_HEADS),
                    (L, DN_HD)]
    sg = dict(zip(small_names, _flat_unpack(small_sum, small_shapes)))
    G = dict(ada_b=sg["dmod"], norm_g=sg["norm_g"], sb_q_g=sg["sb_q_g"], sb_k_g=sg["sb_k_g"],
             conv_w=lax.dynamic_slice_in_dim(sg["conv_w"], shard * n_conv, n_conv, axis=2),
             dn_a_log=sg["dn_a_log"], dn_dt_bias=sg["dn_dt_bias"], dn_norm_g=sg["dn_norm_g"])
    dmod_all = g3.reshape(8, -1)[:, :L * 3 * D].reshape(8, L, 3 * D)
    dmod_sh = lax.dynamic_slice_in_dim(dmod_all, shard * n_ada, n_ada, axis=2).transpose(1, 0, 2)
    G["ada_w"] = _ada_bwd(c_all.T, dmod_sh)

    assert not send0
    mine = [_sum_parts("sum_" + n, g) for n, g in zip(BIG, got)]
    for n, g in zip(BIG, _sibling_join(mine)):
        G[n] = g

    delta, new_m, new_v = {}, {}, {}
    for n in ("ada_w",) + BIG:
        delta[n], new_m[n], new_v[n] = _adamw("adamw_" + n, W[n], G[n], M[n], V[n])
    sm_shapes = [W[n].shape for n in SMALL]
    d, mo, vo = _adamw("adamw_small", *[_flat_pack([T[n] for n in SMALL], LANES * 8)[None] for T in (W, G, M, V)])
    for n, dd, mm, vv in zip(SMALL, _flat_unpack(d, sm_shapes), _flat_unpack(mo, sm_shapes),
                             _flat_unpack(vo, sm_shapes)):
        delta[n], new_m[n], new_v[n] = dd, mm, vv

    order = ("ada_w", "ada_b", "norm_g", "w_in", "sb_q_g", "sb_k_g", "conv_w", "dn_a_log", "dn_dt_bias", "dn_norm_g",
             "w_branch_sb", "w_branch_dn", "w_out")
    return (loss, grad_x, *[G[n] for n in order], *[delta[n] for n in order], *[new_m[n] for n in order],
            *[new_v[n] for n in order])
```

```python
import math

import jax
import jax.numpy as jnp
from jax import lax
from jax.experimental import pallas as pl
from jax.experimental.pallas import tpu as pltpu

F32 = jnp.float32
BF16 = jnp.bfloat16
_MXU_DTYPE = BF16
_VMEM_LIMIT = 48 * 1024 * 1024
LANES = 128

EPS = 1e-6
SB_HEADS, SB_HD, SB_W = 8, 64, 512
DN_HEADS, DN_HD, DN_W = 4, 128, 512
CONV_K = 4
CHUNK = 64
_ROW_BLOCK = 512
QB = 256
_SB_DEAD = 104.0
ADAM_LR, ADAM_B1, ADAM_B2, ADAM_EPS, ADAM_WD, ADAM_STEP = 0.001, 0.9, 0.999, 1e-08, 0.01, 10

C_DN_QKV, C_DN_Z, C_SB_Q, C_SB_K, C_SB_V, C_SB_Z, C_MG = 0, 1536, 2048, 2560, 3072, 3584, 4096

_NN = (((1,), (0,)), ((), ()))
_NT = (((1,), (1,)), ((), ()))
_TN = (((0,), (0,)), ((), ()))
_BNN = (((2,), (1,)), ((0,), (0,)))
_BNT = (((2,), (2,)), ((0,), (0,)))
_BTN = (((1,), (1,)), ((0,), (0,)))
MESH = pl.DeviceIdType.MESH


def _sds(shape, dtype):
    return jax.ShapeDtypeStruct(shape, dtype)


def _cp(n):
    return pltpu.CompilerParams(dimension_semantics=("arbitrary",) * n, vmem_limit_bytes=_VMEM_LIMIT)


def _rb(tm, w, cb=0):
    return pl.BlockSpec((tm, w), lambda i: (i, cb))


def _fs(shape):
    nd = len(shape)
    return pl.BlockSpec(shape, lambda i: (0,) * nd)


def _dg(a, b, dims):
    return lax.dot_general(a, b, dims, preferred_element_type=F32)


def _mm(a, b, dims=_NN):
    return _dg(a.astype(_MXU_DTYPE), b.astype(_MXU_DTYPE), dims)


def _split3(x):
    hi = x.astype(BF16)
    r = x - hi.astype(F32)
    mid = r.astype(BF16)
    lo = (r - mid.astype(F32)).astype(BF16)
    return hi, mid, lo


def _mm_xl(x, const, dims=_NN):
    cb = const.astype(BF16)
    hi, mid, lo = _split3(x)
    return _dg(hi, cb, dims) + _dg(mid, cb, dims) + _dg(lo, cb, dims)


def _mm_xl2(x, const, dims=_NN):
    cb = const.astype(BF16)
    hi = x.astype(BF16)
    lo = (x - hi.astype(F32)).astype(BF16)
    return _dg(hi, cb, dims) + _dg(lo, cb, dims)


def _mm_xr(const, x, dims=_NN):
    cb = const.astype(BF16)
    hi, mid, lo = _split3(x)
    return _dg(cb, hi, dims) + _dg(cb, mid, dims) + _dg(cb, lo, dims)


def _mm3(a, b, dims=_NN):
    ah, am, _ = _split3(a)
    bh, bm, _ = _split3(b)
    return _dg(ah, bh, dims) + (_dg(ah, bm, dims) + _dg(am, bh, dims))


def _sigmoid(z):
    return 1.0 / (1.0 + jnp.exp(-z))


def _silu(z):
    return z * _sigmoid(z)


def _dsilu(z):
    s = _sigmoid(z)
    return s * (1.0 + z * (1.0 - s))


def _softplus(z):
    return jnp.maximum(z, 0.0) + jnp.log(1.0 + jnp.exp(-jnp.abs(z)))


def _iota2(shape, dim):
    return lax.broadcasted_iota(jnp.int32, shape, dim)


def _pick(n, cap, mult):
    best = None
    for t in range(mult, min(n, cap) + 1, mult):
        if n % t == 0:
            best = t
    assert best is not None, (n, cap, mult)
    return best


def _matmul(name, a, b, form, out_dtype, tm_cap=512, tn_cap=1024, tk_cap=1024, exchange=None, ex_src=(), ex_prev=()):
    if form == "nn":
        (M, K), (_, N) = a.shape, b.shape
    elif form == "nt":
        (M, K), (N, _) = a.shape, b.shape
    else:
        (K, M), (_, N) = a.shape, b.shape
    tm = _pick(M, tm_cap, 128 if form == "tn" else 8)
    tn = _pick(N, tn_cap, 128)
    tk = _pick(K, tk_cap, 128)
    nk = K // tk
    dims = {"nn": _NN, "nt": _NT, "tn": _TN}[form]
    if form == "nn":
        a_spec = pl.BlockSpec((tm, tk), lambda i, j, k: (i, k))
        b_spec = pl.BlockSpec((tk, tn), lambda i, j, k: (k, j))
    elif form == "nt":
        a_spec = pl.BlockSpec((tm, tk), lambda i, j, k: (i, k))
        b_spec = pl.BlockSpec((tn, tk), lambda i, j, k: (j, k))
    else:
        a_spec = pl.BlockSpec((tk, tm), lambda i, j, k: (k, i))
        b_spec = pl.BlockSpec((tk, tn), lambda i, j, k: (k, j))

    grid = (M // tm, N // tn, nk)
    nx, npv = len(ex_src), len(ex_prev)
    o0 = 2 + nx + npv

    def body(*refs):
        if exchange is None:
            compute(*refs)
            return
        src, xout, sems = refs[2:2 + nx], refs[o0 + 1:o0 + 1 + nx], refs[o0 + 2 + nx:]
        at = [pl.program_id(d) for d in range(3)]

        @pl.when(jnp.logical_and(jnp.logical_and(at[0] == 0, at[1] == 0), at[2] == 0))
        def _():
            exchange.start(src, xout, sems)

        compute(refs[0], refs[1], refs[o0], refs[o0 + 1 + nx])

        @pl.when(jnp.logical_and(jnp.logical_and(at[0] == grid[0] - 1, at[1] == grid[1] - 1), at[2] == nk - 1))
        def _():
            exchange.wait(src, xout, sems)

    def compute(a_ref, b_ref, o_ref, acc_ref):
        if nk == 1:
            o_ref[...] = _mm(a_ref[...], b_ref[...], dims).astype(o_ref.dtype)
            return
        k = pl.program_id(2)

        @pl.when(k == 0)
        def _():
            acc_ref[...] = _mm(a_ref[...], b_ref[...], dims)

        @pl.when(k > 0)
        def _():
            acc_ref[...] += _mm(a_ref[...], b_ref[...], dims)

        @pl.when(k == nk - 1)
        def _():
            o_ref[...] = acc_ref[...].astype(o_ref.dtype)

    hbm = pl.BlockSpec(memory_space=pl.ANY)
    outs = pl.pallas_call(
        body, name=name, grid=grid,
        in_specs=[a_spec, b_spec] + [hbm] * (nx + npv),
        out_specs=[pl.BlockSpec((tm, tn), lambda i, j, k: (i, j))] + [hbm] * nx,
        out_shape=[_sds((M, N), out_dtype)] + (exchange.out_shape if exchange else []),
        input_output_aliases={2 + nx + t: 1 + t for t in range(npv)},
        scratch_shapes=[pltpu.VMEM((tm, tn), F32)] + (exchange.scratch if exchange else []),
        compiler_params=_cp(3),
    )(a, b, *ex_src, *ex_prev)
    return outs[0] if exchange is None else (outs[0], list(outs[1:]))


def _norm_mod(x, g, scale, shift, tm=_ROW_BLOCK):
    S, D = x.shape

    def body(x_ref, g_ref, sc_ref, sh_ref, h_ref):
        xv = x_ref[...]
        r = lax.rsqrt(jnp.mean(xv * xv, axis=1, keepdims=True) + EPS)
        h_ref[...] = ((xv * r * g_ref[...]) * (1.0 + sc_ref[...]) + sh_ref[...]).astype(h_ref.dtype)

    return pl.pallas_call(
        body, name="norm_mod", grid=(S // tm,),
        in_specs=[_rb(tm, D), _fs((1, D)), _fs((1, D)), _fs((1, D))],
        out_specs=_rb(tm, D), out_shape=_sds((S, D), _MXU_DTYPE), compiler_params=_cp(1),
    )(x, g, scale, shift)


def _norm_mod_bwd(x, dh, dxn, g, scale, tm=_ROW_BLOCK, exchange=None, ex_src=(), ex_prev=()):
    S, D = x.shape
    nx = len(ex_src)

    def body(*refs):
        if exchange is None:
            compute(*refs)
            return
        src, xout, sems = refs[5:5 + nx], refs[9 + 2 * nx:9 + 3 * nx], refs[9 + 3 * nx:]

        @pl.when(pl.program_id(0) == 0)
        def _():
            exchange.start(src, xout, sems)

        compute(*refs[:5], *refs[5 + 2 * nx:9 + 2 * nx])

        @pl.when(pl.program_id(0) == S // tm - 1)
        def _():
            exchange.wait(src, xout, sems)

    def compute(x_ref, dh_ref, dxn_ref, g_ref, sc_ref, dx_ref, dsh_ref, dsc_ref, dg_ref):
        @pl.when(pl.program_id(0) == 0)
        def _():
            dsh_ref[...] = jnp.zeros_like(dsh_ref)
            dsc_ref[...] = jnp.zeros_like(dsc_ref)
            dg_ref[...] = jnp.zeros_like(dg_ref)

        xv, dhv, gv = x_ref[...], dh_ref[...], g_ref[...]
        r = lax.rsqrt(jnp.mean(xv * xv, axis=1, keepdims=True) + EPS)
        xh = xv * r
        one_sc = 1.0 + sc_ref[...]
        dsh_ref[...] += jnp.sum(dhv, axis=0, keepdims=True)
        dsc_ref[...] += jnp.sum(dhv * xh * gv, axis=0, keepdims=True)
        dg_ref[...] += jnp.sum(dhv * one_sc * xh, axis=0, keepdims=True)
        dxh = dhv * (gv * one_sc)
        dx_ref[...] = r * (dxh - xh * jnp.mean(dxh * xh, axis=1, keepdims=True)) + dxn_ref[...]

    hbm = pl.BlockSpec(memory_space=pl.ANY)
    outs = pl.pallas_call(
        body, name="norm_mod_bwd", grid=(S // tm,),
        in_specs=[_rb(tm, D), _rb(tm, D), _rb(tm, D), _fs((1, D)), _fs((1, D))] + [hbm] * (2 * nx),
        out_specs=[_rb(tm, D), _fs((1, D)), _fs((1, D)), _fs((1, D))] + [hbm] * nx,
        out_shape=[_sds((S, D), F32)] + [_sds((1, D), F32)] * 3 + (exchange.out_shape if exchange else []),
        input_output_aliases={5 + nx + t: 4 + t for t in range(nx)},
        scratch_shapes=exchange.scratch if exchange else [], compiler_params=_cp(1),
    )(x, dh, dxn, g, scale, *ex_src, *ex_prev)
    return outs[:4], list(outs[4:])


def _head_sum_matrix():
    r = jnp.arange(SB_W)
    return (r[:, None] // SB_HD == r[None, :] // SB_HD).astype(BF16)


def _sb_prep(p, gq_t, gk_t, tm=_ROW_BLOCK):
    S = p.shape[0]
    bd = _head_sum_matrix()

    def body(q_ref, k_ref, gq_ref, gk_ref, bd_ref, qn_ref, kn_ref):
        for src, g_ref, dst in ((q_ref, gq_ref, qn_ref), (k_ref, gk_ref, kn_ref)):
            v = src[...]
            ms = _mm_xl(v * v, bd_ref[...]) * (1.0 / SB_HD)
            dst[...] = (v * lax.rsqrt(ms + EPS) * g_ref[...]).astype(dst.dtype)

    return pl.pallas_call(
        body, name="sb_prep", grid=(S // tm,),
        in_specs=[_rb(tm, SB_W, C_SB_Q // SB_W), _rb(tm, SB_W, C_SB_K // SB_W),
                  _fs((1, SB_W)), _fs((1, SB_W)), _fs((SB_W, SB_W))],
        out_specs=[_rb(tm, SB_W), _rb(tm, SB_W)],
        out_shape=[_sds((S, SB_W), _MXU_DTYPE)] * 2, compiler_params=_cp(1),
    )(p, p, gq_t, gk_t, bd)


def _sb_prep_bwd(p, dqn, dkn, gq_t, gk_t, tm=_ROW_BLOCK):
    S = p.shape[0]
    bd = _head_sum_matrix()

    def body(q_ref, k_ref, dqn_ref, dkn_ref, gq_ref, gk_ref, bd_ref, dq_ref, dk_ref, dgq_ref, dgk_ref):
        @pl.when(pl.program_id(0) == 0)
        def _():
            dgq_ref[...] = jnp.zeros_like(dgq_ref)
            dgk_ref[...] = jnp.zeros_like(dgk_ref)

        for src, dn_ref, g_ref, dst, dg_ref in ((q_ref, dqn_ref, gq_ref, dq_ref, dgq_ref),
                                                (k_ref, dkn_ref, gk_ref, dk_ref, dgk_ref)):
            v, dn = src[...], dn_ref[...]
            r = lax.rsqrt(_mm_xl(v * v, bd_ref[...]) * (1.0 / SB_HD) + EPS)
            vh = v * r
            dg_ref[...] += jnp.sum(dn * vh, axis=0, keepdims=True)
            dvh = dn * g_ref[...]
            m = _mm_xl(dvh * vh, bd_ref[...]) * (1.0 / SB_HD)
            dst[...] = (r * (dvh - vh * m)).astype(dst.dtype)

    return pl.pallas_call(
        body, name="sb_prep_bwd", grid=(S // tm,),
        in_specs=[_rb(tm, SB_W, C_SB_Q // SB_W), _rb(tm, SB_W, C_SB_K // SB_W), _rb(tm, SB_W), _rb(tm, SB_W),
                  _fs((1, SB_W)), _fs((1, SB_W)), _fs((SB_W, SB_W))],
        out_specs=[_rb(tm, SB_W), _rb(tm, SB_W), _fs((1, SB_W)), _fs((1, SB_W))],
        out_shape=[_sds((S, SB_W), _MXU_DTYPE)] * 2 + [_sds((1, SB_W), F32)] * 2, compiler_params=_cp(1),
    )(p, p, dqn, dkn, gq_t, gk_t, bd)


def _sb_consts():
    r, c = _iota2((QB, QB), 0), _iota2((QB, QB), 1)
    lane = _iota2((1, LANES), 1)
    return r, c, lane


def _sb_fwd(qn, kn, p, gather=None, g_src=()):
    S = qn.shape[0]
    scale = 1.0 / math.sqrt(SB_HD)
    grid = (SB_W // LANES, S // QB)
    nx = len(g_src)

    def body(*refs):
        if gather is None:
            compute(*refs)
            return
        src, gout, sems = refs[3:3 + nx], refs[6 + nx:6 + 2 * nx], refs[6 + 2 * nx:]
        hp, i = pl.program_id(0), pl.program_id(1)

        @pl.when(jnp.logical_and(hp == 0, i == 0))
        def _():
            gather.start(src, gout, sems)

        compute(*refs[:3], *refs[3 + nx:6 + nx])

        @pl.when(jnp.logical_and(hp == grid[0] - 1, i == grid[1] - 1))
        def _():
            gather.wait(src, gout, sems)

    def compute(q_ref, k_ref, v_ref, o_ref, tot_ref, nb_ref):
        i = pl.program_id(1)
        r, c, lane = _sb_consts()
        u_gt = (r > c).astype(BF16)
        strict = jnp.concatenate([c < r, c < r], axis=0)
        q = q_ref[...]
        mask0 = (lane // SB_HD) == 0
        zero = jnp.zeros_like(q)
        qh = jnp.concatenate([jnp.where(mask0, q, zero), jnp.where(mask0, zero, q)], axis=0)

        def block(off, carry, diagonal):
            o, run = carry
            kj = k_ref[pl.ds(off, QB), :]
            vj = v_ref[pl.ds(off, QB), :].astype(_MXU_DTYPE)
            z = _mm(qh, kj, _NT) * scale
            sp = _softplus(z)
            sp_m = jnp.where(strict, sp, 0.0) if diagonal else sp
            later = _mm_xl2(sp_m, u_gt)
            w = jnp.exp((z - sp) - later - run)
            if diagonal:
                w = jnp.where(strict, w, 0.0)
            return o + _mm(w, vj), run + jnp.sum(sp_m, axis=1, keepdims=True)

        init = (jnp.zeros((2 * QB, LANES), F32), jnp.zeros((2 * QB, 1), F32))
        carry = block(pl.multiple_of(i * QB, QB), init, True)
        st = lax.while_loop(
            lambda st: jnp.logical_and(st[0] <= i, jnp.min(st[2]) < _SB_DEAD),
            lambda st: (st[0] + 1,) + block(pl.multiple_of((i - st[0]) * QB, QB), st[1:], False),
            (jnp.int32(1),) + carry)
        o_ref[...] = jnp.where(mask0, st[1][:QB], st[1][QB:])
        tot_ref[...] = jnp.where(mask0, st[2][:QB], st[2][QB:])
        nb_ref[...] = jnp.zeros((8, LANES), F32) + st[0].astype(F32)

    blk = pl.BlockSpec((QB, LANES), lambda hp, i: (i, hp))
    hbm = pl.BlockSpec(memory_space=pl.ANY)
    outs = pl.pallas_call(
        body, name="sb_fwd", grid=grid,
        in_specs=[blk, pl.BlockSpec((S, LANES), lambda hp, i: (0, hp)),
                  pl.BlockSpec((S, LANES), lambda hp, i: (0, C_SB_V // LANES + hp))] + [hbm] * nx,
        out_specs=[blk, blk, pl.BlockSpec((8, LANES), lambda hp, i: (i, hp))] + [hbm] * nx,
        out_shape=[_sds((S, SB_W), F32), _sds((S, SB_W), F32), _sds((8 * S // QB, SB_W), F32)]
        + (gather.out_shape if gather else []),
        scratch_shapes=gather.scratch if gather else [], compiler_params=_cp(2),
    )(qn, kn, p, *g_src)
    return outs[:3], list(outs[3:])


def _sb_bwd(qn, kn, p, do, tot, nblocks, exchange=None, ex_src=(), early=None, early_src=()):
    S = qn.shape[0]
    scale = 1.0 / math.sqrt(SB_HD)
    grid = (SB_W // LANES, S // QB)
    nx, ne = len(ex_src), len(early_src)
    NI = 6

    def body(*refs):
        if exchange is None:
            compute(*refs)
            return
        src, src2 = refs[NI:NI + nx], refs[NI + nx:NI + nx + ne]
        o0 = NI + nx + ne
        xout, sems, sems2 = refs[o0 + 3:o0 + 3 + nx], refs[o0 + 3 + nx:o0 + 6 + nx], refs[o0 + 6 + nx:]
        hp, i = pl.program_id(0), pl.program_id(1)

        @pl.when(jnp.logical_and(hp == 0, i == 0))
        def _():
            exchange.start(src, xout, sems)
            if early is not None:
                early.start(src2, xout[nx - ne:], sems2)

        compute(*refs[:NI], *refs[o0:o0 + 3])

        @pl.when(jnp.logical_and(hp == grid[0] - 1, i == grid[1] - 1))
        def _():
            exchange.wait(src, xout, sems)
            if early is not None:
                early.wait(src2, xout[nx - ne:], sems2)

    def compute(q_ref, k_ref, v_ref, do_ref, tot_ref, nb_ref, dq_ref, dk_ref, dv_ref):
        i = pl.program_id(1)

        @pl.when(i == 0)
        def _():
            dk_ref[...] = jnp.zeros_like(dk_ref)
            dv_ref[...] = jnp.zeros_like(dv_ref)

        r, c, lane = _sb_consts()
        u_le = (r <= c).astype(BF16)
        u_lt = (r < c).astype(BF16)
        strict = jnp.concatenate([c < r, c < r], axis=0)
        q = q_ref[...]
        do = do_ref[...].astype(_MXU_DTYPE)
        mask0 = (lane // SB_HD) == 0
        zero, zero_do = jnp.zeros_like(q), jnp.zeros_like(do)
        qh = jnp.concatenate([jnp.where(mask0, q, zero), jnp.where(mask0, zero, q)], axis=0)
        doh = jnp.concatenate([jnp.where(mask0, do, zero_do), jnp.where(mask0, zero_do, do)], axis=0)

        tot_pair = tot_ref[...]
        tot = jnp.concatenate([jnp.max(jnp.where(mask0, tot_pair, 0.0), axis=1, keepdims=True),
                               jnp.max(jnp.where(mask0, 0.0, tot_pair), axis=1, keepdims=True)], axis=0)
        nb = jnp.clip(jnp.max(nb_ref[...]).astype(jnp.int32), 1, i + 1)
        first = i + 1 - nb

        def block(off, carry, diagonal):
            dq, pre_sp, pre_e = carry
            kj = k_ref[pl.ds(off, QB), :]
            vj = v_ref[pl.ds(off, QB), :].astype(_MXU_DTYPE)
            z = _mm(qh, kj, _NT) * scale
            sp = _softplus(z)
            a = z - sp
            sp_m = jnp.where(strict, sp, 0.0) if diagonal else sp
            incl = _mm_xl2(sp_m, u_le)
            w = jnp.exp(a - ((tot - pre_sp) - incl))
            if diagonal:
                w = jnp.where(strict, w, 0.0)
            e = w * _mm(doh, vj, _NT)
            db = pre_e + _mm_xl2(e, u_lt)
            dz = (e - jnp.exp(a) * (e + db)) * scale
            if diagonal:
                dz = jnp.where(strict, dz, 0.0)
            dk_ref[pl.ds(off, QB), :] += _mm(dz, qh, _TN)
            dv_ref[pl.ds(off, QB), :] += _mm(w, doh, _TN)
            return (dq + _mm(dz, kj), pre_sp + jnp.sum(sp_m, axis=1, keepdims=True),
                    pre_e + jnp.sum(e, axis=1, keepdims=True))

        zero_col = jnp.zeros((2 * QB, 1), F32)
        init = (jnp.zeros((2 * QB, LANES), F32), zero_col, zero_col)
        carry = lax.fori_loop(first, i, lambda j, cr: block(pl.multiple_of(j * QB, QB), cr, False), init)
        carry = block(pl.multiple_of(i * QB, QB), carry, True)
        dq_ref[...] = jnp.where(mask0, carry[0][:QB], carry[0][QB:])

    blk = pl.BlockSpec((QB, LANES), lambda hp, i: (i, hp))
    full = pl.BlockSpec((S, LANES), lambda hp, i: (0, hp))
    hbm = pl.BlockSpec(memory_space=pl.ANY)
    outs = pl.pallas_call(
        body, name="sb_bwd", grid=grid,
        in_specs=[blk, full, pl.BlockSpec((S, LANES), lambda hp, i: (0, C_SB_V // LANES + hp)), blk, blk,
                  pl.BlockSpec((8, LANES), lambda hp, i: (i, hp))] + [hbm] * (nx + ne),
        out_specs=[blk, full, full] + [hbm] * nx,
        out_shape=[_sds((S, SB_W), F32)] * 3 + (exchange.out_shape if exchange else []),
        scratch_shapes=(exchange.scratch if exchange else []) + (early.scratch if early else []),
        compiler_params=_cp(2),
    )(qn, kn, p, do, tot, nblocks, *ex_src, *early_src)
    return outs[:3], outs[3:]


def _dn_prep(p, conv_w, a_row, dtb_row, tm=_ROW_BLOCK):
    S = p.shape[0]
    W3 = 3 * DN_W
    nhalo = tm // 8

    def body(x_ref, halo_ref, w_ref, ba_ref, a_ref, dtb_ref, qkv_ref, bb_ref, gc_ref, gl_ref):
        i = pl.program_id(0)
        halo = jnp.where(i > 0, halo_ref[...], 0.0)
        xf = jnp.concatenate([halo, x_ref[...]], axis=0)
        acc = jnp.zeros((tm, W3), F32)
        for k in range(CONV_K):
            sh = CONV_K - 1 - k
            xs = xf if sh == 0 else pltpu.roll(xf, sh, 0)
            acc = acc + xs[8:, :] * w_ref[k:k + 1, :]
        s = _silu(acc)
        for gi in range(2 * DN_HEADS):
            sl = slice(gi * LANES, (gi + 1) * LANES)
            sg = s[:, sl]
            rinv = lax.rsqrt(jnp.sum(sg * sg, axis=1, keepdims=True) + EPS)
            qkv_ref[:, sl] = sg * rinv * (DN_HD ** -0.5 if gi < DN_HEADS else 1.0)
        qkv_ref[:, 2 * DN_W:] = s[:, 2 * DN_W:]

        ba = ba_ref[...]
        beta = _sigmoid(ba)
        g = -jnp.exp(a_ref[...]) * _softplus(ba + dtb_ref[...])
        lr, lc = _iota2((LANES, DN_W), 0), _iota2((LANES, DN_W), 1)
        sel_b = (lr == lc // LANES).astype(BF16)
        sel_g = (lr == lc // LANES + DN_HEADS).astype(BF16)
        bb_ref[...] = _mm_xl(beta, sel_b)
        graw = _mm_xl(g, sel_g)
        rr, cc = _iota2((tm, tm), 0), _iota2((tm, tm), 1)
        tri = jnp.logical_and(rr >= cc, rr // CHUNK == cc // CHUNK).astype(BF16)
        gc = _mm_xr(tri, graw)
        last = (cc == (rr // CHUNK) * CHUNK + (CHUNK - 1)).astype(BF16)
        gc_ref[...] = gc
        gl_ref[...] = _mm_xr(last, gc)

    return pl.pallas_call(
        body, name="dn_prep", grid=(S // tm,),
        in_specs=[_rb(tm, W3, 0), pl.BlockSpec((8, W3), lambda i: (jnp.maximum(i * nhalo - 1, 0), 0)),
                  _fs((CONV_K, W3)), _rb(tm, LANES, (p.shape[1] - LANES) // LANES),
                  _fs((1, LANES)), _fs((1, LANES))],
        out_specs=[_rb(tm, W3), _rb(tm, DN_W), _rb(tm, DN_W), _rb(tm, DN_W)],
        out_shape=[_sds((S, W3), F32)] + [_sds((S, DN_W), F32)] * 3, compiler_params=_cp(1),
    )(p, p, conv_w, p, a_row, dtb_row)


def _heads(ref, base=0):
    return jnp.stack([ref[:, base + h * LANES:base + (h + 1) * LANES] for h in range(DN_HEADS)])


def _per_head(const):
    return jnp.broadcast_to(const[None], (DN_HEADS,) + const.shape)


def _dn_chunk_terms(q, k, v, beta, gc, gl):
    r, c = _iota2((CHUNK, CHUNK), 0), _iota2((CHUNK, CHUNK), 1)
    tril, strict = r >= c, r > c
    gcol = _mm_xl(gc, _per_head(jnp.full((LANES, CHUNK), 1.0 / LANES, F32)), _BNN)
    grow = _mm_xr(_per_head(jnp.full((CHUNK, LANES), 1.0 / LANES, F32)), gc, _BNT)
    dec = jnp.where(tril, jnp.exp(jnp.where(tril, gcol - grow, 0.0)), 0.0)
    gam = jnp.exp(gc)
    dlt = jnp.exp(gl - gc)
    kb, vb = k * beta, v * beta
    pm = _mm(kb, k, _BNT)
    qk = _mm(q, k, _BNT)
    m = jnp.where(strict, pm * dec, 0.0)
    a = jnp.where(tril, qk * dec, 0.0)
    return dict(tril=tril, strict=strict, dec=dec, gam=gam, dlt=dlt, kb=kb, vb=vb, m=m, a=a)


def _dn_fwd(qkv, bb, gcb, glb, gather=None, g_src=()):
    S = qkv.shape[0]
    N = S // CHUNK
    nx = len(g_src)

    def body(*refs):
        if gather is None:
            compute(*refs)
            return
        src, gout, sems = refs[4:4 + nx], refs[7 + nx:7 + 2 * nx], refs[8 + 2 * nx:]

        @pl.when(pl.program_id(0) == 0)
        def _():
            gather.start(src, gout, sems)

        compute(*refs[:4], *refs[4 + nx:7 + nx], refs[7 + 2 * nx])

        @pl.when(pl.program_id(0) == N - 1)
        def _():
            gather.wait(src, gout, sems)

    def compute(qkv_ref, bb_ref, gc_ref, gl_ref, o_ref, t_ref, sall_ref, s_scr):
        @pl.when(pl.program_id(0) == 0)
        def _():
            s_scr[...] = jnp.zeros_like(s_scr)

        r, c = _iota2((CHUNK, CHUNK), 0), _iota2((CHUNK, CHUNK), 1)
        eye = (r == c).astype(F32)
        q, k, v = _heads(qkv_ref), _heads(qkv_ref, DN_W), _heads(qkv_ref, 2 * DN_W)
        beta, gc, gl = _heads(bb_ref), _heads(gc_ref), _heads(gl_ref)
        s_prev = s_scr[...]
        sall_ref[0] = s_prev.astype(sall_ref.dtype)
        s0 = s_prev.astype(sall_ref.dtype).astype(F32)
        t = _dn_chunk_terms(q, k, v, beta, gc, gl)
        pw = -t["m"]
        tinv = eye + pw
        for _ in range(5):
            pw = _mm3(pw, pw, _BNN)
            tinv = tinv + _mm3(tinv, pw, _BNN)
        t_ref[...] = tinv
        u = _mm3(tinv, t["vb"], _BNN)
        w = _mm3(tinv, t["kb"] * t["gam"], _BNN)
        vn = u - _mm(w, s0, _BNN)
        o = _mm(q * t["gam"], s0, _BNN) + _mm(t["a"], vn, _BNN)
        for h in range(DN_HEADS):
            o_ref[:, h * LANES:(h + 1) * LANES] = o[h]
        egl = jnp.exp(jnp.concatenate([gl, gl], axis=1))
        s_scr[...] = s_prev * egl + _mm(k * t["dlt"], vn, _BTN)

    hbm = pl.BlockSpec(memory_space=pl.ANY)
    outs = pl.pallas_call(
        body, name="dn_fwd", grid=(N,),
        in_specs=[_rb(CHUNK, 3 * DN_W), _rb(CHUNK, DN_W), _rb(CHUNK, DN_W), _rb(CHUNK, DN_W)] + [hbm] * nx,
        out_specs=[_rb(CHUNK, DN_W), pl.BlockSpec((DN_HEADS, CHUNK, CHUNK), lambda n: (0, n, 0)),
                   pl.BlockSpec((1, DN_HEADS, DN_HD, DN_HD), lambda n: (n, 0, 0, 0))] + [hbm] * nx,
        out_shape=[_sds((S, DN_W), F32), _sds((DN_HEADS, S, CHUNK), F32),
                   _sds((N, DN_HEADS, DN_HD, DN_HD), _MXU_DTYPE)] + (gather.out_shape if gather else []),
        scratch_shapes=[pltpu.VMEM((DN_HEADS, DN_HD, DN_HD), F32)] + (gather.scratch if gather else []),
        compiler_params=_cp(1),
    )(qkv, bb, gcb, glb, *g_src)
    return outs[:3], outs[3:]


def _dn_bwd(qkv, bb, gcb, glb, tinv_all, sall, do, exchange=None, ex_src=(), early=None, early_src=()):
    S = qkv.shape[0]
    N = S // CHUNK
    nx = len(ex_src)
    NI = 7

    def body(*refs):
        if exchange is None:
            compute(*refs)
            return
        src, src2 = refs[NI:NI + nx], refs[NI + nx:NI + 2 * nx]
        o0 = NI + 2 * nx
        xout, ds_scr = refs[o0 + 3:o0 + 3 + nx], refs[o0 + 3 + nx]
        sems, sems2 = refs[o0 + 4 + nx:o0 + 7 + nx], refs[o0 + 7 + nx:]

        @pl.when(pl.program_id(0) == 0)
        def _():
            exchange.start(src, xout, sems)
            early.start(src2, xout, sems2)

        compute(*refs[:NI], *refs[o0:o0 + 3], ds_scr)

        @pl.when(pl.program_id(0) == N - 1)
        def _():
            exchange.wait(src, xout, sems)
            early.wait(src2, xout, sems2)

    def compute(qkv_ref, bb_ref, gc_ref, gl_ref, t_ref, sall_ref, do_ref, dqkv_ref, dbb_ref, dg_ref, ds_scr):
        @pl.when(pl.program_id(0) == 0)
        def _():
            ds_scr[...] = jnp.zeros_like(ds_scr)

        r, c = _iota2((CHUNK, CHUNK), 0), _iota2((CHUNK, CHUNK), 1)
        eye = (r == c).astype(F32)
        u_ge = (c >= r).astype(F32)
        last_row = _iota2((CHUNK, LANES), 0) == CHUNK - 1
        eye_h, u_ge_h = _per_head(eye), _per_head(u_ge)
        q, k, v = _heads(qkv_ref), _heads(qkv_ref, DN_W), _heads(qkv_ref, 2 * DN_W)
        beta, gc, gl = _heads(bb_ref), _heads(gc_ref), _heads(gl_ref)
        tinv = t_ref[...]
        s0 = sall_ref[0].astype(F32)
        do = _heads(do_ref)
        ds1 = ds_scr[...]
        t = _dn_chunk_terms(q, k, v, beta, gc, gl)
        gam, dlt, kb, vb, dec = t["gam"], t["dlt"], t["kb"], t["vb"], t["dec"]
        kbg = kb * gam
        u = _mm3(tinv, vb, _BNN)
        w = _mm3(tinv, kbg, _BNN)
        vn = u - _mm(w, s0, _BNN)
        qg, kd = q * gam, k * dlt
        egl = jnp.exp(gl)
        egl2 = jnp.concatenate([egl, egl], axis=1)

        dvn = _mm(t["a"], do, _BTN) + _mm(kd, ds1, _BNN)
        da = jnp.where(t["tril"], _mm(do, vn, _BNT), 0.0)
        dqg = _mm(do, s0, _BNT)
        dkd = _mm(vn, ds1, _BNT)
        dw = -_mm(dvn, s0, _BNT)
        ds_scr[...] = _mm(qg, do, _BTN) + egl2 * ds1 - _mm(w, dvn, _BTN)
        tt = _mm_xr(eye_h, tinv, _BNT)
        dvb = _mm3(tt, dvn, _BNN)
        dkbg = _mm3(tt, dw, _BNN)
        dm = -jnp.where(t["strict"], _mm(dvb, u, _BNT) + _mm(dkbg, w, _BNT), 0.0)
        dpm = dm * dec
        dqk = da * dec
        dkb = dkbg * gam + _mm(dpm, k, _BNN)
        dk = dkd * dlt + _mm(dpm, kb, _BTN) + _mm(dqk, q, _BTN) + dkb * beta
        dq = dqg * gam + _mm(dqk, k, _BNN)
        dv = dvb * beta
        dbeta = jnp.sum(dkb * k, axis=2, keepdims=True) + jnp.sum(dvb * v, axis=2, keepdims=True)
        dgam = jnp.sum(dqg * q, axis=2, keepdims=True) + jnp.sum(dkbg * kb, axis=2, keepdims=True)
        ddlt = jnp.sum(dkd * k, axis=2, keepdims=True)
        xm = dm * t["m"] + da * t["a"]
        xt = _mm_xr(eye_h, xm, _BNT)
        dgc = (dgam * gam - ddlt * dlt + jnp.sum(xm, axis=2, keepdims=True) - jnp.sum(xt, axis=2, keepdims=True))
        dgl = jnp.sum(ddlt * dlt, axis=1, keepdims=True) + jnp.sum(
            jnp.sum(ds1 * s0, axis=2, keepdims=True), axis=1, keepdims=True) * jnp.max(egl, axis=1, keepdims=True)
        dgc = dgc + jnp.where(last_row, dgl, 0.0)
        dg = _mm_xr(u_ge_h, dgc, _BNN)
        for h in range(DN_HEADS):
            sl = slice(h * LANES, (h + 1) * LANES)
            dqkv_ref[:, sl] = dq[h]
            dqkv_ref[:, DN_W + h * LANES:DN_W + (h + 1) * LANES] = dk[h]
            dqkv_ref[:, 2 * DN_W + h * LANES:2 * DN_W + (h + 1) * LANES] = dv[h]
            dbb_ref[:, sl] = jnp.broadcast_to(dbeta[h], (CHUNK, LANES))
            dg_ref[:, sl] = dg[h]

    rev = lambda w: pl.BlockSpec((CHUNK, w), lambda n: (N - 1 - n, 0))
    hbm = pl.BlockSpec(memory_space=pl.ANY)
    outs = pl.pallas_call(
        body, name="dn_bwd", grid=(N,),
        in_specs=[rev(3 * DN_W), rev(DN_W), rev(DN_W), rev(DN_W),
                  pl.BlockSpec((DN_HEADS, CHUNK, CHUNK), lambda n: (0, N - 1 - n, 0)),
                  pl.BlockSpec((1, DN_HEADS, DN_HD, DN_HD), lambda n: (N - 1 - n, 0, 0, 0)), rev(DN_W)]
        + [hbm] * (2 * nx),
        out_specs=[rev(3 * DN_W), rev(DN_W), rev(DN_W)] + [hbm] * nx,
        out_shape=[_sds((S, 3 * DN_W), F32), _sds((S, DN_W), F32), _sds((S, DN_W), F32)]
        + (exchange.out_shape if exchange else []),
        scratch_shapes=[pltpu.VMEM((DN_HEADS, DN_HD, DN_HD), F32)]
        + (exchange.scratch + early.scratch if exchange else []),
        compiler_params=_cp(1),
    )(qkv, bb, gcb, glb, tinv_all, sall, do, *ex_src, *early_src)
    return outs[:3], list(outs[3:])


def _dn_prep_bwd_a(p, dqkv, dbb, dgb, conv_w, a_row, dtb_row, tm=_ROW_BLOCK):
    S, PC = p.shape
    W3 = 3 * DN_W
    nhalo = tm // 8

    def body(x_ref, halo_ref, w_ref, ba_ref, a_ref, dtb_ref, dqkv_ref, dbb_ref, dgb_ref,
             dc_ref, dba_ref, dal_ref, ddt_ref):
        i = pl.program_id(0)

        @pl.when(i == 0)
        def _():
            dal_ref[...] = jnp.zeros_like(dal_ref)
            ddt_ref[...] = jnp.zeros_like(ddt_ref)

        halo = jnp.where(i > 0, halo_ref[...], 0.0)
        xf = jnp.concatenate([halo, x_ref[...]], axis=0)
        acc = jnp.zeros((tm, W3), F32)
        for k in range(CONV_K):
            sh = CONV_K - 1 - k
            xs = xf if sh == 0 else pltpu.roll(xf, sh, 0)
            acc = acc + xs[8:, :] * w_ref[k:k + 1, :]
        s = _silu(acc)
        ds_act = _dsilu(acc)
        for gi in range(2 * DN_HEADS):
            sl = slice(gi * LANES, (gi + 1) * LANES)
            sg = s[:, sl]
            rinv = lax.rsqrt(jnp.sum(sg * sg, axis=1, keepdims=True) + EPS)
            nh = sg * rinv
            dn = dqkv_ref[:, sl] * (DN_HD ** -0.5 if gi < DN_HEADS else 1.0)
            dsg = rinv * (dn - nh * jnp.sum(dn * nh, axis=1, keepdims=True))
            dc_ref[:, sl] = dsg * ds_act[:, sl]
        dc_ref[:, 2 * DN_W:] = dqkv_ref[:, 2 * DN_W:] * ds_act[:, 2 * DN_W:]

        ba = ba_ref[...]
        beta = _sigmoid(ba)
        ea = jnp.exp(a_ref[...])
        pre = ba + dtb_ref[...]
        g = -ea * _softplus(pre)
        lr, lc = _iota2((DN_W, LANES), 0), _iota2((DN_W, LANES), 1)
        pick_b = jnp.where(lc == lr // LANES, 1.0 / LANES, 0.0)
        pick_g = jnp.where(lc == lr // LANES + DN_HEADS, 1.0 / LANES, 0.0)
        dbeta = _mm_xl(dbb_ref[...], pick_b)
        dg = _mm_xl(dgb_ref[...], pick_g)
        lane = _iota2((1, LANES), 1)
        da = dg * (-ea) * _sigmoid(pre)
        dba_ref[...] = jnp.where(lane < DN_HEADS, dbeta * beta * (1.0 - beta),
                                 jnp.where(lane < 2 * DN_HEADS, da, 0.0)).astype(dba_ref.dtype)
        dal_ref[...] += jnp.sum(dg * g, axis=0, keepdims=True)
        ddt_ref[...] += jnp.sum(da, axis=0, keepdims=True)

    return pl.pallas_call(
        body, name="dn_prep_bwd_a", grid=(S // tm,),
        in_specs=[_rb(tm, W3, 0), pl.BlockSpec((8, W3), lambda i: (jnp.maximum(i * nhalo - 1, 0), 0)),
                  _fs((CONV_K, W3)), _rb(tm, LANES, (PC - LANES) // LANES), _fs((1, LANES)), _fs((1, LANES)),
                  _rb(tm, W3), _rb(tm, DN_W), _rb(tm, DN_W)],
        out_specs=[_rb(tm, W3), _rb(tm, LANES), _fs((1, LANES)), _fs((1, LANES))],
        out_shape=[_sds((S, W3), F32), _sds((S, LANES), _MXU_DTYPE), _sds((1, LANES), F32), _sds((1, LANES), F32)],
        compiler_params=_cp(1),
    )(p, p, conv_w, p, a_row, dtb_row, dqkv, dbb, dgb)


def _dn_prep_bwd_b(p, dc, conv_w, tm=_ROW_BLOCK):
    S = p.shape[0]
    W3 = 3 * DN_W
    nhalo = tm // 8
    nblk = S // tm

    def body(x_ref, xh_ref, dc_ref, dch_ref, w_ref, dx_ref, dw_ref):
        i = pl.program_id(0)

        @pl.when(i == 0)
        def _():
            dw_ref[...] = jnp.zeros_like(dw_ref)

        dcv = dc_ref[...]
        xf = jnp.concatenate([jnp.where(i > 0, xh_ref[...], 0.0), x_ref[...]], axis=0)
        df = jnp.concatenate([dcv, jnp.where(i < nblk - 1, dch_ref[...], 0.0)], axis=0)
        acc = jnp.zeros((tm, W3), F32)
        for k in range(CONV_K):
            sh = CONV_K - 1 - k
            xs = xf if sh == 0 else pltpu.roll(xf, sh, 0)
            dw_ref[k:k + 1, :] += jnp.sum(dcv * xs[8:, :], axis=0, keepdims=True)
            ds = df if sh == 0 else pltpu.roll(df, tm + 8 - sh, 0)
            acc = acc + ds[:tm, :] * w_ref[k:k + 1, :]
        dx_ref[...] = acc.astype(dx_ref.dtype)

    return pl.pallas_call(
        body, name="dn_prep_bwd_b", grid=(nblk,),
        in_specs=[_rb(tm, W3, 0), pl.BlockSpec((8, W3), lambda i: (jnp.maximum(i * nhalo - 1, 0), 0)),
                  _rb(tm, W3), pl.BlockSpec((8, W3), lambda i: (jnp.minimum((i + 1) * nhalo, S // 8 - 1), 0)),
                  _fs((CONV_K, W3))],
        out_specs=[_rb(tm, W3), _fs((CONV_K, W3))],
        out_shape=[_sds((S, W3), _MXU_DTYPE), _sds((CONV_K, W3), F32)], compiler_params=_cp(1),
    )(p, p, dc, dc, conv_w)


def _gate(o_att, o_dn, p, gn, tm=_ROW_BLOCK):
    S = p.shape[0]

    def body(oa_ref, zs_ref, od_ref, zd_ref, gn_ref, osb_ref, odn_ref):
        osb_ref[...] = (oa_ref[...] * _silu(zs_ref[...])).astype(osb_ref.dtype)
        for h in range(DN_HEADS):
            sl = slice(h * LANES, (h + 1) * LANES)
            o = od_ref[:, sl]
            r = lax.rsqrt(jnp.mean(o * o, axis=1, keepdims=True) + EPS)
            odn_ref[:, sl] = (o * r * gn_ref[...] * _silu(zd_ref[:, sl])).astype(odn_ref.dtype)

    return pl.pallas_call(
        body, name="gate", grid=(S // tm,),
        in_specs=[_rb(tm, SB_W), _rb(tm, SB_W, C_SB_Z // SB_W), _rb(tm, DN_W), _rb(tm, DN_W, C_DN_Z // DN_W),
                  _fs((1, LANES))],
        out_specs=[_rb(tm, SB_W), _rb(tm, DN_W)],
        out_shape=[_sds((S, SB_W), _MXU_DTYPE), _sds((S, DN_W), _MXU_DTYPE)], compiler_params=_cp(1),
    )(o_att, p, o_dn, p, gn)


def _gate_bwd(db_sb, db_dn, wb_sb, wb_dn, o_att, o_dn, p, gn, tm=_ROW_BLOCK):
    S = p.shape[0]
    D = db_sb.shape[1]

    def body(dbs_ref, dbd_ref, ws_ref, wd_ref, oa_ref, zs_ref, od_ref, zd_ref, gn_ref,
             doa_ref, dzs_ref, dod_ref, dzd_ref, dgn_ref):
        @pl.when(pl.program_id(0) == 0)
        def _():
            dgn_ref[...] = jnp.zeros_like(dgn_ref)

        do_sb = _mm(dbs_ref[...], ws_ref[...], _NT)
        zs = zs_ref[...]
        doa_ref[...] = do_sb * _silu(zs)
        dzs_ref[...] = (do_sb * oa_ref[...] * _dsilu(zs)).astype(dzs_ref.dtype)
        do_dnn = _mm(dbd_ref[...], wd_ref[...], _NT)
        gnv = gn_ref[...]
        for h in range(DN_HEADS):
            sl = slice(h * LANES, (h + 1) * LANES)
            o, z, dout = od_ref[:, sl], zd_ref[:, sl], do_dnn[:, sl]
            r = lax.rsqrt(jnp.mean(o * o, axis=1, keepdims=True) + EPS)
            oh = o * r
            sz = _silu(z)
            dzd_ref[:, sl] = (dout * oh * gnv * _dsilu(z)).astype(dzd_ref.dtype)
            dgn_ref[...] += jnp.sum(dout * sz * oh, axis=0, keepdims=True)
            doh = dout * gnv * sz
            dod_ref[:, sl] = r * (doh - oh * jnp.mean(doh * oh, axis=1, keepdims=True))

    return pl.pallas_call(
        body, name="gate_bwd", grid=(S // tm,),
        in_specs=[_rb(tm, D), _rb(tm, D), _fs((SB_W, D)), _fs((DN_W, D)), _rb(tm, SB_W),
                  _rb(tm, SB_W, C_SB_Z // SB_W), _rb(tm, DN_W), _rb(tm, DN_W, C_DN_Z // DN_W), _fs((1, LANES))],
        out_specs=[_rb(tm, SB_W), _rb(tm, SB_W), _rb(tm, DN_W), _rb(tm, DN_W), _fs((1, LANES))],
        out_shape=[_sds((S, SB_W), F32), _sds((S, SB_W), _MXU_DTYPE), _sds((S, DN_W), F32),
                   _sds((S, DN_W), _MXU_DTYPE), _sds((1, LANES), F32)],
        compiler_params=_cp(1),
    )(db_sb, db_dn, wb_sb, wb_dn, o_att, p, o_dn, p, gn)


def _branch(o_sb, o_dnn, wb_sb, wb_dn, p, D, tm=_ROW_BLOCK):
    S = p.shape[0]

    def body(os_ref, od_ref, ws_ref, wd_ref, ms_ref, md_ref, y_ref, bs_ref, bd_ref):
        bs = _mm(os_ref[...], ws_ref[...])
        bdn = _mm(od_ref[...], wd_ref[...])
        bs_ref[...] = bs
        bd_ref[...] = bdn
        y_ref[...] = (_sigmoid(ms_ref[...]) * bs + _sigmoid(md_ref[...]) * bdn).astype(y_ref.dtype)

    return pl.pallas_call(
        body, name="branch", grid=(S // tm,),
        in_specs=[_rb(tm, SB_W), _rb(tm, DN_W), _fs((SB_W, D)), _fs((DN_W, D)),
                  _rb(tm, D, C_MG // D), _rb(tm, D, C_MG // D + 1)],
        out_specs=[_rb(tm, D), _rb(tm, D), _rb(tm, D)],
        out_shape=[_sds((S, D), _MXU_DTYPE), _sds((S, D), F32), _sds((S, D), F32)], compiler_params=_cp(1),
    )(o_sb, o_dnn, wb_sb, wb_dn, p, p)


def _out_proj(x, y, w_out, gate, tm=_ROW_BLOCK):
    S, D = x.shape

    def body(x_ref, y_ref, w_ref, g_ref, xn_ref, out_ref):
        out = _mm(y_ref[...], w_ref[...])
        out_ref[...] = out
        xn_ref[...] = x_ref[...] + g_ref[...] * out

    return pl.pallas_call(
        body, name="out_proj", grid=(S // tm,),
        in_specs=[_rb(tm, D), _rb(tm, D), _fs((D, D)), _fs((1, D))],
        out_specs=[_rb(tm, D), _rb(tm, D)],
        out_shape=[_sds((S, D), F32), _sds((S, D), F32)], compiler_params=_cp(1),
    )(x, y, w_out, gate)


def _out_bwd(dxn, out, gate, w_out, p, b_sb, b_dn, tm=_ROW_BLOCK):
    S, D = dxn.shape

    def body(dxn_ref, out_ref, g_ref, w_ref, ms_ref, md_ref, bs_ref, bd_ref,
             dout_ref, dbs_ref, dbd_ref, dm_ref, dgate_ref):
        @pl.when(pl.program_id(0) == 0)
        def _():
            dgate_ref[...] = jnp.zeros_like(dgate_ref)

        dxv = dxn_ref[...]
        dgate_ref[...] += jnp.sum(dxv * out_ref[...], axis=0, keepdims=True)
        dout = (g_ref[...] * dxv).astype(dout_ref.dtype)
        dout_ref[...] = dout
        dy = _mm(dout, w_ref[...], _NT)
        s1, s2 = _sigmoid(ms_ref[...]), _sigmoid(md_ref[...])
        dbs_ref[...] = (dy * s1).astype(dbs_ref.dtype)
        dbd_ref[...] = (dy * s2).astype(dbd_ref.dtype)
        dm_ref[:, :D] = (dy * bs_ref[...] * s1 * (1.0 - s1)).astype(dm_ref.dtype)
        dm_ref[:, D:] = (dy * bd_ref[...] * s2 * (1.0 - s2)).astype(dm_ref.dtype)

    return pl.pallas_call(
        body, name="out_bwd", grid=(S // tm,),
        in_specs=[_rb(tm, D), _rb(tm, D), _fs((1, D)), _fs((D, D)), _rb(tm, D, C_MG // D),
                  _rb(tm, D, C_MG // D + 1), _rb(tm, D), _rb(tm, D)],
        out_specs=[_rb(tm, D), _rb(tm, D), _rb(tm, D), _rb(tm, 2 * D), _fs((1, D))],
        out_shape=[_sds((S, D), _MXU_DTYPE)] * 3 + [_sds((S, 2 * D), _MXU_DTYPE), _sds((1, D), F32)],
        compiler_params=_cp(1),
    )(dxn, out, gate, w_out, p, p, b_sb, b_dn)


def _loss_head(xf, target, tm=_ROW_BLOCK):
    S, D = xf.shape

    def body(x_ref, t_ref, dy_ref, loss_ref):
        @pl.when(pl.program_id(0) == 0)
        def _():
            loss_ref[...] = jnp.zeros_like(loss_ref)

        e = x_ref[...] - t_ref[...]
        dy_ref[...] = e * (1.0 / D)
        row = jnp.sum(e * e, axis=1, keepdims=True) * (1.0 / D)
        loss_ref[...] += 0.5 * jnp.sum(row, axis=0, keepdims=True)

    return pl.pallas_call(
        body, name="loss_head", grid=(S // tm,),
        in_specs=[_rb(tm, D), _rb(tm, D)], out_specs=[_rb(tm, D), _fs((1, LANES))],
        out_shape=[_sds((S, D), F32), _sds((1, LANES), F32)], compiler_params=_cp(1),
    )(xf, target)


def _ada_fwd(c_all, ada_w, ada_b_sh):
    L, D, n = ada_w.shape
    B = c_all.shape[0]

    def body(c_ref, w_ref, b_ref, o_ref):
        sc = _silu(c_ref[...])
        o_ref[0] = _mm(sc, w_ref[0]) + b_ref[0]

    return pl.pallas_call(
        body, name="ada_fwd", grid=(L,),
        in_specs=[_fs((B, D)), pl.BlockSpec((1, D, n), lambda l: (l, 0, 0)), pl.BlockSpec((1, 1, n), lambda l: (l, 0, 0))],
        out_specs=pl.BlockSpec((1, B, n), lambda l: (l, 0, 0)),
        out_shape=_sds((L, B, n), F32), compiler_params=_cp(1),
    )(c_all, ada_w, ada_b_sh)


def _ada_bwd(c_all_t, dmod_sh):
    D, B = c_all_t.shape
    L, _, n = dmod_sh.shape

    def body(c_ref, d_ref, o_ref):
        acc = jnp.zeros((D, n), F32)
        for b in range(B):
            acc = acc + _silu(c_ref[:, b:b + 1]) * d_ref[0, b:b + 1, :]
        o_ref[0] = acc

    return pl.pallas_call(
        body, name="ada_bwd", grid=(L,),
        in_specs=[_fs((D, B)), pl.BlockSpec((1, B, n), lambda l: (l, 0, 0))],
        out_specs=pl.BlockSpec((1, D, n), lambda l: (l, 0, 0)),
        out_shape=_sds((L, D, n), F32), compiler_params=_cp(1),
    )(c_all_t, dmod_sh)


def _sum_parts(name, parts):
    P, R, C = parts.shape
    tr = _pick(R, max(16, min(512, (1 << 19) // (P * C))), 16) if R % 16 == 0 else R

    def body(p_ref, o_ref):
        acc = p_ref[0].astype(F32)
        for k in range(1, P):
            acc = acc + p_ref[k].astype(F32)
        o_ref[...] = acc

    return pl.pallas_call(
        body, name=name, grid=(R // tr,),
        in_specs=[pl.BlockSpec((P, tr, C), lambda i: (0, i, 0))], out_specs=_rb(tr, C),
        out_shape=_sds((R, C), F32), compiler_params=_cp(1),
    )(parts)


def _adamw(name, w, g, m, v):
    L, R, C = w.shape
    tr = _pick(R, 256, 8) if R % 8 == 0 else R
    c1 = 1.0 - ADAM_B1 ** ADAM_STEP
    c2 = 1.0 - ADAM_B2 ** ADAM_STEP

    def body(w_ref, g_ref, m_ref, v_ref, d_ref, mo_ref, vo_ref):
        gv = g_ref[...]
        mn = ADAM_B1 * m_ref[...] + (1.0 - ADAM_B1) * gv
        vn = ADAM_B2 * v_ref[...] + (1.0 - ADAM_B2) * (gv * gv)
        mo_ref[...] = mn
        vo_ref[...] = vn
        d_ref[...] = -ADAM_LR * ((mn / c1) / (jnp.sqrt(vn / c2) + ADAM_EPS) + ADAM_WD * w_ref[...])

    spec = pl.BlockSpec((1, tr, C), lambda l, i: (l, i, 0))
    return pl.pallas_call(
        body, name=name, grid=(L, R // tr),
        in_specs=[spec] * 4, out_specs=[spec] * 3, out_shape=[_sds((L, R, C), F32)] * 3, compiler_params=_cp(2),
    )(w, g, m, v)


def _ag_small(name, blk):
    R, C = blk.shape

    def body(x_ref, out_ref, send_sems, recv_sems, local_sem):
        x, y, c = lax.axis_index("x"), lax.axis_index("y"), lax.axis_index("c")
        me, sibling = (x, y, c), (x, y, 1 - c)
        chips = [(1 - x, y), (x, 1 - y), (1 - x, 1 - y)]

        def rows(px, py, pc):
            return out_ref.at[pl.ds((4 * px + 2 * py + pc) * R, R), :]

        def copy(k, block, to, src=None):
            return pltpu.make_async_remote_copy(
                src_ref=rows(*block) if src is None else src, dst_ref=rows(*block),
                send_sem=send_sems.at[k], recv_sem=recv_sems.at[k], device_id=to, device_id_type=MESH)

        mine = pltpu.make_async_copy(x_ref, rows(*me), local_sem)
        mine.start()
        first = [copy(0, me, sibling, src=x_ref)]
        first += [copy(1 + j, me, (*chip, c), src=x_ref) for j, chip in enumerate(chips)]
        for cp in first:
            cp.start()
        passed = [copy(4 + j, (*chip, c), sibling) for j, chip in enumerate(chips)]
        for j, chip in enumerate(chips):
            copy(1 + j, (*chip, c), me).wait_recv()
            passed[j].start()
        copy(0, sibling, me).wait_recv()
        for j, chip in enumerate(chips):
            copy(4 + j, (*chip, 1 - c), me).wait_recv()
        for cp in first + passed:
            cp.wait_send()
        mine.wait()

    return pl.pallas_call(
        body, name=name, out_shape=_sds((8 * R, C), blk.dtype),
        in_specs=[pl.BlockSpec(memory_space=pltpu.VMEM)], out_specs=pl.BlockSpec(memory_space=pltpu.VMEM),
        scratch_shapes=[pltpu.SemaphoreType.DMA((7,)), pltpu.SemaphoreType.DMA((7,)), pltpu.SemaphoreType.DMA],
    )(blk)


def _row_chunks(ts, row_axis):
    pieces = []
    for t, a in enumerate(ts):
        rows = a.shape[row_axis]
        n = 4 if rows >= 1024 else 1
        pieces += [(t, i * (rows // n), rows // n) for i in range(n)]
    return pieces


def _ag_weights_first(ts):
    nt = len(ts)
    pieces = _row_chunks(ts, 0)
    NP = len(pieces)
    sizes = [nr * ts[t].shape[1] for t, _, nr in pieces]
    split = next(pi for pi in range(NP + 1) if 2 * sum(sizes[:pi]) >= sum(sizes))

    def body(*refs):
        w, out = refs[:nt], refs[nt:2 * nt]
        send_sems, recv_sems, local_sems = refs[2 * nt:]
        x, y, c = lax.axis_index("x"), lax.axis_index("y"), lax.axis_index("c")
        me, sibling = (x, y, c), (x, y, 1 - c)
        mine = 2 * x + y
        chips = [(1 - x, y), (x, 1 - y), (1 - x, 1 - y)]

        def blk(t, shard, r0, nr):
            return out[t].at[shard, r0:r0 + nr, :]

        def copy(k, dst, to, src=None):
            return pltpu.make_async_remote_copy(
                src_ref=dst if src is None else src, dst_ref=dst, send_sem=send_sems.at[k], recv_sem=recv_sems.at[k],
                device_id=to, device_id_type=MESH)

        own = [pltpu.make_async_copy(w[t], out[t].at[mine], local_sems.at[t]) for t in range(nt)]
        for cp in own:
            cp.start()
        for fetcher, lo, hi in ((0, 0, split), (1, split, NP)):
            @pl.when(c == fetcher)
            def _(lo=lo, hi=hi):
                sent = []
                for j, chip in enumerate(chips):
                    for pi in range(lo, hi):
                        t, r0, nr = pieces[pi]
                        sent.append(copy(j * NP + pi, blk(t, mine, r0, nr), (*chip, c), src=w[t].at[r0:r0 + nr, :]))
                        sent[-1].start()
                for j, chip in enumerate(chips):
                    theirs = 2 * chip[0] + chip[1]
                    for pi in range(lo, hi):
                        t, r0, nr = pieces[pi]
                        copy(j * NP + pi, blk(t, theirs, r0, nr), me).wait_recv()
                        sent.append(copy((3 + j) * NP + pi, blk(t, theirs, r0, nr), sibling))
                        sent[-1].start()
                for cp in sent:
                    cp.wait_send()

            @pl.when(c != fetcher)
            def _(lo=lo, hi=hi):
                for j, chip in enumerate(chips):
                    theirs = 2 * chip[0] + chip[1]
                    for pi in range(lo, hi):
                        t, r0, nr = pieces[pi]
                        copy((3 + j) * NP + pi, blk(t, theirs, r0, nr), me).wait_recv()

        for cp in own:
            cp.wait()

    return pl.pallas_call(
        body, name="ag_weights_first", out_shape=[_sds((4,) + a.shape, a.dtype) for a in ts],
        in_specs=[pl.BlockSpec(memory_space=pl.ANY)] * nt, out_specs=[pl.BlockSpec(memory_space=pltpu.VMEM)] * nt,
        scratch_shapes=[pltpu.SemaphoreType.DMA((6 * NP,)), pltpu.SemaphoreType.DMA((6 * NP,)),
                        pltpu.SemaphoreType.DMA((nt,))],
        compiler_params=pltpu.CompilerParams(vmem_limit_bytes=_VMEM_LIMIT),
    )(*ts)


class _WeightGather:
    def __init__(self, ts):
        self.nt = len(ts)
        self.pieces = _row_chunks(ts, 0)
        NP = len(self.pieces)
        self.out_shape = [_sds((4,) + a.shape, a.dtype) for a in ts]
        self.scratch = [pltpu.SemaphoreType.DMA((3 * NP,)), pltpu.SemaphoreType.DMA((3 * NP,)),
                        pltpu.SemaphoreType.DMA((self.nt,))]

    def _copies(self, src, out, sems):
        send_sems, recv_sems, local_sems = sems
        NP = len(self.pieces)
        x, y, c = lax.axis_index("x"), lax.axis_index("y"), lax.axis_index("c")
        mine = 2 * x + y
        own = [pltpu.make_async_copy(src[t], out[t].at[mine], local_sems.at[t]) for t in range(self.nt)]
        sends, recvs = [], []
        for j, chip in enumerate([(1 - x, y), (x, 1 - y), (1 - x, 1 - y)]):
            theirs = 2 * chip[0] + chip[1]
            for pi, (t, r0, nr) in enumerate(self.pieces):
                idx = j * NP + pi
                sends.append(pltpu.make_async_remote_copy(
                    src_ref=src[t].at[r0:r0 + nr, :], dst_ref=out[t].at[mine, r0:r0 + nr, :],
                    send_sem=send_sems.at[idx], recv_sem=recv_sems.at[idx], device_id=(*chip, c), device_id_type=MESH))
                recvs.append(pltpu.make_async_remote_copy(
                    src_ref=out[t].at[theirs, r0:r0 + nr, :], dst_ref=out[t].at[theirs, r0:r0 + nr, :],
                    send_sem=send_sems.at[idx], recv_sem=recv_sems.at[idx], device_id=(x, y, c), device_id_type=MESH))
        return own, sends, recvs

    def start(self, src, out, sems):
        own, sends, _ = self._copies(src, out, sems)
        for cp in own + sends:
            cp.start()

    def wait(self, src, out, sems):
        own, sends, recvs = self._copies(src, out, sems)
        for cp in recvs:
            cp.wait_recv()
        for cp in sends:
            cp.wait_send()
        for cp in own:
            cp.wait()


class _GradExchange:
    def __init__(self, ts, layer, chunks=None):
        self.nt, self.layer = len(ts), layer
        self.pieces = [p for i, p in enumerate(_row_chunks(ts, 1)) if chunks is None or i in chunks]
        NP = len(self.pieces)
        self.out_shape = [_sds((8,) + a.shape[1:], a.dtype) for a in ts]
        self.scratch = [pltpu.SemaphoreType.DMA((7 * NP,)), pltpu.SemaphoreType.DMA((7 * NP,)),
                        pltpu.SemaphoreType.DMA((NP,))]

    def _copies(self, src, out, sems):
        send_sems, recv_sems, local_sems = sems
        NP = len(self.pieces)
        x, y, c = lax.axis_index("x"), lax.axis_index("y"), lax.axis_index("c")
        me = 4 * x + 2 * y + c
        owner = c == self.layer
        own = [pltpu.make_async_copy(src[t].at[2 * x + y, r0:r0 + nr, :], out[t].at[me, r0:r0 + nr, :],
                                     local_sems.at[pi]) for pi, (t, r0, nr) in enumerate(self.pieces)]
        rel = []
        for k in range(1, 8):
            px = 1 - x if k & 4 else x
            py = 1 - y if k & 2 else y
            source = 4 * px + 2 * py + (1 - c if k & 1 else c)
            sends, recvs = [], []
            for pi, (t, r0, nr) in enumerate(self.pieces):
                idx = (k - 1) * NP + pi
                sends.append(pltpu.make_async_remote_copy(
                    src_ref=src[t].at[2 * px + py, r0:r0 + nr, :], dst_ref=out[t].at[me, r0:r0 + nr, :],
                    send_sem=send_sems.at[idx], recv_sem=recv_sems.at[idx], device_id=(px, py, self.layer),
                    device_id_type=MESH))
                recvs.append(pltpu.make_async_remote_copy(
                    src_ref=out[t].at[source, r0:r0 + nr, :], dst_ref=out[t].at[source, r0:r0 + nr, :],
                    send_sem=send_sems.at[idx], recv_sem=recv_sems.at[idx], device_id=(x, y, c),
                    device_id_type=MESH))
            rel.append((jnp.logical_not(owner) if k & 1 else owner, sends, recvs))
        return owner, own, rel

    def start(self, src, out, sems):
        owner, own, rel = self._copies(src, out, sems)

        @pl.when(owner)
        def _():
            for cp in own:
                cp.start()

        for sending, sends, _ in rel:
            @pl.when(sending)
            def _(sends=sends):
                for cp in sends:
                    cp.start()

    def wait(self, src, out, sems):
        owner, own, rel = self._copies(src, out, sems)

        @pl.when(owner)
        def _():
            for _, _, recvs in rel:
                for cp in recvs:
                    cp.wait_recv()
            for cp in own:
                cp.wait()

        for sending, sends, _ in rel:
            @pl.when(sending)
            def _(sends=sends):
                for cp in sends:
                    cp.wait_send()


def _sibling_join(ts):
    nt = len(ts)
    pieces = _row_chunks(ts, 0)
    NP = len(pieces)

    def body(*refs):
        src, out = refs[:nt], refs[nt:2 * nt]
        send_sems, recv_sems, local_sems = refs[2 * nt:]
        x, y, c = lax.axis_index("x"), lax.axis_index("y"), lax.axis_index("c")
        own = [pltpu.make_async_copy(src[t], out[t].at[c], local_sems.at[t]) for t in range(nt)]
        for cp in own:
            cp.start()
        sent = []
        for pi, (t, r0, nr) in enumerate(pieces):
            sent.append(pltpu.make_async_remote_copy(
                src_ref=src[t].at[r0:r0 + nr, :], dst_ref=out[t].at[c, r0:r0 + nr, :], send_sem=send_sems.at[pi],
                recv_sem=recv_sems.at[pi], device_id=(x, y, 1 - c), device_id_type=MESH))
            sent[-1].start()
        for pi, (t, r0, nr) in enumerate(pieces):
            pltpu.make_async_remote_copy(
                src_ref=src[t].at[r0:r0 + nr, :], dst_ref=out[t].at[1 - c, r0:r0 + nr, :], send_sem=send_sems.at[pi],
                recv_sem=recv_sems.at[pi], device_id=(x, y, c), device_id_type=MESH).wait_recv()
        for cp in sent:
            cp.wait_send()
        for cp in own:
            cp.wait()

    vmem = pl.BlockSpec(memory_space=pltpu.VMEM)
    return pl.pallas_call(
        body, name="sibling_join", out_shape=[_sds((2,) + a.shape, a.dtype) for a in ts],
        in_specs=[vmem] * nt, out_specs=[vmem] * nt,
        scratch_shapes=[pltpu.SemaphoreType.DMA((NP,)), pltpu.SemaphoreType.DMA((NP,)),
                        pltpu.SemaphoreType.DMA((nt,))],
        compiler_params=pltpu.CompilerParams(vmem_limit_bytes=_VMEM_LIMIT),
    )(*ts)


def _late_weights(gathered):
    g_bs, g_bd, g_out = gathered
    cat = lambda g, axis: jnp.concatenate([g[s] for s in range(4)], axis=axis)
    return dict(wb_sb=cat(g_bs, 1), wb_dn=cat(g_bd, 1), w_out=cat(g_out, 0))


def _layer_fwd(x, shift, scale, gate, lw, next_shards=None, late_shards=None):
    D = x.shape[1]
    h = _norm_mod(x, lw["norm_g"], scale, shift)
    if late_shards is None:
        p = _matmul("in_proj", h, lw["w_cat"], "nn", F32, tm_cap=1024, tn_cap=896)
    else:
        p, late = _matmul("in_proj", h, lw["w_cat"], "nn", F32, tm_cap=1024, tn_cap=896,
                          exchange=_WeightGather(late_shards), ex_src=late_shards)
        lw = {**lw, **_late_weights(late)}
    qn, kn = _sb_prep(p, lw["gq_t"], lw["gk_t"])
    qkv, bb, gcb, glb = _dn_prep(p, lw["conv_w"], lw["a_row"], lw["dtb_row"])
    if next_shards is None:
        (o_att, tot, nblocks), _ = _sb_fwd(qn, kn, p)
        (o_dn, tinv, sall), gathered = _dn_fwd(qkv, bb, gcb, glb)
    else:
        first, rest = next_shards[:1], next_shards[1:]
        (o_att, tot, nblocks), g_rest = _sb_fwd(qn, kn, p, _WeightGather(rest), rest)
        (o_dn, tinv, sall), g_first = _dn_fwd(qkv, bb, gcb, glb, _WeightGather(first), first)
        gathered = list(g_first) + g_rest
    o_sb, o_dnn = _gate(o_att, o_dn, p, lw["gn"])
    y, b_sb, b_dn = _branch(o_sb, o_dnn, lw["wb_sb"], lw["wb_dn"], p, D)
    x_next, out = _out_proj(x, y, lw["w_out"], gate)
    res = dict(x=x, h=h, p=p, qn=qn, kn=kn, o_att=o_att, tot=tot, nblocks=nblocks, qkv=qkv, bb=bb, gcb=gcb, glb=glb, o_dn=o_dn,
               tinv=tinv, sall=sall, o_sb=o_sb, o_dnn=o_dnn, y=y, b_sb=b_sb, b_dn=b_dn, out=out,
               shift=shift, scale=scale, gate=gate)
    return x_next, res, lw, gathered


def _layer_bwd(dxn, res, lw, pending=None):
    p = res["p"]
    dout, db_sb, db_dn, dm, dgate = _out_bwd(dxn, res["out"], res["gate"], lw["w_out"], p, res["b_sb"], res["b_dn"])
    dw_out = _matmul("dw_out", res["y"], dout, "tn", _MXU_DTYPE)
    dwb_sb = _matmul("dwb_sb", res["o_sb"], db_sb, "tn", _MXU_DTYPE)
    dwb_dn = _matmul("dwb_dn", res["o_dnn"], db_dn, "tn", _MXU_DTYPE)
    do_att, dz_sb, do_dn, dz_dn, dgn = _gate_bwd(db_sb, db_dn, lw["wb_sb"], lw["wb_dn"], res["o_att"], res["o_dn"],
                                                  p, lw["gn"])
    D = dxn.shape[1]
    by_shard = lambda g: g.reshape(g.shape[0], 4, g.shape[1] // 4).transpose(1, 0, 2)
    send = [by_shard(dwb_sb), by_shard(dwb_dn), dw_out.reshape(4, D // 4, D)]
    dn_args = (res["qkv"], res["bb"], res["gcb"], res["glb"], res["tinv"], res["sall"], do_dn)
    if pending is None:
        (dqn, dkn, dv), _ = _sb_bwd(res["qn"], res["kn"], p, do_att, res["tot"], res["nblocks"])
        (dqkv, dbb, dgb), received = _dn_bwd(*dn_args)
    else:
        above, above_send = pending
        (dqn, dkn, dv), got_in = _sb_bwd(res["qn"], res["kn"], p, do_att, res["tot"], res["nblocks"],
                                         _GradExchange(above_send[:1], above), above_send[:1])
        (dqkv, dbb, dgb), got_rest = _dn_bwd(*dn_args, _GradExchange(above_send[1:], above), above_send[1:],
                                             _GradExchange(send, above - 1), send)
        received, send = list(got_in) + got_rest, []
    dq_sb, dk_sb, dgq, dgk = _sb_prep_bwd(p, dqn, dkn, lw["gq_t"], lw["gk_t"])
    dc, dp_ba, dal, ddt = _dn_prep_bwd_a(p, dqkv, dbb, dgb, lw["conv_w"], lw["a_row"], lw["dtb_row"])
    dp_dn, dconv = _dn_prep_bwd_b(p, dc, lw["conv_w"])
    dp = jnp.concatenate([dp_dn, dz_dn, dq_sb, dk_sb, dv.astype(_MXU_DTYPE), dz_sb, dm, dp_ba], axis=1)
    dw_cat = _matmul("dw_cat", res["h"], dp, "tn", _MXU_DTYPE, tm_cap=1024, tn_cap=896, tk_cap=2048)
    send = [_shards_from_cat(dw_cat, D)] + send
    if pending is None:
        dh = _matmul("dh", dp, lw["w_cat"], "nt", F32, tm_cap=1024, tk_cap=896)
        (dx, dshift, dscale, dnorm_g), _ = _norm_mod_bwd(res["x"], dh, dxn, lw["norm_g"], res["scale"])
    else:
        dh, arrived = _matmul("dh", dp, lw["w_cat"], "nt", F32, tm_cap=1024, tk_cap=896,
                              exchange=_GradExchange(send, pending[0] - 1, (0, 1, 2)), ex_src=send,
                              ex_prev=received[:1])
        tail = _GradExchange(send, pending[0] - 1, (3,))
        if tail.pieces:
            (dx, dshift, dscale, dnorm_g), arrived = _norm_mod_bwd(
                res["x"], dh, dxn, lw["norm_g"], res["scale"], exchange=tail, ex_src=send, ex_prev=arrived)
        else:
            (dx, dshift, dscale, dnorm_g), _ = _norm_mod_bwd(res["x"], dh, dxn, lw["norm_g"], res["scale"])
        received, send = arrived + list(received[1:]), []
    small = dict(dmod=jnp.concatenate([dshift, dscale, dgate], axis=1)[0], norm_g=dnorm_g[0],
                 sb_q_g=dgq.reshape(SB_HEADS, SB_HD).sum(0), sb_k_g=dgk.reshape(SB_HEADS, SB_HD).sum(0),
                 conv_w=dconv, dn_a_log=dal[0, DN_HEADS:2 * DN_HEADS], dn_dt_bias=ddt[0, DN_HEADS:2 * DN_HEADS],
                 dn_norm_g=dgn[0])
    return dx, small, send, received


def _cat_cols(w, D):
    return jnp.concatenate([w[:, 2048:4096], w[:, 0:2048], w[:, 4104:4104 + 2 * D], w[:, 4096:4104],
                            jnp.zeros((w.shape[0], LANES - 8), w.dtype)], axis=1)


def _shards_from_cat(g, D):
    n = (4104 + 2 * D) // 4
    segments = ((0, 2048, 2048), (2048, 4096, 0), (4096, 4104, 4096 + 2 * D), (4104, 4104 + 2 * D, 4096))

    def shard(lo, hi):
        cuts = [(c0 + max(lo, s0) - s0, c0 + min(hi, s1) - s0) for s0, s1, c0 in segments if max(lo, s0) < min(hi, s1)]
        return jnp.concatenate([g[:, a:b] for a, b in cuts], axis=1)

    return jnp.stack([shard(s * n, (s + 1) * n) for s in range(4)])


def _flat_pack(arrs, mult):
    flat = jnp.concatenate([a.reshape(-1) for a in arrs])
    n = flat.shape[0]
    pad = (-n) % mult
    if pad:
        flat = jnp.concatenate([flat, jnp.zeros((pad,), flat.dtype)])
    return flat.reshape(-1, LANES)


def _flat_unpack(flat, shapes):
    flat = flat.reshape(-1)
    out, off = [], 0
    for s in shapes:
        n = math.prod(s)
        out.append(flat[off:off + n].reshape(s))
        off += n
    return out


BIG = ("w_in", "w_branch_sb", "w_branch_dn", "w_out")
SMALL = ("ada_b", "norm_g", "sb_q_g", "sb_k_g", "conv_w", "dn_a_log", "dn_dt_bias", "dn_norm_g")


def kernel(x, c, ada_w, ada_b, norm_g, w_in, sb_q_g, sb_k_g, conv_w, dn_a_log, dn_dt_bias, dn_norm_g, w_branch_sb, w_branch_dn, w_out, loss_target, m_ada_w, m_ada_b, m_norm_g, m_w_in, m_sb_q_g, m_sb_k_g, m_conv_w, m_dn_a_log, m_dn_dt_bias, m_dn_norm_g, m_w_branch_sb, m_w_branch_dn, m_w_out, v_ada_w, v_ada_b, v_norm_g, v_w_in, v_sb_q_g, v_sb_k_g, v_conv_w, v_dn_a_log, v_dn_dt_bias, v_dn_norm_g, v_w_branch_sb, v_w_branch_dn, v_w_out):
    W = dict(ada_w=ada_w, ada_b=ada_b, norm_g=norm_g, w_in=w_in, sb_q_g=sb_q_g, sb_k_g=sb_k_g, conv_w=conv_w,
             dn_a_log=dn_a_log, dn_dt_bias=dn_dt_bias, dn_norm_g=dn_norm_g, w_branch_sb=w_branch_sb,
             w_branch_dn=w_branch_dn, w_out=w_out)
    M = dict(ada_w=m_ada_w, ada_b=m_ada_b, norm_g=m_norm_g, w_in=m_w_in, sb_q_g=m_sb_q_g, sb_k_g=m_sb_k_g,
             conv_w=m_conv_w, dn_a_log=m_dn_a_log, dn_dt_bias=m_dn_dt_bias, dn_norm_g=m_dn_norm_g,
             w_branch_sb=m_w_branch_sb, w_branch_dn=m_w_branch_dn, w_out=m_w_out)
    V = dict(ada_w=v_ada_w, ada_b=v_ada_b, norm_g=v_norm_g, w_in=v_w_in, sb_q_g=v_sb_q_g, sb_k_g=v_sb_k_g,
             conv_w=v_conv_w, dn_a_log=v_dn_a_log, dn_dt_bias=v_dn_dt_bias, dn_norm_g=v_dn_norm_g,
             w_branch_sb=v_w_branch_sb, w_branch_dn=v_w_branch_dn, w_out=v_w_out)
    L = ada_w.shape[0]
    S, D = x.shape[1], x.shape[2]
    ix, iy, ic = lax.axis_index("x"), lax.axis_index("y"), lax.axis_index("c")
    shard = 2 * ix + iy
    me = 2 * shard + ic
    n_ada = ada_w.shape[2]
    n_in = w_in.shape[2]
    n_conv = conv_w.shape[2]
    n_br = w_branch_sb.shape[2]
    n_out = w_out.shape[1]

    assert L == 2, "the owner of a layer's gradients is the core with the layer's number"
    shards = [W[n].astype(_MXU_DTYPE) for n in BIG]
    gathered0 = _ag_weights_first([shards[0][0]])

    g1 = _ag_small("ag_c_conv", _flat_pack([c, conv_w], LANES * 8))
    g1 = g1.reshape(8, -1)
    c_all = g1[:, :D]
    conv_parts = g1[:, D:D + L * CONV_K * n_conv].reshape(4, 2, L, CONV_K, n_conv)[:, 0]
    conv_full = jnp.concatenate([conv_parts[s] for s in range(4)], axis=2)
    ada_b_sh = lax.dynamic_slice_in_dim(ada_b, shard * n_ada, n_ada, axis=1)[:, None, :]
    mod_sh = _ada_fwd(c_all, ada_w, ada_b_sh)
    g2 = _ag_small("ag_mod", _flat_pack([mod_sh], LANES * 8)).reshape(8, -1)
    mod_parts = g2[:, :L * 8 * n_ada].reshape(4, 2, L, 8, n_ada)[:, 0]
    mod_all = jnp.concatenate([mod_parts[s] for s in range(4)], axis=2)
    mod = lax.dynamic_index_in_dim(mod_all, me, axis=1, keepdims=False)

    def layer_weights(l, g_in, late=None):
        pad_lo = jnp.zeros((DN_HEADS,), F32)
        pad_hi = jnp.zeros((LANES - 2 * DN_HEADS,), F32)
        lw = dict(
            norm_g=norm_g[l][None, :], w_cat=_cat_cols(jnp.concatenate([g_in[s] for s in range(4)], axis=1), D),
            gq_t=jnp.tile(sb_q_g[l], SB_HEADS)[None, :], gk_t=jnp.tile(sb_k_g[l], SB_HEADS)[None, :],
            conv_w=conv_full[l],
            a_row=jnp.concatenate([pad_lo, dn_a_log[l], pad_hi])[None, :],
            dtb_row=jnp.concatenate([pad_lo, dn_dt_bias[l], pad_hi])[None, :], gn=dn_norm_g[l][None, :])
        return lw if late is None else {**lw, **_late_weights(late)}

    mods = lambda l: (mod[l, None, 0:D], mod[l, None, D:2 * D], mod[l, None, 2 * D:3 * D])
    lws, ress = [None] * L, [None] * L
    xs, ress[0], lws[0], gathered1 = _layer_fwd(x[0], *mods(0), layer_weights(0, gathered0[0]),
                                                next_shards=[a[1] for a in shards],
                                                late_shards=[a[0] for a in shards[1:]])
    xs, ress[1], lws[1], _ = _layer_fwd(xs, *mods(1), layer_weights(1, gathered1[0], gathered1[1:]))
    dxs, loss_row = _loss_head(xs, loss_target[0])
    loss = lax.psum(loss_row[0, 0], ("x", "y", "c"))
    smalls = [None] * L
    dxs, smalls[1], send1, _ = _layer_bwd(dxs, ress[1], lws[1])
    dxs, smalls[0], send0, got = _layer_bwd(dxs, ress[0], lws[0], (1, send1))
    grad_x = dxs[None]

    small_names = ("dmod",) + SMALL[1:]
    small_pack = _flat_pack([jnp.stack([smalls[l][n] for l in range(L)]) for n in small_names], LANES * 8)
    g3 = _ag_small("ag_small_grads", small_pack)
    R3 = small_pack.shape[0]
    g3 = g3.reshape(8, R3, LANES)
    small_sum = _sum_parts("sum_small", g3)
    small_shapes = [(L, 3 * D), (L, D), (L, SB_HD), (L, SB_HD), (L, CONV_K, 3 * DN_W), (L, DN_HEADS), (L, DN_HEADS),
                    (L, DN_HD)]
    sg = dict(zip(small_names, _flat_unpack(small_sum, small_shapes)))
    G = dict(ada_b=sg["dmod"], norm_g=sg["norm_g"], sb_q_g=sg["sb_q_g"], sb_k_g=sg["sb_k_g"],
             conv_w=lax.dynamic_slice_in_dim(sg["conv_w"], shard * n_conv, n_conv, axis=2),
             dn_a_log=sg["dn_a_log"], dn_dt_bias=sg["dn_dt_bias"], dn_norm_g=sg["dn_norm_g"])
    dmod_all = g3.reshape(8, -1)[:, :L * 3 * D].reshape(8, L, 3 * D)
    dmod_sh = lax.dynamic_slice_in_dim(dmod_all, shard * n_ada, n_ada, axis=2).transpose(1, 0, 2)
    G["ada_w"] = _ada_bwd(c_all.T, dmod_sh)

    assert not send0
    mine = [_sum_parts("sum_" + n, g) for n, g in zip(BIG, got)]
    for n, g in zip(BIG, _sibling_join(mine)):
        G[n] = g

    delta, new_m, new_v = {}, {}, {}
    for n in ("ada_w",) + BIG:
        delta[n], new_m[n], new_v[n] = _adamw("adamw_" + n, W[n], G[n], M[n], V[n])
    sm_shapes = [W[n].shape for n in SMALL]
    d, mo, vo = _adamw("adamw_small", *[_flat_pack([T[n] for n in SMALL], LANES * 8)[None] for T in (W, G, M, V)])
    for n, dd, mm, vv in zip(SMALL, _flat_unpack(d, sm_shapes), _flat_unpack(mo, sm_shapes),
                             _flat_unpack(vo, sm_shapes)):
        delta[n], new_m[n], new_v[n] = dd, mm, vv

    order = ("ada_w", "ada_b", "norm_g", "w_in", "sb_q_g", "sb_k_g", "conv_w", "dn_a_log", "dn_dt_bias", "dn_norm_g",
             "w_branch_sb", "w_branch_dn", "w_out")
    return (loss, grad_x, *[G[n] for n in order], *[delta[n] for n in order], *[new_m[n] for n in order],
            *[new_v[n] for n in order])
```

```python
import math

import jax
import jax.numpy as jnp
from jax import lax
from jax.experimental import pallas as pl
from jax.experimental.pallas import tpu as pltpu

F32 = jnp.float32
BF16 = jnp.bfloat16
_MXU_DTYPE = BF16
_VMEM_LIMIT = 48 * 1024 * 1024
LANES = 128

EPS = 1e-6
SB_HEADS, SB_HD, SB_W = 8, 64, 512
DN_HEADS, DN_HD, DN_W = 4, 128, 512
CONV_K = 4
CHUNK = 64
_ROW_BLOCK = 512
QB = 256
_SB_DEAD = 104.0
ADAM_LR, ADAM_B1, ADAM_B2, ADAM_EPS, ADAM_WD, ADAM_STEP = 0.001, 0.9, 0.999, 1e-08, 0.01, 10

C_DN_QKV, C_DN_Z, C_SB_Q, C_SB_K, C_SB_V, C_SB_Z, C_MG = 0, 1536, 2048, 2560, 3072, 3584, 4096

_NN = (((1,), (0,)), ((), ()))
_NT = (((1,), (1,)), ((), ()))
_TN = (((0,), (0,)), ((), ()))
_BNN = (((2,), (1,)), ((0,), (0,)))
_BNT = (((2,), (2,)), ((0,), (0,)))
_BTN = (((1,), (1,)), ((0,), (0,)))
MESH = pl.DeviceIdType.MESH


def _sds(shape, dtype):
    return jax.ShapeDtypeStruct(shape, dtype)


def _cp(n):
    return pltpu.CompilerParams(dimension_semantics=("arbitrary",) * n, vmem_limit_bytes=_VMEM_LIMIT)


def _rb(tm, w, cb=0):
    return pl.BlockSpec((tm, w), lambda i: (i, cb))


def _fs(shape):
    nd = len(shape)
    return pl.BlockSpec(shape, lambda i: (0,) * nd)


def _dg(a, b, dims):
    return lax.dot_general(a, b, dims, preferred_element_type=F32)


def _mm(a, b, dims=_NN):
    return _dg(a.astype(_MXU_DTYPE), b.astype(_MXU_DTYPE), dims)


def _split3(x):
    hi = x.astype(BF16)
    r = x - hi.astype(F32)
    mid = r.astype(BF16)
    lo = (r - mid.astype(F32)).astype(BF16)
    return hi, mid, lo


def _mm_xl(x, const, dims=_NN):
    cb = const.astype(BF16)
    hi, mid, lo = _split3(x)
    return _dg(hi, cb, dims) + _dg(mid, cb, dims) + _dg(lo, cb, dims)


def _mm_xl2(x, const, dims=_NN):
    cb = const.astype(BF16)
    hi = x.astype(BF16)
    lo = (x - hi.astype(F32)).astype(BF16)
    return _dg(hi, cb, dims) + _dg(lo, cb, dims)


def _mm_xr(const, x, dims=_NN):
    cb = const.astype(BF16)
    hi, mid, lo = _split3(x)
    return _dg(cb, hi, dims) + _dg(cb, mid, dims) + _dg(cb, lo, dims)


def _mm3(a, b, dims=_NN):
    ah, am, _ = _split3(a)
    bh, bm, _ = _split3(b)
    return _dg(ah, bh, dims) + (_dg(ah, bm, dims) + _dg(am, bh, dims))


def _sigmoid(z):
    return 1.0 / (1.0 + jnp.exp(-z))


def _silu(z):
    return z * _sigmoid(z)


def _dsilu(z):
    s = _sigmoid(z)
    return s * (1.0 + z * (1.0 - s))


def _softplus(z):
    return jnp.maximum(z, 0.0) + jnp.log(1.0 + jnp.exp(-jnp.abs(z)))


def _iota2(shape, dim):
    return lax.broadcasted_iota(jnp.int32, shape, dim)


def _pick(n, cap, mult):
    best = None
    for t in range(mult, min(n, cap) + 1, mult):
        if n % t == 0:
            best = t
    assert best is not None, (n, cap, mult)
    return best


def _matmul(name, a, b, form, out_dtype, tm_cap=512, tn_cap=1024, tk_cap=1024, exchange=None, ex_src=(), ex_prev=()):
    if form == "nn":
        (M, K), (_, N) = a.shape, b.shape
    elif form == "nt":
        (M, K), (N, _) = a.shape, b.shape
    else:
        (K, M), (_, N) = a.shape, b.shape
    tm = _pick(M, tm_cap, 128 if form == "tn" else 8)
    tn = _pick(N, tn_cap, 128)
    tk = _pick(K, tk_cap, 128)
    nk = K // tk
    dims = {"nn": _NN, "nt": _NT, "tn": _TN}[form]
    if form == "nn":
        a_spec = pl.BlockSpec((tm, tk), lambda i, j, k: (i, k))
        b_spec = pl.BlockSpec((tk, tn), lambda i, j, k: (k, j))
    elif form == "nt":
        a_spec = pl.BlockSpec((tm, tk), lambda i, j, k: (i, k))
        b_spec = pl.BlockSpec((tn, tk), lambda i, j, k: (j, k))
    else:
        a_spec = pl.BlockSpec((tk, tm), lambda i, j, k: (k, i))
        b_spec = pl.BlockSpec((tk, tn), lambda i, j, k: (k, j))

    grid = (M // tm, N // tn, nk)
    nx, npv = len(ex_src), len(ex_prev)
    o0 = 2 + nx + npv

    def body(*refs):
        if exchange is None:
            compute(*refs)
            return
        src, xout, sems = refs[2:2 + nx], refs[o0 + 1:o0 + 1 + nx], refs[o0 + 2 + nx:]
        at = [pl.program_id(d) for d in range(3)]

        @pl.when(jnp.logical_and(jnp.logical_and(at[0] == 0, at[1] == 0), at[2] == 0))
        def _():
            exchange.start(src, xout, sems)

        compute(refs[0], refs[1], refs[o0], refs[o0 + 1 + nx])

        @pl.when(jnp.logical_and(jnp.logical_and(at[0] == grid[0] - 1, at[1] == grid[1] - 1), at[2] == nk - 1))
        def _():
            exchange.wait(src, xout, sems)

    def compute(a_ref, b_ref, o_ref, acc_ref):
        if nk == 1:
            o_ref[...] = _mm(a_ref[...], b_ref[...], dims).astype(o_ref.dtype)
            return
        k = pl.program_id(2)

        @pl.when(k == 0)
        def _():
            acc_ref[...] = _mm(a_ref[...], b_ref[...], dims)

        @pl.when(k > 0)
        def _():
            acc_ref[...] += _mm(a_ref[...], b_ref[...], dims)

        @pl.when(k == nk - 1)
        def _():
            o_ref[...] = acc_ref[...].astype(o_ref.dtype)

    hbm = pl.BlockSpec(memory_space=pl.ANY)
    outs = pl.pallas_call(
        body, name=name, grid=grid,
        in_specs=[a_spec, b_spec] + [hbm] * (nx + npv),
        out_specs=[pl.BlockSpec((tm, tn), lambda i, j, k: (i, j))] + [hbm] * nx,
        out_shape=[_sds((M, N), out_dtype)] + (exchange.out_shape if exchange else []),
        input_output_aliases={2 + nx + t: 1 + t for t in range(npv)},
        scratch_shapes=[pltpu.VMEM((tm, tn), F32)] + (exchange.scratch if exchange else []),
        compiler_params=_cp(3),
    )(a, b, *ex_src, *ex_prev)
    return outs[0] if exchange is None else (outs[0], list(outs[1:]))


def _norm_mod(x, g, scale, shift, tm=_ROW_BLOCK):
    S, D = x.shape

    def body(x_ref, g_ref, sc_ref, sh_ref, h_ref):
        xv = x_ref[...]
        r = lax.rsqrt(jnp.mean(xv * xv, axis=1, keepdims=True) + EPS)
        h_ref[...] = ((xv * r * g_ref[...]) * (1.0 + sc_ref[...]) + sh_ref[...]).astype(h_ref.dtype)

    return pl.pallas_call(
        body, name="norm_mod", grid=(S // tm,),
        in_specs=[_rb(tm, D), _fs((1, D)), _fs((1, D)), _fs((1, D))],
        out_specs=_rb(tm, D), out_shape=_sds((S, D), _MXU_DTYPE), compiler_params=_cp(1),
    )(x, g, scale, shift)


def _norm_mod_bwd(x, dh, dxn, g, scale, tm=_ROW_BLOCK, exchange=None, ex_src=(), ex_prev=()):
    S, D = x.shape
    nx = len(ex_src)

    def body(*refs):
        if exchange is None:
            compute(*refs)
            return
        src, xout, sems = refs[5:5 + nx], refs[9 + 2 * nx:9 + 3 * nx], refs[9 + 3 * nx:]

        @pl.when(pl.program_id(0) == 0)
        def _():
            exchange.start(src, xout, sems)

        compute(*refs[:5], *refs[5 + 2 * nx:9 + 2 * nx])

        @pl.when(pl.program_id(0) == S // tm - 1)
        def _():
            exchange.wait(src, xout, sems)

    def compute(x_ref, dh_ref, dxn_ref, g_ref, sc_ref, dx_ref, dsh_ref, dsc_ref, dg_ref):
        @pl.when(pl.program_id(0) == 0)
        def _():
            dsh_ref[...] = jnp.zeros_like(dsh_ref)
            dsc_ref[...] = jnp.zeros_like(dsc_ref)
            dg_ref[...] = jnp.zeros_like(dg_ref)

        xv, dhv, gv = x_ref[...], dh_ref[...], g_ref[...]
        r = lax.rsqrt(jnp.mean(xv * xv, axis=1, keepdims=True) + EPS)
        xh = xv * r
        one_sc = 1.0 + sc_ref[...]
        dsh_ref[...] += jnp.sum(dhv, axis=0, keepdims=True)
        dsc_ref[...] += jnp.sum(dhv * xh * gv, axis=0, keepdims=True)
        dg_ref[...] += jnp.sum(dhv * one_sc * xh, axis=0, keepdims=True)
        dxh = dhv * (gv * one_sc)
        dx_ref[...] = r * (dxh - xh * jnp.mean(dxh * xh, axis=1, keepdims=True)) + dxn_ref[...]

    hbm = pl.BlockSpec(memory_space=pl.ANY)
    outs = pl.pallas_call(
        body, name="norm_mod_bwd", grid=(S // tm,),
        in_specs=[_rb(tm, D), _rb(tm, D), _rb(tm, D), _fs((1, D)), _fs((1, D))] + [hbm] * (2 * nx),
        out_specs=[_rb(tm, D), _fs((1, D)), _fs((1, D)), _fs((1, D))] + [hbm] * nx,
        out_shape=[_sds((S, D), F32)] + [_sds((1, D), F32)] * 3 + (exchange.out_shape if exchange else []),
        input_output_aliases={5 + nx + t: 4 + t for t in range(nx)},
        scratch_shapes=exchange.scratch if exchange else [], compiler_params=_cp(1),
    )(x, dh, dxn, g, scale, *ex_src, *ex_prev)
    return outs[:4], list(outs[4:])


def _head_sum_matrix():
    r = jnp.arange(SB_W)
    return (r[:, None] // SB_HD == r[None, :] // SB_HD).astype(BF16)


def _sb_prep(p, gq_t, gk_t, tm=_ROW_BLOCK):
    S = p.shape[0]
    bd = _head_sum_matrix()

    def body(q_ref, k_ref, gq_ref, gk_ref, bd_ref, qn_ref, kn_ref):
        for src, g_ref, dst in ((q_ref, gq_ref, qn_ref), (k_ref, gk_ref, kn_ref)):
            v = src[...]
            ms = _mm_xl(v * v, bd_ref[...]) * (1.0 / SB_HD)
            dst[...] = (v * lax.rsqrt(ms + EPS) * g_ref[...]).astype(dst.dtype)

    return pl.pallas_call(
        body, name="sb_prep", grid=(S // tm,),
        in_specs=[_rb(tm, SB_W, C_SB_Q // SB_W), _rb(tm, SB_W, C_SB_K // SB_W),
                  _fs((1, SB_W)), _fs((1, SB_W)), _fs((SB_W, SB_W))],
        out_specs=[_rb(tm, SB_W), _rb(tm, SB_W)],
        out_shape=[_sds((S, SB_W), _MXU_DTYPE)] * 2, compiler_params=_cp(1),
    )(p, p, gq_t, gk_t, bd)


def _sb_prep_bwd(p, dqn, dkn, gq_t, gk_t, tm=_ROW_BLOCK):
    S = p.shape[0]
    bd = _head_sum_matrix()

    def body(q_ref, k_ref, dqn_ref, dkn_ref, gq_ref, gk_ref, bd_ref, dq_ref, dk_ref, dgq_ref, dgk_ref):
        @pl.when(pl.program_id(0) == 0)
        def _():
            dgq_ref[...] = jnp.zeros_like(dgq_ref)
            dgk_ref[...] = jnp.zeros_like(dgk_ref)

        for src, dn_ref, g_ref, dst, dg_ref in ((q_ref, dqn_ref, gq_ref, dq_ref, dgq_ref),
                                                (k_ref, dkn_ref, gk_ref, dk_ref, dgk_ref)):
            v, dn = src[...], dn_ref[...]
            r = lax.rsqrt(_mm_xl(v * v, bd_ref[...]) * (1.0 / SB_HD) + EPS)
            vh = v * r
            dg_ref[...] += jnp.sum(dn * vh, axis=0, keepdims=True)
            dvh = dn * g_ref[...]
            m = _mm_xl(dvh * vh, bd_ref[...]) * (1.0 / SB_HD)
            dst[...] = (r * (dvh - vh * m)).astype(dst.dtype)

    return pl.pallas_call(
        body, name="sb_prep_bwd", grid=(S // tm,),
        in_specs=[_rb(tm, SB_W, C_SB_Q // SB_W), _rb(tm, SB_W, C_SB_K // SB_W), _rb(tm, SB_W), _rb(tm, SB_W),
                  _fs((1, SB_W)), _fs((1, SB_W)), _fs((SB_W, SB_W))],
        out_specs=[_rb(tm, SB_W), _rb(tm, SB_W), _fs((1, SB_W)), _fs((1, SB_W))],
        out_shape=[_sds((S, SB_W), _MXU_DTYPE)] * 2 + [_sds((1, SB_W), F32)] * 2, compiler_params=_cp(1),
    )(p, p, dqn, dkn, gq_t, gk_t, bd)


def _sb_consts():
    r, c = _iota2((QB, QB), 0), _iota2((QB, QB), 1)
    lane = _iota2((1, LANES), 1)
    return r, c, lane


def _sb_fwd(qn, kn, p, gather=None, g_src=()):
    S = qn.shape[0]
    scale = 1.0 / math.sqrt(SB_HD)
    grid = (SB_W // LANES, S // QB)
    nx = len(g_src)

    def body(*refs):
        if gather is None:
            compute(*refs)
            return
        src, gout, sems = refs[3:3 + nx], refs[6 + nx:6 + 2 * nx], refs[6 + 2 * nx:]
        hp, i = pl.program_id(0), pl.program_id(1)

        @pl.when(jnp.logical_and(hp == 0, i == 0))
        def _():
            gather.start(src, gout, sems)

        compute(*refs[:3], *refs[3 + nx:6 + nx])

        @pl.when(jnp.logical_and(hp == grid[0] - 1, i == grid[1] - 1))
        def _():
            gather.wait(src, gout, sems)

    def compute(q_ref, k_ref, v_ref, o_ref, tot_ref, nb_ref):
        i = pl.program_id(1)
        r, c, lane = _sb_consts()
        u_gt = (r > c).astype(BF16)
        strict = jnp.concatenate([c < r, c < r], axis=0)
        q = q_ref[...]
        mask0 = (lane // SB_HD) == 0
        zero = jnp.zeros_like(q)
        qh = jnp.concatenate([jnp.where(mask0, q, zero), jnp.where(mask0, zero, q)], axis=0)

        def block(off, carry, diagonal):
            o, run = carry
            kj = k_ref[pl.ds(off, QB), :]
            vj = v_ref[pl.ds(off, QB), :].astype(_MXU_DTYPE)
            z = _mm(qh, kj, _NT) * scale
            sp = _softplus(z)
            sp_m = jnp.where(strict, sp, 0.0) if diagonal else sp
            later = _mm_xl2(sp_m, u_gt)
            w = jnp.exp((z - sp) - later - run)
            if diagonal:
                w = jnp.where(strict, w, 0.0)
            return o + _mm(w, vj), run + jnp.sum(sp_m, axis=1, keepdims=True)

        init = (jnp.zeros((2 * QB, LANES), F32), jnp.zeros((2 * QB, 1), F32))
        carry = block(pl.multiple_of(i * QB, QB), init, True)
        st = lax.while_loop(
            lambda st: jnp.logical_and(st[0] <= i, jnp.min(st[2]) < _SB_DEAD),
            lambda st: (st[0] + 1,) + block(pl.multiple_of((i - st[0]) * QB, QB), st[1:], False),
            (jnp.int32(1),) + carry)
        o_ref[...] = jnp.where(mask0, st[1][:QB], st[1][QB:])
        tot_ref[...] = jnp.where(mask0, st[2][:QB], st[2][QB:])
        nb_ref[...] = jnp.zeros((8, LANES), F32) + st[0].astype(F32)

    blk = pl.BlockSpec((QB, LANES), lambda hp, i: (i, hp))
    hbm = pl.BlockSpec(memory_space=pl.ANY)
    outs = pl.pallas_call(
        body, name="sb_fwd", grid=grid,
        in_specs=[blk, pl.BlockSpec((S, LANES), lambda hp, i: (0, hp)),
                  pl.BlockSpec((S, LANES), lambda hp, i: (0, C_SB_V // LANES + hp))] + [hbm] * nx,
        out_specs=[blk, blk, pl.BlockSpec((8, LANES), lambda hp, i: (i, hp))] + [hbm] * nx,
        out_shape=[_sds((S, SB_W), F32), _sds((S, SB_W), F32), _sds((8 * S // QB, SB_W), F32)]
        + (gather.out_shape if gather else []),
        scratch_shapes=gather.scratch if gather else [], compiler_params=_cp(2),
    )(qn, kn, p, *g_src)
    return outs[:3], list(outs[3:])


def _sb_bwd(qn, kn, p, do, tot, nblocks, exchange=None, ex_src=(), early=None, early_src=()):
    S = qn.shape[0]
    scale = 1.0 / math.sqrt(SB_HD)
    grid = (SB_W // LANES, S // QB)
    nx, ne = len(ex_src), len(early_src)
    NI = 6

    def body(*refs):
        if exchange is None:
            compute(*refs)
            return
        src, src2 = refs[NI:NI + nx], refs[NI + nx:NI + nx + ne]
        o0 = NI + nx + ne
        xout, sems, sems2 = refs[o0 + 3:o0 + 3 + nx], refs[o0 + 3 + nx:o0 + 6 + nx], refs[o0 + 6 + nx:]
        hp, i = pl.program_id(0), pl.program_id(1)

        @pl.when(jnp.logical_and(hp == 0, i == 0))
        def _():
            exchange.start(src, xout, sems)
            if early is not None:
                early.start(src2, xout[nx - ne:], sems2)

        compute(*refs[:NI], *refs[o0:o0 + 3])

        @pl.when(jnp.logical_and(hp == grid[0] - 1, i == grid[1] - 1))
        def _():
            exchange.wait(src, xout, sems)
            if early is not None:
                early.wait(src2, xout[nx - ne:], sems2)

    def compute(q_ref, k_ref, v_ref, do_ref, tot_ref, nb_ref, dq_ref, dk_ref, dv_ref):
        i = pl.program_id(1)

        @pl.when(i == 0)
        def _():
            dk_ref[...] = jnp.zeros_like(dk_ref)
            dv_ref[...] = jnp.zeros_like(dv_ref)

        r, c, lane = _sb_consts()
        u_le = (r <= c).astype(BF16)
        u_lt = (r < c).astype(BF16)
        strict = jnp.concatenate([c < r, c < r], axis=0)
        q = q_ref[...]
        do = do_ref[...].astype(_MXU_DTYPE)
        mask0 = (lane // SB_HD) == 0
        zero, zero_do = jnp.zeros_like(q), jnp.zeros_like(do)
        qh = jnp.concatenate([jnp.where(mask0, q, zero), jnp.where(mask0, zero, q)], axis=0)
        doh = jnp.concatenate([jnp.where(mask0, do, zero_do), jnp.where(mask0, zero_do, do)], axis=0)

        tot_pair = tot_ref[...]
        tot = jnp.concatenate([jnp.max(jnp.where(mask0, tot_pair, 0.0), axis=1, keepdims=True),
                               jnp.max(jnp.where(mask0, 0.0, tot_pair), axis=1, keepdims=True)], axis=0)
        nb = jnp.clip(jnp.max(nb_ref[...]).astype(jnp.int32), 1, i + 1)
        first = i + 1 - nb

        def block(off, carry, diagonal):
            dq, pre_sp, pre_e = carry
            kj = k_ref[pl.ds(off, QB), :]
            vj = v_ref[pl.ds(off, QB), :].astype(_MXU_DTYPE)
            z = _mm(qh, kj, _NT) * scale
            sp = _softplus(z)
            a = z - sp
            sp_m = jnp.where(strict, sp, 0.0) if diagonal else sp
            incl = _mm_xl2(sp_m, u_le)
            w = jnp.exp(a - ((tot - pre_sp) - incl))
            if diagonal:
                w = jnp.where(strict, w, 0.0)
            e = w * _mm(doh, vj, _NT)
            db = pre_e + _mm_xl2(e, u_lt)
            dz = (e - jnp.exp(a) * (e + db)) * scale
            if diagonal:
                dz = jnp.where(strict, dz, 0.0)
            dk_ref[pl.ds(off, QB), :] += _mm(dz, qh, _TN)
            dv_ref[pl.ds(off, QB), :] += _mm(w, doh, _TN)
            return (dq + _mm(dz, kj), pre_sp + jnp.sum(sp_m, axis=1, keepdims=True),
                    pre_e + jnp.sum(e, axis=1, keepdims=True))

        zero_col = jnp.zeros((2 * QB, 1), F32)
        init = (jnp.zeros((2 * QB, LANES), F32), zero_col, zero_col)
        carry = lax.fori_loop(first, i, lambda j, cr: block(pl.multiple_of(j * QB, QB), cr, False), init)
        carry = block(pl.multiple_of(i * QB, QB), carry, True)
        dq_ref[...] = jnp.where(mask0, carry[0][:QB], carry[0][QB:])

    blk = pl.BlockSpec((QB, LANES), lambda hp, i: (i, hp))
    full = pl.BlockSpec((S, LANES), lambda hp, i: (0, hp))
    hbm = pl.BlockSpec(memory_space=pl.ANY)
    outs = pl.pallas_call(
        body, name="sb_bwd", grid=grid,
        in_specs=[blk, full, pl.BlockSpec((S, LANES), lambda hp, i: (0, C_SB_V // LANES + hp)), blk, blk,
                  pl.BlockSpec((8, LANES), lambda hp, i: (i, hp))] + [hbm] * (nx + ne),
        out_specs=[blk, full, full] + [hbm] * nx,
        out_shape=[_sds((S, SB_W), F32)] * 3 + (exchange.out_shape if exchange else []),
        scratch_shapes=(exchange.scratch if exchange else []) + (early.scratch if early else []),
        compiler_params=_cp(2),
    )(qn, kn, p, do, tot, nblocks, *ex_src, *early_src)
    return outs[:3], outs[3:]


def _dn_prep(p, conv_w, a_row, dtb_row, tm=_ROW_BLOCK):
    S = p.shape[0]
    W3 = 3 * DN_W
    nhalo = tm // 8

    def body(x_ref, halo_ref, w_ref, ba_ref, a_ref, dtb_ref, qkv_ref, bb_ref, gc_ref, gl_ref):
        i = pl.program_id(0)
        halo = jnp.where(i > 0, halo_ref[...], 0.0)
        xf = jnp.concatenate([halo, x_ref[...]], axis=0)
        acc = jnp.zeros((tm, W3), F32)
        for k in range(CONV_K):
            sh = CONV_K - 1 - k
            xs = xf if sh == 0 else pltpu.roll(xf, sh, 0)
            acc = acc + xs[8:, :] * w_ref[k:k + 1, :]
        s = _silu(acc)
        for gi in range(2 * DN_HEADS):
            sl = slice(gi * LANES, (gi + 1) * LANES)
            sg = s[:, sl]
            rinv = lax.rsqrt(jnp.sum(sg * sg, axis=1, keepdims=True) + EPS)
            qkv_ref[:, sl] = sg * rinv * (DN_HD ** -0.5 if gi < DN_HEADS else 1.0)
        qkv_ref[:, 2 * DN_W:] = s[:, 2 * DN_W:]

        ba = ba_ref[...]
        beta = _sigmoid(ba)
        g = -jnp.exp(a_ref[...]) * _softplus(ba + dtb_ref[...])
        lr, lc = _iota2((LANES, DN_W), 0), _iota2((LANES, DN_W), 1)
        sel_b = (lr == lc // LANES).astype(BF16)
        sel_g = (lr == lc // LANES + DN_HEADS).astype(BF16)
        bb_ref[...] = _mm_xl(beta, sel_b)
        graw = _mm_xl(g, sel_g)
        rr, cc = _iota2((tm, tm), 0), _iota2((tm, tm), 1)
        tri = jnp.logical_and(rr >= cc, rr // CHUNK == cc // CHUNK).astype(BF16)
        gc = _mm_xr(tri, graw)
        last = (cc == (rr // CHUNK) * CHUNK + (CHUNK - 1)).astype(BF16)
        gc_ref[...] = gc
        gl_ref[...] = _mm_xr(last, gc)

    return pl.pallas_call(
        body, name="dn_prep", grid=(S // tm,),
        in_specs=[_rb(tm, W3, 0), pl.BlockSpec((8, W3), lambda i: (jnp.maximum(i * nhalo - 1, 0), 0)),
                  _fs((CONV_K, W3)), _rb(tm, LANES, (p.shape[1] - LANES) // LANES),
                  _fs((1, LANES)), _fs((1, LANES))],
        out_specs=[_rb(tm, W3), _rb(tm, DN_W), _rb(tm, DN_W), _rb(tm, DN_W)],
        out_shape=[_sds((S, W3), F32)] + [_sds((S, DN_W), F32)] * 3, compiler_params=_cp(1),
    )(p, p, conv_w, p, a_row, dtb_row)


def _heads(ref, base=0):
    return jnp.stack([ref[:, base + h * LANES:base + (h + 1) * LANES] for h in range(DN_HEADS)])


def _per_head(const):
    return jnp.broadcast_to(const[None], (DN_HEADS,) + const.shape)


def _dn_chunk_terms(q, k, v, beta, gc, gl):
    r, c = _iota2((CHUNK, CHUNK), 0), _iota2((CHUNK, CHUNK), 1)
    tril, strict = r >= c, r > c
    gcol = _mm_xl(gc, _per_head(jnp.full((LANES, CHUNK), 1.0 / LANES, F32)), _BNN)
    grow = _mm_xr(_per_head(jnp.full((CHUNK, LANES), 1.0 / LANES, F32)), gc, _BNT)
    dec = jnp.where(tril, jnp.exp(jnp.where(tril, gcol - grow, 0.0)), 0.0)
    gam = jnp.exp(gc)
    dlt = jnp.exp(gl - gc)
    kb, vb = k * beta, v * beta
    pm = _mm(kb, k, _BNT)
    qk = _mm(q, k, _BNT)
    m = jnp.where(strict, pm * dec, 0.0)
    a = jnp.where(tril, qk * dec, 0.0)
    return dict(tril=tril, strict=strict, dec=dec, gam=gam, dlt=dlt, kb=kb, vb=vb, m=m, a=a)


def _dn_fwd(qkv, bb, gcb, glb, gather=None, g_src=()):
    S = qkv.shape[0]
    N = S // CHUNK
    nx = len(g_src)

    def body(*refs):
        if gather is None:
            compute(*refs)
            return
        src, gout, sems = refs[4:4 + nx], refs[7 + nx:7 + 2 * nx], refs[8 + 2 * nx:]

        @pl.when(pl.program_id(0) == 0)
        def _():
            gather.start(src, gout, sems)

        compute(*refs[:4], *refs[4 + nx:7 + nx], refs[7 + 2 * nx])

        @pl.when(pl.program_id(0) == N - 1)
        def _():
            gather.wait(src, gout, sems)

    def compute(qkv_ref, bb_ref, gc_ref, gl_ref, o_ref, t_ref, sall_ref, s_scr):
        @pl.when(pl.program_id(0) == 0)
        def _():
            s_scr[...] = jnp.zeros_like(s_scr)

        r, c = _iota2((CHUNK, CHUNK), 0), _iota2((CHUNK, CHUNK), 1)
        eye = (r == c).astype(F32)
        q, k, v = _heads(qkv_ref), _heads(qkv_ref, DN_W), _heads(qkv_ref, 2 * DN_W)
        beta, gc, gl = _heads(bb_ref), _heads(gc_ref), _heads(gl_ref)
        s_prev = s_scr[...]
        sall_ref[0] = s_prev.astype(sall_ref.dtype)
        s0 = s_prev.astype(sall_ref.dtype).astype(F32)
        t = _dn_chunk_terms(q, k, v, beta, gc, gl)
        pw = -t["m"]
        tinv = eye + pw
        for _ in range(5):
            pw = _mm3(pw, pw, _BNN)
            tinv = tinv + _mm3(tinv, pw, _BNN)
        t_ref[...] = tinv
        u = _mm3(tinv, t["vb"], _BNN)
        w = _mm3(tinv, t["kb"] * t["gam"], _BNN)
        vn = u - _mm(w, s0, _BNN)
        o = _mm(q * t["gam"], s0, _BNN) + _mm(t["a"], vn, _BNN)
        for h in range(DN_HEADS):
            o_ref[:, h * LANES:(h + 1) * LANES] = o[h]
        egl = jnp.exp(jnp.concatenate([gl, gl], axis=1))
        s_scr[...] = s_prev * egl + _mm(k * t["dlt"], vn, _BTN)

    hbm = pl.BlockSpec(memory_space=pl.ANY)
    outs = pl.pallas_call(
        body, name="dn_fwd", grid=(N,),
        in_specs=[_rb(CHUNK, 3 * DN_W), _rb(CHUNK, DN_W), _rb(CHUNK, DN_W), _rb(CHUNK, DN_W)] + [hbm] * nx,
        out_specs=[_rb(CHUNK, DN_W), pl.BlockSpec((DN_HEADS, CHUNK, CHUNK), lambda n: (0, n, 0)),
                   pl.BlockSpec((1, DN_HEADS, DN_HD, DN_HD), lambda n: (n, 0, 0, 0))] + [hbm] * nx,
        out_shape=[_sds((S, DN_W), F32), _sds((DN_HEADS, S, CHUNK), F32),
                   _sds((N, DN_HEADS, DN_HD, DN_HD), _MXU_DTYPE)] + (gather.out_shape if gather else []),
        scratch_shapes=[pltpu.VMEM((DN_HEADS, DN_HD, DN_HD), F32)] + (gather.scratch if gather else []),
        compiler_params=_cp(1),
    )(qkv, bb, gcb, glb, *g_src)
    return outs[:3], outs[3:]


def _dn_bwd(qkv, bb, gcb, glb, tinv_all, sall, do, exchange=None, ex_src=(), early=None, early_src=()):
    S = qkv.shape[0]
    N = S // CHUNK
    nx = len(ex_src)
    NI = 7

    def body(*refs):
        if exchange is None:
            compute(*refs)
            return
        src, src2 = refs[NI:NI + nx], refs[NI + nx:NI + 2 * nx]
        o0 = NI + 2 * nx
        xout, ds_scr = refs[o0 + 3:o0 + 3 + nx], refs[o0 + 3 + nx]
        sems, sems2 = refs[o0 + 4 + nx:o0 + 7 + nx], refs[o0 + 7 + nx:]

        @pl.when(pl.program_id(0) == 0)
        def _():
            exchange.start(src, xout, sems)
            early.start(src2, xout, sems2)

        compute(*refs[:NI], *refs[o0:o0 + 3], ds_scr)

        @pl.when(pl.program_id(0) == N - 1)
        def _():
            exchange.wait(src, xout, sems)
            early.wait(src2, xout, sems2)

    def compute(qkv_ref, bb_ref, gc_ref, gl_ref, t_ref, sall_ref, do_ref, dqkv_ref, dbb_ref, dg_ref, ds_scr):
        @pl.when(pl.program_id(0) == 0)
        def _():
            ds_scr[...] = jnp.zeros_like(ds_scr)

        r, c = _iota2((CHUNK, CHUNK), 0), _iota2((CHUNK, CHUNK), 1)
        eye = (r == c).astype(F32)
        u_ge = (c >= r).astype(F32)
        last_row = _iota2((CHUNK, LANES), 0) == CHUNK - 1
        eye_h, u_ge_h = _per_head(eye), _per_head(u_ge)
        q, k, v = _heads(qkv_ref), _heads(qkv_ref, DN_W), _heads(qkv_ref, 2 * DN_W)
        beta, gc, gl = _heads(bb_ref), _heads(gc_ref), _heads(gl_ref)
        tinv = t_ref[...]
        s0 = sall_ref[0].astype(F32)
        do = _heads(do_ref)
        ds1 = ds_scr[...]
        t = _dn_chunk_terms(q, k, v, beta, gc, gl)
        gam, dlt, kb, vb, dec = t["gam"], t["dlt"], t["kb"], t["vb"], t["dec"]
        kbg = kb * gam
        u = _mm3(tinv, vb, _BNN)
        w = _mm3(tinv, kbg, _BNN)
        vn = u - _mm(w, s0, _BNN)
        qg, kd = q * gam, k * dlt
        egl = jnp.exp(gl)
        egl2 = jnp.concatenate([egl, egl], axis=1)

        dvn = _mm(t["a"], do, _BTN) + _mm(kd, ds1, _BNN)
        da = jnp.where(t["tril"], _mm(do, vn, _BNT), 0.0)
        dqg = _mm(do, s0, _BNT)
        dkd = _mm(vn, ds1, _BNT)
        dw = -_mm(dvn, s0, _BNT)
        ds_scr[...] = _mm(qg, do, _BTN) + egl2 * ds1 - _mm(w, dvn, _BTN)
        tt = _mm_xr(eye_h, tinv, _BNT)
        dvb = _mm3(tt, dvn, _BNN)
        dkbg = _mm3(tt, dw, _BNN)
        dm = -jnp.where(t["strict"], _mm(dvb, u, _BNT) + _mm(dkbg, w, _BNT), 0.0)
        dpm = dm * dec
        dqk = da * dec
        dkb = dkbg * gam + _mm(dpm, k, _BNN)
        dk = dkd * dlt + _mm(dpm, kb, _BTN) + _mm(dqk, q, _BTN) + dkb * beta
        dq = dqg * gam + _mm(dqk, k, _BNN)
        dv = dvb * beta
        dbeta = jnp.sum(dkb * k, axis=2, keepdims=True) + jnp.sum(dvb * v, axis=2, keepdims=True)
        dgam = jnp.sum(dqg * q, axis=2, keepdims=True) + jnp.sum(dkbg * kb, axis=2, keepdims=True)
        ddlt = jnp.sum(dkd * k, axis=2, keepdims=True)
        xm = dm * t["m"] + da * t["a"]
        xt = _mm_xr(eye_h, xm, _BNT)
        dgc = (dgam * gam - ddlt * dlt + jnp.sum(xm, axis=2, keepdims=True) - jnp.sum(xt, axis=2, keepdims=True))
        dgl = jnp.sum(ddlt * dlt, axis=1, keepdims=True) + jnp.sum(
            jnp.sum(ds1 * s0, axis=2, keepdims=True), axis=1, keepdims=True) * jnp.max(egl, axis=1, keepdims=True)
        dgc = dgc + jnp.where(last_row, dgl, 0.0)
        dg = _mm_xr(u_ge_h, dgc, _BNN)
        for h in range(DN_HEADS):
            sl = slice(h * LANES, (h + 1) * LANES)
            dqkv_ref[:, sl] = dq[h]
            dqkv_ref[:, DN_W + h * LANES:DN_W + (h + 1) * LANES] = dk[h]
            dqkv_ref[:, 2 * DN_W + h * LANES:2 * DN_W + (h + 1) * LANES] = dv[h]
            dbb_ref[:, sl] = jnp.broadcast_to(dbeta[h], (CHUNK, LANES))
            dg_ref[:, sl] = dg[h]

    rev = lambda w: pl.BlockSpec((CHUNK, w), lambda n: (N - 1 - n, 0))
    hbm = pl.BlockSpec(memory_space=pl.ANY)
    outs = pl.pallas_call(
        body, name="dn_bwd", grid=(N,),
        in_specs=[rev(3 * DN_W), rev(DN_W), rev(DN_W), rev(DN_W),
                  pl.BlockSpec((DN_HEADS, CHUNK, CHUNK), lambda n: (0, N - 1 - n, 0)),
                  pl.BlockSpec((1, DN_HEADS, DN_HD, DN_HD), lambda n: (N - 1 - n, 0, 0, 0)), rev(DN_W)]
        + [hbm] * (2 * nx),
        out_specs=[rev(3 * DN_W), rev(DN_W), rev(DN_W)] + [hbm] * nx,
        out_shape=[_sds((S, 3 * DN_W), F32), _sds((S, DN_W), F32), _sds((S, DN_W), F32)]
        + (exchange.out_shape if exchange else []),
        scratch_shapes=[pltpu.VMEM((DN_HEADS, DN_HD, DN_HD), F32)]
        + (exchange.scratch + early.scratch if exchange else []),
        compiler_params=_cp(1),
    )(qkv, bb, gcb, glb, tinv_all, sall, do, *ex_src, *early_src)
    return outs[:3], list(outs[3:])


def _dn_prep_bwd_a(p, dqkv, dbb, dgb, conv_w, a_row, dtb_row, tm=_ROW_BLOCK):
    S, PC = p.shape
    W3 = 3 * DN_W
    nhalo = tm // 8

    def body(x_ref, halo_ref, w_ref, ba_ref, a_ref, dtb_ref, dqkv_ref, dbb_ref, dgb_ref,
             dc_ref, dba_ref, dal_ref, ddt_ref):
        i = pl.program_id(0)

        @pl.when(i == 0)
        def _():
            dal_ref[...] = jnp.zeros_like(dal_ref)
            ddt_ref[...] = jnp.zeros_like(ddt_ref)

        halo = jnp.where(i > 0, halo_ref[...], 0.0)
        xf = jnp.concatenate([halo, x_ref[...]], axis=0)
        acc = jnp.zeros((tm, W3), F32)
        for k in range(CONV_K):
            sh = CONV_K - 1 - k
            xs = xf if sh == 0 else pltpu.roll(xf, sh, 0)
            acc = acc + xs[8:, :] * w_ref[k:k + 1, :]
        s = _silu(acc)
        ds_act = _dsilu(acc)
        for gi in range(2 * DN_HEADS):
            sl = slice(gi * LANES, (gi + 1) * LANES)
            sg = s[:, sl]
            rinv = lax.rsqrt(jnp.sum(sg * sg, axis=1, keepdims=True) + EPS)
            nh = sg * rinv
            dn = dqkv_ref[:, sl] * (DN_HD ** -0.5 if gi < DN_HEADS else 1.0)
            dsg = rinv * (dn - nh * jnp.sum(dn * nh, axis=1, keepdims=True))
            dc_ref[:, sl] = dsg * ds_act[:, sl]
        dc_ref[:, 2 * DN_W:] = dqkv_ref[:, 2 * DN_W:] * ds_act[:, 2 * DN_W:]

        ba = ba_ref[...]
        beta = _sigmoid(ba)
        ea = jnp.exp(a_ref[...])
        pre = ba + dtb_ref[...]
        g = -ea * _softplus(pre)
        lr, lc = _iota2((DN_W, LANES), 0), _iota2((DN_W, LANES), 1)
        pick_b = jnp.where(lc == lr // LANES, 1.0 / LANES, 0.0)
        pick_g = jnp.where(lc == lr // LANES + DN_HEADS, 1.0 / LANES, 0.0)
        dbeta = _mm_xl(dbb_ref[...], pick_b)
        dg = _mm_xl(dgb_ref[...], pick_g)
        lane = _iota2((1, LANES), 1)
        da = dg * (-ea) * _sigmoid(pre)
        dba_ref[...] = jnp.where(lane < DN_HEADS, dbeta * beta * (1.0 - beta),
                                 jnp.where(lane < 2 * DN_HEADS, da, 0.0)).astype(dba_ref.dtype)
        dal_ref[...] += jnp.sum(dg * g, axis=0, keepdims=True)
        ddt_ref[...] += jnp.sum(da, axis=0, keepdims=True)

    return pl.pallas_call(
        body, name="dn_prep_bwd_a", grid=(S // tm,),
        in_specs=[_rb(tm, W3, 0), pl.BlockSpec((8, W3), lambda i: (jnp.maximum(i * nhalo - 1, 0), 0)),
                  _fs((CONV_K, W3)), _rb(tm, LANES, (PC - LANES) // LANES), _fs((1, LANES)), _fs((1, LANES)),
                  _rb(tm, W3), _rb(tm, DN_W), _rb(tm, DN_W)],
        out_specs=[_rb(tm, W3), _rb(tm, LANES), _fs((1, LANES)), _fs((1, LANES))],
        out_shape=[_sds((S, W3), F32), _sds((S, LANES), _MXU_DTYPE), _sds((1, LANES), F32), _sds((1, LANES), F32)],
        compiler_params=_cp(1),
    )(p, p, conv_w, p, a_row, dtb_row, dqkv, dbb, dgb)


def _dn_prep_bwd_b(p, dc, conv_w, tm=_ROW_BLOCK):
    S = p.shape[0]
    W3 = 3 * DN_W
    nhalo = tm // 8
    nblk = S // tm

    def body(x_ref, xh_ref, dc_ref, dch_ref, w_ref, dx_ref, dw_ref):
        i = pl.program_id(0)

        @pl.when(i == 0)
        def _():
            dw_ref[...] = jnp.zeros_like(dw_ref)

        dcv = dc_ref[...]
        xf = jnp.concatenate([jnp.where(i > 0, xh_ref[...], 0.0), x_ref[...]], axis=0)
        df = jnp.concatenate([dcv, jnp.where(i < nblk - 1, dch_ref[...], 0.0)], axis=0)
        acc = jnp.zeros((tm, W3), F32)
        for k in range(CONV_K):
            sh = CONV_K - 1 - k
            xs = xf if sh == 0 else pltpu.roll(xf, sh, 0)
            dw_ref[k:k + 1, :] += jnp.sum(dcv * xs[8:, :], axis=0, keepdims=True)
            ds = df if sh == 0 else pltpu.roll(df, tm + 8 - sh, 0)
            acc = acc + ds[:tm, :] * w_ref[k:k + 1, :]
        dx_ref[...] = acc.astype(dx_ref.dtype)

    return pl.pallas_call(
        body, name="dn_prep_bwd_b", grid=(nblk,),
        in_specs=[_rb(tm, W3, 0), pl.BlockSpec((8, W3), lambda i: (jnp.maximum(i * nhalo - 1, 0), 0)),
                  _rb(tm, W3), pl.BlockSpec((8, W3), lambda i: (jnp.minimum((i + 1) * nhalo, S // 8 - 1), 0)),
                  _fs((CONV_K, W3))],
        out_specs=[_rb(tm, W3), _fs((CONV_K, W3))],
        out_shape=[_sds((S, W3), _MXU_DTYPE), _sds((CONV_K, W3), F32)], compiler_params=_cp(1),
    )(p, p, dc, dc, conv_w)


def _gate(o_att, o_dn, p, gn, tm=_ROW_BLOCK):
    S = p.shape[0]

    def body(oa_ref, zs_ref, od_ref, zd_ref, gn_ref, osb_ref, odn_ref):
        osb_ref[...] = (oa_ref[...] * _silu(zs_ref[...])).astype(osb_ref.dtype)
        for h in range(DN_HEADS):
            sl = slice(h * LANES, (h + 1) * LANES)
            o = od_ref[:, sl]
            r = lax.rsqrt(jnp.mean(o * o, axis=1, keepdims=True) + EPS)
            odn_ref[:, sl] = (o * r * gn_ref[...] * _silu(zd_ref[:, sl])).astype(odn_ref.dtype)

    return pl.pallas_call(
        body, name="gate", grid=(S // tm,),
        in_specs=[_rb(tm, SB_W), _rb(tm, SB_W, C_SB_Z // SB_W), _rb(tm, DN_W), _rb(tm, DN_W, C_DN_Z // DN_W),
                  _fs((1, LANES))],
        out_specs=[_rb(tm, SB_W), _rb(tm, DN_W)],
        out_shape=[_sds((S, SB_W), _MXU_DTYPE), _sds((S, DN_W), _MXU_DTYPE)], compiler_params=_cp(1),
    )(o_att, p, o_dn, p, gn)


def _gate_bwd(db_sb, db_dn, wb_sb, wb_dn, o_att, o_dn, p, gn, tm=_ROW_BLOCK):
    S = p.shape[0]
    D = db_sb.shape[1]

    def body(dbs_ref, dbd_ref, ws_ref, wd_ref, oa_ref, zs_ref, od_ref, zd_ref, gn_ref,
             doa_ref, dzs_ref, dod_ref, dzd_ref, dgn_ref):
        @pl.when(pl.program_id(0) == 0)
        def _():
            dgn_ref[...] = jnp.zeros_like(dgn_ref)

        do_sb = _mm(dbs_ref[...], ws_ref[...], _NT)
        zs = zs_ref[...]
        doa_ref[...] = do_sb * _silu(zs)
        dzs_ref[...] = (do_sb * oa_ref[...] * _dsilu(zs)).astype(dzs_ref.dtype)
        do_dnn = _mm(dbd_ref[...], wd_ref[...], _NT)
        gnv = gn_ref[...]
        for h in range(DN_HEADS):
            sl = slice(h * LANES, (h + 1) * LANES)
            o, z, dout = od_ref[:, sl], zd_ref[:, sl], do_dnn[:, sl]
            r = lax.rsqrt(jnp.mean(o * o, axis=1, keepdims=True) + EPS)
            oh = o * r
            sz = _silu(z)
            dzd_ref[:, sl] = (dout * oh * gnv * _dsilu(z)).astype(dzd_ref.dtype)
            dgn_ref[...] += jnp.sum(dout * sz * oh, axis=0, keepdims=True)
            doh = dout * gnv * sz
            dod_ref[:, sl] = r * (doh - oh * jnp.mean(doh * oh, axis=1, keepdims=True))

    return pl.pallas_call(
        body, name="gate_bwd", grid=(S // tm,),
        in_specs=[_rb(tm, D), _rb(tm, D), _fs((SB_W, D)), _fs((DN_W, D)), _rb(tm, SB_W),
                  _rb(tm, SB_W, C_SB_Z // SB_W), _rb(tm, DN_W), _rb(tm, DN_W, C_DN_Z // DN_W), _fs((1, LANES))],
        out_specs=[_rb(tm, SB_W), _rb(tm, SB_W), _rb(tm, DN_W), _rb(tm, DN_W), _fs((1, LANES))],
        out_shape=[_sds((S, SB_W), F32), _sds((S, SB_W), _MXU_DTYPE), _sds((S, DN_W), F32),
                   _sds((S, DN_W), _MXU_DTYPE), _sds((1, LANES), F32)],
        compiler_params=_cp(1),
    )(db_sb, db_dn, wb_sb, wb_dn, o_att, p, o_dn, p, gn)


def _branch(o_sb, o_dnn, wb_sb, wb_dn, p, D, tm=_ROW_BLOCK):
    S = p.shape[0]

    def body(os_ref, od_ref, ws_ref, wd_ref, ms_ref, md_ref, y_ref, bs_ref, bd_ref):
        bs = _mm(os_ref[...], ws_ref[...])
        bdn = _mm(od_ref[...], wd_ref[...])
        bs_ref[...] = bs
        bd_ref[...] = bdn
        y_ref[...] = (_sigmoid(ms_ref[...]) * bs + _sigmoid(md_ref[...]) * bdn).astype(y_ref.dtype)

    return pl.pallas_call(
        body, name="branch", grid=(S // tm,),
        in_specs=[_rb(tm, SB_W), _rb(tm, DN_W), _fs((SB_W, D)), _fs((DN_W, D)),
                  _rb(tm, D, C_MG // D), _rb(tm, D, C_MG // D + 1)],
        out_specs=[_rb(tm, D), _rb(tm, D), _rb(tm, D)],
        out_shape=[_sds((S, D), _MXU_DTYPE), _sds((S, D), F32), _sds((S, D), F32)], compiler_params=_cp(1),
    )(o_sb, o_dnn, wb_sb, wb_dn, p, p)


def _out_proj(x, y, w_out, gate, tm=_ROW_BLOCK):
    S, D = x.shape

    def body(x_ref, y_ref, w_ref, g_ref, xn_ref, out_ref):
        out = _mm(y_ref[...], w_ref[...])
        out_ref[...] = out
        xn_ref[...] = x_ref[...] + g_ref[...] * out

    return pl.pallas_call(
        body, name="out_proj", grid=(S // tm,),
        in_specs=[_rb(tm, D), _rb(tm, D), _fs((D, D)), _fs((1, D))],
        out_specs=[_rb(tm, D), _rb(tm, D)],
        out_shape=[_sds((S, D), F32), _sds((S, D), F32)], compiler_params=_cp(1),
    )(x, y, w_out, gate)


def _out_bwd(dxn, out, gate, w_out, p, b_sb, b_dn, tm=_ROW_BLOCK):
    S, D = dxn.shape

    def body(dxn_ref, out_ref, g_ref, w_ref, ms_ref, md_ref, bs_ref, bd_ref,
             dout_ref, dbs_ref, dbd_ref, dm_ref, dgate_ref):
        @pl.when(pl.program_id(0) == 0)
        def _():
            dgate_ref[...] = jnp.zeros_like(dgate_ref)

        dxv = dxn_ref[...]
        dgate_ref[...] += jnp.sum(dxv * out_ref[...], axis=0, keepdims=True)
        dout = (g_ref[...] * dxv).astype(dout_ref.dtype)
        dout_ref[...] = dout
        dy = _mm(dout, w_ref[...], _NT)
        s1, s2 = _sigmoid(ms_ref[...]), _sigmoid(md_ref[...])
        dbs_ref[...] = (dy * s1).astype(dbs_ref.dtype)
        dbd_ref[...] = (dy * s2).astype(dbd_ref.dtype)
        dm_ref[:, :D] = (dy * bs_ref[...] * s1 * (1.0 - s1)).astype(dm_ref.dtype)
        dm_ref[:, D:] = (dy * bd_ref[...] * s2 * (1.0 - s2)).astype(dm_ref.dtype)

    return pl.pallas_call(
        body, name="out_bwd", grid=(S // tm,),
        in_specs=[_rb(tm, D), _rb(tm, D), _fs((1, D)), _fs((D, D)), _rb(tm, D, C_MG // D),
                  _rb(tm, D, C_MG // D + 1), _rb(tm, D), _rb(tm, D)],
        out_specs=[_rb(tm, D), _rb(tm, D), _rb(tm, D), _rb(tm, 2 * D), _fs((1, D))],
        out_shape=[_sds((S, D), _MXU_DTYPE)] * 3 + [_sds((S, 2 * D), _MXU_DTYPE), _sds((1, D), F32)],
        compiler_params=_cp(1),
    )(dxn, out, gate, w_out, p, p, b_sb, b_dn)


def _loss_head(xf, target, tm=_ROW_BLOCK):
    S, D = xf.shape

    def body(x_ref, t_ref, dy_ref, loss_ref):
        @pl.when(pl.program_id(0) == 0)
        def _():
            loss_ref[...] = jnp.zeros_like(loss_ref)

        e = x_ref[...] - t_ref[...]
        dy_ref[...] = e * (1.0 / D)
        row = jnp.sum(e * e, axis=1, keepdims=True) * (1.0 / D)
        loss_ref[...] += 0.5 * jnp.sum(row, axis=0, keepdims=True)

    return pl.pallas_call(
        body, name="loss_head", grid=(S // tm,),
        in_specs=[_rb(tm, D), _rb(tm, D)], out_specs=[_rb(tm, D), _fs((1, LANES))],
        out_shape=[_sds((S, D), F32), _sds((1, LANES), F32)], compiler_params=_cp(1),
    )(xf, target)


def _ada_fwd(c_all, ada_w, ada_b_sh):
    L, D, n = ada_w.shape
    B = c_all.shape[0]

    def body(c_ref, w_ref, b_ref, o_ref):
        sc = _silu(c_ref[...])
        o_ref[0] = _mm(sc, w_ref[0]) + b_ref[0]

    return pl.pallas_call(
        body, name="ada_fwd", grid=(L,),
        in_specs=[_fs((B, D)), pl.BlockSpec((1, D, n), lambda l: (l, 0, 0)), pl.BlockSpec((1, 1, n), lambda l: (l, 0, 0))],
        out_specs=pl.BlockSpec((1, B, n), lambda l: (l, 0, 0)),
        out_shape=_sds((L, B, n), F32), compiler_params=_cp(1),
    )(c_all, ada_w, ada_b_sh)


def _ada_bwd(c_all_t, dmod_sh):
    D, B = c_all_t.shape
    L, _, n = dmod_sh.shape

    def body(c_ref, d_ref, o_ref):
        acc = jnp.zeros((D, n), F32)
        for b in range(B):
            acc = acc + _silu(c_ref[:, b:b + 1]) * d_ref[0, b:b + 1, :]
        o_ref[0] = acc

    return pl.pallas_call(
        body, name="ada_bwd", grid=(L,),
        in_specs=[_fs((D, B)), pl.BlockSpec((1, B, n), lambda l: (l, 0, 0))],
        out_specs=pl.BlockSpec((1, D, n), lambda l: (l, 0, 0)),
        out_shape=_sds((L, D, n), F32), compiler_params=_cp(1),
    )(c_all_t, dmod_sh)


def _sum_parts(name, parts):
    P, R, C = parts.shape
    tr = _pick(R, max(16, min(512, (1 << 19) // (P * C))), 16) if R % 16 == 0 else R

    def body(p_ref, o_ref):
        acc = p_ref[0].astype(F32)
        for k in range(1, P):
            acc = acc + p_ref[k].astype(F32)
        o_ref[...] = acc

    return pl.pallas_call(
        body, name=name, grid=(R // tr,),
        in_specs=[pl.BlockSpec((P, tr, C), lambda i: (0, i, 0))], out_specs=_rb(tr, C),
        out_shape=_sds((R, C), F32), compiler_params=_cp(1),
    )(parts)


def _adamw(name, w, g, m, v):
    L, R, C = w.shape
    tr = _pick(R, 256, 8) if R % 8 == 0 else R
    c1 = 1.0 - ADAM_B1 ** ADAM_STEP
    c2 = 1.0 - ADAM_B2 ** ADAM_STEP

    def body(w_ref, g_ref, m_ref, v_ref, d_ref, mo_ref, vo_ref):
        gv = g_ref[...]
        mn = ADAM_B1 * m_ref[...] + (1.0 - ADAM_B1) * gv
        vn = ADAM_B2 * v_ref[...] + (1.0 - ADAM_B2) * (gv * gv)
        mo_ref[...] = mn
        vo_ref[...] = vn
        d_ref[...] = -ADAM_LR * ((mn / c1) / (jnp.sqrt(vn / c2) + ADAM_EPS) + ADAM_WD * w_ref[...])

    spec = pl.BlockSpec((1, tr, C), lambda l, i: (l, i, 0))
    return pl.pallas_call(
        body, name=name, grid=(L, R // tr),
        in_specs=[spec] * 4, out_specs=[spec] * 3, out_shape=[_sds((L, R, C), F32)] * 3, compiler_params=_cp(2),
    )(w, g, m, v)


def _ag_small(name, blk):
    R, C = blk.shape

    def body(x_ref, out_ref, send_sems, recv_sems, local_sem):
        x, y, c = lax.axis_index("x"), lax.axis_index("y"), lax.axis_index("c")
        me, sibling = (x, y, c), (x, y, 1 - c)
        chips = [(1 - x, y), (x, 1 - y), (1 - x, 1 - y)]

        def rows(px, py, pc):
            return out_ref.at[pl.ds((4 * px + 2 * py + pc) * R, R), :]

        def copy(k, block, to, src=None):
            return pltpu.make_async_remote_copy(
                src_ref=rows(*block) if src is None else src, dst_ref=rows(*block),
                send_sem=send_sems.at[k], recv_sem=recv_sems.at[k], device_id=to, device_id_type=MESH)

        mine = pltpu.make_async_copy(x_ref, rows(*me), local_sem)
        mine.start()
        first = [copy(0, me, sibling, src=x_ref)]
        first += [copy(1 + j, me, (*chip, c), src=x_ref) for j, chip in enumerate(chips)]
        for cp in first:
            cp.start()
        passed = [copy(4 + j, (*chip, c), sibling) for j, chip in enumerate(chips)]
        for j, chip in enumerate(chips):
            copy(1 + j, (*chip, c), me).wait_recv()
            passed[j].start()
        copy(0, sibling, me).wait_recv()
        for j, chip in enumerate(chips):
            copy(4 + j, (*chip, 1 - c), me).wait_recv()
        for cp in first + passed:
            cp.wait_send()
        mine.wait()

    return pl.pallas_call(
        body, name=name, out_shape=_sds((8 * R, C), blk.dtype),
        in_specs=[pl.BlockSpec(memory_space=pltpu.VMEM)], out_specs=pl.BlockSpec(memory_space=pltpu.VMEM),
        scratch_shapes=[pltpu.SemaphoreType.DMA((7,)), pltpu.SemaphoreType.DMA((7,)), pltpu.SemaphoreType.DMA],
    )(blk)


def _row_chunks(ts, row_axis):
    pieces = []
    for t, a in enumerate(ts):
        rows = a.shape[row_axis]
        n = 4 if rows >= 1024 else 1
        pieces += [(t, i * (rows // n), rows // n) for i in range(n)]
    return pieces


def _ag_weights_first(ts):
    nt = len(ts)
    pieces = _row_chunks(ts, 0)
    NP = len(pieces)
    sizes = [nr * ts[t].shape[1] for t, _, nr in pieces]
    split = next(pi for pi in range(NP + 1) if 2 * sum(sizes[:pi]) >= sum(sizes))

    def body(*refs):
        w, out = refs[:nt], refs[nt:2 * nt]
        send_sems, recv_sems, local_sems = refs[2 * nt:]
        x, y, c = lax.axis_index("x"), lax.axis_index("y"), lax.axis_index("c")
        me, sibling = (x, y, c), (x, y, 1 - c)
        mine = 2 * x + y
        chips = [(1 - x, y), (x, 1 - y), (1 - x, 1 - y)]

        def blk(t, shard, r0, nr):
            return out[t].at[shard, r0:r0 + nr, :]

        def copy(k, dst, to, src=None):
            return pltpu.make_async_remote_copy(
                src_ref=dst if src is None else src, dst_ref=dst, send_sem=send_sems.at[k], recv_sem=recv_sems.at[k],
                device_id=to, device_id_type=MESH)

        own = [pltpu.make_async_copy(w[t], out[t].at[mine], local_sems.at[t]) for t in range(nt)]
        for cp in own:
            cp.start()
        for fetcher, lo, hi in ((0, 0, split), (1, split, NP)):
            @pl.when(c == fetcher)
            def _(lo=lo, hi=hi):
                sent = []
                for j, chip in enumerate(chips):
                    for pi in range(lo, hi):
                        t, r0, nr = pieces[pi]
                        sent.append(copy(j * NP + pi, blk(t, mine, r0, nr), (*chip, c), src=w[t].at[r0:r0 + nr, :]))
                        sent[-1].start()
                for j, chip in enumerate(chips):
                    theirs = 2 * chip[0] + chip[1]
                    for pi in range(lo, hi):
                        t, r0, nr = pieces[pi]
                        copy(j * NP + pi, blk(t, theirs, r0, nr), me).wait_recv()
                        sent.append(copy((3 + j) * NP + pi, blk(t, theirs, r0, nr), sibling))
                        sent[-1].start()
                for cp in sent:
                    cp.wait_send()

            @pl.when(c != fetcher)
            def _(lo=lo, hi=hi):
                for j, chip in enumerate(chips):
                    theirs = 2 * chip[0] + chip[1]
                    for pi in range(lo, hi):
                        t, r0, nr = pieces[pi]
                        copy((3 + j) * NP + pi, blk(t, theirs, r0, nr), me).wait_recv()

        for cp in own:
            cp.wait()

    return pl.pallas_call(
        body, name="ag_weights_first", out_shape=[_sds((4,) + a.shape, a.dtype) for a in ts],
        in_specs=[pl.BlockSpec(memory_space=pl.ANY)] * nt, out_specs=[pl.BlockSpec(memory_space=pltpu.VMEM)] * nt,
        scratch_shapes=[pltpu.SemaphoreType.DMA((6 * NP,)), pltpu.SemaphoreType.DMA((6 * NP,)),
                        pltpu.SemaphoreType.DMA((nt,))],
        compiler_params=pltpu.CompilerParams(vmem_limit_bytes=_VMEM_LIMIT),
    )(*ts)


class _WeightGather:
    def __init__(self, ts):
        self.nt = len(ts)
        self.pieces = _row_chunks(ts, 0)
        NP = len(self.pieces)
        self.out_shape = [_sds((4,) + a.shape, a.dtype) for a in ts]
        self.scratch = [pltpu.SemaphoreType.DMA((3 * NP,)), pltpu.SemaphoreType.DMA((3 * NP,)),
                        pltpu.SemaphoreType.DMA((self.nt,))]

    def _copies(self, src, out, sems):
        send_sems, recv_sems, local_sems = sems
        NP = len(self.pieces)
        x, y, c = lax.axis_index("x"), lax.axis_index("y"), lax.axis_index("c")
        mine = 2 * x + y
        own = [pltpu.make_async_copy(src[t], out[t].at[mine], local_sems.at[t]) for t in range(self.nt)]
        sends, recvs = [], []
        for j, chip in enumerate([(1 - x, y), (x, 1 - y), (1 - x, 1 - y)]):
            theirs = 2 * chip[0] + chip[1]
            for pi, (t, r0, nr) in enumerate(self.pieces):
                idx = j * NP + pi
                sends.append(pltpu.make_async_remote_copy(
                    src_ref=src[t].at[r0:r0 + nr, :], dst_ref=out[t].at[mine, r0:r0 + nr, :],
                    send_sem=send_sems.at[idx], recv_sem=recv_sems.at[idx], device_id=(*chip, c), device_id_type=MESH))
                recvs.append(pltpu.make_async_remote_copy(
                    src_ref=out[t].at[theirs, r0:r0 + nr, :], dst_ref=out[t].at[theirs, r0:r0 + nr, :],
                    send_sem=send_sems.at[idx], recv_sem=recv_sems.at[idx], device_id=(x, y, c), device_id_type=MESH))
        return own, sends, recvs

    def start(self, src, out, sems):
        own, sends, _ = self._copies(src, out, sems)
        for cp in own + sends:
            cp.start()

    def wait(self, src, out, sems):
        own, sends, recvs = self._copies(src, out, sems)
        for cp in recvs:
            cp.wait_recv()
        for cp in sends:
            cp.wait_send()
        for cp in own:
            cp.wait()


class _GradExchange:
    def __init__(self, ts, layer, chunks=None):
        self.nt, self.layer = len(ts), layer
        self.pieces = [p for i, p in enumerate(_row_chunks(ts, 1)) if chunks is None or i in chunks]
        NP = len(self.pieces)
        self.out_shape = [_sds((8,) + a.shape[1:], a.dtype) for a in ts]
        self.scratch = [pltpu.SemaphoreType.DMA((7 * NP,)), pltpu.SemaphoreType.DMA((7 * NP,)),
                        pltpu.SemaphoreType.DMA((NP,))]

    def _copies(self, src, out, sems):
        send_sems, recv_sems, local_sems = sems
        NP = len(self.pieces)
        x, y, c = lax.axis_index("x"), lax.axis_index("y"), lax.axis_index("c")
        me = 4 * x + 2 * y + c
        owner = c == self.layer
        own = [pltpu.make_async_copy(src[t].at[2 * x + y, r0:r0 + nr, :], out[t].at[me, r0:r0 + nr, :],
                                     local_sems.at[pi]) for pi, (t, r0, nr) in enumerate(self.pieces)]
        rel = []
        for k in range(1, 8):
            px = 1 - x if k & 4 else x
            py = 1 - y if k & 2 else y
            source = 4 * px + 2 * py + (1 - c if k & 1 else c)
            sends, recvs = [], []
            for pi, (t, r0, nr) in enumerate(self.pieces):
                idx = (k - 1) * NP + pi
                sends.append(pltpu.make_async_remote_copy(
                    src_ref=src[t].at[2 * px + py, r0:r0 + nr, :], dst_ref=out[t].at[me, r0:r0 + nr, :],
                    send_sem=send_sems.at[idx], recv_sem=recv_sems.at[idx], device_id=(px, py, self.layer),
                    device_id_type=MESH))
                recvs.append(pltpu.make_async_remote_copy(
                    src_ref=out[t].at[source, r0:r0 + nr, :], dst_ref=out[t].at[source, r0:r0 + nr, :],
                    send_sem=send_sems.at[idx], recv_sem=recv_sems.at[idx], device_id=(x, y, c),
                    device_id_type=MESH))
            rel.append((jnp.logical_not(owner) if k & 1 else owner, sends, recvs))
        return owner, own, rel

    def start(self, src, out, sems):
        owner, own, rel = self._copies(src, out, sems)

        @pl.when(owner)
        def _():
            for cp in own:
                cp.start()

        for sending, sends, _ in rel:
            @pl.when(sending)
            def _(sends=sends):
                for cp in sends:
                    cp.start()

    def wait(self, src, out, sems):
        owner, own, rel = self._copies(src, out, sems)

        @pl.when(owner)
        def _():
            for _, _, recvs in rel:
                for cp in recvs:
                    cp.wait_recv()
            for cp in own:
                cp.wait()

        for sending, sends, _ in rel:
            @pl.when(sending)
            def _(sends=sends):
                for cp in sends:
                    cp.wait_send()


def _sibling_join(ts):
    nt = len(ts)
    pieces = _row_chunks(ts, 0)
    NP = len(pieces)

    def body(*refs):
        src, out = refs[:nt], refs[nt:2 * nt]
        send_sems, recv_sems, local_sems = refs[2 * nt:]
        x, y, c = lax.axis_index("x"), lax.axis_index("y"), lax.axis_index("c")
        own = [pltpu.make_async_copy(src[t], out[t].at[c], local_sems.at[t]) for t in range(nt)]
        for cp in own:
            cp.start()
        sent = []
        for pi, (t, r0, nr) in enumerate(pieces):
            sent.append(pltpu.make_async_remote_copy(
                src_ref=src[t].at[r0:r0 + nr, :], dst_ref=out[t].at[c, r0:r0 + nr, :], send_sem=send_sems.at[pi],
                recv_sem=recv_sems.at[pi], device_id=(x, y, 1 - c), device_id_type=MESH))
            sent[-1].start()
        for pi, (t, r0, nr) in enumerate(pieces):
            pltpu.make_async_remote_copy(
                src_ref=src[t].at[r0:r0 + nr, :], dst_ref=out[t].at[1 - c, r0:r0 + nr, :], send_sem=send_sems.at[pi],
                recv_sem=recv_sems.at[pi], device_id=(x, y, c), device_id_type=MESH).wait_recv()
        for cp in sent:
            cp.wait_send()
        for cp in own:
            cp.wait()

    vmem = pl.BlockSpec(memory_space=pltpu.VMEM)
    return pl.pallas_call(
        body, name="sibling_join", out_shape=[_sds((2,) + a.shape, a.dtype) for a in ts],
        in_specs=[vmem] * nt, out_specs=[vmem] * nt,
        scratch_shapes=[pltpu.SemaphoreType.DMA((NP,)), pltpu.SemaphoreType.DMA((NP,)),
                        pltpu.SemaphoreType.DMA((nt,))],
        compiler_params=pltpu.CompilerParams(vmem_limit_bytes=_VMEM_LIMIT),
    )(*ts)


def _late_weights(gathered):
    g_bs, g_bd, g_out = gathered
    cat = lambda g, axis: jnp.concatenate([g[s] for s in range(4)], axis=axis)
    return dict(wb_sb=cat(g_bs, 1), wb_dn=cat(g_bd, 1), w_out=cat(g_out, 0))


def _layer_fwd(x, shift, scale, gate, lw, next_shards=None, late_shards=None):
    D = x.shape[1]
    h = _norm_mod(x, lw["norm_g"], scale, shift)
    if late_shards is None:
        p = _matmul("in_proj", h, lw["w_cat"], "nn", F32, tm_cap=2048, tn_cap=896)
    else:
        p, late = _matmul("in_proj", h, lw["w_cat"], "nn", F32, tm_cap=2048, tn_cap=896,
                          exchange=_WeightGather(late_shards), ex_src=late_shards)
        lw = {**lw, **_late_weights(late)}
    qn, kn = _sb_prep(p, lw["gq_t"], lw["gk_t"])
    qkv, bb, gcb, glb = _dn_prep(p, lw["conv_w"], lw["a_row"], lw["dtb_row"])
    if next_shards is None:
        (o_att, tot, nblocks), _ = _sb_fwd(qn, kn, p)
        (o_dn, tinv, sall), gathered = _dn_fwd(qkv, bb, gcb, glb)
    else:
        first, rest = next_shards[:1], next_shards[1:]
        (o_att, tot, nblocks), g_rest = _sb_fwd(qn, kn, p, _WeightGather(rest), rest)
        (o_dn, tinv, sall), g_first = _dn_fwd(qkv, bb, gcb, glb, _WeightGather(first), first)
        gathered = list(g_first) + g_rest
    o_sb, o_dnn = _gate(o_att, o_dn, p, lw["gn"])
    y, b_sb, b_dn = _branch(o_sb, o_dnn, lw["wb_sb"], lw["wb_dn"], p, D)
    x_next, out = _out_proj(x, y, lw["w_out"], gate)
    res = dict(x=x, h=h, p=p, qn=qn, kn=kn, o_att=o_att, tot=tot, nblocks=nblocks, qkv=qkv, bb=bb, gcb=gcb, glb=glb, o_dn=o_dn,
               tinv=tinv, sall=sall, o_sb=o_sb, o_dnn=o_dnn, y=y, b_sb=b_sb, b_dn=b_dn, out=out,
               shift=shift, scale=scale, gate=gate)
    return x_next, res, lw, gathered


def _layer_bwd(dxn, res, lw, pending=None):
    p = res["p"]
    dout, db_sb, db_dn, dm, dgate = _out_bwd(dxn, res["out"], res["gate"], lw["w_out"], p, res["b_sb"], res["b_dn"])
    dw_out = _matmul("dw_out", res["y"], dout, "tn", _MXU_DTYPE)
    dwb_sb = _matmul("dwb_sb", res["o_sb"], db_sb, "tn", _MXU_DTYPE)
    dwb_dn = _matmul("dwb_dn", res["o_dnn"], db_dn, "tn", _MXU_DTYPE)
    do_att, dz_sb, do_dn, dz_dn, dgn = _gate_bwd(db_sb, db_dn, lw["wb_sb"], lw["wb_dn"], res["o_att"], res["o_dn"],
                                                  p, lw["gn"])
    D = dxn.shape[1]
    by_shard = lambda g: g.reshape(g.shape[0], 4, g.shape[1] // 4).transpose(1, 0, 2)
    send = [by_shard(dwb_sb), by_shard(dwb_dn), dw_out.reshape(4, D // 4, D)]
    dn_args = (res["qkv"], res["bb"], res["gcb"], res["glb"], res["tinv"], res["sall"], do_dn)
    if pending is None:
        (dqn, dkn, dv), _ = _sb_bwd(res["qn"], res["kn"], p, do_att, res["tot"], res["nblocks"])
        (dqkv, dbb, dgb), received = _dn_bwd(*dn_args)
    else:
        above, above_send = pending
        (dqn, dkn, dv), got_in = _sb_bwd(res["qn"], res["kn"], p, do_att, res["tot"], res["nblocks"],
                                         _GradExchange(above_send[:1], above), above_send[:1])
        (dqkv, dbb, dgb), got_rest = _dn_bwd(*dn_args, _GradExchange(above_send[1:], above), above_send[1:],
                                             _GradExchange(send, above - 1), send)
        received, send = list(got_in) + got_rest, []
    dq_sb, dk_sb, dgq, dgk = _sb_prep_bwd(p, dqn, dkn, lw["gq_t"], lw["gk_t"])
    dc, dp_ba, dal, ddt = _dn_prep_bwd_a(p, dqkv, dbb, dgb, lw["conv_w"], lw["a_row"], lw["dtb_row"])
    dp_dn, dconv = _dn_prep_bwd_b(p, dc, lw["conv_w"])
    dp = jnp.concatenate([dp_dn, dz_dn, dq_sb, dk_sb, dv.astype(_MXU_DTYPE), dz_sb, dm, dp_ba], axis=1)
    dw_cat = _matmul("dw_cat", res["h"], dp, "tn", _MXU_DTYPE, tm_cap=1024, tn_cap=896, tk_cap=2048)
    send = [_shards_from_cat(dw_cat, D)] + send
    if pending is None:
        dh = _matmul("dh", dp, lw["w_cat"], "nt", F32, tm_cap=1024, tk_cap=896)
        (dx, dshift, dscale, dnorm_g), _ = _norm_mod_bwd(res["x"], dh, dxn, lw["norm_g"], res["scale"])
    else:
        dh, arrived = _matmul("dh", dp, lw["w_cat"], "nt", F32, tm_cap=1024, tk_cap=896,
                              exchange=_GradExchange(send, pending[0] - 1, (0, 1, 2)), ex_src=send,
                              ex_prev=received[:1])
        tail = _GradExchange(send, pending[0] - 1, (3,))
        if tail.pieces:
            (dx, dshift, dscale, dnorm_g), arrived = _norm_mod_bwd(
                res["x"], dh, dxn, lw["norm_g"], res["scale"], exchange=tail, ex_src=send, ex_prev=arrived)
        else:
            (dx, dshift, dscale, dnorm_g), _ = _norm_mod_bwd(res["x"], dh, dxn, lw["norm_g"], res["scale"])
        received, send = arrived + list(received[1:]), []
    small = dict(dmod=jnp.concatenate([dshift, dscale, dgate], axis=1)[0], norm_g=dnorm_g[0],
                 sb_q_g=dgq.reshape(SB_HEADS, SB_HD).sum(0), sb_k_g=dgk.reshape(SB_HEADS, SB_HD).sum(0),
                 conv_w=dconv, dn_a_log=dal[0, DN_HEADS:2 * DN_HEADS], dn_dt_bias=ddt[0, DN_HEADS:2 * DN_HEADS],
                 dn_norm_g=dgn[0])
    return dx, small, send, received


def _cat_cols(w, D):
    return jnp.concatenate([w[:, 2048:4096], w[:, 0:2048], w[:, 4104:4104 + 2 * D], w[:, 4096:4104],
                            jnp.zeros((w.shape[0], LANES - 8), w.dtype)], axis=1)


def _shards_from_cat(g, D):
    n = (4104 + 2 * D) // 4
    segments = ((0, 2048, 2048), (2048, 4096, 0), (4096, 4104, 4096 + 2 * D), (4104, 4104 + 2 * D, 4096))

    def shard(lo, hi):
        cuts = [(c0 + max(lo, s0) - s0, c0 + min(hi, s1) - s0) for s0, s1, c0 in segments if max(lo, s0) < min(hi, s1)]
        return jnp.concatenate([g[:, a:b] for a, b in cuts], axis=1)

    return jnp.stack([shard(s * n, (s + 1) * n) for s in range(4)])


def _flat_pack(arrs, mult):
    flat = jnp.concatenate([a.reshape(-1) for a in arrs])
    n = flat.shape[0]
    pad = (-n) % mult
    if pad:
        flat = jnp.concatenate([flat, jnp.zeros((pad,), flat.dtype)])
    return flat.reshape(-1, LANES)


def _flat_unpack(flat, shapes):
    flat = flat.reshape(-1)
    out, off = [], 0
    for s in shapes:
        n = math.prod(s)
        out.append(flat[off:off + n].reshape(s))
        off += n
    return out


BIG = ("w_in", "w_branch_sb", "w_branch_dn", "w_out")
SMALL = ("ada_b", "norm_g", "sb_q_g", "sb_k_g", "conv_w", "dn_a_log", "dn_dt_bias", "dn_norm_g")


def kernel(x, c, ada_w, ada_b, norm_g, w_in, sb_q_g, sb_k_g, conv_w, dn_a_log, dn_dt_bias, dn_norm_g, w_branch_sb, w_branch_dn, w_out, loss_target, m_ada_w, m_ada_b, m_norm_g, m_w_in, m_sb_q_g, m_sb_k_g, m_conv_w, m_dn_a_log, m_dn_dt_bias, m_dn_norm_g, m_w_branch_sb, m_w_branch_dn, m_w_out, v_ada_w, v_ada_b, v_norm_g, v_w_in, v_sb_q_g, v_sb_k_g, v_conv_w, v_dn_a_log, v_dn_dt_bias, v_dn_norm_g, v_w_branch_sb, v_w_branch_dn, v_w_out):
    W = dict(ada_w=ada_w, ada_b=ada_b, norm_g=norm_g, w_in=w_in, sb_q_g=sb_q_g, sb_k_g=sb_k_g, conv_w=conv_w,
             dn_a_log=dn_a_log, dn_dt_bias=dn_dt_bias, dn_norm_g=dn_norm_g, w_branch_sb=w_branch_sb,
             w_branch_dn=w_branch_dn, w_out=w_out)
    M = dict(ada_w=m_ada_w, ada_b=m_ada_b, norm_g=m_norm_g, w_in=m_w_in, sb_q_g=m_sb_q_g, sb_k_g=m_sb_k_g,
             conv_w=m_conv_w, dn_a_log=m_dn_a_log, dn_dt_bias=m_dn_dt_bias, dn_norm_g=m_dn_norm_g,
             w_branch_sb=m_w_branch_sb, w_branch_dn=m_w_branch_dn, w_out=m_w_out)
    V = dict(ada_w=v_ada_w, ada_b=v_ada_b, norm_g=v_norm_g, w_in=v_w_in, sb_q_g=v_sb_q_g, sb_k_g=v_sb_k_g,
             conv_w=v_conv_w, dn_a_log=v_dn_a_log, dn_dt_bias=v_dn_dt_bias, dn_norm_g=v_dn_norm_g,
             w_branch_sb=v_w_branch_sb, w_branch_dn=v_w_branch_dn, w_out=v_w_out)
    L = ada_w.shape[0]
    S, D = x.shape[1], x.shape[2]
    ix, iy, ic = lax.axis_index("x"), lax.axis_index("y"), lax.axis_index("c")
    shard = 2 * ix + iy
    me = 2 * shard + ic
    n_ada = ada_w.shape[2]
    n_in = w_in.shape[2]
    n_conv = conv_w.shape[2]
    n_br = w_branch_sb.shape[2]
    n_out = w_out.shape[1]

    assert L == 2, "the owner of a layer's gradients is the core with the layer's number"
    shards = [W[n].astype(_MXU_DTYPE) for n in BIG]
    gathered0 = _ag_weights_first([shards[0][0]])

    g1 = _ag_small("ag_c_conv", _flat_pack([c, conv_w], LANES * 8))
    g1 = g1.reshape(8, -1)
    c_all = g1[:, :D]
    conv_parts = g1[:, D:D + L * CONV_K * n_conv].reshape(4, 2, L, CONV_K, n_conv)[:, 0]
    conv_full = jnp.concatenate([conv_parts[s] for s in range(4)], axis=2)
    ada_b_sh = lax.dynamic_slice_in_dim(ada_b, shard * n_ada, n_ada, axis=1)[:, None, :]
    mod_sh = _ada_fwd(c_all, ada_w, ada_b_sh)
    g2 = _ag_small("ag_mod", _flat_pack([mod_sh], LANES * 8)).reshape(8, -1)
    mod_parts = g2[:, :L * 8 * n_ada].reshape(4, 2, L, 8, n_ada)[:, 0]
    mod_all = jnp.concatenate([mod_parts[s] for s in range(4)], axis=2)
    mod = lax.dynamic_index_in_dim(mod_all, me, axis=1, keepdims=False)

    def layer_weights(l, g_in, late=None):
        pad_lo = jnp.zeros((DN_HEADS,), F32)
        pad_hi = jnp.zeros((LANES - 2 * DN_HEADS,), F32)
        lw = dict(
            norm_g=norm_g[l][None, :], w_cat=_cat_cols(jnp.concatenate([g_in[s] for s in range(4)], axis=1), D),
            gq_t=jnp.tile(sb_q_g[l], SB_HEADS)[None, :], gk_t=jnp.tile(sb_k_g[l], SB_HEADS)[None, :],
            conv_w=conv_full[l],
            a_row=jnp.concatenate([pad_lo, dn_a_log[l], pad_hi])[None, :],
            dtb_row=jnp.concatenate([pad_lo, dn_dt_bias[l], pad_hi])[None, :], gn=dn_norm_g[l][None, :])
        return lw if late is None else {**lw, **_late_weights(late)}

    mods = lambda l: (mod[l, None, 0:D], mod[l, None, D:2 * D], mod[l, None, 2 * D:3 * D])
    lws, ress = [None] * L, [None] * L
    xs, ress[0], lws[0], gathered1 = _layer_fwd(x[0], *mods(0), layer_weights(0, gathered0[0]),
                                                next_shards=[a[1] for a in shards],
                                                late_shards=[a[0] for a in shards[1:]])
    xs, ress[1], lws[1], _ = _layer_fwd(xs, *mods(1), layer_weights(1, gathered1[0], gathered1[1:]))
    dxs, loss_row = _loss_head(xs, loss_target[0])
    loss = lax.psum(loss_row[0, 0], ("x", "y", "c"))
    smalls = [None] * L
    dxs, smalls[1], send1, _ = _layer_bwd(dxs, ress[1], lws[1])
    dxs, smalls[0], send0, got = _layer_bwd(dxs, ress[0], lws[0], (1, send1))
    grad_x = dxs[None]

    small_names = ("dmod",) + SMALL[1:]
    small_pack = _flat_pack([jnp.stack([smalls[l][n] for l in range(L)]) for n in small_names], LANES * 8)
    g3 = _ag_small("ag_small_grads", small_pack)
    R3 = small_pack.shape[0]
    g3 = g3.reshape(8, R3, LANES)
    small_sum = _sum_parts("sum_small", g3)
    small_shapes = [(L, 3 * D), (L, D), (L, SB_HD), (L, SB_HD), (L, CONV_K, 3 * DN_W), (L, DN_HEADS), (L, DN_HEADS),
                    (L, DN_HD)]
    sg = dict(zip(small_names, _flat_unpack(small_sum, small_shapes)))
    G = dict(ada_b=sg["dmod"], norm_g=sg["norm_g"], sb_q_g=sg["sb_q_g"], sb_k_g=sg["sb_k_g"],
             conv_w=lax.dynamic_slice_in_dim(sg["conv_w"], shard * n_conv, n_conv, axis=2),
             dn_a_log=sg["dn_a_log"], dn_dt_bias=sg["dn_dt_bias"], dn_norm_g=sg["dn_norm_g"])
    dmod_all = g3.reshape(8, -1)[:, :L * 3 * D].reshape(8, L, 3 * D)
    dmod_sh = lax.dynamic_slice_in_dim(dmod_all, shard * n_ada, n_ada, axis=2).transpose(1, 0, 2)
    G["ada_w"] = _ada_bwd(c_all.T, dmod_sh)

    assert not send0
    mine = [_sum_parts("sum_" + n, g) for n, g in zip(BIG, got)]
    for n, g in zip(BIG, _sibling_join(mine)):
        G[n] = g

    delta, new_m, new_v = {}, {}, {}
    for n in ("ada_w",) + BIG:
        delta[n], new_m[n], new_v[n] = _adamw("adamw_" + n, W[n], G[n], M[n], V[n])
    sm_shapes = [W[n].shape for n in SMALL]
    d, mo, vo = _adamw("adamw_small", *[_flat_pack([T[n] for n in SMALL], LANES * 8)[None] for T in (W, G, M, V)])
    for n, dd, mm, vv in zip(SMALL, _flat_unpack(d, sm_shapes), _flat_unpack(mo, sm_shapes),
                             _flat_unpack(vo, sm_shapes)):
        delta[n], new_m[n], new_v[n] = dd, mm, vv

    order = ("ada_w", "ada_b", "norm_g", "w_in", "sb_q_g", "sb_k_g", "conv_w", "dn_a_log", "dn_dt_bias", "dn_norm_g",
             "w_branch_sb", "w_branch_dn", "w_out")
    return (loss, grad_x, *[G[n] for n in order], *[delta[n] for n in order], *[new_m[n] for n in order],
            *[new_v[n] for n in order])
```

```python
import math

import jax
import jax.numpy as jnp
from jax import lax
from jax.experimental import pallas as pl
from jax.experimental.pallas import tpu as pltpu

F32 = jnp.float32
BF16 = jnp.bfloat16
_MXU_DTYPE = BF16
_VMEM_LIMIT = 48 * 1024 * 1024
LANES = 128

EPS = 1e-6
SB_HEADS, SB_HD, SB_W = 8, 64, 512
DN_HEADS, DN_HD, DN_W = 4, 128, 512
CONV_K = 4
CHUNK = 64
_ROW_BLOCK = 512
QB = 256
_SB_DEAD = 104.0
ADAM_LR, ADAM_B1, ADAM_B2, ADAM_EPS, ADAM_WD, ADAM_STEP = 0.001, 0.9, 0.999, 1e-08, 0.01, 10

C_DN_QKV, C_DN_Z, C_SB_Q, C_SB_K, C_SB_V, C_SB_Z, C_MG = 0, 1536, 2048, 2560, 3072, 3584, 4096

_NN = (((1,), (0,)), ((), ()))
_NT = (((1,), (1,)), ((), ()))
_TN = (((0,), (0,)), ((), ()))
_BNN = (((2,), (1,)), ((0,), (0,)))
_BNT = (((2,), (2,)), ((0,), (0,)))
_BTN = (((1,), (1,)), ((0,), (0,)))
MESH = pl.DeviceIdType.MESH


def _sds(shape, dtype):
    return jax.ShapeDtypeStruct(shape, dtype)


def _cp(n):
    return pltpu.CompilerParams(dimension_semantics=("arbitrary",) * n, vmem_limit_bytes=_VMEM_LIMIT)


def _rb(tm, w, cb=0):
    return pl.BlockSpec((tm, w), lambda i: (i, cb))


def _fs(shape):
    nd = len(shape)
    return pl.BlockSpec(shape, lambda i: (0,) * nd)


def _dg(a, b, dims):
    return lax.dot_general(a, b, dims, preferred_element_type=F32)


def _mm(a, b, dims=_NN):
    return _dg(a.astype(_MXU_DTYPE), b.astype(_MXU_DTYPE), dims)


def _split3(x):
    hi = x.astype(BF16)
    r = x - hi.astype(F32)
    mid = r.astype(BF16)
    lo = (r - mid.astype(F32)).astype(BF16)
    return hi, mid, lo


def _mm_xl(x, const, dims=_NN):
    cb = const.astype(BF16)
    hi, mid, lo = _split3(x)
    return _dg(hi, cb, dims) + _dg(mid, cb, dims) + _dg(lo, cb, dims)


def _mm_xl2(x, const, dims=_NN):
    cb = const.astype(BF16)
    hi = x.astype(BF16)
    lo = (x - hi.astype(F32)).astype(BF16)
    return _dg(hi, cb, dims) + _dg(lo, cb, dims)


def _mm_xr(const, x, dims=_NN):
    cb = const.astype(BF16)
    hi, mid, lo = _split3(x)
    return _dg(cb, hi, dims) + _dg(cb, mid, dims) + _dg(cb, lo, dims)


def _mm3(a, b, dims=_NN):
    ah, am, _ = _split3(a)
    bh, bm, _ = _split3(b)
    return _dg(ah, bh, dims) + (_dg(ah, bm, dims) + _dg(am, bh, dims))


def _sigmoid(z):
    return 1.0 / (1.0 + jnp.exp(-z))


def _silu(z):
    return z * _sigmoid(z)


def _dsilu(z):
    s = _sigmoid(z)
    return s * (1.0 + z * (1.0 - s))


def _softplus(z):
    return jnp.maximum(z, 0.0) + jnp.log(1.0 + jnp.exp(-jnp.abs(z)))


def _iota2(shape, dim):
    return lax.broadcasted_iota(jnp.int32, shape, dim)


def _pick(n, cap, mult):
    best = None
    for t in range(mult, min(n, cap) + 1, mult):
        if n % t == 0:
            best = t
    assert best is not None, (n, cap, mult)
    return best


def _matmul(name, a, b, form, out_dtype, tm_cap=512, tn_cap=1024, tk_cap=1024, exchange=None, ex_src=(), ex_prev=()):
    if form == "nn":
        (M, K), (_, N) = a.shape, b.shape
    elif form == "nt":
        (M, K), (N, _) = a.shape, b.shape
    else:
        (K, M), (_, N) = a.shape, b.shape
    tm = _pick(M, tm_cap, 128 if form == "tn" else 8)
    tn = _pick(N, tn_cap, 128)
    tk = _pick(K, tk_cap, 128)
    nk = K // tk
    dims = {"nn": _NN, "nt": _NT, "tn": _TN}[form]
    if form == "nn":
        a_spec = pl.BlockSpec((tm, tk), lambda i, j, k: (i, k))
        b_spec = pl.BlockSpec((tk, tn), lambda i, j, k: (k, j))
    elif form == "nt":
        a_spec = pl.BlockSpec((tm, tk), lambda i, j, k: (i, k))
        b_spec = pl.BlockSpec((tn, tk), lambda i, j, k: (j, k))
    else:
        a_spec = pl.BlockSpec((tk, tm), lambda i, j, k: (k, i))
        b_spec = pl.BlockSpec((tk, tn), lambda i, j, k: (k, j))

    grid = (M // tm, N // tn, nk)
    nx, npv = len(ex_src), len(ex_prev)
    o0 = 2 + nx + npv

    def body(*refs):
        if exchange is None:
            compute(*refs)
            return
        src, xout, sems = refs[2:2 + nx], refs[o0 + 1:o0 + 1 + nx], refs[o0 + 2 + nx:]
        at = [pl.program_id(d) for d in range(3)]

        @pl.when(jnp.logical_and(jnp.logical_and(at[0] == 0, at[1] == 0), at[2] == 0))
        def _():
            exchange.start(src, xout, sems)

        compute(refs[0], refs[1], refs[o0], refs[o0 + 1 + nx])

        @pl.when(jnp.logical_and(jnp.logical_and(at[0] == grid[0] - 1, at[1] == grid[1] - 1), at[2] == nk - 1))
        def _():
            exchange.wait(src, xout, sems)

    def compute(a_ref, b_ref, o_ref, acc_ref):
        if nk == 1:
            o_ref[...] = _mm(a_ref[...], b_ref[...], dims).astype(o_ref.dtype)
            return
        k = pl.program_id(2)

        @pl.when(k == 0)
        def _():
            acc_ref[...] = _mm(a_ref[...], b_ref[...], dims)

        @pl.when(k > 0)
        def _():
            acc_ref[...] += _mm(a_ref[...], b_ref[...], dims)

        @pl.when(k == nk - 1)
        def _():
            o_ref[...] = acc_ref[...].astype(o_ref.dtype)

    hbm = pl.BlockSpec(memory_space=pl.ANY)
    outs = pl.pallas_call(
        body, name=name, grid=grid,
        in_specs=[a_spec, b_spec] + [hbm] * (nx + npv),
        out_specs=[pl.BlockSpec((tm, tn), lambda i, j, k: (i, j))] + [hbm] * nx,
        out_shape=[_sds((M, N), out_dtype)] + (exchange.out_shape if exchange else []),
        input_output_aliases={2 + nx + t: 1 + t for t in range(npv)},
        scratch_shapes=[pltpu.VMEM((tm, tn), F32)] + (exchange.scratch if exchange else []),
        compiler_params=_cp(3),
    )(a, b, *ex_src, *ex_prev)
    return outs[0] if exchange is None else (outs[0], list(outs[1:]))


def _norm_mod(x, g, scale, shift, tm=_ROW_BLOCK):
    S, D = x.shape

    def body(x_ref, g_ref, sc_ref, sh_ref, h_ref):
        xv = x_ref[...]
        r = lax.rsqrt(jnp.mean(xv * xv, axis=1, keepdims=True) + EPS)
        h_ref[...] = ((xv * r * g_ref[...]) * (1.0 + sc_ref[...]) + sh_ref[...]).astype(h_ref.dtype)

    return pl.pallas_call(
        body, name="norm_mod", grid=(S // tm,),
        in_specs=[_rb(tm, D), _fs((1, D)), _fs((1, D)), _fs((1, D))],
        out_specs=_rb(tm, D), out_shape=_sds((S, D), _MXU_DTYPE), compiler_params=_cp(1),
    )(x, g, scale, shift)


def _norm_mod_bwd(x, dh, dxn, g, scale, tm=_ROW_BLOCK, exchange=None, ex_src=(), ex_prev=()):
    S, D = x.shape
    nx = len(ex_src)

    def body(*refs):
        if exchange is None:
            compute(*refs)
            return
        src, xout, sems = refs[5:5 + nx], refs[9 + 2 * nx:9 + 3 * nx], refs[9 + 3 * nx:]

        @pl.when(pl.program_id(0) == 0)
        def _():
            exchange.start(src, xout, sems)

        compute(*refs[:5], *refs[5 + 2 * nx:9 + 2 * nx])

        @pl.when(pl.program_id(0) == S // tm - 1)
        def _():
            exchange.wait(src, xout, sems)

    def compute(x_ref, dh_ref, dxn_ref, g_ref, sc_ref, dx_ref, dsh_ref, dsc_ref, dg_ref):
        @pl.when(pl.program_id(0) == 0)
        def _():
            dsh_ref[...] = jnp.zeros_like(dsh_ref)
            dsc_ref[...] = jnp.zeros_like(dsc_ref)
            dg_ref[...] = jnp.zeros_like(dg_ref)

        xv, dhv, gv = x_ref[...], dh_ref[...], g_ref[...]
        r = lax.rsqrt(jnp.mean(xv * xv, axis=1, keepdims=True) + EPS)
        xh = xv * r
        one_sc = 1.0 + sc_ref[...]
        dsh_ref[...] += jnp.sum(dhv, axis=0, keepdims=True)
        dsc_ref[...] += jnp.sum(dhv * xh * gv, axis=0, keepdims=True)
        dg_ref[...] += jnp.sum(dhv * one_sc * xh, axis=0, keepdims=True)
        dxh = dhv * (gv * one_sc)
        dx_ref[...] = r * (dxh - xh * jnp.mean(dxh * xh, axis=1, keepdims=True)) + dxn_ref[...]

    hbm = pl.BlockSpec(memory_space=pl.ANY)
    outs = pl.pallas_call(
        body, name="norm_mod_bwd", grid=(S // tm,),
        in_specs=[_rb(tm, D), _rb(tm, D), _rb(tm, D), _fs((1, D)), _fs((1, D))] + [hbm] * (2 * nx),
        out_specs=[_rb(tm, D), _fs((1, D)), _fs((1, D)), _fs((1, D))] + [hbm] * nx,
        out_shape=[_sds((S, D), F32)] + [_sds((1, D), F32)] * 3 + (exchange.out_shape if exchange else []),
        input_output_aliases={5 + nx + t: 4 + t for t in range(nx)},
        scratch_shapes=exchange.scratch if exchange else [], compiler_params=_cp(1),
    )(x, dh, dxn, g, scale, *ex_src, *ex_prev)
    return outs[:4], list(outs[4:])


def _head_sum_matrix():
    r = jnp.arange(SB_W)
    return (r[:, None] // SB_HD == r[None, :] // SB_HD).astype(BF16)


def _sb_prep(p, gq_t, gk_t, tm=_ROW_BLOCK):
    S = p.shape[0]
    bd = _head_sum_matrix()

    def body(q_ref, k_ref, gq_ref, gk_ref, bd_ref, qn_ref, kn_ref):
        for src, g_ref, dst in ((q_ref, gq_ref, qn_ref), (k_ref, gk_ref, kn_ref)):
            v = src[...]
            ms = _mm_xl(v * v, bd_ref[...]) * (1.0 / SB_HD)
            dst[...] = (v * lax.rsqrt(ms + EPS) * g_ref[...]).astype(dst.dtype)

    return pl.pallas_call(
        body, name="sb_prep", grid=(S // tm,),
        in_specs=[_rb(tm, SB_W, C_SB_Q // SB_W), _rb(tm, SB_W, C_SB_K // SB_W),
                  _fs((1, SB_W)), _fs((1, SB_W)), _fs((SB_W, SB_W))],
        out_specs=[_rb(tm, SB_W), _rb(tm, SB_W)],
        out_shape=[_sds((S, SB_W), _MXU_DTYPE)] * 2, compiler_params=_cp(1),
    )(p, p, gq_t, gk_t, bd)


def _sb_prep_bwd(p, dqn, dkn, gq_t, gk_t, tm=_ROW_BLOCK):
    S = p.shape[0]
    bd = _head_sum_matrix()

    def body(q_ref, k_ref, dqn_ref, dkn_ref, gq_ref, gk_ref, bd_ref, dq_ref, dk_ref, dgq_ref, dgk_ref):
        @pl.when(pl.program_id(0) == 0)
        def _():
            dgq_ref[...] = jnp.zeros_like(dgq_ref)
            dgk_ref[...] = jnp.zeros_like(dgk_ref)

        for src, dn_ref, g_ref, dst, dg_ref in ((q_ref, dqn_ref, gq_ref, dq_ref, dgq_ref),
                                                (k_ref, dkn_ref, gk_ref, dk_ref, dgk_ref)):
            v, dn = src[...], dn_ref[...]
            r = lax.rsqrt(_mm_xl(v * v, bd_ref[...]) * (1.0 / SB_HD) + EPS)
            vh = v * r
            dg_ref[...] += jnp.sum(dn * vh, axis=0, keepdims=True)
            dvh = dn * g_ref[...]
            m = _mm_xl(dvh * vh, bd_ref[...]) * (1.0 / SB_HD)
            dst[...] = (r * (dvh - vh * m)).astype(dst.dtype)

    return pl.pallas_call(
        body, name="sb_prep_bwd", grid=(S // tm,),
        in_specs=[_rb(tm, SB_W, C_SB_Q // SB_W), _rb(tm, SB_W, C_SB_K // SB_W), _rb(tm, SB_W), _rb(tm, SB_W),
                  _fs((1, SB_W)), _fs((1, SB_W)), _fs((SB_W, SB_W))],
        out_specs=[_rb(tm, SB_W), _rb(tm, SB_W), _fs((1, SB_W)), _fs((1, SB_W))],
        out_shape=[_sds((S, SB_W), _MXU_DTYPE)] * 2 + [_sds((1, SB_W), F32)] * 2, compiler_params=_cp(1),
    )(p, p, dqn, dkn, gq_t, gk_t, bd)


def _sb_consts():
    r, c = _iota2((QB, QB), 0), _iota2((QB, QB), 1)
    lane = _iota2((1, LANES), 1)
    return r, c, lane


def _sb_fwd(qn, kn, p, gather=None, g_src=()):
    S = qn.shape[0]
    scale = 1.0 / math.sqrt(SB_HD)
    grid = (SB_W // LANES, S // QB)
    nx = len(g_src)

    def body(*refs):
        if gather is None:
            compute(*refs)
            return
        src, gout, sems = refs[3:3 + nx], refs[6 + nx:6 + 2 * nx], refs[6 + 2 * nx:]
        hp, i = pl.program_id(0), pl.program_id(1)

        @pl.when(jnp.logical_and(hp == 0, i == 0))
        def _():
            gather.start(src, gout, sems)

        compute(*refs[:3], *refs[3 + nx:6 + nx])

        @pl.when(jnp.logical_and(hp == grid[0] - 1, i == grid[1] - 1))
        def _():
            gather.wait(src, gout, sems)

    def compute(q_ref, k_ref, v_ref, o_ref, tot_ref, nb_ref):
        i = pl.program_id(1)
        r, c, lane = _sb_consts()
        u_gt = (r > c).astype(BF16)
        strict = jnp.concatenate([c < r, c < r], axis=0)
        q = q_ref[...]
        mask0 = (lane // SB_HD) == 0
        zero = jnp.zeros_like(q)
        qh = jnp.concatenate([jnp.where(mask0, q, zero), jnp.where(mask0, zero, q)], axis=0)

        def block(off, carry, diagonal):
            o, run = carry
            kj = k_ref[pl.ds(off, QB), :]
            vj = v_ref[pl.ds(off, QB), :].astype(_MXU_DTYPE)
            z = _mm(qh, kj, _NT) * scale
            sp = _softplus(z)
            sp_m = jnp.where(strict, sp, 0.0) if diagonal else sp
            later = _mm_xl2(sp_m, u_gt)
            w = jnp.exp((z - sp) - later - run)
            if diagonal:
                w = jnp.where(strict, w, 0.0)
            return o + _mm(w, vj), run + jnp.sum(sp_m, axis=1, keepdims=True)

        init = (jnp.zeros((2 * QB, LANES), F32), jnp.zeros((2 * QB, 1), F32))
        carry = block(pl.multiple_of(i * QB, QB), init, True)
        st = lax.while_loop(
            lambda st: jnp.logical_and(st[0] <= i, jnp.min(st[2]) < _SB_DEAD),
            lambda st: (st[0] + 1,) + block(pl.multiple_of((i - st[0]) * QB, QB), st[1:], False),
            (jnp.int32(1),) + carry)
        o_ref[...] = jnp.where(mask0, st[1][:QB], st[1][QB:])
        tot_ref[...] = jnp.where(mask0, st[2][:QB], st[2][QB:])
        nb_ref[...] = jnp.zeros((8, LANES), F32) + st[0].astype(F32)

    blk = pl.BlockSpec((QB, LANES), lambda hp, i: (i, hp))
    hbm = pl.BlockSpec(memory_space=pl.ANY)
    outs = pl.pallas_call(
        body, name="sb_fwd", grid=grid,
        in_specs=[blk, pl.BlockSpec((S, LANES), lambda hp, i: (0, hp)),
                  pl.BlockSpec((S, LANES), lambda hp, i: (0, C_SB_V // LANES + hp))] + [hbm] * nx,
        out_specs=[blk, blk, pl.BlockSpec((8, LANES), lambda hp, i: (i, hp))] + [hbm] * nx,
        out_shape=[_sds((S, SB_W), F32), _sds((S, SB_W), F32), _sds((8 * S // QB, SB_W), F32)]
        + (gather.out_shape if gather else []),
        scratch_shapes=gather.scratch if gather else [], compiler_params=_cp(2),
    )(qn, kn, p, *g_src)
    return outs[:3], list(outs[3:])


def _sb_bwd(qn, kn, p, do, tot, nblocks, exchange=None, ex_src=(), early=None, early_src=()):
    S = qn.shape[0]
    scale = 1.0 / math.sqrt(SB_HD)
    grid = (SB_W // LANES, S // QB)
    nx, ne = len(ex_src), len(early_src)
    NI = 6

    def body(*refs):
        if exchange is None:
            compute(*refs)
            return
        src, src2 = refs[NI:NI + nx], refs[NI + nx:NI + nx + ne]
        o0 = NI + nx + ne
        xout, sems, sems2 = refs[o0 + 3:o0 + 3 + nx], refs[o0 + 3 + nx:o0 + 6 + nx], refs[o0 + 6 + nx:]
        hp, i = pl.program_id(0), pl.program_id(1)

        @pl.when(jnp.logical_and(hp == 0, i == 0))
        def _():
            exchange.start(src, xout, sems)
            if early is not None:
                early.start(src2, xout[nx - ne:], sems2)

        compute(*refs[:NI], *refs[o0:o0 + 3])

        @pl.when(jnp.logical_and(hp == grid[0] - 1, i == grid[1] - 1))
        def _():
            exchange.wait(src, xout, sems)
            if early is not None:
                early.wait(src2, xout[nx - ne:], sems2)

    def compute(q_ref, k_ref, v_ref, do_ref, tot_ref, nb_ref, dq_ref, dk_ref, dv_ref):
        i = pl.program_id(1)

        @pl.when(i == 0)
        def _():
            dk_ref[...] = jnp.zeros_like(dk_ref)
            dv_ref[...] = jnp.zeros_like(dv_ref)

        r, c, lane = _sb_consts()
        u_le = (r <= c).astype(BF16)
        u_lt = (r < c).astype(BF16)
        strict = jnp.concatenate([c < r, c < r], axis=0)
        q = q_ref[...]
        do = do_ref[...].astype(_MXU_DTYPE)
        mask0 = (lane // SB_HD) == 0
        zero, zero_do = jnp.zeros_like(q), jnp.zeros_like(do)
        qh = jnp.concatenate([jnp.where(mask0, q, zero), jnp.where(mask0, zero, q)], axis=0)
        doh = jnp.concatenate([jnp.where(mask0, do, zero_do), jnp.where(mask0, zero_do, do)], axis=0)

        tot_pair = tot_ref[...]
        tot = jnp.concatenate([jnp.max(jnp.where(mask0, tot_pair, 0.0), axis=1, keepdims=True),
                               jnp.max(jnp.where(mask0, 0.0, tot_pair), axis=1, keepdims=True)], axis=0)
        nb = jnp.clip(jnp.max(nb_ref[...]).astype(jnp.int32), 1, i + 1)
        first = i + 1 - nb

        def block(off, carry, diagonal):
            dq, pre_sp, pre_e = carry
            kj = k_ref[pl.ds(off, QB), :]
            vj = v_ref[pl.ds(off, QB), :].astype(_MXU_DTYPE)
            z = _mm(qh, kj, _NT) * scale
            sp = _softplus(z)
            a = z - sp
            sp_m = jnp.where(strict, sp, 0.0) if diagonal else sp
            incl = _mm_xl2(sp_m, u_le)
            w = jnp.exp(a - ((tot - pre_sp) - incl))
            if diagonal:
                w = jnp.where(strict, w, 0.0)
            e = w * _mm(doh, vj, _NT)
            db = pre_e + _mm_xl2(e, u_lt)
            dz = (e - jnp.exp(a) * (e + db)) * scale
            if diagonal:
                dz = jnp.where(strict, dz, 0.0)
            dk_ref[pl.ds(off, QB), :] += _mm(dz, qh, _TN)
            dv_ref[pl.ds(off, QB), :] += _mm(w, doh, _TN)
            return (dq + _mm(dz, kj), pre_sp + jnp.sum(sp_m, axis=1, keepdims=True),
                    pre_e + jnp.sum(e, axis=1, keepdims=True))

        zero_col = jnp.zeros((2 * QB, 1), F32)
        init = (jnp.zeros((2 * QB, LANES), F32), zero_col, zero_col)
        carry = lax.fori_loop(first, i, lambda j, cr: block(pl.multiple_of(j * QB, QB), cr, False), init)
        carry = block(pl.multiple_of(i * QB, QB), carry, True)
        dq_ref[...] = jnp.where(mask0, carry[0][:QB], carry[0][QB:])

    blk = pl.BlockSpec((QB, LANES), lambda hp, i: (i, hp))
    full = pl.BlockSpec((S, LANES), lambda hp, i: (0, hp))
    hbm = pl.BlockSpec(memory_space=pl.ANY)
    outs = pl.pallas_call(
        body, name="sb_bwd", grid=grid,
        in_specs=[blk, full, pl.BlockSpec((S, LANES), lambda hp, i: (0, C_SB_V // LANES + hp)), blk, blk,
                  pl.BlockSpec((8, LANES), lambda hp, i: (i, hp))] + [hbm] * (nx + ne),
        out_specs=[blk, full, full] + [hbm] * nx,
        out_shape=[_sds((S, SB_W), F32)] * 3 + (exchange.out_shape if exchange else []),
        scratch_shapes=(exchange.scratch if exchange else []) + (early.scratch if early else []),
        compiler_params=_cp(2),
    )(qn, kn, p, do, tot, nblocks, *ex_src, *early_src)
    return outs[:3], outs[3:]


def _dn_prep(p, conv_w, a_row, dtb_row, tm=_ROW_BLOCK):
    S = p.shape[0]
    W3 = 3 * DN_W
    nhalo = tm // 8

    def body(x_ref, halo_ref, w_ref, ba_ref, a_ref, dtb_ref, qkv_ref, bb_ref, gc_ref, gl_ref):
        i = pl.program_id(0)
        halo = jnp.where(i > 0, halo_ref[...], 0.0)
        xf = jnp.concatenate([halo, x_ref[...]], axis=0)
        acc = jnp.zeros((tm, W3), F32)
        for k in range(CONV_K):
            sh = CONV_K - 1 - k
            xs = xf if sh == 0 else pltpu.roll(xf, sh, 0)
            acc = acc + xs[8:, :] * w_ref[k:k + 1, :]
        s = _silu(acc)
        for gi in range(2 * DN_HEADS):
            sl = slice(gi * LANES, (gi + 1) * LANES)
            sg = s[:, sl]
            rinv = lax.rsqrt(jnp.sum(sg * sg, axis=1, keepdims=True) + EPS)
            qkv_ref[:, sl] = sg * rinv * (DN_HD ** -0.5 if gi < DN_HEADS else 1.0)
        qkv_ref[:, 2 * DN_W:] = s[:, 2 * DN_W:]

        ba = ba_ref[...]
        beta = _sigmoid(ba)
        g = -jnp.exp(a_ref[...]) * _softplus(ba + dtb_ref[...])
        lr, lc = _iota2((LANES, DN_W), 0), _iota2((LANES, DN_W), 1)
        sel_b = (lr == lc // LANES).astype(BF16)
        sel_g = (lr == lc // LANES + DN_HEADS).astype(BF16)
        bb_ref[...] = _mm_xl(beta, sel_b)
        graw = _mm_xl(g, sel_g)
        rr, cc = _iota2((tm, tm), 0), _iota2((tm, tm), 1)
        tri = jnp.logical_and(rr >= cc, rr // CHUNK == cc // CHUNK).astype(BF16)
        gc = _mm_xr(tri, graw)
        last = (cc == (rr // CHUNK) * CHUNK + (CHUNK - 1)).astype(BF16)
        gc_ref[...] = gc
        gl_ref[...] = _mm_xr(last, gc)

    return pl.pallas_call(
        body, name="dn_prep", grid=(S // tm,),
        in_specs=[_rb(tm, W3, 0), pl.BlockSpec((8, W3), lambda i: (jnp.maximum(i * nhalo - 1, 0), 0)),
                  _fs((CONV_K, W3)), _rb(tm, LANES, (p.shape[1] - LANES) // LANES),
                  _fs((1, LANES)), _fs((1, LANES))],
        out_specs=[_rb(tm, W3), _rb(tm, DN_W), _rb(tm, DN_W), _rb(tm, DN_W)],
        out_shape=[_sds((S, W3), F32)] + [_sds((S, DN_W), F32)] * 3, compiler_params=_cp(1),
    )(p, p, conv_w, p, a_row, dtb_row)


def _heads(ref, base=0):
    return jnp.stack([ref[:, base + h * LANES:base + (h + 1) * LANES] for h in range(DN_HEADS)])


def _per_head(const):
    return jnp.broadcast_to(const[None], (DN_HEADS,) + const.shape)


def _dn_chunk_terms(q, k, v, beta, gc, gl):
    r, c = _iota2((CHUNK, CHUNK), 0), _iota2((CHUNK, CHUNK), 1)
    tril, strict = r >= c, r > c
    gcol = _mm_xl(gc, _per_head(jnp.full((LANES, CHUNK), 1.0 / LANES, F32)), _BNN)
    grow = _mm_xr(_per_head(jnp.full((CHUNK, LANES), 1.0 / LANES, F32)), gc, _BNT)
    dec = jnp.where(tril, jnp.exp(jnp.where(tril, gcol - grow, 0.0)), 0.0)
    gam = jnp.exp(gc)
    dlt = jnp.exp(gl - gc)
    kb, vb = k * beta, v * beta
    pm = _mm(kb, k, _BNT)
    qk = _mm(q, k, _BNT)
    m = jnp.where(strict, pm * dec, 0.0)
    a = jnp.where(tril, qk * dec, 0.0)
    return dict(tril=tril, strict=strict, dec=dec, gam=gam, dlt=dlt, kb=kb, vb=vb, m=m, a=a)


def _dn_fwd(qkv, bb, gcb, glb, gather=None, g_src=()):
    S = qkv.shape[0]
    N = S // CHUNK
    nx = len(g_src)

    def body(*refs):
        if gather is None:
            compute(*refs)
            return
        src, gout, sems = refs[4:4 + nx], refs[7 + nx:7 + 2 * nx], refs[8 + 2 * nx:]

        @pl.when(pl.program_id(0) == 0)
        def _():
            gather.start(src, gout, sems)

        compute(*refs[:4], *refs[4 + nx:7 + nx], refs[7 + 2 * nx])

        @pl.when(pl.program_id(0) == N - 1)
        def _():
            gather.wait(src, gout, sems)

    def compute(qkv_ref, bb_ref, gc_ref, gl_ref, o_ref, t_ref, sall_ref, s_scr):
        @pl.when(pl.program_id(0) == 0)
        def _():
            s_scr[...] = jnp.zeros_like(s_scr)

        r, c = _iota2((CHUNK, CHUNK), 0), _iota2((CHUNK, CHUNK), 1)
        eye = (r == c).astype(F32)
        q, k, v = _heads(qkv_ref), _heads(qkv_ref, DN_W), _heads(qkv_ref, 2 * DN_W)
        beta, gc, gl = _heads(bb_ref), _heads(gc_ref), _heads(gl_ref)
        s_prev = s_scr[...]
        sall_ref[0] = s_prev.astype(sall_ref.dtype)
        s0 = s_prev.astype(sall_ref.dtype).astype(F32)
        t = _dn_chunk_terms(q, k, v, beta, gc, gl)
        pw = -t["m"]
        tinv = eye + pw
        for _ in range(5):
            pw = _mm3(pw, pw, _BNN)
            tinv = tinv + _mm3(tinv, pw, _BNN)
        t_ref[...] = tinv
        u = _mm3(tinv, t["vb"], _BNN)
        w = _mm3(tinv, t["kb"] * t["gam"], _BNN)
        vn = u - _mm(w, s0, _BNN)
        o = _mm(q * t["gam"], s0, _BNN) + _mm(t["a"], vn, _BNN)
        for h in range(DN_HEADS):
            o_ref[:, h * LANES:(h + 1) * LANES] = o[h]
        egl = jnp.exp(jnp.concatenate([gl, gl], axis=1))
        s_scr[...] = s_prev * egl + _mm(k * t["dlt"], vn, _BTN)

    hbm = pl.BlockSpec(memory_space=pl.ANY)
    outs = pl.pallas_call(
        body, name="dn_fwd", grid=(N,),
        in_specs=[_rb(CHUNK, 3 * DN_W), _rb(CHUNK, DN_W), _rb(CHUNK, DN_W), _rb(CHUNK, DN_W)] + [hbm] * nx,
        out_specs=[_rb(CHUNK, DN_W), pl.BlockSpec((DN_HEADS, CHUNK, CHUNK), lambda n: (0, n, 0)),
                   pl.BlockSpec((1, DN_HEADS, DN_HD, DN_HD), lambda n: (n, 0, 0, 0))] + [hbm] * nx,
        out_shape=[_sds((S, DN_W), F32), _sds((DN_HEADS, S, CHUNK), F32),
                   _sds((N, DN_HEADS, DN_HD, DN_HD), _MXU_DTYPE)] + (gather.out_shape if gather else []),
        scratch_shapes=[pltpu.VMEM((DN_HEADS, DN_HD, DN_HD), F32)] + (gather.scratch if gather else []),
        compiler_params=_cp(1),
    )(qkv, bb, gcb, glb, *g_src)
    return outs[:3], outs[3:]


def _dn_bwd(qkv, bb, gcb, glb, tinv_all, sall, do, exchange=None, ex_src=(), early=None, early_src=()):
    S = qkv.shape[0]
    N = S // CHUNK
    nx = len(ex_src)
    NI = 7

    def body(*refs):
        if exchange is None:
            compute(*refs)
            return
        src, src2 = refs[NI:NI + nx], refs[NI + nx:NI + 2 * nx]
        o0 = NI + 2 * nx
        xout, ds_scr = refs[o0 + 3:o0 + 3 + nx], refs[o0 + 3 + nx]
        sems, sems2 = refs[o0 + 4 + nx:o0 + 7 + nx], refs[o0 + 7 + nx:]

        @pl.when(pl.program_id(0) == 0)
        def _():
            exchange.start(src, xout, sems)
            early.start(src2, xout, sems2)

        compute(*refs[:NI], *refs[o0:o0 + 3], ds_scr)

        @pl.when(pl.program_id(0) == N - 1)
        def _():
            exchange.wait(src, xout, sems)
            early.wait(src2, xout, sems2)

    def compute(qkv_ref, bb_ref, gc_ref, gl_ref, t_ref, sall_ref, do_ref, dqkv_ref, dbb_ref, dg_ref, ds_scr):
        @pl.when(pl.program_id(0) == 0)
        def _():
            ds_scr[...] = jnp.zeros_like(ds_scr)

        r, c = _iota2((CHUNK, CHUNK), 0), _iota2((CHUNK, CHUNK), 1)
        eye = (r == c).astype(F32)
        u_ge = (c >= r).astype(F32)
        last_row = _iota2((CHUNK, LANES), 0) == CHUNK - 1
        eye_h, u_ge_h = _per_head(eye), _per_head(u_ge)
        q, k, v = _heads(qkv_ref), _heads(qkv_ref, DN_W), _heads(qkv_ref, 2 * DN_W)
        beta, gc, gl = _heads(bb_ref), _heads(gc_ref), _heads(gl_ref)
        tinv = t_ref[...]
        s0 = sall_ref[0].astype(F32)
        do = _heads(do_ref)
        ds1 = ds_scr[...]
        t = _dn_chunk_terms(q, k, v, beta, gc, gl)
        gam, dlt, kb, vb, dec = t["gam"], t["dlt"], t["kb"], t["vb"], t["dec"]
        kbg = kb * gam
        u = _mm3(tinv, vb, _BNN)
        w = _mm3(tinv, kbg, _BNN)
        vn = u - _mm(w, s0, _BNN)
        qg, kd = q * gam, k * dlt
        egl = jnp.exp(gl)
        egl2 = jnp.concatenate([egl, egl], axis=1)

        dvn = _mm(t["a"], do, _BTN) + _mm(kd, ds1, _BNN)
        da = jnp.where(t["tril"], _mm(do, vn, _BNT), 0.0)
        dqg = _mm(do, s0, _BNT)
        dkd = _mm(vn, ds1, _BNT)
        dw = -_mm(dvn, s0, _BNT)
        ds_scr[...] = _mm(qg, do, _BTN) + egl2 * ds1 - _mm(w, dvn, _BTN)
        tt = _mm_xr(eye_h, tinv, _BNT)
        dvb = _mm3(tt, dvn, _BNN)
        dkbg = _mm3(tt, dw, _BNN)
        dm = -jnp.where(t["strict"], _mm(dvb, u, _BNT) + _mm(dkbg, w, _BNT), 0.0)
        dpm = dm * dec
        dqk = da * dec
        dkb = dkbg * gam + _mm(dpm, k, _BNN)
        dk = dkd * dlt + _mm(dpm, kb, _BTN) + _mm(dqk, q, _BTN) + dkb * beta
        dq = dqg * gam + _mm(dqk, k, _BNN)
        dv = dvb * beta
        dbeta = jnp.sum(dkb * k, axis=2, keepdims=True) + jnp.sum(dvb * v, axis=2, keepdims=True)
        dgam = jnp.sum(dqg * q, axis=2, keepdims=True) + jnp.sum(dkbg * kb, axis=2, keepdims=True)
        ddlt = jnp.sum(dkd * k, axis=2, keepdims=True)
        xm = dm * t["m"] + da * t["a"]
        xt = _mm_xr(eye_h, xm, _BNT)
        dgc = (dgam * gam - ddlt * dlt + jnp.sum(xm, axis=2, keepdims=True) - jnp.sum(xt, axis=2, keepdims=True))
        dgl = jnp.sum(ddlt * dlt, axis=1, keepdims=True) + jnp.sum(
            jnp.sum(ds1 * s0, axis=2, keepdims=True), axis=1, keepdims=True) * jnp.max(egl, axis=1, keepdims=True)
        dgc = dgc + jnp.where(last_row, dgl, 0.0)
        dg = _mm_xr(u_ge_h, dgc, _BNN)
        for h in range(DN_HEADS):
            sl = slice(h * LANES, (h + 1) * LANES)
            dqkv_ref[:, sl] = dq[h]
            dqkv_ref[:, DN_W + h * LANES:DN_W + (h + 1) * LANES] = dk[h]
            dqkv_ref[:, 2 * DN_W + h * LANES:2 * DN_W + (h + 1) * LANES] = dv[h]
            dbb_ref[:, sl] = jnp.broadcast_to(dbeta[h], (CHUNK, LANES))
            dg_ref[:, sl] = dg[h]

    rev = lambda w: pl.BlockSpec((CHUNK, w), lambda n: (N - 1 - n, 0))
    hbm = pl.BlockSpec(memory_space=pl.ANY)
    outs = pl.pallas_call(
        body, name="dn_bwd", grid=(N,),
        in_specs=[rev(3 * DN_W), rev(DN_W), rev(DN_W), rev(DN_W),
                  pl.BlockSpec((DN_HEADS, CHUNK, CHUNK), lambda n: (0, N - 1 - n, 0)),
                  pl.BlockSpec((1, DN_HEADS, DN_HD, DN_HD), lambda n: (N - 1 - n, 0, 0, 0)), rev(DN_W)]
        + [hbm] * (2 * nx),
        out_specs=[rev(3 * DN_W), rev(DN_W), rev(DN_W)] + [hbm] * nx,
        out_shape=[_sds((S, 3 * DN_W), F32), _sds((S, DN_W), F32), _sds((S, DN_W), F32)]
        + (exchange.out_shape if exchange else []),
        scratch_shapes=[pltpu.VMEM((DN_HEADS, DN_HD, DN_HD), F32)]
        + (exchange.scratch + early.scratch if exchange else []),
        compiler_params=_cp(1),
    )(qkv, bb, gcb, glb, tinv_all, sall, do, *ex_src, *early_src)
    return outs[:3], list(outs[3:])


def _dn_prep_bwd_a(p, dqkv, dbb, dgb, conv_w, a_row, dtb_row, tm=_ROW_BLOCK):
    S, PC = p.shape
    W3 = 3 * DN_W
    nhalo = tm // 8

    def body(x_ref, halo_ref, w_ref, ba_ref, a_ref, dtb_ref, dqkv_ref, dbb_ref, dgb_ref,
             dc_ref, dba_ref, dal_ref, ddt_ref):
        i = pl.program_id(0)

        @pl.when(i == 0)
        def _():
            dal_ref[...] = jnp.zeros_like(dal_ref)
            ddt_ref[...] = jnp.zeros_like(ddt_ref)

        halo = jnp.where(i > 0, halo_ref[...], 0.0)
        xf = jnp.concatenate([halo, x_ref[...]], axis=0)
        acc = jnp.zeros((tm, W3), F32)
        for k in range(CONV_K):
            sh = CONV_K - 1 - k
            xs = xf if sh == 0 else pltpu.roll(xf, sh, 0)
            acc = acc + xs[8:, :] * w_ref[k:k + 1, :]
        s = _silu(acc)
        ds_act = _dsilu(acc)
        for gi in range(2 * DN_HEADS):
            sl = slice(gi * LANES, (gi + 1) * LANES)
            sg = s[:, sl]
            rinv = lax.rsqrt(jnp.sum(sg * sg, axis=1, keepdims=True) + EPS)
            nh = sg * rinv
            dn = dqkv_ref[:, sl] * (DN_HD ** -0.5 if gi < DN_HEADS else 1.0)
            dsg = rinv * (dn - nh * jnp.sum(dn * nh, axis=1, keepdims=True))
            dc_ref[:, sl] = dsg * ds_act[:, sl]
        dc_ref[:, 2 * DN_W:] = dqkv_ref[:, 2 * DN_W:] * ds_act[:, 2 * DN_W:]

        ba = ba_ref[...]
        beta = _sigmoid(ba)
        ea = jnp.exp(a_ref[...])
        pre = ba + dtb_ref[...]
        g = -ea * _softplus(pre)
        lr, lc = _iota2((DN_W, LANES), 0), _iota2((DN_W, LANES), 1)
        pick_b = jnp.where(lc == lr // LANES, 1.0 / LANES, 0.0)
        pick_g = jnp.where(lc == lr // LANES + DN_HEADS, 1.0 / LANES, 0.0)
        dbeta = _mm_xl(dbb_ref[...], pick_b)
        dg = _mm_xl(dgb_ref[...], pick_g)
        lane = _iota2((1, LANES), 1)
        da = dg * (-ea) * _sigmoid(pre)
        dba_ref[...] = jnp.where(lane < DN_HEADS, dbeta * beta * (1.0 - beta),
                                 jnp.where(lane < 2 * DN_HEADS, da, 0.0)).astype(dba_ref.dtype)
        dal_ref[...] += jnp.sum(dg * g, axis=0, keepdims=True)
        ddt_ref[...] += jnp.sum(da, axis=0, keepdims=True)

    return pl.pallas_call(
        body, name="dn_prep_bwd_a", grid=(S // tm,),
        in_specs=[_rb(tm, W3, 0), pl.BlockSpec((8, W3), lambda i: (jnp.maximum(i * nhalo - 1, 0), 0)),
                  _fs((CONV_K, W3)), _rb(tm, LANES, (PC - LANES) // LANES), _fs((1, LANES)), _fs((1, LANES)),
                  _rb(tm, W3), _rb(tm, DN_W), _rb(tm, DN_W)],
        out_specs=[_rb(tm, W3), _rb(tm, LANES), _fs((1, LANES)), _fs((1, LANES))],
        out_shape=[_sds((S, W3), F32), _sds((S, LANES), _MXU_DTYPE), _sds((1, LANES), F32), _sds((1, LANES), F32)],
        compiler_params=_cp(1),
    )(p, p, conv_w, p, a_row, dtb_row, dqkv, dbb, dgb)


def _dn_prep_bwd_b(p, dc, conv_w, tm=_ROW_BLOCK):
    S = p.shape[0]
    W3 = 3 * DN_W
    nhalo = tm // 8
    nblk = S // tm

    def body(x_ref, xh_ref, dc_ref, dch_ref, w_ref, dx_ref, dw_ref):
        i = pl.program_id(0)

        @pl.when(i == 0)
        def _():
            dw_ref[...] = jnp.zeros_like(dw_ref)

        dcv = dc_ref[...]
        xf = jnp.concatenate([jnp.where(i > 0, xh_ref[...], 0.0), x_ref[...]], axis=0)
        df = jnp.concatenate([dcv, jnp.where(i < nblk - 1, dch_ref[...], 0.0)], axis=0)
        acc = jnp.zeros((tm, W3), F32)
        for k in range(CONV_K):
            sh = CONV_K - 1 - k
            xs = xf if sh == 0 else pltpu.roll(xf, sh, 0)
            dw_ref[k:k + 1, :] += jnp.sum(dcv * xs[8:, :], axis=0, keepdims=True)
            ds = df if sh == 0 else pltpu.roll(df, tm + 8 - sh, 0)
            acc = acc + ds[:tm, :] * w_ref[k:k + 1, :]
        dx_ref[...] = acc.astype(dx_ref.dtype)

    return pl.pallas_call(
        body, name="dn_prep_bwd_b", grid=(nblk,),
        in_specs=[_rb(tm, W3, 0), pl.BlockSpec((8, W3), lambda i: (jnp.maximum(i * nhalo - 1, 0), 0)),
                  _rb(tm, W3), pl.BlockSpec((8, W3), lambda i: (jnp.minimum((i + 1) * nhalo, S // 8 - 1), 0)),
                  _fs((CONV_K, W3))],
        out_specs=[_rb(tm, W3), _fs((CONV_K, W3))],
        out_shape=[_sds((S, W3), _MXU_DTYPE), _sds((CONV_K, W3), F32)], compiler_params=_cp(1),
    )(p, p, dc, dc, conv_w)


def _gate(o_att, o_dn, p, gn, tm=_ROW_BLOCK):
    S = p.shape[0]

    def body(oa_ref, zs_ref, od_ref, zd_ref, gn_ref, osb_ref, odn_ref):
        osb_ref[...] = (oa_ref[...] * _silu(zs_ref[...])).astype(osb_ref.dtype)
        for h in range(DN_HEADS):
            sl = slice(h * LANES, (h + 1) * LANES)
            o = od_ref[:, sl]
            r = lax.rsqrt(jnp.mean(o * o, axis=1, keepdims=True) + EPS)
            odn_ref[:, sl] = (o * r * gn_ref[...] * _silu(zd_ref[:, sl])).astype(odn_ref.dtype)

    return pl.pallas_call(
        body, name="gate", grid=(S // tm,),
        in_specs=[_rb(tm, SB_W), _rb(tm, SB_W, C_SB_Z // SB_W), _rb(tm, DN_W), _rb(tm, DN_W, C_DN_Z // DN_W),
                  _fs((1, LANES))],
        out_specs=[_rb(tm, SB_W), _rb(tm, DN_W)],
        out_shape=[_sds((S, SB_W), _MXU_DTYPE), _sds((S, DN_W), _MXU_DTYPE)], compiler_params=_cp(1),
    )(o_att, p, o_dn, p, gn)


def _gate_bwd(db_sb, db_dn, wb_sb, wb_dn, o_att, o_dn, p, gn, tm=_ROW_BLOCK):
    S = p.shape[0]
    D = db_sb.shape[1]

    def body(dbs_ref, dbd_ref, ws_ref, wd_ref, oa_ref, zs_ref, od_ref, zd_ref, gn_ref,
             doa_ref, dzs_ref, dod_ref, dzd_ref, dgn_ref):
        @pl.when(pl.program_id(0) == 0)
        def _():
            dgn_ref[...] = jnp.zeros_like(dgn_ref)

        do_sb = _mm(dbs_ref[...], ws_ref[...], _NT)
        zs = zs_ref[...]
        doa_ref[...] = do_sb * _silu(zs)
        dzs_ref[...] = (do_sb * oa_ref[...] * _dsilu(zs)).astype(dzs_ref.dtype)
        do_dnn = _mm(dbd_ref[...], wd_ref[...], _NT)
        gnv = gn_ref[...]
        for h in range(DN_HEADS):
            sl = slice(h * LANES, (h + 1) * LANES)
            o, z, dout = od_ref[:, sl], zd_ref[:, sl], do_dnn[:, sl]
            r = lax.rsqrt(jnp.mean(o * o, axis=1, keepdims=True) + EPS)
            oh = o * r
            sz = _silu(z)
            dzd_ref[:, sl] = (dout * oh * gnv * _dsilu(z)).astype(dzd_ref.dtype)
            dgn_ref[...] += jnp.sum(dout * sz * oh, axis=0, keepdims=True)
            doh = dout * gnv * sz
            dod_ref[:, sl] = r * (doh - oh * jnp.mean(doh * oh, axis=1, keepdims=True))

    return pl.pallas_call(
        body, name="gate_bwd", grid=(S // tm,),
        in_specs=[_rb(tm, D), _rb(tm, D), _fs((SB_W, D)), _fs((DN_W, D)), _rb(tm, SB_W),
                  _rb(tm, SB_W, C_SB_Z // SB_W), _rb(tm, DN_W), _rb(tm, DN_W, C_DN_Z // DN_W), _fs((1, LANES))],
        out_specs=[_rb(tm, SB_W), _rb(tm, SB_W), _rb(tm, DN_W), _rb(tm, DN_W), _fs((1, LANES))],
        out_shape=[_sds((S, SB_W), F32), _sds((S, SB_W), _MXU_DTYPE), _sds((S, DN_W), F32),
                   _sds((S, DN_W), _MXU_DTYPE), _sds((1, LANES), F32)],
        compiler_params=_cp(1),
    )(db_sb, db_dn, wb_sb, wb_dn, o_att, p, o_dn, p, gn)


def _branch(o_sb, o_dnn, wb_sb, wb_dn, p, D, tm=_ROW_BLOCK):
    S = p.shape[0]

    def body(os_ref, od_ref, ws_ref, wd_ref, ms_ref, md_ref, y_ref, bs_ref, bd_ref):
        bs = _mm(os_ref[...], ws_ref[...])
        bdn = _mm(od_ref[...], wd_ref[...])
        bs_ref[...] = bs.astype(bs_ref.dtype)
        bd_ref[...] = bdn.astype(bd_ref.dtype)
        y_ref[...] = (_sigmoid(ms_ref[...]) * bs + _sigmoid(md_ref[...]) * bdn).astype(y_ref.dtype)

    return pl.pallas_call(
        body, name="branch", grid=(S // tm,),
        in_specs=[_rb(tm, SB_W), _rb(tm, DN_W), _fs((SB_W, D)), _fs((DN_W, D)),
                  _rb(tm, D, C_MG // D), _rb(tm, D, C_MG // D + 1)],
        out_specs=[_rb(tm, D), _rb(tm, D), _rb(tm, D)],
        out_shape=[_sds((S, D), _MXU_DTYPE)] * 3, compiler_params=_cp(1),
    )(o_sb, o_dnn, wb_sb, wb_dn, p, p)


def _out_proj(x, y, w_out, gate, tm=_ROW_BLOCK):
    S, D = x.shape

    def body(x_ref, y_ref, w_ref, g_ref, xn_ref, out_ref):
        out = _mm(y_ref[...], w_ref[...])
        out_ref[...] = out
        xn_ref[...] = x_ref[...] + g_ref[...] * out

    return pl.pallas_call(
        body, name="out_proj", grid=(S // tm,),
        in_specs=[_rb(tm, D), _rb(tm, D), _fs((D, D)), _fs((1, D))],
        out_specs=[_rb(tm, D), _rb(tm, D)],
        out_shape=[_sds((S, D), F32), _sds((S, D), F32)], compiler_params=_cp(1),
    )(x, y, w_out, gate)


def _out_bwd(dxn, out, gate, w_out, p, b_sb, b_dn, tm=_ROW_BLOCK):
    S, D = dxn.shape

    def body(dxn_ref, out_ref, g_ref, w_ref, ms_ref, md_ref, bs_ref, bd_ref,
             dout_ref, dbs_ref, dbd_ref, dm_ref, dgate_ref):
        @pl.when(pl.program_id(0) == 0)
        def _():
            dgate_ref[...] = jnp.zeros_like(dgate_ref)

        dxv = dxn_ref[...]
        dgate_ref[...] += jnp.sum(dxv * out_ref[...], axis=0, keepdims=True)
        dout = (g_ref[...] * dxv).astype(dout_ref.dtype)
        dout_ref[...] = dout
        dy = _mm(dout, w_ref[...], _NT)
        s1, s2 = _sigmoid(ms_ref[...]), _sigmoid(md_ref[...])
        dbs_ref[...] = (dy * s1).astype(dbs_ref.dtype)
        dbd_ref[...] = (dy * s2).astype(dbd_ref.dtype)
        dm_ref[:, :D] = (dy * bs_ref[...] * s1 * (1.0 - s1)).astype(dm_ref.dtype)
        dm_ref[:, D:] = (dy * bd_ref[...] * s2 * (1.0 - s2)).astype(dm_ref.dtype)

    return pl.pallas_call(
        body, name="out_bwd", grid=(S // tm,),
        in_specs=[_rb(tm, D), _rb(tm, D), _fs((1, D)), _fs((D, D)), _rb(tm, D, C_MG // D),
                  _rb(tm, D, C_MG // D + 1), _rb(tm, D), _rb(tm, D)],
        out_specs=[_rb(tm, D), _rb(tm, D), _rb(tm, D), _rb(tm, 2 * D), _fs((1, D))],
        out_shape=[_sds((S, D), _MXU_DTYPE)] * 3 + [_sds((S, 2 * D), _MXU_DTYPE), _sds((1, D), F32)],
        compiler_params=_cp(1),
    )(dxn, out, gate, w_out, p, p, b_sb, b_dn)


def _loss_head(xf, target, tm=_ROW_BLOCK):
    S, D = xf.shape

    def body(x_ref, t_ref, dy_ref, loss_ref):
        @pl.when(pl.program_id(0) == 0)
        def _():
            loss_ref[...] = jnp.zeros_like(loss_ref)

        e = x_ref[...] - t_ref[...]
        dy_ref[...] = e * (1.0 / D)
        row = jnp.sum(e * e, axis=1, keepdims=True) * (1.0 / D)
        loss_ref[...] += 0.5 * jnp.sum(row, axis=0, keepdims=True)

    return pl.pallas_call(
        body, name="loss_head", grid=(S // tm,),
        in_specs=[_rb(tm, D), _rb(tm, D)], out_specs=[_rb(tm, D), _fs((1, LANES))],
        out_shape=[_sds((S, D), F32), _sds((1, LANES), F32)], compiler_params=_cp(1),
    )(xf, target)


def _ada_fwd(c_all, ada_w, ada_b_sh):
    L, D, n = ada_w.shape
    B = c_all.shape[0]

    def body(c_ref, w_ref, b_ref, o_ref):
        sc = _silu(c_ref[...])
        o_ref[0] = _mm(sc, w_ref[0]) + b_ref[0]

    return pl.pallas_call(
        body, name="ada_fwd", grid=(L,),
        in_specs=[_fs((B, D)), pl.BlockSpec((1, D, n), lambda l: (l, 0, 0)), pl.BlockSpec((1, 1, n), lambda l: (l, 0, 0))],
        out_specs=pl.BlockSpec((1, B, n), lambda l: (l, 0, 0)),
        out_shape=_sds((L, B, n), F32), compiler_params=_cp(1),
    )(c_all, ada_w, ada_b_sh)


def _ada_bwd(c_all_t, dmod_sh):
    D, B = c_all_t.shape
    L, _, n = dmod_sh.shape

    def body(c_ref, d_ref, o_ref):
        acc = jnp.zeros((D, n), F32)
        for b in range(B):
            acc = acc + _silu(c_ref[:, b:b + 1]) * d_ref[0, b:b + 1, :]
        o_ref[0] = acc

    return pl.pallas_call(
        body, name="ada_bwd", grid=(L,),
        in_specs=[_fs((D, B)), pl.BlockSpec((1, B, n), lambda l: (l, 0, 0))],
        out_specs=pl.BlockSpec((1, D, n), lambda l: (l, 0, 0)),
        out_shape=_sds((L, D, n), F32), compiler_params=_cp(1),
    )(c_all_t, dmod_sh)


def _sum_parts(name, parts):
    P, R, C = parts.shape
    tr = _pick(R, max(16, min(512, (1 << 19) // (P * C))), 16) if R % 16 == 0 else R

    def body(p_ref, o_ref):
        acc = p_ref[0].astype(F32)
        for k in range(1, P):
            acc = acc + p_ref[k].astype(F32)
        o_ref[...] = acc

    return pl.pallas_call(
        body, name=name, grid=(R // tr,),
        in_specs=[pl.BlockSpec((P, tr, C), lambda i: (0, i, 0))], out_specs=_rb(tr, C),
        out_shape=_sds((R, C), F32), compiler_params=_cp(1),
    )(parts)


def _adamw(name, w, g, m, v):
    L, R, C = w.shape
    tr = _pick(R, 256, 8) if R % 8 == 0 else R
    c1 = 1.0 - ADAM_B1 ** ADAM_STEP
    c2 = 1.0 - ADAM_B2 ** ADAM_STEP

    def body(w_ref, g_ref, m_ref, v_ref, d_ref, mo_ref, vo_ref):
        gv = g_ref[...]
        mn = ADAM_B1 * m_ref[...] + (1.0 - ADAM_B1) * gv
        vn = ADAM_B2 * v_ref[...] + (1.0 - ADAM_B2) * (gv * gv)
        mo_ref[...] = mn
        vo_ref[...] = vn
        d_ref[...] = -ADAM_LR * ((mn / c1) / (jnp.sqrt(vn / c2) + ADAM_EPS) + ADAM_WD * w_ref[...])

    spec = pl.BlockSpec((1, tr, C), lambda l, i: (l, i, 0))
    return pl.pallas_call(
        body, name=name, grid=(L, R // tr),
        in_specs=[spec] * 4, out_specs=[spec] * 3, out_shape=[_sds((L, R, C), F32)] * 3, compiler_params=_cp(2),
    )(w, g, m, v)


def _ag_small(name, blk):
    R, C = blk.shape

    def body(x_ref, out_ref, send_sems, recv_sems, local_sem):
        x, y, c = lax.axis_index("x"), lax.axis_index("y"), lax.axis_index("c")
        me, sibling = (x, y, c), (x, y, 1 - c)
        chips = [(1 - x, y), (x, 1 - y), (1 - x, 1 - y)]

        def rows(px, py, pc):
            return out_ref.at[pl.ds((4 * px + 2 * py + pc) * R, R), :]

        def copy(k, block, to, src=None):
            return pltpu.make_async_remote_copy(
                src_ref=rows(*block) if src is None else src, dst_ref=rows(*block),
                send_sem=send_sems.at[k], recv_sem=recv_sems.at[k], device_id=to, device_id_type=MESH)

        mine = pltpu.make_async_copy(x_ref, rows(*me), local_sem)
        mine.start()
        first = [copy(0, me, sibling, src=x_ref)]
        first += [copy(1 + j, me, (*chip, c), src=x_ref) for j, chip in enumerate(chips)]
        for cp in first:
            cp.start()
        passed = [copy(4 + j, (*chip, c), sibling) for j, chip in enumerate(chips)]
        for j, chip in enumerate(chips):
            copy(1 + j, (*chip, c), me).wait_recv()
            passed[j].start()
        copy(0, sibling, me).wait_recv()
        for j, chip in enumerate(chips):
            copy(4 + j, (*chip, 1 - c), me).wait_recv()
        for cp in first + passed:
            cp.wait_send()
        mine.wait()

    return pl.pallas_call(
        body, name=name, out_shape=_sds((8 * R, C), blk.dtype),
        in_specs=[pl.BlockSpec(memory_space=pltpu.VMEM)], out_specs=pl.BlockSpec(memory_space=pltpu.VMEM),
        scratch_shapes=[pltpu.SemaphoreType.DMA((7,)), pltpu.SemaphoreType.DMA((7,)), pltpu.SemaphoreType.DMA],
    )(blk)


def _row_chunks(ts, row_axis):
    pieces = []
    for t, a in enumerate(ts):
        rows = a.shape[row_axis]
        n = 4 if rows >= 1024 else 1
        pieces += [(t, i * (rows // n), rows // n) for i in range(n)]
    return pieces


def _ag_weights_first(ts):
    nt = len(ts)
    pieces = _row_chunks(ts, 0)
    NP = len(pieces)
    sizes = [nr * ts[t].shape[1] for t, _, nr in pieces]
    split = next(pi for pi in range(NP + 1) if 2 * sum(sizes[:pi]) >= sum(sizes))

    def body(*refs):
        w, out = refs[:nt], refs[nt:2 * nt]
        send_sems, recv_sems, local_sems = refs[2 * nt:]
        x, y, c = lax.axis_index("x"), lax.axis_index("y"), lax.axis_index("c")
        me, sibling = (x, y, c), (x, y, 1 - c)
        mine = 2 * x + y
        chips = [(1 - x, y), (x, 1 - y), (1 - x, 1 - y)]

        def blk(t, shard, r0, nr):
            return out[t].at[shard, r0:r0 + nr, :]

        def copy(k, dst, to, src=None):
            return pltpu.make_async_remote_copy(
                src_ref=dst if src is None else src, dst_ref=dst, send_sem=send_sems.at[k], recv_sem=recv_sems.at[k],
                device_id=to, device_id_type=MESH)

        own = [pltpu.make_async_copy(w[t], out[t].at[mine], local_sems.at[t]) for t in range(nt)]
        for cp in own:
            cp.start()
        for fetcher, lo, hi in ((0, 0, split), (1, split, NP)):
            @pl.when(c == fetcher)
            def _(lo=lo, hi=hi):
                sent = []
                for j, chip in enumerate(chips):
                    for pi in range(lo, hi):
                        t, r0, nr = pieces[pi]
                        sent.append(copy(j * NP + pi, blk(t, mine, r0, nr), (*chip, c), src=w[t].at[r0:r0 + nr, :]))
                        sent[-1].start()
                for j, chip in enumerate(chips):
                    theirs = 2 * chip[0] + chip[1]
                    for pi in range(lo, hi):
                        t, r0, nr = pieces[pi]
                        copy(j * NP + pi, blk(t, theirs, r0, nr), me).wait_recv()
                        sent.append(copy((3 + j) * NP + pi, blk(t, theirs, r0, nr), sibling))
                        sent[-1].start()
                for cp in sent:
                    cp.wait_send()

            @pl.when(c != fetcher)
            def _(lo=lo, hi=hi):
                for j, chip in enumerate(chips):
                    theirs = 2 * chip[0] + chip[1]
                    for pi in range(lo, hi):
                        t, r0, nr = pieces[pi]
                        copy((3 + j) * NP + pi, blk(t, theirs, r0, nr), me).wait_recv()

        for cp in own:
            cp.wait()

    return pl.pallas_call(
        body, name="ag_weights_first", out_shape=[_sds((4,) + a.shape, a.dtype) for a in ts],
        in_specs=[pl.BlockSpec(memory_space=pl.ANY)] * nt, out_specs=[pl.BlockSpec(memory_space=pltpu.VMEM)] * nt,
        scratch_shapes=[pltpu.SemaphoreType.DMA((6 * NP,)), pltpu.SemaphoreType.DMA((6 * NP,)),
                        pltpu.SemaphoreType.DMA((nt,))],
        compiler_params=pltpu.CompilerParams(vmem_limit_bytes=_VMEM_LIMIT),
    )(*ts)


class _WeightGather:
    def __init__(self, ts):
        self.nt = len(ts)
        self.pieces = _row_chunks(ts, 0)
        NP = len(self.pieces)
        self.out_shape = [_sds((4,) + a.shape, a.dtype) for a in ts]
        self.scratch = [pltpu.SemaphoreType.DMA((3 * NP,)), pltpu.SemaphoreType.DMA((3 * NP,)),
                        pltpu.SemaphoreType.DMA((self.nt,))]

    def _copies(self, src, out, sems):
        send_sems, recv_sems, local_sems = sems
        NP = len(self.pieces)
        x, y, c = lax.axis_index("x"), lax.axis_index("y"), lax.axis_index("c")
        mine = 2 * x + y
        own = [pltpu.make_async_copy(src[t], out[t].at[mine], local_sems.at[t]) for t in range(self.nt)]
        sends, recvs = [], []
        for j, chip in enumerate([(1 - x, y), (x, 1 - y), (1 - x, 1 - y)]):
            theirs = 2 * chip[0] + chip[1]
            for pi, (t, r0, nr) in enumerate(self.pieces):
                idx = j * NP + pi
                sends.append(pltpu.make_async_remote_copy(
                    src_ref=src[t].at[r0:r0 + nr, :], dst_ref=out[t].at[mine, r0:r0 + nr, :],
                    send_sem=send_sems.at[idx], recv_sem=recv_sems.at[idx], device_id=(*chip, c), device_id_type=MESH))
                recvs.append(pltpu.make_async_remote_copy(
                    src_ref=out[t].at[theirs, r0:r0 + nr, :], dst_ref=out[t].at[theirs, r0:r0 + nr, :],
                    send_sem=send_sems.at[idx], recv_sem=recv_sems.at[idx], device_id=(x, y, c), device_id_type=MESH))
        return own, sends, recvs

    def start(self, src, out, sems):
        own, sends, _ = self._copies(src, out, sems)
        for cp in own + sends:
            cp.start()

    def wait(self, src, out, sems):
        own, sends, recvs = self._copies(src, out, sems)
        for cp in recvs:
            cp.wait_recv()
        for cp in sends:
            cp.wait_send()
        for cp in own:
            cp.wait()


class _GradExchange:
    def __init__(self, ts, layer, chunks=None):
        self.nt, self.layer = len(ts), layer
        self.pieces = [p for i, p in enumerate(_row_chunks(ts, 1)) if chunks is None or i in chunks]
        NP = len(self.pieces)
        self.out_shape = [_sds((8,) + a.shape[1:], a.dtype) for a in ts]
        self.scratch = [pltpu.SemaphoreType.DMA((7 * NP,)), pltpu.SemaphoreType.DMA((7 * NP,)),
                        pltpu.SemaphoreType.DMA((NP,))]

    def _copies(self, src, out, sems):
        send_sems, recv_sems, local_sems = sems
        NP = len(self.pieces)
        x, y, c = lax.axis_index("x"), lax.axis_index("y"), lax.axis_index("c")
        me = 4 * x + 2 * y + c
        owner = c == self.layer
        own = [pltpu.make_async_copy(src[t].at[2 * x + y, r0:r0 + nr, :], out[t].at[me, r0:r0 + nr, :],
                                     local_sems.at[pi]) for pi, (t, r0, nr) in enumerate(self.pieces)]
        rel = []
        for k in range(1, 8):
            px = 1 - x if k & 4 else x
            py = 1 - y if k & 2 else y
            source = 4 * px + 2 * py + (1 - c if k & 1 else c)
            sends, recvs = [], []
            for pi, (t, r0, nr) in enumerate(self.pieces):
                idx = (k - 1) * NP + pi
                sends.append(pltpu.make_async_remote_copy(
                    src_ref=src[t].at[2 * px + py, r0:r0 + nr, :], dst_ref=out[t].at[me, r0:r0 + nr, :],
                    send_sem=send_sems.at[idx], recv_sem=recv_sems.at[idx], device_id=(px, py, self.layer),
                    device_id_type=MESH))
                recvs.append(pltpu.make_async_remote_copy(
                    src_ref=out[t].at[source, r0:r0 + nr, :], dst_ref=out[t].at[source, r0:r0 + nr, :],
                    send_sem=send_sems.at[idx], recv_sem=recv_sems.at[idx], device_id=(x, y, c),
                    device_id_type=MESH))
            rel.append((jnp.logical_not(owner) if k & 1 else owner, sends, recvs))
        return owner, own, rel

    def start(self, src, out, sems):
        owner, own, rel = self._copies(src, out, sems)

        @pl.when(owner)
        def _():
            for cp in own:
                cp.start()

        for sending, sends, _ in rel:
            @pl.when(sending)
            def _(sends=sends):
                for cp in sends:
                    cp.start()

    def wait(self, src, out, sems):
        owner, own, rel = self._copies(src, out, sems)

        @pl.when(owner)
        def _():
            for _, _, recvs in rel:
                for cp in recvs:
                    cp.wait_recv()
            for cp in own:
                cp.wait()

        for sending, sends, _ in rel:
            @pl.when(sending)
            def _(sends=sends):
                for cp in sends:
                    cp.wait_send()


def _sibling_join(ts):
    nt = len(ts)
    pieces = _row_chunks(ts, 0)
    NP = len(pieces)

    def body(*refs):
        src, out = refs[:nt], refs[nt:2 * nt]
        send_sems, recv_sems, local_sems = refs[2 * nt:]
        x, y, c = lax.axis_index("x"), lax.axis_index("y"), lax.axis_index("c")
        own = [pltpu.make_async_copy(src[t], out[t].at[c], local_sems.at[t]) for t in range(nt)]
        for cp in own:
            cp.start()
        sent = []
        for pi, (t, r0, nr) in enumerate(pieces):
            sent.append(pltpu.make_async_remote_copy(
                src_ref=src[t].at[r0:r0 + nr, :], dst_ref=out[t].at[c, r0:r0 + nr, :], send_sem=send_sems.at[pi],
                recv_sem=recv_sems.at[pi], device_id=(x, y, 1 - c), device_id_type=MESH))
            sent[-1].start()
        for pi, (t, r0, nr) in enumerate(pieces):
            pltpu.make_async_remote_copy(
                src_ref=src[t].at[r0:r0 + nr, :], dst_ref=out[t].at[1 - c, r0:r0 + nr, :], send_sem=send_sems.at[pi],
                recv_sem=recv_sems.at[pi], device_id=(x, y, c), device_id_type=MESH).wait_recv()
        for cp in sent:
            cp.wait_send()
        for cp in own:
            cp.wait()

    vmem = pl.BlockSpec(memory_space=pltpu.VMEM)
    return pl.pallas_call(
        body, name="sibling_join", out_shape=[_sds((2,) + a.shape, a.dtype) for a in ts],
        in_specs=[vmem] * nt, out_specs=[vmem] * nt,
        scratch_shapes=[pltpu.SemaphoreType.DMA((NP,)), pltpu.SemaphoreType.DMA((NP,)),
                        pltpu.SemaphoreType.DMA((nt,))],
        compiler_params=pltpu.CompilerParams(vmem_limit_bytes=_VMEM_LIMIT),
    )(*ts)


def _late_weights(gathered):
    g_bs, g_bd, g_out = gathered
    cat = lambda g, axis: jnp.concatenate([g[s] for s in range(4)], axis=axis)
    return dict(wb_sb=cat(g_bs, 1), wb_dn=cat(g_bd, 1), w_out=cat(g_out, 0))


def _layer_fwd(x, shift, scale, gate, lw, next_shards=None, late_shards=None):
    D = x.shape[1]
    h = _norm_mod(x, lw["norm_g"], scale, shift)
    if late_shards is None:
        p = _matmul("in_proj", h, lw["w_cat"], "nn", F32, tm_cap=2048, tn_cap=896)
    else:
        p, late = _matmul("in_proj", h, lw["w_cat"], "nn", F32, tm_cap=2048, tn_cap=896,
                          exchange=_WeightGather(late_shards), ex_src=late_shards)
        lw = {**lw, **_late_weights(late)}
    qn, kn = _sb_prep(p, lw["gq_t"], lw["gk_t"])
    qkv, bb, gcb, glb = _dn_prep(p, lw["conv_w"], lw["a_row"], lw["dtb_row"])
    if next_shards is None:
        (o_att, tot, nblocks), _ = _sb_fwd(qn, kn, p)
        (o_dn, tinv, sall), gathered = _dn_fwd(qkv, bb, gcb, glb)
    else:
        first, rest = next_shards[:1], next_shards[1:]
        (o_att, tot, nblocks), g_rest = _sb_fwd(qn, kn, p, _WeightGather(rest), rest)
        (o_dn, tinv, sall), g_first = _dn_fwd(qkv, bb, gcb, glb, _WeightGather(first), first)
        gathered = list(g_first) + g_rest
    o_sb, o_dnn = _gate(o_att, o_dn, p, lw["gn"])
    y, b_sb, b_dn = _branch(o_sb, o_dnn, lw["wb_sb"], lw["wb_dn"], p, D)
    x_next, out = _out_proj(x, y, lw["w_out"], gate)
    res = dict(x=x, h=h, p=p, qn=qn, kn=kn, o_att=o_att, tot=tot, nblocks=nblocks, qkv=qkv, bb=bb, gcb=gcb, glb=glb, o_dn=o_dn,
               tinv=tinv, sall=sall, o_sb=o_sb, o_dnn=o_dnn, y=y, b_sb=b_sb, b_dn=b_dn, out=out,
               shift=shift, scale=scale, gate=gate)
    return x_next, res, lw, gathered


def _layer_bwd(dxn, res, lw, pending=None):
    p = res["p"]
    dout, db_sb, db_dn, dm, dgate = _out_bwd(dxn, res["out"], res["gate"], lw["w_out"], p, res["b_sb"], res["b_dn"])
    dw_out = _matmul("dw_out", res["y"], dout, "tn", _MXU_DTYPE)
    dwb_sb = _matmul("dwb_sb", res["o_sb"], db_sb, "tn", _MXU_DTYPE)
    dwb_dn = _matmul("dwb_dn", res["o_dnn"], db_dn, "tn", _MXU_DTYPE)
    do_att, dz_sb, do_dn, dz_dn, dgn = _gate_bwd(db_sb, db_dn, lw["wb_sb"], lw["wb_dn"], res["o_att"], res["o_dn"],
                                                  p, lw["gn"])
    D = dxn.shape[1]
    by_shard = lambda g: g.reshape(g.shape[0], 4, g.shape[1] // 4).transpose(1, 0, 2)
    send = [by_shard(dwb_sb), by_shard(dwb_dn), dw_out.reshape(4, D // 4, D)]
    dn_args = (res["qkv"], res["bb"], res["gcb"], res["glb"], res["tinv"], res["sall"], do_dn)
    if pending is None:
        (dqn, dkn, dv), _ = _sb_bwd(res["qn"], res["kn"], p, do_att, res["tot"], res["nblocks"])
        (dqkv, dbb, dgb), received = _dn_bwd(*dn_args)
    else:
        above, above_send = pending
        (dqn, dkn, dv), got_in = _sb_bwd(res["qn"], res["kn"], p, do_att, res["tot"], res["nblocks"],
                                         _GradExchange(above_send[:1], above), above_send[:1])
        (dqkv, dbb, dgb), got_rest = _dn_bwd(*dn_args, _GradExchange(above_send[1:], above), above_send[1:],
                                             _GradExchange(send, above - 1), send)
        received, send = list(got_in) + got_rest, []
    dq_sb, dk_sb, dgq, dgk = _sb_prep_bwd(p, dqn, dkn, lw["gq_t"], lw["gk_t"])
    dc, dp_ba, dal, ddt = _dn_prep_bwd_a(p, dqkv, dbb, dgb, lw["conv_w"], lw["a_row"], lw["dtb_row"])
    dp_dn, dconv = _dn_prep_bwd_b(p, dc, lw["conv_w"])
    dp = jnp.concatenate([dp_dn, dz_dn, dq_sb, dk_sb, dv.astype(_MXU_DTYPE), dz_sb, dm, dp_ba], axis=1)
    dw_cat = _matmul("dw_cat", res["h"], dp, "tn", _MXU_DTYPE, tm_cap=1024, tn_cap=896, tk_cap=2048)
    send = [_shards_from_cat(dw_cat, D)] + send
    if pending is None:
        dh = _matmul("dh", dp, lw["w_cat"], "nt", F32, tm_cap=1024, tk_cap=896)
        (dx, dshift, dscale, dnorm_g), _ = _norm_mod_bwd(res["x"], dh, dxn, lw["norm_g"], res["scale"])
    else:
        dh, arrived = _matmul("dh", dp, lw["w_cat"], "nt", F32, tm_cap=1024, tk_cap=896,
                              exchange=_GradExchange(send, pending[0] - 1, (0, 1, 2)), ex_src=send,
                              ex_prev=received[:1])
        tail = _GradExchange(send, pending[0] - 1, (3,))
        if tail.pieces:
            (dx, dshift, dscale, dnorm_g), arrived = _norm_mod_bwd(
                res["x"], dh, dxn, lw["norm_g"], res["scale"], exchange=tail, ex_src=send, ex_prev=arrived)
        else:
            (dx, dshift, dscale, dnorm_g), _ = _norm_mod_bwd(res["x"], dh, dxn, lw["norm_g"], res["scale"])
        received, send = arrived + list(received[1:]), []
    small = dict(dmod=jnp.concatenate([dshift, dscale, dgate], axis=1)[0], norm_g=dnorm_g[0],
                 sb_q_g=dgq.reshape(SB_HEADS, SB_HD).sum(0), sb_k_g=dgk.reshape(SB_HEADS, SB_HD).sum(0),
                 conv_w=dconv, dn_a_log=dal[0, DN_HEADS:2 * DN_HEADS], dn_dt_bias=ddt[0, DN_HEADS:2 * DN_HEADS],
                 dn_norm_g=dgn[0])
    return dx, small, send, received


def _cat_cols(w, D):
    return jnp.concatenate([w[:, 2048:4096], w[:, 0:2048], w[:, 4104:4104 + 2 * D], w[:, 4096:4104],
                            jnp.zeros((w.shape[0], LANES - 8), w.dtype)], axis=1)


def _shards_from_cat(g, D):
    n = (4104 + 2 * D) // 4
    segments = ((0, 2048, 2048), (2048, 4096, 0), (4096, 4104, 4096 + 2 * D), (4104, 4104 + 2 * D, 4096))

    def shard(lo, hi):
        cuts = [(c0 + max(lo, s0) - s0, c0 + min(hi, s1) - s0) for s0, s1, c0 in segments if max(lo, s0) < min(hi, s1)]
        return jnp.concatenate([g[:, a:b] for a, b in cuts], axis=1)

    return jnp.stack([shard(s * n, (s + 1) * n) for s in range(4)])


def _flat_pack(arrs, mult):
    flat = jnp.concatenate([a.reshape(-1) for a in arrs])
    n = flat.shape[0]
    pad = (-n) % mult
    if pad:
        flat = jnp.concatenate([flat, jnp.zeros((pad,), flat.dtype)])
    return flat.reshape(-1, LANES)


def _flat_unpack(flat, shapes):
    flat = flat.reshape(-1)
    out, off = [], 0
    for s in shapes:
        n = math.prod(s)
        out.append(flat[off:off + n].reshape(s))
        off += n
    return out


BIG = ("w_in", "w_branch_sb", "w_branch_dn", "w_out")
SMALL = ("ada_b", "norm_g", "sb_q_g", "sb_k_g", "conv_w", "dn_a_log", "dn_dt_bias", "dn_norm_g")


def kernel(x, c, ada_w, ada_b, norm_g, w_in, sb_q_g, sb_k_g, conv_w, dn_a_log, dn_dt_bias, dn_norm_g, w_branch_sb, w_branch_dn, w_out, loss_target, m_ada_w, m_ada_b, m_norm_g, m_w_in, m_sb_q_g, m_sb_k_g, m_conv_w, m_dn_a_log, m_dn_dt_bias, m_dn_norm_g, m_w_branch_sb, m_w_branch_dn, m_w_out, v_ada_w, v_ada_b, v_norm_g, v_w_in, v_sb_q_g, v_sb_k_g, v_conv_w, v_dn_a_log, v_dn_dt_bias, v_dn_norm_g, v_w_branch_sb, v_w_branch_dn, v_w_out):
    W = dict(ada_w=ada_w, ada_b=ada_b, norm_g=norm_g, w_in=w_in, sb_q_g=sb_q_g, sb_k_g=sb_k_g, conv_w=conv_w,
             dn_a_log=dn_a_log, dn_dt_bias=dn_dt_bias, dn_norm_g=dn_norm_g, w_branch_sb=w_branch_sb,
             w_branch_dn=w_branch_dn, w_out=w_out)
    M = dict(ada_w=m_ada_w, ada_b=m_ada_b, norm_g=m_norm_g, w_in=m_w_in, sb_q_g=m_sb_q_g, sb_k_g=m_sb_k_g,
             conv_w=m_conv_w, dn_a_log=m_dn_a_log, dn_dt_bias=m_dn_dt_bias, dn_norm_g=m_dn_norm_g,
             w_branch_sb=m_w_branch_sb, w_branch_dn=m_w_branch_dn, w_out=m_w_out)
    V = dict(ada_w=v_ada_w, ada_b=v_ada_b, norm_g=v_norm_g, w_in=v_w_in, sb_q_g=v_sb_q_g, sb_k_g=v_sb_k_g,
             conv_w=v_conv_w, dn_a_log=v_dn_a_log, dn_dt_bias=v_dn_dt_bias, dn_norm_g=v_dn_norm_g,
             w_branch_sb=v_w_branch_sb, w_branch_dn=v_w_branch_dn, w_out=v_w_out)
    L = ada_w.shape[0]
    S, D = x.shape[1], x.shape[2]
    ix, iy, ic = lax.axis_index("x"), lax.axis_index("y"), lax.axis_index("c")
    shard = 2 * ix + iy
    me = 2 * shard + ic
    n_ada = ada_w.shape[2]
    n_in = w_in.shape[2]
    n_conv = conv_w.shape[2]
    n_br = w_branch_sb.shape[2]
    n_out = w_out.shape[1]

    assert L == 2, "the owner of a layer's gradients is the core with the layer's number"
    shards = [W[n].astype(_MXU_DTYPE) for n in BIG]
    gathered0 = _ag_weights_first([shards[0][0]])

    g1 = _ag_small("ag_c_conv", _flat_pack([c, conv_w], LANES * 8))
    g1 = g1.reshape(8, -1)
    c_all = g1[:, :D]
    conv_parts = g1[:, D:D + L * CONV_K * n_conv].reshape(4, 2, L, CONV_K, n_conv)[:, 0]
    conv_full = jnp.concatenate([conv_parts[s] for s in range(4)], axis=2)
    ada_b_sh = lax.dynamic_slice_in_dim(ada_b, shard * n_ada, n_ada, axis=1)[:, None, :]
    mod_sh = _ada_fwd(c_all, ada_w, ada_b_sh)
    g2 = _ag_small("ag_mod", _flat_pack([mod_sh], LANES * 8)).reshape(8, -1)
    mod_parts = g2[:, :L * 8 * n_ada].reshape(4, 2, L, 8, n_ada)[:, 0]
    mod_all = jnp.concatenate([mod_parts[s] for s in range(4)], axis=2)
    mod = lax.dynamic_index_in_dim(mod_all, me, axis=1, keepdims=False)

    def layer_weights(l, g_in, late=None):
        pad_lo = jnp.zeros((DN_HEADS,), F32)
        pad_hi = jnp.zeros((LANES - 2 * DN_HEADS,), F32)
        lw = dict(
            norm_g=norm_g[l][None, :], w_cat=_cat_cols(jnp.concatenate([g_in[s] for s in range(4)], axis=1), D),
            gq_t=jnp.tile(sb_q_g[l], SB_HEADS)[None, :], gk_t=jnp.tile(sb_k_g[l], SB_HEADS)[None, :],
            conv_w=conv_full[l],
            a_row=jnp.concatenate([pad_lo, dn_a_log[l], pad_hi])[None, :],
            dtb_row=jnp.concatenate([pad_lo, dn_dt_bias[l], pad_hi])[None, :], gn=dn_norm_g[l][None, :])
        return lw if late is None else {**lw, **_late_weights(late)}

    mods = lambda l: (mod[l, None, 0:D], mod[l, None, D:2 * D], mod[l, None, 2 * D:3 * D])
    lws, ress = [None] * L, [None] * L
    xs, ress[0], lws[0], gathered1 = _layer_fwd(x[0], *mods(0), layer_weights(0, gathered0[0]),
                                                next_shards=[a[1] for a in shards],
                                                late_shards=[a[0] for a in shards[1:]])
    xs, ress[1], lws[1], _ = _layer_fwd(xs, *mods(1), layer_weights(1, gathered1[0], gathered1[1:]))
    dxs, loss_row = _loss_head(xs, loss_target[0])
    loss = lax.psum(loss_row[0, 0], ("x", "y", "c"))
    smalls = [None] * L
    dxs, smalls[1], send1, _ = _layer_bwd(dxs, ress[1], lws[1])
    dxs, smalls[0], send0, got = _layer_bwd(dxs, ress[0], lws[0], (1, send1))
    grad_x = dxs[None]

    small_names = ("dmod",) + SMALL[1:]
    small_pack = _flat_pack([jnp.stack([smalls[l][n] for l in range(L)]) for n in small_names], LANES * 8)
    g3 = _ag_small("ag_small_grads", small_pack)
    R3 = small_pack.shape[0]
    g3 = g3.reshape(8, R3, LANES)
    small_sum = _sum_parts("sum_small", g3)
    small_shapes = [(L, 3 * D), (L, D), (L, SB_HD), (L, SB_HD), (L, CONV_K, 3 * DN_W), (L, DN_HEADS), (L, DN_HEADS),
                    (L, DN_HD)]
    sg = dict(zip(small_names, _flat_unpack(small_sum, small_shapes)))
    G = dict(ada_b=sg["dmod"], norm_g=sg["norm_g"], sb_q_g=sg["sb_q_g"], sb_k_g=sg["sb_k_g"],
             conv_w=lax.dynamic_slice_in_dim(sg["conv_w"], shard * n_conv, n_conv, axis=2),
             dn_a_log=sg["dn_a_log"], dn_dt_bias=sg["dn_dt_bias"], dn_norm_g=sg["dn_norm_g"])
    dmod_all = g3.reshape(8, -1)[:, :L * 3 * D].reshape(8, L, 3 * D)
    dmod_sh = lax.dynamic_slice_in_dim(dmod_all, shard * n_ada, n_ada, axis=2).transpose(1, 0, 2)
    G["ada_w"] = _ada_bwd(c_all.T, dmod_sh)

    assert not send0
    mine = [_sum_parts("sum_" + n, g) for n, g in zip(BIG, got)]
    for n, g in zip(BIG, _sibling_join(mine)):
        G[n] = g

    delta, new_m, new_v = {}, {}, {}
    for n in ("ada_w",) + BIG:
        delta[n], new_m[n], new_v[n] = _adamw("adamw_" + n, W[n], G[n], M[n], V[n])
    sm_shapes = [W[n].shape for n in SMALL]
    d, mo, vo = _adamw("adamw_small", *[_flat_pack([T[n] for n in SMALL], LANES * 8)[None] for T in (W, G, M, V)])
    for n, dd, mm, vv in zip(SMALL, _flat_unpack(d, sm_shapes), _flat_unpack(mo, sm_shapes),
                             _flat_unpack(vo, sm_shapes)):
        delta[n], new_m[n], new_v[n] = dd, mm, vv

    order = ("ada_w", "ada_b", "norm_g", "w_in", "sb_q_g", "sb_k_g", "conv_w", "dn_a_log", "dn_dt_bias", "dn_norm_g",
             "w_branch_sb", "w_branch_dn", "w_out")
    return (loss, grad_x, *[G[n] for n in order], *[delta[n] for n in order], *[new_m[n] for n in order],
            *[new_v[n] for n in order])
```
